```python
import math
import jax, jax.numpy as jnp
from jax import lax
import numpy as np

D_MODEL = 1024
BATCH = 8
SEQ = 8192
DEPTH = 4

N_META = 16
MLA_HEADS = 8
Q_LORA = 384
KV_LORA = 256
QK_NOPE = 64
QK_ROPE = 32
V_HEAD = 64
ROPE_THETA = 10000.0
Q_BLOCK = 128
HG_HEADS = 4
HG_KDIM = 128
HG_VDIM = 128
HG_CHUNK = 64
D_FF = 2816
EPS = 1e-6
NEG_BIG = -1e30
F_MIN = 1e-20

MLA_WIDTH = MLA_HEADS * V_HEAD
HG_FWIDTH = HG_HEADS * HG_KDIM
HG_WIDTH = HG_HEADS * HG_VDIM
IN_SPLITS = (Q_LORA, KV_LORA, QK_ROPE, HG_FWIDTH, HG_FWIDTH, HG_WIDTH, HG_WIDTH, D_MODEL, D_MODEL)
D_IN = sum(IN_SPLITS)

kernel_name = "hybrid_mla_hgrn2_macaron_meta"


def rms_norm(x, w):
    xf = x.astype(jnp.float32)
    y = xf * lax.rsqrt(jnp.mean(xf * xf, axis=-1, keepdims=True) + EPS)
    return (y * w.astype(jnp.float32)).astype(x.dtype)


def split_cols(z, sizes):
    outs, start = [], 0
    for s in sizes:
        outs.append(z[..., start:start + s])
        start += s
    return outs


def swiglu(x, w_gu, w_down):
    gate, up = jnp.split(x @ w_gu, 2, axis=-1)
    return (jax.nn.silu(gate) * up) @ w_down


def rope(x, pos):
    half = x.shape[-1] // 2
    inv = ROPE_THETA ** (-jnp.arange(half, dtype=jnp.float32) / half)
    ang = pos.astype(jnp.float32)[:, None] * inv[None, :]
    cos = jnp.cos(ang)[:, None, :]
    sin = jnp.sin(ang)[:, None, :]
    x1 = x[..., :half].astype(jnp.float32)
    x2 = x[..., half:].astype(jnp.float32)
    return jnp.concatenate([x1 * cos - x2 * sin, x2 * cos + x1 * sin], axis=-1).astype(x.dtype)


def mla(c_q, c_kv, k_pe, pos, q_norm_w, kv_norm_w, w_uq, w_ukv):
    B, L, _ = c_q.shape
    H, DQK = MLA_HEADS, QK_NOPE + QK_ROPE
    q = (rms_norm(c_q, q_norm_w) @ w_uq).reshape(B, L, H, DQK)
    q = jnp.concatenate([q[..., :QK_NOPE], rope(q[..., QK_NOPE:], pos)], axis=-1)
    kv = (rms_norm(c_kv, kv_norm_w) @ w_ukv).reshape(B, L, H, QK_NOPE + V_HEAD)
    v = kv[..., QK_NOPE:]
    k_rot = rope(k_pe[:, :, None, :], pos)
    k = jnp.concatenate([kv[..., :QK_NOPE], jnp.broadcast_to(k_rot, (B, L, H, QK_ROPE))], axis=-1)
    scale = DQK ** -0.5
    n_blocks = -(-L // Q_BLOCK)
    pad = n_blocks * Q_BLOCK - L
    qb = jnp.pad(q, ((0, 0), (0, pad), (0, 0), (0, 0)))
    qb = qb.reshape(B, n_blocks, Q_BLOCK, H, DQK).transpose(1, 0, 2, 3, 4)
    k_pos = jnp.arange(L)

    def block(args):
        qi, blk = args
        s = jnp.einsum('bqhd,bkhd->bhqk', qi, k).astype(jnp.float32) * scale
        q_pos = blk * Q_BLOCK + jnp.arange(Q_BLOCK)
        s = jnp.where(k_pos[None, :] <= q_pos[:, None], s, NEG_BIG)
        p = jax.nn.softmax(s, axis=-1).astype(v.dtype)
        return jnp.einsum('bhqk,bkhd->bqhd', p, v)

    o = lax.map(block, (qb, jnp.arange(n_blocks)))
    o = o.transpose(1, 0, 2, 3, 4).reshape(B, n_blocks * Q_BLOCK, H * V_HEAD)
    return o[:, :L]


def hgrn2(q_in, f_in, i_in, g_in, lb, norm_w):
    B, L, _ = q_in.shape
    dt = q_in.dtype
    f32 = jnp.float32
    q = jax.nn.silu(q_in.astype(f32)).reshape(B, L, HG_HEADS, HG_KDIM)
    z = f_in.astype(f32).reshape(B, L, HG_HEADS, HG_KDIM)
    lbf = lb.astype(f32).reshape(HG_HEADS, HG_KDIM)
    f = lbf + (1.0 - lbf) * jax.nn.sigmoid(z)
    log_f = jnp.log(jnp.maximum(f, F_MIN))
    k = (1.0 - lbf) * jax.nn.sigmoid(-z)
    v = i_in.astype(f32).reshape(B, L, HG_HEADS, HG_VDIM)
    front = (-N_META) % HG_CHUNK

    def to_chunks(t):
        t = jnp.pad(t, ((0, 0), (front, 0), (0, 0), (0, 0)))
        n = t.shape[1] // HG_CHUNK
        return t.reshape(B, n, HG_CHUNK, HG_HEADS, t.shape[-1]).transpose(1, 0, 3, 2, 4)

    qc, kc, vc, gc = to_chunks(q), to_chunks(k), to_chunks(v), to_chunks(log_f)
    scale = HG_KDIM ** -0.5
    causal = jnp.tril(jnp.ones((HG_CHUNK, HG_CHUNK), dtype=bool))

    def step(S, inp):
        qt, kt, vt, gt = inp
        b = jnp.cumsum(gt, axis=2)
        diff = b[:, :, :, None, :] - b[:, :, None, :, :]
        decay = jnp.exp(jnp.where(causal[:, :, None], diff, NEG_BIG))
        attn = jnp.einsum('bhtk,bhtsk,bhsk->bhts', qt, decay, kt) * scale
        o = jnp.einsum('bhts,bhsv->bhtv', attn, vt) + \
            jnp.einsum('bhtk,bhkv->bhtv', qt * jnp.exp(b) * scale, S)
        b_last = b[:, :, -1:, :]
        S = jnp.exp(b_last[:, :, 0, :])[..., None] * S + \
            jnp.einsum('bhsk,bhsv->bhkv', kt * jnp.exp(b_last - b), vt)
        return S, o

    S0 = jnp.zeros((B, HG_HEADS, HG_KDIM, HG_VDIM), f32)
    _, o = lax.scan(step, S0, (qc, kc, vc, gc))
    n = o.shape[0]
    o = o.transpose(1, 0, 3, 2, 4).reshape(B, n * HG_CHUNK, HG_HEADS, HG_VDIM)[:, front:]
    o = o * lax.rsqrt(jnp.mean(o * o, axis=-1, keepdims=True) + EPS) * norm_w.astype(f32)
    gate = jax.nn.silu(g_in.astype(f32)).reshape(B, L, HG_HEADS, HG_VDIM)
    return (o * gate).reshape(B, L, HG_WIDTH).astype(dt)


def _fwd_setup_inputs(seed: int = 0) -> dict:
    key = jax.random.key(seed)
    ks = jax.random.split(key, 24)
    f32 = jnp.float32

    def nrm(k, shape, fan_in):
        return jax.random.normal(k, shape, f32) * (fan_in ** -0.5)

    def gain(k, shape):
        return 1.0 + 0.02 * jax.random.normal(k, shape, f32)

    return {
        "x": jax.random.normal(ks[0], (BATCH, SEQ, D_MODEL), f32),
        "meta_tokens": jax.random.normal(ks[1], (N_META, D_MODEL), f32),
        "ffn1_norm": gain(ks[2], (DEPTH, D_MODEL)),
        "ffn1_w_gu": nrm(ks[3], (DEPTH, D_MODEL, 2 * D_FF), D_MODEL),
        "ffn1_w_down": nrm(ks[4], (DEPTH, D_FF, D_MODEL), D_FF),
        "mix_norm": gain(ks[5], (DEPTH, D_MODEL)),
        "w_in": nrm(ks[6], (DEPTH, D_MODEL, D_IN), D_MODEL),
        "q_norm": gain(ks[7], (DEPTH, Q_LORA)),
        "kv_norm": gain(ks[8], (DEPTH, KV_LORA)),
        "w_uq": nrm(ks[9], (DEPTH, Q_LORA, MLA_HEADS * (QK_NOPE + QK_ROPE)), Q_LORA),
        "w_ukv": nrm(ks[10], (DEPTH, KV_LORA, MLA_HEADS * (QK_NOPE + V_HEAD)), KV_LORA),
        "hg_lb_raw": 0.5 * jax.random.normal(ks[11], (DEPTH, HG_FWIDTH), f32),
        "hg_norm": gain(ks[12], (DEPTH, HG_VDIM)),
        "w_proj_attn": nrm(ks[13], (DEPTH, MLA_WIDTH, D_MODEL), MLA_WIDTH),
        "w_proj_rec": nrm(ks[14], (DEPTH, HG_WIDTH, D_MODEL), HG_WIDTH),
        "w_out": nrm(ks[15], (DEPTH, D_MODEL, D_MODEL), D_MODEL),
        "ffn2_norm": gain(ks[16], (DEPTH, D_MODEL)),
        "ffn2_w_gu": nrm(ks[17], (DEPTH, D_MODEL, 2 * D_FF), D_MODEL),
        "ffn2_w_down": nrm(ks[18], (DEPTH, D_FF, D_MODEL), D_FF),
        "final_norm": gain(ks[19], (D_MODEL,)),
    }


def _fwd_reference(x, meta_tokens, ffn1_norm, ffn1_w_gu, ffn1_w_down, mix_norm, w_in, q_norm, kv_norm,
              w_uq, w_ukv, hg_lb_raw, hg_norm, w_proj_attn, w_proj_rec, w_out,
              ffn2_norm, ffn2_w_gu, ffn2_w_down, final_norm):
    B = x.shape[0]
    meta = jnp.broadcast_to(meta_tokens.astype(x.dtype)[None], (B, N_META, D_MODEL))
    h = jnp.concatenate([meta, x], axis=1)
    L = h.shape[1]
    pos = jnp.arange(L)
    p_lb = jax.nn.softmax(hg_lb_raw.astype(jnp.float32), axis=0)
    lbs = jnp.cumsum(p_lb, axis=0) - p_lb[0:1]
    for l in range(DEPTH):
        h = h + 0.5 * swiglu(rms_norm(h, ffn1_norm[l]), ffn1_w_gu[l], ffn1_w_down[l])
        u = rms_norm(h, mix_norm[l])
        c_q, c_kv, k_pe, hq, hf, hi, hg, ga, gb = split_cols(u @ w_in[l], IN_SPLITS)
        y_a = mla(c_q, c_kv, k_pe, pos, q_norm[l], kv_norm[l], w_uq[l], w_ukv[l]) @ w_proj_attn[l]
        y_b = hgrn2(hq, hf, hi, hg, lbs[l], hg_norm[l]) @ w_proj_rec[l]
        merged = jax.nn.sigmoid(ga) * y_a + jax.nn.sigmoid(gb) * y_b
        h = h + merged @ w_out[l]
        h = h + 0.5 * swiglu(rms_norm(h, ffn2_norm[l]), ffn2_w_gu[l], ffn2_w_down[l])
    return rms_norm(h, final_norm)[:, N_META:]


import jax as _jax
import jax.numpy as _jnp

TWIN_FORMAT = 'train_step'
FWD_PARAMS = ['x', 'meta_tokens', 'ffn1_norm', 'ffn1_w_gu', 'ffn1_w_down', 'mix_norm', 'w_in', 'q_norm', 'kv_norm', 'w_uq', 'w_ukv', 'hg_lb_raw', 'hg_norm', 'w_proj_attn', 'w_proj_rec', 'w_out', 'ffn2_norm', 'ffn2_w_gu', 'ffn2_w_down', 'final_norm']
TWIN_WEIGHTS = ['meta_tokens', 'ffn1_norm', 'ffn1_w_gu', 'ffn1_w_down', 'mix_norm', 'w_in', 'q_norm', 'kv_norm', 'w_uq', 'w_ukv', 'hg_lb_raw', 'hg_norm', 'w_proj_attn', 'w_proj_rec', 'w_out', 'ffn2_norm', 'ffn2_w_gu', 'ffn2_w_down', 'final_norm']
TWIN_DIFF_INPUT = 'x'
TWIN_INPUTS = ['x', 'meta_tokens', 'ffn1_norm', 'ffn1_w_gu', 'ffn1_w_down', 'mix_norm', 'w_in', 'q_norm', 'kv_norm', 'w_uq', 'w_ukv', 'hg_lb_raw', 'hg_norm', 'w_proj_attn', 'w_proj_rec', 'w_out', 'ffn2_norm', 'ffn2_w_gu', 'ffn2_w_down', 'final_norm', 'loss_target', 'm_meta_tokens', 'm_ffn1_norm', 'm_ffn1_w_gu', 'm_ffn1_w_down', 'm_mix_norm', 'm_w_in', 'm_q_norm', 'm_kv_norm', 'm_w_uq', 'm_w_ukv', 'm_hg_lb_raw', 'm_hg_norm', 'm_w_proj_attn', 'm_w_proj_rec', 'm_w_out', 'm_ffn2_norm', 'm_ffn2_w_gu', 'm_ffn2_w_down', 'm_final_norm', 'v_meta_tokens', 'v_ffn1_norm', 'v_ffn1_w_gu', 'v_ffn1_w_down', 'v_mix_norm', 'v_w_in', 'v_q_norm', 'v_kv_norm', 'v_w_uq', 'v_w_ukv', 'v_hg_lb_raw', 'v_hg_norm', 'v_w_proj_attn', 'v_w_proj_rec', 'v_w_out', 'v_ffn2_norm', 'v_ffn2_w_gu', 'v_ffn2_w_down', 'v_final_norm']
TWIN_OUTPUTS = ['loss', 'grad_x', 'grad_meta_tokens', 'grad_ffn1_norm', 'grad_ffn1_w_gu', 'grad_ffn1_w_down', 'grad_mix_norm', 'grad_w_in', 'grad_q_norm', 'grad_kv_norm', 'grad_w_uq', 'grad_w_ukv', 'grad_hg_lb_raw', 'grad_hg_norm', 'grad_w_proj_attn', 'grad_w_proj_rec', 'grad_w_out', 'grad_ffn2_norm', 'grad_ffn2_w_gu', 'grad_ffn2_w_down', 'grad_final_norm', 'delta_meta_tokens', 'delta_ffn1_norm', 'delta_ffn1_w_gu', 'delta_ffn1_w_down', 'delta_mix_norm', 'delta_w_in', 'delta_q_norm', 'delta_kv_norm', 'delta_w_uq', 'delta_w_ukv', 'delta_hg_lb_raw', 'delta_hg_norm', 'delta_w_proj_attn', 'delta_w_proj_rec', 'delta_w_out', 'delta_ffn2_norm', 'delta_ffn2_w_gu', 'delta_ffn2_w_down', 'delta_final_norm', 'new_m_meta_tokens', 'new_m_ffn1_norm', 'new_m_ffn1_w_gu', 'new_m_ffn1_w_down', 'new_m_mix_norm', 'new_m_w_in', 'new_m_q_norm', 'new_m_kv_norm', 'new_m_w_uq', 'new_m_w_ukv', 'new_m_hg_lb_raw', 'new_m_hg_norm', 'new_m_w_proj_attn', 'new_m_w_proj_rec', 'new_m_w_out', 'new_m_ffn2_norm', 'new_m_ffn2_w_gu', 'new_m_ffn2_w_down', 'new_m_final_norm', 'new_v_meta_tokens', 'new_v_ffn1_norm', 'new_v_ffn1_w_gu', 'new_v_ffn1_w_down', 'new_v_mix_norm', 'new_v_w_in', 'new_v_q_norm', 'new_v_kv_norm', 'new_v_w_uq', 'new_v_w_ukv', 'new_v_hg_lb_raw', 'new_v_hg_norm', 'new_v_w_proj_attn', 'new_v_w_proj_rec', 'new_v_w_out', 'new_v_ffn2_norm', 'new_v_ffn2_w_gu', 'new_v_ffn2_w_down', 'new_v_final_norm']
TWIN_LEAF_KINDS = {'loss': 'loss', 'grad_x': 'grad_x', 'grad_meta_tokens': 'grad_w', 'grad_ffn1_norm': 'grad_w', 'grad_ffn1_w_gu': 'grad_w', 'grad_ffn1_w_down': 'grad_w', 'grad_mix_norm': 'grad_w', 'grad_w_in': 'grad_w', 'grad_q_norm': 'grad_w', 'grad_kv_norm': 'grad_w', 'grad_w_uq': 'grad_w', 'grad_w_ukv': 'grad_w', 'grad_hg_lb_raw': 'grad_w', 'grad_hg_norm': 'grad_w', 'grad_w_proj_attn': 'grad_w', 'grad_w_proj_rec': 'grad_w', 'grad_w_out': 'grad_w', 'grad_ffn2_norm': 'grad_w', 'grad_ffn2_w_gu': 'grad_w', 'grad_ffn2_w_down': 'grad_w', 'grad_final_norm': 'grad_w', 'delta_meta_tokens': 'delta_w', 'delta_ffn1_norm': 'delta_w', 'delta_ffn1_w_gu': 'delta_w', 'delta_ffn1_w_down': 'delta_w', 'delta_mix_norm': 'delta_w', 'delta_w_in': 'delta_w', 'delta_q_norm': 'delta_w', 'delta_kv_norm': 'delta_w', 'delta_w_uq': 'delta_w', 'delta_w_ukv': 'delta_w', 'delta_hg_lb_raw': 'delta_w', 'delta_hg_norm': 'delta_w', 'delta_w_proj_attn': 'delta_w', 'delta_w_proj_rec': 'delta_w', 'delta_w_out': 'delta_w', 'delta_ffn2_norm': 'delta_w', 'delta_ffn2_w_gu': 'delta_w', 'delta_ffn2_w_down': 'delta_w', 'delta_final_norm': 'delta_w', 'new_m_meta_tokens': 'new_m', 'new_m_ffn1_norm': 'new_m', 'new_m_ffn1_w_gu': 'new_m', 'new_m_ffn1_w_down': 'new_m', 'new_m_mix_norm': 'new_m', 'new_m_w_in': 'new_m', 'new_m_q_norm': 'new_m', 'new_m_kv_norm': 'new_m', 'new_m_w_uq': 'new_m', 'new_m_w_ukv': 'new_m', 'new_m_hg_lb_raw': 'new_m', 'new_m_hg_norm': 'new_m', 'new_m_w_proj_attn': 'new_m', 'new_m_w_proj_rec': 'new_m', 'new_m_w_out': 'new_m', 'new_m_ffn2_norm': 'new_m', 'new_m_ffn2_w_gu': 'new_m', 'new_m_ffn2_w_down': 'new_m', 'new_m_final_norm': 'new_m', 'new_v_meta_tokens': 'new_v', 'new_v_ffn1_norm': 'new_v', 'new_v_ffn1_w_gu': 'new_v', 'new_v_ffn1_w_down': 'new_v', 'new_v_mix_norm': 'new_v', 'new_v_w_in': 'new_v', 'new_v_q_norm': 'new_v', 'new_v_kv_norm': 'new_v', 'new_v_w_uq': 'new_v', 'new_v_w_ukv': 'new_v', 'new_v_hg_lb_raw': 'new_v', 'new_v_hg_norm': 'new_v', 'new_v_w_proj_attn': 'new_v', 'new_v_w_proj_rec': 'new_v', 'new_v_w_out': 'new_v', 'new_v_ffn2_norm': 'new_v', 'new_v_ffn2_w_gu': 'new_v', 'new_v_ffn2_w_down': 'new_v', 'new_v_final_norm': 'new_v'}


def _forward(args):
    return _fwd_reference(*[args[k] for k in FWD_PARAMS])


def _output_shape():
    def fwd():
        inp = _fwd_setup_inputs(0)
        return _fwd_reference(*[inp[k] for k in FWD_PARAMS])
    out = _jax.eval_shape(fwd)
    return out.shape, out.dtype

N_MICROBATCH = 1
ADAM_LR = 0.001
ADAM_B1 = 0.9
ADAM_B2 = 0.999
ADAM_EPS = 1e-08
ADAM_WD = 0.01
ADAM_STEP = 10
PER_EXAMPLE_BATCH_AXIS = {'x': 0, 'loss_target': 0}
SHARED_INPUTS = []
_WEIGHT_DTYPES = {'meta_tokens': _jnp.float32, 'ffn1_norm': _jnp.float32, 'ffn1_w_gu': _jnp.float32, 'ffn1_w_down': _jnp.float32, 'mix_norm': _jnp.float32, 'w_in': _jnp.float32, 'q_norm': _jnp.float32, 'kv_norm': _jnp.float32, 'w_uq': _jnp.float32, 'w_ukv': _jnp.float32, 'hg_lb_raw': _jnp.float32, 'hg_norm': _jnp.float32, 'w_proj_attn': _jnp.float32, 'w_proj_rec': _jnp.float32, 'w_out': _jnp.float32, 'ffn2_norm': _jnp.float32, 'ffn2_w_gu': _jnp.float32, 'ffn2_w_down': _jnp.float32, 'final_norm': _jnp.float32}
MOMENT_SCALE = {'meta_tokens': 1.160026e-02, 'ffn1_norm': 1.134633e-01, 'ffn1_w_gu': 4.842040e-02, 'ffn1_w_down': 7.900345e-02, 'mix_norm': 1.256447e-01, 'w_in': 5.795538e-02, 'q_norm': 3.545188e-02, 'kv_norm': 7.075202e-02, 'w_uq': 2.493254e-02, 'w_ukv': 3.308876e-02, 'hg_lb_raw': 8.817008e-03, 'hg_norm': 2.290766e-01, 'w_proj_attn': 2.777283e-02, 'w_proj_rec': 7.909516e-02, 'w_out': 8.341991e-02, 'ffn2_norm': 9.816118e-02, 'ffn2_w_gu': 4.186705e-02, 'ffn2_w_down': 6.837654e-02, 'final_norm': 6.398679e+01}


def _to_microbatches(a, axis):
    t = _jnp.moveaxis(a, axis, 0)
    t = t.reshape((N_MICROBATCH, t.shape[0] // N_MICROBATCH) + t.shape[1:])
    return _jnp.moveaxis(t, 1, axis + 1)


def setup_inputs(seed: int = 0) -> dict:
    inp = _fwd_setup_inputs(seed)
    key = _jax.random.fold_in(_jax.random.key(seed), 7919)
    shape, _ = _output_shape()
    out = dict(inp)
    out["loss_target"] = _jax.random.normal(_jax.random.fold_in(key, 0), shape, _jnp.float32)
    for i, name in enumerate(TWIN_WEIGHTS):
        w = inp[name].astype(_jnp.float32)
        if MOMENT_SCALE is None:
            s = _jnp.sqrt(_jnp.mean(_jnp.square(w)) + 1e-30)
        else:
            s = MOMENT_SCALE[name]
        km, kv = _jax.random.split(_jax.random.fold_in(key, i + 1))
        out[name] = w
        out["m_" + name] = s * _jax.random.normal(km, w.shape, _jnp.float32)
        out["v_" + name] = (s * s) * _jax.random.uniform(kv, w.shape, _jnp.float32, 0.5, 1.5)
    if N_MICROBATCH > 1:
        for name, axis in PER_EXAMPLE_BATCH_AXIS.items():
            out[name] = _to_microbatches(out[name], axis)
    return {'x': out['x'], 'meta_tokens': out['meta_tokens'], 'ffn1_norm': out['ffn1_norm'], 'ffn1_w_gu': out['ffn1_w_gu'], 'ffn1_w_down': out['ffn1_w_down'], 'mix_norm': out['mix_norm'], 'w_in': out['w_in'], 'q_norm': out['q_norm'], 'kv_norm': out['kv_norm'], 'w_uq': out['w_uq'], 'w_ukv': out['w_ukv'], 'hg_lb_raw': out['hg_lb_raw'], 'hg_norm': out['hg_norm'], 'w_proj_attn': out['w_proj_attn'], 'w_proj_rec': out['w_proj_rec'], 'w_out': out['w_out'], 'ffn2_norm': out['ffn2_norm'], 'ffn2_w_gu': out['ffn2_w_gu'], 'ffn2_w_down': out['ffn2_w_down'], 'final_norm': out['final_norm'], 'loss_target': out['loss_target'], 'm_meta_tokens': out['m_meta_tokens'], 'm_ffn1_norm': out['m_ffn1_norm'], 'm_ffn1_w_gu': out['m_ffn1_w_gu'], 'm_ffn1_w_down': out['m_ffn1_w_down'], 'm_mix_norm': out['m_mix_norm'], 'm_w_in': out['m_w_in'], 'm_q_norm': out['m_q_norm'], 'm_kv_norm': out['m_kv_norm'], 'm_w_uq': out['m_w_uq'], 'm_w_ukv': out['m_w_ukv'], 'm_hg_lb_raw': out['m_hg_lb_raw'], 'm_hg_norm': out['m_hg_norm'], 'm_w_proj_attn': out['m_w_proj_attn'], 'm_w_proj_rec': out['m_w_proj_rec'], 'm_w_out': out['m_w_out'], 'm_ffn2_norm': out['m_ffn2_norm'], 'm_ffn2_w_gu': out['m_ffn2_w_gu'], 'm_ffn2_w_down': out['m_ffn2_w_down'], 'm_final_norm': out['m_final_norm'], 'v_meta_tokens': out['v_meta_tokens'], 'v_ffn1_norm': out['v_ffn1_norm'], 'v_ffn1_w_gu': out['v_ffn1_w_gu'], 'v_ffn1_w_down': out['v_ffn1_w_down'], 'v_mix_norm': out['v_mix_norm'], 'v_w_in': out['v_w_in'], 'v_q_norm': out['v_q_norm'], 'v_kv_norm': out['v_kv_norm'], 'v_w_uq': out['v_w_uq'], 'v_w_ukv': out['v_w_ukv'], 'v_hg_lb_raw': out['v_hg_lb_raw'], 'v_hg_norm': out['v_hg_norm'], 'v_w_proj_attn': out['v_w_proj_attn'], 'v_w_proj_rec': out['v_w_proj_rec'], 'v_w_out': out['v_w_out'], 'v_ffn2_norm': out['v_ffn2_norm'], 'v_ffn2_w_gu': out['v_ffn2_w_gu'], 'v_ffn2_w_down': out['v_ffn2_w_down'], 'v_final_norm': out['v_final_norm']}


def _loss(weights, diff, rest, loss_target):
    with _jax.named_scope("forward"):
        args = {**rest, TWIN_DIFF_INPUT: diff, **{k: w.astype(_WEIGHT_DTYPES[k]) for k, w in weights.items()}}
        y = _forward(args)
    with _jax.named_scope("loss_head"):
        err = _jnp.square(y.astype(_jnp.float32) - loss_target)
        return 0.5 * _jnp.sum(_jnp.mean(err, axis=-1)) if err.ndim else 0.5 * err


def _adamw(w, g, m, v):
    m = ADAM_B1 * m + (1.0 - ADAM_B1) * g
    v = ADAM_B2 * v + (1.0 - ADAM_B2) * _jnp.square(g)
    m_hat = m / (1.0 - ADAM_B1 ** ADAM_STEP)
    v_hat = v / (1.0 - ADAM_B2 ** ADAM_STEP)
    delta = -ADAM_LR * (m_hat / (_jnp.sqrt(v_hat) + ADAM_EPS) + ADAM_WD * w)
    return delta, m, v


def reference(x, meta_tokens, ffn1_norm, ffn1_w_gu, ffn1_w_down, mix_norm, w_in, q_norm, kv_norm, w_uq, w_ukv, hg_lb_raw, hg_norm, w_proj_attn, w_proj_rec, w_out, ffn2_norm, ffn2_w_gu, ffn2_w_down, final_norm, loss_target, m_meta_tokens, m_ffn1_norm, m_ffn1_w_gu, m_ffn1_w_down, m_mix_norm, m_w_in, m_q_norm, m_kv_norm, m_w_uq, m_w_ukv, m_hg_lb_raw, m_hg_norm, m_w_proj_attn, m_w_proj_rec, m_w_out, m_ffn2_norm, m_ffn2_w_gu, m_ffn2_w_down, m_final_norm, v_meta_tokens, v_ffn1_norm, v_ffn1_w_gu, v_ffn1_w_down, v_mix_norm, v_w_in, v_q_norm, v_kv_norm, v_w_uq, v_w_ukv, v_hg_lb_raw, v_hg_norm, v_w_proj_attn, v_w_proj_rec, v_w_out, v_ffn2_norm, v_ffn2_w_gu, v_ffn2_w_down, v_final_norm):
    given = dict(x=x, meta_tokens=meta_tokens, ffn1_norm=ffn1_norm, ffn1_w_gu=ffn1_w_gu, ffn1_w_down=ffn1_w_down, mix_norm=mix_norm, w_in=w_in, q_norm=q_norm, kv_norm=kv_norm, w_uq=w_uq, w_ukv=w_ukv, hg_lb_raw=hg_lb_raw, hg_norm=hg_norm, w_proj_attn=w_proj_attn, w_proj_rec=w_proj_rec, w_out=w_out, ffn2_norm=ffn2_norm, ffn2_w_gu=ffn2_w_gu, ffn2_w_down=ffn2_w_down, final_norm=final_norm, loss_target=loss_target, m_meta_tokens=m_meta_tokens, m_ffn1_norm=m_ffn1_norm, m_ffn1_w_gu=m_ffn1_w_gu, m_ffn1_w_down=m_ffn1_w_down, m_mix_norm=m_mix_norm, m_w_in=m_w_in, m_q_norm=m_q_norm, m_kv_norm=m_kv_norm, m_w_uq=m_w_uq, m_w_ukv=m_w_ukv, m_hg_lb_raw=m_hg_lb_raw, m_hg_norm=m_hg_norm, m_w_proj_attn=m_w_proj_attn, m_w_proj_rec=m_w_proj_rec, m_w_out=m_w_out, m_ffn2_norm=m_ffn2_norm, m_ffn2_w_gu=m_ffn2_w_gu, m_ffn2_w_down=m_ffn2_w_down, m_final_norm=m_final_norm, v_meta_tokens=v_meta_tokens, v_ffn1_norm=v_ffn1_norm, v_ffn1_w_gu=v_ffn1_w_gu, v_ffn1_w_down=v_ffn1_w_down, v_mix_norm=v_mix_norm, v_w_in=v_w_in, v_q_norm=v_q_norm, v_kv_norm=v_kv_norm, v_w_uq=v_w_uq, v_w_ukv=v_w_ukv, v_hg_lb_raw=v_hg_lb_raw, v_hg_norm=v_hg_norm, v_w_proj_attn=v_w_proj_attn, v_w_proj_rec=v_w_proj_rec, v_w_out=v_w_out, v_ffn2_norm=v_ffn2_norm, v_ffn2_w_gu=v_ffn2_w_gu, v_ffn2_w_down=v_ffn2_w_down, v_final_norm=v_final_norm)
    weights = {n: given[n] for n in TWIN_WEIGHTS}
    shared = {n: given[n] for n in SHARED_INPUTS}
    per_example = {n: given[n] for n in ['x']}
    grad_fn = _jax.value_and_grad(_loss, argnums=(0, 1))

    def one_microbatch(ex, loss_target):
        ex = dict(ex)
        diff = ex.pop(TWIN_DIFF_INPUT)
        return grad_fn(weights, diff, {**shared, **ex}, loss_target)

    if N_MICROBATCH == 1:
        loss, (grad_w, grad_x) = one_microbatch(per_example, given["loss_target"])
    else:
        def body(carry, xs):
            loss_sum, grad_sum = carry
            l_k, (gw_k, gx_k) = one_microbatch(xs[0], xs[1])
            with _jax.named_scope("update"):
                return (loss_sum + l_k, _jax.tree.map(_jnp.add, grad_sum, gw_k)), gx_k

        init = (_jnp.zeros((), _jnp.float32), _jax.tree.map(_jnp.zeros_like, weights))
        (loss, grad_w), grad_x = _jax.lax.scan(body, init, (per_example, given["loss_target"]))
    with _jax.named_scope("update"):
        delta_w, new_m, new_v = {}, {}, {}
        for n in TWIN_WEIGHTS:
            delta_w[n], new_m[n], new_v[n] = _adamw(weights[n], grad_w[n], given["m_" + n], given["v_" + n])
    return (loss, grad_x, *[grad_w[n] for n in TWIN_WEIGHTS], *[delta_w[n] for n in TWIN_WEIGHTS],
            *[new_m[n] for n in TWIN_WEIGHTS], *[new_v[n] for n in TWIN_WEIGHTS])
```

```python
import functools
import math

import numpy as np
import jax
import jax.numpy as jnp
from jax import lax
from jax.experimental import pallas as pl
from jax.experimental.pallas import tpu as pltpu

F32 = jnp.float32
BF16 = jnp.bfloat16

N_META = 16
MLA_HEADS = 8
Q_LORA = 384
KV_LORA = 256
QK_NOPE = 64
QK_ROPE = 32
V_HEAD = 64
ROPE_THETA = 10000.0
HG_HEADS = 4
HG_D = 128
HG_CHUNK = 64
EPS = 1e-6
NEG_BIG = -1e30
F_MIN = 1e-20
DEPTH = 4

ADAM_LR = 0.001
ADAM_B1 = 0.9
ADAM_B2 = 0.999
ADAM_EPS = 1e-08
ADAM_WD = 0.01
ADAM_STEP = 10

LANES = 128
FRONT = (-N_META) % HG_CHUNK
ROW_X = FRONT + N_META
ROW_TILE = 640
HEAD_W = 128
QW = MLA_HEADS * HEAD_W
VMEM_LIMIT = 48 * 1024 * 1024
PACK_W = 1024
MESH = pl.DeviceIdType.MESH

Z_Q, Z_KPA, Z_KV, Z_KPB, Z_PAD, Z_GA = 0, 384, 512, 768, 896, 1024


def _z_layout(d):
    ga = Z_GA
    gb = ga + d
    hq = gb + d
    hf = hq + 512
    hi = hf + 512
    hg = hi + 512
    return dict(ga=ga, gb=gb, hq=hq, hf=hf, hi=hi, hg=hg, total=hg + 512)


def _pick(dim, cap, mult=LANES):
    if dim <= cap:
        return dim
    best = None
    for t in range(mult, cap + 1, mult):
        if dim % t == 0:
            best = t
    assert best is not None, (dim, cap, mult)
    return best


def _params(*sem):
    return pltpu.CompilerParams(dimension_semantics=sem, vmem_limit_bytes=VMEM_LIMIT)


def _sigmoid(x):
    return 1.0 / (1.0 + jnp.exp(-x))


def _row_valid(row0, n, lv):
    r = row0 + lax.broadcasted_iota(jnp.int32, (n, 1), 0)
    return ((r >= FRONT) & (r < lv)).astype(F32)


_DIMS = {"nn": (((1,), (0,)), ((), ())), "nt": (((1,), (1,)), ((), ())), "tn": (((0,), (0,)), ((), ()))}


def _matmul(pairs, mode, *, name, out_dtype=F32, res=None, scale=1.0):
    a0, b0 = pairs[0]
    if mode == "nn":
        (m, k), n = a0.shape, b0.shape[1]
    elif mode == "nt":
        (m, k), n = a0.shape, b0.shape[0]
    else:
        (k, m), n = a0.shape, b0.shape[1]
    if mode == "tn":
        tm, tn, tk = _pick(m, 1024), _pick(n, 1408), _pick(k, ROW_TILE, 8)
    else:
        tm, tn, tk = _pick(m, ROW_TILE, 8), _pick(n, 1408), _pick(k, 1408)
    nk = k // tk
    npair = len(pairs)
    dims = _DIMS[mode]

    def body(*refs):
        ins = refs[:2 * npair]
        pos = 2 * npair
        res_ref = None
        if res is not None:
            res_ref = refs[pos]
            pos += 1
        o_ref, acc = refs[pos], refs[pos + 1]
        kk = pl.program_id(2)

        @pl.when(kk == 0)
        def _():
            acc[...] = jnp.zeros_like(acc)

        part = None
        for p in range(npair):
            a = ins[2 * p][...].astype(BF16)
            b = ins[2 * p + 1][...].astype(BF16)
            d = lax.dot_general(a, b, dims, preferred_element_type=F32)
            part = d if part is None else part + d
        acc[...] += part

        @pl.when(kk == nk - 1)
        def _():
            r = acc[...]
            if scale != 1.0:
                r = r * scale
            if res_ref is not None:
                r = r + res_ref[...]
            o_ref[...] = r.astype(out_dtype)

    if mode == "nn":
        a_spec = pl.BlockSpec((tm, tk), lambda i, j, q: (i, q))
        b_spec = pl.BlockSpec((tk, tn), lambda i, j, q: (q, j))
    elif mode == "nt":
        a_spec = pl.BlockSpec((tm, tk), lambda i, j, q: (i, q))
        b_spec = pl.BlockSpec((tn, tk), lambda i, j, q: (j, q))
    else:
        a_spec = pl.BlockSpec((tk, tm), lambda i, j, q: (q, i))
        b_spec = pl.BlockSpec((tk, tn), lambda i, j, q: (q, j))
    o_spec = pl.BlockSpec((tm, tn), lambda i, j, q: (i, j))
    in_specs, args = [], []
    for a, b in pairs:
        in_specs += [a_spec, b_spec]
        args += [a, b]
    if res is not None:
        in_specs.append(o_spec)
        args.append(res)
    return pl.pallas_call(
        body, name=name, grid=(m // tm, n // tn, nk), in_specs=in_specs, out_specs=o_spec,
        out_shape=jax.ShapeDtypeStruct((m, n), out_dtype),
        scratch_shapes=[pltpu.VMEM((tm, tn), F32)],
        compiler_params=_params("parallel", "parallel", "arbitrary"),
    )(*args)


def _rmsnorm_fwd(x, w, *, width, col_block, name):
    lp = x.shape[0]
    tm = _pick(lp, ROW_TILE, 8)

    def body(x_ref, w_ref, o_ref):
        xv = x_ref[...]
        r = lax.rsqrt(jnp.mean(xv * xv, axis=-1, keepdims=True) + EPS)
        o_ref[...] = (xv * r * w_ref[...]).astype(BF16)

    return pl.pallas_call(
        body, name=name, grid=(lp // tm,),
        in_specs=[pl.BlockSpec((tm, width), lambda i: (i, col_block)), pl.BlockSpec((1, width), lambda i: (0, 0))],
        out_specs=pl.BlockSpec((tm, width), lambda i: (i, 0)),
        out_shape=jax.ShapeDtypeStruct((lp, width), BF16),
        compiler_params=_params("parallel"),
    )(x, w.reshape(1, width))


def _rmsnorm_bwd(x, w, dy, *, width, col_block, lv, name, dres=None):
    lp = x.shape[0]
    tm = _pick(lp, ROW_TILE, 8)

    def body(*refs):
        if dres is None:
            x_ref, w_ref, dy_ref, dx_ref, dw_ref = refs
            dres_ref = None
        else:
            x_ref, w_ref, dy_ref, dres_ref, dx_ref, dw_ref = refs
        i = pl.program_id(0)
        xv = x_ref[...]
        dyv = dy_ref[...] * _row_valid(i * tm, tm, lv)
        r = lax.rsqrt(jnp.mean(xv * xv, axis=-1, keepdims=True) + EPS)
        wdy = dyv * w_ref[...]
        dx = r * wdy - xv * (r * r * r) * jnp.mean(xv * wdy, axis=-1, keepdims=True)
        if dres_ref is not None:
            dx = dx + dres_ref[...]
        dx_ref[...] = dx

        @pl.when(i == 0)
        def _():
            dw_ref[...] = jnp.zeros_like(dw_ref)

        dw_ref[...] += jnp.sum(dyv * xv * r, axis=0, keepdims=True)

    row = pl.BlockSpec((tm, width), lambda i: (i, 0))
    in_specs = [pl.BlockSpec((tm, width), lambda i: (i, col_block)), pl.BlockSpec((1, width), lambda i: (0, 0)), row]
    args = [x, w.reshape(1, width), dy]
    if dres is not None:
        in_specs.append(row)
        args.append(dres)
    dx, dw = pl.pallas_call(
        body, name=name, grid=(lp // tm,), in_specs=in_specs,
        out_specs=[row, pl.BlockSpec((1, width), lambda i: (0, 0))],
        out_shape=[jax.ShapeDtypeStruct((lp, width), F32), jax.ShapeDtypeStruct((1, width), F32)],
        compiler_params=_params("arbitrary"),
    )(*args)
    return dx, dw[0]


def _swiglu_fwd(g, u, *, name):
    lp, f = g.shape
    tm, tf = _pick(lp, ROW_TILE, 8), _pick(f, 1408)

    def body(g_ref, u_ref, o_ref):
        gv = g_ref[...]
        o_ref[...] = (gv * _sigmoid(gv) * u_ref[...]).astype(BF16)

    spec = pl.BlockSpec((tm, tf), lambda i, j: (i, j))
    return pl.pallas_call(
        body, name=name, grid=(lp // tm, f // tf), in_specs=[spec, spec], out_specs=spec,
        out_shape=jax.ShapeDtypeStruct((lp, f), BF16), compiler_params=_params("parallel", "parallel"),
    )(g, u)


def _swiglu_bwd(dact, g, u, *, name):
    lp, f = g.shape
    tm, tf = _pick(lp, ROW_TILE, 8), _pick(f, 1408)

    def body(d_ref, g_ref, u_ref, dg_ref, du_ref):
        gv, dv = g_ref[...], d_ref[...]
        s = _sigmoid(gv)
        dg_ref[...] = (dv * u_ref[...] * s * (1.0 + gv * (1.0 - s))).astype(BF16)
        du_ref[...] = (dv * gv * s).astype(BF16)

    spec = pl.BlockSpec((tm, tf), lambda i, j: (i, j))
    return pl.pallas_call(
        body, name=name, grid=(lp // tm, f // tf), in_specs=[spec, spec, spec], out_specs=[spec, spec],
        out_shape=[jax.ShapeDtypeStruct((lp, f), BF16)] * 2, compiler_params=_params("parallel", "parallel"),
    )(dact, g, u)


def _merge_fwd(ya, yb, z, *, zl, name):
    lp, d = ya.shape
    tm, td = _pick(lp, ROW_TILE, 8), _pick(d, 512)
    oa, ob = zl["ga"] // td, zl["gb"] // td

    def body(ya_ref, yb_ref, ga_ref, gb_ref, o_ref):
        o_ref[...] = (_sigmoid(ga_ref[...]) * ya_ref[...] + _sigmoid(gb_ref[...]) * yb_ref[...]).astype(BF16)

    spec = pl.BlockSpec((tm, td), lambda i, j: (i, j))
    return pl.pallas_call(
        body, name=name, grid=(lp // tm, d // td),
        in_specs=[spec, spec, pl.BlockSpec((tm, td), lambda i, j: (i, oa + j)),
                  pl.BlockSpec((tm, td), lambda i, j: (i, ob + j))],
        out_specs=spec, out_shape=jax.ShapeDtypeStruct((lp, d), BF16),
        compiler_params=_params("parallel", "parallel"),
    )(ya, yb, z, z)


def _merge_bwd(dmg, ya, yb, z, *, zl, name):
    lp, d = ya.shape
    tm, td = _pick(lp, ROW_TILE, 8), _pick(d, 512)
    oa, ob = zl["ga"] // td, zl["gb"] // td

    def body(d_ref, ya_ref, yb_ref, ga_ref, gb_ref, dya_ref, dyb_ref, dga_ref, dgb_ref):
        dv = d_ref[...]
        sa, sb = _sigmoid(ga_ref[...]), _sigmoid(gb_ref[...])
        dya_ref[...] = (dv * sa).astype(BF16)
        dyb_ref[...] = (dv * sb).astype(BF16)
        dga_ref[...] = dv * ya_ref[...] * sa * (1.0 - sa)
        dgb_ref[...] = dv * yb_ref[...] * sb * (1.0 - sb)

    spec = pl.BlockSpec((tm, td), lambda i, j: (i, j))
    return pl.pallas_call(
        body, name=name, grid=(lp // tm, d // td),
        in_specs=[spec, spec, spec, pl.BlockSpec((tm, td), lambda i, j: (i, oa + j)),
                  pl.BlockSpec((tm, td), lambda i, j: (i, ob + j))],
        out_specs=[spec] * 4,
        out_shape=[jax.ShapeDtypeStruct((lp, d), BF16)] * 2 + [jax.ShapeDtypeStruct((lp, d), F32)] * 2,
        compiler_params=_params("parallel", "parallel"),
    )(dmg, ya, yb, z, z)


def _qkv_prep_fwd(q2, kv, z, cos_t, sin_t, *, name):
    lp = q2.shape[0]
    tm = _pick(lp, ROW_TILE, 8)
    h = MLA_HEADS

    def body(qa_ref, qb_ref, kv_ref, za_ref, zb_ref, c_ref, s_ref, q_ref, k_ref, v_ref):
        c, s = c_ref[...], s_ref[...]
        lane = lax.broadcasted_iota(jnp.int32, (tm, HEAD_W), 1)
        q_ref[...] = (qa_ref[...] * c + qb_ref[...] * s).astype(BF16)
        kr = jnp.where(lane >= QK_NOPE, za_ref[...] * c + zb_ref[...] * s, 0.0)
        kvv = kv_ref[...]
        k_ref[...] = (jnp.where(lane < QK_NOPE, kvv, 0.0) + kr).astype(BF16)
        v_ref[...] = jnp.where(lane >= QK_NOPE, kvv, 0.0).astype(BF16)

    blk = lambda f: pl.BlockSpec((tm, HEAD_W), f)
    out = blk(lambda i, j: (i, j))
    return pl.pallas_call(
        body, name=name, grid=(lp // tm, h),
        in_specs=[blk(lambda i, j: (i, j)), blk(lambda i, j: (i, h + j)), blk(lambda i, j: (i, j)),
                  blk(lambda i, j: (i, Z_KPA // HEAD_W)), blk(lambda i, j: (i, Z_KPB // HEAD_W)),
                  blk(lambda i, j: (i, 0)), blk(lambda i, j: (i, 0))],
        out_specs=[out, out, out], out_shape=[jax.ShapeDtypeStruct((lp, QW), BF16)] * 3,
        compiler_params=_params("parallel", "parallel"),
    )(q2, q2, kv, z, z, cos_t, sin_t)


def _qkv_prep_bwd(dq, dk, dv, cos_t, sin_t, *, name):
    lp = dq.shape[0]
    tm = _pick(lp, ROW_TILE, 8)
    h = MLA_HEADS

    def body(dq_ref, dk_ref, dv_ref, c_ref, s_ref, dqa_ref, dqb_ref, dkv_ref, dza_ref, dzb_ref):
        j = pl.program_id(1)
        c, s = c_ref[...], s_ref[...]
        lane = lax.broadcasted_iota(jnp.int32, (tm, HEAD_W), 1)
        dqv, dkv_ = dq_ref[...], dk_ref[...]
        dqa_ref[...] = (dqv * c).astype(BF16)
        dqb_ref[...] = (dqv * s).astype(BF16)
        dkv_ref[...] = jnp.where(lane < QK_NOPE, dkv_, dv_ref[...]).astype(BF16)
        dkr = jnp.where(lane >= QK_NOPE, dkv_, 0.0)

        @pl.when(j == 0)
        def _():
            dza_ref[...] = jnp.zeros_like(dza_ref)
            dzb_ref[...] = jnp.zeros_like(dzb_ref)

        dza_ref[...] += dkr * c
        dzb_ref[...] += dkr * s

    blk = lambda f: pl.BlockSpec((tm, HEAD_W), f)
    per_head, shared = blk(lambda i, j: (i, j)), blk(lambda i, j: (i, 0))
    return pl.pallas_call(
        body, name=name, grid=(lp // tm, h),
        in_specs=[per_head, per_head, per_head, shared, shared],
        out_specs=[per_head, per_head, per_head, shared, shared],
        out_shape=[jax.ShapeDtypeStruct((lp, QW), BF16)] * 3 + [jax.ShapeDtypeStruct((lp, HEAD_W), F32)] * 2,
        compiler_params=_params("parallel", "arbitrary"),
    )(dq, dk, dv, cos_t, sin_t)


def _attn_tile(lp):
    return _pick(lp, ROW_TILE, LANES)


def _attn_mask(qb, kb, t):
    qpos = qb * t + lax.broadcasted_iota(jnp.int32, (t, t), 0)
    kpos = kb * t + lax.broadcasted_iota(jnp.int32, (t, t), 1)
    return (kpos <= qpos) & (kpos >= FRONT)


def _attn_fwd(q, k, v, *, lv, name):
    lp = q.shape[0]
    t = _attn_tile(lp)
    nb = lp // t
    scale = (QK_NOPE + QK_ROPE) ** -0.5

    def body(q_ref, k_ref, v_ref, o_ref, lse_ref, m_s, l_s, acc_s):
        qb, kb = pl.program_id(1), pl.program_id(2)

        @pl.when(kb == 0)
        def _():
            m_s[...] = jnp.full_like(m_s, NEG_BIG)
            l_s[...] = jnp.zeros_like(l_s)
            acc_s[...] = jnp.zeros_like(acc_s)

        def step(masked):
            s = lax.dot_general(q_ref[...], k_ref[...], _DIMS["nt"], preferred_element_type=F32) * scale
            if masked:
                s = jnp.where(_attn_mask(qb, kb, t), s, NEG_BIG)
            m_prev = m_s[...]
            m_new = jnp.maximum(m_prev, jnp.max(s, axis=-1, keepdims=True))
            alpha = jnp.exp(m_prev - m_new)
            p = jnp.exp(s - m_new)
            l_s[...] = alpha * l_s[...] + jnp.sum(p, axis=-1, keepdims=True)
            acc_s[...] = alpha * acc_s[...] + jnp.dot(p.astype(BF16), v_ref[...], preferred_element_type=F32)
            m_s[...] = m_new

        @pl.when((kb > 0) & (kb < qb))
        def _():
            step(False)

        @pl.when((kb <= qb) & ((kb == qb) | (kb == 0)))
        def _():
            step(True)

        @pl.when(kb == nb - 1)
        def _():
            l = l_s[...]
            o_ref[...] = acc_s[...] / l * _row_valid(qb * t, t, lv)
            lse_ref[...] = jnp.broadcast_to(m_s[...] + jnp.log(l), (t, HEAD_W))

    qs = pl.BlockSpec((t, HEAD_W), lambda h, i, j: (i, h))
    ks = pl.BlockSpec((t, HEAD_W), lambda h, i, j: (jnp.minimum(j, i), h))
    return pl.pallas_call(
        body, name=name, grid=(MLA_HEADS, nb, nb), in_specs=[qs, ks, ks], out_specs=[qs, qs],
        out_shape=[jax.ShapeDtypeStruct((lp, QW), F32)] * 2,
        scratch_shapes=[pltpu.VMEM((t, 1), F32), pltpu.VMEM((t, 1), F32), pltpu.VMEM((t, HEAD_W), F32)],
        compiler_params=_params("parallel", "parallel", "arbitrary"),
    )(q, k, v)


def _attn_delta(do, o, *, name):
    lp = do.shape[0]
    tm = _pick(lp, ROW_TILE, 8)

    def body(do_ref, o_ref, d_ref):
        d_ref[...] = jnp.broadcast_to(jnp.sum(do_ref[...] * o_ref[...], axis=-1, keepdims=True), (tm, HEAD_W))

    spec = pl.BlockSpec((tm, HEAD_W), lambda i, j: (i, j))
    return pl.pallas_call(
        body, name=name, grid=(lp // tm, MLA_HEADS), in_specs=[spec, spec], out_specs=spec,
        out_shape=jax.ShapeDtypeStruct((lp, QW), F32), compiler_params=_params("parallel", "parallel"),
    )(do, o)


def _attn_bwd(q, k, v, do, lse, delta, *, name):
    lp = q.shape[0]
    t = _attn_tile(lp)
    nb = lp // t
    scale = (QK_NOPE + QK_ROPE) ** -0.5

    def body(q_ref, k_ref, v_ref, do_ref, lse_ref, dl_ref, dq_ref, dk_ref, dv_ref, dk_s, dv_s):
        kb, qb = pl.program_id(1), pl.program_id(2)

        @pl.when(qb == 0)
        def _():
            dk_s[...] = jnp.zeros_like(dk_s)
            dv_s[...] = jnp.zeros_like(dv_s)

        def step(masked):
            qv, kv_, vv = q_ref[...], k_ref[...], v_ref[...]
            dov = do_ref[...].astype(BF16)
            s = lax.dot_general(qv, kv_, _DIMS["nt"], preferred_element_type=F32) * scale
            p = jnp.exp(s - lse_ref[:, :1])
            if masked:
                p = jnp.where(_attn_mask(qb, kb, t), p, 0.0)
            dv_s[...] += lax.dot_general(p.astype(BF16), dov, _DIMS["tn"], preferred_element_type=F32)
            dp = lax.dot_general(dov, vv, _DIMS["nt"], preferred_element_type=F32)
            ds = (p * (dp - dl_ref[:, :1]) * scale).astype(BF16)
            dk_s[...] += lax.dot_general(ds, qv, _DIMS["tn"], preferred_element_type=F32)
            contrib = jnp.dot(ds, kv_, preferred_element_type=F32)
            rows = pl.ds(pl.multiple_of(qb * t, t), t)

            @pl.when(kb == 0)
            def _():
                dq_ref[rows, :] = contrib

            @pl.when(kb > 0)
            def _():
                dq_ref[rows, :] += contrib

        @pl.when((kb > 0) & (kb < qb))
        def _():
            step(False)

        @pl.when((kb <= qb) & ((kb == qb) | (kb == 0)))
        def _():
            step(True)

        @pl.when(qb == nb - 1)
        def _():
            dk_ref[...] = dk_s[...]
            dv_ref[...] = dv_s[...]

    qs = pl.BlockSpec((t, HEAD_W), lambda h, j, i: (jnp.maximum(i, j), h))
    ks = pl.BlockSpec((t, HEAD_W), lambda h, j, i: (j, h))
    dqs = pl.BlockSpec((lp, HEAD_W), lambda h, j, i: (0, h))
    return pl.pallas_call(
        body, name=name, grid=(MLA_HEADS, nb, nb), in_specs=[qs, ks, ks, qs, qs, qs], out_specs=[dqs, ks, ks],
        out_shape=[jax.ShapeDtypeStruct((lp, QW), F32)] * 3,
        scratch_shapes=[pltpu.VMEM((t, HEAD_W), F32), pltpu.VMEM((t, HEAD_W), F32)],
        compiler_params=_params("arbitrary", "arbitrary", "arbitrary"),
    )(q, k, v, do, lse, delta)


HG_LEVELS = (64, 32, 16, 8, 4, 2)
N_LEV = len(HG_LEVELS)


def _hgrn_consts():
    c = HG_CHUNK
    m = np.zeros((N_LEV + 2, c, c), np.float32)
    masks = np.zeros((N_LEV, c, c), np.float32)
    for li, p in enumerate(HG_LEVELS):
        for t in range(c):
            mid = (t // p) * p + p // 2
            if t >= mid:
                m[li, t, mid:t + 1] = 1.0
            else:
                m[li, t, t + 1:mid] = 1.0
            for s in range(c):
                if s // p == t // p and t >= mid and s < mid:
                    masks[li, t, s] = 1.0
    for t in range(c):
        m[N_LEV, t, :t + 1] = 1.0
        m[N_LEV + 1, t, t + 1:] = 1.0
    mall = m.reshape((N_LEV + 2) * c, c)
    return jnp.asarray(mall, BF16), jnp.asarray(mall.T.copy(), BF16), jnp.asarray(masks, F32)


def _split3(x):
    hi = x.astype(BF16)
    r = x - hi.astype(F32)
    mid = r.astype(BF16)
    lo = (r - mid.astype(F32)).astype(BF16)
    return jnp.concatenate([hi, mid, lo], axis=1)


def _sum3(e3):
    return e3[:, :HG_D] + e3[:, HG_D:2 * HG_D] + e3[:, 2 * HG_D:]


def _hgrn_chunk_fwd(hq, hf, hi, lb, valid, mall, masks, st):
    c = HG_CHUNK
    scale = HG_D ** -0.5
    sq = _sigmoid(hq)
    qv = hq * sq
    sg = _sigmoid(hf)
    f = lb + (1.0 - lb) * sg
    fc = jnp.maximum(f, F_MIN)
    lf = jnp.log(fc) * valid
    kv = (1.0 - lb) * (1.0 - sg) * valid
    e = _sum3(jnp.dot(mall, _split3(lf), preferred_element_type=F32))
    x = jnp.exp(e)
    a = jnp.zeros((c, c), F32)
    qe, ke = [], []
    for l in range(N_LEV):
        xl = x[l * c:(l + 1) * c]
        qe.append(qv * xl)
        ke.append(kv * xl)
        a = a + masks[l] * lax.dot_general(qe[l].astype(BF16), ke[l].astype(BF16), _DIMS["nt"],
                                           preferred_element_type=F32)
    row = lax.broadcasted_iota(jnp.int32, (c, c), 0)
    col = lax.broadcasted_iota(jnp.int32, (c, c), 1)
    a = a + jnp.where(row == col, jnp.sum(qv * kv, axis=-1, keepdims=True), 0.0)
    xb = x[N_LEV * c:(N_LEV + 1) * c]
    qb = qv * xb
    kb = kv * x[(N_LEV + 1) * c:]
    x_last = xb[c - 1:c]
    hib = hi.astype(BF16)
    o = scale * (jnp.dot(a.astype(BF16), hib, preferred_element_type=F32)
                 + lax.dot_general(qb.astype(BF16), st.astype(BF16), _DIMS["nt"], preferred_element_type=F32))
    st_new = st * x_last + lax.dot_general(hib, kb.astype(BF16), _DIMS["tn"], preferred_element_type=F32)
    saved = dict(sq=sq, qv=qv, sg=sg, f=f, fc=fc, kv=kv, x=x, a=a, qe=qe, ke=ke, qb=qb, kb=kb, x_last=x_last)
    return o, st_new, saved


def _hgrn_fwd(z, lb, nw, consts, *, zl, lv, name):
    lp = z.shape[0]
    tb = _pick(lp, ROW_TILE, HG_CHUNK)
    ncb = tb // HG_CHUNK
    nb = lp // tb
    mall, _, masks = consts
    w = HG_HEADS * HG_D

    def body(hq_ref, hf_ref, hi_ref, hg_ref, lb_ref, nw_ref, mall_ref, masks_ref, o_ref, ob_ref, st_ref, st_s):
        i = pl.program_id(1)

        @pl.when(i == 0)
        def _():
            st_s[...] = jnp.zeros_like(st_s)

        lbv, nwv = lb_ref[...], nw_ref[...]
        mallv, masksv = mall_ref[...], masks_ref[...]

        def chunk(cix, carry):
            r0 = pl.multiple_of(cix * HG_CHUNK, HG_CHUNK)
            rows = pl.ds(r0, HG_CHUNK)
            valid = _row_valid(i * tb + r0, HG_CHUNK, lv)
            st = st_s[...]
            st_ref[0, cix] = st
            o, st_new, _ = _hgrn_chunk_fwd(hq_ref[rows, :], hf_ref[rows, :], hi_ref[rows, :], lbv, valid,
                                           mallv, masksv, st)
            st_s[...] = st_new
            o_ref[rows, :] = o
            hg = hg_ref[rows, :]
            r = lax.rsqrt(jnp.mean(o * o, axis=-1, keepdims=True) + EPS)
            ob_ref[rows, :] = (o * r * nwv * (hg * _sigmoid(hg))).astype(BF16)
            return carry

        lax.fori_loop(0, ncb, chunk, 0)

    zb = lambda off: pl.BlockSpec((tb, HG_D), lambda h, i: (i, off // HG_D + h))
    head = pl.BlockSpec((tb, HG_D), lambda h, i: (i, h))
    const = lambda shape: pl.BlockSpec(shape, lambda h, i: (0,) * len(shape))
    return pl.pallas_call(
        body, name=name, grid=(HG_HEADS, nb),
        in_specs=[zb(zl["hq"]), zb(zl["hf"]), zb(zl["hi"]), zb(zl["hg"]),
                  pl.BlockSpec((1, HG_D), lambda h, i: (0, h)), const((1, HG_D)),
                  const(mall.shape), const(masks.shape)],
        out_specs=[head, head, pl.BlockSpec((1, ncb, HG_D, HG_D), lambda h, i: (h, i, 0, 0))],
        out_shape=[jax.ShapeDtypeStruct((lp, w), F32), jax.ShapeDtypeStruct((lp, w), BF16),
                   jax.ShapeDtypeStruct((HG_HEADS, lp // HG_CHUNK, HG_D, HG_D), F32)],
        scratch_shapes=[pltpu.VMEM((HG_D, HG_D), F32)],
        compiler_params=_params("parallel", "arbitrary"),
    )(z, z, z, z, lb.reshape(1, w), nw.reshape(1, HG_D), mall, masks)


def _hgrn_bwd(z, o_pre, dob, states, lb, nw, consts, *, zl, lv, name):
    lp = z.shape[0]
    tb = _pick(lp, ROW_TILE, HG_CHUNK)
    ncb = tb // HG_CHUNK
    nb = lp // tb
    mall, mall_t, masks = consts
    w = HG_HEADS * HG_D
    c = HG_CHUNK
    scale = HG_D ** -0.5

    def body(hq_ref, hf_ref, hi_ref, hg_ref, o_ref, dob_ref, st_ref, lb_ref, nw_ref, mall_ref, mallt_ref, masks_ref,
             dhq_ref, dhf_ref, dhi_ref, dhg_ref, dlb_ref, dnw_ref, dst_s):
        h, i = pl.program_id(0), pl.program_id(1)
        blk = nb - 1 - i

        @pl.when(i == 0)
        def _():
            dst_s[...] = jnp.zeros_like(dst_s)
            dlb_ref[...] = jnp.zeros_like(dlb_ref)

        @pl.when((i == 0) & (h == 0))
        def _():
            dnw_ref[...] = jnp.zeros_like(dnw_ref)

        lbv, nwv = lb_ref[...], nw_ref[...]
        mallv, malltv, masksv = mall_ref[...], mallt_ref[...], masks_ref[...]

        def chunk(jx, carry):
            cix = ncb - 1 - jx
            r0 = pl.multiple_of(cix * c, c)
            rows = pl.ds(r0, c)
            valid = _row_valid(blk * tb + r0, c, lv)
            hq, hf, hi, hg = hq_ref[rows, :], hf_ref[rows, :], hi_ref[rows, :], hg_ref[rows, :]
            st = st_ref[0, cix]
            _, _, sv = _hgrn_chunk_fwd(hq, hf, hi, lbv, valid, mallv, masksv, st)
            o, dout = o_ref[rows, :], dob_ref[rows, :]
            shg = _sigmoid(hg)
            r = lax.rsqrt(jnp.mean(o * o, axis=-1, keepdims=True) + EPS)
            don = dout * (hg * shg)
            dhg_ref[rows, :] = dout * (o * r * nwv) * shg * (1.0 + hg * (1.0 - shg))
            dnw_ref[...] += jnp.sum(don * o * r, axis=0, keepdims=True)
            wd = don * nwv
            do = r * wd - o * (r * r * r) * jnp.mean(o * wd, axis=-1, keepdims=True)
            dob16, hib = do.astype(BF16), hi.astype(BF16)
            dst = dst_s[...]
            dst16 = dst.astype(BF16)
            da = scale * lax.dot_general(dob16, hib, _DIMS["nt"], preferred_element_type=F32)
            dv = (scale * lax.dot_general(sv["a"].astype(BF16), dob16, _DIMS["tn"], preferred_element_type=F32)
                  + lax.dot_general(sv["kb"].astype(BF16), dst16, _DIMS["nt"], preferred_element_type=F32))
            dkb = jnp.dot(hib, dst16, preferred_element_type=F32)
            dqb = scale * jnp.dot(dob16, st.astype(BF16), preferred_element_type=F32)
            dst_s[...] = dst * sv["x_last"] + scale * lax.dot_general(dob16, sv["qb"].astype(BF16), _DIMS["tn"],
                                                                      preferred_element_type=F32)
            dxl = jnp.sum(dst * st, axis=0, keepdims=True)
            x = sv["x"]
            dq = dqb * x[N_LEV * c:(N_LEV + 1) * c]
            dk = dkb * x[(N_LEV + 1) * c:]
            de = []
            for l in range(N_LEV):
                dam = (masksv[l] * da).astype(BF16)
                dqe = jnp.dot(dam, sv["ke"][l].astype(BF16), preferred_element_type=F32)
                dke = lax.dot_general(dam, sv["qe"][l].astype(BF16), _DIMS["tn"], preferred_element_type=F32)
                xl = x[l * c:(l + 1) * c]
                dq = dq + dqe * xl
                dk = dk + dke * xl
                de.append(dqe * sv["qe"][l] + dke * sv["ke"][l])
            dd = scale * jnp.sum(do * hi, axis=-1, keepdims=True)
            dq = dq + dd * sv["kv"]
            dk = dk + dd * sv["qv"]
            last = lax.broadcasted_iota(jnp.int32, (c, 1), 0) == c - 1
            de.append(dqb * sv["qb"] + jnp.where(last, dxl * sv["x_last"], 0.0))
            de.append(dkb * sv["kb"])
            dlf = _sum3(jnp.dot(malltv, _split3(jnp.concatenate(de, axis=0)), preferred_element_type=F32))
            sg = sv["sg"]
            df = jnp.where(sv["f"] > F_MIN, dlf * valid / sv["fc"], 0.0)
            dkm = dk * valid
            dsg = (df - dkm) * (1.0 - lbv)
            dhf_ref[rows, :] = dsg * sg * (1.0 - sg)
            dlb_ref[...] += jnp.sum((df - dkm) * (1.0 - sg), axis=0, keepdims=True)
            sq = sv["sq"]
            dhq_ref[rows, :] = dq * sq * (1.0 + hq * (1.0 - sq))
            dhi_ref[rows, :] = dv
            return carry

        lax.fori_loop(0, ncb, chunk, 0)

    zb = lambda off: pl.BlockSpec((tb, HG_D), lambda h, i: (nb - 1 - i, off // HG_D + h))
    head = pl.BlockSpec((tb, HG_D), lambda h, i: (nb - 1 - i, h))
    const = lambda shape: pl.BlockSpec(shape, lambda h, i: (0,) * len(shape))
    lbs = pl.BlockSpec((1, HG_D), lambda h, i: (0, h))
    outs = pl.pallas_call(
        body, name=name, grid=(HG_HEADS, nb),
        in_specs=[zb(zl["hq"]), zb(zl["hf"]), zb(zl["hi"]), zb(zl["hg"]), head, head,
                  pl.BlockSpec((1, ncb, HG_D, HG_D), lambda h, i: (h, nb - 1 - i, 0, 0)),
                  lbs, const((1, HG_D)), const(mall.shape), const(mall_t.shape), const(masks.shape)],
        out_specs=[head, head, head, head, lbs, const((1, HG_D))],
        out_shape=[jax.ShapeDtypeStruct((lp, w), F32)] * 4
                  + [jax.ShapeDtypeStruct((1, w), F32), jax.ShapeDtypeStruct((1, HG_D), F32)],
        scratch_shapes=[pltpu.VMEM((HG_D, HG_D), F32)],
        compiler_params=_params("arbitrary", "arbitrary"),
    )(z, z, z, z, o_pre, dob, states, lb.reshape(1, w), nw.reshape(1, HG_D), mall, mall_t, masks)
    dhq, dhf, dhi, dhg, dlb, dnw = outs
    return dhq, dhf, dhi, dhg, dlb[0], dnw[0]


def _loss_head(h, w, tpad, *, lv, name):
    lp, d = h.shape
    tm = _pick(lp, ROW_TILE, 8)

    def body(h_ref, w_ref, t_ref, dh_ref, loss_ref, dw_ref):
        i = pl.program_id(0)
        r0 = i * tm + lax.broadcasted_iota(jnp.int32, (tm, 1), 0)
        valid = ((r0 >= ROW_X) & (r0 < lv)).astype(F32)
        xv, wv = h_ref[...], w_ref[...]
        r = lax.rsqrt(jnp.mean(xv * xv, axis=-1, keepdims=True) + EPS)
        e = (xv * r * wv - t_ref[...]) * valid
        dy = e * (1.0 / d)
        wdy = dy * wv
        dh_ref[...] = r * wdy - xv * (r * r * r) * jnp.mean(xv * wdy, axis=-1, keepdims=True)

        @pl.when(i == 0)
        def _():
            loss_ref[...] = jnp.zeros_like(loss_ref)
            dw_ref[...] = jnp.zeros_like(dw_ref)

        loss_ref[...] += 0.5 * jnp.sum(jnp.mean(e * e, axis=-1, keepdims=True), axis=0, keepdims=True)
        dw_ref[...] += jnp.sum(dy * xv * r, axis=0, keepdims=True)

    row = pl.BlockSpec((tm, d), lambda i: (i, 0))
    vec = pl.BlockSpec((1, d), lambda i: (0, 0))
    dh, loss, dw = pl.pallas_call(
        body, name=name, grid=(lp // tm,), in_specs=[row, vec, row],
        out_specs=[row, pl.BlockSpec((8, LANES), lambda i: (0, 0)), vec],
        out_shape=[jax.ShapeDtypeStruct((lp, d), F32), jax.ShapeDtypeStruct((8, LANES), F32),
                   jax.ShapeDtypeStruct((1, d), F32)],
        compiler_params=_params("arbitrary"),
    )(h, w.reshape(1, d), tpad)
    return dh, loss[0, 0], dw[0]


def _adamw(w, g, m, v, *, name):
    shape = w.shape
    cols = shape[-1]
    rows = int(np.prod(shape[:-1])) if len(shape) > 1 else 1
    tr = _pick(rows, 256, 8)
    c1 = 1.0 - ADAM_B1 ** ADAM_STEP
    c2 = 1.0 - ADAM_B2 ** ADAM_STEP

    def body(w_ref, g_ref, m_ref, v_ref, d_ref, nm_ref, nv_ref):
        gv = g_ref[...]
        nm = ADAM_B1 * m_ref[...] + (1.0 - ADAM_B1) * gv
        nv = ADAM_B2 * v_ref[...] + (1.0 - ADAM_B2) * (gv * gv)
        d_ref[...] = -ADAM_LR * ((nm / c1) / (jnp.sqrt(nv / c2) + ADAM_EPS) + ADAM_WD * w_ref[...])
        nm_ref[...] = nm
        nv_ref[...] = nv

    spec = pl.BlockSpec((tr, cols), lambda i: (i, 0))
    r2 = lambda a: a.reshape(rows, cols)
    outs = pl.pallas_call(
        body, name=name, grid=(rows // tr,), in_specs=[spec] * 4, out_specs=[spec] * 3,
        out_shape=[jax.ShapeDtypeStruct((rows, cols), F32)] * 3, compiler_params=_params("parallel"),
    )(r2(w), r2(g), r2(m), r2(v))
    return tuple(o.reshape(shape) for o in outs)


HBM_SPEC = pl.BlockSpec(memory_space=pl.ANY)


def _coords():
    return lax.axis_index("x"), lax.axis_index("y"), lax.axis_index("c")


def _other_chips(x, y):
    return [(1 - x, y), (x, 1 - y), (1 - x, 1 - y)]


def _remote(src, dst, ssem, rsem, dev):
    return pltpu.make_async_remote_copy(src_ref=src, dst_ref=dst, send_sem=ssem, recv_sem=rsem,
                                        device_id=dev, device_id_type=MESH)


def _gather_chips(w, *, name):
    rows, cols = w.shape
    rh = rows // 2
    align = 8 * 4 // w.dtype.itemsize
    assert rh * 2 == rows and rh % align == 0

    def body(w_ref, out_ref, send_sems, recv_sems, local_sem):
        x, y, c = _coords()
        k = 2 * x + y
        sib = (x, y, 1 - c)
        half = pl.ds(pl.multiple_of(c * rh, align), rh)
        ohalf = pl.ds(pl.multiple_of((1 - c) * rh, align), rh)
        chips = _other_chips(x, y)
        mine = pltpu.make_async_copy(w_ref, out_ref.at[k], local_sem)
        mine.start()
        sent = []
        for j, (px, py) in enumerate(chips):
            cp = _remote(w_ref.at[half], out_ref.at[k, half], send_sems.at[j], recv_sems.at[j], (px, py, c))
            cp.start()
            sent.append(cp)
        for j, (px, py) in enumerate(chips):
            blk = out_ref.at[2 * px + py, half]
            _remote(w_ref.at[half], blk, send_sems.at[j], recv_sems.at[j], (px, py, c)).wait_recv()
            fw = _remote(blk, blk, send_sems.at[3 + j], recv_sems.at[3 + j], sib)
            fw.start()
            sent.append(fw)
        for j, (px, py) in enumerate(chips):
            blk = out_ref.at[2 * px + py, ohalf]
            _remote(blk, blk, send_sems.at[3 + j], recv_sems.at[3 + j], sib).wait_recv()
        for cp in sent:
            cp.wait_send()
        mine.wait()

    return pl.pallas_call(
        body, name=name, in_specs=[HBM_SPEC], out_specs=HBM_SPEC,
        out_shape=jax.ShapeDtypeStruct((4, rows, cols), w.dtype),
        scratch_shapes=[pltpu.SemaphoreType.DMA((6,)), pltpu.SemaphoreType.DMA((6,)), pltpu.SemaphoreType.DMA],
    )(w)


def _swap_halves(gp, *, name):
    n, rows, cols = gp.shape
    rh = rows // 2

    def body(g_ref, out_ref, send_sems, recv_sems):
        x, y, c = _coords()
        sib = (x, y, 1 - c)
        ohalf = pl.ds(pl.multiple_of((1 - c) * rh, 8), rh)
        cps = [_remote(g_ref.at[s, ohalf], out_ref.at[s], send_sems.at[s], recv_sems.at[s], sib) for s in range(n)]
        for cp in cps:
            cp.start()
        for cp in cps:
            cp.wait_recv()
        for cp in cps:
            cp.wait_send()

    return pl.pallas_call(
        body, name=name, in_specs=[HBM_SPEC], out_specs=HBM_SPEC,
        out_shape=jax.ShapeDtypeStruct((n, rh, cols), gp.dtype),
        scratch_shapes=[pltpu.SemaphoreType.DMA((n,)), pltpu.SemaphoreType.DMA((n,))],
    )(gp)


def _add_half(gp, got, cidx, *, name):
    n, rows, cols = gp.shape
    rh = rows // 2
    tr = _pick(rh, 512, 8)
    nrb = rh // tr

    def body(c_ref, a_ref, b_ref, o_ref):
        o_ref[...] = a_ref[...] + b_ref[...]

    grid_spec = pltpu.PrefetchScalarGridSpec(
        num_scalar_prefetch=1, grid=(n, nrb),
        in_specs=[pl.BlockSpec((1, tr, cols), lambda s, i, c_ref: (s, c_ref[0] * nrb + i, 0)),
                  pl.BlockSpec((1, tr, cols), lambda s, i, c_ref: (s, i, 0))],
        out_specs=pl.BlockSpec((1, tr, cols), lambda s, i, c_ref: (s, i, 0)))
    return pl.pallas_call(
        body, name=name, grid_spec=grid_spec, out_shape=jax.ShapeDtypeStruct((n, rh, cols), gp.dtype),
        compiler_params=_params("parallel", "parallel"),
    )(cidx, gp, got)


def _scatter_chips(p, *, name):
    _, rh, cols = p.shape

    def body(p_ref, out_ref, send_sems, recv_sems):
        x, y, c = _coords()
        cps = []
        for j, (px, py) in enumerate(_other_chips(x, y)):
            cps.append(_remote(p_ref.at[2 * px + py], out_ref.at[j], send_sems.at[j], recv_sems.at[j], (px, py, c)))
        for cp in cps:
            cp.start()
        for cp in cps:
            cp.wait_recv()
        for cp in cps:
            cp.wait_send()

    return pl.pallas_call(
        body, name=name, in_specs=[HBM_SPEC], out_specs=HBM_SPEC,
        out_shape=jax.ShapeDtypeStruct((3, rh, cols), p.dtype),
        scratch_shapes=[pltpu.SemaphoreType.DMA((3,)), pltpu.SemaphoreType.DMA((3,))],
    )(p)


def _sum_arrivals(p, land, kidx, *, name):
    _, rh, cols = p.shape
    tr = _pick(rh, 512, 8)

    def body(k_ref, a_ref, l_ref, o_ref):
        o_ref[...] = ((a_ref[0] + l_ref[0]) + l_ref[1]) + l_ref[2]

    grid_spec = pltpu.PrefetchScalarGridSpec(
        num_scalar_prefetch=1, grid=(rh // tr,),
        in_specs=[pl.BlockSpec((1, tr, cols), lambda i, k_ref: (k_ref[0], i, 0)),
                  pl.BlockSpec((3, tr, cols), lambda i, k_ref: (0, i, 0))],
        out_specs=pl.BlockSpec((tr, cols), lambda i, k_ref: (i, 0)))
    return pl.pallas_call(
        body, name=name, grid_spec=grid_spec, out_shape=jax.ShapeDtypeStruct((rh, cols), p.dtype),
        compiler_params=_params("parallel"),
    )(kidx, p, land)


def _join_halves(q, *, name):
    rh, cols = q.shape

    def body(q_ref, out_ref, send_sem, recv_sem, local_sem):
        x, y, c = _coords()
        half = pl.ds(pl.multiple_of(c * rh, 8), rh)
        ohalf = pl.ds(pl.multiple_of((1 - c) * rh, 8), rh)
        mine = pltpu.make_async_copy(q_ref, out_ref.at[half], local_sem)
        mine.start()
        cp = _remote(q_ref, out_ref.at[half], send_sem, recv_sem, (x, y, 1 - c))
        cp.start()
        _remote(q_ref, out_ref.at[ohalf], send_sem, recv_sem, (x, y, 1 - c)).wait_recv()
        cp.wait_send()
        mine.wait()

    return pl.pallas_call(
        body, name=name, in_specs=[HBM_SPEC], out_specs=HBM_SPEC,
        out_shape=jax.ShapeDtypeStruct((2 * rh, cols), q.dtype),
        scratch_shapes=[pltpu.SemaphoreType.DMA, pltpu.SemaphoreType.DMA, pltpu.SemaphoreType.DMA],
    )(q)


def _reduce_scatter(gp, cidx, kidx, *, tag):
    got = _swap_halves(gp, name=f"rs_swap_{tag}")
    p = _add_half(gp, got, cidx, name=f"rs_add_{tag}")
    land = _scatter_chips(p, name=f"rs_scatter_{tag}")
    q = _sum_arrivals(p, land, kidx, name=f"rs_sum_{tag}")
    return _join_halves(q, name=f"rs_join_{tag}")


def _allreduce_small(s, *, name):
    rows, cols = s.shape

    def body(s_ref, o_ref, buf, send_sems, recv_sems):
        x, y, c = _coords()
        me = 4 * x + 2 * y + c
        buf[me] = s_ref[...]
        cps = []
        for r in range(1, 8):
            peer = tuple((1 - v) if (r >> sh) & 1 else v for v, sh in ((x, 2), (y, 1), (c, 0)))
            cps.append(_remote(s_ref, buf.at[me], send_sems.at[r - 1], recv_sems.at[r - 1], peer))
        for cp in cps:
            cp.start()
        for cp in cps:
            cp.wait_recv()
        for cp in cps:
            cp.wait_send()
        acc = buf[0]
        for d in range(1, 8):
            acc = acc + buf[d]
        o_ref[...] = acc

    vm = pl.BlockSpec(memory_space=pltpu.VMEM)
    return pl.pallas_call(
        body, name=name, in_specs=[vm], out_specs=vm, out_shape=jax.ShapeDtypeStruct((rows, cols), F32),
        scratch_shapes=[pltpu.VMEM((8, rows, cols), F32), pltpu.SemaphoreType.DMA((7,)),
                        pltpu.SemaphoreType.DMA((7,))],
    )(s)


PACKED = ("ffn1_w_gu", "ffn1_w_down", "w_in", "w_uq", "w_ukv", "w_proj_attn", "w_proj_rec", "w_out",
          "ffn2_w_gu", "ffn2_w_down")
ROW_SHARDED = ("ffn1_w_down", "w_out", "ffn2_w_down")


def _pack_plan(shard_shapes):
    plan, off = {}, 0
    for n in PACKED:
        r, c = shard_shapes[n]
        assert (r * c) % PACK_W == 0
        plan[n] = (off, r * c // PACK_W, (r, c))
        off += r * c // PACK_W
    total = -(-off // 32) * 32
    return plan, total


def _pack(tensors, plan, total, dtype):
    parts = [tensors[n].astype(dtype).reshape(-1, PACK_W) for n in PACKED]
    used = sum(p.shape[0] for p in parts)
    if total > used:
        parts.append(jnp.zeros((total - used, PACK_W), dtype))
    return jnp.concatenate(parts, axis=0)


def _unpack_full(g4, plan):
    out = {}
    for n in PACKED:
        off, nr, (r, c) = plan[n]
        sh = g4[:, off:off + nr].reshape(4, r, c)
        out[n] = jnp.concatenate(list(sh), axis=0 if n in ROW_SHARDED else 1)
    return out


def _pack_grads(grads, plan, total):
    blocks = []
    for s in range(4):
        t = {}
        for n in PACKED:
            _, _, (r, c) = plan[n]
            t[n] = grads[n][s * r:(s + 1) * r] if n in ROW_SHARDED else grads[n][:, s * c:(s + 1) * c]
        blocks.append(_pack(t, plan, total, F32))
    return jnp.stack(blocks)


def _unpack_shard(p, plan):
    return {n: p[plan[n][0]:plan[n][0] + plan[n][1]].reshape(plan[n][2]) for n in PACKED}


def _swap_cols(w):
    hlf = w.shape[1] // 2
    return jnp.concatenate([-w[:, hlf:], w[:, :hlf]], axis=1)


def _unswap_cols(dw):
    hlf = dw.shape[1] // 2
    return jnp.concatenate([dw[:, hlf:], -dw[:, :hlf]], axis=1)


def _layer_weights(full, d):
    zl = _z_layout(d)
    f = full["ffn1_w_down"].shape[0]
    w_in = full["w_in"]
    o = 0
    cols = {}
    for nm, wd in (("cq", Q_LORA), ("ckv", KV_LORA), ("kpe", QK_ROPE), ("hq", 512), ("hf", 512), ("hi", 512),
                   ("hg", 512), ("ga", d), ("gb", d)):
        cols[nm] = w_in[:, o:o + wd]
        o += wd
    zc = lambda n: jnp.zeros((d, n), BF16)
    win_p = jnp.concatenate(
        [cols["cq"], zc(QK_NOPE), cols["kpe"], zc(32), cols["ckv"], zc(QK_NOPE), _swap_cols(cols["kpe"]), zc(32),
         zc(LANES), cols["ga"], cols["gb"], cols["hq"], cols["hf"], cols["hi"], cols["hg"]], axis=1)
    assert win_p.shape[1] == zl["total"]
    wq = full["w_uq"].reshape(Q_LORA, MLA_HEADS, QK_NOPE + QK_ROPE)
    nope, rope = wq[:, :, :QK_NOPE], wq[:, :, QK_NOPE:]
    z32 = jnp.zeros((Q_LORA, MLA_HEADS, 32), BF16)
    z64 = jnp.zeros((Q_LORA, MLA_HEADS, QK_NOPE), BF16)
    rope_sw = jnp.concatenate([-rope[:, :, 16:], rope[:, :, :16]], axis=2)
    wqa = jnp.concatenate([nope, rope, z32], axis=2).reshape(Q_LORA, QW)
    wqb = jnp.concatenate([z64, rope_sw, z32], axis=2).reshape(Q_LORA, QW)
    wpa = full["w_proj_attn"].reshape(MLA_HEADS, V_HEAD, d)
    wpa_p = jnp.concatenate([jnp.zeros_like(wpa), wpa], axis=1).reshape(QW, d)
    return dict(
        wg1=full["ffn1_w_gu"][:, :f], wu1=full["ffn1_w_gu"][:, f:], wd1=full["ffn1_w_down"],
        wg2=full["ffn2_w_gu"][:, :f], wu2=full["ffn2_w_gu"][:, f:], wd2=full["ffn2_w_down"],
        win=win_p, wq2=jnp.concatenate([wqa, wqb], axis=1), wqa=wqa, wqb=wqb, wkv=full["w_ukv"], wpa=wpa_p,
        wpr=full["w_proj_rec"], wout=full["w_out"])


def _natural_grads(g, d):
    zl = _z_layout(d)
    dwin = g["win"]
    kpe = dwin[:, Z_KPA + QK_NOPE:Z_KPA + QK_NOPE + QK_ROPE] + _unswap_cols(
        dwin[:, Z_KPB + QK_NOPE:Z_KPB + QK_NOPE + QK_ROPE])
    w_in = jnp.concatenate(
        [dwin[:, Z_Q:Z_Q + Q_LORA], dwin[:, Z_KV:Z_KV + KV_LORA], kpe, dwin[:, zl["hq"]:zl["hq"] + 2048],
         dwin[:, zl["ga"]:zl["ga"] + 2 * d]], axis=1)
    qa = g["wqa"].reshape(Q_LORA, MLA_HEADS, HEAD_W)
    qb = g["wqb"].reshape(Q_LORA, MLA_HEADS, HEAD_W)[:, :, QK_NOPE:QK_NOPE + QK_ROPE]
    rope = qa[:, :, QK_NOPE:QK_NOPE + QK_ROPE] + jnp.concatenate([qb[:, :, 16:], -qb[:, :, :16]], axis=2)
    w_uq = jnp.concatenate([qa[:, :, :QK_NOPE], rope], axis=2).reshape(Q_LORA, -1)
    wpa = g["wpa"].reshape(MLA_HEADS, 2 * V_HEAD, d)[:, V_HEAD:].reshape(MLA_HEADS * V_HEAD, d)
    return dict(
        ffn1_w_gu=jnp.concatenate([g["wg1"], g["wu1"]], axis=1), ffn1_w_down=g["wd1"],
        ffn2_w_gu=jnp.concatenate([g["wg2"], g["wu2"]], axis=1), ffn2_w_down=g["wd2"],
        w_in=w_in, w_uq=w_uq, w_ukv=g["wkv"], w_proj_attn=wpa, w_proj_rec=g["wpr"], w_out=g["wout"])


def _rope_tables(lp):
    pos = jnp.maximum(jnp.arange(lp) - FRONT, 0).astype(F32)
    half = QK_ROPE // 2
    inv = ROPE_THETA ** (-jnp.arange(half, dtype=F32) / half)
    ang = pos[:, None] * inv[None, :]
    cos, sin = jnp.cos(ang), jnp.sin(ang)
    cos_t = jnp.concatenate([jnp.ones((lp, QK_NOPE), F32), cos, cos, jnp.zeros((lp, 32), F32)], axis=1)
    sin_t = jnp.concatenate([jnp.zeros((lp, QK_NOPE), F32), sin, sin, jnp.zeros((lp, 32), F32)], axis=1)
    return cos_t, sin_t


def _lower_bounds(raw):
    p = jax.nn.softmax(raw.astype(F32), axis=0)
    return jnp.cumsum(p, axis=0) - p[0:1]


def _ffn_fwd(h, nw, wg, wu, wd, tag):
    a = _rmsnorm_fwd(h, nw, width=h.shape[1], col_block=0, name=f"norm_{tag}")
    g = _matmul([(a, wg)], "nn", name=f"gate_{tag}")
    u = _matmul([(a, wu)], "nn", name=f"up_{tag}")
    act = _swiglu_fwd(g, u, name=f"swiglu_{tag}")
    out = _matmul([(act, wd)], "nn", res=h, scale=0.5, name=f"down_{tag}")
    return out, dict(h=h, a=a, g=g, u=u, act=act)


def _ffn_bwd(dout, sv, nw, wg, wu, wd, lv, tag):
    dact = _matmul([(dout, wd)], "nt", scale=0.5, name=f"ddown_{tag}")
    dwd = _matmul([(sv["act"], dout)], "tn", scale=0.5, name=f"dwdown_{tag}")
    dg, du = _swiglu_bwd(dact, sv["g"], sv["u"], name=f"dswiglu_{tag}")
    dwg = _matmul([(sv["a"], dg)], "tn", name=f"dwgate_{tag}")
    dwu = _matmul([(sv["a"], du)], "tn", name=f"dwup_{tag}")
    da = _matmul([(dg, wg), (du, wu)], "nt", name=f"dnormed_{tag}")
    dh, dn = _rmsnorm_bwd(sv["h"], nw, da, width=da.shape[1], col_block=0, lv=lv, dres=dout, name=f"dnorm_{tag}")
    return dh, dn, dwg, dwu, dwd


def _layer_fwd(h0, lw, sm, lb, tabs, consts, lv, l):
    d = h0.shape[1]
    zl = _z_layout(d)
    cos_t, sin_t = tabs
    h1, s1 = _ffn_fwd(h0, sm["ffn1_norm"], lw["wg1"], lw["wu1"], lw["wd1"], f"ffn1_{l}")
    um = _rmsnorm_fwd(h1, sm["mix_norm"], width=d, col_block=0, name=f"norm_mix_{l}")
    z = _matmul([(um, lw["win"])], "nn", name=f"inproj_{l}")
    qn = _rmsnorm_fwd(z, sm["q_norm"], width=Q_LORA, col_block=Z_Q // Q_LORA, name=f"norm_q_{l}")
    kvn = _rmsnorm_fwd(z, sm["kv_norm"], width=KV_LORA, col_block=Z_KV // KV_LORA, name=f"norm_kv_{l}")
    q2 = _matmul([(qn, lw["wq2"])], "nn", name=f"uq_{l}")
    kv = _matmul([(kvn, lw["wkv"])], "nn", name=f"ukv_{l}")
    q, k, v = _qkv_prep_fwd(q2, kv, z, cos_t, sin_t, name=f"qkv_{l}")
    o, lse = _attn_fwd(q, k, v, lv=lv, name=f"attn_{l}")
    ya = _matmul([(o, lw["wpa"])], "nn", name=f"proj_attn_{l}")
    o_pre, ob, states = _hgrn_fwd(z, lb, sm["hg_norm"], consts, zl=zl, lv=lv, name=f"hgrn_{l}")
    yb = _matmul([(ob, lw["wpr"])], "nn", name=f"proj_rec_{l}")
    mg = _merge_fwd(ya, yb, z, zl=zl, name=f"merge_{l}")
    h2 = _matmul([(mg, lw["wout"])], "nn", res=h1, name=f"out_{l}")
    h3, s2 = _ffn_fwd(h2, sm["ffn2_norm"], lw["wg2"], lw["wu2"], lw["wd2"], f"ffn2_{l}")
    saved = dict(s1=s1, s2=s2, h1=h1, um=um, z=z, qn=qn, kvn=kvn, q=q, k=k, v=v, o=o, lse=lse, ya=ya, yb=yb,
                 o_pre=o_pre, ob=ob, states=states, mg=mg)
    return h3, saved


def _layer_bwd(dh3, sv, lw, sm, lb, tabs, consts, lv, l):
    d = dh3.shape[1]
    lp = dh3.shape[0]
    zl = _z_layout(d)
    cos_t, sin_t = tabs
    z = sv["z"]
    g = {}
    sg = {}
    dh2, sg["ffn2_norm"], g["wg2"], g["wu2"], g["wd2"] = _ffn_bwd(
        dh3, sv["s2"], sm["ffn2_norm"], lw["wg2"], lw["wu2"], lw["wd2"], lv, f"ffn2_{l}")
    dmg = _matmul([(dh2, lw["wout"])], "nt", name=f"dmerged_{l}")
    g["wout"] = _matmul([(sv["mg"], dh2)], "tn", name=f"dwout_{l}")
    dya, dyb, dga, dgb = _merge_bwd(dmg, sv["ya"], sv["yb"], z, zl=zl, name=f"dmerge_{l}")
    doa = _matmul([(dya, lw["wpa"])], "nt", name=f"dattn_out_{l}")
    g["wpa"] = _matmul([(sv["o"], dya)], "tn", name=f"dwproj_attn_{l}")
    dob = _matmul([(dyb, lw["wpr"])], "nt", name=f"drec_out_{l}")
    g["wpr"] = _matmul([(sv["ob"], dyb)], "tn", name=f"dwproj_rec_{l}")
    dhq, dhf, dhi, dhg, dlb, sg["hg_norm"] = _hgrn_bwd(
        z, sv["o_pre"], dob, sv["states"], lb, sm["hg_norm"], consts, zl=zl, lv=lv, name=f"dhgrn_{l}")
    delta = _attn_delta(doa, sv["o"], name=f"attn_delta_{l}")
    dq, dk, dv = _attn_bwd(sv["q"], sv["k"], sv["v"], doa, sv["lse"], delta, name=f"dattn_{l}")
    dqa, dqb, dkv, dza, dzb = _qkv_prep_bwd(dq, dk, dv, cos_t, sin_t, name=f"dqkv_{l}")
    dqn = _matmul([(dqa, lw["wqa"]), (dqb, lw["wqb"])], "nt", name=f"dqn_{l}")
    g["wqa"] = _matmul([(sv["qn"], dqa)], "tn", name=f"dwqa_{l}")
    g["wqb"] = _matmul([(sv["qn"], dqb)], "tn", name=f"dwqb_{l}")
    dkvn = _matmul([(dkv, lw["wkv"])], "nt", name=f"dkvn_{l}")
    g["wkv"] = _matmul([(sv["kvn"], dkv)], "tn", name=f"dwkv_{l}")
    dzq, sg["q_norm"] = _rmsnorm_bwd(z, sm["q_norm"], dqn, width=Q_LORA, col_block=Z_Q // Q_LORA, lv=lv,
                                     name=f"dnorm_q_{l}")
    dzkv, sg["kv_norm"] = _rmsnorm_bwd(z, sm["kv_norm"], dkvn, width=KV_LORA, col_block=Z_KV // KV_LORA, lv=lv,
                                       name=f"dnorm_kv_{l}")
    dz = jnp.concatenate([dzq, dza, dzkv, dzb, jnp.zeros((lp, LANES), F32), dga, dgb, dhq, dhf, dhi, dhg], axis=1)
    dum = _matmul([(dz, lw["win"])], "nt", name=f"dmixed_{l}")
    g["win"] = _matmul([(sv["um"], dz)], "tn", name=f"dwin_{l}")
    dh1, sg["mix_norm"] = _rmsnorm_bwd(sv["h1"], sm["mix_norm"], dum, width=d, col_block=0, lv=lv, dres=dh2,
                                       name=f"dnorm_mix_{l}")
    dh0, sg["ffn1_norm"], g["wg1"], g["wu1"], g["wd1"] = _ffn_bwd(
        dh1, sv["s1"], sm["ffn1_norm"], lw["wg1"], lw["wu1"], lw["wd1"], lv, f"ffn1_{l}")
    return dh0, g, sg, dlb


WEIGHTS = ("meta_tokens", "ffn1_norm", "ffn1_w_gu", "ffn1_w_down", "mix_norm", "w_in", "q_norm", "kv_norm", "w_uq",
           "w_ukv", "hg_lb_raw", "hg_norm", "w_proj_attn", "w_proj_rec", "w_out", "ffn2_norm", "ffn2_w_gu",
           "ffn2_w_down", "final_norm")
SMALL = ("ffn1_norm", "mix_norm", "q_norm", "kv_norm", "hg_lb_raw", "hg_norm", "ffn2_norm")


def _small_rows(vals):
    pad = lambda a: jnp.pad(a, ((0, 0), (0, PACK_W - a.shape[1])))
    rows = [pad(vals[n]) for n in SMALL]
    rows.append(pad(vals["final_norm"][None, :]))
    rows.append(pad(vals["meta_tokens"]))
    rows.append(pad(vals["loss"].reshape(1, 1)))
    s = jnp.concatenate(rows, axis=0)
    return jnp.pad(s, ((0, -s.shape[0] % 8), (0, 0)))


def _small_unrows(s, d, widths):
    out, o = {}, 0
    for n in SMALL:
        out[n] = s[o:o + DEPTH, :widths[n]]
        o += DEPTH
    out["final_norm"] = s[o, :d]
    o += 1
    out["meta_tokens"] = s[o:o + N_META, :d]
    o += N_META
    out["loss"] = s[o, 0]
    return out


def _step(args):
    x = args["x"][0]
    seq, d = x.shape
    assert d <= PACK_W
    lv = ROW_X + seq
    lp = -(-lv // ROW_TILE) * ROW_TILE
    xi, yi, ci = _coords()
    kidx = (2 * xi + yi).astype(jnp.int32).reshape(1)
    cidx = ci.astype(jnp.int32).reshape(1)
    consts = _hgrn_consts()
    tabs = _rope_tables(lp)

    shard_shapes = {n: args[n].shape[1:] for n in PACKED}
    plan, total = _pack_plan(shard_shapes)
    lws = []
    for l in range(DEPTH):
        packed = _pack({n: args[n][l] for n in PACKED}, plan, total, BF16)
        g4 = _gather_chips(packed, name=f"gather_{l}")
        lws.append(_layer_weights(_unpack_full(g4, plan), d))
    mt = args["meta_tokens"]
    mt4 = _gather_chips(mt, name="gather_meta")
    meta = jnp.concatenate(list(mt4), axis=1)

    sm = [{n: args[n][l] for n in SMALL} for l in range(DEPTH)]
    lbs = _lower_bounds(args["hg_lb_raw"])

    h = jnp.concatenate([jnp.zeros((FRONT, d), F32), meta, x, jnp.zeros((lp - lv, d), F32)], axis=0)
    saved = []
    for l in range(DEPTH):
        h, sv = _layer_fwd(h, lws[l], sm[l], lbs[l], tabs, consts, lv, l)
        saved.append(sv)
    tpad = jnp.pad(args["loss_target"][0], ((ROW_X, lp - lv), (0, 0)))
    dh, loss, dfinal = _loss_head(h, args["final_norm"], tpad, lv=lv, name="loss_head")

    small = {n: [None] * DEPTH for n in SMALL}
    dlbs = [None] * DEPTH
    shard_grads = [None] * DEPTH
    for l in reversed(range(DEPTH)):
        dh, g, sg, dlbs[l] = _layer_bwd(dh, saved[l], lws[l], sm[l], lbs[l], tabs, consts, lv, l)
        for n in sg:
            small[n][l] = sg[n]
        gp = _pack_grads(_natural_grads(g, d), plan, total)
        shard_grads[l] = _unpack_shard(_reduce_scatter(gp, cidx, kidx, tag=str(l)), plan)

    _, lb_vjp = jax.vjp(_lower_bounds, args["hg_lb_raw"])
    small_vals = {n: jnp.stack(small[n]) for n in SMALL if n != "hg_lb_raw"}
    small_vals["hg_lb_raw"] = lb_vjp(jnp.stack(dlbs))[0]
    small_vals["final_norm"] = dfinal
    small_vals["meta_tokens"] = dh[FRONT:ROW_X]
    small_vals["loss"] = loss
    widths = {n: args[n].shape[1] for n in SMALL}
    tot = _small_unrows(_allreduce_small(_small_rows(small_vals), name="allreduce_small"), d, widths)

    grads = {n: jnp.stack([shard_grads[l][n] for l in range(DEPTH)]) for n in PACKED}
    for n in SMALL:
        grads[n] = tot[n]
    grads["final_norm"] = tot["final_norm"]
    mcols = mt.shape[1]
    grads["meta_tokens"] = lax.dynamic_slice_in_dim(tot["meta_tokens"], (2 * xi + yi) * mcols, mcols, axis=1)
    grad_x = dh[ROW_X:lv][None]

    delta, new_m, new_v = {}, {}, {}
    for n in WEIGHTS:
        delta[n], new_m[n], new_v[n] = _adamw(args[n], grads[n], args["m_" + n], args["v_" + n], name=f"adamw_{n}")
    return (tot["loss"], grad_x, *[grads[n] for n in WEIGHTS], *[delta[n] for n in WEIGHTS],
            *[new_m[n] for n in WEIGHTS], *[new_v[n] for n in WEIGHTS])


def kernel(x, meta_tokens, ffn1_norm, ffn1_w_gu, ffn1_w_down, mix_norm, w_in, q_norm, kv_norm, w_uq, w_ukv, hg_lb_raw, hg_norm, w_proj_attn, w_proj_rec, w_out, ffn2_norm, ffn2_w_gu, ffn2_w_down, final_norm, loss_target, m_meta_tokens, m_ffn1_norm, m_ffn1_w_gu, m_ffn1_w_down, m_mix_norm, m_w_in, m_q_norm, m_kv_norm, m_w_uq, m_w_ukv, m_hg_lb_raw, m_hg_norm, m_w_proj_attn, m_w_proj_rec, m_w_out, m_ffn2_norm, m_ffn2_w_gu, m_ffn2_w_down, m_final_norm, v_meta_tokens, v_ffn1_norm, v_ffn1_w_gu, v_ffn1_w_down, v_mix_norm, v_w_in, v_q_norm, v_kv_norm, v_w_uq, v_w_ukv, v_hg_lb_raw, v_hg_norm, v_w_proj_attn, v_w_proj_rec, v_w_out, v_ffn2_norm, v_ffn2_w_gu, v_ffn2_w_down, v_final_norm):
    return _step(dict(locals()))
```

```python
import functools
import math

import numpy as np
import jax
import jax.numpy as jnp
from jax import lax
from jax.experimental import pallas as pl
from jax.experimental.pallas import tpu as pltpu

F32 = jnp.float32
BF16 = jnp.bfloat16

N_META = 16
MLA_HEADS = 8
Q_LORA = 384
KV_LORA = 256
QK_NOPE = 64
QK_ROPE = 32
V_HEAD = 64
ROPE_THETA = 10000.0
HG_HEADS = 4
HG_D = 128
HG_CHUNK = 64
EPS = 1e-6
NEG_BIG = -1e30
F_MIN = 1e-20
DEPTH = 4

ADAM_LR = 0.001
ADAM_B1 = 0.9
ADAM_B2 = 0.999
ADAM_EPS = 1e-08
ADAM_WD = 0.01
ADAM_STEP = 10

LANES = 128
FRONT = (-N_META) % HG_CHUNK
ROW_X = FRONT + N_META
ROW_TILE = 640
HEAD_W = 128
QW = MLA_HEADS * HEAD_W
VMEM_LIMIT = 48 * 1024 * 1024
PACK_W = 1024
MESH = pl.DeviceIdType.MESH

Z_Q, Z_KPA, Z_KV, Z_KPB, Z_PAD, Z_GA = 0, 384, 512, 768, 896, 1024


def _z_layout(d):
    ga = Z_GA
    gb = ga + d
    hq = gb + d
    hf = hq + 512
    hi = hf + 512
    hg = hi + 512
    return dict(ga=ga, gb=gb, hq=hq, hf=hf, hi=hi, hg=hg, total=hg + 512)


def _pick(dim, cap, mult=LANES):
    if dim <= cap:
        return dim
    best = None
    for t in range(mult, cap + 1, mult):
        if dim % t == 0:
            best = t
    assert best is not None, (dim, cap, mult)
    return best


def _params(*sem):
    return pltpu.CompilerParams(dimension_semantics=sem, vmem_limit_bytes=VMEM_LIMIT)


def _sigmoid(x):
    return 1.0 / (1.0 + jnp.exp(-x))


def _row_valid(row0, n, lv):
    r = row0 + lax.broadcasted_iota(jnp.int32, (n, 1), 0)
    return ((r >= FRONT) & (r < lv)).astype(F32)


_DIMS = {"nn": (((1,), (0,)), ((), ())), "nt": (((1,), (1,)), ((), ())), "tn": (((0,), (0,)), ((), ()))}


def _matmul(pairs, mode, *, name, out_dtype=F32, res=None, scale=1.0):
    a0, b0 = pairs[0]
    if mode == "nn":
        (m, k), n = a0.shape, b0.shape[1]
    elif mode == "nt":
        (m, k), n = a0.shape, b0.shape[0]
    else:
        (k, m), n = a0.shape, b0.shape[1]
    if mode == "tn":
        tm, tn, tk = _pick(m, 1024), _pick(n, 1408), _pick(k, ROW_TILE, 8)
    else:
        tm, tn, tk = _pick(m, ROW_TILE, 8), _pick(n, 1408), _pick(k, 1408)
    nk = k // tk
    npair = len(pairs)
    dims = _DIMS[mode]

    def body(*refs):
        ins = refs[:2 * npair]
        pos = 2 * npair
        res_ref = None
        if res is not None:
            res_ref = refs[pos]
            pos += 1
        o_ref, acc = refs[pos], refs[pos + 1]
        kk = pl.program_id(2)

        @pl.when(kk == 0)
        def _():
            acc[...] = jnp.zeros_like(acc)

        part = None
        for p in range(npair):
            a = ins[2 * p][...].astype(BF16)
            b = ins[2 * p + 1][...].astype(BF16)
            d = lax.dot_general(a, b, dims, preferred_element_type=F32)
            part = d if part is None else part + d
        acc[...] += part

        @pl.when(kk == nk - 1)
        def _():
            r = acc[...]
            if scale != 1.0:
                r = r * scale
            if res_ref is not None:
                r = r + res_ref[...]
            o_ref[...] = r.astype(out_dtype)

    if mode == "nn":
        a_spec = pl.BlockSpec((tm, tk), lambda i, j, q: (i, q))
        b_spec = pl.BlockSpec((tk, tn), lambda i, j, q: (q, j))
    elif mode == "nt":
        a_spec = pl.BlockSpec((tm, tk), lambda i, j, q: (i, q))
        b_spec = pl.BlockSpec((tn, tk), lambda i, j, q: (j, q))
    else:
        a_spec = pl.BlockSpec((tk, tm), lambda i, j, q: (q, i))
        b_spec = pl.BlockSpec((tk, tn), lambda i, j, q: (q, j))
    o_spec = pl.BlockSpec((tm, tn), lambda i, j, q: (i, j))
    in_specs, args = [], []
    for a, b in pairs:
        in_specs += [a_spec, b_spec]
        args += [a, b]
    if res is not None:
        in_specs.append(o_spec)
        args.append(res)
    return pl.pallas_call(
        body, name=name, grid=(m // tm, n // tn, nk), in_specs=in_specs, out_specs=o_spec,
        out_shape=jax.ShapeDtypeStruct((m, n), out_dtype),
        scratch_shapes=[pltpu.VMEM((tm, tn), F32)],
        compiler_params=_params("parallel", "parallel", "arbitrary"),
    )(*args)


def _rmsnorm_fwd(x, w, *, width, col_block, name, transposed=False):
    lp = x.shape[0]
    tm = _pick(lp, ROW_TILE, 8)

    def body(x_ref, w_ref, o_ref, *ot_ref):
        xv = x_ref[...]
        r = lax.rsqrt(jnp.mean(xv * xv, axis=-1, keepdims=True) + EPS)
        y = xv * r * w_ref[...]
        o_ref[...] = y.astype(BF16)
        if transposed:
            ot_ref[0][...] = y.T.astype(BF16)

    out_specs = [pl.BlockSpec((tm, width), lambda i: (i, 0))]
    out_shape = [jax.ShapeDtypeStruct((lp, width), BF16)]
    if transposed:
        out_specs.append(pl.BlockSpec((width, tm), lambda i: (0, i)))
        out_shape.append(jax.ShapeDtypeStruct((width, lp), BF16))
    outs = pl.pallas_call(
        body, name=name, grid=(lp // tm,),
        in_specs=[pl.BlockSpec((tm, width), lambda i: (i, col_block)), pl.BlockSpec((1, width), lambda i: (0, 0))],
        out_specs=out_specs, out_shape=out_shape, compiler_params=_params("parallel"),
    )(x, w.reshape(1, width))
    return tuple(outs) if transposed else outs[0]


def _rmsnorm_bwd(x, w, dy, *, width, col_block, lv, name, dres=None):
    lp = x.shape[0]
    tm = _pick(lp, ROW_TILE, 8)

    def body(*refs):
        if dres is None:
            x_ref, w_ref, dy_ref, dx_ref, dw_ref = refs
            dres_ref = None
        else:
            x_ref, w_ref, dy_ref, dres_ref, dx_ref, dw_ref = refs
        i = pl.program_id(0)
        xv = x_ref[...]
        dyv = dy_ref[...] * _row_valid(i * tm, tm, lv)
        r = lax.rsqrt(jnp.mean(xv * xv, axis=-1, keepdims=True) + EPS)
        wdy = dyv * w_ref[...]
        dx = r * wdy - xv * (r * r * r) * jnp.mean(xv * wdy, axis=-1, keepdims=True)
        if dres_ref is not None:
            dx = dx + dres_ref[...]
        dx_ref[...] = dx

        @pl.when(i == 0)
        def _():
            dw_ref[...] = jnp.zeros_like(dw_ref)

        dw_ref[...] += jnp.sum(dyv * xv * r, axis=0, keepdims=True)

    row = pl.BlockSpec((tm, width), lambda i: (i, 0))
    in_specs = [pl.BlockSpec((tm, width), lambda i: (i, col_block)), pl.BlockSpec((1, width), lambda i: (0, 0)), row]
    args = [x, w.reshape(1, width), dy]
    if dres is not None:
        in_specs.append(row)
        args.append(dres)
    dx, dw = pl.pallas_call(
        body, name=name, grid=(lp // tm,), in_specs=in_specs,
        out_specs=[row, pl.BlockSpec((1, width), lambda i: (0, 0))],
        out_shape=[jax.ShapeDtypeStruct((lp, width), F32), jax.ShapeDtypeStruct((1, width), F32)],
        compiler_params=_params("arbitrary"),
    )(*args)
    return dx, dw[0]


def _swiglu_fwd(g, u, *, name):
    lp, f = g.shape
    tm, tf = _pick(lp, ROW_TILE, 8), _pick(f, 1408)

    def body(g_ref, u_ref, o_ref, ot_ref):
        gv = g_ref[...]
        act = gv * _sigmoid(gv) * u_ref[...]
        o_ref[...] = act.astype(BF16)
        ot_ref[...] = act.T.astype(BF16)

    spec = pl.BlockSpec((tm, tf), lambda i, j: (i, j))
    return pl.pallas_call(
        body, name=name, grid=(lp // tm, f // tf), in_specs=[spec, spec],
        out_specs=[spec, pl.BlockSpec((tf, tm), lambda i, j: (j, i))],
        out_shape=[jax.ShapeDtypeStruct((lp, f), BF16), jax.ShapeDtypeStruct((f, lp), BF16)],
        compiler_params=_params("parallel", "parallel"),
    )(g, u)


def _swiglu_bwd(dact, g, u, *, name):
    lp, f = g.shape
    tm, tf = _pick(lp, ROW_TILE, 8), _pick(f, 1408)

    def body(d_ref, g_ref, u_ref, dg_ref, du_ref):
        gv, dv = g_ref[...], d_ref[...]
        s = _sigmoid(gv)
        dg_ref[...] = (dv * u_ref[...] * s * (1.0 + gv * (1.0 - s))).astype(BF16)
        du_ref[...] = (dv * gv * s).astype(BF16)

    spec = pl.BlockSpec((tm, tf), lambda i, j: (i, j))
    return pl.pallas_call(
        body, name=name, grid=(lp // tm, f // tf), in_specs=[spec, spec, spec], out_specs=[spec, spec],
        out_shape=[jax.ShapeDtypeStruct((lp, f), BF16)] * 2, compiler_params=_params("parallel", "parallel"),
    )(dact, g, u)


def _merge_fwd(ya, yb, z, *, zl, name):
    lp, d = ya.shape
    tm, td = _pick(lp, ROW_TILE, 8), _pick(d, 512)
    oa, ob = zl["ga"] // td, zl["gb"] // td

    def body(ya_ref, yb_ref, ga_ref, gb_ref, o_ref, ot_ref):
        mg = _sigmoid(ga_ref[...]) * ya_ref[...] + _sigmoid(gb_ref[...]) * yb_ref[...]
        o_ref[...] = mg.astype(BF16)
        ot_ref[...] = mg.T.astype(BF16)

    spec = pl.BlockSpec((tm, td), lambda i, j: (i, j))
    return pl.pallas_call(
        body, name=name, grid=(lp // tm, d // td),
        in_specs=[spec, spec, pl.BlockSpec((tm, td), lambda i, j: (i, oa + j)),
                  pl.BlockSpec((tm, td), lambda i, j: (i, ob + j))],
        out_specs=[spec, pl.BlockSpec((td, tm), lambda i, j: (j, i))],
        out_shape=[jax.ShapeDtypeStruct((lp, d), BF16), jax.ShapeDtypeStruct((d, lp), BF16)],
        compiler_params=_params("parallel", "parallel"),
    )(ya, yb, z, z)


def _merge_bwd(dmg, ya, yb, z, *, zl, name):
    lp, d = ya.shape
    tm, td = _pick(lp, ROW_TILE, 8), _pick(d, 512)
    oa, ob = zl["ga"] // td, zl["gb"] // td

    def body(d_ref, ya_ref, yb_ref, ga_ref, gb_ref, dya_ref, dyb_ref, dga_ref, dgb_ref):
        dv = d_ref[...]
        sa, sb = _sigmoid(ga_ref[...]), _sigmoid(gb_ref[...])
        dya_ref[...] = (dv * sa).astype(BF16)
        dyb_ref[...] = (dv * sb).astype(BF16)
        dga_ref[...] = dv * ya_ref[...] * sa * (1.0 - sa)
        dgb_ref[...] = dv * yb_ref[...] * sb * (1.0 - sb)

    spec = pl.BlockSpec((tm, td), lambda i, j: (i, j))
    return pl.pallas_call(
        body, name=name, grid=(lp // tm, d // td),
        in_specs=[spec, spec, spec, pl.BlockSpec((tm, td), lambda i, j: (i, oa + j)),
                  pl.BlockSpec((tm, td), lambda i, j: (i, ob + j))],
        out_specs=[spec] * 4,
        out_shape=[jax.ShapeDtypeStruct((lp, d), BF16)] * 2 + [jax.ShapeDtypeStruct((lp, d), F32)] * 2,
        compiler_params=_params("parallel", "parallel"),
    )(dmg, ya, yb, z, z)


def _qkv_prep_fwd(q2, kv, z, cos_t, sin_t, *, name):
    lp = q2.shape[0]
    tm = _pick(lp, ROW_TILE, 8)
    h = MLA_HEADS

    def body(qa_ref, qb_ref, kv_ref, za_ref, zb_ref, c_ref, s_ref, q_ref, k_ref, v_ref):
        c, s = c_ref[...], s_ref[...]
        lane = lax.broadcasted_iota(jnp.int32, (tm, HEAD_W), 1)
        q_ref[...] = (qa_ref[...] * c + qb_ref[...] * s).astype(BF16)
        kr = jnp.where(lane >= QK_NOPE, za_ref[...] * c + zb_ref[...] * s, 0.0)
        kvv = kv_ref[...]
        k_ref[...] = (jnp.where(lane < QK_NOPE, kvv, 0.0) + kr).astype(BF16)
        v_ref[...] = jnp.where(lane >= QK_NOPE, kvv, 0.0).astype(BF16)

    blk = lambda f: pl.BlockSpec((tm, HEAD_W), f)
    out = blk(lambda i, j: (i, j))
    return pl.pallas_call(
        body, name=name, grid=(lp // tm, h),
        in_specs=[blk(lambda i, j: (i, j)), blk(lambda i, j: (i, h + j)), blk(lambda i, j: (i, j)),
                  blk(lambda i, j: (i, Z_KPA // HEAD_W)), blk(lambda i, j: (i, Z_KPB // HEAD_W)),
                  blk(lambda i, j: (i, 0)), blk(lambda i, j: (i, 0))],
        out_specs=[out, out, out], out_shape=[jax.ShapeDtypeStruct((lp, QW), BF16)] * 3,
        compiler_params=_params("parallel", "parallel"),
    )(q2, q2, kv, z, z, cos_t, sin_t)


def _qkv_prep_bwd(dq, dk, dv, cos_t, sin_t, *, name):
    lp = dq.shape[0]
    tm = _pick(lp, ROW_TILE, 8)
    h = MLA_HEADS

    def body(dq_ref, dk_ref, dv_ref, c_ref, s_ref, dqa_ref, dqb_ref, dkv_ref, dza_ref, dzb_ref):
        j = pl.program_id(1)
        c, s = c_ref[...], s_ref[...]
        lane = lax.broadcasted_iota(jnp.int32, (tm, HEAD_W), 1)
        dqv, dkv_ = dq_ref[...], dk_ref[...]
        dqa_ref[...] = (dqv * c).astype(BF16)
        dqb_ref[...] = (dqv * s).astype(BF16)
        dkv_ref[...] = jnp.where(lane < QK_NOPE, dkv_, dv_ref[...]).astype(BF16)
        dkr = jnp.where(lane >= QK_NOPE, dkv_, 0.0)

        @pl.when(j == 0)
        def _():
            dza_ref[...] = jnp.zeros_like(dza_ref)
            dzb_ref[...] = jnp.zeros_like(dzb_ref)

        dza_ref[...] += dkr * c
        dzb_ref[...] += dkr * s

    blk = lambda f: pl.BlockSpec((tm, HEAD_W), f)
    per_head, shared = blk(lambda i, j: (i, j)), blk(lambda i, j: (i, 0))
    return pl.pallas_call(
        body, name=name, grid=(lp // tm, h),
        in_specs=[per_head, per_head, per_head, shared, shared],
        out_specs=[per_head, per_head, per_head, shared, shared],
        out_shape=[jax.ShapeDtypeStruct((lp, QW), BF16)] * 3 + [jax.ShapeDtypeStruct((lp, HEAD_W), F32)] * 2,
        compiler_params=_params("parallel", "arbitrary"),
    )(dq, dk, dv, cos_t, sin_t)


def _attn_tile(lp):
    return _pick(lp, ROW_TILE, LANES)


def _attn_mask(qb, kb, t):
    qpos = qb * t + lax.broadcasted_iota(jnp.int32, (t, t), 0)
    kpos = kb * t + lax.broadcasted_iota(jnp.int32, (t, t), 1)
    return (kpos <= qpos) & (kpos >= FRONT)


def _attn_fwd(q, k, v, *, lv, name):
    lp = q.shape[0]
    t = _attn_tile(lp)
    nb = lp // t
    scale = (QK_NOPE + QK_ROPE) ** -0.5

    def body(q_ref, k_ref, v_ref, o_ref, lse_ref, m_s, l_s, acc_s):
        qb, kb = pl.program_id(1), pl.program_id(2)

        @pl.when(kb == 0)
        def _():
            m_s[...] = jnp.full_like(m_s, NEG_BIG)
            l_s[...] = jnp.zeros_like(l_s)
            acc_s[...] = jnp.zeros_like(acc_s)

        def step(masked):
            s = lax.dot_general(q_ref[...], k_ref[...], _DIMS["nt"], preferred_element_type=F32) * scale
            if masked:
                s = jnp.where(_attn_mask(qb, kb, t), s, NEG_BIG)
            m_prev = m_s[...]
            m_new = jnp.maximum(m_prev, jnp.max(s, axis=-1, keepdims=True))
            alpha = jnp.exp(m_prev - m_new)
            p = jnp.exp(s - m_new)
            l_s[...] = alpha * l_s[...] + jnp.sum(p, axis=-1, keepdims=True)
            acc_s[...] = alpha * acc_s[...] + jnp.dot(p.astype(BF16), v_ref[...], preferred_element_type=F32)
            m_s[...] = m_new

        @pl.when((kb > 0) & (kb < qb))
        def _():
            step(False)

        @pl.when((kb <= qb) & ((kb == qb) | (kb == 0)))
        def _():
            step(True)

        @pl.when(kb == nb - 1)
        def _():
            l = l_s[...]
            o_ref[...] = acc_s[...] / l * _row_valid(qb * t, t, lv)
            lse_ref[...] = jnp.broadcast_to(m_s[...] + jnp.log(l), (t, HEAD_W))

    qs = pl.BlockSpec((t, HEAD_W), lambda h, i, j: (i, h))
    ks = pl.BlockSpec((t, HEAD_W), lambda h, i, j: (jnp.minimum(j, i), h))
    return pl.pallas_call(
        body, name=name, grid=(MLA_HEADS, nb, nb), in_specs=[qs, ks, ks], out_specs=[qs, qs],
        out_shape=[jax.ShapeDtypeStruct((lp, QW), F32)] * 2,
        scratch_shapes=[pltpu.VMEM((t, 1), F32), pltpu.VMEM((t, 1), F32), pltpu.VMEM((t, HEAD_W), F32)],
        compiler_params=_params("parallel", "parallel", "arbitrary"),
    )(q, k, v)


def _attn_delta(do, o, *, name):
    lp = do.shape[0]
    tm = _pick(lp, ROW_TILE, 8)

    def body(do_ref, o_ref, d_ref):
        d_ref[...] = jnp.broadcast_to(jnp.sum(do_ref[...] * o_ref[...], axis=-1, keepdims=True), (tm, HEAD_W))

    spec = pl.BlockSpec((tm, HEAD_W), lambda i, j: (i, j))
    return pl.pallas_call(
        body, name=name, grid=(lp // tm, MLA_HEADS), in_specs=[spec, spec], out_specs=spec,
        out_shape=jax.ShapeDtypeStruct((lp, QW), F32), compiler_params=_params("parallel", "parallel"),
    )(do, o)


def _attn_bwd(q, k, v, do, lse, delta, *, name):
    lp = q.shape[0]
    t = _attn_tile(lp)
    nb = lp // t
    scale = (QK_NOPE + QK_ROPE) ** -0.5

    def body(q_ref, k_ref, v_ref, do_ref, lse_ref, dl_ref, dq_ref, dk_ref, dv_ref, dk_s, dv_s):
        kb, qb = pl.program_id(1), pl.program_id(2)

        @pl.when(qb == 0)
        def _():
            dk_s[...] = jnp.zeros_like(dk_s)
            dv_s[...] = jnp.zeros_like(dv_s)

        def step(masked):
            qv, kv_, vv = q_ref[...], k_ref[...], v_ref[...]
            dov = do_ref[...].astype(BF16)
            s = lax.dot_general(qv, kv_, _DIMS["nt"], preferred_element_type=F32) * scale
            p = jnp.exp(s - lse_ref[:, :1])
            if masked:
                p = jnp.where(_attn_mask(qb, kb, t), p, 0.0)
            dv_s[...] += lax.dot_general(p.astype(BF16), dov, _DIMS["tn"], preferred_element_type=F32)
            dp = lax.dot_general(dov, vv, _DIMS["nt"], preferred_element_type=F32)
            ds = (p * (dp - dl_ref[:, :1]) * scale).astype(BF16)
            dk_s[...] += lax.dot_general(ds, qv, _DIMS["tn"], preferred_element_type=F32)
            contrib = jnp.dot(ds, kv_, preferred_element_type=F32)
            rows = pl.ds(pl.multiple_of(qb * t, t), t)

            @pl.when(kb == 0)
            def _():
                dq_ref[rows, :] = contrib

            @pl.when(kb > 0)
            def _():
                dq_ref[rows, :] += contrib

        @pl.when((kb > 0) & (kb < qb))
        def _():
            step(False)

        @pl.when((kb <= qb) & ((kb == qb) | (kb == 0)))
        def _():
            step(True)

        @pl.when(qb == nb - 1)
        def _():
            dk_ref[...] = dk_s[...]
            dv_ref[...] = dv_s[...]

    qs = pl.BlockSpec((t, HEAD_W), lambda h, j, i: (jnp.maximum(i, j), h))
    ks = pl.BlockSpec((t, HEAD_W), lambda h, j, i: (j, h))
    dqs = pl.BlockSpec((lp, HEAD_W), lambda h, j, i: (0, h))
    return pl.pallas_call(
        body, name=name, grid=(MLA_HEADS, nb, nb), in_specs=[qs, ks, ks, qs, qs, qs], out_specs=[dqs, ks, ks],
        out_shape=[jax.ShapeDtypeStruct((lp, QW), F32)] * 3,
        scratch_shapes=[pltpu.VMEM((t, HEAD_W), F32), pltpu.VMEM((t, HEAD_W), F32)],
        compiler_params=_params("arbitrary", "arbitrary", "arbitrary"),
    )(q, k, v, do, lse, delta)


HG_LEVELS = (64, 32, 16, 8, 4, 2)
N_LEV = len(HG_LEVELS)


def _hgrn_consts():
    c = HG_CHUNK
    m = np.zeros((N_LEV + 2, c, c), np.float32)
    masks = np.zeros((N_LEV, c, c), np.float32)
    for li, p in enumerate(HG_LEVELS):
        for t in range(c):
            mid = (t // p) * p + p // 2
            if t >= mid:
                m[li, t, mid:t + 1] = 1.0
            else:
                m[li, t, t + 1:mid] = 1.0
            for s in range(c):
                if s // p == t // p and t >= mid and s < mid:
                    masks[li, t, s] = 1.0
    for t in range(c):
        m[N_LEV, t, :t + 1] = 1.0
        m[N_LEV + 1, t, t + 1:] = 1.0
    mall = m.reshape((N_LEV + 2) * c, c)
    return jnp.asarray(mall, BF16), jnp.asarray(mall.T.copy(), BF16), jnp.asarray(masks, F32)


def _split3(x):
    hi = x.astype(BF16)
    r = x - hi.astype(F32)
    mid = r.astype(BF16)
    lo = (r - mid.astype(F32)).astype(BF16)
    return jnp.concatenate([hi, mid, lo], axis=1)


def _sum3(e3):
    return e3[:, :HG_D] + e3[:, HG_D:2 * HG_D] + e3[:, 2 * HG_D:]


def _hgrn_chunk_fwd(hq, hf, hi, lb, valid, mall, masks, st):
    c = HG_CHUNK
    scale = HG_D ** -0.5
    sq = _sigmoid(hq)
    qv = hq * sq
    sg = _sigmoid(hf)
    f = lb + (1.0 - lb) * sg
    fc = jnp.maximum(f, F_MIN)
    lf = jnp.log(fc) * valid
    kv = (1.0 - lb) * (1.0 - sg) * valid
    e = _sum3(jnp.dot(mall, _split3(lf), preferred_element_type=F32))
    x = jnp.exp(e)
    a = jnp.zeros((c, c), F32)
    qe, ke = [], []
    for l in range(N_LEV):
        xl = x[l * c:(l + 1) * c]
        qe.append(qv * xl)
        ke.append(kv * xl)
        a = a + masks[l] * lax.dot_general(qe[l].astype(BF16), ke[l].astype(BF16), _DIMS["nt"],
                                           preferred_element_type=F32)
    row = lax.broadcasted_iota(jnp.int32, (c, c), 0)
    col = lax.broadcasted_iota(jnp.int32, (c, c), 1)
    a = a + jnp.where(row == col, jnp.sum(qv * kv, axis=-1, keepdims=True), 0.0)
    xb = x[N_LEV * c:(N_LEV + 1) * c]
    qb = qv * xb
    kb = kv * x[(N_LEV + 1) * c:]
    x_last = xb[c - 1:c]
    hib = hi.astype(BF16)
    o = scale * (jnp.dot(a.astype(BF16), hib, preferred_element_type=F32)
                 + lax.dot_general(qb.astype(BF16), st.astype(BF16), _DIMS["nt"], preferred_element_type=F32))
    st_new = st * x_last + lax.dot_general(hib, kb.astype(BF16), _DIMS["tn"], preferred_element_type=F32)
    saved = dict(sq=sq, qv=qv, sg=sg, f=f, fc=fc, kv=kv, x=x, a=a, qe=qe, ke=ke, qb=qb, kb=kb, x_last=x_last)
    return o, st_new, saved


def _hgrn_fwd(z, lb, nw, consts, *, zl, lv, name):
    lp = z.shape[0]
    tb = _pick(lp, ROW_TILE, HG_CHUNK)
    ncb = tb // HG_CHUNK
    nb = lp // tb
    mall, _, masks = consts
    w = HG_HEADS * HG_D

    def body(hq_ref, hf_ref, hi_ref, hg_ref, lb_ref, nw_ref, mall_ref, masks_ref, o_ref, ob_ref, st_ref, st_s):
        i = pl.program_id(1)

        @pl.when(i == 0)
        def _():
            st_s[...] = jnp.zeros_like(st_s)

        lbv, nwv = lb_ref[...], nw_ref[...]
        mallv, masksv = mall_ref[...], masks_ref[...]

        def chunk(cix, carry):
            r0 = pl.multiple_of(cix * HG_CHUNK, HG_CHUNK)
            rows = pl.ds(r0, HG_CHUNK)
            valid = _row_valid(i * tb + r0, HG_CHUNK, lv)
            st = st_s[...]
            st_ref[0, cix] = st
            o, st_new, _ = _hgrn_chunk_fwd(hq_ref[rows, :], hf_ref[rows, :], hi_ref[rows, :], lbv, valid,
                                           mallv, masksv, st)
            st_s[...] = st_new
            o_ref[rows, :] = o
            hg = hg_ref[rows, :]
            r = lax.rsqrt(jnp.mean(o * o, axis=-1, keepdims=True) + EPS)
            ob_ref[rows, :] = (o * r * nwv * (hg * _sigmoid(hg))).astype(BF16)
            return carry

        lax.fori_loop(0, ncb, chunk, 0)

    zb = lambda off: pl.BlockSpec((tb, HG_D), lambda h, i: (i, off // HG_D + h))
    head = pl.BlockSpec((tb, HG_D), lambda h, i: (i, h))
    const = lambda shape: pl.BlockSpec(shape, lambda h, i: (0,) * len(shape))
    return pl.pallas_call(
        body, name=name, grid=(HG_HEADS, nb),
        in_specs=[zb(zl["hq"]), zb(zl["hf"]), zb(zl["hi"]), zb(zl["hg"]),
                  pl.BlockSpec((1, HG_D), lambda h, i: (0, h)), const((1, HG_D)),
                  const(mall.shape), const(masks.shape)],
        out_specs=[head, head, pl.BlockSpec((1, ncb, HG_D, HG_D), lambda h, i: (h, i, 0, 0))],
        out_shape=[jax.ShapeDtypeStruct((lp, w), F32), jax.ShapeDtypeStruct((lp, w), BF16),
                   jax.ShapeDtypeStruct((HG_HEADS, lp // HG_CHUNK, HG_D, HG_D), F32)],
        scratch_shapes=[pltpu.VMEM((HG_D, HG_D), F32)],
        compiler_params=_params("parallel", "arbitrary"),
    )(z, z, z, z, lb.reshape(1, w), nw.reshape(1, HG_D), mall, masks)


def _hgrn_bwd(z, o_pre, dob, states, lb, nw, consts, *, zl, lv, name):
    lp = z.shape[0]
    tb = _pick(lp, ROW_TILE, HG_CHUNK)
    ncb = tb // HG_CHUNK
    nb = lp // tb
    mall, mall_t, masks = consts
    w = HG_HEADS * HG_D
    c = HG_CHUNK
    scale = HG_D ** -0.5

    def body(hq_ref, hf_ref, hi_ref, hg_ref, o_ref, dob_ref, st_ref, lb_ref, nw_ref, mall_ref, mallt_ref, masks_ref,
             dhq_ref, dhf_ref, dhi_ref, dhg_ref, dlb_ref, dnw_ref, dst_s):
        h, i = pl.program_id(0), pl.program_id(1)
        blk = nb - 1 - i

        @pl.when(i == 0)
        def _():
            dst_s[...] = jnp.zeros_like(dst_s)
            dlb_ref[...] = jnp.zeros_like(dlb_ref)

        @pl.when((i == 0) & (h == 0))
        def _():
            dnw_ref[...] = jnp.zeros_like(dnw_ref)

        lbv, nwv = lb_ref[...], nw_ref[...]
        mallv, malltv, masksv = mall_ref[...], mallt_ref[...], masks_ref[...]

        def chunk(jx, carry):
            cix = ncb - 1 - jx
            r0 = pl.multiple_of(cix * c, c)
            rows = pl.ds(r0, c)
            valid = _row_valid(blk * tb + r0, c, lv)
            hq, hf, hi, hg = hq_ref[rows, :], hf_ref[rows, :], hi_ref[rows, :], hg_ref[rows, :]
            st = st_ref[0, cix]
            _, _, sv = _hgrn_chunk_fwd(hq, hf, hi, lbv, valid, mallv, masksv, st)
            o, dout = o_ref[rows, :], dob_ref[rows, :]
            shg = _sigmoid(hg)
            r = lax.rsqrt(jnp.mean(o * o, axis=-1, keepdims=True) + EPS)
            don = dout * (hg * shg)
            dhg_ref[rows, :] = dout * (o * r * nwv) * shg * (1.0 + hg * (1.0 - shg))
            dnw_ref[...] += jnp.sum(don * o * r, axis=0, keepdims=True)
            wd = don * nwv
            do = r * wd - o * (r * r * r) * jnp.mean(o * wd, axis=-1, keepdims=True)
            dob16, hib = do.astype(BF16), hi.astype(BF16)
            dst = dst_s[...]
            dst16 = dst.astype(BF16)
            da = scale * lax.dot_general(dob16, hib, _DIMS["nt"], preferred_element_type=F32)
            dv = (scale * lax.dot_general(sv["a"].astype(BF16), dob16, _DIMS["tn"], preferred_element_type=F32)
                  + lax.dot_general(sv["kb"].astype(BF16), dst16, _DIMS["nt"], preferred_element_type=F32))
            dkb = jnp.dot(hib, dst16, preferred_element_type=F32)
            dqb = scale * jnp.dot(dob16, st.astype(BF16), preferred_element_type=F32)
            dst_s[...] = dst * sv["x_last"] + scale * lax.dot_general(dob16, sv["qb"].astype(BF16), _DIMS["tn"],
                                                                      preferred_element_type=F32)
            dxl = jnp.sum(dst * st, axis=0, keepdims=True)
            x = sv["x"]
            dq = dqb * x[N_LEV * c:(N_LEV + 1) * c]
            dk = dkb * x[(N_LEV + 1) * c:]
            de = []
            for l in range(N_LEV):
                dam = (masksv[l] * da).astype(BF16)
                dqe = jnp.dot(dam, sv["ke"][l].astype(BF16), preferred_element_type=F32)
                dke = lax.dot_general(dam, sv["qe"][l].astype(BF16), _DIMS["tn"], preferred_element_type=F32)
                xl = x[l * c:(l + 1) * c]
                dq = dq + dqe * xl
                dk = dk + dke * xl
                de.append(dqe * sv["qe"][l] + dke * sv["ke"][l])
            dd = scale * jnp.sum(do * hi, axis=-1, keepdims=True)
            dq = dq + dd * sv["kv"]
            dk = dk + dd * sv["qv"]
            last = lax.broadcasted_iota(jnp.int32, (c, 1), 0) == c - 1
            de.append(dqb * sv["qb"] + jnp.where(last, dxl * sv["x_last"], 0.0))
            de.append(dkb * sv["kb"])
            dlf = _sum3(jnp.dot(malltv, _split3(jnp.concatenate(de, axis=0)), preferred_element_type=F32))
            sg = sv["sg"]
            df = jnp.where(sv["f"] > F_MIN, dlf * valid / sv["fc"], 0.0)
            dkm = dk * valid
            dsg = (df - dkm) * (1.0 - lbv)
            dhf_ref[rows, :] = dsg * sg * (1.0 - sg)
            dlb_ref[...] += jnp.sum((df - dkm) * (1.0 - sg), axis=0, keepdims=True)
            sq = sv["sq"]
            dhq_ref[rows, :] = dq * sq * (1.0 + hq * (1.0 - sq))
            dhi_ref[rows, :] = dv
            return carry

        lax.fori_loop(0, ncb, chunk, 0)

    zb = lambda off: pl.BlockSpec((tb, HG_D), lambda h, i: (nb - 1 - i, off // HG_D + h))
    head = pl.BlockSpec((tb, HG_D), lambda h, i: (nb - 1 - i, h))
    const = lambda shape: pl.BlockSpec(shape, lambda h, i: (0,) * len(shape))
    lbs = pl.BlockSpec((1, HG_D), lambda h, i: (0, h))
    outs = pl.pallas_call(
        body, name=name, grid=(HG_HEADS, nb),
        in_specs=[zb(zl["hq"]), zb(zl["hf"]), zb(zl["hi"]), zb(zl["hg"]), head, head,
                  pl.BlockSpec((1, ncb, HG_D, HG_D), lambda h, i: (h, nb - 1 - i, 0, 0)),
                  lbs, const((1, HG_D)), const(mall.shape), const(mall_t.shape), const(masks.shape)],
        out_specs=[head, head, head, head, lbs, const((1, HG_D))],
        out_shape=[jax.ShapeDtypeStruct((lp, w), F32)] * 4
                  + [jax.ShapeDtypeStruct((1, w), F32), jax.ShapeDtypeStruct((1, HG_D), F32)],
        scratch_shapes=[pltpu.VMEM((HG_D, HG_D), F32)],
        compiler_params=_params("arbitrary", "arbitrary"),
    )(z, z, z, z, o_pre, dob, states, lb.reshape(1, w), nw.reshape(1, HG_D), mall, mall_t, masks)
    dhq, dhf, dhi, dhg, dlb, dnw = outs
    return dhq, dhf, dhi, dhg, dlb[0], dnw[0]


def _loss_head(h, w, tpad, *, lv, name):
    lp, d = h.shape
    tm = _pick(lp, ROW_TILE, 8)

    def body(h_ref, w_ref, t_ref, dh_ref, loss_ref, dw_ref):
        i = pl.program_id(0)
        r0 = i * tm + lax.broadcasted_iota(jnp.int32, (tm, 1), 0)
        valid = ((r0 >= ROW_X) & (r0 < lv)).astype(F32)
        xv, wv = h_ref[...], w_ref[...]
        r = lax.rsqrt(jnp.mean(xv * xv, axis=-1, keepdims=True) + EPS)
        e = (xv * r * wv - t_ref[...]) * valid
        dy = e * (1.0 / d)
        wdy = dy * wv
        dh_ref[...] = r * wdy - xv * (r * r * r) * jnp.mean(xv * wdy, axis=-1, keepdims=True)

        @pl.when(i == 0)
        def _():
            loss_ref[...] = jnp.zeros_like(loss_ref)
            dw_ref[...] = jnp.zeros_like(dw_ref)

        loss_ref[...] += 0.5 * jnp.sum(jnp.mean(e * e, axis=-1, keepdims=True), axis=0, keepdims=True)
        dw_ref[...] += jnp.sum(dy * xv * r, axis=0, keepdims=True)

    row = pl.BlockSpec((tm, d), lambda i: (i, 0))
    vec = pl.BlockSpec((1, d), lambda i: (0, 0))
    dh, loss, dw = pl.pallas_call(
        body, name=name, grid=(lp // tm,), in_specs=[row, vec, row],
        out_specs=[row, pl.BlockSpec((8, LANES), lambda i: (0, 0)), vec],
        out_shape=[jax.ShapeDtypeStruct((lp, d), F32), jax.ShapeDtypeStruct((8, LANES), F32),
                   jax.ShapeDtypeStruct((1, d), F32)],
        compiler_params=_params("arbitrary"),
    )(h, w.reshape(1, d), tpad)
    return dh, loss[0, 0], dw[0]


def _adamw(w, g, m, v, *, name):
    shape = w.shape
    cols = shape[-1]
    rows = int(np.prod(shape[:-1])) if len(shape) > 1 else 1
    tr = _pick(rows, 256, 8)
    c1 = 1.0 - ADAM_B1 ** ADAM_STEP
    c2 = 1.0 - ADAM_B2 ** ADAM_STEP

    def body(w_ref, g_ref, m_ref, v_ref, d_ref, nm_ref, nv_ref):
        gv = g_ref[...]
        nm = ADAM_B1 * m_ref[...] + (1.0 - ADAM_B1) * gv
        nv = ADAM_B2 * v_ref[...] + (1.0 - ADAM_B2) * (gv * gv)
        d_ref[...] = -ADAM_LR * ((nm / c1) / (jnp.sqrt(nv / c2) + ADAM_EPS) + ADAM_WD * w_ref[...])
        nm_ref[...] = nm
        nv_ref[...] = nv

    spec = pl.BlockSpec((tr, cols), lambda i: (i, 0))
    r2 = lambda a: a.reshape(rows, cols)
    outs = pl.pallas_call(
        body, name=name, grid=(rows // tr,), in_specs=[spec] * 4, out_specs=[spec] * 3,
        out_shape=[jax.ShapeDtypeStruct((rows, cols), F32)] * 3, compiler_params=_params("parallel"),
    )(r2(w), r2(g), r2(m), r2(v))
    return tuple(o.reshape(shape) for o in outs)


HBM_SPEC = pl.BlockSpec(memory_space=pl.ANY)


def _coords():
    return lax.axis_index("x"), lax.axis_index("y"), lax.axis_index("c")


def _other_chips(x, y):
    return [(1 - x, y), (x, 1 - y), (1 - x, 1 - y)]


def _remote(src, dst, ssem, rsem, dev):
    return pltpu.make_async_remote_copy(src_ref=src, dst_ref=dst, send_sem=ssem, recv_sem=rsem,
                                        device_id=dev, device_id_type=MESH)


def _gather_chips(w, *, name):
    rows, cols = w.shape
    rh = rows // 2
    align = 8 * 4 // w.dtype.itemsize
    assert rh * 2 == rows and rh % align == 0

    def body(w_ref, out_ref, send_sems, recv_sems):
        x, y, c = _coords()
        k = 2 * x + y
        sib = (x, y, 1 - c)
        half = pl.ds(pl.multiple_of(c * rh, align), rh)
        ohalf = pl.ds(pl.multiple_of((1 - c) * rh, align), rh)
        chips = _other_chips(x, y)
        sent = []
        for j, (px, py) in enumerate(chips):
            cp = _remote(w_ref.at[half], out_ref.at[k, half], send_sems.at[j], recv_sems.at[j], (px, py, c))
            cp.start()
            sent.append(cp)
        for j, (px, py) in enumerate(chips):
            blk = out_ref.at[2 * px + py, half]
            _remote(w_ref.at[half], blk, send_sems.at[j], recv_sems.at[j], (px, py, c)).wait_recv()
            fw = _remote(blk, blk, send_sems.at[3 + j], recv_sems.at[3 + j], sib)
            fw.start()
            sent.append(fw)
        for j, (px, py) in enumerate(chips):
            blk = out_ref.at[2 * px + py, ohalf]
            _remote(blk, blk, send_sems.at[3 + j], recv_sems.at[3 + j], sib).wait_recv()
        for cp in sent:
            cp.wait_send()

    g4 = pl.pallas_call(
        body, name=name, in_specs=[HBM_SPEC], out_specs=HBM_SPEC,
        out_shape=jax.ShapeDtypeStruct((4, rows, cols), w.dtype),
        scratch_shapes=[pltpu.SemaphoreType.DMA((6,)), pltpu.SemaphoreType.DMA((6,))],
    )(w)
    xi, yi, _ = _coords()
    return lax.dynamic_update_slice(g4, w[None], (2 * xi + yi, 0, 0))


def _swap_halves(gp, *, name):
    n, rows, cols = gp.shape
    rh = rows // 2

    def body(g_ref, out_ref, send_sems, recv_sems):
        x, y, c = _coords()
        sib = (x, y, 1 - c)
        ohalf = pl.ds(pl.multiple_of((1 - c) * rh, 8), rh)
        cps = [_remote(g_ref.at[s, ohalf], out_ref.at[s], send_sems.at[s], recv_sems.at[s], sib) for s in range(n)]
        for cp in cps:
            cp.start()
        for cp in cps:
            cp.wait_recv()
        for cp in cps:
            cp.wait_send()

    return pl.pallas_call(
        body, name=name, in_specs=[HBM_SPEC], out_specs=HBM_SPEC,
        out_shape=jax.ShapeDtypeStruct((n, rh, cols), gp.dtype),
        scratch_shapes=[pltpu.SemaphoreType.DMA((n,)), pltpu.SemaphoreType.DMA((n,))],
    )(gp)


def _add_half(gp, got, cidx, *, name):
    n, rows, cols = gp.shape
    rh = rows // 2
    tr = _pick(rh, 512, 16)
    nrb = rh // tr

    def body(c_ref, a_ref, b_ref, o_ref):
        o_ref[...] = (a_ref[...] + b_ref[...]).astype(BF16)

    grid_spec = pltpu.PrefetchScalarGridSpec(
        num_scalar_prefetch=1, grid=(n, nrb),
        in_specs=[pl.BlockSpec((1, tr, cols), lambda s, i, c_ref: (s, c_ref[0] * nrb + i, 0)),
                  pl.BlockSpec((1, tr, cols), lambda s, i, c_ref: (s, i, 0))],
        out_specs=pl.BlockSpec((1, tr, cols), lambda s, i, c_ref: (s, i, 0)))
    return pl.pallas_call(
        body, name=name, grid_spec=grid_spec, out_shape=jax.ShapeDtypeStruct((n, rh, cols), BF16),
        compiler_params=_params("parallel", "parallel"),
    )(cidx, gp, got)


def _scatter_chips(p, *, name):
    _, rh, cols = p.shape

    def body(p_ref, out_ref, send_sems, recv_sems):
        x, y, c = _coords()
        cps = []
        for j, (px, py) in enumerate(_other_chips(x, y)):
            cps.append(_remote(p_ref.at[2 * px + py], out_ref.at[j], send_sems.at[j], recv_sems.at[j], (px, py, c)))
        for cp in cps:
            cp.start()
        for cp in cps:
            cp.wait_recv()
        for cp in cps:
            cp.wait_send()

    return pl.pallas_call(
        body, name=name, in_specs=[HBM_SPEC], out_specs=HBM_SPEC,
        out_shape=jax.ShapeDtypeStruct((3, rh, cols), p.dtype),
        scratch_shapes=[pltpu.SemaphoreType.DMA((3,)), pltpu.SemaphoreType.DMA((3,))],
    )(p)


def _sum_arrivals(p, land, kidx, *, name):
    _, rh, cols = p.shape
    tr = _pick(rh, 512, 16)

    def body(k_ref, a_ref, l_ref, o_ref):
        f = lambda v: v.astype(F32)
        o_ref[...] = ((f(a_ref[0]) + f(l_ref[0])) + f(l_ref[1])) + f(l_ref[2])

    grid_spec = pltpu.PrefetchScalarGridSpec(
        num_scalar_prefetch=1, grid=(rh // tr,),
        in_specs=[pl.BlockSpec((1, tr, cols), lambda i, k_ref: (k_ref[0], i, 0)),
                  pl.BlockSpec((3, tr, cols), lambda i, k_ref: (0, i, 0))],
        out_specs=pl.BlockSpec((tr, cols), lambda i, k_ref: (i, 0)))
    return pl.pallas_call(
        body, name=name, grid_spec=grid_spec, out_shape=jax.ShapeDtypeStruct((rh, cols), F32),
        compiler_params=_params("parallel"),
    )(kidx, p, land)


def _join_halves(q, *, name):
    rh, cols = q.shape

    def body(q_ref, out_ref, send_sem, recv_sem):
        x, y, c = _coords()
        half = pl.ds(pl.multiple_of(c * rh, 8), rh)
        ohalf = pl.ds(pl.multiple_of((1 - c) * rh, 8), rh)
        cp = _remote(q_ref, out_ref.at[half], send_sem, recv_sem, (x, y, 1 - c))
        cp.start()
        _remote(q_ref, out_ref.at[ohalf], send_sem, recv_sem, (x, y, 1 - c)).wait_recv()
        cp.wait_send()

    full = pl.pallas_call(
        body, name=name, in_specs=[HBM_SPEC], out_specs=HBM_SPEC,
        out_shape=jax.ShapeDtypeStruct((2 * rh, cols), q.dtype),
        scratch_shapes=[pltpu.SemaphoreType.DMA, pltpu.SemaphoreType.DMA],
    )(q)
    return lax.dynamic_update_slice(full, q, (lax.axis_index("c") * rh, 0))


def _reduce_scatter(gp, cidx, kidx, *, tag):
    got = _swap_halves(gp, name=f"rs_swap_{tag}")
    p = _add_half(gp, got, cidx, name=f"rs_add_{tag}")
    land = _scatter_chips(p, name=f"rs_scatter_{tag}")
    q = _sum_arrivals(p, land, kidx, name=f"rs_sum_{tag}")
    return _join_halves(q, name=f"rs_join_{tag}")


def _allreduce_small(s, *, name):
    rows, cols = s.shape

    def body(s_ref, o_ref, buf, send_sems, recv_sems):
        x, y, c = _coords()
        me = 4 * x + 2 * y + c
        buf[me] = s_ref[...]
        cps = []
        for r in range(1, 8):
            peer = tuple((1 - v) if (r >> sh) & 1 else v for v, sh in ((x, 2), (y, 1), (c, 0)))
            cps.append(_remote(s_ref, buf.at[me], send_sems.at[r - 1], recv_sems.at[r - 1], peer))
        for cp in cps:
            cp.start()
        for cp in cps:
            cp.wait_recv()
        for cp in cps:
            cp.wait_send()
        acc = buf[0]
        for d in range(1, 8):
            acc = acc + buf[d]
        o_ref[...] = acc

    vm = pl.BlockSpec(memory_space=pltpu.VMEM)
    return pl.pallas_call(
        body, name=name, in_specs=[vm], out_specs=vm, out_shape=jax.ShapeDtypeStruct((rows, cols), F32),
        scratch_shapes=[pltpu.VMEM((8, rows, cols), F32), pltpu.SemaphoreType.DMA((7,)),
                        pltpu.SemaphoreType.DMA((7,))],
    )(s)


PACKED = ("ffn1_w_gu", "ffn1_w_down", "w_in", "w_uq", "w_ukv", "w_proj_attn", "w_proj_rec", "w_out",
          "ffn2_w_gu", "ffn2_w_down")
ROW_SHARDED = ("ffn1_w_down", "w_out", "ffn2_w_down")


def _pack_plan(shard_shapes):
    plan, off = {}, 0
    for n in PACKED:
        r, c = shard_shapes[n]
        assert (r * c) % PACK_W == 0
        plan[n] = (off, r * c // PACK_W, (r, c))
        off += r * c // PACK_W
    total = -(-off // 32) * 32
    return plan, total


def _pack(tensors, plan, total, dtype):
    parts = [tensors[n].astype(dtype).reshape(-1, PACK_W) for n in PACKED]
    used = sum(p.shape[0] for p in parts)
    if total > used:
        parts.append(jnp.zeros((total - used, PACK_W), dtype))
    return jnp.concatenate(parts, axis=0)


def _unpack_full(g4, plan):
    out = {}
    for n in PACKED:
        off, nr, (r, c) = plan[n]
        sh = g4[:, off:off + nr].reshape(4, r, c)
        out[n] = jnp.concatenate(list(sh), axis=0 if n in ROW_SHARDED else 1)
    return out


def _pack_grads(grads, plan, total):
    blocks = []
    for s in range(4):
        t = {}
        for n in PACKED:
            _, _, (r, c) = plan[n]
            t[n] = grads[n][s * r:(s + 1) * r] if n in ROW_SHARDED else grads[n][:, s * c:(s + 1) * c]
        blocks.append(_pack(t, plan, total, F32))
    return jnp.stack(blocks)


def _unpack_shard(p, plan):
    return {n: p[plan[n][0]:plan[n][0] + plan[n][1]].reshape(plan[n][2]) for n in PACKED}


def _swap_cols(w):
    hlf = w.shape[1] // 2
    return jnp.concatenate([-w[:, hlf:], w[:, :hlf]], axis=1)


def _unswap_cols(dw):
    hlf = dw.shape[1] // 2
    return jnp.concatenate([dw[:, hlf:], -dw[:, :hlf]], axis=1)


def _layer_weights(full, d):
    zl = _z_layout(d)
    f = full["ffn1_w_down"].shape[0]
    w_in = full["w_in"]
    o = 0
    cols = {}
    for nm, wd in (("cq", Q_LORA), ("ckv", KV_LORA), ("kpe", QK_ROPE), ("hq", 512), ("hf", 512), ("hi", 512),
                   ("hg", 512), ("ga", d), ("gb", d)):
        cols[nm] = w_in[:, o:o + wd]
        o += wd
    zc = lambda n: jnp.zeros((d, n), BF16)
    win_p = jnp.concatenate(
        [cols["cq"], zc(QK_NOPE), cols["kpe"], zc(32), cols["ckv"], zc(QK_NOPE), _swap_cols(cols["kpe"]), zc(32),
         zc(LANES), cols["ga"], cols["gb"], cols["hq"], cols["hf"], cols["hi"], cols["hg"]], axis=1)
    assert win_p.shape[1] == zl["total"]
    wq = full["w_uq"].reshape(Q_LORA, MLA_HEADS, QK_NOPE + QK_ROPE)
    nope, rope = wq[:, :, :QK_NOPE], wq[:, :, QK_NOPE:]
    z32 = jnp.zeros((Q_LORA, MLA_HEADS, 32), BF16)
    z64 = jnp.zeros((Q_LORA, MLA_HEADS, QK_NOPE), BF16)
    rope_sw = jnp.concatenate([-rope[:, :, 16:], rope[:, :, :16]], axis=2)
    wqa = jnp.concatenate([nope, rope, z32], axis=2).reshape(Q_LORA, QW)
    wqb = jnp.concatenate([z64, rope_sw, z32], axis=2).reshape(Q_LORA, QW)
    wpa = full["w_proj_attn"].reshape(MLA_HEADS, V_HEAD, d)
    wpa_p = jnp.concatenate([jnp.zeros_like(wpa), wpa], axis=1).reshape(QW, d)
    return dict(
        wg1=full["ffn1_w_gu"][:, :f], wu1=full["ffn1_w_gu"][:, f:], wd1=full["ffn1_w_down"],
        wg2=full["ffn2_w_gu"][:, :f], wu2=full["ffn2_w_gu"][:, f:], wd2=full["ffn2_w_down"],
        win=win_p, wq2=jnp.concatenate([wqa, wqb], axis=1), wqa=wqa, wqb=wqb, wkv=full["w_ukv"], wpa=wpa_p,
        wpr=full["w_proj_rec"], wout=full["w_out"])


def _natural_grads(g, d):
    zl = _z_layout(d)
    dwin = g["win"]
    kpe = dwin[:, Z_KPA + QK_NOPE:Z_KPA + QK_NOPE + QK_ROPE] + _unswap_cols(
        dwin[:, Z_KPB + QK_NOPE:Z_KPB + QK_NOPE + QK_ROPE])
    w_in = jnp.concatenate(
        [dwin[:, Z_Q:Z_Q + Q_LORA], dwin[:, Z_KV:Z_KV + KV_LORA], kpe, dwin[:, zl["hq"]:zl["hq"] + 2048],
         dwin[:, zl["ga"]:zl["ga"] + 2 * d]], axis=1)
    qa = g["wqa"].reshape(Q_LORA, MLA_HEADS, HEAD_W)
    qb = g["wqb"].reshape(Q_LORA, MLA_HEADS, HEAD_W)[:, :, QK_NOPE:QK_NOPE + QK_ROPE]
    rope = qa[:, :, QK_NOPE:QK_NOPE + QK_ROPE] + jnp.concatenate([qb[:, :, 16:], -qb[:, :, :16]], axis=2)
    w_uq = jnp.concatenate([qa[:, :, :QK_NOPE], rope], axis=2).reshape(Q_LORA, -1)
    wpa = g["wpa"].reshape(MLA_HEADS, 2 * V_HEAD, d)[:, V_HEAD:].reshape(MLA_HEADS * V_HEAD, d)
    return dict(
        ffn1_w_gu=jnp.concatenate([g["wg1"], g["wu1"]], axis=1), ffn1_w_down=g["wd1"],
        ffn2_w_gu=jnp.concatenate([g["wg2"], g["wu2"]], axis=1), ffn2_w_down=g["wd2"],
        w_in=w_in, w_uq=w_uq, w_ukv=g["wkv"], w_proj_attn=wpa, w_proj_rec=g["wpr"], w_out=g["wout"])


def _rope_tables(lp):
    pos = jnp.maximum(jnp.arange(lp) - FRONT, 0).astype(F32)
    half = QK_ROPE // 2
    inv = ROPE_THETA ** (-jnp.arange(half, dtype=F32) / half)
    ang = pos[:, None] * inv[None, :]
    cos, sin = jnp.cos(ang), jnp.sin(ang)
    cos_t = jnp.concatenate([jnp.ones((lp, QK_NOPE), F32), cos, cos, jnp.zeros((lp, 32), F32)], axis=1)
    sin_t = jnp.concatenate([jnp.zeros((lp, QK_NOPE), F32), sin, sin, jnp.zeros((lp, 32), F32)], axis=1)
    return cos_t, sin_t


def _lower_bounds(raw):
    p = jax.nn.softmax(raw.astype(F32), axis=0)
    return jnp.cumsum(p, axis=0) - p[0:1]


def _ffn_fwd(h, nw, wg, wu, wd, tag):
    a, a_t = _rmsnorm_fwd(h, nw, width=h.shape[1], col_block=0, transposed=True, name=f"norm_{tag}")
    g = _matmul([(a, wg)], "nn", name=f"gate_{tag}")
    u = _matmul([(a, wu)], "nn", name=f"up_{tag}")
    act, act_t = _swiglu_fwd(g, u, name=f"swiglu_{tag}")
    out = _matmul([(act, wd)], "nn", res=h, scale=0.5, name=f"down_{tag}")
    return out, dict(h=h, a_t=a_t, g=g, u=u, act_t=act_t)


def _ffn_bwd(dout, sv, nw, wg, wu, wd, lv, tag):
    dact = _matmul([(dout, wd)], "nt", scale=0.5, name=f"ddown_{tag}")
    dwd = _matmul([(sv["act_t"], dout)], "nn", scale=0.5, name=f"dwdown_{tag}")
    dg, du = _swiglu_bwd(dact, sv["g"], sv["u"], name=f"dswiglu_{tag}")
    dwg = _matmul([(sv["a_t"], dg)], "nn", name=f"dwgate_{tag}")
    dwu = _matmul([(sv["a_t"], du)], "nn", name=f"dwup_{tag}")
    da = _matmul([(dg, wg), (du, wu)], "nt", name=f"dnormed_{tag}")
    dh, dn = _rmsnorm_bwd(sv["h"], nw, da, width=da.shape[1], col_block=0, lv=lv, dres=dout, name=f"dnorm_{tag}")
    return dh, dn, dwg, dwu, dwd


def _layer_fwd(h0, lw, sm, lb, tabs, consts, lv, l):
    d = h0.shape[1]
    zl = _z_layout(d)
    cos_t, sin_t = tabs
    h1, s1 = _ffn_fwd(h0, sm["ffn1_norm"], lw["wg1"], lw["wu1"], lw["wd1"], f"ffn1_{l}")
    um, um_t = _rmsnorm_fwd(h1, sm["mix_norm"], width=d, col_block=0, transposed=True, name=f"norm_mix_{l}")
    z = _matmul([(um, lw["win"])], "nn", name=f"inproj_{l}")
    qn = _rmsnorm_fwd(z, sm["q_norm"], width=Q_LORA, col_block=Z_Q // Q_LORA, name=f"norm_q_{l}")
    kvn = _rmsnorm_fwd(z, sm["kv_norm"], width=KV_LORA, col_block=Z_KV // KV_LORA, name=f"norm_kv_{l}")
    q2 = _matmul([(qn, lw["wq2"])], "nn", name=f"uq_{l}")
    kv = _matmul([(kvn, lw["wkv"])], "nn", name=f"ukv_{l}")
    q, k, v = _qkv_prep_fwd(q2, kv, z, cos_t, sin_t, name=f"qkv_{l}")
    o, lse = _attn_fwd(q, k, v, lv=lv, name=f"attn_{l}")
    ya = _matmul([(o, lw["wpa"])], "nn", name=f"proj_attn_{l}")
    o_pre, ob, states = _hgrn_fwd(z, lb, sm["hg_norm"], consts, zl=zl, lv=lv, name=f"hgrn_{l}")
    yb = _matmul([(ob, lw["wpr"])], "nn", name=f"proj_rec_{l}")
    mg, mg_t = _merge_fwd(ya, yb, z, zl=zl, name=f"merge_{l}")
    h2 = _matmul([(mg, lw["wout"])], "nn", res=h1, name=f"out_{l}")
    h3, s2 = _ffn_fwd(h2, sm["ffn2_norm"], lw["wg2"], lw["wu2"], lw["wd2"], f"ffn2_{l}")
    saved = dict(s1=s1, s2=s2, h1=h1, um_t=um_t, z=z, qn=qn, kvn=kvn, q=q, k=k, v=v, o=o, lse=lse, ya=ya, yb=yb,
                 o_pre=o_pre, ob=ob, states=states, mg_t=mg_t)
    return h3, saved


def _layer_bwd(dh3, sv, lw, sm, lb, tabs, consts, lv, l):
    d = dh3.shape[1]
    lp = dh3.shape[0]
    zl = _z_layout(d)
    cos_t, sin_t = tabs
    z = sv["z"]
    g = {}
    sg = {}
    dh2, sg["ffn2_norm"], g["wg2"], g["wu2"], g["wd2"] = _ffn_bwd(
        dh3, sv["s2"], sm["ffn2_norm"], lw["wg2"], lw["wu2"], lw["wd2"], lv, f"ffn2_{l}")
    dmg = _matmul([(dh2, lw["wout"])], "nt", name=f"dmerged_{l}")
    g["wout"] = _matmul([(sv["mg_t"], dh2)], "nn", name=f"dwout_{l}")
    dya, dyb, dga, dgb = _merge_bwd(dmg, sv["ya"], sv["yb"], z, zl=zl, name=f"dmerge_{l}")
    doa = _matmul([(dya, lw["wpa"])], "nt", name=f"dattn_out_{l}")
    g["wpa"] = _matmul([(sv["o"], dya)], "tn", name=f"dwproj_attn_{l}")
    dob = _matmul([(dyb, lw["wpr"])], "nt", name=f"drec_out_{l}")
    g["wpr"] = _matmul([(sv["ob"], dyb)], "tn", name=f"dwproj_rec_{l}")
    dhq, dhf, dhi, dhg, dlb, sg["hg_norm"] = _hgrn_bwd(
        z, sv["o_pre"], dob, sv["states"], lb, sm["hg_norm"], consts, zl=zl, lv=lv, name=f"dhgrn_{l}")
    delta = _attn_delta(doa, sv["o"], name=f"attn_delta_{l}")
    dq, dk, dv = _attn_bwd(sv["q"], sv["k"], sv["v"], doa, sv["lse"], delta, name=f"dattn_{l}")
    dqa, dqb, dkv, dza, dzb = _qkv_prep_bwd(dq, dk, dv, cos_t, sin_t, name=f"dqkv_{l}")
    dqn = _matmul([(dqa, lw["wqa"]), (dqb, lw["wqb"])], "nt", name=f"dqn_{l}")
    g["wqa"] = _matmul([(sv["qn"], dqa)], "tn", name=f"dwqa_{l}")
    g["wqb"] = _matmul([(sv["qn"], dqb)], "tn", name=f"dwqb_{l}")
    dkvn = _matmul([(dkv, lw["wkv"])], "nt", name=f"dkvn_{l}")
    g["wkv"] = _matmul([(sv["kvn"], dkv)], "tn", name=f"dwkv_{l}")
    dzq, sg["q_norm"] = _rmsnorm_bwd(z, sm["q_norm"], dqn, width=Q_LORA, col_block=Z_Q // Q_LORA, lv=lv,
                                     name=f"dnorm_q_{l}")
    dzkv, sg["kv_norm"] = _rmsnorm_bwd(z, sm["kv_norm"], dkvn, width=KV_LORA, col_block=Z_KV // KV_LORA, lv=lv,
                                       name=f"dnorm_kv_{l}")
    dz = jnp.concatenate([dzq, dza, dzkv, dzb, jnp.zeros((lp, LANES), F32), dga, dgb, dhq, dhf, dhi, dhg], axis=1)
    dum = _matmul([(dz, lw["win"])], "nt", name=f"dmixed_{l}")
    g["win"] = _matmul([(sv["um_t"], dz)], "nn", name=f"dwin_{l}")
    dh1, sg["mix_norm"] = _rmsnorm_bwd(sv["h1"], sm["mix_norm"], dum, width=d, col_block=0, lv=lv, dres=dh2,
                                       name=f"dnorm_mix_{l}")
    dh0, sg["ffn1_norm"], g["wg1"], g["wu1"], g["wd1"] = _ffn_bwd(
        dh1, sv["s1"], sm["ffn1_norm"], lw["wg1"], lw["wu1"], lw["wd1"], lv, f"ffn1_{l}")
    return dh0, g, sg, dlb


WEIGHTS = ("meta_tokens", "ffn1_norm", "ffn1_w_gu", "ffn1_w_down", "mix_norm", "w_in", "q_norm", "kv_norm", "w_uq",
           "w_ukv", "hg_lb_raw", "hg_norm", "w_proj_attn", "w_proj_rec", "w_out", "ffn2_norm", "ffn2_w_gu",
           "ffn2_w_down", "final_norm")
SMALL = ("ffn1_norm", "mix_norm", "q_norm", "kv_norm", "hg_lb_raw", "hg_norm", "ffn2_norm")


def _small_rows(vals):
    pad = lambda a: jnp.pad(a, ((0, 0), (0, PACK_W - a.shape[1])))
    rows = [pad(vals[n]) for n in SMALL]
    rows.append(pad(vals["final_norm"][None, :]))
    rows.append(pad(vals["meta_tokens"]))
    rows.append(pad(vals["loss"].reshape(1, 1)))
    s = jnp.concatenate(rows, axis=0)
    return jnp.pad(s, ((0, -s.shape[0] % 8), (0, 0)))


def _small_unrows(s, d, widths):
    out, o = {}, 0
    for n in SMALL:
        out[n] = s[o:o + DEPTH, :widths[n]]
        o += DEPTH
    out["final_norm"] = s[o, :d]
    o += 1
    out["meta_tokens"] = s[o:o + N_META, :d]
    o += N_META
    out["loss"] = s[o, 0]
    return out


def _step(args):
    x = args["x"][0]
    seq, d = x.shape
    assert d <= PACK_W
    lv = ROW_X + seq
    lp = -(-lv // ROW_TILE) * ROW_TILE
    xi, yi, ci = _coords()
    kidx = (2 * xi + yi).astype(jnp.int32).reshape(1)
    cidx = ci.astype(jnp.int32).reshape(1)
    consts = _hgrn_consts()
    tabs = _rope_tables(lp)

    shard_shapes = {n: args[n].shape[1:] for n in PACKED}
    plan, total = _pack_plan(shard_shapes)
    lws = []
    for l in range(DEPTH):
        packed = _pack({n: args[n][l] for n in PACKED}, plan, total, BF16)
        g4 = _gather_chips(packed, name=f"gather_{l}")
        lws.append(_layer_weights(_unpack_full(g4, plan), d))
    mt = args["meta_tokens"]
    mt4 = _gather_chips(mt, name="gather_meta")
    meta = jnp.concatenate(list(mt4), axis=1)

    sm = [{n: args[n][l] for n in SMALL} for l in range(DEPTH)]
    lbs = _lower_bounds(args["hg_lb_raw"])

    h = jnp.concatenate([jnp.zeros((FRONT, d), F32), meta, x, jnp.zeros((lp - lv, d), F32)], axis=0)
    saved = []
    for l in range(DEPTH):
        h, sv = _layer_fwd(h, lws[l], sm[l], lbs[l], tabs, consts, lv, l)
        saved.append(sv)
    tpad = jnp.pad(args["loss_target"][0], ((ROW_X, lp - lv), (0, 0)))
    dh, loss, dfinal = _loss_head(h, args["final_norm"], tpad, lv=lv, name="loss_head")

    small = {n: [None] * DEPTH for n in SMALL}
    dlbs = [None] * DEPTH
    shard_grads = [None] * DEPTH
    for l in reversed(range(DEPTH)):
        dh, g, sg, dlbs[l] = _layer_bwd(dh, saved[l], lws[l], sm[l], lbs[l], tabs, consts, lv, l)
        for n in sg:
            small[n][l] = sg[n]
        gp = _pack_grads(_natural_grads(g, d), plan, total)
        shard_grads[l] = _unpack_shard(_reduce_scatter(gp, cidx, kidx, tag=str(l)), plan)

    _, lb_vjp = jax.vjp(_lower_bounds, args["hg_lb_raw"])
    small_vals = {n: jnp.stack(small[n]) for n in SMALL if n != "hg_lb_raw"}
    small_vals["hg_lb_raw"] = lb_vjp(jnp.stack(dlbs))[0]
    small_vals["final_norm"] = dfinal
    small_vals["meta_tokens"] = dh[FRONT:ROW_X]
    small_vals["loss"] = loss
    widths = {n: args[n].shape[1] for n in SMALL}
    tot = _small_unrows(_allreduce_small(_small_rows(small_vals), name="allreduce_small"), d, widths)

    grads = {n: jnp.stack([shard_grads[l][n] for l in range(DEPTH)]) for n in PACKED}
    for n in SMALL:
        grads[n] = tot[n]
    grads["final_norm"] = tot["final_norm"]
    mcols = mt.shape[1]
    grads["meta_tokens"] = lax.dynamic_slice_in_dim(tot["meta_tokens"], (2 * xi + yi) * mcols, mcols, axis=1)
    grad_x = dh[ROW_X:lv][None]

    delta, new_m, new_v = {}, {}, {}
    for n in WEIGHTS:
        delta[n], new_m[n], new_v[n] = _adamw(args[n], grads[n], args["m_" + n], args["v_" + n], name=f"adamw_{n}")
    return (tot["loss"], grad_x, *[grads[n] for n in WEIGHTS], *[delta[n] for n in WEIGHTS],
            *[new_m[n] for n in WEIGHTS], *[new_v[n] for n in WEIGHTS])


def kernel(x, meta_tokens, ffn1_norm, ffn1_w_gu, ffn1_w_down, mix_norm, w_in, q_norm, kv_norm, w_uq, w_ukv, hg_lb_raw, hg_norm, w_proj_attn, w_proj_rec, w_out, ffn2_norm, ffn2_w_gu, ffn2_w_down, final_norm, loss_target, m_meta_tokens, m_ffn1_norm, m_ffn1_w_gu, m_ffn1_w_down, m_mix_norm, m_w_in, m_q_norm, m_kv_norm, m_w_uq, m_w_ukv, m_hg_lb_raw, m_hg_norm, m_w_proj_attn, m_w_proj_rec, m_w_out, m_ffn2_norm, m_ffn2_w_gu, m_ffn2_w_down, m_final_norm, v_meta_tokens, v_ffn1_norm, v_ffn1_w_gu, v_ffn1_w_down, v_mix_norm, v_w_in, v_q_norm, v_kv_norm, v_w_uq, v_w_ukv, v_hg_lb_raw, v_hg_norm, v_w_proj_attn, v_w_proj_rec, v_w_out, v_ffn2_norm, v_ffn2_w_gu, v_ffn2_w_down, v_final_norm):
    return _step(dict(locals()))
```

```python
import functools
import math

import numpy as np
import jax
import jax.numpy as jnp
from jax import lax
from jax.experimental import pallas as pl
from jax.experimental.pallas import tpu as pltpu

F32 = jnp.float32
BF16 = jnp.bfloat16

N_META = 16
MLA_HEADS = 8
Q_LORA = 384
KV_LORA = 256
QK_NOPE = 64
QK_ROPE = 32
V_HEAD = 64
ROPE_THETA = 10000.0
HG_HEADS = 4
HG_D = 128
HG_CHUNK = 64
EPS = 1e-6
NEG_BIG = -1e30
F_MIN = 1e-20
DEPTH = 4

ADAM_LR = 0.001
ADAM_B1 = 0.9
ADAM_B2 = 0.999
ADAM_EPS = 1e-08
ADAM_WD = 0.01
ADAM_STEP = 10

LANES = 128
FRONT = (-N_META) % HG_CHUNK
ROW_X = FRONT + N_META
ROW_TILE = 640
HEAD_W = 128
QW = MLA_HEADS * HEAD_W
VMEM_LIMIT = 48 * 1024 * 1024
PACK_W = 1024
MESH = pl.DeviceIdType.MESH

Z_Q, Z_KPA, Z_KV, Z_KPB, Z_PAD, Z_GA = 0, 384, 512, 768, 896, 1024


def _z_layout(d):
    ga = Z_GA
    gb = ga + d
    hq = gb + d
    hf = hq + 512
    hi = hf + 512
    hg = hi + 512
    return dict(ga=ga, gb=gb, hq=hq, hf=hf, hi=hi, hg=hg, total=hg + 512)


def _pick(dim, cap, mult=LANES):
    if dim <= cap:
        return dim
    best = None
    for t in range(mult, cap + 1, mult):
        if dim % t == 0:
            best = t
    assert best is not None, (dim, cap, mult)
    return best


def _params(*sem):
    return pltpu.CompilerParams(dimension_semantics=sem, vmem_limit_bytes=VMEM_LIMIT)


def _sigmoid(x):
    return 1.0 / (1.0 + jnp.exp(-x))


def _row_valid(row0, n, lv):
    r = row0 + lax.broadcasted_iota(jnp.int32, (n, 1), 0)
    return ((r >= FRONT) & (r < lv)).astype(F32)


_DIMS = {"nn": (((1,), (0,)), ((), ())), "nt": (((1,), (1,)), ((), ())), "tn": (((0,), (0,)), ((), ()))}


def _matmul(pairs, mode, *, name, out_dtype=F32, res=None, scale=1.0):
    a0, b0 = pairs[0]
    if mode == "nn":
        (m, k), n = a0.shape, b0.shape[1]
    elif mode == "nt":
        (m, k), n = a0.shape, b0.shape[0]
    else:
        (k, m), n = a0.shape, b0.shape[1]
    if mode == "tn":
        tm, tn, tk = _pick(m, 1024), _pick(n, 1408), _pick(k, ROW_TILE, 8)
    elif k > m:
        tm, tn, tk = _pick(m, 1408, 16), _pick(n, 1408), _pick(k, ROW_TILE, LANES)
    else:
        tm, tn, tk = _pick(m, ROW_TILE, 8), _pick(n, 1408), _pick(k, 1408)
    nk = k // tk
    npair = len(pairs)
    dims = _DIMS[mode]

    def body(*refs):
        ins = refs[:2 * npair]
        pos = 2 * npair
        res_ref = None
        if res is not None:
            res_ref = refs[pos]
            pos += 1
        o_ref = refs[pos]
        kk = pl.program_id(2)

        part = None
        for p in range(npair):
            a = ins[2 * p][...].astype(BF16)
            b = ins[2 * p + 1][...].astype(BF16)
            d = lax.dot_general(a, b, dims, preferred_element_type=F32)
            part = d if part is None else part + d

        def finish(r):
            if scale != 1.0:
                r = r * scale
            if res_ref is not None:
                r = r + res_ref[...]
            o_ref[...] = r.astype(out_dtype)

        if nk == 1:
            finish(part)
            return
        acc = refs[pos + 1]

        @pl.when(kk == 0)
        def _():
            acc[...] = part

        @pl.when(kk > 0)
        def _():
            acc[...] += part

        @pl.when(kk == nk - 1)
        def _():
            finish(acc[...])

    if mode == "nn":
        a_spec = pl.BlockSpec((tm, tk), lambda i, j, q: (i, q))
        b_spec = pl.BlockSpec((tk, tn), lambda i, j, q: (q, j))
    elif mode == "nt":
        a_spec = pl.BlockSpec((tm, tk), lambda i, j, q: (i, q))
        b_spec = pl.BlockSpec((tn, tk), lambda i, j, q: (j, q))
    else:
        a_spec = pl.BlockSpec((tk, tm), lambda i, j, q: (q, i))
        b_spec = pl.BlockSpec((tk, tn), lambda i, j, q: (q, j))
    o_spec = pl.BlockSpec((tm, tn), lambda i, j, q: (i, j))
    in_specs, args = [], []
    for a, b in pairs:
        in_specs += [a_spec, b_spec]
        args += [a, b]
    if res is not None:
        in_specs.append(o_spec)
        args.append(res)
    return pl.pallas_call(
        body, name=name, grid=(m // tm, n // tn, nk), in_specs=in_specs, out_specs=o_spec,
        out_shape=jax.ShapeDtypeStruct((m, n), out_dtype),
        scratch_shapes=[pltpu.VMEM((tm, tn), F32)] if nk > 1 else [],
        compiler_params=_params("parallel", "parallel", "arbitrary"),
    )(*args)


def _rmsnorm_fwd(x, w, *, width, col_block, name, transposed=False):
    lp = x.shape[0]
    tm = _pick(lp, ROW_TILE, 8)

    def body(x_ref, w_ref, o_ref, *ot_ref):
        xv = x_ref[...]
        r = lax.rsqrt(jnp.mean(xv * xv, axis=-1, keepdims=True) + EPS)
        y = xv * r * w_ref[...]
        o_ref[...] = y.astype(BF16)
        if transposed:
            ot_ref[0][...] = y.T.astype(BF16)

    out_specs = [pl.BlockSpec((tm, width), lambda i: (i, 0))]
    out_shape = [jax.ShapeDtypeStruct((lp, width), BF16)]
    if transposed:
        out_specs.append(pl.BlockSpec((width, tm), lambda i: (0, i)))
        out_shape.append(jax.ShapeDtypeStruct((width, lp), BF16))
    outs = pl.pallas_call(
        body, name=name, grid=(lp // tm,),
        in_specs=[pl.BlockSpec((tm, width), lambda i: (i, col_block)), pl.BlockSpec((1, width), lambda i: (0, 0))],
        out_specs=out_specs, out_shape=out_shape, compiler_params=_params("parallel"),
    )(x, w.reshape(1, width))
    return tuple(outs) if transposed else outs[0]


def _rmsnorm_bwd(x, w, dy, *, width, col_block, lv, name, dres=None):
    lp = x.shape[0]
    tm = _pick(lp, ROW_TILE, 8)

    def body(*refs):
        if dres is None:
            x_ref, w_ref, dy_ref, dx_ref, dw_ref = refs
            dres_ref = None
        else:
            x_ref, w_ref, dy_ref, dres_ref, dx_ref, dw_ref = refs
        i = pl.program_id(0)
        xv = x_ref[...]
        dyv = dy_ref[...] * _row_valid(i * tm, tm, lv)
        r = lax.rsqrt(jnp.mean(xv * xv, axis=-1, keepdims=True) + EPS)
        wdy = dyv * w_ref[...]
        dx = r * wdy - xv * (r * r * r) * jnp.mean(xv * wdy, axis=-1, keepdims=True)
        if dres_ref is not None:
            dx = dx + dres_ref[...]
        dx_ref[...] = dx

        @pl.when(i == 0)
        def _():
            dw_ref[...] = jnp.zeros_like(dw_ref)

        dw_ref[...] += jnp.sum(dyv * xv * r, axis=0, keepdims=True)

    row = pl.BlockSpec((tm, width), lambda i: (i, 0))
    in_specs = [pl.BlockSpec((tm, width), lambda i: (i, col_block)), pl.BlockSpec((1, width), lambda i: (0, 0)), row]
    args = [x, w.reshape(1, width), dy]
    if dres is not None:
        in_specs.append(row)
        args.append(dres)
    dx, dw = pl.pallas_call(
        body, name=name, grid=(lp // tm,), in_specs=in_specs,
        out_specs=[row, pl.BlockSpec((1, width), lambda i: (0, 0))],
        out_shape=[jax.ShapeDtypeStruct((lp, width), F32), jax.ShapeDtypeStruct((1, width), F32)],
        compiler_params=_params("arbitrary"),
    )(*args)
    return dx, dw[0]


def _swiglu_fwd(g, u, *, name):
    lp, f = g.shape
    tm, tf = _pick(lp, ROW_TILE, 8), _pick(f, 1408)

    def body(g_ref, u_ref, o_ref, ot_ref):
        gv = g_ref[...]
        act = gv * _sigmoid(gv) * u_ref[...]
        o_ref[...] = act.astype(BF16)
        ot_ref[...] = act.T.astype(BF16)

    spec = pl.BlockSpec((tm, tf), lambda i, j: (i, j))
    return pl.pallas_call(
        body, name=name, grid=(lp // tm, f // tf), in_specs=[spec, spec],
        out_specs=[spec, pl.BlockSpec((tf, tm), lambda i, j: (j, i))],
        out_shape=[jax.ShapeDtypeStruct((lp, f), BF16), jax.ShapeDtypeStruct((f, lp), BF16)],
        compiler_params=_params("parallel", "parallel"),
    )(g, u)


def _swiglu_bwd(dact, g, u, *, name):
    lp, f = g.shape
    tm, tf = _pick(lp, ROW_TILE, 8), _pick(f, 1408)

    def body(d_ref, g_ref, u_ref, dg_ref, du_ref):
        gv, dv = g_ref[...], d_ref[...]
        s = _sigmoid(gv)
        dg_ref[...] = (dv * u_ref[...] * s * (1.0 + gv * (1.0 - s))).astype(BF16)
        du_ref[...] = (dv * gv * s).astype(BF16)

    spec = pl.BlockSpec((tm, tf), lambda i, j: (i, j))
    return pl.pallas_call(
        body, name=name, grid=(lp // tm, f // tf), in_specs=[spec, spec, spec], out_specs=[spec, spec],
        out_shape=[jax.ShapeDtypeStruct((lp, f), BF16)] * 2, compiler_params=_params("parallel", "parallel"),
    )(dact, g, u)


def _merge_fwd(ya, yb, z, *, zl, name):
    lp, d = ya.shape
    tm, td = _pick(lp, ROW_TILE, 8), _pick(d, 512)
    oa, ob = zl["ga"] // td, zl["gb"] // td

    def body(ya_ref, yb_ref, ga_ref, gb_ref, o_ref, ot_ref):
        mg = _sigmoid(ga_ref[...]) * ya_ref[...] + _sigmoid(gb_ref[...]) * yb_ref[...]
        o_ref[...] = mg.astype(BF16)
        ot_ref[...] = mg.T.astype(BF16)

    spec = pl.BlockSpec((tm, td), lambda i, j: (i, j))
    return pl.pallas_call(
        body, name=name, grid=(lp // tm, d // td),
        in_specs=[spec, spec, pl.BlockSpec((tm, td), lambda i, j: (i, oa + j)),
                  pl.BlockSpec((tm, td), lambda i, j: (i, ob + j))],
        out_specs=[spec, pl.BlockSpec((td, tm), lambda i, j: (j, i))],
        out_shape=[jax.ShapeDtypeStruct((lp, d), BF16), jax.ShapeDtypeStruct((d, lp), BF16)],
        compiler_params=_params("parallel", "parallel"),
    )(ya, yb, z, z)


def _merge_bwd(dmg, ya, yb, z, *, zl, name):
    lp, d = ya.shape
    tm, td = _pick(lp, ROW_TILE, 8), _pick(d, 512)
    oa, ob = zl["ga"] // td, zl["gb"] // td

    def body(d_ref, ya_ref, yb_ref, ga_ref, gb_ref, dya_ref, dyb_ref, dga_ref, dgb_ref):
        dv = d_ref[...]
        sa, sb = _sigmoid(ga_ref[...]), _sigmoid(gb_ref[...])
        dya_ref[...] = (dv * sa).astype(BF16)
        dyb_ref[...] = (dv * sb).astype(BF16)
        dga_ref[...] = dv * ya_ref[...] * sa * (1.0 - sa)
        dgb_ref[...] = dv * yb_ref[...] * sb * (1.0 - sb)

    spec = pl.BlockSpec((tm, td), lambda i, j: (i, j))
    return pl.pallas_call(
        body, name=name, grid=(lp // tm, d // td),
        in_specs=[spec, spec, spec, pl.BlockSpec((tm, td), lambda i, j: (i, oa + j)),
                  pl.BlockSpec((tm, td), lambda i, j: (i, ob + j))],
        out_specs=[spec] * 4,
        out_shape=[jax.ShapeDtypeStruct((lp, d), BF16)] * 2 + [jax.ShapeDtypeStruct((lp, d), F32)] * 2,
        compiler_params=_params("parallel", "parallel"),
    )(dmg, ya, yb, z, z)


def _qkv_prep_fwd(q2, kv, z, cos_t, sin_t, *, name):
    lp = q2.shape[0]
    tm = _pick(lp, ROW_TILE, 8)
    h = MLA_HEADS

    def body(qa_ref, qb_ref, kv_ref, za_ref, zb_ref, c_ref, s_ref, q_ref, k_ref, v_ref):
        c, s = c_ref[...], s_ref[...]
        lane = lax.broadcasted_iota(jnp.int32, (tm, HEAD_W), 1)
        q_ref[...] = ((qa_ref[...] * c + qb_ref[...] * s) * Q_SCALE).astype(BF16)
        kr = jnp.where(lane >= QK_NOPE, za_ref[...] * c + zb_ref[...] * s, 0.0)
        kvv = kv_ref[...]
        k_ref[...] = (jnp.where(lane < QK_NOPE, kvv, 0.0) + kr).astype(BF16)
        v_ref[...] = jnp.where(lane >= QK_NOPE, kvv, 0.0).astype(BF16)

    blk = lambda f: pl.BlockSpec((tm, HEAD_W), f)
    out = blk(lambda i, j: (i, j))
    return pl.pallas_call(
        body, name=name, grid=(lp // tm, h),
        in_specs=[blk(lambda i, j: (i, j)), blk(lambda i, j: (i, h + j)), blk(lambda i, j: (i, j)),
                  blk(lambda i, j: (i, Z_KPA // HEAD_W)), blk(lambda i, j: (i, Z_KPB // HEAD_W)),
                  blk(lambda i, j: (i, 0)), blk(lambda i, j: (i, 0))],
        out_specs=[out, out, out], out_shape=[jax.ShapeDtypeStruct((lp, QW), BF16)] * 3,
        compiler_params=_params("parallel", "parallel"),
    )(q2, q2, kv, z, z, cos_t, sin_t)


def _qkv_prep_bwd(dq, dk, dv, cos_t, sin_t, *, name):
    lp = dq.shape[0]
    tm = _pick(lp, ROW_TILE, 8)
    h = MLA_HEADS

    def body(dq_ref, dk_ref, dv_ref, c_ref, s_ref, dqa_ref, dqb_ref, dkv_ref, dza_ref, dzb_ref):
        j = pl.program_id(1)
        c, s = c_ref[...], s_ref[...]
        lane = lax.broadcasted_iota(jnp.int32, (tm, HEAD_W), 1)
        dqv, dkv_ = dq_ref[...], dk_ref[...]
        dqa_ref[...] = (dqv * c).astype(BF16)
        dqb_ref[...] = (dqv * s).astype(BF16)
        dkv_ref[...] = jnp.where(lane < QK_NOPE, dkv_, dv_ref[...]).astype(BF16)
        dkr = jnp.where(lane >= QK_NOPE, dkv_, 0.0)

        @pl.when(j == 0)
        def _():
            dza_ref[...] = jnp.zeros_like(dza_ref)
            dzb_ref[...] = jnp.zeros_like(dzb_ref)

        dza_ref[...] += dkr * c
        dzb_ref[...] += dkr * s

    blk = lambda f: pl.BlockSpec((tm, HEAD_W), f)
    per_head, shared = blk(lambda i, j: (i, j)), blk(lambda i, j: (i, 0))
    return pl.pallas_call(
        body, name=name, grid=(lp // tm, h),
        in_specs=[per_head, per_head, per_head, shared, shared],
        out_specs=[per_head, per_head, per_head, shared, shared],
        out_shape=[jax.ShapeDtypeStruct((lp, QW), BF16)] * 3 + [jax.ShapeDtypeStruct((lp, HEAD_W), F32)] * 2,
        compiler_params=_params("parallel", "arbitrary"),
    )(dq, dk, dv, cos_t, sin_t)


def _attn_tile(lp):
    return _pick(lp, ROW_TILE, LANES)


Q_SCALE = (QK_NOPE + QK_ROPE) ** -0.5 * math.log2(math.e)


def _attn_consts(lp):
    t = _attn_tile(lp)
    nb = lp // t
    r = np.arange(t)
    causal = np.where(r[None, :] <= r[:, None], 0.0, NEG_BIG).astype(np.float32)
    front = np.where(r >= FRONT, 0.0, NEG_BIG).astype(np.float32)[None, :]
    diag = np.stack([np.minimum(causal, front), causal])
    qmaj = [(i, j) for i in range(nb) for j in range(i + 1)]
    kmaj = [(i, j) for j in range(nb) for i in range(j, nb)]
    tab = lambda pairs, c: jnp.asarray([p[c] for p in pairs], jnp.int32)
    return dict(diag=jnp.asarray(diag), front=jnp.asarray(front),
                fwd=(tab(qmaj, 0), tab(qmaj, 1)), bwd=(tab(kmaj, 0), tab(kmaj, 1)))


def _attn_fwd(q, k, v, ac, *, lv, name):
    lp = q.shape[0]
    t = _attn_tile(lp)
    nb = lp // t
    rep = t // HEAD_W
    qtab, ktab = ac["fwd"]

    def body(qt_ref, kt_ref, q_ref, k_ref, v_ref, bd_ref, bf_ref, o_ref, lse_ref, m_s, l_s, acc_s):
        step_id = pl.program_id(1)
        qb, kb = qt_ref[step_id], kt_ref[step_id]

        @pl.when(kb == 0)
        def _():
            m_s[...] = jnp.full_like(m_s, NEG_BIG)
            l_s[...] = jnp.zeros_like(l_s)
            acc_s[...] = jnp.zeros_like(acc_s)

        def step(bias):
            s = lax.dot_general(q_ref[...], k_ref[...], _DIMS["nt"], preferred_element_type=F32)
            if bias is not None:
                s = s + bias()
            m_prev = m_s[...]
            m_new = jnp.maximum(m_prev, jnp.max(s, axis=-1, keepdims=True))
            alpha = jnp.exp2(m_prev - m_new)
            p = jnp.exp2(s - jnp.tile(m_new, (1, rep)))
            l_s[...] = alpha * l_s[...] + jnp.sum(p, axis=-1, keepdims=True)
            acc_s[...] = alpha * acc_s[...] + jnp.dot(p.astype(BF16), v_ref[...], preferred_element_type=F32)
            m_s[...] = m_new

        @pl.when((kb > 0) & (kb < qb))
        def _():
            step(None)

        @pl.when((kb == 0) & (qb > 0))
        def _():
            step(lambda: bf_ref[...])

        @pl.when(kb == qb)
        def _():
            step(lambda: bd_ref[0])
            l = l_s[...]
            o_ref[...] = acc_s[...] / l * _row_valid(qb * t, t, lv)
            lse_ref[...] = m_s[...] + jnp.log2(l)

    qs = pl.BlockSpec((t, HEAD_W), lambda h, s, qt, kt: (qt[s], h))
    ks = pl.BlockSpec((t, HEAD_W), lambda h, s, qt, kt: (kt[s], h))
    grid_spec = pltpu.PrefetchScalarGridSpec(
        num_scalar_prefetch=2, grid=(MLA_HEADS, int(qtab.shape[0])),
        in_specs=[qs, ks, ks, pl.BlockSpec((1, t, t), lambda h, s, qt, kt: (jnp.minimum(qt[s], 1), 0, 0)),
                  pl.BlockSpec((1, t), lambda h, s, qt, kt: (0, 0))],
        out_specs=[qs, qs],
        scratch_shapes=[pltpu.VMEM((t, HEAD_W), F32), pltpu.VMEM((t, HEAD_W), F32), pltpu.VMEM((t, HEAD_W), F32)])
    return pl.pallas_call(
        body, name=name, grid_spec=grid_spec, out_shape=[jax.ShapeDtypeStruct((lp, QW), F32)] * 2,
        compiler_params=_params("parallel", "arbitrary"),
    )(qtab, ktab, q, k, v, ac["diag"], ac["front"])


def _attn_delta(do, o, *, name):
    lp = do.shape[0]
    tm = _pick(lp, ROW_TILE, 8)

    def body(do_ref, o_ref, d_ref):
        d_ref[...] = jnp.broadcast_to(jnp.sum(do_ref[...] * o_ref[...], axis=-1, keepdims=True), (tm, HEAD_W))

    spec = pl.BlockSpec((tm, HEAD_W), lambda i, j: (i, j))
    return pl.pallas_call(
        body, name=name, grid=(lp // tm, MLA_HEADS), in_specs=[spec, spec], out_specs=spec,
        out_shape=jax.ShapeDtypeStruct((lp, QW), F32), compiler_params=_params("parallel", "parallel"),
    )(do, o)


def _attn_bwd(q, k, v, do, lse, delta, ac, *, name):
    lp = q.shape[0]
    t = _attn_tile(lp)
    nb = lp // t
    rep = t // HEAD_W
    scale = (QK_NOPE + QK_ROPE) ** -0.5
    qtab, ktab = ac["bwd"]

    def body(qt_ref, kt_ref, q_ref, k_ref, v_ref, do_ref, lse_ref, dl_ref, bd_ref, bf_ref, dq_ref, dk_ref, dv_ref,
             dk_s, dv_s):
        step_id = pl.program_id(1)
        qb, kb = qt_ref[step_id], kt_ref[step_id]

        @pl.when(qb == kb)
        def _():
            dk_s[...] = jnp.zeros_like(dk_s)
            dv_s[...] = jnp.zeros_like(dv_s)

        def step(bias):
            qv, kv_, vv = q_ref[...], k_ref[...], v_ref[...]
            dov = do_ref[...].astype(BF16)
            s = lax.dot_general(qv, kv_, _DIMS["nt"], preferred_element_type=F32)
            if bias is not None:
                s = s + bias()
            p = jnp.exp2(s - jnp.tile(lse_ref[...], (1, rep)))
            dv_s[...] += lax.dot_general(p.astype(BF16), dov, _DIMS["tn"], preferred_element_type=F32)
            dp = lax.dot_general(dov, vv, _DIMS["nt"], preferred_element_type=F32)
            ds = (p * (dp - jnp.tile(dl_ref[...], (1, rep))) * scale).astype(BF16)
            dk_s[...] += lax.dot_general(ds, qv, _DIMS["tn"], preferred_element_type=F32)
            contrib = jnp.dot(ds, kv_, preferred_element_type=F32)
            rows = pl.ds(pl.multiple_of(qb * t, t), t)

            @pl.when(kb == 0)
            def _():
                dq_ref[rows, :] = contrib

            @pl.when(kb > 0)
            def _():
                dq_ref[rows, :] += contrib

        @pl.when((kb > 0) & (kb < qb))
        def _():
            step(None)

        @pl.when((kb == 0) & (qb > 0))
        def _():
            step(lambda: bf_ref[...])

        @pl.when(kb == qb)
        def _():
            step(lambda: bd_ref[0])

        @pl.when(qb == nb - 1)
        def _():
            dk_ref[...] = dk_s[...] * (1.0 / Q_SCALE)
            dv_ref[...] = dv_s[...]

    qs = pl.BlockSpec((t, HEAD_W), lambda h, s, qt, kt: (qt[s], h))
    ks = pl.BlockSpec((t, HEAD_W), lambda h, s, qt, kt: (kt[s], h))
    dqs = pl.BlockSpec((lp, HEAD_W), lambda h, s, qt, kt: (0, h))
    grid_spec = pltpu.PrefetchScalarGridSpec(
        num_scalar_prefetch=2, grid=(MLA_HEADS, int(qtab.shape[0])),
        in_specs=[qs, ks, ks, qs, qs, qs,
                  pl.BlockSpec((1, t, t), lambda h, s, qt, kt: (jnp.minimum(qt[s], 1), 0, 0)),
                  pl.BlockSpec((1, t), lambda h, s, qt, kt: (0, 0))],
        out_specs=[dqs, ks, ks],
        scratch_shapes=[pltpu.VMEM((t, HEAD_W), F32), pltpu.VMEM((t, HEAD_W), F32)])
    return pl.pallas_call(
        body, name=name, grid_spec=grid_spec, out_shape=[jax.ShapeDtypeStruct((lp, QW), F32)] * 3,
        compiler_params=_params("arbitrary", "arbitrary"),
    )(qtab, ktab, q, k, v, do, lse, delta, ac["diag"], ac["front"])


HG_LEVELS = (64, 32, 16, 8, 4, 2)
N_LEV = len(HG_LEVELS)


def _hgrn_consts():
    c = HG_CHUNK
    m = np.zeros((N_LEV + 2, c, c), np.float32)
    masks = np.zeros((N_LEV, c, c), np.float32)
    for li, p in enumerate(HG_LEVELS):
        for t in range(c):
            mid = (t // p) * p + p // 2
            if t >= mid:
                m[li, t, mid:t + 1] = 1.0
            else:
                m[li, t, t + 1:mid] = 1.0
            for s in range(c):
                if s // p == t // p and t >= mid and s < mid:
                    masks[li, t, s] = 1.0
    for t in range(c):
        m[N_LEV, t, :t + 1] = 1.0
        m[N_LEV + 1, t, t + 1:] = 1.0
    mall = m.reshape((N_LEV + 2) * c, c)
    return jnp.asarray(mall, BF16), jnp.asarray(mall.T.copy(), BF16), jnp.asarray(masks, F32)


def _split3(x):
    hi = x.astype(BF16)
    r = x - hi.astype(F32)
    mid = r.astype(BF16)
    lo = (r - mid.astype(F32)).astype(BF16)
    return jnp.concatenate([hi, mid, lo], axis=1)


def _sum3(e3):
    return e3[:, :HG_D] + e3[:, HG_D:2 * HG_D] + e3[:, 2 * HG_D:]


def _hgrn_chunk_fwd(hq, hf, hi, lb, valid, mall, masks, st):
    c = HG_CHUNK
    scale = HG_D ** -0.5
    sq = _sigmoid(hq)
    qv = hq * sq
    sg = _sigmoid(hf)
    f = lb + (1.0 - lb) * sg
    fc = jnp.maximum(f, F_MIN)
    lf = jnp.log(fc) * valid
    kv = (1.0 - lb) * (1.0 - sg) * valid
    e = _sum3(jnp.dot(mall, _split3(lf), preferred_element_type=F32))
    x = jnp.exp(e)
    a = jnp.zeros((c, c), F32)
    qe, ke = [], []
    for l in range(N_LEV):
        xl = x[l * c:(l + 1) * c]
        qe.append(qv * xl)
        ke.append(kv * xl)
        a = a + masks[l] * lax.dot_general(qe[l].astype(BF16), ke[l].astype(BF16), _DIMS["nt"],
                                           preferred_element_type=F32)
    row = lax.broadcasted_iota(jnp.int32, (c, c), 0)
    col = lax.broadcasted_iota(jnp.int32, (c, c), 1)
    a = a + jnp.where(row == col, jnp.sum(qv * kv, axis=-1, keepdims=True), 0.0)
    xb = x[N_LEV * c:(N_LEV + 1) * c]
    qb = qv * xb
    kb = kv * x[(N_LEV + 1) * c:]
    x_last = xb[c - 1:c]
    hib = hi.astype(BF16)
    o = scale * (jnp.dot(a.astype(BF16), hib, preferred_element_type=F32)
                 + lax.dot_general(qb.astype(BF16), st.astype(BF16), _DIMS["nt"], preferred_element_type=F32))
    st_new = st * x_last + lax.dot_general(hib, kb.astype(BF16), _DIMS["tn"], preferred_element_type=F32)
    saved = dict(sq=sq, qv=qv, sg=sg, f=f, fc=fc, kv=kv, x=x, a=a, qe=qe, ke=ke, qb=qb, kb=kb, x_last=x_last)
    return o, st_new, saved


def _hgrn_fwd(z, lb, nw, consts, *, zl, lv, name):
    lp = z.shape[0]
    tb = _pick(lp, ROW_TILE, HG_CHUNK)
    ncb = tb // HG_CHUNK
    nb = lp // tb
    mall, _, masks = consts
    w = HG_HEADS * HG_D

    def body(hq_ref, hf_ref, hi_ref, hg_ref, lb_ref, nw_ref, mall_ref, masks_ref, o_ref, ob_ref, st_ref, st_s):
        i = pl.program_id(1)

        @pl.when(i == 0)
        def _():
            st_s[...] = jnp.zeros_like(st_s)

        lbv, nwv = lb_ref[...], nw_ref[...]
        mallv, masksv = mall_ref[...], masks_ref[...]

        def chunk(cix, carry):
            r0 = pl.multiple_of(cix * HG_CHUNK, HG_CHUNK)
            rows = pl.ds(r0, HG_CHUNK)
            valid = _row_valid(i * tb + r0, HG_CHUNK, lv)
            st = st_s[...]
            st_ref[0, cix] = st
            o, st_new, _ = _hgrn_chunk_fwd(hq_ref[rows, :], hf_ref[rows, :], hi_ref[rows, :], lbv, valid,
                                           mallv, masksv, st)
            st_s[...] = st_new
            o_ref[rows, :] = o
            hg = hg_ref[rows, :]
            r = lax.rsqrt(jnp.mean(o * o, axis=-1, keepdims=True) + EPS)
            ob_ref[rows, :] = (o * r * nwv * (hg * _sigmoid(hg))).astype(BF16)
            return carry

        lax.fori_loop(0, ncb, chunk, 0)

    zb = lambda off: pl.BlockSpec((tb, HG_D), lambda h, i: (i, off // HG_D + h))
    head = pl.BlockSpec((tb, HG_D), lambda h, i: (i, h))
    const = lambda shape: pl.BlockSpec(shape, lambda h, i: (0,) * len(shape))
    return pl.pallas_call(
        body, name=name, grid=(HG_HEADS, nb),
        in_specs=[zb(zl["hq"]), zb(zl["hf"]), zb(zl["hi"]), zb(zl["hg"]),
                  pl.BlockSpec((1, HG_D), lambda h, i: (0, h)), const((1, HG_D)),
                  const(mall.shape), const(masks.shape)],
        out_specs=[head, head, pl.BlockSpec((1, ncb, HG_D, HG_D), lambda h, i: (h, i, 0, 0))],
        out_shape=[jax.ShapeDtypeStruct((lp, w), F32), jax.ShapeDtypeStruct((lp, w), BF16),
                   jax.ShapeDtypeStruct((HG_HEADS, lp // HG_CHUNK, HG_D, HG_D), F32)],
        scratch_shapes=[pltpu.VMEM((HG_D, HG_D), F32)],
        compiler_params=_params("parallel", "arbitrary"),
    )(z, z, z, z, lb.reshape(1, w), nw.reshape(1, HG_D), mall, masks)


def _hgrn_bwd(z, o_pre, dob, states, lb, nw, consts, *, zl, lv, name):
    lp = z.shape[0]
    tb = _pick(lp, ROW_TILE, HG_CHUNK)
    ncb = tb // HG_CHUNK
    nb = lp // tb
    mall, mall_t, masks = consts
    w = HG_HEADS * HG_D
    c = HG_CHUNK
    scale = HG_D ** -0.5

    def body(hq_ref, hf_ref, hi_ref, hg_ref, o_ref, dob_ref, st_ref, lb_ref, nw_ref, mall_ref, mallt_ref, masks_ref,
             dhq_ref, dhf_ref, dhi_ref, dhg_ref, dlb_ref, dnw_ref, dst_s):
        h, i = pl.program_id(0), pl.program_id(1)
        blk = nb - 1 - i

        @pl.when(i == 0)
        def _():
            dst_s[...] = jnp.zeros_like(dst_s)
            dlb_ref[...] = jnp.zeros_like(dlb_ref)

        @pl.when((i == 0) & (h == 0))
        def _():
            dnw_ref[...] = jnp.zeros_like(dnw_ref)

        lbv, nwv = lb_ref[...], nw_ref[...]
        mallv, malltv, masksv = mall_ref[...], mallt_ref[...], masks_ref[...]

        def chunk(jx, carry):
            cix = ncb - 1 - jx
            r0 = pl.multiple_of(cix * c, c)
            rows = pl.ds(r0, c)
            valid = _row_valid(blk * tb + r0, c, lv)
            hq, hf, hi, hg = hq_ref[rows, :], hf_ref[rows, :], hi_ref[rows, :], hg_ref[rows, :]
            st = st_ref[0, cix]
            _, _, sv = _hgrn_chunk_fwd(hq, hf, hi, lbv, valid, mallv, masksv, st)
            o, dout = o_ref[rows, :], dob_ref[rows, :]
            shg = _sigmoid(hg)
            r = lax.rsqrt(jnp.mean(o * o, axis=-1, keepdims=True) + EPS)
            don = dout * (hg * shg)
            dhg_ref[rows, :] = dout * (o * r * nwv) * shg * (1.0 + hg * (1.0 - shg))
            dnw_ref[...] += jnp.sum(don * o * r, axis=0, keepdims=True)
            wd = don * nwv
            do = r * wd - o * (r * r * r) * jnp.mean(o * wd, axis=-1, keepdims=True)
            dob16, hib = do.astype(BF16), hi.astype(BF16)
            dst = dst_s[...]
            dst16 = dst.astype(BF16)
            da = scale * lax.dot_general(dob16, hib, _DIMS["nt"], preferred_element_type=F32)
            dv = (scale * lax.dot_general(sv["a"].astype(BF16), dob16, _DIMS["tn"], preferred_element_type=F32)
                  + lax.dot_general(sv["kb"].astype(BF16), dst16, _DIMS["nt"], preferred_element_type=F32))
            dkb = jnp.dot(hib, dst16, preferred_element_type=F32)
            dqb = scale * jnp.dot(dob16, st.astype(BF16), preferred_element_type=F32)
            dst_s[...] = dst * sv["x_last"] + scale * lax.dot_general(dob16, sv["qb"].astype(BF16), _DIMS["tn"],
                                                                      preferred_element_type=F32)
            dxl = jnp.sum(dst * st, axis=0, keepdims=True)
            x = sv["x"]
            dq = dqb * x[N_LEV * c:(N_LEV + 1) * c]
            dk = dkb * x[(N_LEV + 1) * c:]
            de = []
            for l in range(N_LEV):
                dam = (masksv[l] * da).astype(BF16)
                dqe = jnp.dot(dam, sv["ke"][l].astype(BF16), preferred_element_type=F32)
                dke = lax.dot_general(dam, sv["qe"][l].astype(BF16), _DIMS["tn"], preferred_element_type=F32)
                xl = x[l * c:(l + 1) * c]
                dq = dq + dqe * xl
                dk = dk + dke * xl
                de.append(dqe * sv["qe"][l] + dke * sv["ke"][l])
            dd = scale * jnp.sum(do * hi, axis=-1, keepdims=True)
            dq = dq + dd * sv["kv"]
            dk = dk + dd * sv["qv"]
            last = lax.broadcasted_iota(jnp.int32, (c, 1), 0) == c - 1
            de.append(dqb * sv["qb"] + jnp.where(last, dxl * sv["x_last"], 0.0))
            de.append(dkb * sv["kb"])
            dlf = _sum3(jnp.dot(malltv, _split3(jnp.concatenate(de, axis=0)), preferred_element_type=F32))
            sg = sv["sg"]
            df = jnp.where(sv["f"] > F_MIN, dlf * valid / sv["fc"], 0.0)
            dkm = dk * valid
            dsg = (df - dkm) * (1.0 - lbv)
            dhf_ref[rows, :] = dsg * sg * (1.0 - sg)
            dlb_ref[...] += jnp.sum((df - dkm) * (1.0 - sg), axis=0, keepdims=True)
            sq = sv["sq"]
            dhq_ref[rows, :] = dq * sq * (1.0 + hq * (1.0 - sq))
            dhi_ref[rows, :] = dv
            return carry

        lax.fori_loop(0, ncb, chunk, 0)

    zb = lambda off: pl.BlockSpec((tb, HG_D), lambda h, i: (nb - 1 - i, off // HG_D + h))
    head = pl.BlockSpec((tb, HG_D), lambda h, i: (nb - 1 - i, h))
    const = lambda shape: pl.BlockSpec(shape, lambda h, i: (0,) * len(shape))
    lbs = pl.BlockSpec((1, HG_D), lambda h, i: (0, h))
    outs = pl.pallas_call(
        body, name=name, grid=(HG_HEADS, nb),
        in_specs=[zb(zl["hq"]), zb(zl["hf"]), zb(zl["hi"]), zb(zl["hg"]), head, head,
                  pl.BlockSpec((1, ncb, HG_D, HG_D), lambda h, i: (h, nb - 1 - i, 0, 0)),
                  lbs, const((1, HG_D)), const(mall.shape), const(mall_t.shape), const(masks.shape)],
        out_specs=[head, head, head, head, lbs, const((1, HG_D))],
        out_shape=[jax.ShapeDtypeStruct((lp, w), F32)] * 4
                  + [jax.ShapeDtypeStruct((1, w), F32), jax.ShapeDtypeStruct((1, HG_D), F32)],
        scratch_shapes=[pltpu.VMEM((HG_D, HG_D), F32)],
        compiler_params=_params("arbitrary", "arbitrary"),
    )(z, z, z, z, o_pre, dob, states, lb.reshape(1, w), nw.reshape(1, HG_D), mall, mall_t, masks)
    dhq, dhf, dhi, dhg, dlb, dnw = outs
    return dhq, dhf, dhi, dhg, dlb[0], dnw[0]


def _loss_head(h, w, tpad, *, lv, name):
    lp, d = h.shape
    tm = _pick(lp, ROW_TILE, 8)

    def body(h_ref, w_ref, t_ref, dh_ref, loss_ref, dw_ref):
        i = pl.program_id(0)
        r0 = i * tm + lax.broadcasted_iota(jnp.int32, (tm, 1), 0)
        valid = ((r0 >= ROW_X) & (r0 < lv)).astype(F32)
        xv, wv = h_ref[...], w_ref[...]
        r = lax.rsqrt(jnp.mean(xv * xv, axis=-1, keepdims=True) + EPS)
        e = (xv * r * wv - t_ref[...]) * valid
        dy = e * (1.0 / d)
        wdy = dy * wv
        dh_ref[...] = r * wdy - xv * (r * r * r) * jnp.mean(xv * wdy, axis=-1, keepdims=True)

        @pl.when(i == 0)
        def _():
            loss_ref[...] = jnp.zeros_like(loss_ref)
            dw_ref[...] = jnp.zeros_like(dw_ref)

        loss_ref[...] += 0.5 * jnp.sum(jnp.mean(e * e, axis=-1, keepdims=True), axis=0, keepdims=True)
        dw_ref[...] += jnp.sum(dy * xv * r, axis=0, keepdims=True)

    row = pl.BlockSpec((tm, d), lambda i: (i, 0))
    vec = pl.BlockSpec((1, d), lambda i: (0, 0))
    dh, loss, dw = pl.pallas_call(
        body, name=name, grid=(lp // tm,), in_specs=[row, vec, row],
        out_specs=[row, pl.BlockSpec((8, LANES), lambda i: (0, 0)), vec],
        out_shape=[jax.ShapeDtypeStruct((lp, d), F32), jax.ShapeDtypeStruct((8, LANES), F32),
                   jax.ShapeDtypeStruct((1, d), F32)],
        compiler_params=_params("arbitrary"),
    )(h, w.reshape(1, d), tpad)
    return dh, loss[0, 0], dw[0]


def _adamw(w, g, m, v, *, name):
    shape = w.shape
    cols = shape[-1]
    rows = int(np.prod(shape[:-1])) if len(shape) > 1 else 1
    tr = _pick(rows, 256, 8)
    c1 = 1.0 - ADAM_B1 ** ADAM_STEP
    c2 = 1.0 - ADAM_B2 ** ADAM_STEP

    def body(w_ref, g_ref, m_ref, v_ref, d_ref, nm_ref, nv_ref):
        gv = g_ref[...]
        nm = ADAM_B1 * m_ref[...] + (1.0 - ADAM_B1) * gv
        nv = ADAM_B2 * v_ref[...] + (1.0 - ADAM_B2) * (gv * gv)
        d_ref[...] = -ADAM_LR * ((nm / c1) / (jnp.sqrt(nv / c2) + ADAM_EPS) + ADAM_WD * w_ref[...])
        nm_ref[...] = nm
        nv_ref[...] = nv

    spec = pl.BlockSpec((tr, cols), lambda i: (i, 0))
    r2 = lambda a: a.reshape(rows, cols)
    outs = pl.pallas_call(
        body, name=name, grid=(rows // tr,), in_specs=[spec] * 4, out_specs=[spec] * 3,
        out_shape=[jax.ShapeDtypeStruct((rows, cols), F32)] * 3, compiler_params=_params("parallel"),
    )(r2(w), r2(g), r2(m), r2(v))
    return tuple(o.reshape(shape) for o in outs)


HBM_SPEC = pl.BlockSpec(memory_space=pl.ANY)


def _coords():
    return lax.axis_index("x"), lax.axis_index("y"), lax.axis_index("c")


def _other_chips(x, y):
    return [(1 - x, y), (x, 1 - y), (1 - x, 1 - y)]


def _remote(src, dst, ssem, rsem, dev):
    return pltpu.make_async_remote_copy(src_ref=src, dst_ref=dst, send_sem=ssem, recv_sem=rsem,
                                        device_id=dev, device_id_type=MESH)


def _gather_chips(w, *, name):
    rows, cols = w.shape
    rh = rows // 2
    align = 8 * 4 // w.dtype.itemsize
    assert rh * 2 == rows and rh % align == 0

    def body(w_ref, out_ref, send_sems, recv_sems):
        x, y, c = _coords()
        k = 2 * x + y
        sib = (x, y, 1 - c)
        half = pl.ds(pl.multiple_of(c * rh, align), rh)
        ohalf = pl.ds(pl.multiple_of((1 - c) * rh, align), rh)
        chips = _other_chips(x, y)
        sent = []
        for j, (px, py) in enumerate(chips):
            cp = _remote(w_ref.at[half], out_ref.at[k, half], send_sems.at[j], recv_sems.at[j], (px, py, c))
            cp.start()
            sent.append(cp)
        for j, (px, py) in enumerate(chips):
            blk = out_ref.at[2 * px + py, half]
            _remote(w_ref.at[half], blk, send_sems.at[j], recv_sems.at[j], (px, py, c)).wait_recv()
            fw = _remote(blk, blk, send_sems.at[3 + j], recv_sems.at[3 + j], sib)
            fw.start()
            sent.append(fw)
        for j, (px, py) in enumerate(chips):
            blk = out_ref.at[2 * px + py, ohalf]
            _remote(blk, blk, send_sems.at[3 + j], recv_sems.at[3 + j], sib).wait_recv()
        for cp in sent:
            cp.wait_send()

    g4 = pl.pallas_call(
        body, name=name, in_specs=[HBM_SPEC], out_specs=HBM_SPEC,
        out_shape=jax.ShapeDtypeStruct((4, rows, cols), w.dtype),
        scratch_shapes=[pltpu.SemaphoreType.DMA((6,)), pltpu.SemaphoreType.DMA((6,))],
    )(w)
    xi, yi, _ = _coords()
    return lax.dynamic_update_slice(g4, w[None], (2 * xi + yi, 0, 0))


def _swap_halves(gp, *, name):
    n, rows, cols = gp.shape
    rh = rows // 2

    def body(g_ref, out_ref, send_sems, recv_sems):
        x, y, c = _coords()
        sib = (x, y, 1 - c)
        ohalf = pl.ds(pl.multiple_of((1 - c) * rh, 8), rh)
        cps = [_remote(g_ref.at[s, ohalf], out_ref.at[s], send_sems.at[s], recv_sems.at[s], sib) for s in range(n)]
        for cp in cps:
            cp.start()
        for cp in cps:
            cp.wait_recv()
        for cp in cps:
            cp.wait_send()

    return pl.pallas_call(
        body, name=name, in_specs=[HBM_SPEC], out_specs=HBM_SPEC,
        out_shape=jax.ShapeDtypeStruct((n, rh, cols), gp.dtype),
        scratch_shapes=[pltpu.SemaphoreType.DMA((n,)), pltpu.SemaphoreType.DMA((n,))],
    )(gp)


def _add_half(gp, got, cidx, *, name):
    n, rows, cols = gp.shape
    rh = rows // 2
    tr = _pick(rh, 512, 16)
    nrb = rh // tr

    def body(c_ref, a_ref, b_ref, o_ref):
        o_ref[...] = (a_ref[...] + b_ref[...]).astype(BF16)

    grid_spec = pltpu.PrefetchScalarGridSpec(
        num_scalar_prefetch=1, grid=(n, nrb),
        in_specs=[pl.BlockSpec((1, tr, cols), lambda s, i, c_ref: (s, c_ref[0] * nrb + i, 0)),
                  pl.BlockSpec((1, tr, cols), lambda s, i, c_ref: (s, i, 0))],
        out_specs=pl.BlockSpec((1, tr, cols), lambda s, i, c_ref: (s, i, 0)))
    return pl.pallas_call(
        body, name=name, grid_spec=grid_spec, out_shape=jax.ShapeDtypeStruct((n, rh, cols), BF16),
        compiler_params=_params("parallel", "parallel"),
    )(cidx, gp, got)


def _scatter_chips(p, *, name):
    _, rh, cols = p.shape

    def body(p_ref, out_ref, send_sems, recv_sems):
        x, y, c = _coords()
        cps = []
        for j, (px, py) in enumerate(_other_chips(x, y)):
            cps.append(_remote(p_ref.at[2 * px + py], out_ref.at[j], send_sems.at[j], recv_sems.at[j], (px, py, c)))
        for cp in cps:
            cp.start()
        for cp in cps:
            cp.wait_recv()
        for cp in cps:
            cp.wait_send()

    return pl.pallas_call(
        body, name=name, in_specs=[HBM_SPEC], out_specs=HBM_SPEC,
        out_shape=jax.ShapeDtypeStruct((3, rh, cols), p.dtype),
        scratch_shapes=[pltpu.SemaphoreType.DMA((3,)), pltpu.SemaphoreType.DMA((3,))],
    )(p)


def _sum_arrivals(p, land, kidx, *, name):
    _, rh, cols = p.shape
    tr = _pick(rh, 512, 16)

    def body(k_ref, a_ref, l_ref, o_ref):
        f = lambda v: v.astype(F32)
        o_ref[...] = ((f(a_ref[0]) + f(l_ref[0])) + f(l_ref[1])) + f(l_ref[2])

    grid_spec = pltpu.PrefetchScalarGridSpec(
        num_scalar_prefetch=1, grid=(rh // tr,),
        in_specs=[pl.BlockSpec((1, tr, cols), lambda i, k_ref: (k_ref[0], i, 0)),
                  pl.BlockSpec((3, tr, cols), lambda i, k_ref: (0, i, 0))],
        out_specs=pl.BlockSpec((tr, cols), lambda i, k_ref: (i, 0)))
    return pl.pallas_call(
        body, name=name, grid_spec=grid_spec, out_shape=jax.ShapeDtypeStruct((rh, cols), F32),
        compiler_params=_params("parallel"),
    )(kidx, p, land)


def _join_halves(q, *, name):
    rh, cols = q.shape

    def body(q_ref, out_ref, send_sem, recv_sem):
        x, y, c = _coords()
        half = pl.ds(pl.multiple_of(c * rh, 8), rh)
        ohalf = pl.ds(pl.multiple_of((1 - c) * rh, 8), rh)
        cp = _remote(q_ref, out_ref.at[half], send_sem, recv_sem, (x, y, 1 - c))
        cp.start()
        _remote(q_ref, out_ref.at[ohalf], send_sem, recv_sem, (x, y, 1 - c)).wait_recv()
        cp.wait_send()

    full = pl.pallas_call(
        body, name=name, in_specs=[HBM_SPEC], out_specs=HBM_SPEC,
        out_shape=jax.ShapeDtypeStruct((2 * rh, cols), q.dtype),
        scratch_shapes=[pltpu.SemaphoreType.DMA, pltpu.SemaphoreType.DMA],
    )(q)
    return lax.dynamic_update_slice(full, q, (lax.axis_index("c") * rh, 0))


def _reduce_scatter(gp, cidx, kidx, *, tag):
    got = _swap_halves(gp, name=f"rs_swap_{tag}")
    p = _add_half(gp, got, cidx, name=f"rs_add_{tag}")
    land = _scatter_chips(p, name=f"rs_scatter_{tag}")
    q = _sum_arrivals(p, land, kidx, name=f"rs_sum_{tag}")
    return _join_halves(q, name=f"rs_join_{tag}")


def _allreduce_small(s, *, name):
    rows, cols = s.shape

    def body(s_ref, o_ref, buf, send_sems, recv_sems):
        x, y, c = _coords()
        me = 4 * x + 2 * y + c
        buf[me] = s_ref[...]
        cps = []
        for r in range(1, 8):
            peer = tuple((1 - v) if (r >> sh) & 1 else v for v, sh in ((x, 2), (y, 1), (c, 0)))
            cps.append(_remote(s_ref, buf.at[me], send_sems.at[r - 1], recv_sems.at[r - 1], peer))
        for cp in cps:
            cp.start()
        for cp in cps:
            cp.wait_recv()
        for cp in cps:
            cp.wait_send()
        acc = buf[0]
        for d in range(1, 8):
            acc = acc + buf[d]
        o_ref[...] = acc

    vm = pl.BlockSpec(memory_space=pltpu.VMEM)
    return pl.pallas_call(
        body, name=name, in_specs=[vm], out_specs=vm, out_shape=jax.ShapeDtypeStruct((rows, cols), F32),
        scratch_shapes=[pltpu.VMEM((8, rows, cols), F32), pltpu.SemaphoreType.DMA((7,)),
                        pltpu.SemaphoreType.DMA((7,))],
    )(s)


PACKED = ("ffn1_w_gu", "ffn1_w_down", "w_in", "w_uq", "w_ukv", "w_proj_attn", "w_proj_rec", "w_out",
          "ffn2_w_gu", "ffn2_w_down")
ROW_SHARDED = ("ffn1_w_down", "w_out", "ffn2_w_down")


def _pack_plan(shard_shapes):
    plan, off = {}, 0
    for n in PACKED:
        r, c = shard_shapes[n]
        assert (r * c) % PACK_W == 0
        plan[n] = (off, r * c // PACK_W, (r, c))
        off += r * c // PACK_W
    total = -(-off // 32) * 32
    return plan, total


def _pack(tensors, plan, total, dtype):
    parts = [tensors[n].astype(dtype).reshape(-1, PACK_W) for n in PACKED]
    used = sum(p.shape[0] for p in parts)
    if total > used:
        parts.append(jnp.zeros((total - used, PACK_W), dtype))
    return jnp.concatenate(parts, axis=0)


def _unpack_full(g4, plan):
    out = {}
    for n in PACKED:
        off, nr, (r, c) = plan[n]
        sh = g4[:, off:off + nr].reshape(4, r, c)
        out[n] = jnp.concatenate(list(sh), axis=0 if n in ROW_SHARDED else 1)
    return out


def _pack_grads(grads, plan, total):
    blocks = []
    for s in range(4):
        t = {}
        for n in PACKED:
            _, _, (r, c) = plan[n]
            t[n] = grads[n][s * r:(s + 1) * r] if n in ROW_SHARDED else grads[n][:, s * c:(s + 1) * c]
        blocks.append(_pack(t, plan, total, F32))
    return jnp.stack(blocks)


def _unpack_shard(p, plan):
    return {n: p[plan[n][0]:plan[n][0] + plan[n][1]].reshape(plan[n][2]) for n in PACKED}


def _swap_cols(w):
    hlf = w.shape[1] // 2
    return jnp.concatenate([-w[:, hlf:], w[:, :hlf]], axis=1)


def _unswap_cols(dw):
    hlf = dw.shape[1] // 2
    return jnp.concatenate([dw[:, hlf:], -dw[:, :hlf]], axis=1)


def _layer_weights(full, d):
    zl = _z_layout(d)
    f = full["ffn1_w_down"].shape[0]
    w_in = full["w_in"]
    o = 0
    cols = {}
    for nm, wd in (("cq", Q_LORA), ("ckv", KV_LORA), ("kpe", QK_ROPE), ("hq", 512), ("hf", 512), ("hi", 512),
                   ("hg", 512), ("ga", d), ("gb", d)):
        cols[nm] = w_in[:, o:o + wd]
        o += wd
    zc = lambda n: jnp.zeros((d, n), BF16)
    win_p = jnp.concatenate(
        [cols["cq"], zc(QK_NOPE), cols["kpe"], zc(32), cols["ckv"], zc(QK_NOPE), _swap_cols(cols["kpe"]), zc(32),
         zc(LANES), cols["ga"], cols["gb"], cols["hq"], cols["hf"], cols["hi"], cols["hg"]], axis=1)
    assert win_p.shape[1] == zl["total"]
    wq = full["w_uq"].reshape(Q_LORA, MLA_HEADS, QK_NOPE + QK_ROPE)
    nope, rope = wq[:, :, :QK_NOPE], wq[:, :, QK_NOPE:]
    z32 = jnp.zeros((Q_LORA, MLA_HEADS, 32), BF16)
    z64 = jnp.zeros((Q_LORA, MLA_HEADS, QK_NOPE), BF16)
    rope_sw = jnp.concatenate([-rope[:, :, 16:], rope[:, :, :16]], axis=2)
    wqa = jnp.concatenate([nope, rope, z32], axis=2).reshape(Q_LORA, QW)
    wqb = jnp.concatenate([z64, rope_sw, z32], axis=2).reshape(Q_LORA, QW)
    wpa = full["w_proj_attn"].reshape(MLA_HEADS, V_HEAD, d)
    wpa_p = jnp.concatenate([jnp.zeros_like(wpa), wpa], axis=1).reshape(QW, d)
    return dict(
        wg1=full["ffn1_w_gu"][:, :f], wu1=full["ffn1_w_gu"][:, f:], wd1=full["ffn1_w_down"],
        wg2=full["ffn2_w_gu"][:, :f], wu2=full["ffn2_w_gu"][:, f:], wd2=full["ffn2_w_down"],
        win=win_p, wq2=jnp.concatenate([wqa, wqb], axis=1), wqa=wqa, wqb=wqb, wkv=full["w_ukv"], wpa=wpa_p,
        wpr=full["w_proj_rec"], wout=full["w_out"])


def _natural_grads(g, d):
    zl = _z_layout(d)
    dwin = g["win"]
    kpe = dwin[:, Z_KPA + QK_NOPE:Z_KPA + QK_NOPE + QK_ROPE] + _unswap_cols(
        dwin[:, Z_KPB + QK_NOPE:Z_KPB + QK_NOPE + QK_ROPE])
    w_in = jnp.concatenate(
        [dwin[:, Z_Q:Z_Q + Q_LORA], dwin[:, Z_KV:Z_KV + KV_LORA], kpe, dwin[:, zl["hq"]:zl["hq"] + 2048],
         dwin[:, zl["ga"]:zl["ga"] + 2 * d]], axis=1)
    qa = g["wqa"].reshape(Q_LORA, MLA_HEADS, HEAD_W)
    qb = g["wqb"].reshape(Q_LORA, MLA_HEADS, HEAD_W)[:, :, QK_NOPE:QK_NOPE + QK_ROPE]
    rope = qa[:, :, QK_NOPE:QK_NOPE + QK_ROPE] + jnp.concatenate([qb[:, :, 16:], -qb[:, :, :16]], axis=2)
    w_uq = jnp.concatenate([qa[:, :, :QK_NOPE], rope], axis=2).reshape(Q_LORA, -1)
    wpa = g["wpa"].reshape(MLA_HEADS, 2 * V_HEAD, d)[:, V_HEAD:].reshape(MLA_HEADS * V_HEAD, d)
    return dict(
        ffn1_w_gu=jnp.concatenate([g["wg1"], g["wu1"]], axis=1), ffn1_w_down=g["wd1"],
        ffn2_w_gu=jnp.concatenate([g["wg2"], g["wu2"]], axis=1), ffn2_w_down=g["wd2"],
        w_in=w_in, w_uq=w_uq, w_ukv=g["wkv"], w_proj_attn=wpa, w_proj_rec=g["wpr"], w_out=g["wout"])


def _rope_tables(lp):
    pos = jnp.maximum(jnp.arange(lp) - FRONT, 0).astype(F32)
    half = QK_ROPE // 2
    inv = ROPE_THETA ** (-jnp.arange(half, dtype=F32) / half)
    ang = pos[:, None] * inv[None, :]
    cos, sin = jnp.cos(ang), jnp.sin(ang)
    cos_t = jnp.concatenate([jnp.ones((lp, QK_NOPE), F32), cos, cos, jnp.zeros((lp, 32), F32)], axis=1)
    sin_t = jnp.concatenate([jnp.zeros((lp, QK_NOPE), F32), sin, sin, jnp.zeros((lp, 32), F32)], axis=1)
    return cos_t, sin_t


def _lower_bounds(raw):
    p = jax.nn.softmax(raw.astype(F32), axis=0)
    return jnp.cumsum(p, axis=0) - p[0:1]


def _ffn_fwd(h, nw, wg, wu, wd, tag):
    a, a_t = _rmsnorm_fwd(h, nw, width=h.shape[1], col_block=0, transposed=True, name=f"norm_{tag}")
    g = _matmul([(a, wg)], "nn", name=f"gate_{tag}")
    u = _matmul([(a, wu)], "nn", name=f"up_{tag}")
    act, act_t = _swiglu_fwd(g, u, name=f"swiglu_{tag}")
    out = _matmul([(act, wd)], "nn", res=h, scale=0.5, name=f"down_{tag}")
    return out, dict(h=h, a_t=a_t, g=g, u=u, act_t=act_t)


def _ffn_bwd(dout, sv, nw, wg, wu, wd, lv, tag):
    dact = _matmul([(dout, wd)], "nt", scale=0.5, name=f"ddown_{tag}")
    dwd = _matmul([(sv["act_t"], dout)], "nn", scale=0.5, name=f"dwdown_{tag}")
    dg, du = _swiglu_bwd(dact, sv["g"], sv["u"], name=f"dswiglu_{tag}")
    dwg = _matmul([(sv["a_t"], dg)], "nn", name=f"dwgate_{tag}")
    dwu = _matmul([(sv["a_t"], du)], "nn", name=f"dwup_{tag}")
    da = _matmul([(dg, wg), (du, wu)], "nt", name=f"dnormed_{tag}")
    dh, dn = _rmsnorm_bwd(sv["h"], nw, da, width=da.shape[1], col_block=0, lv=lv, dres=dout, name=f"dnorm_{tag}")
    return dh, dn, dwg, dwu, dwd


def _layer_fwd(h0, lw, sm, lb, tabs, consts, lv, l):
    d = h0.shape[1]
    zl = _z_layout(d)
    cos_t, sin_t = tabs[:2]
    h1, s1 = _ffn_fwd(h0, sm["ffn1_norm"], lw["wg1"], lw["wu1"], lw["wd1"], f"ffn1_{l}")
    um, um_t = _rmsnorm_fwd(h1, sm["mix_norm"], width=d, col_block=0, transposed=True, name=f"norm_mix_{l}")
    z = _matmul([(um, lw["win"])], "nn", name=f"inproj_{l}")
    qn = _rmsnorm_fwd(z, sm["q_norm"], width=Q_LORA, col_block=Z_Q // Q_LORA, name=f"norm_q_{l}")
    kvn = _rmsnorm_fwd(z, sm["kv_norm"], width=KV_LORA, col_block=Z_KV // KV_LORA, name=f"norm_kv_{l}")
    q2 = _matmul([(qn, lw["wq2"])], "nn", name=f"uq_{l}")
    kv = _matmul([(kvn, lw["wkv"])], "nn", name=f"ukv_{l}")
    q, k, v = _qkv_prep_fwd(q2, kv, z, cos_t, sin_t, name=f"qkv_{l}")
    o, lse = _attn_fwd(q, k, v, tabs[2], lv=lv, name=f"attn_{l}")
    ya = _matmul([(o, lw["wpa"])], "nn", name=f"proj_attn_{l}")
    o_pre, ob, states = _hgrn_fwd(z, lb, sm["hg_norm"], consts, zl=zl, lv=lv, name=f"hgrn_{l}")
    yb = _matmul([(ob, lw["wpr"])], "nn", name=f"proj_rec_{l}")
    mg, mg_t = _merge_fwd(ya, yb, z, zl=zl, name=f"merge_{l}")
    h2 = _matmul([(mg, lw["wout"])], "nn", res=h1, name=f"out_{l}")
    h3, s2 = _ffn_fwd(h2, sm["ffn2_norm"], lw["wg2"], lw["wu2"], lw["wd2"], f"ffn2_{l}")
    saved = dict(s1=s1, s2=s2, h1=h1, um_t=um_t, z=z, qn=qn, kvn=kvn, q=q, k=k, v=v, o=o, lse=lse, ya=ya, yb=yb,
                 o_pre=o_pre, ob=ob, states=states, mg_t=mg_t)
    return h3, saved


def _layer_bwd(dh3, sv, lw, sm, lb, tabs, consts, lv, l):
    d = dh3.shape[1]
    lp = dh3.shape[0]
    zl = _z_layout(d)
    cos_t, sin_t = tabs[:2]
    z = sv["z"]
    g = {}
    sg = {}
    dh2, sg["ffn2_norm"], g["wg2"], g["wu2"], g["wd2"] = _ffn_bwd(
        dh3, sv["s2"], sm["ffn2_norm"], lw["wg2"], lw["wu2"], lw["wd2"], lv, f"ffn2_{l}")
    dmg = _matmul([(dh2, lw["wout"])], "nt", name=f"dmerged_{l}")
    g["wout"] = _matmul([(sv["mg_t"], dh2)], "nn", name=f"dwout_{l}")
    dya, dyb, dga, dgb = _merge_bwd(dmg, sv["ya"], sv["yb"], z, zl=zl, name=f"dmerge_{l}")
    doa = _matmul([(dya, lw["wpa"])], "nt", name=f"dattn_out_{l}")
    g["wpa"] = _matmul([(sv["o"], dya)], "tn", name=f"dwproj_attn_{l}")
    dob = _matmul([(dyb, lw["wpr"])], "nt", name=f"drec_out_{l}")
    g["wpr"] = _matmul([(sv["ob"], dyb)], "tn", name=f"dwproj_rec_{l}")
    dhq, dhf, dhi, dhg, dlb, sg["hg_norm"] = _hgrn_bwd(
        z, sv["o_pre"], dob, sv["states"], lb, sm["hg_norm"], consts, zl=zl, lv=lv, name=f"dhgrn_{l}")
    delta = _attn_delta(doa, sv["o"], name=f"attn_delta_{l}")
    dq, dk, dv = _attn_bwd(sv["q"], sv["k"], sv["v"], doa, sv["lse"], delta, tabs[2], name=f"dattn_{l}")
    dqa, dqb, dkv, dza, dzb = _qkv_prep_bwd(dq, dk, dv, cos_t, sin_t, name=f"dqkv_{l}")
    dqn = _matmul([(dqa, lw["wqa"]), (dqb, lw["wqb"])], "nt", name=f"dqn_{l}")
    g["wqa"] = _matmul([(sv["qn"], dqa)], "tn", name=f"dwqa_{l}")
    g["wqb"] = _matmul([(sv["qn"], dqb)], "tn", name=f"dwqb_{l}")
    dkvn = _matmul([(dkv, lw["wkv"])], "nt", name=f"dkvn_{l}")
    g["wkv"] = _matmul([(sv["kvn"], dkv)], "tn", name=f"dwkv_{l}")
    dzq, sg["q_norm"] = _rmsnorm_bwd(z, sm["q_norm"], dqn, width=Q_LORA, col_block=Z_Q // Q_LORA, lv=lv,
                                     name=f"dnorm_q_{l}")
    dzkv, sg["kv_norm"] = _rmsnorm_bwd(z, sm["kv_norm"], dkvn, width=KV_LORA, col_block=Z_KV // KV_LORA, lv=lv,
                                       name=f"dnorm_kv_{l}")
    dz = jnp.concatenate([dzq, dza, dzkv, dzb, jnp.zeros((lp, LANES), F32), dga, dgb, dhq, dhf, dhi, dhg],
                         axis=1).astype(BF16)
    dum = _matmul([(dz, lw["win"])], "nt", name=f"dmixed_{l}")
    g["win"] = _matmul([(sv["um_t"], dz)], "nn", name=f"dwin_{l}")
    dh1, sg["mix_norm"] = _rmsnorm_bwd(sv["h1"], sm["mix_norm"], dum, width=d, col_block=0, lv=lv, dres=dh2,
                                       name=f"dnorm_mix_{l}")
    dh0, sg["ffn1_norm"], g["wg1"], g["wu1"], g["wd1"] = _ffn_bwd(
        dh1, sv["s1"], sm["ffn1_norm"], lw["wg1"], lw["wu1"], lw["wd1"], lv, f"ffn1_{l}")
    return dh0, g, sg, dlb


WEIGHTS = ("meta_tokens", "ffn1_norm", "ffn1_w_gu", "ffn1_w_down", "mix_norm", "w_in", "q_norm", "kv_norm", "w_uq",
           "w_ukv", "hg_lb_raw", "hg_norm", "w_proj_attn", "w_proj_rec", "w_out", "ffn2_norm", "ffn2_w_gu",
           "ffn2_w_down", "final_norm")
SMALL = ("ffn1_norm", "mix_norm", "q_norm", "kv_norm", "hg_lb_raw", "hg_norm", "ffn2_norm")


def _small_rows(vals):
    pad = lambda a: jnp.pad(a, ((0, 0), (0, PACK_W - a.shape[1])))
    rows = [pad(vals[n]) for n in SMALL]
    rows.append(pad(vals["final_norm"][None, :]))
    rows.append(pad(vals["meta_tokens"]))
    rows.append(pad(vals["loss"].reshape(1, 1)))
    s = jnp.concatenate(rows, axis=0)
    return jnp.pad(s, ((0, -s.shape[0] % 8), (0, 0)))


def _small_unrows(s, d, widths):
    out, o = {}, 0
    for n in SMALL:
        out[n] = s[o:o + DEPTH, :widths[n]]
        o += DEPTH
    out["final_norm"] = s[o, :d]
    o += 1
    out["meta_tokens"] = s[o:o + N_META, :d]
    o += N_META
    out["loss"] = s[o, 0]
    return out


def _step(args):
    x = args["x"][0]
    seq, d = x.shape
    assert d <= PACK_W
    lv = ROW_X + seq
    lp = -(-lv // ROW_TILE) * ROW_TILE
    xi, yi, ci = _coords()
    kidx = (2 * xi + yi).astype(jnp.int32).reshape(1)
    cidx = ci.astype(jnp.int32).reshape(1)
    consts = _hgrn_consts()
    tabs = (*_rope_tables(lp), _attn_consts(lp))

    shard_shapes = {n: args[n].shape[1:] for n in PACKED}
    plan, total = _pack_plan(shard_shapes)
    lws = []
    for l in range(DEPTH):
        packed = _pack({n: args[n][l] for n in PACKED}, plan, total, BF16)
        g4 = _gather_chips(packed, name=f"gather_{l}")
        lws.append(_layer_weights(_unpack_full(g4, plan), d))
    mt = args["meta_tokens"]
    mt4 = _gather_chips(mt, name="gather_meta")
    meta = jnp.concatenate(list(mt4), axis=1)

    sm = [{n: args[n][l] for n in SMALL} for l in range(DEPTH)]
    lbs = _lower_bounds(args["hg_lb_raw"])

    h = jnp.concatenate([jnp.zeros((FRONT, d), F32), meta, x, jnp.zeros((lp - lv, d), F32)], axis=0)
    saved = []
    for l in range(DEPTH):
        h, sv = _layer_fwd(h, lws[l], sm[l], lbs[l], tabs, consts, lv, l)
        saved.append(sv)
    tpad = jnp.pad(args["loss_target"][0], ((ROW_X, lp - lv), (0, 0)))
    dh, loss, dfinal = _loss_head(h, args["final_norm"], tpad, lv=lv, name="loss_head")

    small = {n: [None] * DEPTH for n in SMALL}
    dlbs = [None] * DEPTH
    shard_grads = [None] * DEPTH
    for l in reversed(range(DEPTH)):
        dh, g, sg, dlbs[l] = _layer_bwd(dh, saved[l], lws[l], sm[l], lbs[l], tabs, consts, lv, l)
        for n in sg:
            small[n][l] = sg[n]
        gp = _pack_grads(_natural_grads(g, d), plan, total)
        shard_grads[l] = _unpack_shard(_reduce_scatter(gp, cidx, kidx, tag=str(l)), plan)

    _, lb_vjp = jax.vjp(_lower_bounds, args["hg_lb_raw"])
    small_vals = {n: jnp.stack(small[n]) for n in SMALL if n != "hg_lb_raw"}
    small_vals["hg_lb_raw"] = lb_vjp(jnp.stack(dlbs))[0]
    small_vals["final_norm"] = dfinal
    small_vals["meta_tokens"] = dh[FRONT:ROW_X]
    small_vals["loss"] = loss
    widths = {n: args[n].shape[1] for n in SMALL}
    tot = _small_unrows(_allreduce_small(_small_rows(small_vals), name="allreduce_small"), d, widths)

    grads = {n: jnp.stack([shard_grads[l][n] for l in range(DEPTH)]) for n in PACKED}
    for n in SMALL:
        grads[n] = tot[n]
    grads["final_norm"] = tot["final_norm"]
    mcols = mt.shape[1]
    grads["meta_tokens"] = lax.dynamic_slice_in_dim(tot["meta_tokens"], (2 * xi + yi) * mcols, mcols, axis=1)
    grad_x = dh[ROW_X:lv][None]

    delta, new_m, new_v = {}, {}, {}
    for n in WEIGHTS:
        delta[n], new_m[n], new_v[n] = _adamw(args[n], grads[n], args["m_" + n], args["v_" + n], name=f"adamw_{n}")
    return (tot["loss"], grad_x, *[grads[n] for n in WEIGHTS], *[delta[n] for n in WEIGHTS],
            *[new_m[n] for n in WEIGHTS], *[new_v[n] for n in WEIGHTS])


def kernel(x, meta_tokens, ffn1_norm, ffn1_w_gu, ffn1_w_down, mix_norm, w_in, q_norm, kv_norm, w_uq, w_ukv, hg_lb_raw, hg_norm, w_proj_attn, w_proj_rec, w_out, ffn2_norm, ffn2_w_gu, ffn2_w_down, final_norm, loss_target, m_meta_tokens, m_ffn1_norm, m_ffn1_w_gu, m_ffn1_w_down, m_mix_norm, m_w_in, m_q_norm, m_kv_norm, m_w_uq, m_w_ukv, m_hg_lb_raw, m_hg_norm, m_w_proj_attn, m_w_proj_rec, m_w_out, m_ffn2_norm, m_ffn2_w_gu, m_ffn2_w_down, m_final_norm, v_meta_tokens, v_ffn1_norm, v_ffn1_w_gu, v_ffn1_w_down, v_mix_norm, v_w_in, v_q_norm, v_kv_norm, v_w_uq, v_w_ukv, v_hg_lb_raw, v_hg_norm, v_w_proj_attn, v_w_proj_rec, v_w_out, v_ffn2_norm, v_ffn2_w_gu, v_ffn2_w_down, v_final_norm):
    return _step(dict(locals()))
```

```python
import functools
import math

import numpy as np
import jax
import jax.numpy as jnp
from jax import lax
from jax.experimental import pallas as pl
from jax.experimental.pallas import tpu as pltpu

F32 = jnp.float32
BF16 = jnp.bfloat16

N_META = 16
MLA_HEADS = 8
Q_LORA = 384
KV_LORA = 256
QK_NOPE = 64
QK_ROPE = 32
V_HEAD = 64
ROPE_THETA = 10000.0
HG_HEADS = 4
HG_D = 128
HG_CHUNK = 64
EPS = 1e-6
NEG_BIG = -1e30
F_MIN = 1e-20
DEPTH = 4

ADAM_LR = 0.001
ADAM_B1 = 0.9
ADAM_B2 = 0.999
ADAM_EPS = 1e-08
ADAM_WD = 0.01
ADAM_STEP = 10

LANES = 128
FRONT = (-N_META) % HG_CHUNK
ROW_X = FRONT + N_META
ROW_TILE = 640
HEAD_W = 128
QW = MLA_HEADS * HEAD_W
VMEM_LIMIT = 56 * 1024 * 1024
MATMUL_VMEM = 42 * 1024 * 1024
PACK_W = 1024
MESH = pl.DeviceIdType.MESH

Z_Q, Z_KPA, Z_KV, Z_KPB, Z_PAD, Z_GA = 0, 384, 512, 768, 896, 1024


def _z_layout(d):
    ga = Z_GA
    gb = ga + d
    hq = gb + d
    hf = hq + 512
    hi = hf + 512
    hg = hi + 512
    return dict(ga=ga, gb=gb, hq=hq, hf=hf, hi=hi, hg=hg, total=hg + 512)


def _pick(dim, cap, mult=LANES):
    if dim <= cap:
        return dim
    best = None
    for t in range(mult, cap + 1, mult):
        if dim % t == 0:
            best = t
    assert best is not None, (dim, cap, mult)
    return best


def _params(*sem):
    return pltpu.CompilerParams(dimension_semantics=sem, vmem_limit_bytes=VMEM_LIMIT)


def _sigmoid(x):
    return 1.0 / (1.0 + jnp.exp(-x))


def _row_valid(row0, n, lv):
    r = row0 + lax.broadcasted_iota(jnp.int32, (n, 1), 0)
    return ((r >= FRONT) & (r < lv)).astype(F32)


_DIMS = {"nn": (((1,), (0,)), ((), ())), "nt": (((1,), (1,)), ((), ())), "tn": (((0,), (0,)), ((), ()))}


def _matmul(pairs, mode, *, name, out_dtype=F32, res=None, scale=1.0):
    a0, b0 = pairs[0]
    if mode == "nn":
        (m, k), n = a0.shape, b0.shape[1]
    elif mode == "nt":
        (m, k), n = a0.shape, b0.shape[0]
    else:
        (k, m), n = a0.shape, b0.shape[1]
    if mode == "tn":
        tm, tn, tk = _pick(m, 1024), _pick(n, 1408), _pick(k, ROW_TILE, 8)
    else:
        if k > m:
            tm, tn, kcap = _pick(m, 1408, 16), _pick(n, 1408), 1664
        else:
            tm, tn, kcap = _pick(m, ROW_TILE, 8), _pick(n, 2816), 2816
        out_b = jnp.dtype(out_dtype).itemsize
        per_k = len(pairs) * 2 * (tm * a0.dtype.itemsize + tn * b0.dtype.itemsize)
        fixed = tm * tn * (2 * out_b + 4 + (8 if res is not None else 0))
        tk = _pick(k, kcap)
        while tk > LANES and fixed + per_k * tk > MATMUL_VMEM:
            tk = _pick(k, tk - LANES)
    nk = k // tk
    npair = len(pairs)
    dims = _DIMS[mode]

    def body(*refs):
        ins = refs[:2 * npair]
        pos = 2 * npair
        res_ref = None
        if res is not None:
            res_ref = refs[pos]
            pos += 1
        o_ref = refs[pos]
        kk = pl.program_id(2)

        part = None
        for p in range(npair):
            a = ins[2 * p][...].astype(BF16)
            b = ins[2 * p + 1][...].astype(BF16)
            d = lax.dot_general(a, b, dims, preferred_element_type=F32)
            part = d if part is None else part + d

        def finish(r):
            if scale != 1.0:
                r = r * scale
            if res_ref is not None:
                r = r + res_ref[...]
            o_ref[...] = r.astype(out_dtype)

        if nk == 1:
            finish(part)
            return
        acc = refs[pos + 1]

        @pl.when(kk == 0)
        def _():
            acc[...] = part

        @pl.when(kk > 0)
        def _():
            acc[...] += part

        @pl.when(kk == nk - 1)
        def _():
            finish(acc[...])

    if mode == "nn":
        a_spec = pl.BlockSpec((tm, tk), lambda i, j, q: (i, q))
        b_spec = pl.BlockSpec((tk, tn), lambda i, j, q: (q, j))
    elif mode == "nt":
        a_spec = pl.BlockSpec((tm, tk), lambda i, j, q: (i, q))
        b_spec = pl.BlockSpec((tn, tk), lambda i, j, q: (j, q))
    else:
        a_spec = pl.BlockSpec((tk, tm), lambda i, j, q: (q, i))
        b_spec = pl.BlockSpec((tk, tn), lambda i, j, q: (q, j))
    o_spec = pl.BlockSpec((tm, tn), lambda i, j, q: (i, j))
    in_specs, args = [], []
    for a, b in pairs:
        in_specs += [a_spec, b_spec]
        args += [a, b]
    if res is not None:
        in_specs.append(o_spec)
        args.append(res)
    return pl.pallas_call(
        body, name=name, grid=(m // tm, n // tn, nk), in_specs=in_specs, out_specs=o_spec,
        out_shape=jax.ShapeDtypeStruct((m, n), out_dtype),
        scratch_shapes=[pltpu.VMEM((tm, tn), F32)] if nk > 1 else [],
        compiler_params=_params("parallel", "parallel", "arbitrary"),
    )(*args)


def _rmsnorm_fwd(x, w, *, width, col_block, name, transposed=False):
    lp = x.shape[0]
    tm = _pick(lp, ROW_TILE, 8)

    def body(x_ref, w_ref, o_ref, *ot_ref):
        xv = x_ref[...]
        r = lax.rsqrt(jnp.mean(xv * xv, axis=-1, keepdims=True) + EPS)
        y = xv * r * w_ref[...]
        o_ref[...] = y.astype(BF16)
        if transposed:
            ot_ref[0][...] = y.T.astype(BF16)

    out_specs = [pl.BlockSpec((tm, width), lambda i: (i, 0))]
    out_shape = [jax.ShapeDtypeStruct((lp, width), BF16)]
    if transposed:
        out_specs.append(pl.BlockSpec((width, tm), lambda i: (0, i)))
        out_shape.append(jax.ShapeDtypeStruct((width, lp), BF16))
    outs = pl.pallas_call(
        body, name=name, grid=(lp // tm,),
        in_specs=[pl.BlockSpec((tm, width), lambda i: (i, col_block)), pl.BlockSpec((1, width), lambda i: (0, 0))],
        out_specs=out_specs, out_shape=out_shape, compiler_params=_params("parallel"),
    )(x, w.reshape(1, width))
    return tuple(outs) if transposed else outs[0]


def _rmsnorm_bwd(x, w, dy, *, width, col_block, lv, name, dres=None):
    lp = x.shape[0]
    tm = _pick(lp, ROW_TILE, 8)

    def body(*refs):
        if dres is None:
            x_ref, w_ref, dy_ref, dx_ref, dw_ref = refs
            dres_ref = None
        else:
            x_ref, w_ref, dy_ref, dres_ref, dx_ref, dw_ref = refs
        i = pl.program_id(0)
        xv = x_ref[...]
        dyv = dy_ref[...] * _row_valid(i * tm, tm, lv)
        r = lax.rsqrt(jnp.mean(xv * xv, axis=-1, keepdims=True) + EPS)
        wdy = dyv * w_ref[...]
        dx = r * wdy - xv * (r * r * r) * jnp.mean(xv * wdy, axis=-1, keepdims=True)
        if dres_ref is not None:
            dx = dx + dres_ref[...]
        dx_ref[...] = dx

        @pl.when(i == 0)
        def _():
            dw_ref[...] = jnp.zeros_like(dw_ref)

        dw_ref[...] += jnp.sum(dyv * xv * r, axis=0, keepdims=True)

    row = pl.BlockSpec((tm, width), lambda i: (i, 0))
    in_specs = [pl.BlockSpec((tm, width), lambda i: (i, col_block)), pl.BlockSpec((1, width), lambda i: (0, 0)), row]
    args = [x, w.reshape(1, width), dy]
    if dres is not None:
        in_specs.append(row)
        args.append(dres)
    dx, dw = pl.pallas_call(
        body, name=name, grid=(lp // tm,), in_specs=in_specs,
        out_specs=[row, pl.BlockSpec((1, width), lambda i: (0, 0))],
        out_shape=[jax.ShapeDtypeStruct((lp, width), F32), jax.ShapeDtypeStruct((1, width), F32)],
        compiler_params=_params("arbitrary"),
    )(*args)
    return dx, dw[0]


def _swiglu_fwd(g, u, *, name):
    lp, f = g.shape
    tm, tf = _pick(lp, ROW_TILE, 8), _pick(f, 1408)

    def body(g_ref, u_ref, o_ref, ot_ref):
        gv = g_ref[...]
        act = gv * _sigmoid(gv) * u_ref[...]
        o_ref[...] = act.astype(BF16)
        ot_ref[...] = act.T.astype(BF16)

    spec = pl.BlockSpec((tm, tf), lambda i, j: (i, j))
    return pl.pallas_call(
        body, name=name, grid=(lp // tm, f // tf), in_specs=[spec, spec],
        out_specs=[spec, pl.BlockSpec((tf, tm), lambda i, j: (j, i))],
        out_shape=[jax.ShapeDtypeStruct((lp, f), BF16), jax.ShapeDtypeStruct((f, lp), BF16)],
        compiler_params=_params("parallel", "parallel"),
    )(g, u)


def _swiglu_bwd(dact, g, u, *, name):
    lp, f = g.shape
    tm, tf = _pick(lp, ROW_TILE, 8), _pick(f, 1408)

    def body(d_ref, g_ref, u_ref, dg_ref, du_ref):
        gv, dv = g_ref[...], d_ref[...]
        s = _sigmoid(gv)
        dg_ref[...] = (dv * u_ref[...] * s * (1.0 + gv * (1.0 - s))).astype(BF16)
        du_ref[...] = (dv * gv * s).astype(BF16)

    spec = pl.BlockSpec((tm, tf), lambda i, j: (i, j))
    return pl.pallas_call(
        body, name=name, grid=(lp // tm, f // tf), in_specs=[spec, spec, spec], out_specs=[spec, spec],
        out_shape=[jax.ShapeDtypeStruct((lp, f), BF16)] * 2, compiler_params=_params("parallel", "parallel"),
    )(dact, g, u)


def _merge_fwd(ya, yb, z, *, zl, name):
    lp, d = ya.shape
    tm, td = _pick(lp, ROW_TILE, 8), _pick(d, 512)
    oa, ob = zl["ga"] // td, zl["gb"] // td

    def body(ya_ref, yb_ref, ga_ref, gb_ref, o_ref, ot_ref):
        mg = _sigmoid(ga_ref[...]) * ya_ref[...] + _sigmoid(gb_ref[...]) * yb_ref[...]
        o_ref[...] = mg.astype(BF16)
        ot_ref[...] = mg.T.astype(BF16)

    spec = pl.BlockSpec((tm, td), lambda i, j: (i, j))
    return pl.pallas_call(
        body, name=name, grid=(lp // tm, d // td),
        in_specs=[spec, spec, pl.BlockSpec((tm, td), lambda i, j: (i, oa + j)),
                  pl.BlockSpec((tm, td), lambda i, j: (i, ob + j))],
        out_specs=[spec, pl.BlockSpec((td, tm), lambda i, j: (j, i))],
        out_shape=[jax.ShapeDtypeStruct((lp, d), BF16), jax.ShapeDtypeStruct((d, lp), BF16)],
        compiler_params=_params("parallel", "parallel"),
    )(ya, yb, z, z)


def _merge_bwd(dmg, ya, yb, z, *, zl, name):
    lp, d = ya.shape
    tm, td = _pick(lp, ROW_TILE, 8), _pick(d, 512)
    oa, ob = zl["ga"] // td, zl["gb"] // td

    def body(d_ref, ya_ref, yb_ref, ga_ref, gb_ref, dya_ref, dyb_ref, dga_ref, dgb_ref):
        dv = d_ref[...]
        sa, sb = _sigmoid(ga_ref[...]), _sigmoid(gb_ref[...])
        dya_ref[...] = (dv * sa).astype(BF16)
        dyb_ref[...] = (dv * sb).astype(BF16)
        dga_ref[...] = dv * ya_ref[...] * sa * (1.0 - sa)
        dgb_ref[...] = dv * yb_ref[...] * sb * (1.0 - sb)

    spec = pl.BlockSpec((tm, td), lambda i, j: (i, j))
    return pl.pallas_call(
        body, name=name, grid=(lp // tm, d // td),
        in_specs=[spec, spec, spec, pl.BlockSpec((tm, td), lambda i, j: (i, oa + j)),
                  pl.BlockSpec((tm, td), lambda i, j: (i, ob + j))],
        out_specs=[spec] * 4,
        out_shape=[jax.ShapeDtypeStruct((lp, d), BF16)] * 2 + [jax.ShapeDtypeStruct((lp, d), F32)] * 2,
        compiler_params=_params("parallel", "parallel"),
    )(dmg, ya, yb, z, z)


def _qkv_prep_fwd(q2, kv, z, cos_t, sin_t, *, name):
    lp = q2.shape[0]
    tm = _pick(lp, ROW_TILE, 8)
    h = MLA_HEADS

    def body(qa_ref, qb_ref, kv_ref, za_ref, zb_ref, c_ref, s_ref, q_ref, k_ref, v_ref):
        c, s = c_ref[...], s_ref[...]
        lane = lax.broadcasted_iota(jnp.int32, (tm, HEAD_W), 1)
        q_ref[...] = ((qa_ref[...] * c + qb_ref[...] * s) * Q_SCALE).astype(BF16)
        kr = jnp.where(lane >= QK_NOPE, za_ref[...] * c + zb_ref[...] * s, 0.0)
        kvv = kv_ref[...]
        k_ref[...] = (jnp.where(lane < QK_NOPE, kvv, 0.0) + kr).astype(BF16)
        v_ref[...] = jnp.where(lane >= QK_NOPE, kvv, 0.0).astype(BF16)

    blk = lambda f: pl.BlockSpec((tm, HEAD_W), f)
    out = blk(lambda i, j: (i, j))
    return pl.pallas_call(
        body, name=name, grid=(lp // tm, h),
        in_specs=[blk(lambda i, j: (i, j)), blk(lambda i, j: (i, h + j)), blk(lambda i, j: (i, j)),
                  blk(lambda i, j: (i, Z_KPA // HEAD_W)), blk(lambda i, j: (i, Z_KPB // HEAD_W)),
                  blk(lambda i, j: (i, 0)), blk(lambda i, j: (i, 0))],
        out_specs=[out, out, out], out_shape=[jax.ShapeDtypeStruct((lp, QW), BF16)] * 3,
        compiler_params=_params("parallel", "parallel"),
    )(q2, q2, kv, z, z, cos_t, sin_t)


def _qkv_prep_bwd(dq, dk, dv, cos_t, sin_t, *, name):
    lp = dq.shape[0]
    tm = _pick(lp, ROW_TILE, 8)
    h = MLA_HEADS

    def body(dq_ref, dk_ref, dv_ref, c_ref, s_ref, dqa_ref, dqb_ref, dkv_ref, dza_ref, dzb_ref):
        j = pl.program_id(1)
        c, s = c_ref[...], s_ref[...]
        lane = lax.broadcasted_iota(jnp.int32, (tm, HEAD_W), 1)
        dqv, dkv_ = dq_ref[...], dk_ref[...]
        dqa_ref[...] = (dqv * c).astype(BF16)
        dqb_ref[...] = (dqv * s).astype(BF16)
        dkv_ref[...] = jnp.where(lane < QK_NOPE, dkv_, dv_ref[...]).astype(BF16)
        dkr = jnp.where(lane >= QK_NOPE, dkv_, 0.0)

        @pl.when(j == 0)
        def _():
            dza_ref[...] = jnp.zeros_like(dza_ref)
            dzb_ref[...] = jnp.zeros_like(dzb_ref)

        dza_ref[...] += dkr * c
        dzb_ref[...] += dkr * s

    blk = lambda f: pl.BlockSpec((tm, HEAD_W), f)
    per_head, shared = blk(lambda i, j: (i, j)), blk(lambda i, j: (i, 0))
    return pl.pallas_call(
        body, name=name, grid=(lp // tm, h),
        in_specs=[per_head, per_head, per_head, shared, shared],
        out_specs=[per_head, per_head, per_head, shared, shared],
        out_shape=[jax.ShapeDtypeStruct((lp, QW), BF16)] * 3 + [jax.ShapeDtypeStruct((lp, HEAD_W), F32)] * 2,
        compiler_params=_params("parallel", "arbitrary"),
    )(dq, dk, dv, cos_t, sin_t)


def _attn_tile(lp):
    return _pick(lp, ROW_TILE, LANES)


Q_SCALE = (QK_NOPE + QK_ROPE) ** -0.5 * math.log2(math.e)
ATT_HP = 2


def _attn_consts(lp):
    t = _attn_tile(lp)
    nb = lp // t
    r = np.arange(t)
    causal = np.where(r[None, :] <= r[:, None], 0.0, NEG_BIG).astype(np.float32)
    front = np.where(r >= FRONT, 0.0, NEG_BIG).astype(np.float32)[None, :]
    diag = np.stack([np.minimum(causal, front), causal])
    qmaj = [(i, j) for i in range(nb) for j in range(i + 1)]
    kmaj = [(i, j) for j in range(nb) for i in range(j, nb)]
    tab = lambda pairs, c: jnp.asarray([p[c] for p in pairs], jnp.int32)
    return dict(diag=jnp.asarray(diag), front=jnp.asarray(front),
                fwd=(tab(qmaj, 0), tab(qmaj, 1)), bwd=(tab(kmaj, 0), tab(kmaj, 1)))


def _attn_fwd(q, k, v, ac, *, lv, name):
    lp = q.shape[0]
    t = _attn_tile(lp)
    nb = lp // t
    rep = t // HEAD_W
    qtab, ktab = ac["fwd"]

    def body(qt_ref, kt_ref, q_ref, k_ref, v_ref, bd_ref, bf_ref, o_ref, lse_ref, m_s, l_s, acc_s):
        step_id = pl.program_id(1)
        qb, kb = qt_ref[step_id], kt_ref[step_id]

        @pl.when(kb == 0)
        def _():
            m_s[...] = jnp.full_like(m_s, NEG_BIG)
            l_s[...] = jnp.zeros_like(l_s)
            acc_s[...] = jnp.zeros_like(acc_s)

        def step(bias):
            b = None if bias is None else bias()
            for hh in range(ATT_HP):
                sl = slice(hh * HEAD_W, (hh + 1) * HEAD_W)
                s = lax.dot_general(q_ref[:, sl], k_ref[:, sl], _DIMS["nt"], preferred_element_type=F32)
                if b is not None:
                    s = s + b
                m_prev = m_s[:, sl]
                m_new = jnp.maximum(m_prev, jnp.max(s, axis=-1, keepdims=True))
                alpha = jnp.exp2(m_prev - m_new)
                p = jnp.exp2(s - jnp.tile(m_new, (1, rep)))
                l_s[:, sl] = alpha * l_s[:, sl] + jnp.sum(p, axis=-1, keepdims=True)
                acc_s[:, sl] = alpha * acc_s[:, sl] + jnp.dot(p.astype(BF16), v_ref[:, sl],
                                                              preferred_element_type=F32)
                m_s[:, sl] = m_new

        @pl.when((kb > 0) & (kb < qb))
        def _():
            step(None)

        @pl.when((kb == 0) & (qb > 0))
        def _():
            step(lambda: bf_ref[...])

        @pl.when(kb == qb)
        def _():
            step(lambda: bd_ref[0])
            l = l_s[...]
            o_ref[...] = acc_s[...] / l * _row_valid(qb * t, t, lv)
            lse_ref[...] = m_s[...] + jnp.log2(l)

    wd = ATT_HP * HEAD_W
    qs = pl.BlockSpec((t, wd), lambda h, s, qt, kt: (qt[s], h))
    ks = pl.BlockSpec((t, wd), lambda h, s, qt, kt: (kt[s], h))
    grid_spec = pltpu.PrefetchScalarGridSpec(
        num_scalar_prefetch=2, grid=(MLA_HEADS // ATT_HP, int(qtab.shape[0])),
        in_specs=[qs, ks, ks, pl.BlockSpec((1, t, t), lambda h, s, qt, kt: (jnp.minimum(qt[s], 1), 0, 0)),
                  pl.BlockSpec((1, t), lambda h, s, qt, kt: (0, 0))],
        out_specs=[qs, qs],
        scratch_shapes=[pltpu.VMEM((t, wd), F32), pltpu.VMEM((t, wd), F32), pltpu.VMEM((t, wd), F32)])
    return pl.pallas_call(
        body, name=name, grid_spec=grid_spec, out_shape=[jax.ShapeDtypeStruct((lp, QW), F32)] * 2,
        compiler_params=_params("parallel", "arbitrary"),
    )(qtab, ktab, q, k, v, ac["diag"], ac["front"])


def _attn_delta(do, o, *, name):
    lp = do.shape[0]
    tm = _pick(lp, ROW_TILE, 8)

    def body(do_ref, o_ref, d_ref):
        d_ref[...] = jnp.broadcast_to(jnp.sum(do_ref[...] * o_ref[...], axis=-1, keepdims=True), (tm, HEAD_W))

    spec = pl.BlockSpec((tm, HEAD_W), lambda i, j: (i, j))
    return pl.pallas_call(
        body, name=name, grid=(lp // tm, MLA_HEADS), in_specs=[spec, spec], out_specs=spec,
        out_shape=jax.ShapeDtypeStruct((lp, QW), F32), compiler_params=_params("parallel", "parallel"),
    )(do, o)


def _attn_bwd(q, k, v, do, lse, delta, ac, *, name):
    lp = q.shape[0]
    t = _attn_tile(lp)
    nb = lp // t
    rep = t // HEAD_W
    scale = (QK_NOPE + QK_ROPE) ** -0.5
    qtab, ktab = ac["bwd"]

    def body(qt_ref, kt_ref, q_ref, k_ref, v_ref, do_ref, lse_ref, dl_ref, bd_ref, bf_ref, dq_ref, dk_ref, dv_ref,
             dk_s, dv_s):
        step_id = pl.program_id(1)
        qb, kb = qt_ref[step_id], kt_ref[step_id]

        @pl.when(qb == kb)
        def _():
            dk_s[...] = jnp.zeros_like(dk_s)
            dv_s[...] = jnp.zeros_like(dv_s)

        def step(bias):
            b = None if bias is None else bias()
            rows = pl.ds(pl.multiple_of(qb * t, t), t)
            contribs = []
            for hh in range(ATT_HP):
                sl = slice(hh * HEAD_W, (hh + 1) * HEAD_W)
                qv, kv_, vv = q_ref[:, sl], k_ref[:, sl], v_ref[:, sl]
                dov = do_ref[:, sl].astype(BF16)
                s = lax.dot_general(qv, kv_, _DIMS["nt"], preferred_element_type=F32)
                if b is not None:
                    s = s + b
                p = jnp.exp2(s - jnp.tile(lse_ref[:, sl], (1, rep)))
                dv_s[:, sl] += lax.dot_general(p.astype(BF16), dov, _DIMS["tn"], preferred_element_type=F32)
                dp = lax.dot_general(dov, vv, _DIMS["nt"], preferred_element_type=F32)
                ds = (p * (dp - jnp.tile(dl_ref[:, sl], (1, rep))) * scale).astype(BF16)
                dk_s[:, sl] += lax.dot_general(ds, qv, _DIMS["tn"], preferred_element_type=F32)
                contribs.append(jnp.dot(ds, kv_, preferred_element_type=F32))
            contrib = jnp.concatenate(contribs, axis=1)

            @pl.when(kb == 0)
            def _():
                dq_ref[rows, :] = contrib

            @pl.when(kb > 0)
            def _():
                dq_ref[rows, :] += contrib

        @pl.when((kb > 0) & (kb < qb))
        def _():
            step(None)

        @pl.when((kb == 0) & (qb > 0))
        def _():
            step(lambda: bf_ref[...])

        @pl.when(kb == qb)
        def _():
            step(lambda: bd_ref[0])

        @pl.when(qb == nb - 1)
        def _():
            dk_ref[...] = dk_s[...] * (1.0 / Q_SCALE)
            dv_ref[...] = dv_s[...]

    wd = ATT_HP * HEAD_W
    qs = pl.BlockSpec((t, wd), lambda h, s, qt, kt: (qt[s], h))
    ks = pl.BlockSpec((t, wd), lambda h, s, qt, kt: (kt[s], h))
    dqs = pl.BlockSpec((lp, wd), lambda h, s, qt, kt: (0, h))
    grid_spec = pltpu.PrefetchScalarGridSpec(
        num_scalar_prefetch=2, grid=(MLA_HEADS // ATT_HP, int(qtab.shape[0])),
        in_specs=[qs, ks, ks, qs, qs, qs,
                  pl.BlockSpec((1, t, t), lambda h, s, qt, kt: (jnp.minimum(qt[s], 1), 0, 0)),
                  pl.BlockSpec((1, t), lambda h, s, qt, kt: (0, 0))],
        out_specs=[dqs, ks, ks],
        scratch_shapes=[pltpu.VMEM((t, wd), F32), pltpu.VMEM((t, wd), F32)])
    return pl.pallas_call(
        body, name=name, grid_spec=grid_spec, out_shape=[jax.ShapeDtypeStruct((lp, QW), F32)] * 3,
        compiler_params=_params("arbitrary", "arbitrary"),
    )(qtab, ktab, q, k, v, do, lse, delta, ac["diag"], ac["front"])


HG_LEVELS = (64, 32, 16, 8, 4, 2)
N_LEV = len(HG_LEVELS)


def _hgrn_consts():
    c = HG_CHUNK
    m = np.zeros((N_LEV + 2, c, c), np.float32)
    masks = np.zeros((N_LEV, c, c), np.float32)
    for li, p in enumerate(HG_LEVELS):
        for t in range(c):
            mid = (t // p) * p + p // 2
            if t >= mid:
                m[li, t, mid:t + 1] = 1.0
            else:
                m[li, t, t + 1:mid] = 1.0
            for s in range(c):
                if s // p == t // p and t >= mid and s < mid:
                    masks[li, t, s] = 1.0
    for t in range(c):
        m[N_LEV, t, :t + 1] = 1.0
        m[N_LEV + 1, t, t + 1:] = 1.0
    mall = m.reshape((N_LEV + 2) * c, c)
    return jnp.asarray(mall, BF16), jnp.asarray(mall.T.copy(), BF16), jnp.asarray(masks, F32)


def _split3(x):
    hi = x.astype(BF16)
    r = x - hi.astype(F32)
    mid = r.astype(BF16)
    lo = (r - mid.astype(F32)).astype(BF16)
    return jnp.concatenate([hi, mid, lo], axis=1)


def _sum3(e3):
    return e3[:, :HG_D] + e3[:, HG_D:2 * HG_D] + e3[:, 2 * HG_D:]


def _hgrn_chunk_fwd(hq, hf, hi, lb, valid, mall, masks, st):
    c = HG_CHUNK
    scale = HG_D ** -0.5
    sq = _sigmoid(hq)
    qv = hq * sq
    sg = _sigmoid(hf)
    f = lb + (1.0 - lb) * sg
    fc = jnp.maximum(f, F_MIN)
    lf = jnp.log(fc) * valid
    kv = (1.0 - lb) * (1.0 - sg) * valid
    e = _sum3(jnp.dot(mall, _split3(lf), preferred_element_type=F32))
    x = jnp.exp(e)
    a = jnp.zeros((c, c), F32)
    qe, ke = [], []
    for l in range(N_LEV):
        xl = x[l * c:(l + 1) * c]
        qe.append(qv * xl)
        ke.append(kv * xl)
        a = a + masks[l] * lax.dot_general(qe[l].astype(BF16), ke[l].astype(BF16), _DIMS["nt"],
                                           preferred_element_type=F32)
    row = lax.broadcasted_iota(jnp.int32, (c, c), 0)
    col = lax.broadcasted_iota(jnp.int32, (c, c), 1)
    a = a + jnp.where(row == col, jnp.sum(qv * kv, axis=-1, keepdims=True), 0.0)
    xb = x[N_LEV * c:(N_LEV + 1) * c]
    qb = qv * xb
    kb = kv * x[(N_LEV + 1) * c:]
    x_last = xb[c - 1:c]
    hib = hi.astype(BF16)
    o = scale * (jnp.dot(a.astype(BF16), hib, preferred_element_type=F32)
                 + lax.dot_general(qb.astype(BF16), st.astype(BF16), _DIMS["nt"], preferred_element_type=F32))
    st_new = st * x_last + lax.dot_general(hib, kb.astype(BF16), _DIMS["tn"], preferred_element_type=F32)
    saved = dict(sq=sq, qv=qv, sg=sg, f=f, fc=fc, kv=kv, x=x, a=a, qe=qe, ke=ke, qb=qb, kb=kb, x_last=x_last)
    return o, st_new, saved


def _hgrn_fwd(z, lb, nw, consts, *, zl, lv, name):
    lp = z.shape[0]
    tb = _pick(lp, ROW_TILE, HG_CHUNK)
    ncb = tb // HG_CHUNK
    nb = lp // tb
    mall, _, masks = consts
    w = HG_HEADS * HG_D

    def body(hq_ref, hf_ref, hi_ref, hg_ref, lb_ref, nw_ref, mall_ref, masks_ref, o_ref, ob_ref, st_ref, st_s):
        i = pl.program_id(0)

        @pl.when(i == 0)
        def _():
            st_s[...] = jnp.zeros_like(st_s)

        nwv = nw_ref[...]
        mallv, masksv = mall_ref[...], masks_ref[...]

        def chunk(cix, carry):
            r0 = pl.multiple_of(cix * HG_CHUNK, HG_CHUNK)
            rows = pl.ds(r0, HG_CHUNK)
            valid = _row_valid(i * tb + r0, HG_CHUNK, lv)
            for h in range(HG_HEADS):
                sl = slice(h * HG_D, (h + 1) * HG_D)
                st = st_s[h]
                st_ref[h, cix] = st
                o, st_new, _ = _hgrn_chunk_fwd(hq_ref[rows, sl], hf_ref[rows, sl], hi_ref[rows, sl], lb_ref[:, sl],
                                               valid, mallv, masksv, st)
                st_s[h] = st_new
                o_ref[rows, sl] = o
                hg = hg_ref[rows, sl]
                r = lax.rsqrt(jnp.mean(o * o, axis=-1, keepdims=True) + EPS)
                ob_ref[rows, sl] = (o * r * nwv * (hg * _sigmoid(hg))).astype(BF16)
            return carry

        lax.fori_loop(0, ncb, chunk, 0)

    zb = lambda off: pl.BlockSpec((tb, w), lambda i: (i, off // w))
    full = pl.BlockSpec((tb, w), lambda i: (i, 0))
    const = lambda shape: pl.BlockSpec(shape, lambda i: (0,) * len(shape))
    return pl.pallas_call(
        body, name=name, grid=(nb,),
        in_specs=[zb(zl["hq"]), zb(zl["hf"]), zb(zl["hi"]), zb(zl["hg"]), const((1, w)), const((1, HG_D)),
                  const(mall.shape), const(masks.shape)],
        out_specs=[full, full, pl.BlockSpec((HG_HEADS, ncb, HG_D, HG_D), lambda i: (0, i, 0, 0))],
        out_shape=[jax.ShapeDtypeStruct((lp, w), F32), jax.ShapeDtypeStruct((lp, w), BF16),
                   jax.ShapeDtypeStruct((HG_HEADS, lp // HG_CHUNK, HG_D, HG_D), F32)],
        scratch_shapes=[pltpu.VMEM((HG_HEADS, HG_D, HG_D), F32)],
        compiler_params=_params("arbitrary"),
    )(z, z, z, z, lb.reshape(1, w), nw.reshape(1, HG_D), mall, masks)


def _hgrn_chunk_bwd(hq, hf, hi, hg, o, dout, st, dst, lbv, nwv, valid, mallv, malltv, masksv):
    c = HG_CHUNK
    scale = HG_D ** -0.5
    _, _, sv = _hgrn_chunk_fwd(hq, hf, hi, lbv, valid, mallv, masksv, st)
    shg = _sigmoid(hg)
    r = lax.rsqrt(jnp.mean(o * o, axis=-1, keepdims=True) + EPS)
    don = dout * (hg * shg)
    dhg = dout * (o * r * nwv) * shg * (1.0 + hg * (1.0 - shg))
    dnw = jnp.sum(don * o * r, axis=0, keepdims=True)
    wd = don * nwv
    do = r * wd - o * (r * r * r) * jnp.mean(o * wd, axis=-1, keepdims=True)
    dob16, hib = do.astype(BF16), hi.astype(BF16)
    dst16 = dst.astype(BF16)
    da = scale * lax.dot_general(dob16, hib, _DIMS["nt"], preferred_element_type=F32)
    dv = (scale * lax.dot_general(sv["a"].astype(BF16), dob16, _DIMS["tn"], preferred_element_type=F32)
          + lax.dot_general(sv["kb"].astype(BF16), dst16, _DIMS["nt"], preferred_element_type=F32))
    dkb = jnp.dot(hib, dst16, preferred_element_type=F32)
    dqb = scale * jnp.dot(dob16, st.astype(BF16), preferred_element_type=F32)
    dst_new = dst * sv["x_last"] + scale * lax.dot_general(dob16, sv["qb"].astype(BF16), _DIMS["tn"],
                                                           preferred_element_type=F32)
    dxl = jnp.sum(dst * st, axis=0, keepdims=True)
    x = sv["x"]
    dq = dqb * x[N_LEV * c:(N_LEV + 1) * c]
    dk = dkb * x[(N_LEV + 1) * c:]
    de = []
    for l in range(N_LEV):
        dam = (masksv[l] * da).astype(BF16)
        dqe = jnp.dot(dam, sv["ke"][l].astype(BF16), preferred_element_type=F32)
        dke = lax.dot_general(dam, sv["qe"][l].astype(BF16), _DIMS["tn"], preferred_element_type=F32)
        xl = x[l * c:(l + 1) * c]
        dq = dq + dqe * xl
        dk = dk + dke * xl
        de.append(dqe * sv["qe"][l] + dke * sv["ke"][l])
    dd = scale * jnp.sum(do * hi, axis=-1, keepdims=True)
    dq = dq + dd * sv["kv"]
    dk = dk + dd * sv["qv"]
    last = lax.broadcasted_iota(jnp.int32, (c, 1), 0) == c - 1
    de.append(dqb * sv["qb"] + jnp.where(last, dxl * sv["x_last"], 0.0))
    de.append(dkb * sv["kb"])
    dlf = _sum3(jnp.dot(malltv, _split3(jnp.concatenate(de, axis=0)), preferred_element_type=F32))
    sg, sq = sv["sg"], sv["sq"]
    df = jnp.where(sv["f"] > F_MIN, dlf * valid / sv["fc"], 0.0)
    dkm = dk * valid
    dhf = (df - dkm) * (1.0 - lbv) * sg * (1.0 - sg)
    dlb = jnp.sum((df - dkm) * (1.0 - sg), axis=0, keepdims=True)
    dhq = dq * sq * (1.0 + hq * (1.0 - sq))
    return dhq, dhf, dv, dhg, dlb, dnw, dst_new


def _hgrn_bwd(z, o_pre, dob, states, lb, nw, consts, *, zl, lv, name):
    lp = z.shape[0]
    tb = _pick(lp, ROW_TILE, HG_CHUNK)
    ncb = tb // HG_CHUNK
    nb = lp // tb
    mall, mall_t, masks = consts
    w = HG_HEADS * HG_D
    c = HG_CHUNK

    def body(hq_ref, hf_ref, hi_ref, hg_ref, o_ref, dob_ref, st_ref, lb_ref, nw_ref, mall_ref, mallt_ref, masks_ref,
             dhq_ref, dhf_ref, dhi_ref, dhg_ref, dlb_ref, dnw_ref, dst_s):
        i = pl.program_id(0)
        blk = nb - 1 - i

        @pl.when(i == 0)
        def _():
            dst_s[...] = jnp.zeros_like(dst_s)
            dlb_ref[...] = jnp.zeros_like(dlb_ref)
            dnw_ref[...] = jnp.zeros_like(dnw_ref)

        nwv = nw_ref[...]
        mallv, malltv, masksv = mall_ref[...], mallt_ref[...], masks_ref[...]

        def chunk(jx, carry):
            cix = ncb - 1 - jx
            r0 = pl.multiple_of(cix * c, c)
            rows = pl.ds(r0, c)
            valid = _row_valid(blk * tb + r0, c, lv)
            for h in range(HG_HEADS):
                sl = slice(h * HG_D, (h + 1) * HG_D)
                dhq, dhf, dhi, dhg, dlb, dnw, dst_new = _hgrn_chunk_bwd(
                    hq_ref[rows, sl], hf_ref[rows, sl], hi_ref[rows, sl], hg_ref[rows, sl], o_ref[rows, sl],
                    dob_ref[rows, sl], st_ref[h, cix], dst_s[h], lb_ref[:, sl], nwv, valid, mallv, malltv, masksv)
                dst_s[h] = dst_new
                dhq_ref[rows, sl] = dhq
                dhf_ref[rows, sl] = dhf
                dhi_ref[rows, sl] = dhi
                dhg_ref[rows, sl] = dhg
                dlb_ref[:, sl] += dlb
                dnw_ref[...] += dnw
            return carry

        lax.fori_loop(0, ncb, chunk, 0)

    zb = lambda off: pl.BlockSpec((tb, w), lambda i: (nb - 1 - i, off // w))
    full = pl.BlockSpec((tb, w), lambda i: (nb - 1 - i, 0))
    const = lambda shape: pl.BlockSpec(shape, lambda i: (0,) * len(shape))
    outs = pl.pallas_call(
        body, name=name, grid=(nb,),
        in_specs=[zb(zl["hq"]), zb(zl["hf"]), zb(zl["hi"]), zb(zl["hg"]), full, full,
                  pl.BlockSpec((HG_HEADS, ncb, HG_D, HG_D), lambda i: (0, nb - 1 - i, 0, 0)),
                  const((1, w)), const((1, HG_D)), const(mall.shape), const(mall_t.shape), const(masks.shape)],
        out_specs=[full, full, full, full, const((1, w)), const((1, HG_D))],
        out_shape=[jax.ShapeDtypeStruct((lp, w), F32)] * 4
                  + [jax.ShapeDtypeStruct((1, w), F32), jax.ShapeDtypeStruct((1, HG_D), F32)],
        scratch_shapes=[pltpu.VMEM((HG_HEADS, HG_D, HG_D), F32)],
        compiler_params=_params("arbitrary"),
    )(z, z, z, z, o_pre, dob, states, lb.reshape(1, w), nw.reshape(1, HG_D), mall, mall_t, masks)
    dhq, dhf, dhi, dhg, dlb, dnw = outs
    return dhq, dhf, dhi, dhg, dlb[0], dnw[0]


def _loss_head(h, w, tpad, *, lv, name):
    lp, d = h.shape
    tm = _pick(lp, ROW_TILE, 8)

    def body(h_ref, w_ref, t_ref, dh_ref, loss_ref, dw_ref):
        i = pl.program_id(0)
        r0 = i * tm + lax.broadcasted_iota(jnp.int32, (tm, 1), 0)
        valid = ((r0 >= ROW_X) & (r0 < lv)).astype(F32)
        xv, wv = h_ref[...], w_ref[...]
        r = lax.rsqrt(jnp.mean(xv * xv, axis=-1, keepdims=True) + EPS)
        e = (xv * r * wv - t_ref[...]) * valid
        dy = e * (1.0 / d)
        wdy = dy * wv
        dh_ref[...] = r * wdy - xv * (r * r * r) * jnp.mean(xv * wdy, axis=-1, keepdims=True)

        @pl.when(i == 0)
        def _():
            loss_ref[...] = jnp.zeros_like(loss_ref)
            dw_ref[...] = jnp.zeros_like(dw_ref)

        loss_ref[...] += 0.5 * jnp.sum(jnp.mean(e * e, axis=-1, keepdims=True), axis=0, keepdims=True)
        dw_ref[...] += jnp.sum(dy * xv * r, axis=0, keepdims=True)

    row = pl.BlockSpec((tm, d), lambda i: (i, 0))
    vec = pl.BlockSpec((1, d), lambda i: (0, 0))
    dh, loss, dw = pl.pallas_call(
        body, name=name, grid=(lp // tm,), in_specs=[row, vec, row],
        out_specs=[row, pl.BlockSpec((8, LANES), lambda i: (0, 0)), vec],
        out_shape=[jax.ShapeDtypeStruct((lp, d), F32), jax.ShapeDtypeStruct((8, LANES), F32),
                   jax.ShapeDtypeStruct((1, d), F32)],
        compiler_params=_params("arbitrary"),
    )(h, w.reshape(1, d), tpad)
    return dh, loss[0, 0], dw[0]


def _adamw(w, g, m, v, *, name):
    shape = w.shape
    cols = shape[-1]
    rows = int(np.prod(shape[:-1])) if len(shape) > 1 else 1
    tr = _pick(rows, 256, 8)
    c1 = 1.0 - ADAM_B1 ** ADAM_STEP
    c2 = 1.0 - ADAM_B2 ** ADAM_STEP

    def body(w_ref, g_ref, m_ref, v_ref, d_ref, nm_ref, nv_ref):
        gv = g_ref[...]
        nm = ADAM_B1 * m_ref[...] + (1.0 - ADAM_B1) * gv
        nv = ADAM_B2 * v_ref[...] + (1.0 - ADAM_B2) * (gv * gv)
        d_ref[...] = -ADAM_LR * ((nm / c1) / (jnp.sqrt(nv / c2) + ADAM_EPS) + ADAM_WD * w_ref[...])
        nm_ref[...] = nm
        nv_ref[...] = nv

    spec = pl.BlockSpec((tr, cols), lambda i: (i, 0))
    r2 = lambda a: a.reshape(rows, cols)
    outs = pl.pallas_call(
        body, name=name, grid=(rows // tr,), in_specs=[spec] * 4, out_specs=[spec] * 3,
        out_shape=[jax.ShapeDtypeStruct((rows, cols), F32)] * 3, compiler_params=_params("parallel"),
    )(r2(w), r2(g), r2(m), r2(v))
    return tuple(o.reshape(shape) for o in outs)


HBM_SPEC = pl.BlockSpec(memory_space=pl.ANY)


def _coords():
    return lax.axis_index("x"), lax.axis_index("y"), lax.axis_index("c")


def _other_chips(x, y):
    return [(1 - x, y), (x, 1 - y), (1 - x, 1 - y)]


def _remote(src, dst, ssem, rsem, dev):
    return pltpu.make_async_remote_copy(src_ref=src, dst_ref=dst, send_sem=ssem, recv_sem=rsem,
                                        device_id=dev, device_id_type=MESH)


def _gather_chips(w, *, name):
    rows, cols = w.shape
    rh = rows // 2
    align = 8 * 4 // w.dtype.itemsize
    assert rh * 2 == rows and rh % align == 0

    def body(w_ref, out_ref, send_sems, recv_sems):
        x, y, c = _coords()
        k = 2 * x + y
        sib = (x, y, 1 - c)
        half = pl.ds(pl.multiple_of(c * rh, align), rh)
        ohalf = pl.ds(pl.multiple_of((1 - c) * rh, align), rh)
        chips = _other_chips(x, y)
        sent = []
        for j, (px, py) in enumerate(chips):
            cp = _remote(w_ref.at[half], out_ref.at[k, half], send_sems.at[j], recv_sems.at[j], (px, py, c))
            cp.start()
            sent.append(cp)
        for j, (px, py) in enumerate(chips):
            blk = out_ref.at[2 * px + py, half]
            _remote(w_ref.at[half], blk, send_sems.at[j], recv_sems.at[j], (px, py, c)).wait_recv()
            fw = _remote(blk, blk, send_sems.at[3 + j], recv_sems.at[3 + j], sib)
            fw.start()
            sent.append(fw)
        for j, (px, py) in enumerate(chips):
            blk = out_ref.at[2 * px + py, ohalf]
            _remote(blk, blk, send_sems.at[3 + j], recv_sems.at[3 + j], sib).wait_recv()
        for cp in sent:
            cp.wait_send()

    g4 = pl.pallas_call(
        body, name=name, in_specs=[HBM_SPEC], out_specs=HBM_SPEC,
        out_shape=jax.ShapeDtypeStruct((4, rows, cols), w.dtype),
        scratch_shapes=[pltpu.SemaphoreType.DMA((6,)), pltpu.SemaphoreType.DMA((6,))],
    )(w)
    xi, yi, _ = _coords()
    return lax.dynamic_update_slice(g4, w[None], (2 * xi + yi, 0, 0))


def _swap_halves(gp, *, name):
    n, rows, cols = gp.shape
    rh = rows // 2

    def body(g_ref, out_ref, send_sems, recv_sems):
        x, y, c = _coords()
        sib = (x, y, 1 - c)
        ohalf = pl.ds(pl.multiple_of((1 - c) * rh, 8), rh)
        cps = [_remote(g_ref.at[s, ohalf], out_ref.at[s], send_sems.at[s], recv_sems.at[s], sib) for s in range(n)]
        for cp in cps:
            cp.start()
        for cp in cps:
            cp.wait_recv()
        for cp in cps:
            cp.wait_send()

    return pl.pallas_call(
        body, name=name, in_specs=[HBM_SPEC], out_specs=HBM_SPEC,
        out_shape=jax.ShapeDtypeStruct((n, rh, cols), gp.dtype),
        scratch_shapes=[pltpu.SemaphoreType.DMA((n,)), pltpu.SemaphoreType.DMA((n,))],
    )(gp)


def _add_half(gp, got, cidx, *, name):
    n, rows, cols = gp.shape
    rh = rows // 2
    tr = _pick(rh, 512, 16)
    nrb = rh // tr

    def body(c_ref, a_ref, b_ref, o_ref):
        o_ref[...] = (a_ref[...] + b_ref[...]).astype(BF16)

    grid_spec = pltpu.PrefetchScalarGridSpec(
        num_scalar_prefetch=1, grid=(n, nrb),
        in_specs=[pl.BlockSpec((1, tr, cols), lambda s, i, c_ref: (s, c_ref[0] * nrb + i, 0)),
                  pl.BlockSpec((1, tr, cols), lambda s, i, c_ref: (s, i, 0))],
        out_specs=pl.BlockSpec((1, tr, cols), lambda s, i, c_ref: (s, i, 0)))
    return pl.pallas_call(
        body, name=name, grid_spec=grid_spec, out_shape=jax.ShapeDtypeStruct((n, rh, cols), BF16),
        compiler_params=_params("parallel", "parallel"),
    )(cidx, gp, got)


def _scatter_chips(p, *, name):
    _, rh, cols = p.shape

    def body(p_ref, out_ref, send_sems, recv_sems):
        x, y, c = _coords()
        cps = []
        for j, (px, py) in enumerate(_other_chips(x, y)):
            cps.append(_remote(p_ref.at[2 * px + py], out_ref.at[j], send_sems.at[j], recv_sems.at[j], (px, py, c)))
        for cp in cps:
            cp.start()
        for cp in cps:
            cp.wait_recv()
        for cp in cps:
            cp.wait_send()

    return pl.pallas_call(
        body, name=name, in_specs=[HBM_SPEC], out_specs=HBM_SPEC,
        out_shape=jax.ShapeDtypeStruct((3, rh, cols), p.dtype),
        scratch_shapes=[pltpu.SemaphoreType.DMA((3,)), pltpu.SemaphoreType.DMA((3,))],
    )(p)


def _sum_arrivals(p, land, kidx, *, name):
    _, rh, cols = p.shape
    tr = _pick(rh, 512, 16)

    def body(k_ref, a_ref, l_ref, o_ref):
        f = lambda v: v.astype(F32)
        o_ref[...] = ((f(a_ref[0]) + f(l_ref[0])) + f(l_ref[1])) + f(l_ref[2])

    grid_spec = pltpu.PrefetchScalarGridSpec(
        num_scalar_prefetch=1, grid=(rh // tr,),
        in_specs=[pl.BlockSpec((1, tr, cols), lambda i, k_ref: (k_ref[0], i, 0)),
                  pl.BlockSpec((3, tr, cols), lambda i, k_ref: (0, i, 0))],
        out_specs=pl.BlockSpec((tr, cols), lambda i, k_ref: (i, 0)))
    return pl.pallas_call(
        body, name=name, grid_spec=grid_spec, out_shape=jax.ShapeDtypeStruct((rh, cols), F32),
        compiler_params=_params("parallel"),
    )(kidx, p, land)


def _join_halves(q, *, name):
    rh, cols = q.shape

    def body(q_ref, out_ref, send_sem, recv_sem):
        x, y, c = _coords()
        half = pl.ds(pl.multiple_of(c * rh, 8), rh)
        ohalf = pl.ds(pl.multiple_of((1 - c) * rh, 8), rh)
        cp = _remote(q_ref, out_ref.at[half], send_sem, recv_sem, (x, y, 1 - c))
        cp.start()
        _remote(q_ref, out_ref.at[ohalf], send_sem, recv_sem, (x, y, 1 - c)).wait_recv()
        cp.wait_send()

    full = pl.pallas_call(
        body, name=name, in_specs=[HBM_SPEC], out_specs=HBM_SPEC,
        out_shape=jax.ShapeDtypeStruct((2 * rh, cols), q.dtype),
        scratch_shapes=[pltpu.SemaphoreType.DMA, pltpu.SemaphoreType.DMA],
    )(q)
    return lax.dynamic_update_slice(full, q, (lax.axis_index("c") * rh, 0))


def _reduce_scatter(gp, cidx, kidx, *, tag):
    got = _swap_halves(gp, name=f"rs_swap_{tag}")
    p = _add_half(gp, got, cidx, name=f"rs_add_{tag}")
    land = _scatter_chips(p, name=f"rs_scatter_{tag}")
    q = _sum_arrivals(p, land, kidx, name=f"rs_sum_{tag}")
    return _join_halves(q, name=f"rs_join_{tag}")


def _allreduce_small(s, *, name):
    rows, cols = s.shape

    def body(s_ref, o_ref, buf, send_sems, recv_sems):
        x, y, c = _coords()
        me = 4 * x + 2 * y + c
        buf[me] = s_ref[...]
        cps = []
        for r in range(1, 8):
            peer = tuple((1 - v) if (r >> sh) & 1 else v for v, sh in ((x, 2), (y, 1), (c, 0)))
            cps.append(_remote(s_ref, buf.at[me], send_sems.at[r - 1], recv_sems.at[r - 1], peer))
        for cp in cps:
            cp.start()
        for cp in cps:
            cp.wait_recv()
        for cp in cps:
            cp.wait_send()
        acc = buf[0]
        for d in range(1, 8):
            acc = acc + buf[d]
        o_ref[...] = acc

    vm = pl.BlockSpec(memory_space=pltpu.VMEM)
    return pl.pallas_call(
        body, name=name, in_specs=[vm], out_specs=vm, out_shape=jax.ShapeDtypeStruct((rows, cols), F32),
        scratch_shapes=[pltpu.VMEM((8, rows, cols), F32), pltpu.SemaphoreType.DMA((7,)),
                        pltpu.SemaphoreType.DMA((7,))],
    )(s)


PACKED = ("ffn1_w_gu", "ffn1_w_down", "w_in", "w_uq", "w_ukv", "w_proj_attn", "w_proj_rec", "w_out",
          "ffn2_w_gu", "ffn2_w_down")
ROW_SHARDED = ("ffn1_w_down", "w_out", "ffn2_w_down")


def _pack_plan(shard_shapes):
    plan, off = {}, 0
    for n in PACKED:
        r, c = shard_shapes[n]
        assert (r * c) % PACK_W == 0
        plan[n] = (off, r * c // PACK_W, (r, c))
        off += r * c // PACK_W
    total = -(-off // 32) * 32
    return plan, total


def _pack(tensors, plan, total, dtype):
    parts = [tensors[n].astype(dtype).reshape(-1, PACK_W) for n in PACKED]
    used = sum(p.shape[0] for p in parts)
    if total > used:
        parts.append(jnp.zeros((total - used, PACK_W), dtype))
    return jnp.concatenate(parts, axis=0)


def _unpack_full(g4, plan):
    out = {}
    for n in PACKED:
        off, nr, (r, c) = plan[n]
        sh = g4[:, off:off + nr].reshape(4, r, c)
        out[n] = jnp.concatenate(list(sh), axis=0 if n in ROW_SHARDED else 1)
    return out


def _pack_grads(grads, plan, total):
    blocks = []
    for s in range(4):
        t = {}
        for n in PACKED:
            _, _, (r, c) = plan[n]
            t[n] = grads[n][s * r:(s + 1) * r] if n in ROW_SHARDED else grads[n][:, s * c:(s + 1) * c]
        blocks.append(_pack(t, plan, total, F32))
    return jnp.stack(blocks)


def _unpack_shard(p, plan):
    return {n: p[plan[n][0]:plan[n][0] + plan[n][1]].reshape(plan[n][2]) for n in PACKED}


def _swap_cols(w):
    hlf = w.shape[1] // 2
    return jnp.concatenate([-w[:, hlf:], w[:, :hlf]], axis=1)


def _unswap_cols(dw):
    hlf = dw.shape[1] // 2
    return jnp.concatenate([dw[:, hlf:], -dw[:, :hlf]], axis=1)


def _layer_weights(full, d):
    zl = _z_layout(d)
    f = full["ffn1_w_down"].shape[0]
    w_in = full["w_in"]
    o = 0
    cols = {}
    for nm, wd in (("cq", Q_LORA), ("ckv", KV_LORA), ("kpe", QK_ROPE), ("hq", 512), ("hf", 512), ("hi", 512),
                   ("hg", 512), ("ga", d), ("gb", d)):
        cols[nm] = w_in[:, o:o + wd]
        o += wd
    zc = lambda n: jnp.zeros((d, n), BF16)
    win_p = jnp.concatenate(
        [cols["cq"], zc(QK_NOPE), cols["kpe"], zc(32), cols["ckv"], zc(QK_NOPE), _swap_cols(cols["kpe"]), zc(32),
         zc(LANES), cols["ga"], cols["gb"], cols["hq"], cols["hf"], cols["hi"], cols["hg"]], axis=1)
    assert win_p.shape[1] == zl["total"]
    wq = full["w_uq"].reshape(Q_LORA, MLA_HEADS, QK_NOPE + QK_ROPE)
    nope, rope = wq[:, :, :QK_NOPE], wq[:, :, QK_NOPE:]
    z32 = jnp.zeros((Q_LORA, MLA_HEADS, 32), BF16)
    z64 = jnp.zeros((Q_LORA, MLA_HEADS, QK_NOPE), BF16)
    rope_sw = jnp.concatenate([-rope[:, :, 16:], rope[:, :, :16]], axis=2)
    wqa = jnp.concatenate([nope, rope, z32], axis=2).reshape(Q_LORA, QW)
    wqb = jnp.concatenate([z64, rope_sw, z32], axis=2).reshape(Q_LORA, QW)
    wpa = full["w_proj_attn"].reshape(MLA_HEADS, V_HEAD, d)
    wpa_p = jnp.concatenate([jnp.zeros_like(wpa), wpa], axis=1).reshape(QW, d)
    return dict(
        wg1=full["ffn1_w_gu"][:, :f], wu1=full["ffn1_w_gu"][:, f:], wd1=full["ffn1_w_down"],
        wg2=full["ffn2_w_gu"][:, :f], wu2=full["ffn2_w_gu"][:, f:], wd2=full["ffn2_w_down"],
        win=win_p, wq2=jnp.concatenate([wqa, wqb], axis=1), wqa=wqa, wqb=wqb, wkv=full["w_ukv"], wpa=wpa_p,
        wpr=full["w_proj_rec"], wout=full["w_out"])


def _natural_grads(g, d):
    zl = _z_layout(d)
    dwin = g["win"]
    kpe = dwin[:, Z_KPA + QK_NOPE:Z_KPA + QK_NOPE + QK_ROPE] + _unswap_cols(
        dwin[:, Z_KPB + QK_NOPE:Z_KPB + QK_NOPE + QK_ROPE])
    w_in = jnp.concatenate(
        [dwin[:, Z_Q:Z_Q + Q_LORA], dwin[:, Z_KV:Z_KV + KV_LORA], kpe, dwin[:, zl["hq"]:zl["hq"] + 2048],
         dwin[:, zl["ga"]:zl["ga"] + 2 * d]], axis=1)
    qa = g["wqa"].reshape(Q_LORA, MLA_HEADS, HEAD_W)
    qb = g["wqb"].reshape(Q_LORA, MLA_HEADS, HEAD_W)[:, :, QK_NOPE:QK_NOPE + QK_ROPE]
    rope = qa[:, :, QK_NOPE:QK_NOPE + QK_ROPE] + jnp.concatenate([qb[:, :, 16:], -qb[:, :, :16]], axis=2)
    w_uq = jnp.concatenate([qa[:, :, :QK_NOPE], rope], axis=2).reshape(Q_LORA, -1)
    wpa = g["wpa"].reshape(MLA_HEADS, 2 * V_HEAD, d)[:, V_HEAD:].reshape(MLA_HEADS * V_HEAD, d)
    return dict(
        ffn1_w_gu=jnp.concatenate([g["wg1"], g["wu1"]], axis=1), ffn1_w_down=g["wd1"],
        ffn2_w_gu=jnp.concatenate([g["wg2"], g["wu2"]], axis=1), ffn2_w_down=g["wd2"],
        w_in=w_in, w_uq=w_uq, w_ukv=g["wkv"], w_proj_attn=wpa, w_proj_rec=g["wpr"], w_out=g["wout"])


def _rope_tables(lp):
    pos = jnp.maximum(jnp.arange(lp) - FRONT, 0).astype(F32)
    half = QK_ROPE // 2
    inv = ROPE_THETA ** (-jnp.arange(half, dtype=F32) / half)
    ang = pos[:, None] * inv[None, :]
    cos, sin = jnp.cos(ang), jnp.sin(ang)
    cos_t = jnp.concatenate([jnp.ones((lp, QK_NOPE), F32), cos, cos, jnp.zeros((lp, 32), F32)], axis=1)
    sin_t = jnp.concatenate([jnp.zeros((lp, QK_NOPE), F32), sin, sin, jnp.zeros((lp, 32), F32)], axis=1)
    return cos_t, sin_t


def _lower_bounds(raw):
    p = jax.nn.softmax(raw.astype(F32), axis=0)
    return jnp.cumsum(p, axis=0) - p[0:1]


def _ffn_fwd(h, nw, wg, wu, wd, tag):
    a, a_t = _rmsnorm_fwd(h, nw, width=h.shape[1], col_block=0, transposed=True, name=f"norm_{tag}")
    g = _matmul([(a, wg)], "nn", name=f"gate_{tag}")
    u = _matmul([(a, wu)], "nn", name=f"up_{tag}")
    act, act_t = _swiglu_fwd(g, u, name=f"swiglu_{tag}")
    out = _matmul([(act, wd)], "nn", res=h, scale=0.5, name=f"down_{tag}")
    return out, dict(h=h, a_t=a_t, g=g, u=u, act_t=act_t)


def _ffn_bwd(dout, sv, nw, wg, wu, wd, lv, tag):
    dact = _matmul([(dout, wd)], "nt", scale=0.5, name=f"ddown_{tag}")
    dwd = _matmul([(sv["act_t"], dout)], "nn", scale=0.5, name=f"dwdown_{tag}")
    dg, du = _swiglu_bwd(dact, sv["g"], sv["u"], name=f"dswiglu_{tag}")
    dwg = _matmul([(sv["a_t"], dg)], "nn", name=f"dwgate_{tag}")
    dwu = _matmul([(sv["a_t"], du)], "nn", name=f"dwup_{tag}")
    da = _matmul([(dg, wg), (du, wu)], "nt", name=f"dnormed_{tag}")
    dh, dn = _rmsnorm_bwd(sv["h"], nw, da, width=da.shape[1], col_block=0, lv=lv, dres=dout, name=f"dnorm_{tag}")
    return dh, dn, dwg, dwu, dwd


def _layer_fwd(h0, lw, sm, lb, tabs, consts, lv, l):
    d = h0.shape[1]
    zl = _z_layout(d)
    cos_t, sin_t = tabs[:2]
    h1, s1 = _ffn_fwd(h0, sm["ffn1_norm"], lw["wg1"], lw["wu1"], lw["wd1"], f"ffn1_{l}")
    um, um_t = _rmsnorm_fwd(h1, sm["mix_norm"], width=d, col_block=0, transposed=True, name=f"norm_mix_{l}")
    z = _matmul([(um, lw["win"])], "nn", name=f"inproj_{l}")
    qn = _rmsnorm_fwd(z, sm["q_norm"], width=Q_LORA, col_block=Z_Q // Q_LORA, name=f"norm_q_{l}")
    kvn = _rmsnorm_fwd(z, sm["kv_norm"], width=KV_LORA, col_block=Z_KV // KV_LORA, name=f"norm_kv_{l}")
    q2 = _matmul([(qn, lw["wq2"])], "nn", name=f"uq_{l}")
    kv = _matmul([(kvn, lw["wkv"])], "nn", name=f"ukv_{l}")
    q, k, v = _qkv_prep_fwd(q2, kv, z, cos_t, sin_t, name=f"qkv_{l}")
    o, lse = _attn_fwd(q, k, v, tabs[2], lv=lv, name=f"attn_{l}")
    ya = _matmul([(o, lw["wpa"])], "nn", name=f"proj_attn_{l}")
    o_pre, ob, states = _hgrn_fwd(z, lb, sm["hg_norm"], consts, zl=zl, lv=lv, name=f"hgrn_{l}")
    yb = _matmul([(ob, lw["wpr"])], "nn", name=f"proj_rec_{l}")
    mg, mg_t = _merge_fwd(ya, yb, z, zl=zl, name=f"merge_{l}")
    h2 = _matmul([(mg, lw["wout"])], "nn", res=h1, name=f"out_{l}")
    h3, s2 = _ffn_fwd(h2, sm["ffn2_norm"], lw["wg2"], lw["wu2"], lw["wd2"], f"ffn2_{l}")
    saved = dict(s1=s1, s2=s2, h1=h1, um_t=um_t, z=z, qn=qn, kvn=kvn, q=q, k=k, v=v, o=o, lse=lse, ya=ya, yb=yb,
                 o_pre=o_pre, ob=ob, states=states, mg_t=mg_t)
    return h3, saved


def _layer_bwd(dh3, sv, lw, sm, lb, tabs, consts, lv, l):
    d = dh3.shape[1]
    lp = dh3.shape[0]
    zl = _z_layout(d)
    cos_t, sin_t = tabs[:2]
    z = sv["z"]
    g = {}
    sg = {}
    dh2, sg["ffn2_norm"], g["wg2"], g["wu2"], g["wd2"] = _ffn_bwd(
        dh3, sv["s2"], sm["ffn2_norm"], lw["wg2"], lw["wu2"], lw["wd2"], lv, f"ffn2_{l}")
    dmg = _matmul([(dh2, lw["wout"])], "nt", name=f"dmerged_{l}")
    g["wout"] = _matmul([(sv["mg_t"], dh2)], "nn", name=f"dwout_{l}")
    dya, dyb, dga, dgb = _merge_bwd(dmg, sv["ya"], sv["yb"], z, zl=zl, name=f"dmerge_{l}")
    doa = _matmul([(dya, lw["wpa"])], "nt", name=f"dattn_out_{l}")
    g["wpa"] = _matmul([(sv["o"], dya)], "tn", name=f"dwproj_attn_{l}")
    dob = _matmul([(dyb, lw["wpr"])], "nt", name=f"drec_out_{l}")
    g["wpr"] = _matmul([(sv["ob"], dyb)], "tn", name=f"dwproj_rec_{l}")
    dhq, dhf, dhi, dhg, dlb, sg["hg_norm"] = _hgrn_bwd(
        z, sv["o_pre"], dob, sv["states"], lb, sm["hg_norm"], consts, zl=zl, lv=lv, name=f"dhgrn_{l}")
    delta = _attn_delta(doa, sv["o"], name=f"attn_delta_{l}")
    dq, dk, dv = _attn_bwd(sv["q"], sv["k"], sv["v"], doa, sv["lse"], delta, tabs[2], name=f"dattn_{l}")
    dqa, dqb, dkv, dza, dzb = _qkv_prep_bwd(dq, dk, dv, cos_t, sin_t, name=f"dqkv_{l}")
    dqn = _matmul([(dqa, lw["wqa"]), (dqb, lw["wqb"])], "nt", name=f"dqn_{l}")
    g["wqa"] = _matmul([(sv["qn"], dqa)], "tn", name=f"dwqa_{l}")
    g["wqb"] = _matmul([(sv["qn"], dqb)], "tn", name=f"dwqb_{l}")
    dkvn = _matmul([(dkv, lw["wkv"])], "nt", name=f"dkvn_{l}")
    g["wkv"] = _matmul([(sv["kvn"], dkv)], "tn", name=f"dwkv_{l}")
    dzq, sg["q_norm"] = _rmsnorm_bwd(z, sm["q_norm"], dqn, width=Q_LORA, col_block=Z_Q // Q_LORA, lv=lv,
                                     name=f"dnorm_q_{l}")
    dzkv, sg["kv_norm"] = _rmsnorm_bwd(z, sm["kv_norm"], dkvn, width=KV_LORA, col_block=Z_KV // KV_LORA, lv=lv,
                                       name=f"dnorm_kv_{l}")
    dz = jnp.concatenate([dzq, dza, dzkv, dzb, jnp.zeros((lp, LANES), F32), dga, dgb, dhq, dhf, dhi, dhg],
                         axis=1).astype(BF16)
    dum = _matmul([(dz, lw["win"])], "nt", name=f"dmixed_{l}")
    g["win"] = _matmul([(sv["um_t"], dz)], "nn", name=f"dwin_{l}")
    dh1, sg["mix_norm"] = _rmsnorm_bwd(sv["h1"], sm["mix_norm"], dum, width=d, col_block=0, lv=lv, dres=dh2,
                                       name=f"dnorm_mix_{l}")
    dh0, sg["ffn1_norm"], g["wg1"], g["wu1"], g["wd1"] = _ffn_bwd(
        dh1, sv["s1"], sm["ffn1_norm"], lw["wg1"], lw["wu1"], lw["wd1"], lv, f"ffn1_{l}")
    return dh0, g, sg, dlb


WEIGHTS = ("meta_tokens", "ffn1_norm", "ffn1_w_gu", "ffn1_w_down", "mix_norm", "w_in", "q_norm", "kv_norm", "w_uq",
           "w_ukv", "hg_lb_raw", "hg_norm", "w_proj_attn", "w_proj_rec", "w_out", "ffn2_norm", "ffn2_w_gu",
           "ffn2_w_down", "final_norm")
SMALL = ("ffn1_norm", "mix_norm", "q_norm", "kv_norm", "hg_lb_raw", "hg_norm", "ffn2_norm")


def _small_rows(vals):
    pad = lambda a: jnp.pad(a, ((0, 0), (0, PACK_W - a.shape[1])))
    rows = [pad(vals[n]) for n in SMALL]
    rows.append(pad(vals["final_norm"][None, :]))
    rows.append(pad(vals["meta_tokens"]))
    rows.append(pad(vals["loss"].reshape(1, 1)))
    s = jnp.concatenate(rows, axis=0)
    return jnp.pad(s, ((0, -s.shape[0] % 8), (0, 0)))


def _small_unrows(s, d, widths):
    out, o = {}, 0
    for n in SMALL:
        out[n] = s[o:o + DEPTH, :widths[n]]
        o += DEPTH
    out["final_norm"] = s[o, :d]
    o += 1
    out["meta_tokens"] = s[o:o + N_META, :d]
    o += N_META
    out["loss"] = s[o, 0]
    return out


def _step(args):
    x = args["x"][0]
    seq, d = x.shape
    assert d <= PACK_W
    lv = ROW_X + seq
    lp = -(-lv // ROW_TILE) * ROW_TILE
    xi, yi, ci = _coords()
    kidx = (2 * xi + yi).astype(jnp.int32).reshape(1)
    cidx = ci.astype(jnp.int32).reshape(1)
    consts = _hgrn_consts()
    tabs = (*_rope_tables(lp), _attn_consts(lp))

    shard_shapes = {n: args[n].shape[1:] for n in PACKED}
    plan, total = _pack_plan(shard_shapes)
    lws = []
    for l in range(DEPTH):
        packed = _pack({n: args[n][l] for n in PACKED}, plan, total, BF16)
        g4 = _gather_chips(packed, name=f"gather_{l}")
        lws.append(_layer_weights(_unpack_full(g4, plan), d))
    mt = args["meta_tokens"]
    mt4 = _gather_chips(mt, name="gather_meta")
    meta = jnp.concatenate(list(mt4), axis=1)

    sm = [{n: args[n][l] for n in SMALL} for l in range(DEPTH)]
    lbs = _lower_bounds(args["hg_lb_raw"])

    h = jnp.concatenate([jnp.zeros((FRONT, d), F32), meta, x, jnp.zeros((lp - lv, d), F32)], axis=0)
    saved = []
    for l in range(DEPTH):
        h, sv = _layer_fwd(h, lws[l], sm[l], lbs[l], tabs, consts, lv, l)
        saved.append(sv)
    tpad = jnp.pad(args["loss_target"][0], ((ROW_X, lp - lv), (0, 0)))
    dh, loss, dfinal = _loss_head(h, args["final_norm"], tpad, lv=lv, name="loss_head")

    small = {n: [None] * DEPTH for n in SMALL}
    dlbs = [None] * DEPTH
    shard_grads = [None] * DEPTH
    for l in reversed(range(DEPTH)):
        dh, g, sg, dlbs[l] = _layer_bwd(dh, saved[l], lws[l], sm[l], lbs[l], tabs, consts, lv, l)
        for n in sg:
            small[n][l] = sg[n]
        gp = _pack_grads(_natural_grads(g, d), plan, total)
        shard_grads[l] = _unpack_shard(_reduce_scatter(gp, cidx, kidx, tag=str(l)), plan)

    _, lb_vjp = jax.vjp(_lower_bounds, args["hg_lb_raw"])
    small_vals = {n: jnp.stack(small[n]) for n in SMALL if n != "hg_lb_raw"}
    small_vals["hg_lb_raw"] = lb_vjp(jnp.stack(dlbs))[0]
    small_vals["final_norm"] = dfinal
    small_vals["meta_tokens"] = dh[FRONT:ROW_X]
    small_vals["loss"] = loss
    widths = {n: args[n].shape[1] for n in SMALL}
    tot = _small_unrows(_allreduce_small(_small_rows(small_vals), name="allreduce_small"), d, widths)

    grads = {n: jnp.stack([shard_grads[l][n] for l in range(DEPTH)]) for n in PACKED}
    for n in SMALL:
        grads[n] = tot[n]
    grads["final_norm"] = tot["final_norm"]
    mcols = mt.shape[1]
    grads["meta_tokens"] = lax.dynamic_slice_in_dim(tot["meta_tokens"], (2 * xi + yi) * mcols, mcols, axis=1)
    grad_x = dh[ROW_X:lv][None]

    delta, new_m, new_v = {}, {}, {}
    for n in WEIGHTS:
        delta[n], new_m[n], new_v[n] = _adamw(args[n], grads[n], args["m_" + n], args["v_" + n], name=f"adamw_{n}")
    return (tot["loss"], grad_x, *[grads[n] for n in WEIGHTS], *[delta[n] for n in WEIGHTS],
            *[new_m[n] for n in WEIGHTS], *[new_v[n] for n in WEIGHTS])


def kernel(x, meta_tokens, ffn1_norm, ffn1_w_gu, ffn1_w_down, mix_norm, w_in, q_norm, kv_norm, w_uq, w_ukv, hg_lb_raw, hg_norm, w_proj_attn, w_proj_rec, w_out, ffn2_norm, ffn2_w_gu, ffn2_w_down, final_norm, loss_target, m_meta_tokens, m_ffn1_norm, m_ffn1_w_gu, m_ffn1_w_down, m_mix_norm, m_w_in, m_q_norm, m_kv_norm, m_w_uq, m_w_ukv, m_hg_lb_raw, m_hg_norm, m_w_proj_attn, m_w_proj_rec, m_w_out, m_ffn2_norm, m_ffn2_w_gu, m_ffn2_w_down, m_final_norm, v_meta_tokens, v_ffn1_norm, v_ffn1_w_gu, v_ffn1_w_down, v_mix_norm, v_w_in, v_q_norm, v_kv_norm, v_w_uq, v_w_ukv, v_hg_lb_raw, v_hg_norm, v_w_proj_attn, v_w_proj_rec, v_w_out, v_ffn2_norm, v_ffn2_w_gu, v_ffn2_w_down, v_final_norm):
    return _step(dict(locals()))
```

```python
import functools
import math

import numpy as np
import jax
import jax.numpy as jnp
from jax import lax
from jax.experimental import pallas as pl
from jax.experimental.pallas import tpu as pltpu

F32 = jnp.float32
BF16 = jnp.bfloat16

N_META = 16
MLA_HEADS = 8
Q_LORA = 384
KV_LORA = 256
QK_NOPE = 64
QK_ROPE = 32
V_HEAD = 64
ROPE_THETA = 10000.0
HG_HEADS = 4
HG_D = 128
HG_CHUNK = 64
EPS = 1e-6
NEG_BIG = -1e30
F_MIN = 1e-20
DEPTH = 4

ADAM_LR = 0.001
ADAM_B1 = 0.9
ADAM_B2 = 0.999
ADAM_EPS = 1e-08
ADAM_WD = 0.01
ADAM_STEP = 10

LANES = 128
FRONT = (-N_META) % HG_CHUNK
ROW_X = FRONT + N_META
ROW_TILE = 640
HEAD_W = 128
QW = MLA_HEADS * HEAD_W
VMEM_LIMIT = 56 * 1024 * 1024
MATMUL_VMEM = 42 * 1024 * 1024
PACK_W = 1024
MESH = pl.DeviceIdType.MESH

Z_Q, Z_KPA, Z_KV, Z_KPB, Z_PAD, Z_GA = 0, 384, 512, 768, 896, 1024


def _z_layout(d):
    ga = Z_GA
    gb = ga + d
    hq = gb + d
    hf = hq + 512
    hi = hf + 512
    hg = hi + 512
    return dict(ga=ga, gb=gb, hq=hq, hf=hf, hi=hi, hg=hg, total=hg + 512)


def _pick(dim, cap, mult=LANES):
    if dim <= cap:
        return dim
    best = None
    for t in range(mult, cap + 1, mult):
        if dim % t == 0:
            best = t
    assert best is not None, (dim, cap, mult)
    return best


def _params(*sem):
    return pltpu.CompilerParams(dimension_semantics=sem, vmem_limit_bytes=VMEM_LIMIT)


def _sigmoid(x):
    return 1.0 / (1.0 + jnp.exp(-x))


def _row_valid(row0, n, lv):
    r = row0 + lax.broadcasted_iota(jnp.int32, (n, 1), 0)
    return ((r >= FRONT) & (r < lv)).astype(F32)


_DIMS = {"nn": (((1,), (0,)), ((), ())), "nt": (((1,), (1,)), ((), ())), "tn": (((0,), (0,)), ((), ()))}


def _matmul(pairs, mode, *, name, out_dtype=F32, res=None, scale=1.0):
    a0, b0 = pairs[0]
    if mode == "nn":
        (m, k), n = a0.shape, b0.shape[1]
    elif mode == "nt":
        (m, k), n = a0.shape, b0.shape[0]
    else:
        (k, m), n = a0.shape, b0.shape[1]
    if mode == "tn":
        tm, tn, tk = _pick(m, 1024), _pick(n, 1408), _pick(k, ROW_TILE, 8)
    else:
        if k > m:
            tm, tn, kcap = _pick(m, 1408, 16), _pick(n, 1408), 1664
        else:
            tm, tn, kcap = _pick(m, ROW_TILE, 8), _pick(n, 2816), 2816
        out_b = jnp.dtype(out_dtype).itemsize
        per_k = len(pairs) * 2 * (tm * a0.dtype.itemsize + tn * b0.dtype.itemsize)
        fixed = tm * tn * (2 * out_b + 4 + (8 if res is not None else 0))
        tk = _pick(k, kcap)
        while tk > LANES and fixed + per_k * tk > MATMUL_VMEM:
            tk = _pick(k, tk - LANES)
    nk = k // tk
    npair = len(pairs)
    dims = _DIMS[mode]

    def body(*refs):
        ins = refs[:2 * npair]
        pos = 2 * npair
        res_ref = None
        if res is not None:
            res_ref = refs[pos]
            pos += 1
        o_ref = refs[pos]
        kk = pl.program_id(2)

        part = None
        for p in range(npair):
            a = ins[2 * p][...].astype(BF16)
            b = ins[2 * p + 1][...].astype(BF16)
            d = lax.dot_general(a, b, dims, preferred_element_type=F32)
            part = d if part is None else part + d

        def finish(r):
            if scale != 1.0:
                r = r * scale
            if res_ref is not None:
                r = r + res_ref[...]
            o_ref[...] = r.astype(out_dtype)

        if nk == 1:
            finish(part)
            return
        acc = refs[pos + 1]

        @pl.when(kk == 0)
        def _():
            acc[...] = part

        @pl.when(kk > 0)
        def _():
            acc[...] += part

        @pl.when(kk == nk - 1)
        def _():
            finish(acc[...])

    if mode == "nn":
        a_spec = pl.BlockSpec((tm, tk), lambda i, j, q: (i, q))
        b_spec = pl.BlockSpec((tk, tn), lambda i, j, q: (q, j))
    elif mode == "nt":
        a_spec = pl.BlockSpec((tm, tk), lambda i, j, q: (i, q))
        b_spec = pl.BlockSpec((tn, tk), lambda i, j, q: (j, q))
    else:
        a_spec = pl.BlockSpec((tk, tm), lambda i, j, q: (q, i))
        b_spec = pl.BlockSpec((tk, tn), lambda i, j, q: (q, j))
    o_spec = pl.BlockSpec((tm, tn), lambda i, j, q: (i, j))
    in_specs, args = [], []
    for a, b in pairs:
        in_specs += [a_spec, b_spec]
        args += [a, b]
    if res is not None:
        in_specs.append(o_spec)
        args.append(res)
    return pl.pallas_call(
        body, name=name, grid=(m // tm, n // tn, nk), in_specs=in_specs, out_specs=o_spec,
        out_shape=jax.ShapeDtypeStruct((m, n), out_dtype),
        scratch_shapes=[pltpu.VMEM((tm, tn), F32)] if nk > 1 else [],
        compiler_params=_params("parallel", "parallel", "arbitrary"),
    )(*args)


def _rmsnorm_fwd(x, w, *, width, col_block, name, transposed=False):
    lp = x.shape[0]
    tm = _pick(lp, ROW_TILE, 8)

    def body(x_ref, w_ref, o_ref, *ot_ref):
        xv = x_ref[...]
        r = lax.rsqrt(jnp.mean(xv * xv, axis=-1, keepdims=True) + EPS)
        y = xv * r * w_ref[...]
        o_ref[...] = y.astype(BF16)
        if transposed:
            ot_ref[0][...] = y.T.astype(BF16)

    out_specs = [pl.BlockSpec((tm, width), lambda i: (i, 0))]
    out_shape = [jax.ShapeDtypeStruct((lp, width), BF16)]
    if transposed:
        out_specs.append(pl.BlockSpec((width, tm), lambda i: (0, i)))
        out_shape.append(jax.ShapeDtypeStruct((width, lp), BF16))
    outs = pl.pallas_call(
        body, name=name, grid=(lp // tm,),
        in_specs=[pl.BlockSpec((tm, width), lambda i: (i, col_block)), pl.BlockSpec((1, width), lambda i: (0, 0))],
        out_specs=out_specs, out_shape=out_shape, compiler_params=_params("parallel"),
    )(x, w.reshape(1, width))
    return tuple(outs) if transposed else outs[0]


def _rmsnorm_bwd(x, w, dy, *, width, col_block, lv, name, dres=None):
    lp = x.shape[0]
    tm = _pick(lp, ROW_TILE, 8)

    def body(*refs):
        if dres is None:
            x_ref, w_ref, dy_ref, dx_ref, dw_ref = refs
            dres_ref = None
        else:
            x_ref, w_ref, dy_ref, dres_ref, dx_ref, dw_ref = refs
        i = pl.program_id(0)
        xv = x_ref[...]
        dyv = dy_ref[...] * _row_valid(i * tm, tm, lv)
        r = lax.rsqrt(jnp.mean(xv * xv, axis=-1, keepdims=True) + EPS)
        wdy = dyv * w_ref[...]
        dx = r * wdy - xv * (r * r * r) * jnp.mean(xv * wdy, axis=-1, keepdims=True)
        if dres_ref is not None:
            dx = dx + dres_ref[...]
        dx_ref[...] = dx

        @pl.when(i == 0)
        def _():
            dw_ref[...] = jnp.zeros_like(dw_ref)

        dw_ref[...] += jnp.sum(dyv * xv * r, axis=0, keepdims=True)

    row = pl.BlockSpec((tm, width), lambda i: (i, 0))
    in_specs = [pl.BlockSpec((tm, width), lambda i: (i, col_block)), pl.BlockSpec((1, width), lambda i: (0, 0)), row]
    args = [x, w.reshape(1, width), dy]
    if dres is not None:
        in_specs.append(row)
        args.append(dres)
    dx, dw = pl.pallas_call(
        body, name=name, grid=(lp // tm,), in_specs=in_specs,
        out_specs=[row, pl.BlockSpec((1, width), lambda i: (0, 0))],
        out_shape=[jax.ShapeDtypeStruct((lp, width), F32), jax.ShapeDtypeStruct((1, width), F32)],
        compiler_params=_params("arbitrary"),
    )(*args)
    return dx, dw[0]


def _swiglu_fwd(g, u, *, name):
    lp, f = g.shape
    tm, tf = _pick(lp, ROW_TILE, 8), _pick(f, 1408)

    def body(g_ref, u_ref, o_ref, ot_ref):
        gv = g_ref[...].astype(F32)
        act = gv * _sigmoid(gv) * u_ref[...].astype(F32)
        o_ref[...] = act.astype(BF16)
        ot_ref[...] = act.T.astype(BF16)

    spec = pl.BlockSpec((tm, tf), lambda i, j: (i, j))
    return pl.pallas_call(
        body, name=name, grid=(lp // tm, f // tf), in_specs=[spec, spec],
        out_specs=[spec, pl.BlockSpec((tf, tm), lambda i, j: (j, i))],
        out_shape=[jax.ShapeDtypeStruct((lp, f), BF16), jax.ShapeDtypeStruct((f, lp), BF16)],
        compiler_params=_params("parallel", "parallel"),
    )(g, u)


def _swiglu_bwd(dact, g, u, *, name):
    lp, f = g.shape
    tm, tf = _pick(lp, ROW_TILE, 8), _pick(f, 1408)

    def body(d_ref, g_ref, u_ref, dg_ref, du_ref):
        gv, dv = g_ref[...].astype(F32), d_ref[...].astype(F32)
        s = _sigmoid(gv)
        dg_ref[...] = (dv * u_ref[...].astype(F32) * s * (1.0 + gv * (1.0 - s))).astype(BF16)
        du_ref[...] = (dv * gv * s).astype(BF16)

    spec = pl.BlockSpec((tm, tf), lambda i, j: (i, j))
    return pl.pallas_call(
        body, name=name, grid=(lp // tm, f // tf), in_specs=[spec, spec, spec], out_specs=[spec, spec],
        out_shape=[jax.ShapeDtypeStruct((lp, f), BF16)] * 2, compiler_params=_params("parallel", "parallel"),
    )(dact, g, u)


def _merge_fwd(ya, yb, z, *, zl, name):
    lp, d = ya.shape
    tm, td = _pick(lp, ROW_TILE, 8), _pick(d, 512)
    oa, ob = zl["ga"] // td, zl["gb"] // td

    def body(ya_ref, yb_ref, ga_ref, gb_ref, o_ref, ot_ref):
        mg = _sigmoid(ga_ref[...]) * ya_ref[...] + _sigmoid(gb_ref[...]) * yb_ref[...]
        o_ref[...] = mg.astype(BF16)
        ot_ref[...] = mg.T.astype(BF16)

    spec = pl.BlockSpec((tm, td), lambda i, j: (i, j))
    return pl.pallas_call(
        body, name=name, grid=(lp // tm, d // td),
        in_specs=[spec, spec, pl.BlockSpec((tm, td), lambda i, j: (i, oa + j)),
                  pl.BlockSpec((tm, td), lambda i, j: (i, ob + j))],
        out_specs=[spec, pl.BlockSpec((td, tm), lambda i, j: (j, i))],
        out_shape=[jax.ShapeDtypeStruct((lp, d), BF16), jax.ShapeDtypeStruct((d, lp), BF16)],
        compiler_params=_params("parallel", "parallel"),
    )(ya, yb, z, z)


def _merge_bwd(dmg, ya, yb, z, *, zl, name):
    lp, d = ya.shape
    tm, td = _pick(lp, ROW_TILE, 8), _pick(d, 512)
    oa, ob = zl["ga"] // td, zl["gb"] // td

    def body(d_ref, ya_ref, yb_ref, ga_ref, gb_ref, dya_ref, dyb_ref, dga_ref, dgb_ref):
        dv = d_ref[...]
        sa, sb = _sigmoid(ga_ref[...]), _sigmoid(gb_ref[...])
        dya_ref[...] = (dv * sa).astype(BF16)
        dyb_ref[...] = (dv * sb).astype(BF16)
        dga_ref[...] = dv * ya_ref[...] * sa * (1.0 - sa)
        dgb_ref[...] = dv * yb_ref[...] * sb * (1.0 - sb)

    spec = pl.BlockSpec((tm, td), lambda i, j: (i, j))
    return pl.pallas_call(
        body, name=name, grid=(lp // tm, d // td),
        in_specs=[spec, spec, spec, pl.BlockSpec((tm, td), lambda i, j: (i, oa + j)),
                  pl.BlockSpec((tm, td), lambda i, j: (i, ob + j))],
        out_specs=[spec] * 4,
        out_shape=[jax.ShapeDtypeStruct((lp, d), BF16)] * 2 + [jax.ShapeDtypeStruct((lp, d), F32)] * 2,
        compiler_params=_params("parallel", "parallel"),
    )(dmg, ya, yb, z, z)


def _qkv_prep_fwd(q2, kv, z, cos_t, sin_t, *, name):
    lp = q2.shape[0]
    tm = _pick(lp, ROW_TILE, 8)
    h = MLA_HEADS

    def body(qa_ref, qb_ref, kv_ref, za_ref, zb_ref, c_ref, s_ref, q_ref, k_ref, v_ref):
        c, s = c_ref[...], s_ref[...]
        lane = lax.broadcasted_iota(jnp.int32, (tm, HEAD_W), 1)
        q_ref[...] = ((qa_ref[...] * c + qb_ref[...] * s) * Q_SCALE).astype(BF16)
        kr = jnp.where(lane >= QK_NOPE, za_ref[...] * c + zb_ref[...] * s, 0.0)
        kvv = kv_ref[...]
        k_ref[...] = (jnp.where(lane < QK_NOPE, kvv, 0.0) + kr).astype(BF16)
        v_ref[...] = jnp.where(lane >= QK_NOPE, kvv, 0.0).astype(BF16)

    blk = lambda f: pl.BlockSpec((tm, HEAD_W), f)
    out = blk(lambda i, j: (i, j))
    return pl.pallas_call(
        body, name=name, grid=(lp // tm, h),
        in_specs=[blk(lambda i, j: (i, j)), blk(lambda i, j: (i, h + j)), blk(lambda i, j: (i, j)),
                  blk(lambda i, j: (i, Z_KPA // HEAD_W)), blk(lambda i, j: (i, Z_KPB // HEAD_W)),
                  blk(lambda i, j: (i, 0)), blk(lambda i, j: (i, 0))],
        out_specs=[out, out, out], out_shape=[jax.ShapeDtypeStruct((lp, QW), BF16)] * 3,
        compiler_params=_params("parallel", "parallel"),
    )(q2, q2, kv, z, z, cos_t, sin_t)


def _qkv_prep_bwd(dq, dk, dv, cos_t, sin_t, *, name):
    lp = dq.shape[0]
    tm = _pick(lp, ROW_TILE, 8)
    h = MLA_HEADS

    def body(dq_ref, dk_ref, dv_ref, c_ref, s_ref, dqa_ref, dqb_ref, dkv_ref, dza_ref, dzb_ref):
        j = pl.program_id(1)
        c, s = c_ref[...], s_ref[...]
        lane = lax.broadcasted_iota(jnp.int32, (tm, HEAD_W), 1)
        dqv, dkv_ = dq_ref[...], dk_ref[...]
        dqa_ref[...] = (dqv * c).astype(BF16)
        dqb_ref[...] = (dqv * s).astype(BF16)
        dkv_ref[...] = jnp.where(lane < QK_NOPE, dkv_, dv_ref[...]).astype(BF16)
        dkr = jnp.where(lane >= QK_NOPE, dkv_, 0.0)

        @pl.when(j == 0)
        def _():
            dza_ref[...] = jnp.zeros_like(dza_ref)
            dzb_ref[...] = jnp.zeros_like(dzb_ref)

        dza_ref[...] += dkr * c
        dzb_ref[...] += dkr * s

    blk = lambda f: pl.BlockSpec((tm, HEAD_W), f)
    per_head, shared = blk(lambda i, j: (i, j)), blk(lambda i, j: (i, 0))
    return pl.pallas_call(
        body, name=name, grid=(lp // tm, h),
        in_specs=[per_head, per_head, per_head, shared, shared],
        out_specs=[per_head, per_head, per_head, shared, shared],
        out_shape=[jax.ShapeDtypeStruct((lp, QW), BF16)] * 3 + [jax.ShapeDtypeStruct((lp, HEAD_W), F32)] * 2,
        compiler_params=_params("parallel", "arbitrary"),
    )(dq, dk, dv, cos_t, sin_t)


def _attn_tile(lp):
    return _pick(lp, ROW_TILE, LANES)


Q_SCALE = (QK_NOPE + QK_ROPE) ** -0.5 * math.log2(math.e)
ATT_HP = 2


def _attn_consts(lp):
    t = _attn_tile(lp)
    nb = lp // t
    r = np.arange(t)
    causal = np.where(r[None, :] <= r[:, None], 0.0, NEG_BIG).astype(np.float32)
    front = np.where(r >= FRONT, 0.0, NEG_BIG).astype(np.float32)[None, :]
    diag = np.stack([np.minimum(causal, front), causal])
    qmaj = [(i, j) for i in range(nb) for j in range(i + 1)]
    kmaj = [(i, j) for j in range(nb) for i in range(j, nb)]
    tab = lambda pairs, c: jnp.asarray([p[c] for p in pairs], jnp.int32)
    return dict(diag=jnp.asarray(diag), front=jnp.asarray(front),
                fwd=(tab(qmaj, 0), tab(qmaj, 1)), bwd=(tab(kmaj, 0), tab(kmaj, 1)))


def _attn_fwd(q, k, v, ac, *, lv, name):
    lp = q.shape[0]
    t = _attn_tile(lp)
    nb = lp // t
    rep = t // HEAD_W
    qtab, ktab = ac["fwd"]

    def body(qt_ref, kt_ref, q_ref, k_ref, v_ref, bd_ref, bf_ref, o_ref, lse_ref, m_s, l_s, acc_s):
        step_id = pl.program_id(1)
        qb, kb = qt_ref[step_id], kt_ref[step_id]

        @pl.when(kb == 0)
        def _():
            m_s[...] = jnp.full_like(m_s, NEG_BIG)
            l_s[...] = jnp.zeros_like(l_s)
            acc_s[...] = jnp.zeros_like(acc_s)

        def step(bias):
            b = None if bias is None else bias()
            for hh in range(ATT_HP):
                sl = slice(hh * HEAD_W, (hh + 1) * HEAD_W)
                s = lax.dot_general(q_ref[:, sl], k_ref[:, sl], _DIMS["nt"], preferred_element_type=F32)
                if b is not None:
                    s = s + b
                m_prev = m_s[:, sl]
                m_new = jnp.maximum(m_prev, jnp.max(s, axis=-1, keepdims=True))
                alpha = jnp.exp2(m_prev - m_new)
                p = jnp.exp2(s - jnp.tile(m_new, (1, rep)))
                l_s[:, sl] = alpha * l_s[:, sl] + jnp.sum(p, axis=-1, keepdims=True)
                acc_s[:, sl] = alpha * acc_s[:, sl] + jnp.dot(p.astype(BF16), v_ref[:, sl],
                                                              preferred_element_type=F32)
                m_s[:, sl] = m_new

        @pl.when((kb > 0) & (kb < qb))
        def _():
            step(None)

        @pl.when((kb == 0) & (qb > 0))
        def _():
            step(lambda: bf_ref[...])

        @pl.when(kb == qb)
        def _():
            step(lambda: bd_ref[0])
            l = l_s[...]
            o_ref[...] = acc_s[...] / l * _row_valid(qb * t, t, lv)
            lse_ref[...] = m_s[...] + jnp.log2(l)

    wd = ATT_HP * HEAD_W
    qs = pl.BlockSpec((t, wd), lambda h, s, qt, kt: (qt[s], h))
    ks = pl.BlockSpec((t, wd), lambda h, s, qt, kt: (kt[s], h))
    grid_spec = pltpu.PrefetchScalarGridSpec(
        num_scalar_prefetch=2, grid=(MLA_HEADS // ATT_HP, int(qtab.shape[0])),
        in_specs=[qs, ks, ks, pl.BlockSpec((1, t, t), lambda h, s, qt, kt: (jnp.minimum(qt[s], 1), 0, 0)),
                  pl.BlockSpec((1, t), lambda h, s, qt, kt: (0, 0))],
        out_specs=[qs, qs],
        scratch_shapes=[pltpu.VMEM((t, wd), F32), pltpu.VMEM((t, wd), F32), pltpu.VMEM((t, wd), F32)])
    return pl.pallas_call(
        body, name=name, grid_spec=grid_spec, out_shape=[jax.ShapeDtypeStruct((lp, QW), F32)] * 2,
        compiler_params=_params("parallel", "arbitrary"),
    )(qtab, ktab, q, k, v, ac["diag"], ac["front"])


def _attn_delta(do, o, *, name):
    lp = do.shape[0]
    tm = _pick(lp, ROW_TILE, 8)

    def body(do_ref, o_ref, d_ref):
        d_ref[...] = jnp.broadcast_to(jnp.sum(do_ref[...] * o_ref[...], axis=-1, keepdims=True), (tm, HEAD_W))

    spec = pl.BlockSpec((tm, HEAD_W), lambda i, j: (i, j))
    return pl.pallas_call(
        body, name=name, grid=(lp // tm, MLA_HEADS), in_specs=[spec, spec], out_specs=spec,
        out_shape=jax.ShapeDtypeStruct((lp, QW), F32), compiler_params=_params("parallel", "parallel"),
    )(do, o)


def _attn_bwd(q, k, v, do, lse, delta, ac, *, name):
    lp = q.shape[0]
    t = _attn_tile(lp)
    nb = lp // t
    rep = t // HEAD_W
    scale = (QK_NOPE + QK_ROPE) ** -0.5
    qtab, ktab = ac["bwd"]

    def body(qt_ref, kt_ref, q_ref, k_ref, v_ref, do_ref, lse_ref, dl_ref, bd_ref, bf_ref, dq_ref, dk_ref, dv_ref,
             dk_s, dv_s):
        step_id = pl.program_id(1)
        qb, kb = qt_ref[step_id], kt_ref[step_id]

        @pl.when(qb == kb)
        def _():
            dk_s[...] = jnp.zeros_like(dk_s)
            dv_s[...] = jnp.zeros_like(dv_s)

        def step(bias):
            b = None if bias is None else bias()
            rows = pl.ds(pl.multiple_of(qb * t, t), t)
            contribs = []
            for hh in range(ATT_HP):
                sl = slice(hh * HEAD_W, (hh + 1) * HEAD_W)
                qv, kv_, vv = q_ref[:, sl], k_ref[:, sl], v_ref[:, sl]
                dov = do_ref[:, sl].astype(BF16)
                s = lax.dot_general(qv, kv_, _DIMS["nt"], preferred_element_type=F32)
                if b is not None:
                    s = s + b
                p = jnp.exp2(s - jnp.tile(lse_ref[:, sl], (1, rep)))
                dv_s[:, sl] += lax.dot_general(p.astype(BF16), dov, _DIMS["tn"], preferred_element_type=F32)
                dp = lax.dot_general(dov, vv, _DIMS["nt"], preferred_element_type=F32)
                ds = (p * (dp - jnp.tile(dl_ref[:, sl], (1, rep))) * scale).astype(BF16)
                dk_s[:, sl] += lax.dot_general(ds, qv, _DIMS["tn"], preferred_element_type=F32)
                contribs.append(jnp.dot(ds, kv_, preferred_element_type=F32))
            contrib = jnp.concatenate(contribs, axis=1)

            @pl.when(kb == 0)
            def _():
                dq_ref[rows, :] = contrib

            @pl.when(kb > 0)
            def _():
                dq_ref[rows, :] += contrib

        @pl.when((kb > 0) & (kb < qb))
        def _():
            step(None)

        @pl.when((kb == 0) & (qb > 0))
        def _():
            step(lambda: bf_ref[...])

        @pl.when(kb == qb)
        def _():
            step(lambda: bd_ref[0])

        @pl.when(qb == nb - 1)
        def _():
            dk_ref[...] = dk_s[...] * (1.0 / Q_SCALE)
            dv_ref[...] = dv_s[...]

    wd = ATT_HP * HEAD_W
    qs = pl.BlockSpec((t, wd), lambda h, s, qt, kt: (qt[s], h))
    ks = pl.BlockSpec((t, wd), lambda h, s, qt, kt: (kt[s], h))
    dqs = pl.BlockSpec((lp, wd), lambda h, s, qt, kt: (0, h))
    grid_spec = pltpu.PrefetchScalarGridSpec(
        num_scalar_prefetch=2, grid=(MLA_HEADS // ATT_HP, int(qtab.shape[0])),
        in_specs=[qs, ks, ks, qs, qs, qs,
                  pl.BlockSpec((1, t, t), lambda h, s, qt, kt: (jnp.minimum(qt[s], 1), 0, 0)),
                  pl.BlockSpec((1, t), lambda h, s, qt, kt: (0, 0))],
        out_specs=[dqs, ks, ks],
        scratch_shapes=[pltpu.VMEM((t, wd), F32), pltpu.VMEM((t, wd), F32)])
    return pl.pallas_call(
        body, name=name, grid_spec=grid_spec, out_shape=[jax.ShapeDtypeStruct((lp, QW), F32)] * 3,
        compiler_params=_params("arbitrary", "arbitrary"),
    )(qtab, ktab, q, k, v, do, lse, delta, ac["diag"], ac["front"])


HG_LEVELS = (64, 32, 16, 8, 4, 2)
N_LEV = len(HG_LEVELS)


def _hgrn_consts():
    c = HG_CHUNK
    m = np.zeros((N_LEV + 2, c, c), np.float32)
    masks = np.zeros((N_LEV, c, c), np.float32)
    for li, p in enumerate(HG_LEVELS):
        for t in range(c):
            mid = (t // p) * p + p // 2
            if t >= mid:
                m[li, t, mid:t + 1] = 1.0
            else:
                m[li, t, t + 1:mid] = 1.0
            for s in range(c):
                if s // p == t // p and t >= mid and s < mid:
                    masks[li, t, s] = 1.0
    for t in range(c):
        m[N_LEV, t, :t + 1] = 1.0
        m[N_LEV + 1, t, t + 1:] = 1.0
    mall = m.reshape((N_LEV + 2) * c, c)
    return jnp.asarray(mall, BF16), jnp.asarray(mall.T.copy(), BF16), jnp.asarray(masks, F32)


def _split3(x):
    hi = x.astype(BF16)
    r = x - hi.astype(F32)
    mid = r.astype(BF16)
    lo = (r - mid.astype(F32)).astype(BF16)
    return jnp.concatenate([hi, mid, lo], axis=1)


def _sum3(e3):
    return e3[:, :HG_D] + e3[:, HG_D:2 * HG_D] + e3[:, 2 * HG_D:]


def _hgrn_chunk_fwd(hq, hf, hi, lb, valid, mall, masks, st):
    c = HG_CHUNK
    scale = HG_D ** -0.5
    sq = _sigmoid(hq)
    qv = hq * sq
    sg = _sigmoid(hf)
    f = lb + (1.0 - lb) * sg
    fc = jnp.maximum(f, F_MIN)
    lf = jnp.log(fc) * valid
    kv = (1.0 - lb) * (1.0 - sg) * valid
    e = _sum3(jnp.dot(mall, _split3(lf), preferred_element_type=F32))
    x = jnp.exp(e)
    a = jnp.zeros((c, c), F32)
    qe, ke = [], []
    for l in range(N_LEV):
        xl = x[l * c:(l + 1) * c]
        qe.append(qv * xl)
        ke.append(kv * xl)
        a = a + masks[l] * lax.dot_general(qe[l].astype(BF16), ke[l].astype(BF16), _DIMS["nt"],
                                           preferred_element_type=F32)
    row = lax.broadcasted_iota(jnp.int32, (c, c), 0)
    col = lax.broadcasted_iota(jnp.int32, (c, c), 1)
    a = a + jnp.where(row == col, jnp.sum(qv * kv, axis=-1, keepdims=True), 0.0)
    xb = x[N_LEV * c:(N_LEV + 1) * c]
    qb = qv * xb
    kb = kv * x[(N_LEV + 1) * c:]
    x_last = xb[c - 1:c]
    hib = hi.astype(BF16)
    o = scale * (jnp.dot(a.astype(BF16), hib, preferred_element_type=F32)
                 + lax.dot_general(qb.astype(BF16), st.astype(BF16), _DIMS["nt"], preferred_element_type=F32))
    st_new = st * x_last + lax.dot_general(hib, kb.astype(BF16), _DIMS["tn"], preferred_element_type=F32)
    saved = dict(sq=sq, qv=qv, sg=sg, f=f, fc=fc, kv=kv, x=x, a=a, qe=qe, ke=ke, qb=qb, kb=kb, x_last=x_last)
    return o, st_new, saved


def _split_ride(refs, n_in, n_out, n_scratch, ride):
    ri = len(ride.args) if ride else 0
    ro = len(ride.out_shape) if ride else 0
    a = n_in + ri
    b = a + n_out + ro
    c = b + n_scratch
    return refs[:n_in], refs[a:a + n_out], refs[b:c], refs[n_in:a] + refs[a + n_out:b] + refs[c:]


def _ride_call(ride):
    if ride is None:
        return [], [], [], [], []
    hbm = [HBM_SPEC] * len(ride.args)
    return hbm, list(ride.args), [HBM_SPEC] * len(ride.out_shape), list(ride.out_shape), list(ride.scratch)


def _hgrn_fwd(z, lb, nw, consts, *, zl, lv, name, ride=None):
    lp = z.shape[0]
    tb = _pick(lp, ROW_TILE, HG_CHUNK)
    ncb = tb // HG_CHUNK
    nb = lp // tb
    mall, _, masks = consts
    w = HG_HEADS * HG_D

    def body(*refs):
        ins, outs, (st_s,), ride_refs = _split_ride(refs, 8, 3, 1, ride)
        hq_ref, hf_ref, hi_ref, hg_ref, lb_ref, nw_ref, mall_ref, masks_ref = ins
        o_ref, ob_ref, st_ref = outs
        i = pl.program_id(0)

        @pl.when(i == 0)
        def _():
            st_s[...] = jnp.zeros_like(st_s)
            if ride is not None:
                ride.start(*ride_refs)

        nwv = nw_ref[...]
        mallv, masksv = mall_ref[...], masks_ref[...]

        def chunk(cix, carry):
            r0 = pl.multiple_of(cix * HG_CHUNK, HG_CHUNK)
            rows = pl.ds(r0, HG_CHUNK)
            valid = _row_valid(i * tb + r0, HG_CHUNK, lv)
            for h in range(HG_HEADS):
                sl = slice(h * HG_D, (h + 1) * HG_D)
                st = st_s[h]
                st_ref[h, cix] = st
                o, st_new, _ = _hgrn_chunk_fwd(hq_ref[rows, sl], hf_ref[rows, sl], hi_ref[rows, sl], lb_ref[:, sl],
                                               valid, mallv, masksv, st)
                st_s[h] = st_new
                o_ref[rows, sl] = o
                hg = hg_ref[rows, sl]
                r = lax.rsqrt(jnp.mean(o * o, axis=-1, keepdims=True) + EPS)
                ob_ref[rows, sl] = (o * r * nwv * (hg * _sigmoid(hg))).astype(BF16)
            return carry

        lax.fori_loop(0, ncb, chunk, 0)

        if ride is not None:
            @pl.when(i == nb - 1)
            def _():
                ride.finish(*ride_refs)

    zb = lambda off: pl.BlockSpec((tb, w), lambda i: (i, off // w))
    full = pl.BlockSpec((tb, w), lambda i: (i, 0))
    const = lambda shape: pl.BlockSpec(shape, lambda i: (0,) * len(shape))
    r_in, r_args, r_out, r_shape, r_scratch = _ride_call(ride)
    outs = pl.pallas_call(
        body, name=name, grid=(nb,),
        in_specs=[zb(zl["hq"]), zb(zl["hf"]), zb(zl["hi"]), zb(zl["hg"]), const((1, w)), const((1, HG_D)),
                  const(mall.shape), const(masks.shape)] + r_in,
        out_specs=[full, full, pl.BlockSpec((HG_HEADS, ncb, HG_D, HG_D), lambda i: (0, i, 0, 0))] + r_out,
        out_shape=[jax.ShapeDtypeStruct((lp, w), F32), jax.ShapeDtypeStruct((lp, w), BF16),
                   jax.ShapeDtypeStruct((HG_HEADS, lp // HG_CHUNK, HG_D, HG_D), F32)] + r_shape,
        scratch_shapes=[pltpu.VMEM((HG_HEADS, HG_D, HG_D), F32)] + r_scratch,
        compiler_params=_params("arbitrary"),
    )(z, z, z, z, lb.reshape(1, w), nw.reshape(1, HG_D), mall, masks, *r_args)
    return outs[0], outs[1], outs[2], list(outs[3:])


def _hgrn_chunk_bwd(hq, hf, hi, hg, o, dout, st, dst, lbv, nwv, valid, mallv, malltv, masksv):
    c = HG_CHUNK
    scale = HG_D ** -0.5
    _, _, sv = _hgrn_chunk_fwd(hq, hf, hi, lbv, valid, mallv, masksv, st)
    shg = _sigmoid(hg)
    r = lax.rsqrt(jnp.mean(o * o, axis=-1, keepdims=True) + EPS)
    don = dout * (hg * shg)
    dhg = dout * (o * r * nwv) * shg * (1.0 + hg * (1.0 - shg))
    dnw = jnp.sum(don * o * r, axis=0, keepdims=True)
    wd = don * nwv
    do = r * wd - o * (r * r * r) * jnp.mean(o * wd, axis=-1, keepdims=True)
    dob16, hib = do.astype(BF16), hi.astype(BF16)
    dst16 = dst.astype(BF16)
    da = scale * lax.dot_general(dob16, hib, _DIMS["nt"], preferred_element_type=F32)
    dv = (scale * lax.dot_general(sv["a"].astype(BF16), dob16, _DIMS["tn"], preferred_element_type=F32)
          + lax.dot_general(sv["kb"].astype(BF16), dst16, _DIMS["nt"], preferred_element_type=F32))
    dkb = jnp.dot(hib, dst16, preferred_element_type=F32)
    dqb = scale * jnp.dot(dob16, st.astype(BF16), preferred_element_type=F32)
    dst_new = dst * sv["x_last"] + scale * lax.dot_general(dob16, sv["qb"].astype(BF16), _DIMS["tn"],
                                                           preferred_element_type=F32)
    dxl = jnp.sum(dst * st, axis=0, keepdims=True)
    x = sv["x"]
    dq = dqb * x[N_LEV * c:(N_LEV + 1) * c]
    dk = dkb * x[(N_LEV + 1) * c:]
    de = []
    for l in range(N_LEV):
        dam = (masksv[l] * da).astype(BF16)
        dqe = jnp.dot(dam, sv["ke"][l].astype(BF16), preferred_element_type=F32)
        dke = lax.dot_general(dam, sv["qe"][l].astype(BF16), _DIMS["tn"], preferred_element_type=F32)
        xl = x[l * c:(l + 1) * c]
        dq = dq + dqe * xl
        dk = dk + dke * xl
        de.append(dqe * sv["qe"][l] + dke * sv["ke"][l])
    dd = scale * jnp.sum(do * hi, axis=-1, keepdims=True)
    dq = dq + dd * sv["kv"]
    dk = dk + dd * sv["qv"]
    last = lax.broadcasted_iota(jnp.int32, (c, 1), 0) == c - 1
    de.append(dqb * sv["qb"] + jnp.where(last, dxl * sv["x_last"], 0.0))
    de.append(dkb * sv["kb"])
    dlf = _sum3(jnp.dot(malltv, _split3(jnp.concatenate(de, axis=0)), preferred_element_type=F32))
    sg, sq = sv["sg"], sv["sq"]
    df = jnp.where(sv["f"] > F_MIN, dlf * valid / sv["fc"], 0.0)
    dkm = dk * valid
    dhf = (df - dkm) * (1.0 - lbv) * sg * (1.0 - sg)
    dlb = jnp.sum((df - dkm) * (1.0 - sg), axis=0, keepdims=True)
    dhq = dq * sq * (1.0 + hq * (1.0 - sq))
    return dhq, dhf, dv, dhg, dlb, dnw, dst_new


def _hgrn_bwd(z, o_pre, dob, states, lb, nw, consts, *, zl, lv, name, ride=None):
    lp = z.shape[0]
    tb = _pick(lp, ROW_TILE, HG_CHUNK)
    ncb = tb // HG_CHUNK
    nb = lp // tb
    mall, mall_t, masks = consts
    w = HG_HEADS * HG_D
    c = HG_CHUNK

    def body(*refs):
        ins, outs, (dst_s,), ride_refs = _split_ride(refs, 12, 6, 1, ride)
        hq_ref, hf_ref, hi_ref, hg_ref, o_ref, dob_ref, st_ref, lb_ref, nw_ref, mall_ref, mallt_ref, masks_ref = ins
        dhq_ref, dhf_ref, dhi_ref, dhg_ref, dlb_ref, dnw_ref = outs
        i = pl.program_id(0)
        blk = nb - 1 - i

        @pl.when(i == 0)
        def _():
            dst_s[...] = jnp.zeros_like(dst_s)
            dlb_ref[...] = jnp.zeros_like(dlb_ref)
            dnw_ref[...] = jnp.zeros_like(dnw_ref)
            if ride is not None:
                ride.start(*ride_refs)

        nwv = nw_ref[...]
        mallv, malltv, masksv = mall_ref[...], mallt_ref[...], masks_ref[...]

        def chunk(jx, carry):
            cix = ncb - 1 - jx
            r0 = pl.multiple_of(cix * c, c)
            rows = pl.ds(r0, c)
            valid = _row_valid(blk * tb + r0, c, lv)
            for h in range(HG_HEADS):
                sl = slice(h * HG_D, (h + 1) * HG_D)
                dhq, dhf, dhi, dhg, dlb, dnw, dst_new = _hgrn_chunk_bwd(
                    hq_ref[rows, sl], hf_ref[rows, sl], hi_ref[rows, sl], hg_ref[rows, sl], o_ref[rows, sl],
                    dob_ref[rows, sl], st_ref[h, cix], dst_s[h], lb_ref[:, sl], nwv, valid, mallv, malltv, masksv)
                dst_s[h] = dst_new
                dhq_ref[rows, sl] = dhq
                dhf_ref[rows, sl] = dhf
                dhi_ref[rows, sl] = dhi
                dhg_ref[rows, sl] = dhg
                dlb_ref[:, sl] += dlb
                dnw_ref[...] += dnw
            return carry

        lax.fori_loop(0, ncb, chunk, 0)

        if ride is not None:
            @pl.when(i == nb - 1)
            def _():
                ride.finish(*ride_refs)

    zb = lambda off: pl.BlockSpec((tb, w), lambda i: (nb - 1 - i, off // w))
    full = pl.BlockSpec((tb, w), lambda i: (nb - 1 - i, 0))
    const = lambda shape: pl.BlockSpec(shape, lambda i: (0,) * len(shape))
    r_in, r_args, r_out, r_shape, r_scratch = _ride_call(ride)
    outs = pl.pallas_call(
        body, name=name, grid=(nb,),
        in_specs=[zb(zl["hq"]), zb(zl["hf"]), zb(zl["hi"]), zb(zl["hg"]), full, full,
                  pl.BlockSpec((HG_HEADS, ncb, HG_D, HG_D), lambda i: (0, nb - 1 - i, 0, 0)),
                  const((1, w)), const((1, HG_D)), const(mall.shape), const(mall_t.shape), const(masks.shape)] + r_in,
        out_specs=[full, full, full, full, const((1, w)), const((1, HG_D))] + r_out,
        out_shape=[jax.ShapeDtypeStruct((lp, w), F32)] * 4
                  + [jax.ShapeDtypeStruct((1, w), F32), jax.ShapeDtypeStruct((1, HG_D), F32)] + r_shape,
        scratch_shapes=[pltpu.VMEM((HG_HEADS, HG_D, HG_D), F32)] + r_scratch,
        compiler_params=_params("arbitrary"),
    )(z, z, z, z, o_pre, dob, states, lb.reshape(1, w), nw.reshape(1, HG_D), mall, mall_t, masks, *r_args)
    dhq, dhf, dhi, dhg, dlb, dnw = outs[:6]
    return dhq, dhf, dhi, dhg, dlb[0], dnw[0], list(outs[6:])


def _loss_head(h, w, tpad, *, lv, name):
    lp, d = h.shape
    tm = _pick(lp, ROW_TILE, 8)

    def body(h_ref, w_ref, t_ref, dh_ref, loss_ref, dw_ref):
        i = pl.program_id(0)
        r0 = i * tm + lax.broadcasted_iota(jnp.int32, (tm, 1), 0)
        valid = ((r0 >= ROW_X) & (r0 < lv)).astype(F32)
        xv, wv = h_ref[...], w_ref[...]
        r = lax.rsqrt(jnp.mean(xv * xv, axis=-1, keepdims=True) + EPS)
        e = (xv * r * wv - t_ref[...]) * valid
        dy = e * (1.0 / d)
        wdy = dy * wv
        dh_ref[...] = r * wdy - xv * (r * r * r) * jnp.mean(xv * wdy, axis=-1, keepdims=True)

        @pl.when(i == 0)
        def _():
            loss_ref[...] = jnp.zeros_like(loss_ref)
            dw_ref[...] = jnp.zeros_like(dw_ref)

        loss_ref[...] += 0.5 * jnp.sum(jnp.mean(e * e, axis=-1, keepdims=True), axis=0, keepdims=True)
        dw_ref[...] += jnp.sum(dy * xv * r, axis=0, keepdims=True)

    row = pl.BlockSpec((tm, d), lambda i: (i, 0))
    vec = pl.BlockSpec((1, d), lambda i: (0, 0))
    dh, loss, dw = pl.pallas_call(
        body, name=name, grid=(lp // tm,), in_specs=[row, vec, row],
        out_specs=[row, pl.BlockSpec((8, LANES), lambda i: (0, 0)), vec],
        out_shape=[jax.ShapeDtypeStruct((lp, d), F32), jax.ShapeDtypeStruct((8, LANES), F32),
                   jax.ShapeDtypeStruct((1, d), F32)],
        compiler_params=_params("arbitrary"),
    )(h, w.reshape(1, d), tpad)
    return dh, loss[0, 0], dw[0]


def _adamw(w, g, m, v, *, name):
    shape = w.shape
    cols = shape[-1]
    rows = int(np.prod(shape[:-1])) if len(shape) > 1 else 1
    tr = _pick(rows, 256, 8)
    c1 = 1.0 - ADAM_B1 ** ADAM_STEP
    c2 = 1.0 - ADAM_B2 ** ADAM_STEP

    def body(w_ref, g_ref, m_ref, v_ref, d_ref, nm_ref, nv_ref):
        gv = g_ref[...]
        nm = ADAM_B1 * m_ref[...] + (1.0 - ADAM_B1) * gv
        nv = ADAM_B2 * v_ref[...] + (1.0 - ADAM_B2) * (gv * gv)
        d_ref[...] = -ADAM_LR * ((nm / c1) / (jnp.sqrt(nv / c2) + ADAM_EPS) + ADAM_WD * w_ref[...])
        nm_ref[...] = nm
        nv_ref[...] = nv

    spec = pl.BlockSpec((tr, cols), lambda i: (i, 0))
    r2 = lambda a: a.reshape(rows, cols)
    outs = pl.pallas_call(
        body, name=name, grid=(rows // tr,), in_specs=[spec] * 4, out_specs=[spec] * 3,
        out_shape=[jax.ShapeDtypeStruct((rows, cols), F32)] * 3, compiler_params=_params("parallel"),
    )(r2(w), r2(g), r2(m), r2(v))
    return tuple(o.reshape(shape) for o in outs)


HBM_SPEC = pl.BlockSpec(memory_space=pl.ANY)


def _coords():
    return lax.axis_index("x"), lax.axis_index("y"), lax.axis_index("c")


def _other_chips(x, y):
    return [(1 - x, y), (x, 1 - y), (1 - x, 1 - y)]


def _remote(src, dst, ssem, rsem, dev):
    return pltpu.make_async_remote_copy(src_ref=src, dst_ref=dst, send_sem=ssem, recv_sem=rsem,
                                        device_id=dev, device_id_type=MESH)


def _gather_chips(w, *, name):
    rows, cols = w.shape
    rh = rows // 2
    align = 8 * 4 // w.dtype.itemsize
    assert rh * 2 == rows and rh % align == 0

    def body(w_ref, out_ref, send_sems, recv_sems):
        x, y, c = _coords()
        k = 2 * x + y
        sib = (x, y, 1 - c)
        half = pl.ds(pl.multiple_of(c * rh, align), rh)
        ohalf = pl.ds(pl.multiple_of((1 - c) * rh, align), rh)
        chips = _other_chips(x, y)
        sent = []
        for j, (px, py) in enumerate(chips):
            cp = _remote(w_ref.at[half], out_ref.at[k, half], send_sems.at[j], recv_sems.at[j], (px, py, c))
            cp.start()
            sent.append(cp)
        for j, (px, py) in enumerate(chips):
            blk = out_ref.at[2 * px + py, half]
            _remote(w_ref.at[half], blk, send_sems.at[j], recv_sems.at[j], (px, py, c)).wait_recv()
            fw = _remote(blk, blk, send_sems.at[3 + j], recv_sems.at[3 + j], sib)
            fw.start()
            sent.append(fw)
        for j, (px, py) in enumerate(chips):
            blk = out_ref.at[2 * px + py, ohalf]
            _remote(blk, blk, send_sems.at[3 + j], recv_sems.at[3 + j], sib).wait_recv()
        for cp in sent:
            cp.wait_send()

    g4 = pl.pallas_call(
        body, name=name, in_specs=[HBM_SPEC], out_specs=HBM_SPEC,
        out_shape=jax.ShapeDtypeStruct((4, rows, cols), w.dtype),
        scratch_shapes=[pltpu.SemaphoreType.DMA((6,)), pltpu.SemaphoreType.DMA((6,))],
    )(w)
    xi, yi, _ = _coords()
    return lax.dynamic_update_slice(g4, w[None], (2 * xi + yi, 0, 0))


def _swap_halves(gp, *, name):
    n, rows, cols = gp.shape
    rh = rows // 2

    def body(g_ref, out_ref, send_sems, recv_sems):
        x, y, c = _coords()
        sib = (x, y, 1 - c)
        ohalf = pl.ds(pl.multiple_of((1 - c) * rh, 8 * 4 // gp.dtype.itemsize), rh)
        cps = [_remote(g_ref.at[s, ohalf], out_ref.at[s], send_sems.at[s], recv_sems.at[s], sib) for s in range(n)]
        for cp in cps:
            cp.start()
        for cp in cps:
            cp.wait_recv()
        for cp in cps:
            cp.wait_send()

    return pl.pallas_call(
        body, name=name, in_specs=[HBM_SPEC], out_specs=HBM_SPEC,
        out_shape=jax.ShapeDtypeStruct((n, rh, cols), gp.dtype),
        scratch_shapes=[pltpu.SemaphoreType.DMA((n,)), pltpu.SemaphoreType.DMA((n,))],
    )(gp)


def _add_half(gp, got, cidx, *, name):
    n, rows, cols = gp.shape
    rh = rows // 2
    tr = _pick(rh, 512, 16)
    nrb = rh // tr

    def body(c_ref, a_ref, b_ref, o_ref):
        o_ref[...] = (a_ref[...].astype(F32) + b_ref[...].astype(F32)).astype(BF16)

    grid_spec = pltpu.PrefetchScalarGridSpec(
        num_scalar_prefetch=1, grid=(n, nrb),
        in_specs=[pl.BlockSpec((1, tr, cols), lambda s, i, c_ref: (s, c_ref[0] * nrb + i, 0)),
                  pl.BlockSpec((1, tr, cols), lambda s, i, c_ref: (s, i, 0))],
        out_specs=pl.BlockSpec((1, tr, cols), lambda s, i, c_ref: (s, i, 0)))
    return pl.pallas_call(
        body, name=name, grid_spec=grid_spec, out_shape=jax.ShapeDtypeStruct((n, rh, cols), BF16),
        compiler_params=_params("parallel", "parallel"),
    )(cidx, gp, got)


def _scatter_chips(p, *, name):
    _, rh, cols = p.shape

    def body(p_ref, out_ref, send_sems, recv_sems):
        x, y, c = _coords()
        cps = []
        for j, (px, py) in enumerate(_other_chips(x, y)):
            cps.append(_remote(p_ref.at[2 * px + py], out_ref.at[j], send_sems.at[j], recv_sems.at[j], (px, py, c)))
        for cp in cps:
            cp.start()
        for cp in cps:
            cp.wait_recv()
        for cp in cps:
            cp.wait_send()

    return pl.pallas_call(
        body, name=name, in_specs=[HBM_SPEC], out_specs=HBM_SPEC,
        out_shape=jax.ShapeDtypeStruct((3, rh, cols), p.dtype),
        scratch_shapes=[pltpu.SemaphoreType.DMA((3,)), pltpu.SemaphoreType.DMA((3,))],
    )(p)


def _sum_arrivals(p, land, kidx, *, name):
    _, rh, cols = p.shape
    tr = _pick(rh, 512, 16)

    def body(k_ref, a_ref, l_ref, o_ref):
        f = lambda v: v.astype(F32)
        o_ref[...] = ((f(a_ref[0]) + f(l_ref[0])) + f(l_ref[1])) + f(l_ref[2])

    grid_spec = pltpu.PrefetchScalarGridSpec(
        num_scalar_prefetch=1, grid=(rh // tr,),
        in_specs=[pl.BlockSpec((1, tr, cols), lambda i, k_ref: (k_ref[0], i, 0)),
                  pl.BlockSpec((3, tr, cols), lambda i, k_ref: (0, i, 0))],
        out_specs=pl.BlockSpec((tr, cols), lambda i, k_ref: (i, 0)))
    return pl.pallas_call(
        body, name=name, grid_spec=grid_spec, out_shape=jax.ShapeDtypeStruct((rh, cols), F32),
        compiler_params=_params("parallel"),
    )(kidx, p, land)


def _join_halves(q, *, name):
    rh, cols = q.shape

    def body(q_ref, out_ref, send_sem, recv_sem):
        x, y, c = _coords()
        half = pl.ds(pl.multiple_of(c * rh, 8), rh)
        ohalf = pl.ds(pl.multiple_of((1 - c) * rh, 8), rh)
        cp = _remote(q_ref, out_ref.at[half], send_sem, recv_sem, (x, y, 1 - c))
        cp.start()
        _remote(q_ref, out_ref.at[ohalf], send_sem, recv_sem, (x, y, 1 - c)).wait_recv()
        cp.wait_send()

    full = pl.pallas_call(
        body, name=name, in_specs=[HBM_SPEC], out_specs=HBM_SPEC,
        out_shape=jax.ShapeDtypeStruct((2 * rh, cols), q.dtype),
        scratch_shapes=[pltpu.SemaphoreType.DMA, pltpu.SemaphoreType.DMA],
    )(q)
    return lax.dynamic_update_slice(full, q, (lax.axis_index("c") * rh, 0))


def _rs_begin(gp, cidx, *, tag):
    got = _swap_halves(gp, name=f"rs_swap_{tag}")
    return _add_half(gp, got, cidx, name=f"rs_add_{tag}")


def _rs_end(p, land, kidx, *, tag):
    q = _sum_arrivals(p, land, kidx, name=f"rs_sum_{tag}")
    return _join_halves(q, name=f"rs_join_{tag}")


class _ScatterRide:
    def __init__(self, p):
        _, rh, cols = p.shape
        self.args = [p]
        self.out_shape = [jax.ShapeDtypeStruct((3, rh, cols), p.dtype)]
        self.scratch = [pltpu.SemaphoreType.DMA((3,)), pltpu.SemaphoreType.DMA((3,))]

    def _copies(self, p_ref, out_ref, ssem, rsem):
        x, y, c = _coords()
        return [_remote(p_ref.at[2 * px + py], out_ref.at[j], ssem.at[j], rsem.at[j], (px, py, c))
                for j, (px, py) in enumerate(_other_chips(x, y))]

    def start(self, *refs):
        for cp in self._copies(*refs):
            cp.start()

    def finish(self, *refs):
        cps = self._copies(*refs)
        for cp in cps:
            cp.wait_recv()
        for cp in cps:
            cp.wait_send()


class _GatherRide:
    def __init__(self, w):
        rows, cols = w.shape
        self.rh = rows // 2
        self.align = 8 * 4 // w.dtype.itemsize
        assert self.rh * 2 == rows and self.rh % self.align == 0
        self.args = [w]
        self.out_shape = [jax.ShapeDtypeStruct((4, rows, cols), w.dtype)]
        self.scratch = [pltpu.SemaphoreType.DMA((3,)), pltpu.SemaphoreType.DMA((3,))]

    def _copies(self, w_ref, out_ref, ssem, rsem):
        x, y, c = _coords()
        half = pl.ds(pl.multiple_of(c * self.rh, self.align), self.rh)
        send, recv = [], []
        for j, (px, py) in enumerate(_other_chips(x, y)):
            send.append(_remote(w_ref.at[half], out_ref.at[2 * x + y, half], ssem.at[j], rsem.at[j], (px, py, c)))
            recv.append(_remote(w_ref.at[half], out_ref.at[2 * px + py, half], ssem.at[j], rsem.at[j], (px, py, c)))
        return send, recv

    def start(self, *refs):
        for cp in self._copies(*refs)[0]:
            cp.start()

    def finish(self, *refs):
        send, recv = self._copies(*refs)
        for cp in recv:
            cp.wait_recv()
        for cp in send:
            cp.wait_send()


def _gather_forward(g4, *, name):
    _, rows, cols = g4.shape
    rh = rows // 2
    align = 8 * 4 // g4.dtype.itemsize

    def body(g_ref, out_ref, send_sems, recv_sems):
        x, y, c = _coords()
        sib = (x, y, 1 - c)
        half = pl.ds(pl.multiple_of(c * rh, align), rh)
        ohalf = pl.ds(pl.multiple_of((1 - c) * rh, align), rh)
        chips = _other_chips(x, y)
        sent = []
        for j, (px, py) in enumerate(chips):
            blk = out_ref.at[2 * px + py, half]
            cp = _remote(blk, blk, send_sems.at[j], recv_sems.at[j], sib)
            cp.start()
            sent.append(cp)
        for j, (px, py) in enumerate(chips):
            blk = out_ref.at[2 * px + py, ohalf]
            _remote(blk, blk, send_sems.at[j], recv_sems.at[j], sib).wait_recv()
        for cp in sent:
            cp.wait_send()

    return pl.pallas_call(
        body, name=name, in_specs=[HBM_SPEC], out_specs=HBM_SPEC, out_shape=jax.ShapeDtypeStruct(g4.shape, g4.dtype),
        input_output_aliases={0: 0},
        scratch_shapes=[pltpu.SemaphoreType.DMA((3,)), pltpu.SemaphoreType.DMA((3,))],
    )(g4)


def _own_block(g4, w):
    xi, yi, _ = _coords()
    return lax.dynamic_update_slice(g4, w[None], (2 * xi + yi, 0, 0))


def _allreduce_small(s, *, name):
    rows, cols = s.shape

    def body(s_ref, o_ref, buf, send_sems, recv_sems):
        x, y, c = _coords()
        me = 4 * x + 2 * y + c
        buf[me] = s_ref[...]
        cps = []
        for r in range(1, 8):
            peer = tuple((1 - v) if (r >> sh) & 1 else v for v, sh in ((x, 2), (y, 1), (c, 0)))
            cps.append(_remote(s_ref, buf.at[me], send_sems.at[r - 1], recv_sems.at[r - 1], peer))
        for cp in cps:
            cp.start()
        for cp in cps:
            cp.wait_recv()
        for cp in cps:
            cp.wait_send()
        acc = buf[0]
        for d in range(1, 8):
            acc = acc + buf[d]
        o_ref[...] = acc

    vm = pl.BlockSpec(memory_space=pltpu.VMEM)
    return pl.pallas_call(
        body, name=name, in_specs=[vm], out_specs=vm, out_shape=jax.ShapeDtypeStruct((rows, cols), F32),
        scratch_shapes=[pltpu.VMEM((8, rows, cols), F32), pltpu.SemaphoreType.DMA((7,)),
                        pltpu.SemaphoreType.DMA((7,))],
    )(s)


PACKED = ("ffn1_w_gu", "ffn1_w_down", "w_in", "w_uq", "w_ukv", "w_proj_attn", "w_proj_rec", "w_out",
          "ffn2_w_gu", "ffn2_w_down")
ROW_SHARDED = ("ffn1_w_down", "w_out", "ffn2_w_down")


def _pack_plan(shard_shapes):
    plan, off = {}, 0
    for n in PACKED:
        r, c = shard_shapes[n]
        assert (r * c) % PACK_W == 0
        plan[n] = (off, r * c // PACK_W, (r, c))
        off += r * c // PACK_W
    total = -(-off // 32) * 32
    return plan, total


def _pack(tensors, plan, total, dtype):
    parts = [tensors[n].astype(dtype).reshape(-1, PACK_W) for n in PACKED]
    used = sum(p.shape[0] for p in parts)
    if total > used:
        parts.append(jnp.zeros((total - used, PACK_W), dtype))
    return jnp.concatenate(parts, axis=0)


def _unpack_full(g4, plan):
    out = {}
    for n in PACKED:
        off, nr, (r, c) = plan[n]
        sh = g4[:, off:off + nr].reshape(4, r, c)
        out[n] = jnp.concatenate(list(sh), axis=0 if n in ROW_SHARDED else 1)
    return out


def _pack_grads(grads, plan, total):
    blocks = []
    for s in range(4):
        t = {}
        for n in PACKED:
            _, _, (r, c) = plan[n]
            t[n] = grads[n][s * r:(s + 1) * r] if n in ROW_SHARDED else grads[n][:, s * c:(s + 1) * c]
        blocks.append(_pack(t, plan, total, BF16))
    return jnp.stack(blocks)


def _unpack_shard(p, plan):
    return {n: p[plan[n][0]:plan[n][0] + plan[n][1]].reshape(plan[n][2]) for n in PACKED}


def _swap_cols(w):
    hlf = w.shape[1] // 2
    return jnp.concatenate([-w[:, hlf:], w[:, :hlf]], axis=1)


def _unswap_cols(dw):
    hlf = dw.shape[1] // 2
    return jnp.concatenate([dw[:, hlf:], -dw[:, :hlf]], axis=1)


def _layer_weights(full, d):
    zl = _z_layout(d)
    f = full["ffn1_w_down"].shape[0]
    w_in = full["w_in"]
    o = 0
    cols = {}
    for nm, wd in (("cq", Q_LORA), ("ckv", KV_LORA), ("kpe", QK_ROPE), ("hq", 512), ("hf", 512), ("hi", 512),
                   ("hg", 512), ("ga", d), ("gb", d)):
        cols[nm] = w_in[:, o:o + wd]
        o += wd
    zc = lambda n: jnp.zeros((d, n), BF16)
    win_p = jnp.concatenate(
        [cols["cq"], zc(QK_NOPE), cols["kpe"], zc(32), cols["ckv"], zc(QK_NOPE), _swap_cols(cols["kpe"]), zc(32),
         zc(LANES), cols["ga"], cols["gb"], cols["hq"], cols["hf"], cols["hi"], cols["hg"]], axis=1)
    assert win_p.shape[1] == zl["total"]
    wq = full["w_uq"].reshape(Q_LORA, MLA_HEADS, QK_NOPE + QK_ROPE)
    nope, rope = wq[:, :, :QK_NOPE], wq[:, :, QK_NOPE:]
    z32 = jnp.zeros((Q_LORA, MLA_HEADS, 32), BF16)
    z64 = jnp.zeros((Q_LORA, MLA_HEADS, QK_NOPE), BF16)
    rope_sw = jnp.concatenate([-rope[:, :, 16:], rope[:, :, :16]], axis=2)
    wqa = jnp.concatenate([nope, rope, z32], axis=2).reshape(Q_LORA, QW)
    wqb = jnp.concatenate([z64, rope_sw, z32], axis=2).reshape(Q_LORA, QW)
    wpa = full["w_proj_attn"].reshape(MLA_HEADS, V_HEAD, d)
    wpa_p = jnp.concatenate([jnp.zeros_like(wpa), wpa], axis=1).reshape(QW, d)
    return dict(
        wg1=full["ffn1_w_gu"][:, :f], wu1=full["ffn1_w_gu"][:, f:], wd1=full["ffn1_w_down"],
        wg2=full["ffn2_w_gu"][:, :f], wu2=full["ffn2_w_gu"][:, f:], wd2=full["ffn2_w_down"],
        win=win_p, wq2=jnp.concatenate([wqa, wqb], axis=1), wqa=wqa, wqb=wqb, wkv=full["w_ukv"], wpa=wpa_p,
        wpr=full["w_proj_rec"], wout=full["w_out"])


def _natural_grads(g, d):
    zl = _z_layout(d)
    dwin = g["win"]
    kpe = dwin[:, Z_KPA + QK_NOPE:Z_KPA + QK_NOPE + QK_ROPE] + _unswap_cols(
        dwin[:, Z_KPB + QK_NOPE:Z_KPB + QK_NOPE + QK_ROPE])
    w_in = jnp.concatenate(
        [dwin[:, Z_Q:Z_Q + Q_LORA], dwin[:, Z_KV:Z_KV + KV_LORA], kpe, dwin[:, zl["hq"]:zl["hq"] + 2048],
         dwin[:, zl["ga"]:zl["ga"] + 2 * d]], axis=1)
    qa = g["wqa"].reshape(Q_LORA, MLA_HEADS, HEAD_W)
    qb = g["wqb"].reshape(Q_LORA, MLA_HEADS, HEAD_W)[:, :, QK_NOPE:QK_NOPE + QK_ROPE]
    rope = qa[:, :, QK_NOPE:QK_NOPE + QK_ROPE] + jnp.concatenate([qb[:, :, 16:], -qb[:, :, :16]], axis=2)
    w_uq = jnp.concatenate([qa[:, :, :QK_NOPE], rope], axis=2).reshape(Q_LORA, -1)
    wpa = g["wpa"].reshape(MLA_HEADS, 2 * V_HEAD, d)[:, V_HEAD:].reshape(MLA_HEADS * V_HEAD, d)
    return dict(
        ffn1_w_gu=jnp.concatenate([g["wg1"], g["wu1"]], axis=1), ffn1_w_down=g["wd1"],
        ffn2_w_gu=jnp.concatenate([g["wg2"], g["wu2"]], axis=1), ffn2_w_down=g["wd2"],
        w_in=w_in, w_uq=w_uq, w_ukv=g["wkv"], w_proj_attn=wpa, w_proj_rec=g["wpr"], w_out=g["wout"])


def _rope_tables(lp):
    pos = jnp.maximum(jnp.arange(lp) - FRONT, 0).astype(F32)
    half = QK_ROPE // 2
    inv = ROPE_THETA ** (-jnp.arange(half, dtype=F32) / half)
    ang = pos[:, None] * inv[None, :]
    cos, sin = jnp.cos(ang), jnp.sin(ang)
    cos_t = jnp.concatenate([jnp.ones((lp, QK_NOPE), F32), cos, cos, jnp.zeros((lp, 32), F32)], axis=1)
    sin_t = jnp.concatenate([jnp.zeros((lp, QK_NOPE), F32), sin, sin, jnp.zeros((lp, 32), F32)], axis=1)
    return cos_t, sin_t


def _lower_bounds(raw):
    p = jax.nn.softmax(raw.astype(F32), axis=0)
    return jnp.cumsum(p, axis=0) - p[0:1]


def _ffn_fwd(h, nw, wg, wu, wd, tag):
    a, a_t = _rmsnorm_fwd(h, nw, width=h.shape[1], col_block=0, transposed=True, name=f"norm_{tag}")
    g = _matmul([(a, wg)], "nn", out_dtype=BF16, name=f"gate_{tag}")
    u = _matmul([(a, wu)], "nn", out_dtype=BF16, name=f"up_{tag}")
    act, act_t = _swiglu_fwd(g, u, name=f"swiglu_{tag}")
    out = _matmul([(act, wd)], "nn", res=h, scale=0.5, name=f"down_{tag}")
    return out, dict(h=h, a_t=a_t, g=g, u=u, act_t=act_t)


def _ffn_bwd(dout, sv, nw, wg, wu, wd, lv, tag):
    dact = _matmul([(dout, wd)], "nt", scale=0.5, out_dtype=BF16, name=f"ddown_{tag}")
    dwd = _matmul([(sv["act_t"], dout)], "nn", scale=0.5, name=f"dwdown_{tag}")
    dg, du = _swiglu_bwd(dact, sv["g"], sv["u"], name=f"dswiglu_{tag}")
    dwg = _matmul([(sv["a_t"], dg)], "nn", name=f"dwgate_{tag}")
    dwu = _matmul([(sv["a_t"], du)], "nn", name=f"dwup_{tag}")
    da = _matmul([(dg, wg), (du, wu)], "nt", name=f"dnormed_{tag}")
    dh, dn = _rmsnorm_bwd(sv["h"], nw, da, width=da.shape[1], col_block=0, lv=lv, dres=dout, name=f"dnorm_{tag}")
    return dh, dn, dwg, dwu, dwd


def _layer_fwd(h0, lw, sm, lb, tabs, consts, lv, l, ride=None):
    d = h0.shape[1]
    zl = _z_layout(d)
    cos_t, sin_t = tabs[:2]
    h1, s1 = _ffn_fwd(h0, sm["ffn1_norm"], lw["wg1"], lw["wu1"], lw["wd1"], f"ffn1_{l}")
    um, um_t = _rmsnorm_fwd(h1, sm["mix_norm"], width=d, col_block=0, transposed=True, name=f"norm_mix_{l}")
    z = _matmul([(um, lw["win"])], "nn", name=f"inproj_{l}")
    qn = _rmsnorm_fwd(z, sm["q_norm"], width=Q_LORA, col_block=Z_Q // Q_LORA, name=f"norm_q_{l}")
    kvn = _rmsnorm_fwd(z, sm["kv_norm"], width=KV_LORA, col_block=Z_KV // KV_LORA, name=f"norm_kv_{l}")
    q2 = _matmul([(qn, lw["wq2"])], "nn", name=f"uq_{l}")
    kv = _matmul([(kvn, lw["wkv"])], "nn", name=f"ukv_{l}")
    q, k, v = _qkv_prep_fwd(q2, kv, z, cos_t, sin_t, name=f"qkv_{l}")
    o, lse = _attn_fwd(q, k, v, tabs[2], lv=lv, name=f"attn_{l}")
    ya = _matmul([(o, lw["wpa"])], "nn", name=f"proj_attn_{l}")
    o_pre, ob, states, rode = _hgrn_fwd(z, lb, sm["hg_norm"], consts, zl=zl, lv=lv, name=f"hgrn_{l}", ride=ride)
    yb = _matmul([(ob, lw["wpr"])], "nn", name=f"proj_rec_{l}")
    mg, mg_t = _merge_fwd(ya, yb, z, zl=zl, name=f"merge_{l}")
    h2 = _matmul([(mg, lw["wout"])], "nn", res=h1, name=f"out_{l}")
    h3, s2 = _ffn_fwd(h2, sm["ffn2_norm"], lw["wg2"], lw["wu2"], lw["wd2"], f"ffn2_{l}")
    saved = dict(s1=s1, s2=s2, h1=h1, um_t=um_t, z=z, qn=qn, kvn=kvn, q=q, k=k, v=v, o=o, lse=lse, ya=ya, yb=yb,
                 o_pre=o_pre, ob=ob, states=states, mg_t=mg_t)
    return h3, saved, rode


def _layer_bwd(dh3, sv, lw, sm, lb, tabs, consts, lv, l, ride=None):
    d = dh3.shape[1]
    lp = dh3.shape[0]
    zl = _z_layout(d)
    cos_t, sin_t = tabs[:2]
    z = sv["z"]
    g = {}
    sg = {}
    dh2, sg["ffn2_norm"], g["wg2"], g["wu2"], g["wd2"] = _ffn_bwd(
        dh3, sv["s2"], sm["ffn2_norm"], lw["wg2"], lw["wu2"], lw["wd2"], lv, f"ffn2_{l}")
    dmg = _matmul([(dh2, lw["wout"])], "nt", name=f"dmerged_{l}")
    g["wout"] = _matmul([(sv["mg_t"], dh2)], "nn", name=f"dwout_{l}")
    dya, dyb, dga, dgb = _merge_bwd(dmg, sv["ya"], sv["yb"], z, zl=zl, name=f"dmerge_{l}")
    doa = _matmul([(dya, lw["wpa"])], "nt", name=f"dattn_out_{l}")
    g["wpa"] = _matmul([(sv["o"], dya)], "tn", name=f"dwproj_attn_{l}")
    dob = _matmul([(dyb, lw["wpr"])], "nt", name=f"drec_out_{l}")
    g["wpr"] = _matmul([(sv["ob"], dyb)], "tn", name=f"dwproj_rec_{l}")
    dhq, dhf, dhi, dhg, dlb, sg["hg_norm"], rode = _hgrn_bwd(
        z, sv["o_pre"], dob, sv["states"], lb, sm["hg_norm"], consts, zl=zl, lv=lv, name=f"dhgrn_{l}", ride=ride)
    delta = _attn_delta(doa, sv["o"], name=f"attn_delta_{l}")
    dq, dk, dv = _attn_bwd(sv["q"], sv["k"], sv["v"], doa, sv["lse"], delta, tabs[2], name=f"dattn_{l}")
    dqa, dqb, dkv, dza, dzb = _qkv_prep_bwd(dq, dk, dv, cos_t, sin_t, name=f"dqkv_{l}")
    dqn = _matmul([(dqa, lw["wqa"]), (dqb, lw["wqb"])], "nt", name=f"dqn_{l}")
    g["wqa"] = _matmul([(sv["qn"], dqa)], "tn", name=f"dwqa_{l}")
    g["wqb"] = _matmul([(sv["qn"], dqb)], "tn", name=f"dwqb_{l}")
    dkvn = _matmul([(dkv, lw["wkv"])], "nt", name=f"dkvn_{l}")
    g["wkv"] = _matmul([(sv["kvn"], dkv)], "tn", name=f"dwkv_{l}")
    dzq, sg["q_norm"] = _rmsnorm_bwd(z, sm["q_norm"], dqn, width=Q_LORA, col_block=Z_Q // Q_LORA, lv=lv,
                                     name=f"dnorm_q_{l}")
    dzkv, sg["kv_norm"] = _rmsnorm_bwd(z, sm["kv_norm"], dkvn, width=KV_LORA, col_block=Z_KV // KV_LORA, lv=lv,
                                       name=f"dnorm_kv_{l}")
    dz = jnp.concatenate([dzq, dza, dzkv, dzb, jnp.zeros((lp, LANES), F32), dga, dgb, dhq, dhf, dhi, dhg],
                         axis=1).astype(BF16)
    dum = _matmul([(dz, lw["win"])], "nt", name=f"dmixed_{l}")
    g["win"] = _matmul([(sv["um_t"], dz)], "nn", name=f"dwin_{l}")
    dh1, sg["mix_norm"] = _rmsnorm_bwd(sv["h1"], sm["mix_norm"], dum, width=d, col_block=0, lv=lv, dres=dh2,
                                       name=f"dnorm_mix_{l}")
    dh0, sg["ffn1_norm"], g["wg1"], g["wu1"], g["wd1"] = _ffn_bwd(
        dh1, sv["s1"], sm["ffn1_norm"], lw["wg1"], lw["wu1"], lw["wd1"], lv, f"ffn1_{l}")
    return dh0, g, sg, dlb, rode


WEIGHTS = ("meta_tokens", "ffn1_norm", "ffn1_w_gu", "ffn1_w_down", "mix_norm", "w_in", "q_norm", "kv_norm", "w_uq",
           "w_ukv", "hg_lb_raw", "hg_norm", "w_proj_attn", "w_proj_rec", "w_out", "ffn2_norm", "ffn2_w_gu",
           "ffn2_w_down", "final_norm")
SMALL = ("ffn1_norm", "mix_norm", "q_norm", "kv_norm", "hg_lb_raw", "hg_norm", "ffn2_norm")


def _small_rows(vals):
    pad = lambda a: jnp.pad(a, ((0, -a.shape[0] % 8), (0, PACK_W - a.shape[1])))
    rows = [pad(vals[n]) for n in SMALL]
    rows.append(pad(vals["final_norm"][None, :]))
    rows.append(pad(vals["meta_tokens"]))
    rows.append(pad(vals["loss"].reshape(1, 1)))
    return jnp.concatenate(rows, axis=0)


def _small_unrows(s, d, widths):
    out, o = {}, 0
    for n in SMALL:
        out[n] = s[o:o + DEPTH, :widths[n]]
        o += -(-DEPTH // 8) * 8
    out["final_norm"] = s[o, :d]
    o += 8
    out["meta_tokens"] = s[o:o + N_META, :d]
    o += -(-N_META // 8) * 8
    out["loss"] = s[o, 0]
    return out


def _step(args):
    x = args["x"][0]
    seq, d = x.shape
    assert d <= PACK_W
    lv = ROW_X + seq
    lp = -(-lv // ROW_TILE) * ROW_TILE
    xi, yi, ci = _coords()
    kidx = (2 * xi + yi).astype(jnp.int32).reshape(1)
    cidx = ci.astype(jnp.int32).reshape(1)
    consts = _hgrn_consts()
    tabs = (*_rope_tables(lp), _attn_consts(lp))

    shard_shapes = {n: args[n].shape[1:] for n in PACKED}
    plan, total = _pack_plan(shard_shapes)
    packed = [_pack({n: args[n][l] for n in PACKED}, plan, total, BF16) for l in range(DEPTH)]
    mt = args["meta_tokens"]
    mt4 = _gather_chips(mt, name="gather_meta")
    meta = jnp.concatenate(list(mt4), axis=1)

    sm = [{n: args[n][l] for n in SMALL} for l in range(DEPTH)]
    lbs = _lower_bounds(args["hg_lb_raw"])

    h = jnp.concatenate([jnp.zeros((FRONT, d), F32), meta, x, jnp.zeros((lp - lv, d), F32)], axis=0)
    saved, lws = [], []
    g4 = _gather_chips(packed[0], name="gather_0")
    for l in range(DEPTH):
        lws.append(_layer_weights(_unpack_full(g4, plan), d))
        ride = _GatherRide(packed[l + 1]) if l + 1 < DEPTH else None
        h, sv, rode = _layer_fwd(h, lws[l], sm[l], lbs[l], tabs, consts, lv, l, ride)
        saved.append(sv)
        if ride is not None:
            g4 = _own_block(_gather_forward(rode[0], name=f"gather_fwd_{l + 1}"), packed[l + 1])
    tpad = jnp.pad(args["loss_target"][0], ((ROW_X, lp - lv), (0, 0)))
    dh, loss, dfinal = _loss_head(h, args["final_norm"], tpad, lv=lv, name="loss_head")

    small = {n: [None] * DEPTH for n in SMALL}
    dlbs = [None] * DEPTH
    shard_grads = [None] * DEPTH
    waiting = None
    for l in reversed(range(DEPTH)):
        ride = _ScatterRide(waiting) if waiting is not None else None
        dh, g, sg, dlbs[l], rode = _layer_bwd(dh, saved[l], lws[l], sm[l], lbs[l], tabs, consts, lv, l, ride)
        if ride is not None:
            shard_grads[l + 1] = _unpack_shard(_rs_end(waiting, rode[0], kidx, tag=str(l + 1)), plan)
        for n in sg:
            small[n][l] = sg[n]
        waiting = _rs_begin(_pack_grads(_natural_grads(g, d), plan, total), cidx, tag=str(l))
    land = _scatter_chips(waiting, name="rs_scatter_0")
    shard_grads[0] = _unpack_shard(_rs_end(waiting, land, kidx, tag="0"), plan)

    _, lb_vjp = jax.vjp(_lower_bounds, args["hg_lb_raw"])
    small_vals = {n: jnp.stack(small[n]) for n in SMALL if n != "hg_lb_raw"}
    small_vals["hg_lb_raw"] = lb_vjp(jnp.stack(dlbs))[0]
    small_vals["final_norm"] = dfinal
    small_vals["meta_tokens"] = dh[FRONT:ROW_X]
    small_vals["loss"] = loss
    widths = {n: args[n].shape[1] for n in SMALL}
    tot = _small_unrows(_allreduce_small(_small_rows(small_vals), name="allreduce_small"), d, widths)

    grads = {n: jnp.stack([shard_grads[l][n] for l in range(DEPTH)]) for n in PACKED}
    for n in SMALL:
        grads[n] = tot[n]
    grads["final_norm"] = tot["final_norm"]
    mcols = mt.shape[1]
    grads["meta_tokens"] = lax.dynamic_slice_in_dim(tot["meta_tokens"], (2 * xi + yi) * mcols, mcols, axis=1)
    grad_x = dh[ROW_X:lv][None]

    delta, new_m, new_v = {}, {}, {}
    for n in WEIGHTS:
        delta[n], new_m[n], new_v[n] = _adamw(args[n], grads[n], args["m_" + n], args["v_" + n], name=f"adamw_{n}")
    return (tot["loss"], grad_x, *[grads[n] for n in WEIGHTS], *[delta[n] for n in WEIGHTS],
            *[new_m[n] for n in WEIGHTS], *[new_v[n] for n in WEIGHTS])


def kernel(x, meta_tokens, ffn1_norm, ffn1_w_gu, ffn1_w_down, mix_norm, w_in, q_norm, kv_norm, w_uq, w_ukv, hg_lb_raw, hg_norm, w_proj_attn, w_proj_rec, w_out, ffn2_norm, ffn2_w_gu, ffn2_w_down, final_norm, loss_target, m_meta_tokens, m_ffn1_norm, m_ffn1_w_gu, m_ffn1_w_down, m_mix_norm, m_w_in, m_q_norm, m_kv_norm, m_w_uq, m_w_ukv, m_hg_lb_raw, m_hg_norm, m_w_proj_attn, m_w_proj_rec, m_w_out, m_ffn2_norm, m_ffn2_w_gu, m_ffn2_w_down, m_final_norm, v_meta_tokens, v_ffn1_norm, v_ffn1_w_gu, v_ffn1_w_down, v_mix_norm, v_w_in, v_q_norm, v_kv_norm, v_w_uq, v_w_ukv, v_hg_lb_raw, v_hg_norm, v_w_proj_attn, v_w_proj_rec, v_w_out, v_ffn2_norm, v_ffn2_w_gu, v_ffn2_w_down, v_final_norm):
    return _step(dict(locals()))
```

```python
import functools
import math

import numpy as np
import jax
import jax.numpy as jnp
from jax import lax
from jax.experimental import pallas as pl
from jax.experimental.pallas import tpu as pltpu

F32 = jnp.float32
BF16 = jnp.bfloat16

N_META = 16
MLA_HEADS = 8
Q_LORA = 384
KV_LORA = 256
QK_NOPE = 64
QK_ROPE = 32
V_HEAD = 64
ROPE_THETA = 10000.0
HG_HEADS = 4
HG_D = 128
HG_CHUNK = 64
EPS = 1e-6
NEG_BIG = -1e30
F_MIN = 1e-20
DEPTH = 4

ADAM_LR = 0.001
ADAM_B1 = 0.9
ADAM_B2 = 0.999
ADAM_EPS = 1e-08
ADAM_WD = 0.01
ADAM_STEP = 10

LANES = 128
FRONT = (-N_META) % HG_CHUNK
ROW_X = FRONT + N_META
ROW_TILE = 640
HEAD_W = 128
QW = MLA_HEADS * HEAD_W
PREP_HEADS = 4
VMEM_LIMIT = 56 * 1024 * 1024
MATMUL_VMEM = 42 * 1024 * 1024
PACK_W = 1024
MESH = pl.DeviceIdType.MESH

Z_Q, Z_KPA, Z_KV, Z_KPB, Z_PAD, Z_GA = 0, 384, 512, 768, 896, 1024


def _z_layout(d):
    ga = Z_GA
    gb = ga + d
    hq = gb + d
    hf = hq + 512
    hi = hf + 512
    hg = hi + 512
    return dict(ga=ga, gb=gb, hq=hq, hf=hf, hi=hi, hg=hg, total=hg + 512)


def _pick(dim, cap, mult=LANES):
    if dim <= cap:
        return dim
    best = None
    for t in range(mult, cap + 1, mult):
        if dim % t == 0:
            best = t
    assert best is not None, (dim, cap, mult)
    return best


def _params(*sem):
    return pltpu.CompilerParams(dimension_semantics=sem, vmem_limit_bytes=VMEM_LIMIT)


def _sigmoid(x):
    return 1.0 / (1.0 + jnp.exp(-x))


def _row_valid(row0, n, lv):
    r = row0 + lax.broadcasted_iota(jnp.int32, (n, 1), 0)
    return ((r >= FRONT) & (r < lv)).astype(F32)


_DIMS = {"nn": (((1,), (0,)), ((), ())), "nt": (((1,), (1,)), ((), ())), "tn": (((0,), (0,)), ((), ()))}


def _matmul(pairs, mode, *, name, out_dtype=F32, res=None, scale=1.0):
    a0, b0 = pairs[0]
    if mode == "nn":
        (m, k), n = a0.shape, b0.shape[1]
    elif mode == "nt":
        (m, k), n = a0.shape, b0.shape[0]
    else:
        (k, m), n = a0.shape, b0.shape[1]
    if mode == "tn":
        tm, tn, tk = _pick(m, 1024), _pick(n, 1408), _pick(k, ROW_TILE, 8)
    else:
        if k > m:
            tm, tn, kcap = _pick(m, 1408, 16), _pick(n, 1408), 1664
        else:
            tm, tn, kcap = _pick(m, ROW_TILE, 8), _pick(n, 2816), 2816
        out_b = jnp.dtype(out_dtype).itemsize
        per_k = len(pairs) * 2 * (tm * a0.dtype.itemsize + tn * b0.dtype.itemsize)
        fixed = tm * tn * (2 * out_b + 4 + (8 if res is not None else 0))
        tk = _pick(k, kcap)
        while tk > LANES and fixed + per_k * tk > MATMUL_VMEM:
            tk = _pick(k, tk - LANES)
    nk = k // tk
    npair = len(pairs)
    dims = _DIMS[mode]

    def body(*refs):
        ins = refs[:2 * npair]
        pos = 2 * npair
        res_ref = None
        if res is not None:
            res_ref = refs[pos]
            pos += 1
        o_ref = refs[pos]
        kk = pl.program_id(2)

        part = None
        for p in range(npair):
            a = ins[2 * p][...].astype(BF16)
            b = ins[2 * p + 1][...].astype(BF16)
            d = lax.dot_general(a, b, dims, preferred_element_type=F32)
            part = d if part is None else part + d

        def finish(r):
            if scale != 1.0:
                r = r * scale
            if res_ref is not None:
                r = r + res_ref[...]
            o_ref[...] = r.astype(out_dtype)

        if nk == 1:
            finish(part)
            return
        acc = refs[pos + 1]

        @pl.when(kk == 0)
        def _():
            acc[...] = part

        @pl.when(kk > 0)
        def _():
            acc[...] += part

        @pl.when(kk == nk - 1)
        def _():
            finish(acc[...])

    if mode == "nn":
        a_spec = pl.BlockSpec((tm, tk), lambda i, j, q: (i, q))
        b_spec = pl.BlockSpec((tk, tn), lambda i, j, q: (q, j))
    elif mode == "nt":
        a_spec = pl.BlockSpec((tm, tk), lambda i, j, q: (i, q))
        b_spec = pl.BlockSpec((tn, tk), lambda i, j, q: (j, q))
    else:
        a_spec = pl.BlockSpec((tk, tm), lambda i, j, q: (q, i))
        b_spec = pl.BlockSpec((tk, tn), lambda i, j, q: (q, j))
    o_spec = pl.BlockSpec((tm, tn), lambda i, j, q: (i, j))
    in_specs, args = [], []
    for a, b in pairs:
        in_specs += [a_spec, b_spec]
        args += [a, b]
    if res is not None:
        in_specs.append(o_spec)
        args.append(res)
    return pl.pallas_call(
        body, name=name, grid=(m // tm, n // tn, nk), in_specs=in_specs, out_specs=o_spec,
        out_shape=jax.ShapeDtypeStruct((m, n), out_dtype),
        scratch_shapes=[pltpu.VMEM((tm, tn), F32)] if nk > 1 else [],
        compiler_params=_params("parallel", "parallel", "arbitrary"),
    )(*args)


def _rmsnorm_fwd(x, w, *, width, col_block, name, transposed=False):
    lp = x.shape[0]
    tm = _pick(lp, ROW_TILE, 8)

    def body(x_ref, w_ref, o_ref, *ot_ref):
        xv = x_ref[...]
        r = lax.rsqrt(jnp.mean(xv * xv, axis=-1, keepdims=True) + EPS)
        y = xv * r * w_ref[...]
        o_ref[...] = y.astype(BF16)
        if transposed:
            ot_ref[0][...] = y.T.astype(BF16)

    out_specs = [pl.BlockSpec((tm, width), lambda i: (i, 0))]
    out_shape = [jax.ShapeDtypeStruct((lp, width), BF16)]
    if transposed:
        out_specs.append(pl.BlockSpec((width, tm), lambda i: (0, i)))
        out_shape.append(jax.ShapeDtypeStruct((width, lp), BF16))
    outs = pl.pallas_call(
        body, name=name, grid=(lp // tm,),
        in_specs=[pl.BlockSpec((tm, width), lambda i: (i, col_block)), pl.BlockSpec((1, width), lambda i: (0, 0))],
        out_specs=out_specs, out_shape=out_shape, compiler_params=_params("parallel"),
    )(x, w.reshape(1, width))
    return tuple(outs) if transposed else outs[0]


def _rmsnorm_bwd(x, w, dy, *, width, col_block, lv, name, dres=None):
    lp = x.shape[0]
    tm = _pick(lp, ROW_TILE, 8)

    def body(*refs):
        if dres is None:
            x_ref, w_ref, dy_ref, dx_ref, dw_ref = refs
            dres_ref = None
        else:
            x_ref, w_ref, dy_ref, dres_ref, dx_ref, dw_ref = refs
        i = pl.program_id(0)
        xv = x_ref[...]
        dyv = dy_ref[...] * _row_valid(i * tm, tm, lv)
        r = lax.rsqrt(jnp.mean(xv * xv, axis=-1, keepdims=True) + EPS)
        wdy = dyv * w_ref[...]
        dx = r * wdy - xv * (r * r * r) * jnp.mean(xv * wdy, axis=-1, keepdims=True)
        if dres_ref is not None:
            dx = dx + dres_ref[...]
        dx_ref[...] = dx

        @pl.when(i == 0)
        def _():
            dw_ref[...] = jnp.zeros_like(dw_ref)

        dw_ref[...] += jnp.sum(dyv * xv * r, axis=0, keepdims=True)

    row = pl.BlockSpec((tm, width), lambda i: (i, 0))
    in_specs = [pl.BlockSpec((tm, width), lambda i: (i, col_block)), pl.BlockSpec((1, width), lambda i: (0, 0)), row]
    args = [x, w.reshape(1, width), dy]
    if dres is not None:
        in_specs.append(row)
        args.append(dres)
    dx, dw = pl.pallas_call(
        body, name=name, grid=(lp // tm,), in_specs=in_specs,
        out_specs=[row, pl.BlockSpec((1, width), lambda i: (0, 0))],
        out_shape=[jax.ShapeDtypeStruct((lp, width), F32), jax.ShapeDtypeStruct((1, width), F32)],
        compiler_params=_params("arbitrary"),
    )(*args)
    return dx, dw[0]


def _swiglu_fwd(g, u, *, name):
    lp, f = g.shape
    tm, tf = _pick(lp, ROW_TILE, 8), _pick(f, 1408)

    def body(g_ref, u_ref, o_ref, ot_ref):
        gv = g_ref[...].astype(F32)
        act = gv * _sigmoid(gv) * u_ref[...].astype(F32)
        o_ref[...] = act.astype(BF16)
        ot_ref[...] = act.T.astype(BF16)

    spec = pl.BlockSpec((tm, tf), lambda i, j: (i, j))
    return pl.pallas_call(
        body, name=name, grid=(lp // tm, f // tf), in_specs=[spec, spec],
        out_specs=[spec, pl.BlockSpec((tf, tm), lambda i, j: (j, i))],
        out_shape=[jax.ShapeDtypeStruct((lp, f), BF16), jax.ShapeDtypeStruct((f, lp), BF16)],
        compiler_params=_params("parallel", "parallel"),
    )(g, u)


def _swiglu_bwd(dact, g, u, *, name):
    lp, f = g.shape
    tm, tf = _pick(lp, ROW_TILE, 8), _pick(f, 1408)

    def body(d_ref, g_ref, u_ref, dg_ref, du_ref):
        gv, dv = g_ref[...].astype(F32), d_ref[...].astype(F32)
        s = _sigmoid(gv)
        dg_ref[...] = (dv * u_ref[...].astype(F32) * s * (1.0 + gv * (1.0 - s))).astype(BF16)
        du_ref[...] = (dv * gv * s).astype(BF16)

    spec = pl.BlockSpec((tm, tf), lambda i, j: (i, j))
    return pl.pallas_call(
        body, name=name, grid=(lp // tm, f // tf), in_specs=[spec, spec, spec], out_specs=[spec, spec],
        out_shape=[jax.ShapeDtypeStruct((lp, f), BF16)] * 2, compiler_params=_params("parallel", "parallel"),
    )(dact, g, u)


def _merge_fwd(ya, yb, z, *, zl, name):
    lp, d = ya.shape
    tm, td = _pick(lp, ROW_TILE, 8), _pick(d, 512)
    oa, ob = zl["ga"] // td, zl["gb"] // td

    def body(ya_ref, yb_ref, ga_ref, gb_ref, o_ref, ot_ref):
        mg = _sigmoid(ga_ref[...]) * ya_ref[...] + _sigmoid(gb_ref[...]) * yb_ref[...]
        o_ref[...] = mg.astype(BF16)
        ot_ref[...] = mg.T.astype(BF16)

    spec = pl.BlockSpec((tm, td), lambda i, j: (i, j))
    return pl.pallas_call(
        body, name=name, grid=(lp // tm, d // td),
        in_specs=[spec, spec, pl.BlockSpec((tm, td), lambda i, j: (i, oa + j)),
                  pl.BlockSpec((tm, td), lambda i, j: (i, ob + j))],
        out_specs=[spec, pl.BlockSpec((td, tm), lambda i, j: (j, i))],
        out_shape=[jax.ShapeDtypeStruct((lp, d), BF16), jax.ShapeDtypeStruct((d, lp), BF16)],
        compiler_params=_params("parallel", "parallel"),
    )(ya, yb, z, z)


def _merge_bwd(dmg, ya, yb, z, *, zl, name):
    lp, d = ya.shape
    tm, td = _pick(lp, ROW_TILE, 8), _pick(d, 512)
    oa, ob = zl["ga"] // td, zl["gb"] // td

    def body(d_ref, ya_ref, yb_ref, ga_ref, gb_ref, dya_ref, dyb_ref, dga_ref, dgb_ref):
        dv = d_ref[...]
        sa, sb = _sigmoid(ga_ref[...]), _sigmoid(gb_ref[...])
        dya_ref[...] = (dv * sa).astype(BF16)
        dyb_ref[...] = (dv * sb).astype(BF16)
        dga_ref[...] = dv * ya_ref[...] * sa * (1.0 - sa)
        dgb_ref[...] = dv * yb_ref[...] * sb * (1.0 - sb)

    spec = pl.BlockSpec((tm, td), lambda i, j: (i, j))
    return pl.pallas_call(
        body, name=name, grid=(lp // tm, d // td),
        in_specs=[spec, spec, spec, pl.BlockSpec((tm, td), lambda i, j: (i, oa + j)),
                  pl.BlockSpec((tm, td), lambda i, j: (i, ob + j))],
        out_specs=[spec] * 4,
        out_shape=[jax.ShapeDtypeStruct((lp, d), BF16)] * 2 + [jax.ShapeDtypeStruct((lp, d), F32)] * 2,
        compiler_params=_params("parallel", "parallel"),
    )(dmg, ya, yb, z, z)


def _qkv_prep_fwd(q2, kv, z, cos_t, sin_t, *, name):
    lp = q2.shape[0]
    tm = _pick(lp, ROW_TILE, 8)
    h = MLA_HEADS
    wd = PREP_HEADS * HEAD_W

    def body(qa_ref, qb_ref, kv_ref, za_ref, zb_ref, c_ref, s_ref, q_ref, k_ref, v_ref):
        c, s = c_ref[...], s_ref[...]
        lane = lax.broadcasted_iota(jnp.int32, (tm, HEAD_W), 1)
        kr = jnp.where(lane >= QK_NOPE, za_ref[...] * c + zb_ref[...] * s, 0.0)
        for g in range(PREP_HEADS):
            sl = slice(g * HEAD_W, (g + 1) * HEAD_W)
            q_ref[:, sl] = ((qa_ref[:, sl] * c + qb_ref[:, sl] * s) * Q_SCALE).astype(BF16)
            kvv = kv_ref[:, sl]
            k_ref[:, sl] = (jnp.where(lane < QK_NOPE, kvv, 0.0) + kr).astype(BF16)
            v_ref[:, sl] = jnp.where(lane >= QK_NOPE, kvv, 0.0).astype(BF16)

    blk = lambda w, f: pl.BlockSpec((tm, w), f)
    out = blk(wd, lambda i, j: (i, j))
    return pl.pallas_call(
        body, name=name, grid=(lp // tm, h // PREP_HEADS),
        in_specs=[blk(wd, lambda i, j: (i, j)), blk(wd, lambda i, j: (i, h // PREP_HEADS + j)),
                  blk(wd, lambda i, j: (i, j)),
                  blk(HEAD_W, lambda i, j: (i, Z_KPA // HEAD_W)), blk(HEAD_W, lambda i, j: (i, Z_KPB // HEAD_W)),
                  blk(HEAD_W, lambda i, j: (i, 0)), blk(HEAD_W, lambda i, j: (i, 0))],
        out_specs=[out, out, out], out_shape=[jax.ShapeDtypeStruct((lp, QW), BF16)] * 3,
        compiler_params=_params("parallel", "parallel"),
    )(q2, q2, kv, z, z, cos_t, sin_t)


def _qkv_prep_bwd(dq, dk, dv, cos_t, sin_t, *, name):
    lp = dq.shape[0]
    tm = _pick(lp, ROW_TILE, 8)
    h = MLA_HEADS
    wd = PREP_HEADS * HEAD_W

    def body(dq_ref, dk_ref, dv_ref, c_ref, s_ref, dqa_ref, dqb_ref, dkv_ref, dza_ref, dzb_ref):
        j = pl.program_id(1)
        c, s = c_ref[...], s_ref[...]
        lane = lax.broadcasted_iota(jnp.int32, (tm, HEAD_W), 1)
        dkr = jnp.zeros((tm, HEAD_W), F32)
        for g in range(PREP_HEADS):
            sl = slice(g * HEAD_W, (g + 1) * HEAD_W)
            dqv, dkv_ = dq_ref[:, sl], dk_ref[:, sl]
            dqa_ref[:, sl] = (dqv * c).astype(BF16)
            dqb_ref[:, sl] = (dqv * s).astype(BF16)
            dkv_ref[:, sl] = jnp.where(lane < QK_NOPE, dkv_, dv_ref[:, sl]).astype(BF16)
            dkr = dkr + jnp.where(lane >= QK_NOPE, dkv_, 0.0)

        @pl.when(j == 0)
        def _():
            dza_ref[...] = jnp.zeros_like(dza_ref)
            dzb_ref[...] = jnp.zeros_like(dzb_ref)

        dza_ref[...] += dkr * c
        dzb_ref[...] += dkr * s

    blk = lambda w, f: pl.BlockSpec((tm, w), f)
    per_head, shared = blk(wd, lambda i, j: (i, j)), blk(HEAD_W, lambda i, j: (i, 0))
    return pl.pallas_call(
        body, name=name, grid=(lp // tm, h // PREP_HEADS),
        in_specs=[per_head, per_head, per_head, shared, shared],
        out_specs=[per_head, per_head, per_head, shared, shared],
        out_shape=[jax.ShapeDtypeStruct((lp, QW), BF16)] * 3 + [jax.ShapeDtypeStruct((lp, HEAD_W), F32)] * 2,
        compiler_params=_params("parallel", "arbitrary"),
    )(dq, dk, dv, cos_t, sin_t)


def _attn_tile(lp):
    return _pick(lp, ROW_TILE, LANES)


Q_SCALE = (QK_NOPE + QK_ROPE) ** -0.5 * math.log2(math.e)
ATT_HP = 2
ATT_HP_FWD = 4


def _attn_consts(lp):
    t = _attn_tile(lp)
    nb = lp // t
    r = np.arange(t)
    causal = np.where(r[None, :] <= r[:, None], 0.0, NEG_BIG).astype(np.float32)
    front = np.where(r >= FRONT, 0.0, NEG_BIG).astype(np.float32)[None, :]
    diag = np.stack([np.minimum(causal, front), causal])
    qmaj = [(i, j) for i in range(nb) for j in range(i + 1)]
    kmaj = [(i, j) for j in range(nb) for i in range(j, nb)]
    tab = lambda pairs, c: jnp.asarray([p[c] for p in pairs], jnp.int32)
    return dict(diag=jnp.asarray(diag), front=jnp.asarray(front),
                fwd=(tab(qmaj, 0), tab(qmaj, 1)), bwd=(tab(kmaj, 0), tab(kmaj, 1)))


def _attn_fwd(q, k, v, ac, *, lv, name):
    lp = q.shape[0]
    t = _attn_tile(lp)
    nb = lp // t
    rep = t // HEAD_W
    qtab, ktab = ac["fwd"]

    def body(qt_ref, kt_ref, q_ref, k_ref, v_ref, bd_ref, bf_ref, o_ref, lse_ref, m_s, l_s, acc_s):
        step_id = pl.program_id(1)
        qb, kb = qt_ref[step_id], kt_ref[step_id]

        @pl.when(kb == 0)
        def _():
            m_s[...] = jnp.full_like(m_s, NEG_BIG)
            l_s[...] = jnp.zeros_like(l_s)
            acc_s[...] = jnp.zeros_like(acc_s)

        def step(bias):
            b = None if bias is None else bias()
            for hh in range(ATT_HP_FWD):
                sl = slice(hh * HEAD_W, (hh + 1) * HEAD_W)
                s = lax.dot_general(q_ref[:, sl], k_ref[:, sl], _DIMS["nt"], preferred_element_type=F32)
                if b is not None:
                    s = s + b
                m_prev = m_s[:, sl]
                m_new = jnp.maximum(m_prev, jnp.max(s, axis=-1, keepdims=True))
                alpha = jnp.exp2(m_prev - m_new)
                p = jnp.exp2(s - jnp.tile(m_new, (1, rep)))
                l_s[:, sl] = alpha * l_s[:, sl] + jnp.sum(p, axis=-1, keepdims=True)
                acc_s[:, sl] = alpha * acc_s[:, sl] + jnp.dot(p.astype(BF16), v_ref[:, sl],
                                                              preferred_element_type=F32)
                m_s[:, sl] = m_new

        @pl.when((kb > 0) & (kb < qb))
        def _():
            step(None)

        @pl.when((kb == 0) & (qb > 0))
        def _():
            step(lambda: bf_ref[...])

        @pl.when(kb == qb)
        def _():
            step(lambda: bd_ref[0])
            l = l_s[...]
            o_ref[...] = acc_s[...] / l * _row_valid(qb * t, t, lv)
            lse_ref[...] = m_s[...] + jnp.log2(l)

    wd = ATT_HP_FWD * HEAD_W
    qs = pl.BlockSpec((t, wd), lambda h, s, qt, kt: (qt[s], h))
    ks = pl.BlockSpec((t, wd), lambda h, s, qt, kt: (kt[s], h))
    grid_spec = pltpu.PrefetchScalarGridSpec(
        num_scalar_prefetch=2, grid=(MLA_HEADS // ATT_HP_FWD, int(qtab.shape[0])),
        in_specs=[qs, ks, ks, pl.BlockSpec((1, t, t), lambda h, s, qt, kt: (jnp.minimum(qt[s], 1), 0, 0)),
                  pl.BlockSpec((1, t), lambda h, s, qt, kt: (0, 0))],
        out_specs=[qs, qs],
        scratch_shapes=[pltpu.VMEM((t, wd), F32), pltpu.VMEM((t, wd), F32), pltpu.VMEM((t, wd), F32)])
    return pl.pallas_call(
        body, name=name, grid_spec=grid_spec, out_shape=[jax.ShapeDtypeStruct((lp, QW), F32)] * 2,
        compiler_params=_params("parallel", "arbitrary"),
    )(qtab, ktab, q, k, v, ac["diag"], ac["front"])


def _attn_delta(do, o, *, name):
    lp = do.shape[0]
    tm = _pick(lp, ROW_TILE, 8)
    wd = PREP_HEADS * HEAD_W

    def body(do_ref, o_ref, d_ref):
        for g in range(PREP_HEADS):
            sl = slice(g * HEAD_W, (g + 1) * HEAD_W)
            d_ref[:, sl] = jnp.broadcast_to(jnp.sum(do_ref[:, sl] * o_ref[:, sl], axis=-1, keepdims=True),
                                            (tm, HEAD_W))

    spec = pl.BlockSpec((tm, wd), lambda i, j: (i, j))
    return pl.pallas_call(
        body, name=name, grid=(lp // tm, MLA_HEADS // PREP_HEADS), in_specs=[spec, spec], out_specs=spec,
        out_shape=jax.ShapeDtypeStruct((lp, QW), F32), compiler_params=_params("parallel", "parallel"),
    )(do, o)


def _attn_bwd(q, k, v, do, lse, delta, ac, *, name):
    lp = q.shape[0]
    t = _attn_tile(lp)
    nb = lp // t
    rep = t // HEAD_W
    scale = (QK_NOPE + QK_ROPE) ** -0.5
    qtab, ktab = ac["bwd"]

    def body(qt_ref, kt_ref, q_ref, k_ref, v_ref, do_ref, lse_ref, dl_ref, bd_ref, bf_ref, dq_ref, dk_ref, dv_ref,
             dk_s, dv_s):
        step_id = pl.program_id(1)
        qb, kb = qt_ref[step_id], kt_ref[step_id]

        @pl.when(qb == kb)
        def _():
            dk_s[...] = jnp.zeros_like(dk_s)
            dv_s[...] = jnp.zeros_like(dv_s)

        def step(bias):
            b = None if bias is None else bias()
            rows = pl.ds(pl.multiple_of(qb * t, t), t)
            contribs = []
            for hh in range(ATT_HP):
                sl = slice(hh * HEAD_W, (hh + 1) * HEAD_W)
                qv, kv_, vv = q_ref[:, sl], k_ref[:, sl], v_ref[:, sl]
                dov = do_ref[:, sl].astype(BF16)
                s = lax.dot_general(qv, kv_, _DIMS["nt"], preferred_element_type=F32)
                if b is not None:
                    s = s + b
                p = jnp.exp2(s - jnp.tile(lse_ref[:, sl], (1, rep)))
                dv_s[:, sl] += lax.dot_general(p.astype(BF16), dov, _DIMS["tn"], preferred_element_type=F32)
                dp = lax.dot_general(dov, vv, _DIMS["nt"], preferred_element_type=F32)
                ds = (p * (dp - jnp.tile(dl_ref[:, sl], (1, rep))) * scale).astype(BF16)
                dk_s[:, sl] += lax.dot_general(ds, qv, _DIMS["tn"], preferred_element_type=F32)
                contribs.append(jnp.dot(ds, kv_, preferred_element_type=F32))
            contrib = jnp.concatenate(contribs, axis=1)

            @pl.when(kb == 0)
            def _():
                dq_ref[rows, :] = contrib

            @pl.when(kb > 0)
            def _():
                dq_ref[rows, :] += contrib

        @pl.when((kb > 0) & (kb < qb))
        def _():
            step(None)

        @pl.when((kb == 0) & (qb > 0))
        def _():
            step(lambda: bf_ref[...])

        @pl.when(kb == qb)
        def _():
            step(lambda: bd_ref[0])

        @pl.when(qb == nb - 1)
        def _():
            dk_ref[...] = dk_s[...] * (1.0 / Q_SCALE)
            dv_ref[...] = dv_s[...]

    wd = ATT_HP * HEAD_W
    qs = pl.BlockSpec((t, wd), lambda h, s, qt, kt: (qt[s], h))
    ks = pl.BlockSpec((t, wd), lambda h, s, qt, kt: (kt[s], h))
    dqs = pl.BlockSpec((lp, wd), lambda h, s, qt, kt: (0, h))
    grid_spec = pltpu.PrefetchScalarGridSpec(
        num_scalar_prefetch=2, grid=(MLA_HEADS // ATT_HP, int(qtab.shape[0])),
        in_specs=[qs, ks, ks, qs, qs, qs,
                  pl.BlockSpec((1, t, t), lambda h, s, qt, kt: (jnp.minimum(qt[s], 1), 0, 0)),
                  pl.BlockSpec((1, t), lambda h, s, qt, kt: (0, 0))],
        out_specs=[dqs, ks, ks],
        scratch_shapes=[pltpu.VMEM((t, wd), F32), pltpu.VMEM((t, wd), F32)])
    return pl.pallas_call(
        body, name=name, grid_spec=grid_spec, out_shape=[jax.ShapeDtypeStruct((lp, QW), F32)] * 3,
        compiler_params=_params("arbitrary", "arbitrary"),
    )(qtab, ktab, q, k, v, do, lse, delta, ac["diag"], ac["front"])


HG_UNROLL = 2
HG_LEVELS = (64, 32, 16, 8, 4, 2)
N_LEV = len(HG_LEVELS)


def _hgrn_consts():
    c = HG_CHUNK
    m = np.zeros((N_LEV + 2, c, c), np.float32)
    masks = np.zeros((N_LEV, c, c), np.float32)
    for li, p in enumerate(HG_LEVELS):
        for t in range(c):
            mid = (t // p) * p + p // 2
            if t >= mid:
                m[li, t, mid:t + 1] = 1.0
            else:
                m[li, t, t + 1:mid] = 1.0
            for s in range(c):
                if s // p == t // p and t >= mid and s < mid:
                    masks[li, t, s] = 1.0
    for t in range(c):
        m[N_LEV, t, :t + 1] = 1.0
        m[N_LEV + 1, t, t + 1:] = 1.0
    mall = m.reshape((N_LEV + 2) * c, c)
    return jnp.asarray(mall, BF16), jnp.asarray(mall.T.copy(), BF16), jnp.asarray(masks, F32)


def _split3(x):
    hi = x.astype(BF16)
    r = x - hi.astype(F32)
    mid = r.astype(BF16)
    lo = (r - mid.astype(F32)).astype(BF16)
    return jnp.concatenate([hi, mid, lo], axis=1)


def _sum3(e3):
    return e3[:, :HG_D] + e3[:, HG_D:2 * HG_D] + e3[:, 2 * HG_D:]


def _hgrn_chunk_fwd(hq, hf, hi, lb, valid, mall, masks, st):
    c = HG_CHUNK
    scale = HG_D ** -0.5
    sq = _sigmoid(hq)
    qv = hq * sq
    sg = _sigmoid(hf)
    f = lb + (1.0 - lb) * sg
    fc = jnp.maximum(f, F_MIN)
    lf = jnp.log(fc) * valid
    kv = (1.0 - lb) * (1.0 - sg) * valid
    e = _sum3(jnp.dot(mall, _split3(lf), preferred_element_type=F32))
    x = jnp.exp(e)
    a = jnp.zeros((c, c), F32)
    qe, ke = [], []
    for l in range(N_LEV):
        xl = x[l * c:(l + 1) * c]
        qe.append(qv * xl)
        ke.append(kv * xl)
        a = a + masks[l] * lax.dot_general(qe[l].astype(BF16), ke[l].astype(BF16), _DIMS["nt"],
                                           preferred_element_type=F32)
    row = lax.broadcasted_iota(jnp.int32, (c, c), 0)
    col = lax.broadcasted_iota(jnp.int32, (c, c), 1)
    a = a + jnp.where(row == col, jnp.sum(qv * kv, axis=-1, keepdims=True), 0.0)
    xb = x[N_LEV * c:(N_LEV + 1) * c]
    qb = qv * xb
    kb = kv * x[(N_LEV + 1) * c:]
    x_last = xb[c - 1:c]
    hib = hi.astype(BF16)
    o = scale * (jnp.dot(a.astype(BF16), hib, preferred_element_type=F32)
                 + lax.dot_general(qb.astype(BF16), st.astype(BF16), _DIMS["nt"], preferred_element_type=F32))
    st_new = st * x_last + lax.dot_general(hib, kb.astype(BF16), _DIMS["tn"], preferred_element_type=F32)
    saved = dict(sq=sq, qv=qv, sg=sg, f=f, fc=fc, kv=kv, x=x, a=a, qe=qe, ke=ke, qb=qb, kb=kb, x_last=x_last)
    return o, st_new, saved


def _split_ride(refs, n_in, n_out, n_scratch, ride):
    ri = len(ride.args) if ride else 0
    ro = len(ride.out_shape) if ride else 0
    a = n_in + ri
    b = a + n_out + ro
    c = b + n_scratch
    return refs[:n_in], refs[a:a + n_out], refs[b:c], refs[n_in:a] + refs[a + n_out:b] + refs[c:]


def _ride_call(ride):
    if ride is None:
        return [], [], [], [], []
    hbm = [HBM_SPEC] * len(ride.args)
    return hbm, list(ride.args), [HBM_SPEC] * len(ride.out_shape), list(ride.out_shape), list(ride.scratch)


def _hgrn_fwd(z, lb, nw, consts, *, zl, lv, name, ride=None):
    lp = z.shape[0]
    tb = _pick(lp, ROW_TILE, HG_CHUNK)
    ncb = tb // HG_CHUNK
    nb = lp // tb
    mall, _, masks = consts
    w = HG_HEADS * HG_D

    def body(*refs):
        ins, outs, (st_s,), ride_refs = _split_ride(refs, 8, 3, 1, ride)
        hq_ref, hf_ref, hi_ref, hg_ref, lb_ref, nw_ref, mall_ref, masks_ref = ins
        o_ref, ob_ref, st_ref = outs
        i = pl.program_id(0)

        @pl.when(i == 0)
        def _():
            st_s[...] = jnp.zeros_like(st_s)
            if ride is not None:
                ride.start(*ride_refs)

        nwv = nw_ref[...]
        mallv, masksv = mall_ref[...], masks_ref[...]

        def chunk(cix, carry):
            r0 = pl.multiple_of(cix * HG_CHUNK, HG_CHUNK)
            rows = pl.ds(r0, HG_CHUNK)
            valid = _row_valid(i * tb + r0, HG_CHUNK, lv)
            for h in range(HG_HEADS):
                sl = slice(h * HG_D, (h + 1) * HG_D)
                st = st_s[h]
                st_ref[h, cix] = st
                o, st_new, _ = _hgrn_chunk_fwd(hq_ref[rows, sl], hf_ref[rows, sl], hi_ref[rows, sl], lb_ref[:, sl],
                                               valid, mallv, masksv, st)
                st_s[h] = st_new
                o_ref[rows, sl] = o
                hg = hg_ref[rows, sl]
                r = lax.rsqrt(jnp.mean(o * o, axis=-1, keepdims=True) + EPS)
                ob_ref[rows, sl] = (o * r * nwv * (hg * _sigmoid(hg))).astype(BF16)
            return carry

        lax.fori_loop(0, ncb, chunk, 0, unroll=HG_UNROLL)

        if ride is not None:
            @pl.when(i == nb - 1)
            def _():
                ride.finish(*ride_refs)

    zb = lambda off: pl.BlockSpec((tb, w), lambda i: (i, off // w))
    full = pl.BlockSpec((tb, w), lambda i: (i, 0))
    const = lambda shape: pl.BlockSpec(shape, lambda i: (0,) * len(shape))
    r_in, r_args, r_out, r_shape, r_scratch = _ride_call(ride)
    outs = pl.pallas_call(
        body, name=name, grid=(nb,),
        in_specs=[zb(zl["hq"]), zb(zl["hf"]), zb(zl["hi"]), zb(zl["hg"]), const((1, w)), const((1, HG_D)),
                  const(mall.shape), const(masks.shape)] + r_in,
        out_specs=[full, full, pl.BlockSpec((HG_HEADS, ncb, HG_D, HG_D), lambda i: (0, i, 0, 0))] + r_out,
        out_shape=[jax.ShapeDtypeStruct((lp, w), F32), jax.ShapeDtypeStruct((lp, w), BF16),
                   jax.ShapeDtypeStruct((HG_HEADS, lp // HG_CHUNK, HG_D, HG_D), F32)] + r_shape,
        scratch_shapes=[pltpu.VMEM((HG_HEADS, HG_D, HG_D), F32)] + r_scratch,
        compiler_params=_params("arbitrary"),
    )(z, z, z, z, lb.reshape(1, w), nw.reshape(1, HG_D), mall, masks, *r_args)
    return outs[0], outs[1], outs[2], list(outs[3:])


def _hgrn_chunk_bwd(hq, hf, hi, hg, o, dout, st, dst, lbv, nwv, valid, mallv, malltv, masksv):
    c = HG_CHUNK
    scale = HG_D ** -0.5
    _, _, sv = _hgrn_chunk_fwd(hq, hf, hi, lbv, valid, mallv, masksv, st)
    shg = _sigmoid(hg)
    r = lax.rsqrt(jnp.mean(o * o, axis=-1, keepdims=True) + EPS)
    don = dout * (hg * shg)
    dhg = dout * (o * r * nwv) * shg * (1.0 + hg * (1.0 - shg))
    dnw = jnp.sum(don * o * r, axis=0, keepdims=True)
    wd = don * nwv
    do = r * wd - o * (r * r * r) * jnp.mean(o * wd, axis=-1, keepdims=True)
    dob16, hib = do.astype(BF16), hi.astype(BF16)
    dst16 = dst.astype(BF16)
    da = scale * lax.dot_general(dob16, hib, _DIMS["nt"], preferred_element_type=F32)
    dv = (scale * lax.dot_general(sv["a"].astype(BF16), dob16, _DIMS["tn"], preferred_element_type=F32)
          + lax.dot_general(sv["kb"].astype(BF16), dst16, _DIMS["nt"], preferred_element_type=F32))
    dkb = jnp.dot(hib, dst16, preferred_element_type=F32)
    dqb = scale * jnp.dot(dob16, st.astype(BF16), preferred_element_type=F32)
    dst_new = dst * sv["x_last"] + scale * lax.dot_general(dob16, sv["qb"].astype(BF16), _DIMS["tn"],
                                                           preferred_element_type=F32)
    dxl = jnp.sum(dst * st, axis=0, keepdims=True)
    x = sv["x"]
    dq = dqb * x[N_LEV * c:(N_LEV + 1) * c]
    dk = dkb * x[(N_LEV + 1) * c:]
    de = []
    for l in range(N_LEV):
        dam = (masksv[l] * da).astype(BF16)
        dqe = jnp.dot(dam, sv["ke"][l].astype(BF16), preferred_element_type=F32)
        dke = lax.dot_general(dam, sv["qe"][l].astype(BF16), _DIMS["tn"], preferred_element_type=F32)
        xl = x[l * c:(l + 1) * c]
        dq = dq + dqe * xl
        dk = dk + dke * xl
        de.append(dqe * sv["qe"][l] + dke * sv["ke"][l])
    dd = scale * jnp.sum(do * hi, axis=-1, keepdims=True)
    dq = dq + dd * sv["kv"]
    dk = dk + dd * sv["qv"]
    last = lax.broadcasted_iota(jnp.int32, (c, 1), 0) == c - 1
    de.append(dqb * sv["qb"] + jnp.where(last, dxl * sv["x_last"], 0.0))
    de.append(dkb * sv["kb"])
    dlf = _sum3(jnp.dot(malltv, _split3(jnp.concatenate(de, axis=0)), preferred_element_type=F32))
    sg, sq = sv["sg"], sv["sq"]
    df = jnp.where(sv["f"] > F_MIN, dlf * valid / sv["fc"], 0.0)
    dkm = dk * valid
    dhf = (df - dkm) * (1.0 - lbv) * sg * (1.0 - sg)
    dlb = jnp.sum((df - dkm) * (1.0 - sg), axis=0, keepdims=True)
    dhq = dq * sq * (1.0 + hq * (1.0 - sq))
    return dhq, dhf, dv, dhg, dlb, dnw, dst_new


def _hgrn_bwd(z, o_pre, dob, states, lb, nw, consts, *, zl, lv, name, ride=None):
    lp = z.shape[0]
    tb = _pick(lp, ROW_TILE, HG_CHUNK)
    ncb = tb // HG_CHUNK
    nb = lp // tb
    mall, mall_t, masks = consts
    w = HG_HEADS * HG_D
    c = HG_CHUNK

    def body(*refs):
        ins, outs, (dst_s,), ride_refs = _split_ride(refs, 12, 6, 1, ride)
        hq_ref, hf_ref, hi_ref, hg_ref, o_ref, dob_ref, st_ref, lb_ref, nw_ref, mall_ref, mallt_ref, masks_ref = ins
        dhq_ref, dhf_ref, dhi_ref, dhg_ref, dlb_ref, dnw_ref = outs
        i = pl.program_id(0)
        blk = nb - 1 - i

        @pl.when(i == 0)
        def _():
            dst_s[...] = jnp.zeros_like(dst_s)
            dlb_ref[...] = jnp.zeros_like(dlb_ref)
            dnw_ref[...] = jnp.zeros_like(dnw_ref)
            if ride is not None:
                ride.start(*ride_refs)

        nwv = nw_ref[...]
        mallv, malltv, masksv = mall_ref[...], mallt_ref[...], masks_ref[...]

        def chunk(jx, carry):
            cix = ncb - 1 - jx
            r0 = pl.multiple_of(cix * c, c)
            rows = pl.ds(r0, c)
            valid = _row_valid(blk * tb + r0, c, lv)
            for h in range(HG_HEADS):
                sl = slice(h * HG_D, (h + 1) * HG_D)
                dhq, dhf, dhi, dhg, dlb, dnw, dst_new = _hgrn_chunk_bwd(
                    hq_ref[rows, sl], hf_ref[rows, sl], hi_ref[rows, sl], hg_ref[rows, sl], o_ref[rows, sl],
                    dob_ref[rows, sl], st_ref[h, cix], dst_s[h], lb_ref[:, sl], nwv, valid, mallv, malltv, masksv)
                dst_s[h] = dst_new
                dhq_ref[rows, sl] = dhq
                dhf_ref[rows, sl] = dhf
                dhi_ref[rows, sl] = dhi
                dhg_ref[rows, sl] = dhg
                dlb_ref[:, sl] += dlb
                dnw_ref[...] += dnw
            return carry

        lax.fori_loop(0, ncb, chunk, 0, unroll=HG_UNROLL)

        if ride is not None:
            @pl.when(i == nb - 1)
            def _():
                ride.finish(*ride_refs)

    zb = lambda off: pl.BlockSpec((tb, w), lambda i: (nb - 1 - i, off // w))
    full = pl.BlockSpec((tb, w), lambda i: (nb - 1 - i, 0))
    const = lambda shape: pl.BlockSpec(shape, lambda i: (0,) * len(shape))
    r_in, r_args, r_out, r_shape, r_scratch = _ride_call(ride)
    outs = pl.pallas_call(
        body, name=name, grid=(nb,),
        in_specs=[zb(zl["hq"]), zb(zl["hf"]), zb(zl["hi"]), zb(zl["hg"]), full, full,
                  pl.BlockSpec((HG_HEADS, ncb, HG_D, HG_D), lambda i: (0, nb - 1 - i, 0, 0)),
                  const((1, w)), const((1, HG_D)), const(mall.shape), const(mall_t.shape), const(masks.shape)] + r_in,
        out_specs=[full, full, full, full, const((1, w)), const((1, HG_D))] + r_out,
        out_shape=[jax.ShapeDtypeStruct((lp, w), F32)] * 4
                  + [jax.ShapeDtypeStruct((1, w), F32), jax.ShapeDtypeStruct((1, HG_D), F32)] + r_shape,
        scratch_shapes=[pltpu.VMEM((HG_HEADS, HG_D, HG_D), F32)] + r_scratch,
        compiler_params=_params("arbitrary"),
    )(z, z, z, z, o_pre, dob, states, lb.reshape(1, w), nw.reshape(1, HG_D), mall, mall_t, masks, *r_args)
    dhq, dhf, dhi, dhg, dlb, dnw = outs[:6]
    return dhq, dhf, dhi, dhg, dlb[0], dnw[0], list(outs[6:])


def _loss_head(h, w, tpad, *, lv, name):
    lp, d = h.shape
    tm = _pick(lp, ROW_TILE, 8)

    def body(h_ref, w_ref, t_ref, dh_ref, loss_ref, dw_ref):
        i = pl.program_id(0)
        r0 = i * tm + lax.broadcasted_iota(jnp.int32, (tm, 1), 0)
        valid = ((r0 >= ROW_X) & (r0 < lv)).astype(F32)
        xv, wv = h_ref[...], w_ref[...]
        r = lax.rsqrt(jnp.mean(xv * xv, axis=-1, keepdims=True) + EPS)
        e = (xv * r * wv - t_ref[...]) * valid
        dy = e * (1.0 / d)
        wdy = dy * wv
        dh_ref[...] = r * wdy - xv * (r * r * r) * jnp.mean(xv * wdy, axis=-1, keepdims=True)

        @pl.when(i == 0)
        def _():
            loss_ref[...] = jnp.zeros_like(loss_ref)
            dw_ref[...] = jnp.zeros_like(dw_ref)

        loss_ref[...] += 0.5 * jnp.sum(jnp.mean(e * e, axis=-1, keepdims=True), axis=0, keepdims=True)
        dw_ref[...] += jnp.sum(dy * xv * r, axis=0, keepdims=True)

    row = pl.BlockSpec((tm, d), lambda i: (i, 0))
    vec = pl.BlockSpec((1, d), lambda i: (0, 0))
    dh, loss, dw = pl.pallas_call(
        body, name=name, grid=(lp // tm,), in_specs=[row, vec, row],
        out_specs=[row, pl.BlockSpec((8, LANES), lambda i: (0, 0)), vec],
        out_shape=[jax.ShapeDtypeStruct((lp, d), F32), jax.ShapeDtypeStruct((8, LANES), F32),
                   jax.ShapeDtypeStruct((1, d), F32)],
        compiler_params=_params("arbitrary"),
    )(h, w.reshape(1, d), tpad)
    return dh, loss[0, 0], dw[0]


def _adamw(w, g, m, v, *, name):
    shape = w.shape
    cols = shape[-1]
    rows = int(np.prod(shape[:-1])) if len(shape) > 1 else 1
    tr = _pick(rows, 256, 8)
    c1 = 1.0 - ADAM_B1 ** ADAM_STEP
    c2 = 1.0 - ADAM_B2 ** ADAM_STEP

    def body(w_ref, g_ref, m_ref, v_ref, d_ref, nm_ref, nv_ref):
        gv = g_ref[...]
        nm = ADAM_B1 * m_ref[...] + (1.0 - ADAM_B1) * gv
        nv = ADAM_B2 * v_ref[...] + (1.0 - ADAM_B2) * (gv * gv)
        d_ref[...] = -ADAM_LR * ((nm / c1) / (jnp.sqrt(nv / c2) + ADAM_EPS) + ADAM_WD * w_ref[...])
        nm_ref[...] = nm
        nv_ref[...] = nv

    spec = pl.BlockSpec((tr, cols), lambda i: (i, 0))
    r2 = lambda a: a.reshape(rows, cols)
    outs = pl.pallas_call(
        body, name=name, grid=(rows // tr,), in_specs=[spec] * 4, out_specs=[spec] * 3,
        out_shape=[jax.ShapeDtypeStruct((rows, cols), F32)] * 3, compiler_params=_params("parallel"),
    )(r2(w), r2(g), r2(m), r2(v))
    return tuple(o.reshape(shape) for o in outs)


HBM_SPEC = pl.BlockSpec(memory_space=pl.ANY)


def _coords():
    return lax.axis_index("x"), lax.axis_index("y"), lax.axis_index("c")


def _other_chips(x, y):
    return [(1 - x, y), (x, 1 - y), (1 - x, 1 - y)]


def _remote(src, dst, ssem, rsem, dev):
    return pltpu.make_async_remote_copy(src_ref=src, dst_ref=dst, send_sem=ssem, recv_sem=rsem,
                                        device_id=dev, device_id_type=MESH)


def _gather_chips(w, *, name):
    rows, cols = w.shape
    rh = rows // 2
    align = 8 * 4 // w.dtype.itemsize
    assert rh * 2 == rows and rh % align == 0

    def body(w_ref, out_ref, send_sems, recv_sems):
        x, y, c = _coords()
        k = 2 * x + y
        sib = (x, y, 1 - c)
        half = pl.ds(pl.multiple_of(c * rh, align), rh)
        ohalf = pl.ds(pl.multiple_of((1 - c) * rh, align), rh)
        chips = _other_chips(x, y)
        sent = []
        for j, (px, py) in enumerate(chips):
            cp = _remote(w_ref.at[half], out_ref.at[k, half], send_sems.at[j], recv_sems.at[j], (px, py, c))
            cp.start()
            sent.append(cp)
        for j, (px, py) in enumerate(chips):
            blk = out_ref.at[2 * px + py, half]
            _remote(w_ref.at[half], blk, send_sems.at[j], recv_sems.at[j], (px, py, c)).wait_recv()
            fw = _remote(blk, blk, send_sems.at[3 + j], recv_sems.at[3 + j], sib)
            fw.start()
            sent.append(fw)
        for j, (px, py) in enumerate(chips):
            blk = out_ref.at[2 * px + py, ohalf]
            _remote(blk, blk, send_sems.at[3 + j], recv_sems.at[3 + j], sib).wait_recv()
        for cp in sent:
            cp.wait_send()

    g4 = pl.pallas_call(
        body, name=name, in_specs=[HBM_SPEC], out_specs=HBM_SPEC,
        out_shape=jax.ShapeDtypeStruct((4, rows, cols), w.dtype),
        scratch_shapes=[pltpu.SemaphoreType.DMA((6,)), pltpu.SemaphoreType.DMA((6,))],
    )(w)
    xi, yi, _ = _coords()
    return lax.dynamic_update_slice(g4, w[None], (2 * xi + yi, 0, 0))


def _swap_halves(gp, *, name):
    n, rows, cols = gp.shape
    rh = rows // 2

    def body(g_ref, out_ref, send_sems, recv_sems):
        x, y, c = _coords()
        sib = (x, y, 1 - c)
        ohalf = pl.ds(pl.multiple_of((1 - c) * rh, 8 * 4 // gp.dtype.itemsize), rh)
        cps = [_remote(g_ref.at[s, ohalf], out_ref.at[s], send_sems.at[s], recv_sems.at[s], sib) for s in range(n)]
        for cp in cps:
            cp.start()
        for cp in cps:
            cp.wait_recv()
        for cp in cps:
            cp.wait_send()

    return pl.pallas_call(
        body, name=name, in_specs=[HBM_SPEC], out_specs=HBM_SPEC,
        out_shape=jax.ShapeDtypeStruct((n, rh, cols), gp.dtype),
        scratch_shapes=[pltpu.SemaphoreType.DMA((n,)), pltpu.SemaphoreType.DMA((n,))],
    )(gp)


def _add_half(gp, got, cidx, *, name):
    n, rows, cols = gp.shape
    rh = rows // 2
    tr = _pick(rh, 512, 16)
    nrb = rh // tr

    def body(c_ref, a_ref, b_ref, o_ref):
        o_ref[...] = (a_ref[...].astype(F32) + b_ref[...].astype(F32)).astype(BF16)

    grid_spec = pltpu.PrefetchScalarGridSpec(
        num_scalar_prefetch=1, grid=(n, nrb),
        in_specs=[pl.BlockSpec((1, tr, cols), lambda s, i, c_ref: (s, c_ref[0] * nrb + i, 0)),
                  pl.BlockSpec((1, tr, cols), lambda s, i, c_ref: (s, i, 0))],
        out_specs=pl.BlockSpec((1, tr, cols), lambda s, i, c_ref: (s, i, 0)))
    return pl.pallas_call(
        body, name=name, grid_spec=grid_spec, out_shape=jax.ShapeDtypeStruct((n, rh, cols), BF16),
        compiler_params=_params("parallel", "parallel"),
    )(cidx, gp, got)


def _scatter_chips(p, *, name):
    _, rh, cols = p.shape

    def body(p_ref, out_ref, send_sems, recv_sems):
        x, y, c = _coords()
        cps = []
        for j, (px, py) in enumerate(_other_chips(x, y)):
            cps.append(_remote(p_ref.at[2 * px + py], out_ref.at[j], send_sems.at[j], recv_sems.at[j], (px, py, c)))
        for cp in cps:
            cp.start()
        for cp in cps:
            cp.wait_recv()
        for cp in cps:
            cp.wait_send()

    return pl.pallas_call(
        body, name=name, in_specs=[HBM_SPEC], out_specs=HBM_SPEC,
        out_shape=jax.ShapeDtypeStruct((3, rh, cols), p.dtype),
        scratch_shapes=[pltpu.SemaphoreType.DMA((3,)), pltpu.SemaphoreType.DMA((3,))],
    )(p)


def _sum_arrivals(p, land, kidx, *, name):
    _, rh, cols = p.shape
    tr = _pick(rh, 512, 16)

    def body(k_ref, a_ref, l_ref, o_ref):
        f = lambda v: v.astype(F32)
        o_ref[...] = ((f(a_ref[0]) + f(l_ref[0])) + f(l_ref[1])) + f(l_ref[2])

    grid_spec = pltpu.PrefetchScalarGridSpec(
        num_scalar_prefetch=1, grid=(rh // tr,),
        in_specs=[pl.BlockSpec((1, tr, cols), lambda i, k_ref: (k_ref[0], i, 0)),
                  pl.BlockSpec((3, tr, cols), lambda i, k_ref: (0, i, 0))],
        out_specs=pl.BlockSpec((tr, cols), lambda i, k_ref: (i, 0)))
    return pl.pallas_call(
        body, name=name, grid_spec=grid_spec, out_shape=jax.ShapeDtypeStruct((rh, cols), F32),
        compiler_params=_params("parallel"),
    )(kidx, p, land)


def _join_halves(q, *, name):
    rh, cols = q.shape

    def body(q_ref, out_ref, send_sem, recv_sem):
        x, y, c = _coords()
        half = pl.ds(pl.multiple_of(c * rh, 8), rh)
        ohalf = pl.ds(pl.multiple_of((1 - c) * rh, 8), rh)
        cp = _remote(q_ref, out_ref.at[half], send_sem, recv_sem, (x, y, 1 - c))
        cp.start()
        _remote(q_ref, out_ref.at[ohalf], send_sem, recv_sem, (x, y, 1 - c)).wait_recv()
        cp.wait_send()

    full = pl.pallas_call(
        body, name=name, in_specs=[HBM_SPEC], out_specs=HBM_SPEC,
        out_shape=jax.ShapeDtypeStruct((2 * rh, cols), q.dtype),
        scratch_shapes=[pltpu.SemaphoreType.DMA, pltpu.SemaphoreType.DMA],
    )(q)
    return lax.dynamic_update_slice(full, q, (lax.axis_index("c") * rh, 0))


def _rs_begin(gp, cidx, *, tag):
    got = _swap_halves(gp, name=f"rs_swap_{tag}")
    return _add_half(gp, got, cidx, name=f"rs_add_{tag}")


def _rs_end(p, land, kidx, *, tag):
    q = _sum_arrivals(p, land, kidx, name=f"rs_sum_{tag}")
    return _join_halves(q, name=f"rs_join_{tag}")


class _ScatterRide:
    def __init__(self, p):
        _, rh, cols = p.shape
        self.args = [p]
        self.out_shape = [jax.ShapeDtypeStruct((3, rh, cols), p.dtype)]
        self.scratch = [pltpu.SemaphoreType.DMA((3,)), pltpu.SemaphoreType.DMA((3,))]

    def _copies(self, p_ref, out_ref, ssem, rsem):
        x, y, c = _coords()
        return [_remote(p_ref.at[2 * px + py], out_ref.at[j], ssem.at[j], rsem.at[j], (px, py, c))
                for j, (px, py) in enumerate(_other_chips(x, y))]

    def start(self, *refs):
        for cp in self._copies(*refs):
            cp.start()

    def finish(self, *refs):
        cps = self._copies(*refs)
        for cp in cps:
            cp.wait_recv()
        for cp in cps:
            cp.wait_send()


class _GatherRide:
    def __init__(self, w):
        rows, cols = w.shape
        self.rh = rows // 2
        self.align = 8 * 4 // w.dtype.itemsize
        assert self.rh * 2 == rows and self.rh % self.align == 0
        self.args = [w]
        self.out_shape = [jax.ShapeDtypeStruct((4, rows, cols), w.dtype)]
        self.scratch = [pltpu.SemaphoreType.DMA((3,)), pltpu.SemaphoreType.DMA((3,))]

    def _copies(self, w_ref, out_ref, ssem, rsem):
        x, y, c = _coords()
        half = pl.ds(pl.multiple_of(c * self.rh, self.align), self.rh)
        send, recv = [], []
        for j, (px, py) in enumerate(_other_chips(x, y)):
            send.append(_remote(w_ref.at[half], out_ref.at[2 * x + y, half], ssem.at[j], rsem.at[j], (px, py, c)))
            recv.append(_remote(w_ref.at[half], out_ref.at[2 * px + py, half], ssem.at[j], rsem.at[j], (px, py, c)))
        return send, recv

    def start(self, *refs):
        for cp in self._copies(*refs)[0]:
            cp.start()

    def finish(self, *refs):
        send, recv = self._copies(*refs)
        for cp in recv:
            cp.wait_recv()
        for cp in send:
            cp.wait_send()


def _gather_forward(g4, *, name):
    _, rows, cols = g4.shape
    rh = rows // 2
    align = 8 * 4 // g4.dtype.itemsize

    def body(g_ref, out_ref, send_sems, recv_sems):
        x, y, c = _coords()
        sib = (x, y, 1 - c)
        half = pl.ds(pl.multiple_of(c * rh, align), rh)
        ohalf = pl.ds(pl.multiple_of((1 - c) * rh, align), rh)
        chips = _other_chips(x, y)
        sent = []
        for j, (px, py) in enumerate(chips):
            blk = out_ref.at[2 * px + py, half]
            cp = _remote(blk, blk, send_sems.at[j], recv_sems.at[j], sib)
            cp.start()
            sent.append(cp)
        for j, (px, py) in enumerate(chips):
            blk = out_ref.at[2 * px + py, ohalf]
            _remote(blk, blk, send_sems.at[j], recv_sems.at[j], sib).wait_recv()
        for cp in sent:
            cp.wait_send()

    return pl.pallas_call(
        body, name=name, in_specs=[HBM_SPEC], out_specs=HBM_SPEC, out_shape=jax.ShapeDtypeStruct(g4.shape, g4.dtype),
        input_output_aliases={0: 0},
        scratch_shapes=[pltpu.SemaphoreType.DMA((3,)), pltpu.SemaphoreType.DMA((3,))],
    )(g4)


def _own_block(g4, w):
    xi, yi, _ = _coords()
    return lax.dynamic_update_slice(g4, w[None], (2 * xi + yi, 0, 0))


def _allreduce_small(s, *, name):
    rows, cols = s.shape

    def body(s_ref, o_ref, buf, send_sems, recv_sems):
        x, y, c = _coords()
        me = 4 * x + 2 * y + c
        buf[me] = s_ref[...]
        cps = []
        for r in range(1, 8):
            peer = tuple((1 - v) if (r >> sh) & 1 else v for v, sh in ((x, 2), (y, 1), (c, 0)))
            cps.append(_remote(s_ref, buf.at[me], send_sems.at[r - 1], recv_sems.at[r - 1], peer))
        for cp in cps:
            cp.start()
        for cp in cps:
            cp.wait_recv()
        for cp in cps:
            cp.wait_send()
        acc = buf[0]
        for d in range(1, 8):
            acc = acc + buf[d]
        o_ref[...] = acc

    vm = pl.BlockSpec(memory_space=pltpu.VMEM)
    return pl.pallas_call(
        body, name=name, in_specs=[vm], out_specs=vm, out_shape=jax.ShapeDtypeStruct((rows, cols), F32),
        scratch_shapes=[pltpu.VMEM((8, rows, cols), F32), pltpu.SemaphoreType.DMA((7,)),
                        pltpu.SemaphoreType.DMA((7,))],
    )(s)


PACKED = ("ffn1_w_gu", "ffn1_w_down", "w_in", "w_uq", "w_ukv", "w_proj_attn", "w_proj_rec", "w_out",
          "ffn2_w_gu", "ffn2_w_down")
ROW_SHARDED = ("ffn1_w_down", "w_out", "ffn2_w_down")


def _pack_plan(shard_shapes):
    plan, off = {}, 0
    for n in PACKED:
        r, c = shard_shapes[n]
        assert (r * c) % PACK_W == 0
        plan[n] = (off, r * c // PACK_W, (r, c))
        off += r * c // PACK_W
    total = -(-off // 32) * 32
    return plan, total


def _pack(tensors, plan, total, dtype):
    parts = [tensors[n].astype(dtype).reshape(-1, PACK_W) for n in PACKED]
    used = sum(p.shape[0] for p in parts)
    if total > used:
        parts.append(jnp.zeros((total - used, PACK_W), dtype))
    return jnp.concatenate(parts, axis=0)


def _unpack_full(g4, plan):
    out = {}
    for n in PACKED:
        off, nr, (r, c) = plan[n]
        sh = g4[:, off:off + nr].reshape(4, r, c)
        out[n] = sh.reshape(4 * r, c) if n in ROW_SHARDED else jnp.swapaxes(sh, 0, 1).reshape(r, 4 * c)
    return out


def _pack_grads(grads, plan, total):
    blocks = []
    for s in range(4):
        t = {}
        for n in PACKED:
            _, _, (r, c) = plan[n]
            t[n] = grads[n][s * r:(s + 1) * r] if n in ROW_SHARDED else grads[n][:, s * c:(s + 1) * c]
        blocks.append(_pack(t, plan, total, BF16))
    return jnp.stack(blocks)


def _unpack_shard(p, plan):
    return {n: p[plan[n][0]:plan[n][0] + plan[n][1]].reshape(plan[n][2]) for n in PACKED}


def _swap_cols(w):
    hlf = w.shape[1] // 2
    return jnp.concatenate([-w[:, hlf:], w[:, :hlf]], axis=1)


def _unswap_cols(dw):
    hlf = dw.shape[1] // 2
    return jnp.concatenate([dw[:, hlf:], -dw[:, :hlf]], axis=1)


def _layer_weights(full, d):
    zl = _z_layout(d)
    f = full["ffn1_w_down"].shape[0]
    w_in = full["w_in"]
    o = 0
    cols = {}
    for nm, wd in (("cq", Q_LORA), ("ckv", KV_LORA), ("kpe", QK_ROPE), ("hq", 512), ("hf", 512), ("hi", 512),
                   ("hg", 512), ("ga", d), ("gb", d)):
        cols[nm] = w_in[:, o:o + wd]
        o += wd
    zc = lambda n: jnp.zeros((d, n), BF16)
    win_p = jnp.concatenate(
        [cols["cq"], zc(QK_NOPE), cols["kpe"], zc(32), cols["ckv"], zc(QK_NOPE), _swap_cols(cols["kpe"]), zc(32),
         zc(LANES), cols["ga"], cols["gb"], cols["hq"], cols["hf"], cols["hi"], cols["hg"]], axis=1)
    assert win_p.shape[1] == zl["total"]
    wq = full["w_uq"].reshape(Q_LORA, MLA_HEADS, QK_NOPE + QK_ROPE)
    nope, rope = wq[:, :, :QK_NOPE], wq[:, :, QK_NOPE:]
    z32 = jnp.zeros((Q_LORA, MLA_HEADS, 32), BF16)
    z64 = jnp.zeros((Q_LORA, MLA_HEADS, QK_NOPE), BF16)
    rope_sw = jnp.concatenate([-rope[:, :, 16:], rope[:, :, :16]], axis=2)
    wqa = jnp.concatenate([nope, rope, z32], axis=2).reshape(Q_LORA, QW)
    wqb = jnp.concatenate([z64, rope_sw, z32], axis=2).reshape(Q_LORA, QW)
    wpa = full["w_proj_attn"].reshape(MLA_HEADS, V_HEAD, d)
    wpa_p = jnp.concatenate([jnp.zeros_like(wpa), wpa], axis=1).reshape(QW, d)
    return dict(
        wg1=full["ffn1_w_gu"][:, :f], wu1=full["ffn1_w_gu"][:, f:], wd1=full["ffn1_w_down"],
        wg2=full["ffn2_w_gu"][:, :f], wu2=full["ffn2_w_gu"][:, f:], wd2=full["ffn2_w_down"],
        win=win_p, wq2=jnp.concatenate([wqa, wqb], axis=1), wqa=wqa, wqb=wqb, wkv=full["w_ukv"], wpa=wpa_p,
        wpr=full["w_proj_rec"], wout=full["w_out"])


def _natural_grads(g, d):
    zl = _z_layout(d)
    dwin = g["win"]
    kpe = dwin[:, Z_KPA + QK_NOPE:Z_KPA + QK_NOPE + QK_ROPE] + _unswap_cols(
        dwin[:, Z_KPB + QK_NOPE:Z_KPB + QK_NOPE + QK_ROPE])
    w_in = jnp.concatenate(
        [dwin[:, Z_Q:Z_Q + Q_LORA], dwin[:, Z_KV:Z_KV + KV_LORA], kpe, dwin[:, zl["hq"]:zl["hq"] + 2048],
         dwin[:, zl["ga"]:zl["ga"] + 2 * d]], axis=1)
    qa = g["wqa"].reshape(Q_LORA, MLA_HEADS, HEAD_W)
    qb = g["wqb"].reshape(Q_LORA, MLA_HEADS, HEAD_W)[:, :, QK_NOPE:QK_NOPE + QK_ROPE]
    rope = qa[:, :, QK_NOPE:QK_NOPE + QK_ROPE] + jnp.concatenate([qb[:, :, 16:], -qb[:, :, :16]], axis=2)
    w_uq = jnp.concatenate([qa[:, :, :QK_NOPE], rope], axis=2).reshape(Q_LORA, -1)
    wpa = g["wpa"].reshape(MLA_HEADS, 2 * V_HEAD, d)[:, V_HEAD:].reshape(MLA_HEADS * V_HEAD, d)
    return dict(
        ffn1_w_gu=jnp.concatenate([g["wg1"], g["wu1"]], axis=1), ffn1_w_down=g["wd1"],
        ffn2_w_gu=jnp.concatenate([g["wg2"], g["wu2"]], axis=1), ffn2_w_down=g["wd2"],
        w_in=w_in, w_uq=w_uq, w_ukv=g["wkv"], w_proj_attn=wpa, w_proj_rec=g["wpr"], w_out=g["wout"])


def _rope_tables(lp):
    pos = jnp.maximum(jnp.arange(lp) - FRONT, 0).astype(F32)
    half = QK_ROPE // 2
    inv = ROPE_THETA ** (-jnp.arange(half, dtype=F32) / half)
    ang = pos[:, None] * inv[None, :]
    cos, sin = jnp.cos(ang), jnp.sin(ang)
    cos_t = jnp.concatenate([jnp.ones((lp, QK_NOPE), F32), cos, cos, jnp.zeros((lp, 32), F32)], axis=1)
    sin_t = jnp.concatenate([jnp.zeros((lp, QK_NOPE), F32), sin, sin, jnp.zeros((lp, 32), F32)], axis=1)
    return cos_t, sin_t


def _lower_bounds(raw):
    p = jax.nn.softmax(raw.astype(F32), axis=0)
    return jnp.cumsum(p, axis=0) - p[0:1]


def _ffn_fwd(h, nw, wg, wu, wd, tag):
    a, a_t = _rmsnorm_fwd(h, nw, width=h.shape[1], col_block=0, transposed=True, name=f"norm_{tag}")
    g = _matmul([(a, wg)], "nn", out_dtype=BF16, name=f"gate_{tag}")
    u = _matmul([(a, wu)], "nn", out_dtype=BF16, name=f"up_{tag}")
    act, act_t = _swiglu_fwd(g, u, name=f"swiglu_{tag}")
    out = _matmul([(act, wd)], "nn", res=h, scale=0.5, name=f"down_{tag}")
    return out, dict(h=h, a_t=a_t, g=g, u=u, act_t=act_t)


def _ffn_bwd(dout, sv, nw, wg, wu, wd, lv, tag):
    dact = _matmul([(dout, wd)], "nt", scale=0.5, out_dtype=BF16, name=f"ddown_{tag}")
    dwd = _matmul([(sv["act_t"], dout)], "nn", scale=0.5, name=f"dwdown_{tag}")
    dg, du = _swiglu_bwd(dact, sv["g"], sv["u"], name=f"dswiglu_{tag}")
    dwg = _matmul([(sv["a_t"], dg)], "nn", name=f"dwgate_{tag}")
    dwu = _matmul([(sv["a_t"], du)], "nn", name=f"dwup_{tag}")
    da = _matmul([(dg, wg), (du, wu)], "nt", name=f"dnormed_{tag}")
    dh, dn = _rmsnorm_bwd(sv["h"], nw, da, width=da.shape[1], col_block=0, lv=lv, dres=dout, name=f"dnorm_{tag}")
    return dh, dn, dwg, dwu, dwd


def _layer_fwd(h0, lw, sm, lb, tabs, consts, lv, l, ride=None):
    d = h0.shape[1]
    zl = _z_layout(d)
    cos_t, sin_t = tabs[:2]
    h1, s1 = _ffn_fwd(h0, sm["ffn1_norm"], lw["wg1"], lw["wu1"], lw["wd1"], f"ffn1_{l}")
    um, um_t = _rmsnorm_fwd(h1, sm["mix_norm"], width=d, col_block=0, transposed=True, name=f"norm_mix_{l}")
    z = _matmul([(um, lw["win"])], "nn", name=f"inproj_{l}")
    qn = _rmsnorm_fwd(z, sm["q_norm"], width=Q_LORA, col_block=Z_Q // Q_LORA, name=f"norm_q_{l}")
    kvn = _rmsnorm_fwd(z, sm["kv_norm"], width=KV_LORA, col_block=Z_KV // KV_LORA, name=f"norm_kv_{l}")
    q2 = _matmul([(qn, lw["wq2"])], "nn", name=f"uq_{l}")
    kv = _matmul([(kvn, lw["wkv"])], "nn", name=f"ukv_{l}")
    q, k, v = _qkv_prep_fwd(q2, kv, z, cos_t, sin_t, name=f"qkv_{l}")
    o, lse = _attn_fwd(q, k, v, tabs[2], lv=lv, name=f"attn_{l}")
    ya = _matmul([(o, lw["wpa"])], "nn", name=f"proj_attn_{l}")
    o_pre, ob, states, rode = _hgrn_fwd(z, lb, sm["hg_norm"], consts, zl=zl, lv=lv, name=f"hgrn_{l}", ride=ride)
    yb = _matmul([(ob, lw["wpr"])], "nn", name=f"proj_rec_{l}")
    mg, mg_t = _merge_fwd(ya, yb, z, zl=zl, name=f"merge_{l}")
    h2 = _matmul([(mg, lw["wout"])], "nn", res=h1, name=f"out_{l}")
    h3, s2 = _ffn_fwd(h2, sm["ffn2_norm"], lw["wg2"], lw["wu2"], lw["wd2"], f"ffn2_{l}")
    saved = dict(s1=s1, s2=s2, h1=h1, um_t=um_t, z=z, qn=qn, kvn=kvn, q=q, k=k, v=v, o=o, lse=lse, ya=ya, yb=yb,
                 o_pre=o_pre, ob=ob, states=states, mg_t=mg_t)
    return h3, saved, rode


def _layer_bwd(dh3, sv, lw, sm, lb, tabs, consts, lv, l, ride=None):
    d = dh3.shape[1]
    lp = dh3.shape[0]
    zl = _z_layout(d)
    cos_t, sin_t = tabs[:2]
    z = sv["z"]
    g = {}
    sg = {}
    dh2, sg["ffn2_norm"], g["wg2"], g["wu2"], g["wd2"] = _ffn_bwd(
        dh3, sv["s2"], sm["ffn2_norm"], lw["wg2"], lw["wu2"], lw["wd2"], lv, f"ffn2_{l}")
    dmg = _matmul([(dh2, lw["wout"])], "nt", name=f"dmerged_{l}")
    g["wout"] = _matmul([(sv["mg_t"], dh2)], "nn", name=f"dwout_{l}")
    dya, dyb, dga, dgb = _merge_bwd(dmg, sv["ya"], sv["yb"], z, zl=zl, name=f"dmerge_{l}")
    doa = _matmul([(dya, lw["wpa"])], "nt", name=f"dattn_out_{l}")
    g["wpa"] = _matmul([(sv["o"], dya)], "tn", name=f"dwproj_attn_{l}")
    dob = _matmul([(dyb, lw["wpr"])], "nt", name=f"drec_out_{l}")
    g["wpr"] = _matmul([(sv["ob"], dyb)], "tn", name=f"dwproj_rec_{l}")
    dhq, dhf, dhi, dhg, dlb, sg["hg_norm"], rode = _hgrn_bwd(
        z, sv["o_pre"], dob, sv["states"], lb, sm["hg_norm"], consts, zl=zl, lv=lv, name=f"dhgrn_{l}", ride=ride)
    delta = _attn_delta(doa, sv["o"], name=f"attn_delta_{l}")
    dq, dk, dv = _attn_bwd(sv["q"], sv["k"], sv["v"], doa, sv["lse"], delta, tabs[2], name=f"dattn_{l}")
    dqa, dqb, dkv, dza, dzb = _qkv_prep_bwd(dq, dk, dv, cos_t, sin_t, name=f"dqkv_{l}")
    dqn = _matmul([(dqa, lw["wqa"]), (dqb, lw["wqb"])], "nt", name=f"dqn_{l}")
    g["wqa"] = _matmul([(sv["qn"], dqa)], "tn", name=f"dwqa_{l}")
    g["wqb"] = _matmul([(sv["qn"], dqb)], "tn", name=f"dwqb_{l}")
    dkvn = _matmul([(dkv, lw["wkv"])], "nt", name=f"dkvn_{l}")
    g["wkv"] = _matmul([(sv["kvn"], dkv)], "tn", name=f"dwkv_{l}")
    dzq, sg["q_norm"] = _rmsnorm_bwd(z, sm["q_norm"], dqn, width=Q_LORA, col_block=Z_Q // Q_LORA, lv=lv,
                                     name=f"dnorm_q_{l}")
    dzkv, sg["kv_norm"] = _rmsnorm_bwd(z, sm["kv_norm"], dkvn, width=KV_LORA, col_block=Z_KV // KV_LORA, lv=lv,
                                       name=f"dnorm_kv_{l}")
    dz = jnp.concatenate([dzq, dza, dzkv, dzb, jnp.zeros((lp, LANES), F32), dga, dgb, dhq, dhf, dhi, dhg],
                         axis=1).astype(BF16)
    dum = _matmul([(dz, lw["win"])], "nt", name=f"dmixed_{l}")
    g["win"] = _matmul([(sv["um_t"], dz)], "nn", name=f"dwin_{l}")
    dh1, sg["mix_norm"] = _rmsnorm_bwd(sv["h1"], sm["mix_norm"], dum, width=d, col_block=0, lv=lv, dres=dh2,
                                       name=f"dnorm_mix_{l}")
    dh0, sg["ffn1_norm"], g["wg1"], g["wu1"], g["wd1"] = _ffn_bwd(
        dh1, sv["s1"], sm["ffn1_norm"], lw["wg1"], lw["wu1"], lw["wd1"], lv, f"ffn1_{l}")
    return dh0, g, sg, dlb, rode


WEIGHTS = ("meta_tokens", "ffn1_norm", "ffn1_w_gu", "ffn1_w_down", "mix_norm", "w_in", "q_norm", "kv_norm", "w_uq",
           "w_ukv", "hg_lb_raw", "hg_norm", "w_proj_attn", "w_proj_rec", "w_out", "ffn2_norm", "ffn2_w_gu",
           "ffn2_w_down", "final_norm")
SMALL = ("ffn1_norm", "mix_norm", "q_norm", "kv_norm", "hg_lb_raw", "hg_norm", "ffn2_norm")


def _small_rows(vals):
    pad = lambda a: jnp.pad(a, ((0, -a.shape[0] % 8), (0, PACK_W - a.shape[1])))
    rows = [pad(vals[n]) for n in SMALL]
    rows.append(pad(vals["final_norm"][None, :]))
    rows.append(pad(vals["meta_tokens"]))
    rows.append(pad(vals["loss"].reshape(1, 1)))
    return jnp.concatenate(rows, axis=0)


def _small_unrows(s, d, widths):
    out, o = {}, 0
    for n in SMALL:
        out[n] = s[o:o + DEPTH, :widths[n]]
        o += -(-DEPTH // 8) * 8
    out["final_norm"] = s[o, :d]
    o += 8
    out["meta_tokens"] = s[o:o + N_META, :d]
    o += -(-N_META // 8) * 8
    out["loss"] = s[o, 0]
    return out


def _step(args):
    x = args["x"][0]
    seq, d = x.shape
    assert d <= PACK_W
    lv = ROW_X + seq
    lp = -(-lv // ROW_TILE) * ROW_TILE
    xi, yi, ci = _coords()
    kidx = (2 * xi + yi).astype(jnp.int32).reshape(1)
    cidx = ci.astype(jnp.int32).reshape(1)
    consts = _hgrn_consts()
    tabs = (*_rope_tables(lp), _attn_consts(lp))

    shard_shapes = {n: args[n].shape[1:] for n in PACKED}
    plan, total = _pack_plan(shard_shapes)
    packed = [_pack({n: args[n][l] for n in PACKED}, plan, total, BF16) for l in range(DEPTH)]
    mt = args["meta_tokens"]
    mt4 = _gather_chips(mt, name="gather_meta")
    meta = jnp.concatenate(list(mt4), axis=1)

    sm = [{n: args[n][l] for n in SMALL} for l in range(DEPTH)]
    lbs = _lower_bounds(args["hg_lb_raw"])

    h = jnp.concatenate([jnp.zeros((FRONT, d), F32), meta, x, jnp.zeros((lp - lv, d), F32)], axis=0)
    saved, lws = [], []
    g4 = _gather_chips(packed[0], name="gather_0")
    for l in range(DEPTH):
        lws.append(_layer_weights(_unpack_full(g4, plan), d))
        ride = _GatherRide(packed[l + 1]) if l + 1 < DEPTH else None
        h, sv, rode = _layer_fwd(h, lws[l], sm[l], lbs[l], tabs, consts, lv, l, ride)
        saved.append(sv)
        if ride is not None:
            g4 = _own_block(_gather_forward(rode[0], name=f"gather_fwd_{l + 1}"), packed[l + 1])
    tpad = jnp.pad(args["loss_target"][0], ((ROW_X, lp - lv), (0, 0)))
    dh, loss, dfinal = _loss_head(h, args["final_norm"], tpad, lv=lv, name="loss_head")

    small = {n: [None] * DEPTH for n in SMALL}
    dlbs = [None] * DEPTH
    shard_grads = [None] * DEPTH
    waiting = None
    for l in reversed(range(DEPTH)):
        ride = _ScatterRide(waiting) if waiting is not None else None
        dh, g, sg, dlbs[l], rode = _layer_bwd(dh, saved[l], lws[l], sm[l], lbs[l], tabs, consts, lv, l, ride)
        if ride is not None:
            shard_grads[l + 1] = _unpack_shard(_rs_end(waiting, rode[0], kidx, tag=str(l + 1)), plan)
        for n in sg:
            small[n][l] = sg[n]
        waiting = _rs_begin(_pack_grads(_natural_grads(g, d), plan, total), cidx, tag=str(l))
    land = _scatter_chips(waiting, name="rs_scatter_0")
    shard_grads[0] = _unpack_shard(_rs_end(waiting, land, kidx, tag="0"), plan)

    _, lb_vjp = jax.vjp(_lower_bounds, args["hg_lb_raw"])
    small_vals = {n: jnp.stack(small[n]) for n in SMALL if n != "hg_lb_raw"}
    small_vals["hg_lb_raw"] = lb_vjp(jnp.stack(dlbs))[0]
    small_vals["final_norm"] = dfinal
    small_vals["meta_tokens"] = dh[FRONT:ROW_X]
    small_vals["loss"] = loss
    widths = {n: args[n].shape[1] for n in SMALL}
    tot = _small_unrows(_allreduce_small(_small_rows(small_vals), name="allreduce_small"), d, widths)

    grads = {n: jnp.stack([shard_grads[l][n] for l in range(DEPTH)]) for n in PACKED}
    for n in SMALL:
        grads[n] = tot[n]
    grads["final_norm"] = tot["final_norm"]
    mcols = mt.shape[1]
    grads["meta_tokens"] = lax.dynamic_slice_in_dim(tot["meta_tokens"], (2 * xi + yi) * mcols, mcols, axis=1)
    grad_x = dh[ROW_X:lv][None]

    delta, new_m, new_v = {}, {}, {}
    for n in WEIGHTS:
        delta[n], new_m[n], new_v[n] = _adamw(args[n], grads[n], args["m_" + n], args["v_" + n], name=f"adamw_{n}")
    return (tot["loss"], grad_x, *[grads[n] for n in WEIGHTS], *[delta[n] for n in WEIGHTS],
            *[new_m[n] for n in WEIGHTS], *[new_v[n] for n in WEIGHTS])


def kernel(x, meta_tokens, ffn1_norm, ffn1_w_gu, ffn1_w_down, mix_norm, w_in, q_norm, kv_norm, w_uq, w_ukv, hg_lb_raw, hg_norm, w_proj_attn, w_proj_rec, w_out, ffn2_norm, ffn2_w_gu, ffn2_w_down, final_norm, loss_target, m_meta_tokens, m_ffn1_norm, m_ffn1_w_gu, m_ffn1_w_down, m_mix_norm, m_w_in, m_q_norm, m_kv_norm, m_w_uq, m_w_ukv, m_hg_lb_raw, m_hg_norm, m_w_proj_attn, m_w_proj_rec, m_w_out, m_ffn2_norm, m_ffn2_w_gu, m_ffn2_w_down, m_final_norm, v_meta_tokens, v_ffn1_norm, v_ffn1_w_gu, v_ffn1_w_down, v_mix_norm, v_w_in, v_q_norm, v_kv_norm, v_w_uq, v_w_ukv, v_hg_lb_raw, v_hg_norm, v_w_proj_attn, v_w_proj_rec, v_w_out, v_ffn2_norm, v_ffn2_w_gu, v_ffn2_w_down, v_final_norm):
    return _step(dict(locals()))
```

```python
import functools
import math

import numpy as np
import jax
import jax.numpy as jnp
from jax import lax
from jax.experimental import pallas as pl
from jax.experimental.pallas import tpu as pltpu

F32 = jnp.float32
BF16 = jnp.bfloat16

N_META = 16
MLA_HEADS = 8
Q_LORA = 384
KV_LORA = 256
QK_NOPE = 64
QK_ROPE = 32
V_HEAD = 64
ROPE_THETA = 10000.0
HG_HEADS = 4
HG_D = 128
HG_CHUNK = 64
EPS = 1e-6
NEG_BIG = -1e30
F_MIN = 1e-20
DEPTH = 4

ADAM_LR = 0.001
ADAM_B1 = 0.9
ADAM_B2 = 0.999
ADAM_EPS = 1e-08
ADAM_WD = 0.01
ADAM_STEP = 10

LANES = 128
FRONT = (-N_META) % HG_CHUNK
ROW_X = FRONT + N_META
ROW_TILE = 640
HEAD_W = 128
QW = MLA_HEADS * HEAD_W
PREP_HEADS = 4
VMEM_LIMIT = 56 * 1024 * 1024
MATMUL_VMEM = 42 * 1024 * 1024
PACK_W = 1024
MESH = pl.DeviceIdType.MESH

Z_Q, Z_KPA, Z_KV, Z_KPB, Z_PAD, Z_GA = 0, 384, 512, 768, 896, 1024


def _z_layout(d):
    ga = Z_GA
    gb = ga + d
    hq = gb + d
    hf = hq + 512
    hi = hf + 512
    hg = hi + 512
    return dict(ga=ga, gb=gb, hq=hq, hf=hf, hi=hi, hg=hg, total=hg + 512)


def _pick(dim, cap, mult=LANES):
    if dim <= cap:
        return dim
    best = None
    for t in range(mult, cap + 1, mult):
        if dim % t == 0:
            best = t
    assert best is not None, (dim, cap, mult)
    return best


def _params(*sem):
    return pltpu.CompilerParams(dimension_semantics=sem, vmem_limit_bytes=VMEM_LIMIT)


def _sigmoid(x):
    return 1.0 / (1.0 + jnp.exp(-x))


def _row_valid(row0, n, lv):
    r = row0 + lax.broadcasted_iota(jnp.int32, (n, 1), 0)
    return ((r >= FRONT) & (r < lv)).astype(F32)


_DIMS = {"nn": (((1,), (0,)), ((), ())), "nt": (((1,), (1,)), ((), ())), "tn": (((0,), (0,)), ((), ()))}


def _matmul(pairs, mode, *, name, out_dtype=F32, res=None, scale=1.0):
    a0, b0 = pairs[0]
    if mode == "nn":
        (m, k), n = a0.shape, b0.shape[1]
    elif mode == "nt":
        (m, k), n = a0.shape, b0.shape[0]
    else:
        (k, m), n = a0.shape, b0.shape[1]
    if mode == "tn":
        tm, tn, tk = _pick(m, 1024), _pick(n, 1408), _pick(k, ROW_TILE, 8)
    else:
        if k > m:
            tm, tn, kcap = _pick(m, 1408, 16), _pick(n, 1408), 1664
        else:
            tm, tn, kcap = _pick(m, ROW_TILE, 8), _pick(n, 2816), 2816
        out_b = jnp.dtype(out_dtype).itemsize
        per_k = len(pairs) * 2 * (tm * a0.dtype.itemsize + tn * b0.dtype.itemsize)
        fixed = tm * tn * (2 * out_b + 4 + (8 if res is not None else 0))
        tk = _pick(k, kcap)
        while tk > LANES and fixed + per_k * tk > MATMUL_VMEM:
            tk = _pick(k, tk - LANES)
    nk = k // tk
    npair = len(pairs)
    dims = _DIMS[mode]

    def body(*refs):
        ins = refs[:2 * npair]
        pos = 2 * npair
        res_ref = None
        if res is not None:
            res_ref = refs[pos]
            pos += 1
        o_ref = refs[pos]
        kk = pl.program_id(2)

        part = None
        for p in range(npair):
            a = ins[2 * p][...].astype(BF16)
            b = ins[2 * p + 1][...].astype(BF16)
            d = lax.dot_general(a, b, dims, preferred_element_type=F32)
            part = d if part is None else part + d

        def finish(r):
            if scale != 1.0:
                r = r * scale
            if res_ref is not None:
                r = r + res_ref[...]
            o_ref[...] = r.astype(out_dtype)

        if nk == 1:
            finish(part)
            return
        acc = refs[pos + 1]

        @pl.when(kk == 0)
        def _():
            acc[...] = part

        @pl.when(kk > 0)
        def _():
            acc[...] += part

        @pl.when(kk == nk - 1)
        def _():
            finish(acc[...])

    if mode == "nn":
        a_spec = pl.BlockSpec((tm, tk), lambda i, j, q: (i, q))
        b_spec = pl.BlockSpec((tk, tn), lambda i, j, q: (q, j))
    elif mode == "nt":
        a_spec = pl.BlockSpec((tm, tk), lambda i, j, q: (i, q))
        b_spec = pl.BlockSpec((tn, tk), lambda i, j, q: (j, q))
    else:
        a_spec = pl.BlockSpec((tk, tm), lambda i, j, q: (q, i))
        b_spec = pl.BlockSpec((tk, tn), lambda i, j, q: (q, j))
    o_spec = pl.BlockSpec((tm, tn), lambda i, j, q: (i, j))
    in_specs, args = [], []
    for a, b in pairs:
        in_specs += [a_spec, b_spec]
        args += [a, b]
    if res is not None:
        in_specs.append(o_spec)
        args.append(res)
    return pl.pallas_call(
        body, name=name, grid=(m // tm, n // tn, nk), in_specs=in_specs, out_specs=o_spec,
        out_shape=jax.ShapeDtypeStruct((m, n), out_dtype),
        scratch_shapes=[pltpu.VMEM((tm, tn), F32)] if nk > 1 else [],
        compiler_params=_params("parallel", "parallel", "arbitrary"),
    )(*args)


def _rmsnorm_fwd(x, w, *, width, col_block, name, transposed=False):
    lp = x.shape[0]
    tm = _pick(lp, ROW_TILE, 8)

    def body(x_ref, w_ref, o_ref, *ot_ref):
        xv = x_ref[...]
        r = lax.rsqrt(jnp.mean(xv * xv, axis=-1, keepdims=True) + EPS)
        y = xv * r * w_ref[...]
        o_ref[...] = y.astype(BF16)
        if transposed:
            ot_ref[0][...] = y.T.astype(BF16)

    out_specs = [pl.BlockSpec((tm, width), lambda i: (i, 0))]
    out_shape = [jax.ShapeDtypeStruct((lp, width), BF16)]
    if transposed:
        out_specs.append(pl.BlockSpec((width, tm), lambda i: (0, i)))
        out_shape.append(jax.ShapeDtypeStruct((width, lp), BF16))
    outs = pl.pallas_call(
        body, name=name, grid=(lp // tm,),
        in_specs=[pl.BlockSpec((tm, width), lambda i: (i, col_block)), pl.BlockSpec((1, width), lambda i: (0, 0))],
        out_specs=out_specs, out_shape=out_shape, compiler_params=_params("parallel"),
    )(x, w.reshape(1, width))
    return tuple(outs) if transposed else outs[0]


def _rmsnorm_bwd(x, w, dy, *, width, col_block, lv, name, dres=None):
    lp = x.shape[0]
    tm = _pick(lp, ROW_TILE, 8)

    def body(*refs):
        if dres is None:
            x_ref, w_ref, dy_ref, dx_ref, dw_ref = refs
            dres_ref = None
        else:
            x_ref, w_ref, dy_ref, dres_ref, dx_ref, dw_ref = refs
        i = pl.program_id(0)
        xv = x_ref[...]
        dyv = dy_ref[...] * _row_valid(i * tm, tm, lv)
        r = lax.rsqrt(jnp.mean(xv * xv, axis=-1, keepdims=True) + EPS)
        wdy = dyv * w_ref[...]
        dx = r * wdy - xv * (r * r * r) * jnp.mean(xv * wdy, axis=-1, keepdims=True)
        if dres_ref is not None:
            dx = dx + dres_ref[...]
        dx_ref[...] = dx

        @pl.when(i == 0)
        def _():
            dw_ref[...] = jnp.zeros_like(dw_ref)

        dw_ref[...] += jnp.sum(dyv * xv * r, axis=0, keepdims=True)

    row = pl.BlockSpec((tm, width), lambda i: (i, 0))
    in_specs = [pl.BlockSpec((tm, width), lambda i: (i, col_block)), pl.BlockSpec((1, width), lambda i: (0, 0)), row]
    args = [x, w.reshape(1, width), dy]
    if dres is not None:
        in_specs.append(row)
        args.append(dres)
    dx, dw = pl.pallas_call(
        body, name=name, grid=(lp // tm,), in_specs=in_specs,
        out_specs=[row, pl.BlockSpec((1, width), lambda i: (0, 0))],
        out_shape=[jax.ShapeDtypeStruct((lp, width), F32), jax.ShapeDtypeStruct((1, width), F32)],
        compiler_params=_params("arbitrary"),
    )(*args)
    return dx, dw[0]


def _swiglu_fwd(g, u, *, name):
    lp, f = g.shape
    tm, tf = _pick(lp, ROW_TILE, 8), _pick(f, 1408)

    def body(g_ref, u_ref, o_ref, ot_ref):
        gv = g_ref[...].astype(F32)
        act = gv * _sigmoid(gv) * u_ref[...].astype(F32)
        o_ref[...] = act.astype(BF16)
        ot_ref[...] = act.T.astype(BF16)

    spec = pl.BlockSpec((tm, tf), lambda i, j: (i, j))
    return pl.pallas_call(
        body, name=name, grid=(lp // tm, f // tf), in_specs=[spec, spec],
        out_specs=[spec, pl.BlockSpec((tf, tm), lambda i, j: (j, i))],
        out_shape=[jax.ShapeDtypeStruct((lp, f), BF16), jax.ShapeDtypeStruct((f, lp), BF16)],
        compiler_params=_params("parallel", "parallel"),
    )(g, u)


def _swiglu_bwd(dact, g, u, *, name):
    lp, f = g.shape
    tm, tf = _pick(lp, ROW_TILE, 8), _pick(f, 1408)

    def body(d_ref, g_ref, u_ref, dg_ref, du_ref):
        gv, dv = g_ref[...].astype(F32), d_ref[...].astype(F32)
        s = _sigmoid(gv)
        dg_ref[...] = (dv * u_ref[...].astype(F32) * s * (1.0 + gv * (1.0 - s))).astype(BF16)
        du_ref[...] = (dv * gv * s).astype(BF16)

    spec = pl.BlockSpec((tm, tf), lambda i, j: (i, j))
    return pl.pallas_call(
        body, name=name, grid=(lp // tm, f // tf), in_specs=[spec, spec, spec], out_specs=[spec, spec],
        out_shape=[jax.ShapeDtypeStruct((lp, f), BF16)] * 2, compiler_params=_params("parallel", "parallel"),
    )(dact, g, u)


def _merge_fwd(ya, yb, z, *, zl, name):
    lp, d = ya.shape
    tm, td = _pick(lp, ROW_TILE, 8), _pick(d, 512)
    oa, ob = zl["ga"] // td, zl["gb"] // td

    def body(ya_ref, yb_ref, ga_ref, gb_ref, o_ref, ot_ref):
        mg = _sigmoid(ga_ref[...]) * ya_ref[...] + _sigmoid(gb_ref[...]) * yb_ref[...]
        o_ref[...] = mg.astype(BF16)
        ot_ref[...] = mg.T.astype(BF16)

    spec = pl.BlockSpec((tm, td), lambda i, j: (i, j))
    return pl.pallas_call(
        body, name=name, grid=(lp // tm, d // td),
        in_specs=[spec, spec, pl.BlockSpec((tm, td), lambda i, j: (i, oa + j)),
                  pl.BlockSpec((tm, td), lambda i, j: (i, ob + j))],
        out_specs=[spec, pl.BlockSpec((td, tm), lambda i, j: (j, i))],
        out_shape=[jax.ShapeDtypeStruct((lp, d), BF16), jax.ShapeDtypeStruct((d, lp), BF16)],
        compiler_params=_params("parallel", "parallel"),
    )(ya, yb, z, z)


def _merge_bwd(dmg, ya, yb, z, *, zl, name):
    lp, d = ya.shape
    tm, td = _pick(lp, ROW_TILE, 8), _pick(d, 512)
    oa, ob = zl["ga"] // td, zl["gb"] // td

    def body(d_ref, ya_ref, yb_ref, ga_ref, gb_ref, dya_ref, dyb_ref, dga_ref, dgb_ref):
        dv = d_ref[...]
        sa, sb = _sigmoid(ga_ref[...]), _sigmoid(gb_ref[...])
        dya_ref[...] = (dv * sa).astype(BF16)
        dyb_ref[...] = (dv * sb).astype(BF16)
        dga_ref[...] = dv * ya_ref[...] * sa * (1.0 - sa)
        dgb_ref[...] = dv * yb_ref[...] * sb * (1.0 - sb)

    spec = pl.BlockSpec((tm, td), lambda i, j: (i, j))
    return pl.pallas_call(
        body, name=name, grid=(lp // tm, d // td),
        in_specs=[spec, spec, spec, pl.BlockSpec((tm, td), lambda i, j: (i, oa + j)),
                  pl.BlockSpec((tm, td), lambda i, j: (i, ob + j))],
        out_specs=[spec] * 4,
        out_shape=[jax.ShapeDtypeStruct((lp, d), BF16)] * 2 + [jax.ShapeDtypeStruct((lp, d), F32)] * 2,
        compiler_params=_params("parallel", "parallel"),
    )(dmg, ya, yb, z, z)


def _qkv_prep_fwd(q2, kv, z, cos_t, sin_t, *, name):
    lp = q2.shape[0]
    tm = _pick(lp, ROW_TILE, 8)
    h = MLA_HEADS
    wd = PREP_HEADS * HEAD_W

    def body(qa_ref, qb_ref, kv_ref, za_ref, zb_ref, c_ref, s_ref, q_ref, k_ref, v_ref):
        c, s = c_ref[...], s_ref[...]
        lane = lax.broadcasted_iota(jnp.int32, (tm, HEAD_W), 1)
        kr = jnp.where(lane >= QK_NOPE, za_ref[...] * c + zb_ref[...] * s, 0.0)
        for g in range(PREP_HEADS):
            sl = slice(g * HEAD_W, (g + 1) * HEAD_W)
            q_ref[:, sl] = ((qa_ref[:, sl] * c + qb_ref[:, sl] * s) * Q_SCALE).astype(BF16)
            kvv = kv_ref[:, sl]
            k_ref[:, sl] = (jnp.where(lane < QK_NOPE, kvv, 0.0) + kr).astype(BF16)
            v_ref[:, sl] = jnp.where(lane >= QK_NOPE, kvv, 0.0).astype(BF16)

    blk = lambda w, f: pl.BlockSpec((tm, w), f)
    out = blk(wd, lambda i, j: (i, j))
    return pl.pallas_call(
        body, name=name, grid=(lp // tm, h // PREP_HEADS),
        in_specs=[blk(wd, lambda i, j: (i, j)), blk(wd, lambda i, j: (i, h // PREP_HEADS + j)),
                  blk(wd, lambda i, j: (i, j)),
                  blk(HEAD_W, lambda i, j: (i, Z_KPA // HEAD_W)), blk(HEAD_W, lambda i, j: (i, Z_KPB // HEAD_W)),
                  blk(HEAD_W, lambda i, j: (i, 0)), blk(HEAD_W, lambda i, j: (i, 0))],
        out_specs=[out, out, out], out_shape=[jax.ShapeDtypeStruct((lp, QW), BF16)] * 3,
        compiler_params=_params("parallel", "parallel"),
    )(q2, q2, kv, z, z, cos_t, sin_t)


def _qkv_prep_bwd(dq, dk, dv, cos_t, sin_t, *, name):
    lp = dq.shape[0]
    tm = _pick(lp, ROW_TILE, 8)
    h = MLA_HEADS
    wd = PREP_HEADS * HEAD_W

    def body(dq_ref, dk_ref, dv_ref, c_ref, s_ref, dqa_ref, dqb_ref, dkv_ref, dza_ref, dzb_ref):
        j = pl.program_id(1)
        c, s = c_ref[...], s_ref[...]
        lane = lax.broadcasted_iota(jnp.int32, (tm, HEAD_W), 1)
        dkr = jnp.zeros((tm, HEAD_W), F32)
        for g in range(PREP_HEADS):
            sl = slice(g * HEAD_W, (g + 1) * HEAD_W)
            dqv, dkv_ = dq_ref[:, sl], dk_ref[:, sl]
            dqa_ref[:, sl] = (dqv * c).astype(BF16)
            dqb_ref[:, sl] = (dqv * s).astype(BF16)
            dkv_ref[:, sl] = jnp.where(lane < QK_NOPE, dkv_, dv_ref[:, sl]).astype(BF16)
            dkr = dkr + jnp.where(lane >= QK_NOPE, dkv_, 0.0)

        @pl.when(j == 0)
        def _():
            dza_ref[...] = jnp.zeros_like(dza_ref)
            dzb_ref[...] = jnp.zeros_like(dzb_ref)

        dza_ref[...] += dkr * c
        dzb_ref[...] += dkr * s

    blk = lambda w, f: pl.BlockSpec((tm, w), f)
    per_head, shared = blk(wd, lambda i, j: (i, j)), blk(HEAD_W, lambda i, j: (i, 0))
    return pl.pallas_call(
        body, name=name, grid=(lp // tm, h // PREP_HEADS),
        in_specs=[per_head, per_head, per_head, shared, shared],
        out_specs=[per_head, per_head, per_head, shared, shared],
        out_shape=[jax.ShapeDtypeStruct((lp, QW), BF16)] * 3 + [jax.ShapeDtypeStruct((lp, HEAD_W), F32)] * 2,
        compiler_params=_params("parallel", "arbitrary"),
    )(dq, dk, dv, cos_t, sin_t)


def _attn_tile(lp):
    return _pick(lp, ROW_TILE, LANES)


Q_SCALE = (QK_NOPE + QK_ROPE) ** -0.5 * math.log2(math.e)
ATT_HP = 2
ATT_HP_FWD = 4


def _attn_consts(lp):
    t = _attn_tile(lp)
    nb = lp // t
    r = np.arange(t)
    causal = np.where(r[None, :] <= r[:, None], 0.0, NEG_BIG).astype(np.float32)
    front = np.where(r >= FRONT, 0.0, NEG_BIG).astype(np.float32)[None, :]
    diag = np.stack([np.minimum(causal, front), causal])
    qmaj = [(i, j) for i in range(nb) for j in range(i + 1)]
    kmaj = [(i, j) for j in range(nb) for i in range(j, nb)]
    tab = lambda pairs, c: jnp.asarray([p[c] for p in pairs], jnp.int32)
    return dict(diag=jnp.asarray(diag), front=jnp.asarray(front),
                fwd=(tab(qmaj, 0), tab(qmaj, 1)), bwd=(tab(kmaj, 0), tab(kmaj, 1)))


def _attn_fwd(q, k, v, ac, *, lv, name):
    lp = q.shape[0]
    t = _attn_tile(lp)
    nb = lp // t
    rep = t // HEAD_W
    qtab, ktab = ac["fwd"]

    def body(qt_ref, kt_ref, q_ref, k_ref, v_ref, bd_ref, bf_ref, o_ref, lse_ref, m_s, l_s, acc_s):
        step_id = pl.program_id(1)
        qb, kb = qt_ref[step_id], kt_ref[step_id]

        @pl.when(kb == 0)
        def _():
            m_s[...] = jnp.full_like(m_s, NEG_BIG)
            l_s[...] = jnp.zeros_like(l_s)
            acc_s[...] = jnp.zeros_like(acc_s)

        def step(bias):
            b = None if bias is None else bias()
            for hh in range(ATT_HP_FWD):
                sl = slice(hh * HEAD_W, (hh + 1) * HEAD_W)
                s = lax.dot_general(q_ref[:, sl], k_ref[:, sl], _DIMS["nt"], preferred_element_type=F32)
                if b is not None:
                    s = s + b
                m_prev = m_s[:, sl]
                m_new = jnp.maximum(m_prev, jnp.max(s, axis=-1, keepdims=True))
                alpha = jnp.exp2(m_prev - m_new)
                p = jnp.exp2(s - jnp.tile(m_new, (1, rep)))
                l_s[:, sl] = alpha * l_s[:, sl] + jnp.sum(p, axis=-1, keepdims=True)
                acc_s[:, sl] = alpha * acc_s[:, sl] + jnp.dot(p.astype(BF16), v_ref[:, sl],
                                                              preferred_element_type=F32)
                m_s[:, sl] = m_new

        @pl.when((kb > 0) & (kb < qb))
        def _():
            step(None)

        @pl.when((kb == 0) & (qb > 0))
        def _():
            step(lambda: bf_ref[...])

        @pl.when(kb == qb)
        def _():
            step(lambda: bd_ref[0])
            l = l_s[...]
            o_ref[...] = acc_s[...] / l * _row_valid(qb * t, t, lv)
            lse_ref[...] = m_s[...] + jnp.log2(l)

    wd = ATT_HP_FWD * HEAD_W
    qs = pl.BlockSpec((t, wd), lambda h, s, qt, kt: (qt[s], h))
    ks = pl.BlockSpec((t, wd), lambda h, s, qt, kt: (kt[s], h))
    grid_spec = pltpu.PrefetchScalarGridSpec(
        num_scalar_prefetch=2, grid=(MLA_HEADS // ATT_HP_FWD, int(qtab.shape[0])),
        in_specs=[qs, ks, ks, pl.BlockSpec((1, t, t), lambda h, s, qt, kt: (jnp.minimum(qt[s], 1), 0, 0)),
                  pl.BlockSpec((1, t), lambda h, s, qt, kt: (0, 0))],
        out_specs=[qs, qs],
        scratch_shapes=[pltpu.VMEM((t, wd), F32), pltpu.VMEM((t, wd), F32), pltpu.VMEM((t, wd), F32)])
    return pl.pallas_call(
        body, name=name, grid_spec=grid_spec, out_shape=[jax.ShapeDtypeStruct((lp, QW), F32)] * 2,
        compiler_params=_params("parallel", "arbitrary"),
    )(qtab, ktab, q, k, v, ac["diag"], ac["front"])


def _attn_delta(do, o, *, name):
    lp = do.shape[0]
    tm = _pick(lp, ROW_TILE, 8)
    wd = PREP_HEADS * HEAD_W

    def body(do_ref, o_ref, d_ref):
        for g in range(PREP_HEADS):
            sl = slice(g * HEAD_W, (g + 1) * HEAD_W)
            d_ref[:, sl] = jnp.broadcast_to(jnp.sum(do_ref[:, sl] * o_ref[:, sl], axis=-1, keepdims=True),
                                            (tm, HEAD_W))

    spec = pl.BlockSpec((tm, wd), lambda i, j: (i, j))
    return pl.pallas_call(
        body, name=name, grid=(lp // tm, MLA_HEADS // PREP_HEADS), in_specs=[spec, spec], out_specs=spec,
        out_shape=jax.ShapeDtypeStruct((lp, QW), F32), compiler_params=_params("parallel", "parallel"),
    )(do, o)


def _attn_bwd(q, k, v, do, lse, delta, ac, *, name):
    lp = q.shape[0]
    t = _attn_tile(lp)
    nb = lp // t
    rep = t // HEAD_W
    scale = (QK_NOPE + QK_ROPE) ** -0.5
    qtab, ktab = ac["bwd"]

    def body(qt_ref, kt_ref, q_ref, k_ref, v_ref, do_ref, lse_ref, dl_ref, bd_ref, bf_ref, dq_ref, dk_ref, dv_ref,
             dk_s, dv_s):
        step_id = pl.program_id(1)
        qb, kb = qt_ref[step_id], kt_ref[step_id]

        @pl.when(qb == kb)
        def _():
            dk_s[...] = jnp.zeros_like(dk_s)
            dv_s[...] = jnp.zeros_like(dv_s)

        def step(bias):
            b = None if bias is None else bias()
            rows = pl.ds(pl.multiple_of(qb * t, t), t)
            contribs = []
            for hh in range(ATT_HP):
                sl = slice(hh * HEAD_W, (hh + 1) * HEAD_W)
                qv, kv_, vv = q_ref[:, sl], k_ref[:, sl], v_ref[:, sl]
                dov = do_ref[:, sl].astype(BF16)
                s = lax.dot_general(qv, kv_, _DIMS["nt"], preferred_element_type=F32)
                if b is not None:
                    s = s + b
                p = jnp.exp2(s - jnp.tile(lse_ref[:, sl], (1, rep)))
                dv_s[:, sl] += lax.dot_general(p.astype(BF16), dov, _DIMS["tn"], preferred_element_type=F32)
                dp = lax.dot_general(dov, vv, _DIMS["nt"], preferred_element_type=F32)
                ds = (p * (dp - jnp.tile(dl_ref[:, sl], (1, rep))) * scale).astype(BF16)
                dk_s[:, sl] += lax.dot_general(ds, qv, _DIMS["tn"], preferred_element_type=F32)
                contribs.append(jnp.dot(ds, kv_, preferred_element_type=F32))
            contrib = jnp.concatenate(contribs, axis=1)

            @pl.when(kb == 0)
            def _():
                dq_ref[rows, :] = contrib

            @pl.when(kb > 0)
            def _():
                dq_ref[rows, :] += contrib

        @pl.when((kb > 0) & (kb < qb))
        def _():
            step(None)

        @pl.when((kb == 0) & (qb > 0))
        def _():
            step(lambda: bf_ref[...])

        @pl.when(kb == qb)
        def _():
            step(lambda: bd_ref[0])

        @pl.when(qb == nb - 1)
        def _():
            dk_ref[...] = dk_s[...] * (1.0 / Q_SCALE)
            dv_ref[...] = dv_s[...]

    wd = ATT_HP * HEAD_W
    qs = pl.BlockSpec((t, wd), lambda h, s, qt, kt: (qt[s], h))
    ks = pl.BlockSpec((t, wd), lambda h, s, qt, kt: (kt[s], h))
    dqs = pl.BlockSpec((lp, wd), lambda h, s, qt, kt: (0, h))
    grid_spec = pltpu.PrefetchScalarGridSpec(
        num_scalar_prefetch=2, grid=(MLA_HEADS // ATT_HP, int(qtab.shape[0])),
        in_specs=[qs, ks, ks, qs, qs, qs,
                  pl.BlockSpec((1, t, t), lambda h, s, qt, kt: (jnp.minimum(qt[s], 1), 0, 0)),
                  pl.BlockSpec((1, t), lambda h, s, qt, kt: (0, 0))],
        out_specs=[dqs, ks, ks],
        scratch_shapes=[pltpu.VMEM((t, wd), F32), pltpu.VMEM((t, wd), F32)])
    return pl.pallas_call(
        body, name=name, grid_spec=grid_spec, out_shape=[jax.ShapeDtypeStruct((lp, QW), F32)] * 3,
        compiler_params=_params("arbitrary", "arbitrary"),
    )(qtab, ktab, q, k, v, do, lse, delta, ac["diag"], ac["front"])


HG_UNROLL = 2
HG_LEVELS = (64, 32, 16, 8, 4, 2)
N_LEV = len(HG_LEVELS)


def _hgrn_consts():
    c = HG_CHUNK
    m = np.zeros((N_LEV + 2, c, c), np.float32)
    masks = np.zeros((N_LEV, c, c), np.float32)
    for li, p in enumerate(HG_LEVELS):
        for t in range(c):
            mid = (t // p) * p + p // 2
            if t >= mid:
                m[li, t, mid:t + 1] = 1.0
            else:
                m[li, t, t + 1:mid] = 1.0
            for s in range(c):
                if s // p == t // p and t >= mid and s < mid:
                    masks[li, t, s] = 1.0
    for t in range(c):
        m[N_LEV, t, :t + 1] = 1.0
        m[N_LEV + 1, t, t + 1:] = 1.0
    mall = m.reshape((N_LEV + 2) * c, c)
    return jnp.asarray(mall, BF16), jnp.asarray(mall.T.copy(), BF16), jnp.asarray(masks, F32)


def _split_terms(x):
    hi = x.astype(BF16)
    lo = (x - hi.astype(F32)).astype(BF16)
    return jnp.concatenate([hi, lo], axis=1)


def _sum_terms(e3):
    return e3[:, :HG_D] + e3[:, HG_D:]


def _hgrn_chunk_fwd(hq, hf, hi, lb, valid, mall, masks, st):
    c = HG_CHUNK
    scale = HG_D ** -0.5
    sq = _sigmoid(hq)
    qv = hq * sq
    sg = _sigmoid(hf)
    f = lb + (1.0 - lb) * sg
    fc = jnp.maximum(f, F_MIN)
    lf = jnp.log(fc) * valid
    kv = (1.0 - lb) * (1.0 - sg) * valid
    e = _sum_terms(jnp.dot(mall, _split_terms(lf), preferred_element_type=F32))
    x = jnp.exp(e)
    a = jnp.zeros((c, c), F32)
    qe, ke = [], []
    for l in range(N_LEV):
        xl = x[l * c:(l + 1) * c]
        qe.append(qv * xl)
        ke.append(kv * xl)
        a = a + masks[l] * lax.dot_general(qe[l].astype(BF16), ke[l].astype(BF16), _DIMS["nt"],
                                           preferred_element_type=F32)
    row = lax.broadcasted_iota(jnp.int32, (c, c), 0)
    col = lax.broadcasted_iota(jnp.int32, (c, c), 1)
    a = a + jnp.where(row == col, jnp.sum(qv * kv, axis=-1, keepdims=True), 0.0)
    xb = x[N_LEV * c:(N_LEV + 1) * c]
    qb = qv * xb
    kb = kv * x[(N_LEV + 1) * c:]
    x_last = xb[c - 1:c]
    hib = hi.astype(BF16)
    o = scale * (jnp.dot(a.astype(BF16), hib, preferred_element_type=F32)
                 + lax.dot_general(qb.astype(BF16), st.astype(BF16), _DIMS["nt"], preferred_element_type=F32))
    st_new = st * x_last + lax.dot_general(hib, kb.astype(BF16), _DIMS["tn"], preferred_element_type=F32)
    saved = dict(sq=sq, qv=qv, sg=sg, f=f, fc=fc, kv=kv, x=x, a=a, qe=qe, ke=ke, qb=qb, kb=kb, x_last=x_last)
    return o, st_new, saved


def _split_ride(refs, n_in, n_out, n_scratch, ride):
    ri = len(ride.args) if ride else 0
    ro = len(ride.out_shape) if ride else 0
    a = n_in + ri
    b = a + n_out + ro
    c = b + n_scratch
    return refs[:n_in], refs[a:a + n_out], refs[b:c], refs[n_in:a] + refs[a + n_out:b] + refs[c:]


def _ride_call(ride):
    if ride is None:
        return [], [], [], [], []
    hbm = [HBM_SPEC] * len(ride.args)
    return hbm, list(ride.args), [HBM_SPEC] * len(ride.out_shape), list(ride.out_shape), list(ride.scratch)


def _hgrn_fwd(z, lb, nw, consts, *, zl, lv, name, ride=None):
    lp = z.shape[0]
    tb = _pick(lp, ROW_TILE, HG_CHUNK)
    ncb = tb // HG_CHUNK
    nb = lp // tb
    mall, _, masks = consts
    w = HG_HEADS * HG_D

    def body(*refs):
        ins, outs, (st_s,), ride_refs = _split_ride(refs, 8, 3, 1, ride)
        hq_ref, hf_ref, hi_ref, hg_ref, lb_ref, nw_ref, mall_ref, masks_ref = ins
        o_ref, ob_ref, st_ref = outs
        i = pl.program_id(0)

        @pl.when(i == 0)
        def _():
            st_s[...] = jnp.zeros_like(st_s)
            if ride is not None:
                ride.start(*ride_refs)

        nwv = nw_ref[...]
        mallv, masksv = mall_ref[...], masks_ref[...]

        def chunk(cix, carry):
            r0 = pl.multiple_of(cix * HG_CHUNK, HG_CHUNK)
            rows = pl.ds(r0, HG_CHUNK)
            valid = _row_valid(i * tb + r0, HG_CHUNK, lv)
            for h in range(HG_HEADS):
                sl = slice(h * HG_D, (h + 1) * HG_D)
                st = st_s[h]
                st_ref[h, cix] = st
                o, st_new, _ = _hgrn_chunk_fwd(hq_ref[rows, sl], hf_ref[rows, sl], hi_ref[rows, sl], lb_ref[:, sl],
                                               valid, mallv, masksv, st)
                st_s[h] = st_new
                o_ref[rows, sl] = o
                hg = hg_ref[rows, sl]
                r = lax.rsqrt(jnp.mean(o * o, axis=-1, keepdims=True) + EPS)
                ob_ref[rows, sl] = (o * r * nwv * (hg * _sigmoid(hg))).astype(BF16)
            return carry

        lax.fori_loop(0, ncb, chunk, 0, unroll=HG_UNROLL)

        if ride is not None:
            @pl.when(i == nb - 1)
            def _():
                ride.finish(*ride_refs)

    zb = lambda off: pl.BlockSpec((tb, w), lambda i: (i, off // w))
    full = pl.BlockSpec((tb, w), lambda i: (i, 0))
    const = lambda shape: pl.BlockSpec(shape, lambda i: (0,) * len(shape))
    r_in, r_args, r_out, r_shape, r_scratch = _ride_call(ride)
    outs = pl.pallas_call(
        body, name=name, grid=(nb,),
        in_specs=[zb(zl["hq"]), zb(zl["hf"]), zb(zl["hi"]), zb(zl["hg"]), const((1, w)), const((1, HG_D)),
                  const(mall.shape), const(masks.shape)] + r_in,
        out_specs=[full, full, pl.BlockSpec((HG_HEADS, ncb, HG_D, HG_D), lambda i: (0, i, 0, 0))] + r_out,
        out_shape=[jax.ShapeDtypeStruct((lp, w), F32), jax.ShapeDtypeStruct((lp, w), BF16),
                   jax.ShapeDtypeStruct((HG_HEADS, lp // HG_CHUNK, HG_D, HG_D), F32)] + r_shape,
        scratch_shapes=[pltpu.VMEM((HG_HEADS, HG_D, HG_D), F32)] + r_scratch,
        compiler_params=_params("arbitrary"),
    )(z, z, z, z, lb.reshape(1, w), nw.reshape(1, HG_D), mall, masks, *r_args)
    return outs[0], outs[1], outs[2], list(outs[3:])


def _hgrn_chunk_bwd(hq, hf, hi, hg, o, dout, st, dst, lbv, nwv, valid, mallv, malltv, masksv):
    c = HG_CHUNK
    scale = HG_D ** -0.5
    _, _, sv = _hgrn_chunk_fwd(hq, hf, hi, lbv, valid, mallv, masksv, st)
    shg = _sigmoid(hg)
    r = lax.rsqrt(jnp.mean(o * o, axis=-1, keepdims=True) + EPS)
    don = dout * (hg * shg)
    dhg = dout * (o * r * nwv) * shg * (1.0 + hg * (1.0 - shg))
    dnw = jnp.sum(don * o * r, axis=0, keepdims=True)
    wd = don * nwv
    do = r * wd - o * (r * r * r) * jnp.mean(o * wd, axis=-1, keepdims=True)
    dob16, hib = do.astype(BF16), hi.astype(BF16)
    dst16 = dst.astype(BF16)
    da = scale * lax.dot_general(dob16, hib, _DIMS["nt"], preferred_element_type=F32)
    dv = (scale * lax.dot_general(sv["a"].astype(BF16), dob16, _DIMS["tn"], preferred_element_type=F32)
          + lax.dot_general(sv["kb"].astype(BF16), dst16, _DIMS["nt"], preferred_element_type=F32))
    dkb = jnp.dot(hib, dst16, preferred_element_type=F32)
    dqb = scale * jnp.dot(dob16, st.astype(BF16), preferred_element_type=F32)
    dst_new = dst * sv["x_last"] + scale * lax.dot_general(dob16, sv["qb"].astype(BF16), _DIMS["tn"],
                                                           preferred_element_type=F32)
    dxl = jnp.sum(dst * st, axis=0, keepdims=True)
    x = sv["x"]
    dq = dqb * x[N_LEV * c:(N_LEV + 1) * c]
    dk = dkb * x[(N_LEV + 1) * c:]
    de = []
    for l in range(N_LEV):
        dam = (masksv[l] * da).astype(BF16)
        dqe = jnp.dot(dam, sv["ke"][l].astype(BF16), preferred_element_type=F32)
        dke = lax.dot_general(dam, sv["qe"][l].astype(BF16), _DIMS["tn"], preferred_element_type=F32)
        xl = x[l * c:(l + 1) * c]
        dq = dq + dqe * xl
        dk = dk + dke * xl
        de.append(dqe * sv["qe"][l] + dke * sv["ke"][l])
    dd = scale * jnp.sum(do * hi, axis=-1, keepdims=True)
    dq = dq + dd * sv["kv"]
    dk = dk + dd * sv["qv"]
    last = lax.broadcasted_iota(jnp.int32, (c, 1), 0) == c - 1
    de.append(dqb * sv["qb"] + jnp.where(last, dxl * sv["x_last"], 0.0))
    de.append(dkb * sv["kb"])
    dlf = _sum_terms(jnp.dot(malltv, _split_terms(jnp.concatenate(de, axis=0)), preferred_element_type=F32))
    sg, sq = sv["sg"], sv["sq"]
    df = jnp.where(sv["f"] > F_MIN, dlf * valid / sv["fc"], 0.0)
    dkm = dk * valid
    dhf = (df - dkm) * (1.0 - lbv) * sg * (1.0 - sg)
    dlb = jnp.sum((df - dkm) * (1.0 - sg), axis=0, keepdims=True)
    dhq = dq * sq * (1.0 + hq * (1.0 - sq))
    return dhq, dhf, dv, dhg, dlb, dnw, dst_new


def _hgrn_bwd(z, o_pre, dob, states, lb, nw, consts, *, zl, lv, name, ride=None):
    lp = z.shape[0]
    tb = _pick(lp, ROW_TILE, HG_CHUNK)
    ncb = tb // HG_CHUNK
    nb = lp // tb
    mall, mall_t, masks = consts
    w = HG_HEADS * HG_D
    c = HG_CHUNK

    def body(*refs):
        ins, outs, (dst_s,), ride_refs = _split_ride(refs, 12, 6, 1, ride)
        hq_ref, hf_ref, hi_ref, hg_ref, o_ref, dob_ref, st_ref, lb_ref, nw_ref, mall_ref, mallt_ref, masks_ref = ins
        dhq_ref, dhf_ref, dhi_ref, dhg_ref, dlb_ref, dnw_ref = outs
        i = pl.program_id(0)
        blk = nb - 1 - i

        @pl.when(i == 0)
        def _():
            dst_s[...] = jnp.zeros_like(dst_s)
            dlb_ref[...] = jnp.zeros_like(dlb_ref)
            dnw_ref[...] = jnp.zeros_like(dnw_ref)
            if ride is not None:
                ride.start(*ride_refs)

        nwv = nw_ref[...]
        mallv, malltv, masksv = mall_ref[...], mallt_ref[...], masks_ref[...]

        def chunk(jx, carry):
            cix = ncb - 1 - jx
            r0 = pl.multiple_of(cix * c, c)
            rows = pl.ds(r0, c)
            valid = _row_valid(blk * tb + r0, c, lv)
            for h in range(HG_HEADS):
                sl = slice(h * HG_D, (h + 1) * HG_D)
                dhq, dhf, dhi, dhg, dlb, dnw, dst_new = _hgrn_chunk_bwd(
                    hq_ref[rows, sl], hf_ref[rows, sl], hi_ref[rows, sl], hg_ref[rows, sl], o_ref[rows, sl],
                    dob_ref[rows, sl], st_ref[h, cix], dst_s[h], lb_ref[:, sl], nwv, valid, mallv, malltv, masksv)
                dst_s[h] = dst_new
                dhq_ref[rows, sl] = dhq
                dhf_ref[rows, sl] = dhf
                dhi_ref[rows, sl] = dhi
                dhg_ref[rows, sl] = dhg
                dlb_ref[:, sl] += dlb
                dnw_ref[...] += dnw
            return carry

        lax.fori_loop(0, ncb, chunk, 0, unroll=HG_UNROLL)

        if ride is not None:
            @pl.when(i == nb - 1)
            def _():
                ride.finish(*ride_refs)

    zb = lambda off: pl.BlockSpec((tb, w), lambda i: (nb - 1 - i, off // w))
    full = pl.BlockSpec((tb, w), lambda i: (nb - 1 - i, 0))
    const = lambda shape: pl.BlockSpec(shape, lambda i: (0,) * len(shape))
    r_in, r_args, r_out, r_shape, r_scratch = _ride_call(ride)
    outs = pl.pallas_call(
        body, name=name, grid=(nb,),
        in_specs=[zb(zl["hq"]), zb(zl["hf"]), zb(zl["hi"]), zb(zl["hg"]), full, full,
                  pl.BlockSpec((HG_HEADS, ncb, HG_D, HG_D), lambda i: (0, nb - 1 - i, 0, 0)),
                  const((1, w)), const((1, HG_D)), const(mall.shape), const(mall_t.shape), const(masks.shape)] + r_in,
        out_specs=[full, full, full, full, const((1, w)), const((1, HG_D))] + r_out,
        out_shape=[jax.ShapeDtypeStruct((lp, w), F32)] * 4
                  + [jax.ShapeDtypeStruct((1, w), F32), jax.ShapeDtypeStruct((1, HG_D), F32)] + r_shape,
        scratch_shapes=[pltpu.VMEM((HG_HEADS, HG_D, HG_D), F32)] + r_scratch,
        compiler_params=_params("arbitrary"),
    )(z, z, z, z, o_pre, dob, states, lb.reshape(1, w), nw.reshape(1, HG_D), mall, mall_t, masks, *r_args)
    dhq, dhf, dhi, dhg, dlb, dnw = outs[:6]
    return dhq, dhf, dhi, dhg, dlb[0], dnw[0], list(outs[6:])


def _loss_head(h, w, tpad, *, lv, name):
    lp, d = h.shape
    tm = _pick(lp, ROW_TILE, 8)

    def body(h_ref, w_ref, t_ref, dh_ref, loss_ref, dw_ref):
        i = pl.program_id(0)
        r0 = i * tm + lax.broadcasted_iota(jnp.int32, (tm, 1), 0)
        valid = ((r0 >= ROW_X) & (r0 < lv)).astype(F32)
        xv, wv = h_ref[...], w_ref[...]
        r = lax.rsqrt(jnp.mean(xv * xv, axis=-1, keepdims=True) + EPS)
        e = (xv * r * wv - t_ref[...]) * valid
        dy = e * (1.0 / d)
        wdy = dy * wv
        dh_ref[...] = r * wdy - xv * (r * r * r) * jnp.mean(xv * wdy, axis=-1, keepdims=True)

        @pl.when(i == 0)
        def _():
            loss_ref[...] = jnp.zeros_like(loss_ref)
            dw_ref[...] = jnp.zeros_like(dw_ref)

        loss_ref[...] += 0.5 * jnp.sum(jnp.mean(e * e, axis=-1, keepdims=True), axis=0, keepdims=True)
        dw_ref[...] += jnp.sum(dy * xv * r, axis=0, keepdims=True)

    row = pl.BlockSpec((tm, d), lambda i: (i, 0))
    vec = pl.BlockSpec((1, d), lambda i: (0, 0))
    dh, loss, dw = pl.pallas_call(
        body, name=name, grid=(lp // tm,), in_specs=[row, vec, row],
        out_specs=[row, pl.BlockSpec((8, LANES), lambda i: (0, 0)), vec],
        out_shape=[jax.ShapeDtypeStruct((lp, d), F32), jax.ShapeDtypeStruct((8, LANES), F32),
                   jax.ShapeDtypeStruct((1, d), F32)],
        compiler_params=_params("arbitrary"),
    )(h, w.reshape(1, d), tpad)
    return dh, loss[0, 0], dw[0]


def _adamw(w, g, m, v, *, name):
    shape = w.shape
    cols = shape[-1]
    rows = int(np.prod(shape[:-1])) if len(shape) > 1 else 1
    tr = _pick(rows, 256, 8)
    c1 = 1.0 - ADAM_B1 ** ADAM_STEP
    c2 = 1.0 - ADAM_B2 ** ADAM_STEP

    def body(w_ref, g_ref, m_ref, v_ref, d_ref, nm_ref, nv_ref):
        gv = g_ref[...]
        nm = ADAM_B1 * m_ref[...] + (1.0 - ADAM_B1) * gv
        nv = ADAM_B2 * v_ref[...] + (1.0 - ADAM_B2) * (gv * gv)
        d_ref[...] = -ADAM_LR * ((nm / c1) / (jnp.sqrt(nv / c2) + ADAM_EPS) + ADAM_WD * w_ref[...])
        nm_ref[...] = nm
        nv_ref[...] = nv

    spec = pl.BlockSpec((tr, cols), lambda i: (i, 0))
    r2 = lambda a: a.reshape(rows, cols)
    outs = pl.pallas_call(
        body, name=name, grid=(rows // tr,), in_specs=[spec] * 4, out_specs=[spec] * 3,
        out_shape=[jax.ShapeDtypeStruct((rows, cols), F32)] * 3, compiler_params=_params("parallel"),
    )(r2(w), r2(g), r2(m), r2(v))
    return tuple(o.reshape(shape) for o in outs)


HBM_SPEC = pl.BlockSpec(memory_space=pl.ANY)


def _coords():
    return lax.axis_index("x"), lax.axis_index("y"), lax.axis_index("c")


def _other_chips(x, y):
    return [(1 - x, y), (x, 1 - y), (1 - x, 1 - y)]


def _remote(src, dst, ssem, rsem, dev):
    return pltpu.make_async_remote_copy(src_ref=src, dst_ref=dst, send_sem=ssem, recv_sem=rsem,
                                        device_id=dev, device_id_type=MESH)


def _row_halves(w, c):
    rows = w.shape[-2]
    rh = rows // 2
    align = 8 * 4 // w.dtype.itemsize
    assert rh * 2 == rows and rh % align == 0, w.shape
    return pl.ds(pl.multiple_of(c * rh, align), rh), pl.ds(pl.multiple_of((1 - c) * rh, align), rh)


def _gathered_shapes(ws):
    return [jax.ShapeDtypeStruct((4, *w.shape), w.dtype) for w in ws]


def _own_block(g4s, ws):
    xi, yi, _ = _coords()
    return [lax.dynamic_update_slice(g4, w[None], (2 * xi + yi, 0, 0)) for g4, w in zip(g4s, ws)]


def _gather_chips(ws, *, name):
    n = len(ws)

    def body(*refs):
        w_refs, out_refs, (send_sems, recv_sems) = refs[:n], refs[n:2 * n], refs[2 * n:]
        x, y, c = _coords()
        sib = (x, y, 1 - c)
        chips = _other_chips(x, y)
        sent = []
        for t in range(n):
            half, _ = _row_halves(ws[t], c)
            for j, (px, py) in enumerate(chips):
                cp = _remote(w_refs[t].at[half], out_refs[t].at[2 * x + y, half], send_sems.at[6 * t + j],
                             recv_sems.at[6 * t + j], (px, py, c))
                cp.start()
                sent.append(cp)
        for t in range(n):
            half, _ = _row_halves(ws[t], c)
            for j, (px, py) in enumerate(chips):
                blk = out_refs[t].at[2 * px + py, half]
                _remote(w_refs[t].at[half], blk, send_sems.at[6 * t + j], recv_sems.at[6 * t + j],
                        (px, py, c)).wait_recv()
                fw = _remote(blk, blk, send_sems.at[6 * t + 3 + j], recv_sems.at[6 * t + 3 + j], sib)
                fw.start()
                sent.append(fw)
        for t in range(n):
            _, ohalf = _row_halves(ws[t], c)
            for j, (px, py) in enumerate(chips):
                blk = out_refs[t].at[2 * px + py, ohalf]
                _remote(blk, blk, send_sems.at[6 * t + 3 + j], recv_sems.at[6 * t + 3 + j], sib).wait_recv()
        for cp in sent:
            cp.wait_send()

    g4s = pl.pallas_call(
        body, name=name, in_specs=[HBM_SPEC] * n, out_specs=[HBM_SPEC] * n, out_shape=_gathered_shapes(ws),
        scratch_shapes=[pltpu.SemaphoreType.DMA((6 * n,)), pltpu.SemaphoreType.DMA((6 * n,))],
    )(*ws)
    return _own_block(g4s, ws)


def _swap_halves(gp, *, name):
    n, rows, cols = gp.shape
    rh = rows // 2

    def body(g_ref, out_ref, send_sems, recv_sems):
        x, y, c = _coords()
        sib = (x, y, 1 - c)
        ohalf = pl.ds(pl.multiple_of((1 - c) * rh, 8 * 4 // gp.dtype.itemsize), rh)
        cps = [_remote(g_ref.at[s, ohalf], out_ref.at[s], send_sems.at[s], recv_sems.at[s], sib) for s in range(n)]
        for cp in cps:
            cp.start()
        for cp in cps:
            cp.wait_recv()
        for cp in cps:
            cp.wait_send()

    return pl.pallas_call(
        body, name=name, in_specs=[HBM_SPEC], out_specs=HBM_SPEC,
        out_shape=jax.ShapeDtypeStruct((n, rh, cols), gp.dtype),
        scratch_shapes=[pltpu.SemaphoreType.DMA((n,)), pltpu.SemaphoreType.DMA((n,))],
    )(gp)


def _add_half(gp, got, cidx, *, name):
    n, rows, cols = gp.shape
    rh = rows // 2
    tr = _pick(rh, 512, 16)
    nrb = rh // tr

    def body(c_ref, a_ref, b_ref, o_ref):
        o_ref[...] = (a_ref[...].astype(F32) + b_ref[...].astype(F32)).astype(BF16)

    grid_spec = pltpu.PrefetchScalarGridSpec(
        num_scalar_prefetch=1, grid=(n, nrb),
        in_specs=[pl.BlockSpec((1, tr, cols), lambda s, i, c_ref: (s, c_ref[0] * nrb + i, 0)),
                  pl.BlockSpec((1, tr, cols), lambda s, i, c_ref: (s, i, 0))],
        out_specs=pl.BlockSpec((1, tr, cols), lambda s, i, c_ref: (s, i, 0)))
    return pl.pallas_call(
        body, name=name, grid_spec=grid_spec, out_shape=jax.ShapeDtypeStruct((n, rh, cols), BF16),
        compiler_params=_params("parallel", "parallel"),
    )(cidx, gp, got)


def _scatter_chips(p, *, name):
    _, rh, cols = p.shape

    def body(p_ref, out_ref, send_sems, recv_sems):
        x, y, c = _coords()
        cps = []
        for j, (px, py) in enumerate(_other_chips(x, y)):
            cps.append(_remote(p_ref.at[2 * px + py], out_ref.at[j], send_sems.at[j], recv_sems.at[j], (px, py, c)))
        for cp in cps:
            cp.start()
        for cp in cps:
            cp.wait_recv()
        for cp in cps:
            cp.wait_send()

    return pl.pallas_call(
        body, name=name, in_specs=[HBM_SPEC], out_specs=HBM_SPEC,
        out_shape=jax.ShapeDtypeStruct((3, rh, cols), p.dtype),
        scratch_shapes=[pltpu.SemaphoreType.DMA((3,)), pltpu.SemaphoreType.DMA((3,))],
    )(p)


def _sum_arrivals(p, land, kidx, *, name):
    _, rh, cols = p.shape
    tr = _pick(rh, 512, 16)

    def body(k_ref, a_ref, l_ref, o_ref):
        f = lambda v: v.astype(F32)
        o_ref[...] = ((f(a_ref[0]) + f(l_ref[0])) + f(l_ref[1])) + f(l_ref[2])

    grid_spec = pltpu.PrefetchScalarGridSpec(
        num_scalar_prefetch=1, grid=(rh // tr,),
        in_specs=[pl.BlockSpec((1, tr, cols), lambda i, k_ref: (k_ref[0], i, 0)),
                  pl.BlockSpec((3, tr, cols), lambda i, k_ref: (0, i, 0))],
        out_specs=pl.BlockSpec((tr, cols), lambda i, k_ref: (i, 0)))
    return pl.pallas_call(
        body, name=name, grid_spec=grid_spec, out_shape=jax.ShapeDtypeStruct((rh, cols), F32),
        compiler_params=_params("parallel"),
    )(kidx, p, land)


def _join_halves(q, *, name):
    rh, cols = q.shape

    def body(q_ref, out_ref, send_sem, recv_sem):
        x, y, c = _coords()
        half = pl.ds(pl.multiple_of(c * rh, 8), rh)
        ohalf = pl.ds(pl.multiple_of((1 - c) * rh, 8), rh)
        cp = _remote(q_ref, out_ref.at[half], send_sem, recv_sem, (x, y, 1 - c))
        cp.start()
        _remote(q_ref, out_ref.at[ohalf], send_sem, recv_sem, (x, y, 1 - c)).wait_recv()
        cp.wait_send()

    full = pl.pallas_call(
        body, name=name, in_specs=[HBM_SPEC], out_specs=HBM_SPEC,
        out_shape=jax.ShapeDtypeStruct((2 * rh, cols), q.dtype),
        scratch_shapes=[pltpu.SemaphoreType.DMA, pltpu.SemaphoreType.DMA],
    )(q)
    return lax.dynamic_update_slice(full, q, (lax.axis_index("c") * rh, 0))


def _rs_begin(gp, cidx, *, tag):
    got = _swap_halves(gp, name=f"rs_swap_{tag}")
    return _add_half(gp, got, cidx, name=f"rs_add_{tag}")


def _rs_end(p, land, kidx, *, tag):
    q = _sum_arrivals(p, land, kidx, name=f"rs_sum_{tag}")
    return _join_halves(q, name=f"rs_join_{tag}")


class _ScatterRide:
    def __init__(self, p):
        _, rh, cols = p.shape
        self.args = [p]
        self.out_shape = [jax.ShapeDtypeStruct((3, rh, cols), p.dtype)]
        self.scratch = [pltpu.SemaphoreType.DMA((3,)), pltpu.SemaphoreType.DMA((3,))]

    def _copies(self, p_ref, out_ref, ssem, rsem):
        x, y, c = _coords()
        return [_remote(p_ref.at[2 * px + py], out_ref.at[j], ssem.at[j], rsem.at[j], (px, py, c))
                for j, (px, py) in enumerate(_other_chips(x, y))]

    def start(self, *refs):
        for cp in self._copies(*refs):
            cp.start()

    def finish(self, *refs):
        cps = self._copies(*refs)
        for cp in cps:
            cp.wait_recv()
        for cp in cps:
            cp.wait_send()


class _GatherRide:
    def __init__(self, ws):
        n = len(ws)
        self.args = list(ws)
        self.out_shape = _gathered_shapes(ws)
        self.scratch = [pltpu.SemaphoreType.DMA((3 * n,)), pltpu.SemaphoreType.DMA((3 * n,))]

    def _copies(self, *refs):
        n = len(self.args)
        w_refs, out_refs, (ssem, rsem) = refs[:n], refs[n:2 * n], refs[2 * n:]
        x, y, c = _coords()
        send, recv = [], []
        for t in range(n):
            half, _ = _row_halves(self.args[t], c)
            for j, (px, py) in enumerate(_other_chips(x, y)):
                sems = (ssem.at[3 * t + j], rsem.at[3 * t + j], (px, py, c))
                send.append(_remote(w_refs[t].at[half], out_refs[t].at[2 * x + y, half], *sems))
                recv.append(_remote(w_refs[t].at[half], out_refs[t].at[2 * px + py, half], *sems))
        return send, recv

    def start(self, *refs):
        for cp in self._copies(*refs)[0]:
            cp.start()

    def finish(self, *refs):
        send, recv = self._copies(*refs)
        for cp in recv:
            cp.wait_recv()
        for cp in send:
            cp.wait_send()


def _gather_forward(g4s, *, name):
    n = len(g4s)

    def body(*refs):
        out_refs, (send_sems, recv_sems) = refs[n:2 * n], refs[2 * n:]
        x, y, c = _coords()
        sib = (x, y, 1 - c)
        chips = _other_chips(x, y)
        sent = []
        for t in range(n):
            half, _ = _row_halves(g4s[t], c)
            for j, (px, py) in enumerate(chips):
                blk = out_refs[t].at[2 * px + py, half]
                cp = _remote(blk, blk, send_sems.at[3 * t + j], recv_sems.at[3 * t + j], sib)
                cp.start()
                sent.append(cp)
        for t in range(n):
            _, ohalf = _row_halves(g4s[t], c)
            for j, (px, py) in enumerate(chips):
                blk = out_refs[t].at[2 * px + py, ohalf]
                _remote(blk, blk, send_sems.at[3 * t + j], recv_sems.at[3 * t + j], sib).wait_recv()
        for cp in sent:
            cp.wait_send()

    return pl.pallas_call(
        body, name=name, in_specs=[HBM_SPEC] * n, out_specs=[HBM_SPEC] * n,
        out_shape=[jax.ShapeDtypeStruct(g.shape, g.dtype) for g in g4s],
        input_output_aliases={t: t for t in range(n)},
        scratch_shapes=[pltpu.SemaphoreType.DMA((3 * n,)), pltpu.SemaphoreType.DMA((3 * n,))],
    )(*g4s)


def _allreduce_small(s, *, name):
    rows, cols = s.shape

    def body(s_ref, o_ref, buf, send_sems, recv_sems):
        x, y, c = _coords()
        me = 4 * x + 2 * y + c
        buf[me] = s_ref[...]
        cps = []
        for r in range(1, 8):
            peer = tuple((1 - v) if (r >> sh) & 1 else v for v, sh in ((x, 2), (y, 1), (c, 0)))
            cps.append(_remote(s_ref, buf.at[me], send_sems.at[r - 1], recv_sems.at[r - 1], peer))
        for cp in cps:
            cp.start()
        for cp in cps:
            cp.wait_recv()
        for cp in cps:
            cp.wait_send()
        acc = buf[0]
        for d in range(1, 8):
            acc = acc + buf[d]
        o_ref[...] = acc

    vm = pl.BlockSpec(memory_space=pltpu.VMEM)
    return pl.pallas_call(
        body, name=name, in_specs=[vm], out_specs=vm, out_shape=jax.ShapeDtypeStruct((rows, cols), F32),
        scratch_shapes=[pltpu.VMEM((8, rows, cols), F32), pltpu.SemaphoreType.DMA((7,)),
                        pltpu.SemaphoreType.DMA((7,))],
    )(s)


PACKED = ("ffn1_w_gu", "ffn1_w_down", "w_in", "w_uq", "w_ukv", "w_proj_attn", "w_proj_rec", "w_out",
          "ffn2_w_gu", "ffn2_w_down")
ROW_SHARDED = ("ffn1_w_down", "w_out", "ffn2_w_down")


def _pack_plan(shard_shapes):
    plan, off = {}, 0
    for n in PACKED:
        r, c = shard_shapes[n]
        assert (r * c) % PACK_W == 0
        plan[n] = (off, r * c // PACK_W, (r, c))
        off += r * c // PACK_W
    total = -(-off // 32) * 32
    return plan, total


def _pack(tensors, plan, total, dtype):
    parts = [tensors[n].astype(dtype).reshape(-1, PACK_W) for n in PACKED]
    used = sum(p.shape[0] for p in parts)
    if total > used:
        parts.append(jnp.zeros((total - used, PACK_W), dtype))
    return jnp.concatenate(parts, axis=0)


def _full_weights(g4s):
    out = {}
    for n, g in zip(PACKED, g4s):
        _, r, c = g.shape
        out[n] = g.reshape(4 * r, c) if n in ROW_SHARDED else jnp.swapaxes(g, 0, 1).reshape(r, 4 * c)
    return out


def _pack_grads(grads, plan, total):
    blocks = []
    for s in range(4):
        t = {}
        for n in PACKED:
            _, _, (r, c) = plan[n]
            t[n] = grads[n][s * r:(s + 1) * r] if n in ROW_SHARDED else grads[n][:, s * c:(s + 1) * c]
        blocks.append(_pack(t, plan, total, BF16))
    return jnp.stack(blocks)


def _unpack_shard(p, plan):
    return {n: p[plan[n][0]:plan[n][0] + plan[n][1]].reshape(plan[n][2]) for n in PACKED}


def _swap_cols(w):
    hlf = w.shape[1] // 2
    return jnp.concatenate([-w[:, hlf:], w[:, :hlf]], axis=1)


def _unswap_cols(dw):
    hlf = dw.shape[1] // 2
    return jnp.concatenate([dw[:, hlf:], -dw[:, :hlf]], axis=1)


def _layer_weights(full, d):
    zl = _z_layout(d)
    f = full["ffn1_w_down"].shape[0]
    w_in = full["w_in"]
    o = 0
    cols = {}
    for nm, wd in (("cq", Q_LORA), ("ckv", KV_LORA), ("kpe", QK_ROPE), ("hq", 512), ("hf", 512), ("hi", 512),
                   ("hg", 512), ("ga", d), ("gb", d)):
        cols[nm] = w_in[:, o:o + wd]
        o += wd
    zc = lambda n: jnp.zeros((d, n), BF16)
    win_p = jnp.concatenate(
        [cols["cq"], zc(QK_NOPE), cols["kpe"], zc(32), cols["ckv"], zc(QK_NOPE), _swap_cols(cols["kpe"]), zc(32),
         zc(LANES), cols["ga"], cols["gb"], cols["hq"], cols["hf"], cols["hi"], cols["hg"]], axis=1)
    assert win_p.shape[1] == zl["total"]
    wq = full["w_uq"].reshape(Q_LORA, MLA_HEADS, QK_NOPE + QK_ROPE)
    nope, rope = wq[:, :, :QK_NOPE], wq[:, :, QK_NOPE:]
    z32 = jnp.zeros((Q_LORA, MLA_HEADS, 32), BF16)
    z64 = jnp.zeros((Q_LORA, MLA_HEADS, QK_NOPE), BF16)
    rope_sw = jnp.concatenate([-rope[:, :, 16:], rope[:, :, :16]], axis=2)
    wqa = jnp.concatenate([nope, rope, z32], axis=2).reshape(Q_LORA, QW)
    wqb = jnp.concatenate([z64, rope_sw, z32], axis=2).reshape(Q_LORA, QW)
    wpa = full["w_proj_attn"].reshape(MLA_HEADS, V_HEAD, d)
    wpa_p = jnp.concatenate([jnp.zeros_like(wpa), wpa], axis=1).reshape(QW, d)
    return dict(
        wg1=full["ffn1_w_gu"][:, :f], wu1=full["ffn1_w_gu"][:, f:], wd1=full["ffn1_w_down"],
        wg2=full["ffn2_w_gu"][:, :f], wu2=full["ffn2_w_gu"][:, f:], wd2=full["ffn2_w_down"],
        win=win_p, wq2=jnp.concatenate([wqa, wqb], axis=1), wqa=wqa, wqb=wqb, wkv=full["w_ukv"], wpa=wpa_p,
        wpr=full["w_proj_rec"], wout=full["w_out"])


def _natural_grads(g, d):
    zl = _z_layout(d)
    dwin = g["win"]
    kpe = dwin[:, Z_KPA + QK_NOPE:Z_KPA + QK_NOPE + QK_ROPE] + _unswap_cols(
        dwin[:, Z_KPB + QK_NOPE:Z_KPB + QK_NOPE + QK_ROPE])
    w_in = jnp.concatenate(
        [dwin[:, Z_Q:Z_Q + Q_LORA], dwin[:, Z_KV:Z_KV + KV_LORA], kpe, dwin[:, zl["hq"]:zl["hq"] + 2048],
         dwin[:, zl["ga"]:zl["ga"] + 2 * d]], axis=1)
    qa = g["wqa"].reshape(Q_LORA, MLA_HEADS, HEAD_W)
    qb = g["wqb"].reshape(Q_LORA, MLA_HEADS, HEAD_W)[:, :, QK_NOPE:QK_NOPE + QK_ROPE]
    rope = qa[:, :, QK_NOPE:QK_NOPE + QK_ROPE] + jnp.concatenate([qb[:, :, 16:], -qb[:, :, :16]], axis=2)
    w_uq = jnp.concatenate([qa[:, :, :QK_NOPE], rope], axis=2).reshape(Q_LORA, -1)
    wpa = g["wpa"].reshape(MLA_HEADS, 2 * V_HEAD, d)[:, V_HEAD:].reshape(MLA_HEADS * V_HEAD, d)
    return dict(
        ffn1_w_gu=jnp.concatenate([g["wg1"], g["wu1"]], axis=1), ffn1_w_down=g["wd1"],
        ffn2_w_gu=jnp.concatenate([g["wg2"], g["wu2"]], axis=1), ffn2_w_down=g["wd2"],
        w_in=w_in, w_uq=w_uq, w_ukv=g["wkv"], w_proj_attn=wpa, w_proj_rec=g["wpr"], w_out=g["wout"])


def _rope_tables(lp):
    pos = jnp.maximum(jnp.arange(lp) - FRONT, 0).astype(F32)
    half = QK_ROPE // 2
    inv = ROPE_THETA ** (-jnp.arange(half, dtype=F32) / half)
    ang = pos[:, None] * inv[None, :]
    cos, sin = jnp.cos(ang), jnp.sin(ang)
    cos_t = jnp.concatenate([jnp.ones((lp, QK_NOPE), F32), cos, cos, jnp.zeros((lp, 32), F32)], axis=1)
    sin_t = jnp.concatenate([jnp.zeros((lp, QK_NOPE), F32), sin, sin, jnp.zeros((lp, 32), F32)], axis=1)
    return cos_t, sin_t


def _lower_bounds(raw):
    p = jax.nn.softmax(raw.astype(F32), axis=0)
    return jnp.cumsum(p, axis=0) - p[0:1]


def _ffn_fwd(h, nw, wg, wu, wd, tag):
    a, a_t = _rmsnorm_fwd(h, nw, width=h.shape[1], col_block=0, transposed=True, name=f"norm_{tag}")
    g = _matmul([(a, wg)], "nn", out_dtype=BF16, name=f"gate_{tag}")
    u = _matmul([(a, wu)], "nn", out_dtype=BF16, name=f"up_{tag}")
    act, act_t = _swiglu_fwd(g, u, name=f"swiglu_{tag}")
    out = _matmul([(act, wd)], "nn", res=h, scale=0.5, name=f"down_{tag}")
    return out, dict(h=h, a_t=a_t, g=g, u=u, act_t=act_t)


def _ffn_bwd(dout, sv, nw, wg, wu, wd, lv, tag):
    dact = _matmul([(dout, wd)], "nt", scale=0.5, out_dtype=BF16, name=f"ddown_{tag}")
    dwd = _matmul([(sv["act_t"], dout)], "nn", scale=0.5, name=f"dwdown_{tag}")
    dg, du = _swiglu_bwd(dact, sv["g"], sv["u"], name=f"dswiglu_{tag}")
    dwg = _matmul([(sv["a_t"], dg)], "nn", name=f"dwgate_{tag}")
    dwu = _matmul([(sv["a_t"], du)], "nn", name=f"dwup_{tag}")
    da = _matmul([(dg, wg), (du, wu)], "nt", name=f"dnormed_{tag}")
    dh, dn = _rmsnorm_bwd(sv["h"], nw, da, width=da.shape[1], col_block=0, lv=lv, dres=dout, name=f"dnorm_{tag}")
    return dh, dn, dwg, dwu, dwd


def _layer_fwd(h0, lw, sm, lb, tabs, consts, lv, l, ride=None):
    d = h0.shape[1]
    zl = _z_layout(d)
    cos_t, sin_t = tabs[:2]
    h1, s1 = _ffn_fwd(h0, sm["ffn1_norm"], lw["wg1"], lw["wu1"], lw["wd1"], f"ffn1_{l}")
    um, um_t = _rmsnorm_fwd(h1, sm["mix_norm"], width=d, col_block=0, transposed=True, name=f"norm_mix_{l}")
    z = _matmul([(um, lw["win"])], "nn", name=f"inproj_{l}")
    qn = _rmsnorm_fwd(z, sm["q_norm"], width=Q_LORA, col_block=Z_Q // Q_LORA, name=f"norm_q_{l}")
    kvn = _rmsnorm_fwd(z, sm["kv_norm"], width=KV_LORA, col_block=Z_KV // KV_LORA, name=f"norm_kv_{l}")
    q2 = _matmul([(qn, lw["wq2"])], "nn", name=f"uq_{l}")
    kv = _matmul([(kvn, lw["wkv"])], "nn", name=f"ukv_{l}")
    q, k, v = _qkv_prep_fwd(q2, kv, z, cos_t, sin_t, name=f"qkv_{l}")
    o, lse = _attn_fwd(q, k, v, tabs[2], lv=lv, name=f"attn_{l}")
    ya = _matmul([(o, lw["wpa"])], "nn", name=f"proj_attn_{l}")
    o_pre, ob, states, rode = _hgrn_fwd(z, lb, sm["hg_norm"], consts, zl=zl, lv=lv, name=f"hgrn_{l}", ride=ride)
    yb = _matmul([(ob, lw["wpr"])], "nn", name=f"proj_rec_{l}")
    mg, mg_t = _merge_fwd(ya, yb, z, zl=zl, name=f"merge_{l}")
    h2 = _matmul([(mg, lw["wout"])], "nn", res=h1, name=f"out_{l}")
    h3, s2 = _ffn_fwd(h2, sm["ffn2_norm"], lw["wg2"], lw["wu2"], lw["wd2"], f"ffn2_{l}")
    saved = dict(s1=s1, s2=s2, h1=h1, um_t=um_t, z=z, qn=qn, kvn=kvn, q=q, k=k, v=v, o=o, lse=lse, ya=ya, yb=yb,
                 o_pre=o_pre, ob=ob, states=states, mg_t=mg_t)
    return h3, saved, rode


def _layer_bwd(dh3, sv, lw, sm, lb, tabs, consts, lv, l, ride=None):
    d = dh3.shape[1]
    lp = dh3.shape[0]
    zl = _z_layout(d)
    cos_t, sin_t = tabs[:2]
    z = sv["z"]
    g = {}
    sg = {}
    dh2, sg["ffn2_norm"], g["wg2"], g["wu2"], g["wd2"] = _ffn_bwd(
        dh3, sv["s2"], sm["ffn2_norm"], lw["wg2"], lw["wu2"], lw["wd2"], lv, f"ffn2_{l}")
    dmg = _matmul([(dh2, lw["wout"])], "nt", name=f"dmerged_{l}")
    g["wout"] = _matmul([(sv["mg_t"], dh2)], "nn", name=f"dwout_{l}")
    dya, dyb, dga, dgb = _merge_bwd(dmg, sv["ya"], sv["yb"], z, zl=zl, name=f"dmerge_{l}")
    doa = _matmul([(dya, lw["wpa"])], "nt", name=f"dattn_out_{l}")
    g["wpa"] = _matmul([(sv["o"], dya)], "tn", name=f"dwproj_attn_{l}")
    dob = _matmul([(dyb, lw["wpr"])], "nt", name=f"drec_out_{l}")
    g["wpr"] = _matmul([(sv["ob"], dyb)], "tn", name=f"dwproj_rec_{l}")
    dhq, dhf, dhi, dhg, dlb, sg["hg_norm"], rode = _hgrn_bwd(
        z, sv["o_pre"], dob, sv["states"], lb, sm["hg_norm"], consts, zl=zl, lv=lv, name=f"dhgrn_{l}", ride=ride)
    delta = _attn_delta(doa, sv["o"], name=f"attn_delta_{l}")
    dq, dk, dv = _attn_bwd(sv["q"], sv["k"], sv["v"], doa, sv["lse"], delta, tabs[2], name=f"dattn_{l}")
    dqa, dqb, dkv, dza, dzb = _qkv_prep_bwd(dq, dk, dv, cos_t, sin_t, name=f"dqkv_{l}")
    dqn = _matmul([(dqa, lw["wqa"]), (dqb, lw["wqb"])], "nt", name=f"dqn_{l}")
    g["wqa"] = _matmul([(sv["qn"], dqa)], "tn", name=f"dwqa_{l}")
    g["wqb"] = _matmul([(sv["qn"], dqb)], "tn", name=f"dwqb_{l}")
    dkvn = _matmul([(dkv, lw["wkv"])], "nt", name=f"dkvn_{l}")
    g["wkv"] = _matmul([(sv["kvn"], dkv)], "tn", name=f"dwkv_{l}")
    dzq, sg["q_norm"] = _rmsnorm_bwd(z, sm["q_norm"], dqn, width=Q_LORA, col_block=Z_Q // Q_LORA, lv=lv,
                                     name=f"dnorm_q_{l}")
    dzkv, sg["kv_norm"] = _rmsnorm_bwd(z, sm["kv_norm"], dkvn, width=KV_LORA, col_block=Z_KV // KV_LORA, lv=lv,
                                       name=f"dnorm_kv_{l}")
    dz = jnp.concatenate([dzq, dza, dzkv, dzb, jnp.zeros((lp, LANES), F32), dga, dgb, dhq, dhf, dhi, dhg],
                         axis=1).astype(BF16)
    dum = _matmul([(dz, lw["win"])], "nt", name=f"dmixed_{l}")
    g["win"] = _matmul([(sv["um_t"], dz)], "nn", name=f"dwin_{l}")
    dh1, sg["mix_norm"] = _rmsnorm_bwd(sv["h1"], sm["mix_norm"], dum, width=d, col_block=0, lv=lv, dres=dh2,
                                       name=f"dnorm_mix_{l}")
    dh0, sg["ffn1_norm"], g["wg1"], g["wu1"], g["wd1"] = _ffn_bwd(
        dh1, sv["s1"], sm["ffn1_norm"], lw["wg1"], lw["wu1"], lw["wd1"], lv, f"ffn1_{l}")
    return dh0, g, sg, dlb, rode


WEIGHTS = ("meta_tokens", "ffn1_norm", "ffn1_w_gu", "ffn1_w_down", "mix_norm", "w_in", "q_norm", "kv_norm", "w_uq",
           "w_ukv", "hg_lb_raw", "hg_norm", "w_proj_attn", "w_proj_rec", "w_out", "ffn2_norm", "ffn2_w_gu",
           "ffn2_w_down", "final_norm")
SMALL = ("ffn1_norm", "mix_norm", "q_norm", "kv_norm", "hg_lb_raw", "hg_norm", "ffn2_norm")


def _small_rows(vals):
    pad = lambda a: jnp.pad(a, ((0, -a.shape[0] % 8), (0, PACK_W - a.shape[1])))
    rows = [pad(vals[n]) for n in SMALL]
    rows.append(pad(vals["final_norm"][None, :]))
    rows.append(pad(vals["meta_tokens"]))
    rows.append(pad(vals["loss"].reshape(1, 1)))
    return jnp.concatenate(rows, axis=0)


def _small_unrows(s, d, widths):
    out, o = {}, 0
    for n in SMALL:
        out[n] = s[o:o + DEPTH, :widths[n]]
        o += -(-DEPTH // 8) * 8
    out["final_norm"] = s[o, :d]
    o += 8
    out["meta_tokens"] = s[o:o + N_META, :d]
    o += -(-N_META // 8) * 8
    out["loss"] = s[o, 0]
    return out


def _step(args):
    x = args["x"][0]
    seq, d = x.shape
    assert d <= PACK_W
    lv = ROW_X + seq
    lp = -(-lv // ROW_TILE) * ROW_TILE
    xi, yi, ci = _coords()
    kidx = (2 * xi + yi).astype(jnp.int32).reshape(1)
    cidx = ci.astype(jnp.int32).reshape(1)
    consts = _hgrn_consts()
    tabs = (*_rope_tables(lp), _attn_consts(lp))

    shard_shapes = {n: args[n].shape[1:] for n in PACKED}
    plan, total = _pack_plan(shard_shapes)
    shards = [[args[n][l].astype(BF16) for n in PACKED] for l in range(DEPTH)]
    mt = args["meta_tokens"]
    mt4 = _gather_chips([mt], name="gather_meta")[0]
    meta = jnp.concatenate(list(mt4), axis=1)

    sm = [{n: args[n][l] for n in SMALL} for l in range(DEPTH)]
    lbs = _lower_bounds(args["hg_lb_raw"])

    h = jnp.concatenate([jnp.zeros((FRONT, d), F32), meta, x, jnp.zeros((lp - lv, d), F32)], axis=0)
    saved, lws = [], []
    g4s = _gather_chips(shards[0], name="gather_0")
    for l in range(DEPTH):
        lws.append(_layer_weights(_full_weights(g4s), d))
        ride = _GatherRide(shards[l + 1]) if l + 1 < DEPTH else None
        h, sv, rode = _layer_fwd(h, lws[l], sm[l], lbs[l], tabs, consts, lv, l, ride)
        saved.append(sv)
        if ride is not None:
            g4s = _own_block(_gather_forward(rode, name=f"gather_fwd_{l + 1}"), shards[l + 1])
    tpad = jnp.pad(args["loss_target"][0], ((ROW_X, lp - lv), (0, 0)))
    dh, loss, dfinal = _loss_head(h, args["final_norm"], tpad, lv=lv, name="loss_head")

    small = {n: [None] * DEPTH for n in SMALL}
    dlbs = [None] * DEPTH
    shard_grads = [None] * DEPTH
    waiting = None
    for l in reversed(range(DEPTH)):
        ride = _ScatterRide(waiting) if waiting is not None else None
        dh, g, sg, dlbs[l], rode = _layer_bwd(dh, saved[l], lws[l], sm[l], lbs[l], tabs, consts, lv, l, ride)
        if ride is not None:
            shard_grads[l + 1] = _unpack_shard(_rs_end(waiting, rode[0], kidx, tag=str(l + 1)), plan)
        for n in sg:
            small[n][l] = sg[n]
        waiting = _rs_begin(_pack_grads(_natural_grads(g, d), plan, total), cidx, tag=str(l))
    land = _scatter_chips(waiting, name="rs_scatter_0")
    shard_grads[0] = _unpack_shard(_rs_end(waiting, land, kidx, tag="0"), plan)

    _, lb_vjp = jax.vjp(_lower_bounds, args["hg_lb_raw"])
    small_vals = {n: jnp.stack(small[n]) for n in SMALL if n != "hg_lb_raw"}
    small_vals["hg_lb_raw"] = lb_vjp(jnp.stack(dlbs))[0]
    small_vals["final_norm"] = dfinal
    small_vals["meta_tokens"] = dh[FRONT:ROW_X]
    small_vals["loss"] = loss
    widths = {n: args[n].shape[1] for n in SMALL}
    tot = _small_unrows(_allreduce_small(_small_rows(small_vals), name="allreduce_small"), d, widths)

    grads = {n: jnp.stack([shard_grads[l][n] for l in range(DEPTH)]) for n in PACKED}
    for n in SMALL:
        grads[n] = tot[n]
    grads["final_norm"] = tot["final_norm"]
    mcols = mt.shape[1]
    grads["meta_tokens"] = lax.dynamic_slice_in_dim(tot["meta_tokens"], (2 * xi + yi) * mcols, mcols, axis=1)
    grad_x = dh[ROW_X:lv][None]

    delta, new_m, new_v = {}, {}, {}
    for n in WEIGHTS:
        delta[n], new_m[n], new_v[n] = _adamw(args[n], grads[n], args["m_" + n], args["v_" + n], name=f"adamw_{n}")
    return (tot["loss"], grad_x, *[grads[n] for n in WEIGHTS], *[delta[n] for n in WEIGHTS],
            *[new_m[n] for n in WEIGHTS], *[new_v[n] for n in WEIGHTS])


def kernel(x, meta_tokens, ffn1_norm, ffn1_w_gu, ffn1_w_down, mix_norm, w_in, q_norm, kv_norm, w_uq, w_ukv, hg_lb_raw, hg_norm, w_proj_attn, w_proj_rec, w_out, ffn2_norm, ffn2_w_gu, ffn2_w_down, final_norm, loss_target, m_meta_tokens, m_ffn1_norm, m_ffn1_w_gu, m_ffn1_w_down, m_mix_norm, m_w_in, m_q_norm, m_kv_norm, m_w_uq, m_w_ukv, m_hg_lb_raw, m_hg_norm, m_w_proj_attn, m_w_proj_rec, m_w_out, m_ffn2_norm, m_ffn2_w_gu, m_ffn2_w_down, m_final_norm, v_meta_tokens, v_ffn1_norm, v_ffn1_w_gu, v_ffn1_w_down, v_mix_norm, v_w_in, v_q_norm, v_kv_norm, v_w_uq, v_w_ukv, v_hg_lb_raw, v_hg_norm, v_w_proj_attn, v_w_proj_rec, v_w_out, v_ffn2_norm, v_ffn2_w_gu, v_ffn2_w_down, v_final_norm):
    return _step(dict(locals()))
```

```python
import functools
import math

import numpy as np
import jax
import jax.numpy as jnp
from jax import lax
from jax.experimental import pallas as pl
from jax.experimental.pallas import tpu as pltpu

F32 = jnp.float32
BF16 = jnp.bfloat16

N_META = 16
MLA_HEADS = 8
Q_LORA = 384
KV_LORA = 256
QK_NOPE = 64
QK_ROPE = 32
V_HEAD = 64
ROPE_THETA = 10000.0
HG_HEADS = 4
HG_D = 128
HG_CHUNK = 64
EPS = 1e-6
NEG_BIG = -1e30
F_MIN = 1e-20
DEPTH = 4

ADAM_LR = 0.001
ADAM_B1 = 0.9
ADAM_B2 = 0.999
ADAM_EPS = 1e-08
ADAM_WD = 0.01
ADAM_STEP = 10

LANES = 128
FRONT = (-N_META) % HG_CHUNK
ROW_X = FRONT + N_META
ROW_TILE = 640
HEAD_W = 128
QW = MLA_HEADS * HEAD_W
PREP_HEADS = 4
VMEM_LIMIT = 56 * 1024 * 1024
MATMUL_VMEM = 42 * 1024 * 1024
PACK_W = 1024
MESH = pl.DeviceIdType.MESH

Z_Q, Z_KPA, Z_KV, Z_KPB, Z_PAD, Z_GA = 0, 384, 512, 768, 896, 1024


def _z_layout(d):
    ga = Z_GA
    gb = ga + d
    hq = gb + d
    hf = hq + 512
    hi = hf + 512
    hg = hi + 512
    return dict(ga=ga, gb=gb, hq=hq, hf=hf, hi=hi, hg=hg, total=hg + 512)


def _pick(dim, cap, mult=LANES):
    if dim <= cap:
        return dim
    best = None
    for t in range(mult, cap + 1, mult):
        if dim % t == 0:
            best = t
    assert best is not None, (dim, cap, mult)
    return best


def _params(*sem):
    return pltpu.CompilerParams(dimension_semantics=sem, vmem_limit_bytes=VMEM_LIMIT)


def _sigmoid(x):
    return 1.0 / (1.0 + jnp.exp(-x))


def _row_valid(row0, n, lv):
    r = row0 + lax.broadcasted_iota(jnp.int32, (n, 1), 0)
    return ((r >= FRONT) & (r < lv)).astype(F32)


_DIMS = {"nn": (((1,), (0,)), ((), ())), "nt": (((1,), (1,)), ((), ())), "tn": (((0,), (0,)), ((), ()))}


def _matmul(pairs, mode, *, name, out_dtype=F32, res=None, scale=1.0):
    a0, b0 = pairs[0]
    if mode == "nn":
        (m, k), n = a0.shape, b0.shape[1]
    elif mode == "nt":
        (m, k), n = a0.shape, b0.shape[0]
    else:
        (k, m), n = a0.shape, b0.shape[1]
    if mode == "tn":
        tm, tn, tk = _pick(m, 1024), _pick(n, 1408), _pick(k, ROW_TILE, 8)
    else:
        if k > m:
            tm, tn, kcap = _pick(m, 1408, 16), _pick(n, 1408), 1664
        else:
            tm, tn, kcap = _pick(m, ROW_TILE, 8), _pick(n, 2816), 2816
        out_b = jnp.dtype(out_dtype).itemsize
        per_k = len(pairs) * 2 * (tm * a0.dtype.itemsize + tn * b0.dtype.itemsize)
        fixed = tm * tn * (2 * out_b + 4 + (8 if res is not None else 0))
        tk = _pick(k, kcap)
        while tk > LANES and fixed + per_k * tk > MATMUL_VMEM:
            tk = _pick(k, tk - LANES)
    nk = k // tk
    npair = len(pairs)
    dims = _DIMS[mode]

    def body(*refs):
        ins = refs[:2 * npair]
        pos = 2 * npair
        res_ref = None
        if res is not None:
            res_ref = refs[pos]
            pos += 1
        o_ref = refs[pos]
        kk = pl.program_id(2)

        part = None
        for p in range(npair):
            a = ins[2 * p][...].astype(BF16)
            b = ins[2 * p + 1][...].astype(BF16)
            d = lax.dot_general(a, b, dims, preferred_element_type=F32)
            part = d if part is None else part + d

        def finish(r):
            if scale != 1.0:
                r = r * scale
            if res_ref is not None:
                r = r + res_ref[...]
            o_ref[...] = r.astype(out_dtype)

        if nk == 1:
            finish(part)
            return
        acc = refs[pos + 1]

        @pl.when(kk == 0)
        def _():
            acc[...] = part

        @pl.when(kk > 0)
        def _():
            acc[...] += part

        @pl.when(kk == nk - 1)
        def _():
            finish(acc[...])

    if mode == "nn":
        a_spec = pl.BlockSpec((tm, tk), lambda i, j, q: (i, q))
        b_spec = pl.BlockSpec((tk, tn), lambda i, j, q: (q, j))
    elif mode == "nt":
        a_spec = pl.BlockSpec((tm, tk), lambda i, j, q: (i, q))
        b_spec = pl.BlockSpec((tn, tk), lambda i, j, q: (j, q))
    else:
        a_spec = pl.BlockSpec((tk, tm), lambda i, j, q: (q, i))
        b_spec = pl.BlockSpec((tk, tn), lambda i, j, q: (q, j))
    o_spec = pl.BlockSpec((tm, tn), lambda i, j, q: (i, j))
    in_specs, args = [], []
    for a, b in pairs:
        in_specs += [a_spec, b_spec]
        args += [a, b]
    if res is not None:
        in_specs.append(o_spec)
        args.append(res)
    return pl.pallas_call(
        body, name=name, grid=(m // tm, n // tn, nk), in_specs=in_specs, out_specs=o_spec,
        out_shape=jax.ShapeDtypeStruct((m, n), out_dtype),
        scratch_shapes=[pltpu.VMEM((tm, tn), F32)] if nk > 1 else [],
        compiler_params=_params("parallel", "parallel", "arbitrary"),
    )(*args)


def _rmsnorm_fwd(x, w, *, width, col_block, name, transposed=False):
    lp = x.shape[0]
    tm = _pick(lp, ROW_TILE, 8)

    def body(x_ref, w_ref, o_ref, *ot_ref):
        xv = x_ref[...]
        r = lax.rsqrt(jnp.mean(xv * xv, axis=-1, keepdims=True) + EPS)
        y = xv * r * w_ref[...]
        o_ref[...] = y.astype(BF16)
        if transposed:
            ot_ref[0][...] = y.T.astype(BF16)

    out_specs = [pl.BlockSpec((tm, width), lambda i: (i, 0))]
    out_shape = [jax.ShapeDtypeStruct((lp, width), BF16)]
    if transposed:
        out_specs.append(pl.BlockSpec((width, tm), lambda i: (0, i)))
        out_shape.append(jax.ShapeDtypeStruct((width, lp), BF16))
    outs = pl.pallas_call(
        body, name=name, grid=(lp // tm,),
        in_specs=[pl.BlockSpec((tm, width), lambda i: (i, col_block)), pl.BlockSpec((1, width), lambda i: (0, 0))],
        out_specs=out_specs, out_shape=out_shape, compiler_params=_params("parallel"),
    )(x, w.reshape(1, width))
    return tuple(outs) if transposed else outs[0]


def _rmsnorm_bwd(x, w, dy, *, width, col_block, lv, name, dres=None):
    lp = x.shape[0]
    tm = _pick(lp, ROW_TILE, 8)

    def body(*refs):
        if dres is None:
            x_ref, w_ref, dy_ref, dx_ref, dw_ref = refs
            dres_ref = None
        else:
            x_ref, w_ref, dy_ref, dres_ref, dx_ref, dw_ref = refs
        i = pl.program_id(0)
        xv = x_ref[...]
        dyv = dy_ref[...] * _row_valid(i * tm, tm, lv)
        r = lax.rsqrt(jnp.mean(xv * xv, axis=-1, keepdims=True) + EPS)
        wdy = dyv * w_ref[...]
        dx = r * wdy - xv * (r * r * r) * jnp.mean(xv * wdy, axis=-1, keepdims=True)
        if dres_ref is not None:
            dx = dx + dres_ref[...]
        dx_ref[...] = dx

        @pl.when(i == 0)
        def _():
            dw_ref[...] = jnp.zeros_like(dw_ref)

        dw_ref[...] += jnp.sum(dyv * xv * r, axis=0, keepdims=True)

    row = pl.BlockSpec((tm, width), lambda i: (i, 0))
    in_specs = [pl.BlockSpec((tm, width), lambda i: (i, col_block)), pl.BlockSpec((1, width), lambda i: (0, 0)), row]
    args = [x, w.reshape(1, width), dy]
    if dres is not None:
        in_specs.append(row)
        args.append(dres)
    dx, dw = pl.pallas_call(
        body, name=name, grid=(lp // tm,), in_specs=in_specs,
        out_specs=[row, pl.BlockSpec((1, width), lambda i: (0, 0))],
        out_shape=[jax.ShapeDtypeStruct((lp, width), F32), jax.ShapeDtypeStruct((1, width), F32)],
        compiler_params=_params("arbitrary"),
    )(*args)
    return dx, dw[0]


def _swiglu_fwd(g, u, *, name):
    lp, f = g.shape
    tm, tf = _pick(lp, ROW_TILE, 8), _pick(f, 1408)

    def body(g_ref, u_ref, o_ref, ot_ref):
        gv = g_ref[...].astype(F32)
        act = gv * _sigmoid(gv) * u_ref[...].astype(F32)
        o_ref[...] = act.astype(BF16)
        ot_ref[...] = act.T.astype(BF16)

    spec = pl.BlockSpec((tm, tf), lambda i, j: (i, j))
    return pl.pallas_call(
        body, name=name, grid=(lp // tm, f // tf), in_specs=[spec, spec],
        out_specs=[spec, pl.BlockSpec((tf, tm), lambda i, j: (j, i))],
        out_shape=[jax.ShapeDtypeStruct((lp, f), BF16), jax.ShapeDtypeStruct((f, lp), BF16)],
        compiler_params=_params("parallel", "parallel"),
    )(g, u)


def _swiglu_bwd(dact, g, u, *, name):
    lp, f = g.shape
    tm, tf = _pick(lp, ROW_TILE, 8), _pick(f, 1408)

    def body(d_ref, g_ref, u_ref, dg_ref, du_ref):
        gv, dv = g_ref[...].astype(F32), d_ref[...].astype(F32)
        s = _sigmoid(gv)
        dg_ref[...] = (dv * u_ref[...].astype(F32) * s * (1.0 + gv * (1.0 - s))).astype(BF16)
        du_ref[...] = (dv * gv * s).astype(BF16)

    spec = pl.BlockSpec((tm, tf), lambda i, j: (i, j))
    return pl.pallas_call(
        body, name=name, grid=(lp // tm, f // tf), in_specs=[spec, spec, spec], out_specs=[spec, spec],
        out_shape=[jax.ShapeDtypeStruct((lp, f), BF16)] * 2, compiler_params=_params("parallel", "parallel"),
    )(dact, g, u)


def _merge_fwd(ya, yb, z, *, zl, name):
    lp, d = ya.shape
    tm, td = _pick(lp, ROW_TILE, 8), _pick(d, 512)
    oa, ob = zl["ga"] // td, zl["gb"] // td

    def body(ya_ref, yb_ref, ga_ref, gb_ref, o_ref, ot_ref):
        mg = _sigmoid(ga_ref[...]) * ya_ref[...] + _sigmoid(gb_ref[...]) * yb_ref[...]
        o_ref[...] = mg.astype(BF16)
        ot_ref[...] = mg.T.astype(BF16)

    spec = pl.BlockSpec((tm, td), lambda i, j: (i, j))
    return pl.pallas_call(
        body, name=name, grid=(lp // tm, d // td),
        in_specs=[spec, spec, pl.BlockSpec((tm, td), lambda i, j: (i, oa + j)),
                  pl.BlockSpec((tm, td), lambda i, j: (i, ob + j))],
        out_specs=[spec, pl.BlockSpec((td, tm), lambda i, j: (j, i))],
        out_shape=[jax.ShapeDtypeStruct((lp, d), BF16), jax.ShapeDtypeStruct((d, lp), BF16)],
        compiler_params=_params("parallel", "parallel"),
    )(ya, yb, z, z)


def _merge_bwd(dmg, ya, yb, z, *, zl, name):
    lp, d = ya.shape
    tm, td = _pick(lp, ROW_TILE, 8), _pick(d, 512)
    oa, ob = zl["ga"] // td, zl["gb"] // td

    def body(d_ref, ya_ref, yb_ref, ga_ref, gb_ref, dya_ref, dyb_ref, dga_ref, dgb_ref):
        dv = d_ref[...]
        sa, sb = _sigmoid(ga_ref[...]), _sigmoid(gb_ref[...])
        dya_ref[...] = (dv * sa).astype(BF16)
        dyb_ref[...] = (dv * sb).astype(BF16)
        dga_ref[...] = dv * ya_ref[...] * sa * (1.0 - sa)
        dgb_ref[...] = dv * yb_ref[...] * sb * (1.0 - sb)

    spec = pl.BlockSpec((tm, td), lambda i, j: (i, j))
    return pl.pallas_call(
        body, name=name, grid=(lp // tm, d // td),
        in_specs=[spec, spec, spec, pl.BlockSpec((tm, td), lambda i, j: (i, oa + j)),
                  pl.BlockSpec((tm, td), lambda i, j: (i, ob + j))],
        out_specs=[spec] * 4,
        out_shape=[jax.ShapeDtypeStruct((lp, d), BF16)] * 2 + [jax.ShapeDtypeStruct((lp, d), F32)] * 2,
        compiler_params=_params("parallel", "parallel"),
    )(dmg, ya, yb, z, z)


def _qkv_prep_fwd(q2, kv, z, cos_t, sin_t, *, name):
    lp = q2.shape[0]
    tm = _pick(lp, ROW_TILE, 8)
    h = MLA_HEADS
    wd = PREP_HEADS * HEAD_W

    def body(qa_ref, qb_ref, kv_ref, za_ref, zb_ref, c_ref, s_ref, q_ref, k_ref, v_ref):
        c, s = c_ref[...], s_ref[...]
        lane = lax.broadcasted_iota(jnp.int32, (tm, HEAD_W), 1)
        kr = jnp.where(lane >= QK_NOPE, za_ref[...] * c + zb_ref[...] * s, 0.0)
        for g in range(PREP_HEADS):
            sl = slice(g * HEAD_W, (g + 1) * HEAD_W)
            q_ref[:, sl] = ((qa_ref[:, sl] * c + qb_ref[:, sl] * s) * Q_SCALE).astype(BF16)
            kvv = kv_ref[:, sl]
            k_ref[:, sl] = (jnp.where(lane < QK_NOPE, kvv, 0.0) + kr).astype(BF16)
            v_ref[:, sl] = jnp.where(lane >= QK_NOPE, kvv, jnp.where(lane == 0, 1.0, 0.0)).astype(BF16)

    blk = lambda w, f: pl.BlockSpec((tm, w), f)
    out = blk(wd, lambda i, j: (i, j))
    return pl.pallas_call(
        body, name=name, grid=(lp // tm, h // PREP_HEADS),
        in_specs=[blk(wd, lambda i, j: (i, j)), blk(wd, lambda i, j: (i, h // PREP_HEADS + j)),
                  blk(wd, lambda i, j: (i, j)),
                  blk(HEAD_W, lambda i, j: (i, Z_KPA // HEAD_W)), blk(HEAD_W, lambda i, j: (i, Z_KPB // HEAD_W)),
                  blk(HEAD_W, lambda i, j: (i, 0)), blk(HEAD_W, lambda i, j: (i, 0))],
        out_specs=[out, out, out], out_shape=[jax.ShapeDtypeStruct((lp, QW), BF16)] * 3,
        compiler_params=_params("parallel", "parallel"),
    )(q2, q2, kv, z, z, cos_t, sin_t)


def _qkv_prep_bwd(dq, dk, dv, cos_t, sin_t, *, name):
    lp = dq.shape[0]
    tm = _pick(lp, ROW_TILE, 8)
    h = MLA_HEADS
    wd = PREP_HEADS * HEAD_W

    def body(dq_ref, dk_ref, dv_ref, c_ref, s_ref, dqa_ref, dqb_ref, dkv_ref, dza_ref, dzb_ref):
        j = pl.program_id(1)
        c, s = c_ref[...], s_ref[...]
        lane = lax.broadcasted_iota(jnp.int32, (tm, HEAD_W), 1)
        dkr = jnp.zeros((tm, HEAD_W), F32)
        for g in range(PREP_HEADS):
            sl = slice(g * HEAD_W, (g + 1) * HEAD_W)
            dqv, dkv_ = dq_ref[:, sl], dk_ref[:, sl]
            dqa_ref[:, sl] = (dqv * c).astype(BF16)
            dqb_ref[:, sl] = (dqv * s).astype(BF16)
            dkv_ref[:, sl] = jnp.where(lane < QK_NOPE, dkv_, dv_ref[:, sl]).astype(BF16)
            dkr = dkr + jnp.where(lane >= QK_NOPE, dkv_, 0.0)

        @pl.when(j == 0)
        def _():
            dza_ref[...] = jnp.zeros_like(dza_ref)
            dzb_ref[...] = jnp.zeros_like(dzb_ref)

        dza_ref[...] += dkr * c
        dzb_ref[...] += dkr * s

    blk = lambda w, f: pl.BlockSpec((tm, w), f)
    per_head, shared = blk(wd, lambda i, j: (i, j)), blk(HEAD_W, lambda i, j: (i, 0))
    return pl.pallas_call(
        body, name=name, grid=(lp // tm, h // PREP_HEADS),
        in_specs=[per_head, per_head, per_head, shared, shared],
        out_specs=[per_head, per_head, per_head, shared, shared],
        out_shape=[jax.ShapeDtypeStruct((lp, QW), BF16)] * 3 + [jax.ShapeDtypeStruct((lp, HEAD_W), F32)] * 2,
        compiler_params=_params("parallel", "arbitrary"),
    )(dq, dk, dv, cos_t, sin_t)


def _attn_tile(lp):
    return _pick(lp, ROW_TILE, LANES)


Q_SCALE = (QK_NOPE + QK_ROPE) ** -0.5 * math.log2(math.e)
ATT_HP = 2
ATT_HP_FWD = 4


def _attn_consts(lp):
    t = _attn_tile(lp)
    nb = lp // t
    r = np.arange(t)
    causal = np.where(r[None, :] <= r[:, None], 0.0, NEG_BIG).astype(np.float32)
    front = np.where(r >= FRONT, 0.0, NEG_BIG).astype(np.float32)[None, :]
    diag = np.stack([np.minimum(causal, front), causal])
    qmaj = [(i, j) for i in range(nb) for j in range(i + 1)]
    kmaj = [(i, j) for j in range(nb) for i in range(j, nb)]
    tab = lambda pairs, c: jnp.asarray([p[c] for p in pairs], jnp.int32)
    return dict(diag=jnp.asarray(diag), front=jnp.asarray(front),
                fwd=(tab(qmaj, 0), tab(qmaj, 1)), bwd=(tab(kmaj, 0), tab(kmaj, 1)))


def _attn_fwd(q, k, v, ac, *, lv, name):
    lp = q.shape[0]
    t = _attn_tile(lp)
    nb = lp // t
    rep = t // HEAD_W
    qtab, ktab = ac["fwd"]

    def body(qt_ref, kt_ref, q_ref, k_ref, v_ref, bd_ref, bf_ref, o_ref, lse_ref, m_s, acc_s):
        step_id = pl.program_id(1)
        qb, kb = qt_ref[step_id], kt_ref[step_id]

        @pl.when(kb == 0)
        def _():
            m_s[...] = jnp.full_like(m_s, NEG_BIG)
            acc_s[...] = jnp.zeros_like(acc_s)

        def step(bias):
            b = None if bias is None else bias()
            for hh in range(ATT_HP_FWD):
                sl = slice(hh * HEAD_W, (hh + 1) * HEAD_W)
                s = lax.dot_general(q_ref[:, sl], k_ref[:, sl], _DIMS["nt"], preferred_element_type=F32)
                if b is not None:
                    s = s + b
                m_prev = m_s[:, sl]
                m_new = jnp.maximum(m_prev, jnp.max(s, axis=-1, keepdims=True))
                alpha = jnp.exp2(m_prev - m_new)
                p = jnp.exp2(s - jnp.tile(m_new, (1, rep)))
                acc_s[:, sl] = alpha * acc_s[:, sl] + jnp.dot(p.astype(BF16), v_ref[:, sl],
                                                              preferred_element_type=F32)
                m_s[:, sl] = m_new

        @pl.when((kb > 0) & (kb < qb))
        def _():
            step(None)

        @pl.when((kb == 0) & (qb > 0))
        def _():
            step(lambda: bf_ref[...])

        @pl.when(kb == qb)
        def _():
            step(lambda: bd_ref[0])
            valid = _row_valid(qb * t, t, lv)
            for hh in range(ATT_HP_FWD):
                sl = slice(hh * HEAD_W, (hh + 1) * HEAD_W)
                acc = acc_s[:, sl]
                l = acc[:, :1]
                o_ref[:, sl] = acc / l * valid
                lse_ref[:, sl] = m_s[:, sl] + jnp.log2(l)

    wd = ATT_HP_FWD * HEAD_W
    qs = pl.BlockSpec((t, wd), lambda h, s, qt, kt: (qt[s], h))
    ks = pl.BlockSpec((t, wd), lambda h, s, qt, kt: (kt[s], h))
    grid_spec = pltpu.PrefetchScalarGridSpec(
        num_scalar_prefetch=2, grid=(MLA_HEADS // ATT_HP_FWD, int(qtab.shape[0])),
        in_specs=[qs, ks, ks, pl.BlockSpec((1, t, t), lambda h, s, qt, kt: (jnp.minimum(qt[s], 1), 0, 0)),
                  pl.BlockSpec((1, t), lambda h, s, qt, kt: (0, 0))],
        out_specs=[qs, qs],
        scratch_shapes=[pltpu.VMEM((t, wd), F32), pltpu.VMEM((t, wd), F32)])
    return pl.pallas_call(
        body, name=name, grid_spec=grid_spec, out_shape=[jax.ShapeDtypeStruct((lp, QW), F32)] * 2,
        compiler_params=_params("parallel", "arbitrary"),
    )(qtab, ktab, q, k, v, ac["diag"], ac["front"])


def _attn_delta(do, o, *, name):
    lp = do.shape[0]
    tm = _pick(lp, ROW_TILE, 8)
    wd = PREP_HEADS * HEAD_W

    def body(do_ref, o_ref, d_ref):
        for g in range(PREP_HEADS):
            sl = slice(g * HEAD_W, (g + 1) * HEAD_W)
            d_ref[:, sl] = jnp.broadcast_to(jnp.sum(do_ref[:, sl] * o_ref[:, sl], axis=-1, keepdims=True),
                                            (tm, HEAD_W))

    spec = pl.BlockSpec((tm, wd), lambda i, j: (i, j))
    return pl.pallas_call(
        body, name=name, grid=(lp // tm, MLA_HEADS // PREP_HEADS), in_specs=[spec, spec], out_specs=spec,
        out_shape=jax.ShapeDtypeStruct((lp, QW), F32), compiler_params=_params("parallel", "parallel"),
    )(do, o)


def _attn_bwd(q, k, v, do, lse, delta, ac, *, name):
    lp = q.shape[0]
    t = _attn_tile(lp)
    nb = lp // t
    rep = t // HEAD_W
    scale = (QK_NOPE + QK_ROPE) ** -0.5
    qtab, ktab = ac["bwd"]

    def body(qt_ref, kt_ref, q_ref, k_ref, v_ref, do_ref, lse_ref, dl_ref, bd_ref, bf_ref, dq_ref, dk_ref, dv_ref,
             dk_s, dv_s):
        step_id = pl.program_id(1)
        qb, kb = qt_ref[step_id], kt_ref[step_id]

        @pl.when(qb == kb)
        def _():
            dk_s[...] = jnp.zeros_like(dk_s)
            dv_s[...] = jnp.zeros_like(dv_s)

        def step(bias):
            b = None if bias is None else bias()
            rows = pl.ds(pl.multiple_of(qb * t, t), t)
            contribs = []
            for hh in range(ATT_HP):
                sl = slice(hh * HEAD_W, (hh + 1) * HEAD_W)
                qv, kv_, vv = q_ref[:, sl], k_ref[:, sl], v_ref[:, sl]
                dov = do_ref[:, sl].astype(BF16)
                s = lax.dot_general(qv, kv_, _DIMS["nt"], preferred_element_type=F32)
                if b is not None:
                    s = s + b
                p = jnp.exp2(s - jnp.tile(lse_ref[:, sl], (1, rep)))
                dv_s[:, sl] += lax.dot_general(p.astype(BF16), dov, _DIMS["tn"], preferred_element_type=F32)
                dp = lax.dot_general(dov, vv, _DIMS["nt"], preferred_element_type=F32)
                ds = (p * (dp - jnp.tile(dl_ref[:, sl], (1, rep))) * scale).astype(BF16)
                dk_s[:, sl] += lax.dot_general(ds, qv, _DIMS["tn"], preferred_element_type=F32)
                contribs.append(jnp.dot(ds, kv_, preferred_element_type=F32))
            contrib = jnp.concatenate(contribs, axis=1)

            @pl.when(kb == 0)
            def _():
                dq_ref[rows, :] = contrib

            @pl.when(kb > 0)
            def _():
                dq_ref[rows, :] += contrib

        @pl.when((kb > 0) & (kb < qb))
        def _():
            step(None)

        @pl.when((kb == 0) & (qb > 0))
        def _():
            step(lambda: bf_ref[...])

        @pl.when(kb == qb)
        def _():
            step(lambda: bd_ref[0])

        @pl.when(qb == nb - 1)
        def _():
            dk_ref[...] = dk_s[...] * (1.0 / Q_SCALE)
            dv_ref[...] = dv_s[...]

    wd = ATT_HP * HEAD_W
    qs = pl.BlockSpec((t, wd), lambda h, s, qt, kt: (qt[s], h))
    ks = pl.BlockSpec((t, wd), lambda h, s, qt, kt: (kt[s], h))
    dqs = pl.BlockSpec((lp, wd), lambda h, s, qt, kt: (0, h))
    grid_spec = pltpu.PrefetchScalarGridSpec(
        num_scalar_prefetch=2, grid=(MLA_HEADS // ATT_HP, int(qtab.shape[0])),
        in_specs=[qs, ks, ks, qs, qs, qs,
                  pl.BlockSpec((1, t, t), lambda h, s, qt, kt: (jnp.minimum(qt[s], 1), 0, 0)),
                  pl.BlockSpec((1, t), lambda h, s, qt, kt: (0, 0))],
        out_specs=[dqs, ks, ks],
        scratch_shapes=[pltpu.VMEM((t, wd), F32), pltpu.VMEM((t, wd), F32)])
    return pl.pallas_call(
        body, name=name, grid_spec=grid_spec, out_shape=[jax.ShapeDtypeStruct((lp, QW), F32)] * 3,
        compiler_params=_params("arbitrary", "arbitrary"),
    )(qtab, ktab, q, k, v, do, lse, delta, ac["diag"], ac["front"])


HG_UNROLL = 2
HG_LEVELS = (64, 32, 16, 8, 4, 2)
N_LEV = len(HG_LEVELS)


def _hgrn_consts():
    c = HG_CHUNK
    m = np.zeros((N_LEV + 2, c, c), np.float32)
    masks = np.zeros((N_LEV, c, c), np.float32)
    for li, p in enumerate(HG_LEVELS):
        for t in range(c):
            mid = (t // p) * p + p // 2
            if t >= mid:
                m[li, t, mid:t + 1] = 1.0
            else:
                m[li, t, t + 1:mid] = 1.0
            for s in range(c):
                if s // p == t // p and t >= mid and s < mid:
                    masks[li, t, s] = 1.0
    for t in range(c):
        m[N_LEV, t, :t + 1] = 1.0
        m[N_LEV + 1, t, t + 1:] = 1.0
    mall = m.reshape((N_LEV + 2) * c, c)
    return jnp.asarray(mall, BF16), jnp.asarray(mall.T.copy(), BF16), jnp.asarray(masks, F32)


def _split_terms(x):
    hi = x.astype(BF16)
    lo = (x - hi.astype(F32)).astype(BF16)
    return jnp.concatenate([hi, lo], axis=1)


def _sum_terms(e3):
    return e3[:, :HG_D] + e3[:, HG_D:]


def _hgrn_chunk_fwd(hq, hf, hi, lb, valid, mall, masks, st):
    c = HG_CHUNK
    scale = HG_D ** -0.5
    sq = _sigmoid(hq)
    qv = hq * sq
    sg = _sigmoid(hf)
    f = lb + (1.0 - lb) * sg
    fc = jnp.maximum(f, F_MIN)
    lf = jnp.log(fc) * valid
    kv = (1.0 - lb) * (1.0 - sg) * valid
    e = _sum_terms(jnp.dot(mall, _split_terms(lf), preferred_element_type=F32))
    x = jnp.exp(e)
    a = jnp.zeros((c, c), F32)
    qe, ke = [], []
    for l in range(N_LEV):
        xl = x[l * c:(l + 1) * c]
        qe.append(qv * xl)
        ke.append(kv * xl)
        a = a + masks[l] * lax.dot_general(qe[l].astype(BF16), ke[l].astype(BF16), _DIMS["nt"],
                                           preferred_element_type=F32)
    row = lax.broadcasted_iota(jnp.int32, (c, c), 0)
    col = lax.broadcasted_iota(jnp.int32, (c, c), 1)
    a = a + jnp.where(row == col, jnp.sum(qv * kv, axis=-1, keepdims=True), 0.0)
    xb = x[N_LEV * c:(N_LEV + 1) * c]
    qb = qv * xb
    kb = kv * x[(N_LEV + 1) * c:]
    x_last = xb[c - 1:c]
    hib = hi.astype(BF16)
    o = scale * (jnp.dot(a.astype(BF16), hib, preferred_element_type=F32)
                 + lax.dot_general(qb.astype(BF16), st.astype(BF16), _DIMS["nt"], preferred_element_type=F32))
    st_new = st * x_last + lax.dot_general(hib, kb.astype(BF16), _DIMS["tn"], preferred_element_type=F32)
    saved = dict(sq=sq, qv=qv, sg=sg, f=f, fc=fc, kv=kv, x=x, a=a, qe=qe, ke=ke, qb=qb, kb=kb, x_last=x_last)
    return o, st_new, saved


def _split_ride(refs, n_in, n_out, n_scratch, ride):
    ri = len(ride.args) if ride else 0
    ro = len(ride.out_shape) if ride else 0
    a = n_in + ri
    b = a + n_out + ro
    c = b + n_scratch
    return refs[:n_in], refs[a:a + n_out], refs[b:c], refs[n_in:a] + refs[a + n_out:b] + refs[c:]


def _ride_call(ride):
    if ride is None:
        return [], [], [], [], []
    hbm = [HBM_SPEC] * len(ride.args)
    return hbm, list(ride.args), [HBM_SPEC] * len(ride.out_shape), list(ride.out_shape), list(ride.scratch)


def _hgrn_fwd(z, lb, nw, consts, *, zl, lv, name, ride=None):
    lp = z.shape[0]
    tb = _pick(lp, ROW_TILE, HG_CHUNK)
    ncb = tb // HG_CHUNK
    nb = lp // tb
    mall, _, masks = consts
    w = HG_HEADS * HG_D

    def body(*refs):
        ins, outs, (st_s,), ride_refs = _split_ride(refs, 8, 3, 1, ride)
        hq_ref, hf_ref, hi_ref, hg_ref, lb_ref, nw_ref, mall_ref, masks_ref = ins
        o_ref, ob_ref, st_ref = outs
        i = pl.program_id(0)

        @pl.when(i == 0)
        def _():
            st_s[...] = jnp.zeros_like(st_s)
            if ride is not None:
                ride.start(*ride_refs)

        nwv = nw_ref[...]
        mallv, masksv = mall_ref[...], masks_ref[...]

        def chunk(cix, carry):
            r0 = pl.multiple_of(cix * HG_CHUNK, HG_CHUNK)
            rows = pl.ds(r0, HG_CHUNK)
            valid = _row_valid(i * tb + r0, HG_CHUNK, lv)
            for h in range(HG_HEADS):
                sl = slice(h * HG_D, (h + 1) * HG_D)
                st = st_s[h]
                st_ref[h, cix] = st
                o, st_new, _ = _hgrn_chunk_fwd(hq_ref[rows, sl], hf_ref[rows, sl], hi_ref[rows, sl], lb_ref[:, sl],
                                               valid, mallv, masksv, st)
                st_s[h] = st_new
                o_ref[rows, sl] = o
                hg = hg_ref[rows, sl]
                r = lax.rsqrt(jnp.mean(o * o, axis=-1, keepdims=True) + EPS)
                ob_ref[rows, sl] = (o * r * nwv * (hg * _sigmoid(hg))).astype(BF16)
            return carry

        lax.fori_loop(0, ncb, chunk, 0, unroll=HG_UNROLL)

        if ride is not None:
            @pl.when(i == nb - 1)
            def _():
                ride.finish(*ride_refs)

    zb = lambda off: pl.BlockSpec((tb, w), lambda i: (i, off // w))
    full = pl.BlockSpec((tb, w), lambda i: (i, 0))
    const = lambda shape: pl.BlockSpec(shape, lambda i: (0,) * len(shape))
    r_in, r_args, r_out, r_shape, r_scratch = _ride_call(ride)
    outs = pl.pallas_call(
        body, name=name, grid=(nb,),
        in_specs=[zb(zl["hq"]), zb(zl["hf"]), zb(zl["hi"]), zb(zl["hg"]), const((1, w)), const((1, HG_D)),
                  const(mall.shape), const(masks.shape)] + r_in,
        out_specs=[full, full, pl.BlockSpec((HG_HEADS, ncb, HG_D, HG_D), lambda i: (0, i, 0, 0))] + r_out,
        out_shape=[jax.ShapeDtypeStruct((lp, w), F32), jax.ShapeDtypeStruct((lp, w), BF16),
                   jax.ShapeDtypeStruct((HG_HEADS, lp // HG_CHUNK, HG_D, HG_D), F32)] + r_shape,
        scratch_shapes=[pltpu.VMEM((HG_HEADS, HG_D, HG_D), F32)] + r_scratch,
        compiler_params=_params("arbitrary"),
    )(z, z, z, z, lb.reshape(1, w), nw.reshape(1, HG_D), mall, masks, *r_args)
    return outs[0], outs[1], outs[2], list(outs[3:])


def _hgrn_chunk_bwd(hq, hf, hi, hg, o, dout, st, dst, lbv, nwv, valid, mallv, malltv, masksv):
    c = HG_CHUNK
    scale = HG_D ** -0.5
    _, _, sv = _hgrn_chunk_fwd(hq, hf, hi, lbv, valid, mallv, masksv, st)
    shg = _sigmoid(hg)
    r = lax.rsqrt(jnp.mean(o * o, axis=-1, keepdims=True) + EPS)
    don = dout * (hg * shg)
    dhg = dout * (o * r * nwv) * shg * (1.0 + hg * (1.0 - shg))
    dnw = jnp.sum(don * o * r, axis=0, keepdims=True)
    wd = don * nwv
    do = r * wd - o * (r * r * r) * jnp.mean(o * wd, axis=-1, keepdims=True)
    dob16, hib = do.astype(BF16), hi.astype(BF16)
    dst16 = dst.astype(BF16)
    da = scale * lax.dot_general(dob16, hib, _DIMS["nt"], preferred_element_type=F32)
    dv = (scale * lax.dot_general(sv["a"].astype(BF16), dob16, _DIMS["tn"], preferred_element_type=F32)
          + lax.dot_general(sv["kb"].astype(BF16), dst16, _DIMS["nt"], preferred_element_type=F32))
    dkb = jnp.dot(hib, dst16, preferred_element_type=F32)
    dqb = scale * jnp.dot(dob16, st.astype(BF16), preferred_element_type=F32)
    dst_new = dst * sv["x_last"] + scale * lax.dot_general(dob16, sv["qb"].astype(BF16), _DIMS["tn"],
                                                           preferred_element_type=F32)
    dxl = jnp.sum(dst * st, axis=0, keepdims=True)
    x = sv["x"]
    dq = dqb * x[N_LEV * c:(N_LEV + 1) * c]
    dk = dkb * x[(N_LEV + 1) * c:]
    de = []
    for l in range(N_LEV):
        dam = (masksv[l] * da).astype(BF16)
        dqe = jnp.dot(dam, sv["ke"][l].astype(BF16), preferred_element_type=F32)
        dke = lax.dot_general(dam, sv["qe"][l].astype(BF16), _DIMS["tn"], preferred_element_type=F32)
        xl = x[l * c:(l + 1) * c]
        dq = dq + dqe * xl
        dk = dk + dke * xl
        de.append(dqe * sv["qe"][l] + dke * sv["ke"][l])
    dd = scale * jnp.sum(do * hi, axis=-1, keepdims=True)
    dq = dq + dd * sv["kv"]
    dk = dk + dd * sv["qv"]
    last = lax.broadcasted_iota(jnp.int32, (c, 1), 0) == c - 1
    de.append(dqb * sv["qb"] + jnp.where(last, dxl * sv["x_last"], 0.0))
    de.append(dkb * sv["kb"])
    dlf = _sum_terms(jnp.dot(malltv, _split_terms(jnp.concatenate(de, axis=0)), preferred_element_type=F32))
    sg, sq = sv["sg"], sv["sq"]
    df = jnp.where(sv["f"] > F_MIN, dlf * valid / sv["fc"], 0.0)
    dkm = dk * valid
    dhf = (df - dkm) * (1.0 - lbv) * sg * (1.0 - sg)
    dlb = jnp.sum((df - dkm) * (1.0 - sg), axis=0, keepdims=True)
    dhq = dq * sq * (1.0 + hq * (1.0 - sq))
    return dhq, dhf, dv, dhg, dlb, dnw, dst_new


def _hgrn_bwd(z, o_pre, dob, states, lb, nw, consts, *, zl, lv, name, ride=None):
    lp = z.shape[0]
    tb = _pick(lp, ROW_TILE, HG_CHUNK)
    ncb = tb // HG_CHUNK
    nb = lp // tb
    mall, mall_t, masks = consts
    w = HG_HEADS * HG_D
    c = HG_CHUNK

    def body(*refs):
        ins, outs, (dst_s,), ride_refs = _split_ride(refs, 12, 6, 1, ride)
        hq_ref, hf_ref, hi_ref, hg_ref, o_ref, dob_ref, st_ref, lb_ref, nw_ref, mall_ref, mallt_ref, masks_ref = ins
        dhq_ref, dhf_ref, dhi_ref, dhg_ref, dlb_ref, dnw_ref = outs
        i = pl.program_id(0)
        blk = nb - 1 - i

        @pl.when(i == 0)
        def _():
            dst_s[...] = jnp.zeros_like(dst_s)
            dlb_ref[...] = jnp.zeros_like(dlb_ref)
            dnw_ref[...] = jnp.zeros_like(dnw_ref)
            if ride is not None:
                ride.start(*ride_refs)

        nwv = nw_ref[...]
        mallv, malltv, masksv = mall_ref[...], mallt_ref[...], masks_ref[...]

        def chunk(jx, carry):
            cix = ncb - 1 - jx
            r0 = pl.multiple_of(cix * c, c)
            rows = pl.ds(r0, c)
            valid = _row_valid(blk * tb + r0, c, lv)
            for h in range(HG_HEADS):
                sl = slice(h * HG_D, (h + 1) * HG_D)
                dhq, dhf, dhi, dhg, dlb, dnw, dst_new = _hgrn_chunk_bwd(
                    hq_ref[rows, sl], hf_ref[rows, sl], hi_ref[rows, sl], hg_ref[rows, sl], o_ref[rows, sl],
                    dob_ref[rows, sl], st_ref[h, cix], dst_s[h], lb_ref[:, sl], nwv, valid, mallv, malltv, masksv)
                dst_s[h] = dst_new
                dhq_ref[rows, sl] = dhq
                dhf_ref[rows, sl] = dhf
                dhi_ref[rows, sl] = dhi
                dhg_ref[rows, sl] = dhg
                dlb_ref[:, sl] += dlb
                dnw_ref[...] += dnw
            return carry

        lax.fori_loop(0, ncb, chunk, 0, unroll=HG_UNROLL)

        if ride is not None:
            @pl.when(i == nb - 1)
            def _():
                ride.finish(*ride_refs)

    zb = lambda off: pl.BlockSpec((tb, w), lambda i: (nb - 1 - i, off // w))
    full = pl.BlockSpec((tb, w), lambda i: (nb - 1 - i, 0))
    const = lambda shape: pl.BlockSpec(shape, lambda i: (0,) * len(shape))
    r_in, r_args, r_out, r_shape, r_scratch = _ride_call(ride)
    outs = pl.pallas_call(
        body, name=name, grid=(nb,),
        in_specs=[zb(zl["hq"]), zb(zl["hf"]), zb(zl["hi"]), zb(zl["hg"]), full, full,
                  pl.BlockSpec((HG_HEADS, ncb, HG_D, HG_D), lambda i: (0, nb - 1 - i, 0, 0)),
                  const((1, w)), const((1, HG_D)), const(mall.shape), const(mall_t.shape), const(masks.shape)] + r_in,
        out_specs=[full, full, full, full, const((1, w)), const((1, HG_D))] + r_out,
        out_shape=[jax.ShapeDtypeStruct((lp, w), F32)] * 4
                  + [jax.ShapeDtypeStruct((1, w), F32), jax.ShapeDtypeStruct((1, HG_D), F32)] + r_shape,
        scratch_shapes=[pltpu.VMEM((HG_HEADS, HG_D, HG_D), F32)] + r_scratch,
        compiler_params=_params("arbitrary"),
    )(z, z, z, z, o_pre, dob, states, lb.reshape(1, w), nw.reshape(1, HG_D), mall, mall_t, masks, *r_args)
    dhq, dhf, dhi, dhg, dlb, dnw = outs[:6]
    return dhq, dhf, dhi, dhg, dlb[0], dnw[0], list(outs[6:])


def _loss_head(h, w, tpad, *, lv, name):
    lp, d = h.shape
    tm = _pick(lp, ROW_TILE, 8)

    def body(h_ref, w_ref, t_ref, dh_ref, loss_ref, dw_ref):
        i = pl.program_id(0)
        r0 = i * tm + lax.broadcasted_iota(jnp.int32, (tm, 1), 0)
        valid = ((r0 >= ROW_X) & (r0 < lv)).astype(F32)
        xv, wv = h_ref[...], w_ref[...]
        r = lax.rsqrt(jnp.mean(xv * xv, axis=-1, keepdims=True) + EPS)
        e = (xv * r * wv - t_ref[...]) * valid
        dy = e * (1.0 / d)
        wdy = dy * wv
        dh_ref[...] = r * wdy - xv * (r * r * r) * jnp.mean(xv * wdy, axis=-1, keepdims=True)

        @pl.when(i == 0)
        def _():
            loss_ref[...] = jnp.zeros_like(loss_ref)
            dw_ref[...] = jnp.zeros_like(dw_ref)

        loss_ref[...] += 0.5 * jnp.sum(jnp.mean(e * e, axis=-1, keepdims=True), axis=0, keepdims=True)
        dw_ref[...] += jnp.sum(dy * xv * r, axis=0, keepdims=True)

    row = pl.BlockSpec((tm, d), lambda i: (i, 0))
    vec = pl.BlockSpec((1, d), lambda i: (0, 0))
    dh, loss, dw = pl.pallas_call(
        body, name=name, grid=(lp // tm,), in_specs=[row, vec, row],
        out_specs=[row, pl.BlockSpec((8, LANES), lambda i: (0, 0)), vec],
        out_shape=[jax.ShapeDtypeStruct((lp, d), F32), jax.ShapeDtypeStruct((8, LANES), F32),
                   jax.ShapeDtypeStruct((1, d), F32)],
        compiler_params=_params("arbitrary"),
    )(h, w.reshape(1, d), tpad)
    return dh, loss[0, 0], dw[0]


def _adamw(w, g, m, v, *, name):
    shape = w.shape
    cols = shape[-1]
    rows = int(np.prod(shape[:-1])) if len(shape) > 1 else 1
    tr = _pick(rows, 256, 8)
    c1 = 1.0 - ADAM_B1 ** ADAM_STEP
    c2 = 1.0 - ADAM_B2 ** ADAM_STEP

    def body(w_ref, g_ref, m_ref, v_ref, d_ref, nm_ref, nv_ref):
        gv = g_ref[...]
        nm = ADAM_B1 * m_ref[...] + (1.0 - ADAM_B1) * gv
        nv = ADAM_B2 * v_ref[...] + (1.0 - ADAM_B2) * (gv * gv)
        d_ref[...] = -ADAM_LR * ((nm / c1) / (jnp.sqrt(nv / c2) + ADAM_EPS) + ADAM_WD * w_ref[...])
        nm_ref[...] = nm
        nv_ref[...] = nv

    spec = pl.BlockSpec((tr, cols), lambda i: (i, 0))
    r2 = lambda a: a.reshape(rows, cols)
    outs = pl.pallas_call(
        body, name=name, grid=(rows // tr,), in_specs=[spec] * 4, out_specs=[spec] * 3,
        out_shape=[jax.ShapeDtypeStruct((rows, cols), F32)] * 3, compiler_params=_params("parallel"),
    )(r2(w), r2(g), r2(m), r2(v))
    return tuple(o.reshape(shape) for o in outs)


HBM_SPEC = pl.BlockSpec(memory_space=pl.ANY)


def _coords():
    return lax.axis_index("x"), lax.axis_index("y"), lax.axis_index("c")


def _other_chips(x, y):
    return [(1 - x, y), (x, 1 - y), (1 - x, 1 - y)]


def _remote(src, dst, ssem, rsem, dev):
    return pltpu.make_async_remote_copy(src_ref=src, dst_ref=dst, send_sem=ssem, recv_sem=rsem,
                                        device_id=dev, device_id_type=MESH)


def _row_halves(w, c):
    rows = w.shape[-2]
    rh = rows // 2
    align = 8 * 4 // w.dtype.itemsize
    assert rh * 2 == rows and rh % align == 0, w.shape
    return pl.ds(pl.multiple_of(c * rh, align), rh), pl.ds(pl.multiple_of((1 - c) * rh, align), rh)


def _gathered_shapes(ws):
    return [jax.ShapeDtypeStruct((4, *w.shape), w.dtype) for w in ws]


def _own_block(g4s, ws):
    xi, yi, _ = _coords()
    return [lax.dynamic_update_slice(g4, w[None], (2 * xi + yi, 0, 0)) for g4, w in zip(g4s, ws)]


def _gather_chips(ws, *, name):
    n = len(ws)

    def body(*refs):
        w_refs, out_refs, (send_sems, recv_sems) = refs[:n], refs[n:2 * n], refs[2 * n:]
        x, y, c = _coords()
        sib = (x, y, 1 - c)
        chips = _other_chips(x, y)
        sent = []
        for t in range(n):
            half, _ = _row_halves(ws[t], c)
            for j, (px, py) in enumerate(chips):
                cp = _remote(w_refs[t].at[half], out_refs[t].at[2 * x + y, half], send_sems.at[6 * t + j],
                             recv_sems.at[6 * t + j], (px, py, c))
                cp.start()
                sent.append(cp)
        for t in range(n):
            half, _ = _row_halves(ws[t], c)
            for j, (px, py) in enumerate(chips):
                blk = out_refs[t].at[2 * px + py, half]
                _remote(w_refs[t].at[half], blk, send_sems.at[6 * t + j], recv_sems.at[6 * t + j],
                        (px, py, c)).wait_recv()
                fw = _remote(blk, blk, send_sems.at[6 * t + 3 + j], recv_sems.at[6 * t + 3 + j], sib)
                fw.start()
                sent.append(fw)
        for t in range(n):
            _, ohalf = _row_halves(ws[t], c)
            for j, (px, py) in enumerate(chips):
                blk = out_refs[t].at[2 * px + py, ohalf]
                _remote(blk, blk, send_sems.at[6 * t + 3 + j], recv_sems.at[6 * t + 3 + j], sib).wait_recv()
        for cp in sent:
            cp.wait_send()

    g4s = pl.pallas_call(
        body, name=name, in_specs=[HBM_SPEC] * n, out_specs=[HBM_SPEC] * n, out_shape=_gathered_shapes(ws),
        scratch_shapes=[pltpu.SemaphoreType.DMA((6 * n,)), pltpu.SemaphoreType.DMA((6 * n,))],
    )(*ws)
    return _own_block(g4s, ws)


def _swap_halves(gp, *, name):
    n, rows, cols = gp.shape
    rh = rows // 2

    def body(g_ref, out_ref, send_sems, recv_sems):
        x, y, c = _coords()
        sib = (x, y, 1 - c)
        ohalf = pl.ds(pl.multiple_of((1 - c) * rh, 8 * 4 // gp.dtype.itemsize), rh)
        cps = [_remote(g_ref.at[s, ohalf], out_ref.at[s], send_sems.at[s], recv_sems.at[s], sib) for s in range(n)]
        for cp in cps:
            cp.start()
        for cp in cps:
            cp.wait_recv()
        for cp in cps:
            cp.wait_send()

    return pl.pallas_call(
        body, name=name, in_specs=[HBM_SPEC], out_specs=HBM_SPEC,
        out_shape=jax.ShapeDtypeStruct((n, rh, cols), gp.dtype),
        scratch_shapes=[pltpu.SemaphoreType.DMA((n,)), pltpu.SemaphoreType.DMA((n,))],
    )(gp)


def _add_half(gp, got, cidx, *, name):
    n, rows, cols = gp.shape
    rh = rows // 2
    tr = _pick(rh, 512, 16)
    nrb = rh // tr

    def body(c_ref, a_ref, b_ref, o_ref):
        o_ref[...] = (a_ref[...].astype(F32) + b_ref[...].astype(F32)).astype(BF16)

    grid_spec = pltpu.PrefetchScalarGridSpec(
        num_scalar_prefetch=1, grid=(n, nrb),
        in_specs=[pl.BlockSpec((1, tr, cols), lambda s, i, c_ref: (s, c_ref[0] * nrb + i, 0)),
                  pl.BlockSpec((1, tr, cols), lambda s, i, c_ref: (s, i, 0))],
        out_specs=pl.BlockSpec((1, tr, cols), lambda s, i, c_ref: (s, i, 0)))
    return pl.pallas_call(
        body, name=name, grid_spec=grid_spec, out_shape=jax.ShapeDtypeStruct((n, rh, cols), BF16),
        compiler_params=_params("parallel", "parallel"),
    )(cidx, gp, got)


def _scatter_chips(p, *, name):
    _, rh, cols = p.shape

    def body(p_ref, out_ref, send_sems, recv_sems):
        x, y, c = _coords()
        cps = []
        for j, (px, py) in enumerate(_other_chips(x, y)):
            cps.append(_remote(p_ref.at[2 * px + py], out_ref.at[j], send_sems.at[j], recv_sems.at[j], (px, py, c)))
        for cp in cps:
            cp.start()
        for cp in cps:
            cp.wait_recv()
        for cp in cps:
            cp.wait_send()

    return pl.pallas_call(
        body, name=name, in_specs=[HBM_SPEC], out_specs=HBM_SPEC,
        out_shape=jax.ShapeDtypeStruct((3, rh, cols), p.dtype),
        scratch_shapes=[pltpu.SemaphoreType.DMA((3,)), pltpu.SemaphoreType.DMA((3,))],
    )(p)


def _sum_arrivals(p, land, kidx, *, name):
    _, rh, cols = p.shape
    tr = _pick(rh, 512, 16)

    def body(k_ref, a_ref, l_ref, o_ref):
        f = lambda v: v.astype(F32)
        o_ref[...] = ((f(a_ref[0]) + f(l_ref[0])) + f(l_ref[1])) + f(l_ref[2])

    grid_spec = pltpu.PrefetchScalarGridSpec(
        num_scalar_prefetch=1, grid=(rh // tr,),
        in_specs=[pl.BlockSpec((1, tr, cols), lambda i, k_ref: (k_ref[0], i, 0)),
                  pl.BlockSpec((3, tr, cols), lambda i, k_ref: (0, i, 0))],
        out_specs=pl.BlockSpec((tr, cols), lambda i, k_ref: (i, 0)))
    return pl.pallas_call(
        body, name=name, grid_spec=grid_spec, out_shape=jax.ShapeDtypeStruct((rh, cols), F32),
        compiler_params=_params("parallel"),
    )(kidx, p, land)


def _join_halves(q, *, name):
    rh, cols = q.shape

    def body(q_ref, out_ref, send_sem, recv_sem):
        x, y, c = _coords()
        half = pl.ds(pl.multiple_of(c * rh, 8), rh)
        ohalf = pl.ds(pl.multiple_of((1 - c) * rh, 8), rh)
        cp = _remote(q_ref, out_ref.at[half], send_sem, recv_sem, (x, y, 1 - c))
        cp.start()
        _remote(q_ref, out_ref.at[ohalf], send_sem, recv_sem, (x, y, 1 - c)).wait_recv()
        cp.wait_send()

    full = pl.pallas_call(
        body, name=name, in_specs=[HBM_SPEC], out_specs=HBM_SPEC,
        out_shape=jax.ShapeDtypeStruct((2 * rh, cols), q.dtype),
        scratch_shapes=[pltpu.SemaphoreType.DMA, pltpu.SemaphoreType.DMA],
    )(q)
    return lax.dynamic_update_slice(full, q, (lax.axis_index("c") * rh, 0))


def _rs_begin(gp, cidx, *, tag):
    got = _swap_halves(gp, name=f"rs_swap_{tag}")
    return _add_half(gp, got, cidx, name=f"rs_add_{tag}")


def _rs_end(p, land, kidx, *, tag):
    q = _sum_arrivals(p, land, kidx, name=f"rs_sum_{tag}")
    return _join_halves(q, name=f"rs_join_{tag}")


class _ScatterRide:
    def __init__(self, p):
        _, rh, cols = p.shape
        self.args = [p]
        self.out_shape = [jax.ShapeDtypeStruct((3, rh, cols), p.dtype)]
        self.scratch = [pltpu.SemaphoreType.DMA((3,)), pltpu.SemaphoreType.DMA((3,))]

    def _copies(self, p_ref, out_ref, ssem, rsem):
        x, y, c = _coords()
        return [_remote(p_ref.at[2 * px + py], out_ref.at[j], ssem.at[j], rsem.at[j], (px, py, c))
                for j, (px, py) in enumerate(_other_chips(x, y))]

    def start(self, *refs):
        for cp in self._copies(*refs):
            cp.start()

    def finish(self, *refs):
        cps = self._copies(*refs)
        for cp in cps:
            cp.wait_recv()
        for cp in cps:
            cp.wait_send()


class _GatherRide:
    def __init__(self, ws):
        n = len(ws)
        self.args = list(ws)
        self.out_shape = _gathered_shapes(ws)
        self.scratch = [pltpu.SemaphoreType.DMA((3 * n,)), pltpu.SemaphoreType.DMA((3 * n,))]

    def _copies(self, *refs):
        n = len(self.args)
        w_refs, out_refs, (ssem, rsem) = refs[:n], refs[n:2 * n], refs[2 * n:]
        x, y, c = _coords()
        send, recv = [], []
        for t in range(n):
            half, _ = _row_halves(self.args[t], c)
            for j, (px, py) in enumerate(_other_chips(x, y)):
                sems = (ssem.at[3 * t + j], rsem.at[3 * t + j], (px, py, c))
                send.append(_remote(w_refs[t].at[half], out_refs[t].at[2 * x + y, half], *sems))
                recv.append(_remote(w_refs[t].at[half], out_refs[t].at[2 * px + py, half], *sems))
        return send, recv

    def start(self, *refs):
        for cp in self._copies(*refs)[0]:
            cp.start()

    def finish(self, *refs):
        send, recv = self._copies(*refs)
        for cp in recv:
            cp.wait_recv()
        for cp in send:
            cp.wait_send()


def _gather_forward(g4s, *, name):
    n = len(g4s)

    def body(*refs):
        out_refs, (send_sems, recv_sems) = refs[n:2 * n], refs[2 * n:]
        x, y, c = _coords()
        sib = (x, y, 1 - c)
        chips = _other_chips(x, y)
        sent = []
        for t in range(n):
            half, _ = _row_halves(g4s[t], c)
            for j, (px, py) in enumerate(chips):
                blk = out_refs[t].at[2 * px + py, half]
                cp = _remote(blk, blk, send_sems.at[3 * t + j], recv_sems.at[3 * t + j], sib)
                cp.start()
                sent.append(cp)
        for t in range(n):
            _, ohalf = _row_halves(g4s[t], c)
            for j, (px, py) in enumerate(chips):
                blk = out_refs[t].at[2 * px + py, ohalf]
                _remote(blk, blk, send_sems.at[3 * t + j], recv_sems.at[3 * t + j], sib).wait_recv()
        for cp in sent:
            cp.wait_send()

    return pl.pallas_call(
        body, name=name, in_specs=[HBM_SPEC] * n, out_specs=[HBM_SPEC] * n,
        out_shape=[jax.ShapeDtypeStruct(g.shape, g.dtype) for g in g4s],
        input_output_aliases={t: t for t in range(n)},
        scratch_shapes=[pltpu.SemaphoreType.DMA((3 * n,)), pltpu.SemaphoreType.DMA((3 * n,))],
    )(*g4s)


def _allreduce_small(s, *, name):
    rows, cols = s.shape

    def body(s_ref, o_ref, buf, send_sems, recv_sems):
        x, y, c = _coords()
        me = 4 * x + 2 * y + c
        buf[me] = s_ref[...]
        cps = []
        for r in range(1, 8):
            peer = tuple((1 - v) if (r >> sh) & 1 else v for v, sh in ((x, 2), (y, 1), (c, 0)))
            cps.append(_remote(s_ref, buf.at[me], send_sems.at[r - 1], recv_sems.at[r - 1], peer))
        for cp in cps:
            cp.start()
        for cp in cps:
            cp.wait_recv()
        for cp in cps:
            cp.wait_send()
        acc = buf[0]
        for d in range(1, 8):
            acc = acc + buf[d]
        o_ref[...] = acc

    vm = pl.BlockSpec(memory_space=pltpu.VMEM)
    return pl.pallas_call(
        body, name=name, in_specs=[vm], out_specs=vm, out_shape=jax.ShapeDtypeStruct((rows, cols), F32),
        scratch_shapes=[pltpu.VMEM((8, rows, cols), F32), pltpu.SemaphoreType.DMA((7,)),
                        pltpu.SemaphoreType.DMA((7,))],
    )(s)


PACKED = ("ffn1_w_gu", "ffn1_w_down", "w_in", "w_uq", "w_ukv", "w_proj_attn", "w_proj_rec", "w_out",
          "ffn2_w_gu", "ffn2_w_down")
ROW_SHARDED = ("ffn1_w_down", "w_out", "ffn2_w_down")


def _pack_plan(shard_shapes):
    plan, off = {}, 0
    for n in PACKED:
        r, c = shard_shapes[n]
        assert (r * c) % PACK_W == 0
        plan[n] = (off, r * c // PACK_W, (r, c))
        off += r * c // PACK_W
    total = -(-off // 32) * 32
    return plan, total


def _pack(tensors, plan, total, dtype):
    parts = [tensors[n].astype(dtype).reshape(-1, PACK_W) for n in PACKED]
    used = sum(p.shape[0] for p in parts)
    if total > used:
        parts.append(jnp.zeros((total - used, PACK_W), dtype))
    return jnp.concatenate(parts, axis=0)


def _full_weights(g4s):
    out = {}
    for n, g in zip(PACKED, g4s):
        _, r, c = g.shape
        out[n] = g.reshape(4 * r, c) if n in ROW_SHARDED else jnp.swapaxes(g, 0, 1).reshape(r, 4 * c)
    return out


def _pack_grads(grads, plan, total):
    blocks = []
    for s in range(4):
        t = {}
        for n in PACKED:
            _, _, (r, c) = plan[n]
            t[n] = grads[n][s * r:(s + 1) * r] if n in ROW_SHARDED else grads[n][:, s * c:(s + 1) * c]
        blocks.append(_pack(t, plan, total, BF16))
    return jnp.stack(blocks)


def _unpack_shard(p, plan):
    return {n: p[plan[n][0]:plan[n][0] + plan[n][1]].reshape(plan[n][2]) for n in PACKED}


def _swap_cols(w):
    hlf = w.shape[1] // 2
    return jnp.concatenate([-w[:, hlf:], w[:, :hlf]], axis=1)


def _unswap_cols(dw):
    hlf = dw.shape[1] // 2
    return jnp.concatenate([dw[:, hlf:], -dw[:, :hlf]], axis=1)


def _layer_weights(full, d):
    zl = _z_layout(d)
    f = full["ffn1_w_down"].shape[0]
    w_in = full["w_in"]
    o = 0
    cols = {}
    for nm, wd in (("cq", Q_LORA), ("ckv", KV_LORA), ("kpe", QK_ROPE), ("hq", 512), ("hf", 512), ("hi", 512),
                   ("hg", 512), ("ga", d), ("gb", d)):
        cols[nm] = w_in[:, o:o + wd]
        o += wd
    zc = lambda n: jnp.zeros((d, n), BF16)
    win_p = jnp.concatenate(
        [cols["cq"], zc(QK_NOPE), cols["kpe"], zc(32), cols["ckv"], zc(QK_NOPE), _swap_cols(cols["kpe"]), zc(32),
         zc(LANES), cols["ga"], cols["gb"], cols["hq"], cols["hf"], cols["hi"], cols["hg"]], axis=1)
    assert win_p.shape[1] == zl["total"]
    wq = full["w_uq"].reshape(Q_LORA, MLA_HEADS, QK_NOPE + QK_ROPE)
    nope, rope = wq[:, :, :QK_NOPE], wq[:, :, QK_NOPE:]
    z32 = jnp.zeros((Q_LORA, MLA_HEADS, 32), BF16)
    z64 = jnp.zeros((Q_LORA, MLA_HEADS, QK_NOPE), BF16)
    rope_sw = jnp.concatenate([-rope[:, :, 16:], rope[:, :, :16]], axis=2)
    wqa = jnp.concatenate([nope, rope, z32], axis=2).reshape(Q_LORA, QW)
    wqb = jnp.concatenate([z64, rope_sw, z32], axis=2).reshape(Q_LORA, QW)
    wpa = full["w_proj_attn"].reshape(MLA_HEADS, V_HEAD, d)
    wpa_p = jnp.concatenate([jnp.zeros_like(wpa), wpa], axis=1).reshape(QW, d)
    return dict(
        wg1=full["ffn1_w_gu"][:, :f], wu1=full["ffn1_w_gu"][:, f:], wd1=full["ffn1_w_down"],
        wg2=full["ffn2_w_gu"][:, :f], wu2=full["ffn2_w_gu"][:, f:], wd2=full["ffn2_w_down"],
        win=win_p, wq2=jnp.concatenate([wqa, wqb], axis=1), wqa=wqa, wqb=wqb, wkv=full["w_ukv"], wpa=wpa_p,
        wpr=full["w_proj_rec"], wout=full["w_out"])


def _natural_grads(g, d):
    zl = _z_layout(d)
    dwin = g["win"]
    kpe = dwin[:, Z_KPA + QK_NOPE:Z_KPA + QK_NOPE + QK_ROPE] + _unswap_cols(
        dwin[:, Z_KPB + QK_NOPE:Z_KPB + QK_NOPE + QK_ROPE])
    w_in = jnp.concatenate(
        [dwin[:, Z_Q:Z_Q + Q_LORA], dwin[:, Z_KV:Z_KV + KV_LORA], kpe, dwin[:, zl["hq"]:zl["hq"] + 2048],
         dwin[:, zl["ga"]:zl["ga"] + 2 * d]], axis=1)
    qa = g["wqa"].reshape(Q_LORA, MLA_HEADS, HEAD_W)
    qb = g["wqb"].reshape(Q_LORA, MLA_HEADS, HEAD_W)[:, :, QK_NOPE:QK_NOPE + QK_ROPE]
    rope = qa[:, :, QK_NOPE:QK_NOPE + QK_ROPE] + jnp.concatenate([qb[:, :, 16:], -qb[:, :, :16]], axis=2)
    w_uq = jnp.concatenate([qa[:, :, :QK_NOPE], rope], axis=2).reshape(Q_LORA, -1)
    wpa = g["wpa"].reshape(MLA_HEADS, 2 * V_HEAD, d)[:, V_HEAD:].reshape(MLA_HEADS * V_HEAD, d)
    return dict(
        ffn1_w_gu=jnp.concatenate([g["wg1"], g["wu1"]], axis=1), ffn1_w_down=g["wd1"],
        ffn2_w_gu=jnp.concatenate([g["wg2"], g["wu2"]], axis=1), ffn2_w_down=g["wd2"],
        w_in=w_in, w_uq=w_uq, w_ukv=g["wkv"], w_proj_attn=wpa, w_proj_rec=g["wpr"], w_out=g["wout"])


def _rope_tables(lp):
    pos = jnp.maximum(jnp.arange(lp) - FRONT, 0).astype(F32)
    half = QK_ROPE // 2
    inv = ROPE_THETA ** (-jnp.arange(half, dtype=F32) / half)
    ang = pos[:, None] * inv[None, :]
    cos, sin = jnp.cos(ang), jnp.sin(ang)
    cos_t = jnp.concatenate([jnp.ones((lp, QK_NOPE), F32), cos, cos, jnp.zeros((lp, 32), F32)], axis=1)
    sin_t = jnp.concatenate([jnp.zeros((lp, QK_NOPE), F32), sin, sin, jnp.zeros((lp, 32), F32)], axis=1)
    return cos_t, sin_t


def _lower_bounds(raw):
    p = jax.nn.softmax(raw.astype(F32), axis=0)
    return jnp.cumsum(p, axis=0) - p[0:1]


def _ffn_fwd(h, nw, wg, wu, wd, tag):
    a, a_t = _rmsnorm_fwd(h, nw, width=h.shape[1], col_block=0, transposed=True, name=f"norm_{tag}")
    g = _matmul([(a, wg)], "nn", out_dtype=BF16, name=f"gate_{tag}")
    u = _matmul([(a, wu)], "nn", out_dtype=BF16, name=f"up_{tag}")
    act, act_t = _swiglu_fwd(g, u, name=f"swiglu_{tag}")
    out = _matmul([(act, wd)], "nn", res=h, scale=0.5, name=f"down_{tag}")
    return out, dict(h=h, a_t=a_t, g=g, u=u, act_t=act_t)


def _ffn_bwd(dout, sv, nw, wg, wu, wd, lv, tag):
    dact = _matmul([(dout, wd)], "nt", scale=0.5, out_dtype=BF16, name=f"ddown_{tag}")
    dwd = _matmul([(sv["act_t"], dout)], "nn", scale=0.5, name=f"dwdown_{tag}")
    dg, du = _swiglu_bwd(dact, sv["g"], sv["u"], name=f"dswiglu_{tag}")
    dwg = _matmul([(sv["a_t"], dg)], "nn", name=f"dwgate_{tag}")
    dwu = _matmul([(sv["a_t"], du)], "nn", name=f"dwup_{tag}")
    da = _matmul([(dg, wg), (du, wu)], "nt", name=f"dnormed_{tag}")
    dh, dn = _rmsnorm_bwd(sv["h"], nw, da, width=da.shape[1], col_block=0, lv=lv, dres=dout, name=f"dnorm_{tag}")
    return dh, dn, dwg, dwu, dwd


def _layer_fwd(h0, lw, sm, lb, tabs, consts, lv, l, ride=None):
    d = h0.shape[1]
    zl = _z_layout(d)
    cos_t, sin_t = tabs[:2]
    h1, s1 = _ffn_fwd(h0, sm["ffn1_norm"], lw["wg1"], lw["wu1"], lw["wd1"], f"ffn1_{l}")
    um, um_t = _rmsnorm_fwd(h1, sm["mix_norm"], width=d, col_block=0, transposed=True, name=f"norm_mix_{l}")
    z = _matmul([(um, lw["win"])], "nn", name=f"inproj_{l}")
    qn = _rmsnorm_fwd(z, sm["q_norm"], width=Q_LORA, col_block=Z_Q // Q_LORA, name=f"norm_q_{l}")
    kvn = _rmsnorm_fwd(z, sm["kv_norm"], width=KV_LORA, col_block=Z_KV // KV_LORA, name=f"norm_kv_{l}")
    q2 = _matmul([(qn, lw["wq2"])], "nn", name=f"uq_{l}")
    kv = _matmul([(kvn, lw["wkv"])], "nn", name=f"ukv_{l}")
    q, k, v = _qkv_prep_fwd(q2, kv, z, cos_t, sin_t, name=f"qkv_{l}")
    o, lse = _attn_fwd(q, k, v, tabs[2], lv=lv, name=f"attn_{l}")
    ya = _matmul([(o, lw["wpa"])], "nn", name=f"proj_attn_{l}")
    o_pre, ob, states, rode = _hgrn_fwd(z, lb, sm["hg_norm"], consts, zl=zl, lv=lv, name=f"hgrn_{l}", ride=ride)
    yb = _matmul([(ob, lw["wpr"])], "nn", name=f"proj_rec_{l}")
    mg, mg_t = _merge_fwd(ya, yb, z, zl=zl, name=f"merge_{l}")
    h2 = _matmul([(mg, lw["wout"])], "nn", res=h1, name=f"out_{l}")
    h3, s2 = _ffn_fwd(h2, sm["ffn2_norm"], lw["wg2"], lw["wu2"], lw["wd2"], f"ffn2_{l}")
    saved = dict(s1=s1, s2=s2, h1=h1, um_t=um_t, z=z, qn=qn, kvn=kvn, q=q, k=k, v=v, o=o, lse=lse, ya=ya, yb=yb,
                 o_pre=o_pre, ob=ob, states=states, mg_t=mg_t)
    return h3, saved, rode


def _layer_bwd(dh3, sv, lw, sm, lb, tabs, consts, lv, l, ride=None):
    d = dh3.shape[1]
    lp = dh3.shape[0]
    zl = _z_layout(d)
    cos_t, sin_t = tabs[:2]
    z = sv["z"]
    g = {}
    sg = {}
    dh2, sg["ffn2_norm"], g["wg2"], g["wu2"], g["wd2"] = _ffn_bwd(
        dh3, sv["s2"], sm["ffn2_norm"], lw["wg2"], lw["wu2"], lw["wd2"], lv, f"ffn2_{l}")
    dmg = _matmul([(dh2, lw["wout"])], "nt", name=f"dmerged_{l}")
    g["wout"] = _matmul([(sv["mg_t"], dh2)], "nn", name=f"dwout_{l}")
    dya, dyb, dga, dgb = _merge_bwd(dmg, sv["ya"], sv["yb"], z, zl=zl, name=f"dmerge_{l}")
    doa = _matmul([(dya, lw["wpa"])], "nt", name=f"dattn_out_{l}")
    g["wpa"] = _matmul([(sv["o"], dya)], "tn", name=f"dwproj_attn_{l}")
    dob = _matmul([(dyb, lw["wpr"])], "nt", name=f"drec_out_{l}")
    g["wpr"] = _matmul([(sv["ob"], dyb)], "tn", name=f"dwproj_rec_{l}")
    dhq, dhf, dhi, dhg, dlb, sg["hg_norm"], rode = _hgrn_bwd(
        z, sv["o_pre"], dob, sv["states"], lb, sm["hg_norm"], consts, zl=zl, lv=lv, name=f"dhgrn_{l}", ride=ride)
    delta = _attn_delta(doa, sv["o"], name=f"attn_delta_{l}")
    dq, dk, dv = _attn_bwd(sv["q"], sv["k"], sv["v"], doa, sv["lse"], delta, tabs[2], name=f"dattn_{l}")
    dqa, dqb, dkv, dza, dzb = _qkv_prep_bwd(dq, dk, dv, cos_t, sin_t, name=f"dqkv_{l}")
    dqn = _matmul([(dqa, lw["wqa"]), (dqb, lw["wqb"])], "nt", name=f"dqn_{l}")
    g["wqa"] = _matmul([(sv["qn"], dqa)], "tn", name=f"dwqa_{l}")
    g["wqb"] = _matmul([(sv["qn"], dqb)], "tn", name=f"dwqb_{l}")
    dkvn = _matmul([(dkv, lw["wkv"])], "nt", name=f"dkvn_{l}")
    g["wkv"] = _matmul([(sv["kvn"], dkv)], "tn", name=f"dwkv_{l}")
    dzq, sg["q_norm"] = _rmsnorm_bwd(z, sm["q_norm"], dqn, width=Q_LORA, col_block=Z_Q // Q_LORA, lv=lv,
                                     name=f"dnorm_q_{l}")
    dzkv, sg["kv_norm"] = _rmsnorm_bwd(z, sm["kv_norm"], dkvn, width=KV_LORA, col_block=Z_KV // KV_LORA, lv=lv,
                                       name=f"dnorm_kv_{l}")
    dz = jnp.concatenate([dzq, dza, dzkv, dzb, jnp.zeros((lp, LANES), F32), dga, dgb, dhq, dhf, dhi, dhg],
                         axis=1).astype(BF16)
    dum = _matmul([(dz, lw["win"])], "nt", name=f"dmixed_{l}")
    g["win"] = _matmul([(sv["um_t"], dz)], "nn", name=f"dwin_{l}")
    dh1, sg["mix_norm"] = _rmsnorm_bwd(sv["h1"], sm["mix_norm"], dum, width=d, col_block=0, lv=lv, dres=dh2,
                                       name=f"dnorm_mix_{l}")
    dh0, sg["ffn1_norm"], g["wg1"], g["wu1"], g["wd1"] = _ffn_bwd(
        dh1, sv["s1"], sm["ffn1_norm"], lw["wg1"], lw["wu1"], lw["wd1"], lv, f"ffn1_{l}")
    return dh0, g, sg, dlb, rode


WEIGHTS = ("meta_tokens", "ffn1_norm", "ffn1_w_gu", "ffn1_w_down", "mix_norm", "w_in", "q_norm", "kv_norm", "w_uq",
           "w_ukv", "hg_lb_raw", "hg_norm", "w_proj_attn", "w_proj_rec", "w_out", "ffn2_norm", "ffn2_w_gu",
           "ffn2_w_down", "final_norm")
SMALL = ("ffn1_norm", "mix_norm", "q_norm", "kv_norm", "hg_lb_raw", "hg_norm", "ffn2_norm")


def _small_rows(vals):
    pad = lambda a: jnp.pad(a, ((0, -a.shape[0] % 8), (0, PACK_W - a.shape[1])))
    rows = [pad(vals[n]) for n in SMALL]
    rows.append(pad(vals["final_norm"][None, :]))
    rows.append(pad(vals["meta_tokens"]))
    rows.append(pad(vals["loss"].reshape(1, 1)))
    return jnp.concatenate(rows, axis=0)


def _small_unrows(s, d, widths):
    out, o = {}, 0
    for n in SMALL:
        out[n] = s[o:o + DEPTH, :widths[n]]
        o += -(-DEPTH // 8) * 8
    out["final_norm"] = s[o, :d]
    o += 8
    out["meta_tokens"] = s[o:o + N_META, :d]
    o += -(-N_META // 8) * 8
    out["loss"] = s[o, 0]
    return out


def _step(args):
    x = args["x"][0]
    seq, d = x.shape
    assert d <= PACK_W
    lv = ROW_X + seq
    lp = -(-lv // ROW_TILE) * ROW_TILE
    xi, yi, ci = _coords()
    kidx = (2 * xi + yi).astype(jnp.int32).reshape(1)
    cidx = ci.astype(jnp.int32).reshape(1)
    consts = _hgrn_consts()
    tabs = (*_rope_tables(lp), _attn_consts(lp))

    shard_shapes = {n: args[n].shape[1:] for n in PACKED}
    plan, total = _pack_plan(shard_shapes)
    shards = [[args[n][l].astype(BF16) for n in PACKED] for l in range(DEPTH)]
    mt = args["meta_tokens"]
    mt4 = _gather_chips([mt], name="gather_meta")[0]
    meta = jnp.concatenate(list(mt4), axis=1)

    sm = [{n: args[n][l] for n in SMALL} for l in range(DEPTH)]
    lbs = _lower_bounds(args["hg_lb_raw"])

    h = jnp.concatenate([jnp.zeros((FRONT, d), F32), meta, x, jnp.zeros((lp - lv, d), F32)], axis=0)
    saved, lws = [], []
    g4s = _gather_chips(shards[0], name="gather_0")
    for l in range(DEPTH):
        lws.append(_layer_weights(_full_weights(g4s), d))
        ride = _GatherRide(shards[l + 1]) if l + 1 < DEPTH else None
        h, sv, rode = _layer_fwd(h, lws[l], sm[l], lbs[l], tabs, consts, lv, l, ride)
        saved.append(sv)
        if ride is not None:
            g4s = _own_block(_gather_forward(rode, name=f"gather_fwd_{l + 1}"), shards[l + 1])
    tpad = jnp.pad(args["loss_target"][0], ((ROW_X, lp - lv), (0, 0)))
    dh, loss, dfinal = _loss_head(h, args["final_norm"], tpad, lv=lv, name="loss_head")

    small = {n: [None] * DEPTH for n in SMALL}
    dlbs = [None] * DEPTH
    shard_grads = [None] * DEPTH
    waiting = None
    for l in reversed(range(DEPTH)):
        ride = _ScatterRide(waiting) if waiting is not None else None
        dh, g, sg, dlbs[l], rode = _layer_bwd(dh, saved[l], lws[l], sm[l], lbs[l], tabs, consts, lv, l, ride)
        if ride is not None:
            shard_grads[l + 1] = _unpack_shard(_rs_end(waiting, rode[0], kidx, tag=str(l + 1)), plan)
        for n in sg:
            small[n][l] = sg[n]
        waiting = _rs_begin(_pack_grads(_natural_grads(g, d), plan, total), cidx, tag=str(l))
    land = _scatter_chips(waiting, name="rs_scatter_0")
    shard_grads[0] = _unpack_shard(_rs_end(waiting, land, kidx, tag="0"), plan)

    _, lb_vjp = jax.vjp(_lower_bounds, args["hg_lb_raw"])
    small_vals = {n: jnp.stack(small[n]) for n in SMALL if n != "hg_lb_raw"}
    small_vals["hg_lb_raw"] = lb_vjp(jnp.stack(dlbs))[0]
    small_vals["final_norm"] = dfinal
    small_vals["meta_tokens"] = dh[FRONT:ROW_X]
    small_vals["loss"] = loss
    widths = {n: args[n].shape[1] for n in SMALL}
    tot = _small_unrows(_allreduce_small(_small_rows(small_vals), name="allreduce_small"), d, widths)

    grads = {n: jnp.stack([shard_grads[l][n] for l in range(DEPTH)]) for n in PACKED}
    for n in SMALL:
        grads[n] = tot[n]
    grads["final_norm"] = tot["final_norm"]
    mcols = mt.shape[1]
    grads["meta_tokens"] = lax.dynamic_slice_in_dim(tot["meta_tokens"], (2 * xi + yi) * mcols, mcols, axis=1)
    grad_x = dh[ROW_X:lv][None]

    delta, new_m, new_v = {}, {}, {}
    for n in WEIGHTS:
        delta[n], new_m[n], new_v[n] = _adamw(args[n], grads[n], args["m_" + n], args["v_" + n], name=f"adamw_{n}")
    return (tot["loss"], grad_x, *[grads[n] for n in WEIGHTS], *[delta[n] for n in WEIGHTS],
            *[new_m[n] for n in WEIGHTS], *[new_v[n] for n in WEIGHTS])


def kernel(x, meta_tokens, ffn1_norm, ffn1_w_gu, ffn1_w_down, mix_norm, w_in, q_norm, kv_norm, w_uq, w_ukv, hg_lb_raw, hg_norm, w_proj_attn, w_proj_rec, w_out, ffn2_norm, ffn2_w_gu, ffn2_w_down, final_norm, loss_target, m_meta_tokens, m_ffn1_norm, m_ffn1_w_gu, m_ffn1_w_down, m_mix_norm, m_w_in, m_q_norm, m_kv_norm, m_w_uq, m_w_ukv, m_hg_lb_raw, m_hg_norm, m_w_proj_attn, m_w_proj_rec, m_w_out, m_ffn2_norm, m_ffn2_w_gu, m_ffn2_w_down, m_final_norm, v_meta_tokens, v_ffn1_norm, v_ffn1_w_gu, v_ffn1_w_down, v_mix_norm, v_w_in, v_q_norm, v_kv_norm, v_w_uq, v_w_ukv, v_hg_lb_raw, v_hg_norm, v_w_proj_attn, v_w_proj_rec, v_w_out, v_ffn2_norm, v_ffn2_w_gu, v_ffn2_w_down, v_final_norm):
    return _step(dict(locals()))
```

```python
import functools
import math

import numpy as np
import jax
import jax.numpy as jnp
from jax import lax
from jax.experimental import pallas as pl
from jax.experimental.pallas import tpu as pltpu

F32 = jnp.float32
BF16 = jnp.bfloat16

N_META = 16
MLA_HEADS = 8
Q_LORA = 384
KV_LORA = 256
QK_NOPE = 64
QK_ROPE = 32
V_HEAD = 64
ROPE_THETA = 10000.0
HG_HEADS = 4
HG_D = 128
HG_CHUNK = 64
EPS = 1e-6
NEG_BIG = -1e30
F_MIN = 1e-20
DEPTH = 4

ADAM_LR = 0.001
ADAM_B1 = 0.9
ADAM_B2 = 0.999
ADAM_EPS = 1e-08
ADAM_WD = 0.01
ADAM_STEP = 10

LANES = 128
FRONT = (-N_META) % HG_CHUNK
ROW_X = FRONT + N_META
ROW_TILE = 640
HEAD_W = 128
QW = MLA_HEADS * HEAD_W
PREP_HEADS = 4
VMEM_LIMIT = 56 * 1024 * 1024
MATMUL_VMEM = 42 * 1024 * 1024
PACK_W = 1024
MESH = pl.DeviceIdType.MESH

Z_Q, Z_KPA, Z_KV, Z_KPB, Z_PAD, Z_GA = 0, 384, 512, 768, 896, 1024


def _z_layout(d):
    ga = Z_GA
    gb = ga + d
    hq = gb + d
    hf = hq + 512
    hi = hf + 512
    hg = hi + 512
    return dict(ga=ga, gb=gb, hq=hq, hf=hf, hi=hi, hg=hg, total=hg + 512)


def _pick(dim, cap, mult=LANES):
    if dim <= cap:
        return dim
    best = None
    for t in range(mult, cap + 1, mult):
        if dim % t == 0:
            best = t
    assert best is not None, (dim, cap, mult)
    return best


def _params(*sem):
    return pltpu.CompilerParams(dimension_semantics=sem, vmem_limit_bytes=VMEM_LIMIT)


def _sigmoid(x):
    return 1.0 / (1.0 + jnp.exp(-x))


def _row_valid(row0, n, lv):
    r = row0 + lax.broadcasted_iota(jnp.int32, (n, 1), 0)
    return ((r >= FRONT) & (r < lv)).astype(F32)


_DIMS = {"nn": (((1,), (0,)), ((), ())), "nt": (((1,), (1,)), ((), ())), "tn": (((0,), (0,)), ((), ()))}


def _matmul(pairs, mode, *, name, out_dtype=F32, res=None, scale=1.0):
    a0, b0 = pairs[0]
    if mode == "nn":
        (m, k), n = a0.shape, b0.shape[1]
    elif mode == "nt":
        (m, k), n = a0.shape, b0.shape[0]
    else:
        (k, m), n = a0.shape, b0.shape[1]
    if mode == "tn":
        tm, tn, tk = _pick(m, 1024), _pick(n, 1408), _pick(k, ROW_TILE, 8)
    else:
        if k > m:
            tm, tn, kcap = _pick(m, 1408, 16), _pick(n, 1408), 1664
        else:
            tm, tn, kcap = _pick(m, ROW_TILE, 8), _pick(n, 2816), 2816
        out_b = jnp.dtype(out_dtype).itemsize
        per_k = len(pairs) * 2 * (tm * a0.dtype.itemsize + tn * b0.dtype.itemsize)
        fixed = tm * tn * (2 * out_b + 4 + (8 if res is not None else 0))
        tk = _pick(k, kcap)
        while tk > LANES and fixed + per_k * tk > MATMUL_VMEM:
            tk = _pick(k, tk - LANES)
    nk = k // tk
    npair = len(pairs)
    dims = _DIMS[mode]

    def body(*refs):
        ins = refs[:2 * npair]
        pos = 2 * npair
        res_ref = None
        if res is not None:
            res_ref = refs[pos]
            pos += 1
        o_ref = refs[pos]
        kk = pl.program_id(2)

        part = None
        for p in range(npair):
            a = ins[2 * p][...].astype(BF16)
            b = ins[2 * p + 1][...].astype(BF16)
            d = lax.dot_general(a, b, dims, preferred_element_type=F32)
            part = d if part is None else part + d

        def finish(r):
            if scale != 1.0:
                r = r * scale
            if res_ref is not None:
                r = r + res_ref[...]
            o_ref[...] = r.astype(out_dtype)

        if nk == 1:
            finish(part)
            return
        acc = refs[pos + 1]

        @pl.when(kk == 0)
        def _():
            acc[...] = part

        @pl.when(kk > 0)
        def _():
            acc[...] += part

        @pl.when(kk == nk - 1)
        def _():
            finish(acc[...])

    if mode == "nn":
        a_spec = pl.BlockSpec((tm, tk), lambda i, j, q: (i, q))
        b_spec = pl.BlockSpec((tk, tn), lambda i, j, q: (q, j))
    elif mode == "nt":
        a_spec = pl.BlockSpec((tm, tk), lambda i, j, q: (i, q))
        b_spec = pl.BlockSpec((tn, tk), lambda i, j, q: (j, q))
    else:
        a_spec = pl.BlockSpec((tk, tm), lambda i, j, q: (q, i))
        b_spec = pl.BlockSpec((tk, tn), lambda i, j, q: (q, j))
    o_spec = pl.BlockSpec((tm, tn), lambda i, j, q: (i, j))
    in_specs, args = [], []
    for a, b in pairs:
        in_specs += [a_spec, b_spec]
        args += [a, b]
    if res is not None:
        in_specs.append(o_spec)
        args.append(res)
    return pl.pallas_call(
        body, name=name, grid=(m // tm, n // tn, nk), in_specs=in_specs, out_specs=o_spec,
        out_shape=jax.ShapeDtypeStruct((m, n), out_dtype),
        scratch_shapes=[pltpu.VMEM((tm, tn), F32)] if nk > 1 else [],
        compiler_params=_params("parallel", "parallel", "arbitrary"),
    )(*args)


def _rmsnorm_fwd(x, w, *, width, col_block, name, transposed=False):
    lp = x.shape[0]
    tm = _pick(lp, ROW_TILE, 8)

    def body(x_ref, w_ref, o_ref, *ot_ref):
        xv = x_ref[...]
        r = lax.rsqrt(jnp.mean(xv * xv, axis=-1, keepdims=True) + EPS)
        y = xv * r * w_ref[...]
        o_ref[...] = y.astype(BF16)
        if transposed:
            ot_ref[0][...] = y.T.astype(BF16)

    out_specs = [pl.BlockSpec((tm, width), lambda i: (i, 0))]
    out_shape = [jax.ShapeDtypeStruct((lp, width), BF16)]
    if transposed:
        out_specs.append(pl.BlockSpec((width, tm), lambda i: (0, i)))
        out_shape.append(jax.ShapeDtypeStruct((width, lp), BF16))
    outs = pl.pallas_call(
        body, name=name, grid=(lp // tm,),
        in_specs=[pl.BlockSpec((tm, width), lambda i: (i, col_block)), pl.BlockSpec((1, width), lambda i: (0, 0))],
        out_specs=out_specs, out_shape=out_shape, compiler_params=_params("parallel"),
    )(x, w.reshape(1, width))
    return tuple(outs) if transposed else outs[0]


def _rmsnorm_bwd(x, w, dy, *, width, col_block, lv, name, dres=None):
    lp = x.shape[0]
    tm = _pick(lp, ROW_TILE, 8)

    def body(*refs):
        if dres is None:
            x_ref, w_ref, dy_ref, dx_ref, dw_ref = refs
            dres_ref = None
        else:
            x_ref, w_ref, dy_ref, dres_ref, dx_ref, dw_ref = refs
        i = pl.program_id(0)
        xv = x_ref[...]
        dyv = dy_ref[...] * _row_valid(i * tm, tm, lv)
        r = lax.rsqrt(jnp.mean(xv * xv, axis=-1, keepdims=True) + EPS)
        wdy = dyv * w_ref[...]
        dx = r * wdy - xv * (r * r * r) * jnp.mean(xv * wdy, axis=-1, keepdims=True)
        if dres_ref is not None:
            dx = dx + dres_ref[...]
        dx_ref[...] = dx

        @pl.when(i == 0)
        def _():
            dw_ref[...] = jnp.zeros_like(dw_ref)

        dw_ref[...] += jnp.sum(dyv * xv * r, axis=0, keepdims=True)

    row = pl.BlockSpec((tm, width), lambda i: (i, 0))
    in_specs = [pl.BlockSpec((tm, width), lambda i: (i, col_block)), pl.BlockSpec((1, width), lambda i: (0, 0)), row]
    args = [x, w.reshape(1, width), dy]
    if dres is not None:
        in_specs.append(row)
        args.append(dres)
    dx, dw = pl.pallas_call(
        body, name=name, grid=(lp // tm,), in_specs=in_specs,
        out_specs=[row, pl.BlockSpec((1, width), lambda i: (0, 0))],
        out_shape=[jax.ShapeDtypeStruct((lp, width), F32), jax.ShapeDtypeStruct((1, width), F32)],
        compiler_params=_params("arbitrary"),
    )(*args)
    return dx, dw[0]


def _swiglu_fwd(g, u, *, name):
    lp, f = g.shape
    tm, tf = _pick(lp, ROW_TILE, 8), _pick(f, 1408)

    def body(g_ref, u_ref, o_ref, ot_ref):
        gv = g_ref[...].astype(F32)
        act = gv * _sigmoid(gv) * u_ref[...].astype(F32)
        o_ref[...] = act.astype(BF16)
        ot_ref[...] = act.T.astype(BF16)

    spec = pl.BlockSpec((tm, tf), lambda i, j: (i, j))
    return pl.pallas_call(
        body, name=name, grid=(lp // tm, f // tf), in_specs=[spec, spec],
        out_specs=[spec, pl.BlockSpec((tf, tm), lambda i, j: (j, i))],
        out_shape=[jax.ShapeDtypeStruct((lp, f), BF16), jax.ShapeDtypeStruct((f, lp), BF16)],
        compiler_params=_params("parallel", "parallel"),
    )(g, u)


def _swiglu_bwd(dact, g, u, *, name):
    lp, f = g.shape
    tm, tf = _pick(lp, ROW_TILE, 8), _pick(f, 1408)

    def body(d_ref, g_ref, u_ref, dg_ref, du_ref):
        gv, dv = g_ref[...].astype(F32), d_ref[...].astype(F32)
        s = _sigmoid(gv)
        dg_ref[...] = (dv * u_ref[...].astype(F32) * s * (1.0 + gv * (1.0 - s))).astype(BF16)
        du_ref[...] = (dv * gv * s).astype(BF16)

    spec = pl.BlockSpec((tm, tf), lambda i, j: (i, j))
    return pl.pallas_call(
        body, name=name, grid=(lp // tm, f // tf), in_specs=[spec, spec, spec], out_specs=[spec, spec],
        out_shape=[jax.ShapeDtypeStruct((lp, f), BF16)] * 2, compiler_params=_params("parallel", "parallel"),
    )(dact, g, u)


def _merge_fwd(ya, yb, z, *, zl, name):
    lp, d = ya.shape
    tm, td = _pick(lp, ROW_TILE, 8), _pick(d, 512)
    oa, ob = zl["ga"] // td, zl["gb"] // td

    def body(ya_ref, yb_ref, ga_ref, gb_ref, o_ref, ot_ref):
        mg = _sigmoid(ga_ref[...]) * ya_ref[...] + _sigmoid(gb_ref[...]) * yb_ref[...]
        o_ref[...] = mg.astype(BF16)
        ot_ref[...] = mg.T.astype(BF16)

    spec = pl.BlockSpec((tm, td), lambda i, j: (i, j))
    return pl.pallas_call(
        body, name=name, grid=(lp // tm, d // td),
        in_specs=[spec, spec, pl.BlockSpec((tm, td), lambda i, j: (i, oa + j)),
                  pl.BlockSpec((tm, td), lambda i, j: (i, ob + j))],
        out_specs=[spec, pl.BlockSpec((td, tm), lambda i, j: (j, i))],
        out_shape=[jax.ShapeDtypeStruct((lp, d), BF16), jax.ShapeDtypeStruct((d, lp), BF16)],
        compiler_params=_params("parallel", "parallel"),
    )(ya, yb, z, z)


def _merge_bwd(dmg, ya, yb, z, *, zl, name):
    lp, d = ya.shape
    tm, td = _pick(lp, ROW_TILE, 8), _pick(d, 512)
    oa, ob = zl["ga"] // td, zl["gb"] // td

    def body(d_ref, ya_ref, yb_ref, ga_ref, gb_ref, dya_ref, dyb_ref, dga_ref, dgb_ref):
        dv = d_ref[...]
        sa, sb = _sigmoid(ga_ref[...]), _sigmoid(gb_ref[...])
        dya_ref[...] = (dv * sa).astype(BF16)
        dyb_ref[...] = (dv * sb).astype(BF16)
        dga_ref[...] = dv * ya_ref[...] * sa * (1.0 - sa)
        dgb_ref[...] = dv * yb_ref[...] * sb * (1.0 - sb)

    spec = pl.BlockSpec((tm, td), lambda i, j: (i, j))
    return pl.pallas_call(
        body, name=name, grid=(lp // tm, d // td),
        in_specs=[spec, spec, spec, pl.BlockSpec((tm, td), lambda i, j: (i, oa + j)),
                  pl.BlockSpec((tm, td), lambda i, j: (i, ob + j))],
        out_specs=[spec] * 4,
        out_shape=[jax.ShapeDtypeStruct((lp, d), BF16)] * 2 + [jax.ShapeDtypeStruct((lp, d), F32)] * 2,
        compiler_params=_params("parallel", "parallel"),
    )(dmg, ya, yb, z, z)


def _qkv_prep_fwd(q2, kv, z, cos_t, sin_t, *, name):
    lp = q2.shape[0]
    tm = _pick(lp, ROW_TILE, 8)
    h = MLA_HEADS
    wd = PREP_HEADS * HEAD_W

    def body(qa_ref, qb_ref, kv_ref, za_ref, zb_ref, c_ref, s_ref, q_ref, k_ref, v_ref):
        c, s = c_ref[...], s_ref[...]
        lane = lax.broadcasted_iota(jnp.int32, (tm, HEAD_W), 1)
        kr = jnp.where(lane >= QK_NOPE, za_ref[...] * c + zb_ref[...] * s, 0.0)
        for g in range(PREP_HEADS):
            sl = slice(g * HEAD_W, (g + 1) * HEAD_W)
            q_ref[:, sl] = ((qa_ref[:, sl] * c + qb_ref[:, sl] * s) * Q_SCALE).astype(BF16)
            kvv = kv_ref[:, sl]
            k_ref[:, sl] = (jnp.where(lane < QK_NOPE, kvv, 0.0) + kr).astype(BF16)
            v_ref[:, sl] = jnp.where(lane >= QK_NOPE, kvv, jnp.where(lane == 0, 1.0, 0.0)).astype(BF16)

    blk = lambda w, f: pl.BlockSpec((tm, w), f)
    out = blk(wd, lambda i, j: (i, j))
    return pl.pallas_call(
        body, name=name, grid=(lp // tm, h // PREP_HEADS),
        in_specs=[blk(wd, lambda i, j: (i, j)), blk(wd, lambda i, j: (i, h // PREP_HEADS + j)),
                  blk(wd, lambda i, j: (i, j)),
                  blk(HEAD_W, lambda i, j: (i, Z_KPA // HEAD_W)), blk(HEAD_W, lambda i, j: (i, Z_KPB // HEAD_W)),
                  blk(HEAD_W, lambda i, j: (i, 0)), blk(HEAD_W, lambda i, j: (i, 0))],
        out_specs=[out, out, out], out_shape=[jax.ShapeDtypeStruct((lp, QW), BF16)] * 3,
        compiler_params=_params("parallel", "parallel"),
    )(q2, q2, kv, z, z, cos_t, sin_t)


def _qkv_prep_bwd(dq, dk, dv, cos_t, sin_t, *, name):
    lp = dq.shape[0]
    tm = _pick(lp, ROW_TILE, 8)
    h = MLA_HEADS
    wd = PREP_HEADS * HEAD_W

    def body(dq_ref, dk_ref, dv_ref, c_ref, s_ref, dqa_ref, dqb_ref, dkv_ref, dza_ref, dzb_ref):
        j = pl.program_id(1)
        c, s = c_ref[...], s_ref[...]
        lane = lax.broadcasted_iota(jnp.int32, (tm, HEAD_W), 1)
        dkr = jnp.zeros((tm, HEAD_W), F32)
        for g in range(PREP_HEADS):
            sl = slice(g * HEAD_W, (g + 1) * HEAD_W)
            dqv, dkv_ = dq_ref[:, sl], dk_ref[:, sl]
            dqa_ref[:, sl] = (dqv * c).astype(BF16)
            dqb_ref[:, sl] = (dqv * s).astype(BF16)
            dkv_ref[:, sl] = jnp.where(lane < QK_NOPE, dkv_, dv_ref[:, sl]).astype(BF16)
            dkr = dkr + jnp.where(lane >= QK_NOPE, dkv_, 0.0)

        @pl.when(j == 0)
        def _():
            dza_ref[...] = jnp.zeros_like(dza_ref)
            dzb_ref[...] = jnp.zeros_like(dzb_ref)

        dza_ref[...] += dkr * c
        dzb_ref[...] += dkr * s

    blk = lambda w, f: pl.BlockSpec((tm, w), f)
    per_head, shared = blk(wd, lambda i, j: (i, j)), blk(HEAD_W, lambda i, j: (i, 0))
    return pl.pallas_call(
        body, name=name, grid=(lp // tm, h // PREP_HEADS),
        in_specs=[per_head, per_head, per_head, shared, shared],
        out_specs=[per_head, per_head, per_head, shared, shared],
        out_shape=[jax.ShapeDtypeStruct((lp, QW), BF16)] * 3 + [jax.ShapeDtypeStruct((lp, HEAD_W), F32)] * 2,
        compiler_params=_params("parallel", "arbitrary"),
    )(dq, dk, dv, cos_t, sin_t)


def _attn_tile(lp):
    return _pick(lp, ROW_TILE, LANES)


Q_SCALE = (QK_NOPE + QK_ROPE) ** -0.5 * math.log2(math.e)
ATT_HP = 2
ATT_HP_FWD = 4


def _attn_consts(lp):
    t = _attn_tile(lp)
    nb = lp // t
    r = np.arange(t)
    causal = np.where(r[None, :] <= r[:, None], 0.0, NEG_BIG).astype(np.float32)
    front = np.where(r >= FRONT, 0.0, NEG_BIG).astype(np.float32)[None, :]
    diag = np.stack([np.minimum(causal, front), causal])
    qmaj = [(i, j) for i in range(nb) for j in range(i + 1)]
    kmaj = [(i, j) for j in range(nb) for i in range(j, nb)]
    tab = lambda pairs, c: jnp.asarray([p[c] for p in pairs], jnp.int32)
    return dict(diag=jnp.asarray(diag), front=jnp.asarray(front),
                fwd=(tab(qmaj, 0), tab(qmaj, 1)), bwd=(tab(kmaj, 0), tab(kmaj, 1)))


def _attn_fwd(q, k, v, ac, *, lv, name):
    lp = q.shape[0]
    t = _attn_tile(lp)
    nb = lp // t
    rep = t // HEAD_W
    qtab, ktab = ac["fwd"]

    def body(qt_ref, kt_ref, q_ref, k_ref, v_ref, bd_ref, bf_ref, o_ref, lse_ref, m_s, acc_s):
        step_id = pl.program_id(1)
        qb, kb = qt_ref[step_id], kt_ref[step_id]

        @pl.when(kb == 0)
        def _():
            m_s[...] = jnp.full_like(m_s, NEG_BIG)
            acc_s[...] = jnp.zeros_like(acc_s)

        def step(bias):
            b = None if bias is None else bias()
            for hh in range(ATT_HP_FWD):
                sl = slice(hh * HEAD_W, (hh + 1) * HEAD_W)
                s = lax.dot_general(q_ref[:, sl], k_ref[:, sl], _DIMS["nt"], preferred_element_type=F32)
                if b is not None:
                    s = s + b
                m_prev = m_s[:, sl]
                m_new = jnp.maximum(m_prev, jnp.max(s, axis=-1, keepdims=True))
                alpha = jnp.exp2(m_prev - m_new)
                p = jnp.exp2(s - jnp.tile(m_new, (1, rep)))
                acc_s[:, sl] = alpha * acc_s[:, sl] + jnp.dot(p.astype(BF16), v_ref[:, sl],
                                                              preferred_element_type=F32)
                m_s[:, sl] = m_new

        @pl.when((kb > 0) & (kb < qb))
        def _():
            step(None)

        @pl.when((kb == 0) & (qb > 0))
        def _():
            step(lambda: bf_ref[...])

        @pl.when(kb == qb)
        def _():
            step(lambda: bd_ref[0])
            valid = _row_valid(qb * t, t, lv)
            for hh in range(ATT_HP_FWD):
                sl = slice(hh * HEAD_W, (hh + 1) * HEAD_W)
                acc = acc_s[:, sl]
                l = acc[:, :1]
                o_ref[:, sl] = acc / l * valid
                lse_ref[:, sl] = m_s[:, sl] + jnp.log2(l)

    wd = ATT_HP_FWD * HEAD_W
    qs = pl.BlockSpec((t, wd), lambda h, s, qt, kt: (qt[s], h))
    ks = pl.BlockSpec((t, wd), lambda h, s, qt, kt: (kt[s], h))
    grid_spec = pltpu.PrefetchScalarGridSpec(
        num_scalar_prefetch=2, grid=(MLA_HEADS // ATT_HP_FWD, int(qtab.shape[0])),
        in_specs=[qs, ks, ks, pl.BlockSpec((1, t, t), lambda h, s, qt, kt: (jnp.minimum(qt[s], 1), 0, 0)),
                  pl.BlockSpec((1, t), lambda h, s, qt, kt: (0, 0))],
        out_specs=[qs, qs],
        scratch_shapes=[pltpu.VMEM((t, wd), F32), pltpu.VMEM((t, wd), F32)])
    return pl.pallas_call(
        body, name=name, grid_spec=grid_spec, out_shape=[jax.ShapeDtypeStruct((lp, QW), F32)] * 2,
        compiler_params=_params("parallel", "arbitrary"),
    )(qtab, ktab, q, k, v, ac["diag"], ac["front"])


def _attn_delta(do, o, *, name):
    lp = do.shape[0]
    tm = _pick(lp, ROW_TILE, 8)
    wd = PREP_HEADS * HEAD_W

    def body(do_ref, o_ref, d_ref):
        for g in range(PREP_HEADS):
            sl = slice(g * HEAD_W, (g + 1) * HEAD_W)
            d_ref[:, sl] = jnp.broadcast_to(jnp.sum(do_ref[:, sl] * o_ref[:, sl], axis=-1, keepdims=True),
                                            (tm, HEAD_W))

    spec = pl.BlockSpec((tm, wd), lambda i, j: (i, j))
    return pl.pallas_call(
        body, name=name, grid=(lp // tm, MLA_HEADS // PREP_HEADS), in_specs=[spec, spec], out_specs=spec,
        out_shape=jax.ShapeDtypeStruct((lp, QW), F32), compiler_params=_params("parallel", "parallel"),
    )(do, o)


def _attn_bwd(q, k, v, do, lse, delta, ac, *, name):
    lp = q.shape[0]
    t = _attn_tile(lp)
    nb = lp // t
    rep = t // HEAD_W
    scale = (QK_NOPE + QK_ROPE) ** -0.5
    qtab, ktab = ac["bwd"]

    def body(qt_ref, kt_ref, q_ref, k_ref, v_ref, do_ref, lse_ref, dl_ref, bd_ref, bf_ref, dq_ref, dk_ref, dv_ref,
             dk_s, dv_s):
        step_id = pl.program_id(1)
        qb, kb = qt_ref[step_id], kt_ref[step_id]

        @pl.when(qb == kb)
        def _():
            dk_s[...] = jnp.zeros_like(dk_s)
            dv_s[...] = jnp.zeros_like(dv_s)

        def step(bias):
            b = None if bias is None else bias()
            rows = pl.ds(pl.multiple_of(qb * t, t), t)
            contribs = []
            for hh in range(ATT_HP):
                sl = slice(hh * HEAD_W, (hh + 1) * HEAD_W)
                qv, kv_, vv = q_ref[:, sl], k_ref[:, sl], v_ref[:, sl]
                dof = do_ref[:, sl]
                dov = dof.astype(BF16)
                s = lax.dot_general(qv, kv_, _DIMS["nt"], preferred_element_type=F32)
                if b is not None:
                    s = s + b
                p = jnp.exp2(s - jnp.tile(lse_ref[:, sl], (1, rep)))
                dv_s[:, sl] += lax.dot_general(p.astype(BF16), dov, _DIMS["tn"], preferred_element_type=F32)
                dp = lax.dot_general((dof * scale).astype(BF16), vv, _DIMS["nt"], preferred_element_type=F32)
                ds = (p * (dp - jnp.tile(dl_ref[:, sl] * scale, (1, rep)))).astype(BF16)
                dk_s[:, sl] += lax.dot_general(ds, qv, _DIMS["tn"], preferred_element_type=F32)
                contribs.append(jnp.dot(ds, kv_, preferred_element_type=F32))
            contrib = jnp.concatenate(contribs, axis=1)

            @pl.when(kb == 0)
            def _():
                dq_ref[rows, :] = contrib

            @pl.when(kb > 0)
            def _():
                dq_ref[rows, :] += contrib

        @pl.when((kb > 0) & (kb < qb))
        def _():
            step(None)

        @pl.when((kb == 0) & (qb > 0))
        def _():
            step(lambda: bf_ref[...])

        @pl.when(kb == qb)
        def _():
            step(lambda: bd_ref[0])

        @pl.when(qb == nb - 1)
        def _():
            dk_ref[...] = dk_s[...] * (1.0 / Q_SCALE)
            dv_ref[...] = dv_s[...]

    wd = ATT_HP * HEAD_W
    qs = pl.BlockSpec((t, wd), lambda h, s, qt, kt: (qt[s], h))
    ks = pl.BlockSpec((t, wd), lambda h, s, qt, kt: (kt[s], h))
    dqs = pl.BlockSpec((lp, wd), lambda h, s, qt, kt: (0, h))
    grid_spec = pltpu.PrefetchScalarGridSpec(
        num_scalar_prefetch=2, grid=(MLA_HEADS // ATT_HP, int(qtab.shape[0])),
        in_specs=[qs, ks, ks, qs, qs, qs,
                  pl.BlockSpec((1, t, t), lambda h, s, qt, kt: (jnp.minimum(qt[s], 1), 0, 0)),
                  pl.BlockSpec((1, t), lambda h, s, qt, kt: (0, 0))],
        out_specs=[dqs, ks, ks],
        scratch_shapes=[pltpu.VMEM((t, wd), F32), pltpu.VMEM((t, wd), F32)])
    return pl.pallas_call(
        body, name=name, grid_spec=grid_spec, out_shape=[jax.ShapeDtypeStruct((lp, QW), F32)] * 3,
        compiler_params=_params("arbitrary", "arbitrary"),
    )(qtab, ktab, q, k, v, do, lse, delta, ac["diag"], ac["front"])


HG_UNROLL = 2
HG_LEVELS = (64, 32, 16, 8, 4, 2)
N_LEV = len(HG_LEVELS)


def _hgrn_consts():
    c = HG_CHUNK
    m = np.zeros((N_LEV + 2, c, c), np.float32)
    masks = np.zeros((N_LEV, c, c), np.float32)
    for li, p in enumerate(HG_LEVELS):
        for t in range(c):
            mid = (t // p) * p + p // 2
            if t >= mid:
                m[li, t, mid:t + 1] = 1.0
            else:
                m[li, t, t + 1:mid] = 1.0
            for s in range(c):
                if s // p == t // p and t >= mid and s < mid:
                    masks[li, t, s] = 1.0
    for t in range(c):
        m[N_LEV, t, :t + 1] = 1.0
        m[N_LEV + 1, t, t + 1:] = 1.0
    mall = m.reshape((N_LEV + 2) * c, c)
    return jnp.asarray(mall, BF16), jnp.asarray(mall.T.copy(), BF16), jnp.asarray(masks, F32)


def _split_terms(x):
    hi = x.astype(BF16)
    lo = (x - hi.astype(F32)).astype(BF16)
    return jnp.concatenate([hi, lo], axis=1)


def _sum_terms(e3):
    return e3[:, :HG_D] + e3[:, HG_D:]


def _hgrn_chunk_fwd(hq, hf, hi, lb, valid, mall, masks, st):
    c = HG_CHUNK
    scale = HG_D ** -0.5
    sq = _sigmoid(hq)
    qv = hq * sq
    sg = _sigmoid(hf)
    f = lb + (1.0 - lb) * sg
    fc = jnp.maximum(f, F_MIN)
    lf = jnp.log(fc) * valid
    kv = (1.0 - lb) * (1.0 - sg) * valid
    e = _sum_terms(jnp.dot(mall, _split_terms(lf), preferred_element_type=F32))
    x = jnp.exp(e)
    a = jnp.zeros((c, c), F32)
    qe, ke = [], []
    for l in range(N_LEV):
        xl = x[l * c:(l + 1) * c]
        qe.append(qv * xl)
        ke.append(kv * xl)
        a = a + masks[l] * lax.dot_general(qe[l].astype(BF16), ke[l].astype(BF16), _DIMS["nt"],
                                           preferred_element_type=F32)
    row = lax.broadcasted_iota(jnp.int32, (c, c), 0)
    col = lax.broadcasted_iota(jnp.int32, (c, c), 1)
    a = a + jnp.where(row == col, jnp.sum(qv * kv, axis=-1, keepdims=True), 0.0)
    xb = x[N_LEV * c:(N_LEV + 1) * c]
    qb = qv * xb
    kb = kv * x[(N_LEV + 1) * c:]
    x_last = xb[c - 1:c]
    hib = hi.astype(BF16)
    o = scale * (jnp.dot(a.astype(BF16), hib, preferred_element_type=F32)
                 + lax.dot_general(qb.astype(BF16), st.astype(BF16), _DIMS["nt"], preferred_element_type=F32))
    st_new = st * x_last + lax.dot_general(hib, kb.astype(BF16), _DIMS["tn"], preferred_element_type=F32)
    saved = dict(sq=sq, qv=qv, sg=sg, f=f, fc=fc, kv=kv, x=x, a=a, qe=qe, ke=ke, qb=qb, kb=kb, x_last=x_last)
    return o, st_new, saved


def _split_ride(refs, n_in, n_out, n_scratch, ride):
    ri = len(ride.args) if ride else 0
    ro = len(ride.out_shape) if ride else 0
    a = n_in + ri
    b = a + n_out + ro
    c = b + n_scratch
    return refs[:n_in], refs[a:a + n_out], refs[b:c], refs[n_in:a] + refs[a + n_out:b] + refs[c:]


def _ride_call(ride):
    if ride is None:
        return [], [], [], [], []
    hbm = [HBM_SPEC] * len(ride.args)
    return hbm, list(ride.args), [HBM_SPEC] * len(ride.out_shape), list(ride.out_shape), list(ride.scratch)


def _hgrn_fwd(z, lb, nw, consts, *, zl, lv, name, ride=None):
    lp = z.shape[0]
    tb = _pick(lp, ROW_TILE, HG_CHUNK)
    ncb = tb // HG_CHUNK
    nb = lp // tb
    mall, _, masks = consts
    w = HG_HEADS * HG_D

    def body(*refs):
        ins, outs, (st_s,), ride_refs = _split_ride(refs, 8, 3, 1, ride)
        hq_ref, hf_ref, hi_ref, hg_ref, lb_ref, nw_ref, mall_ref, masks_ref = ins
        o_ref, ob_ref, st_ref = outs
        i = pl.program_id(0)

        @pl.when(i == 0)
        def _():
            st_s[...] = jnp.zeros_like(st_s)
            if ride is not None:
                ride.start(*ride_refs)

        nwv = nw_ref[...]
        mallv, masksv = mall_ref[...], masks_ref[...]

        def chunk(cix, carry):
            r0 = pl.multiple_of(cix * HG_CHUNK, HG_CHUNK)
            rows = pl.ds(r0, HG_CHUNK)
            valid = _row_valid(i * tb + r0, HG_CHUNK, lv)
            for h in range(HG_HEADS):
                sl = slice(h * HG_D, (h + 1) * HG_D)
                st = st_s[h]
                st_ref[h, cix] = st
                o, st_new, _ = _hgrn_chunk_fwd(hq_ref[rows, sl], hf_ref[rows, sl], hi_ref[rows, sl], lb_ref[:, sl],
                                               valid, mallv, masksv, st)
                st_s[h] = st_new
                o_ref[rows, sl] = o
                hg = hg_ref[rows, sl]
                r = lax.rsqrt(jnp.mean(o * o, axis=-1, keepdims=True) + EPS)
                ob_ref[rows, sl] = (o * r * nwv * (hg * _sigmoid(hg))).astype(BF16)
            return carry

        lax.fori_loop(0, ncb, chunk, 0, unroll=HG_UNROLL)

        if ride is not None:
            @pl.when(i == nb - 1)
            def _():
                ride.finish(*ride_refs)

    zb = lambda off: pl.BlockSpec((tb, w), lambda i: (i, off // w))
    full = pl.BlockSpec((tb, w), lambda i: (i, 0))
    const = lambda shape: pl.BlockSpec(shape, lambda i: (0,) * len(shape))
    r_in, r_args, r_out, r_shape, r_scratch = _ride_call(ride)
    outs = pl.pallas_call(
        body, name=name, grid=(nb,),
        in_specs=[zb(zl["hq"]), zb(zl["hf"]), zb(zl["hi"]), zb(zl["hg"]), const((1, w)), const((1, HG_D)),
                  const(mall.shape), const(masks.shape)] + r_in,
        out_specs=[full, full, pl.BlockSpec((HG_HEADS, ncb, HG_D, HG_D), lambda i: (0, i, 0, 0))] + r_out,
        out_shape=[jax.ShapeDtypeStruct((lp, w), F32), jax.ShapeDtypeStruct((lp, w), BF16),
                   jax.ShapeDtypeStruct((HG_HEADS, lp // HG_CHUNK, HG_D, HG_D), F32)] + r_shape,
        scratch_shapes=[pltpu.VMEM((HG_HEADS, HG_D, HG_D), F32)] + r_scratch,
        compiler_params=_params("arbitrary"),
    )(z, z, z, z, lb.reshape(1, w), nw.reshape(1, HG_D), mall, masks, *r_args)
    return outs[0], outs[1], outs[2], list(outs[3:])


def _hgrn_chunk_bwd(hq, hf, hi, hg, o, dout, st, dst, lbv, nwv, valid, mallv, malltv, masksv):
    c = HG_CHUNK
    scale = HG_D ** -0.5
    _, _, sv = _hgrn_chunk_fwd(hq, hf, hi, lbv, valid, mallv, masksv, st)
    shg = _sigmoid(hg)
    r = lax.rsqrt(jnp.mean(o * o, axis=-1, keepdims=True) + EPS)
    don = dout * (hg * shg)
    dhg = dout * (o * r * nwv) * shg * (1.0 + hg * (1.0 - shg))
    dnw = jnp.sum(don * o * r, axis=0, keepdims=True)
    wd = don * nwv
    do = r * wd - o * (r * r * r) * jnp.mean(o * wd, axis=-1, keepdims=True)
    dob16, hib = do.astype(BF16), hi.astype(BF16)
    dst16 = dst.astype(BF16)
    da = scale * lax.dot_general(dob16, hib, _DIMS["nt"], preferred_element_type=F32)
    dv = (scale * lax.dot_general(sv["a"].astype(BF16), dob16, _DIMS["tn"], preferred_element_type=F32)
          + lax.dot_general(sv["kb"].astype(BF16), dst16, _DIMS["nt"], preferred_element_type=F32))
    dkb = jnp.dot(hib, dst16, preferred_element_type=F32)
    dqb = scale * jnp.dot(dob16, st.astype(BF16), preferred_element_type=F32)
    dst_new = dst * sv["x_last"] + scale * lax.dot_general(dob16, sv["qb"].astype(BF16), _DIMS["tn"],
                                                           preferred_element_type=F32)
    dxl = jnp.sum(dst * st, axis=0, keepdims=True)
    x = sv["x"]
    dq = dqb * x[N_LEV * c:(N_LEV + 1) * c]
    dk = dkb * x[(N_LEV + 1) * c:]
    de = []
    for l in range(N_LEV):
        dam = (masksv[l] * da).astype(BF16)
        dqe = jnp.dot(dam, sv["ke"][l].astype(BF16), preferred_element_type=F32)
        dke = lax.dot_general(dam, sv["qe"][l].astype(BF16), _DIMS["tn"], preferred_element_type=F32)
        xl = x[l * c:(l + 1) * c]
        dq = dq + dqe * xl
        dk = dk + dke * xl
        de.append(dqe * sv["qe"][l] + dke * sv["ke"][l])
    dd = scale * jnp.sum(do * hi, axis=-1, keepdims=True)
    dq = dq + dd * sv["kv"]
    dk = dk + dd * sv["qv"]
    last = lax.broadcasted_iota(jnp.int32, (c, 1), 0) == c - 1
    de.append(dqb * sv["qb"] + jnp.where(last, dxl * sv["x_last"], 0.0))
    de.append(dkb * sv["kb"])
    dlf = _sum_terms(jnp.dot(malltv, _split_terms(jnp.concatenate(de, axis=0)), preferred_element_type=F32))
    sg, sq = sv["sg"], sv["sq"]
    df = jnp.where(sv["f"] > F_MIN, dlf * valid / sv["fc"], 0.0)
    dkm = dk * valid
    dhf = (df - dkm) * (1.0 - lbv) * sg * (1.0 - sg)
    dlb = jnp.sum((df - dkm) * (1.0 - sg), axis=0, keepdims=True)
    dhq = dq * sq * (1.0 + hq * (1.0 - sq))
    return dhq, dhf, dv, dhg, dlb, dnw, dst_new


def _hgrn_bwd(z, o_pre, dob, states, lb, nw, consts, *, zl, lv, name, ride=None):
    lp = z.shape[0]
    tb = _pick(lp, ROW_TILE, HG_CHUNK)
    ncb = tb // HG_CHUNK
    nb = lp // tb
    mall, mall_t, masks = consts
    w = HG_HEADS * HG_D
    c = HG_CHUNK

    def body(*refs):
        ins, outs, (dst_s,), ride_refs = _split_ride(refs, 12, 6, 1, ride)
        hq_ref, hf_ref, hi_ref, hg_ref, o_ref, dob_ref, st_ref, lb_ref, nw_ref, mall_ref, mallt_ref, masks_ref = ins
        dhq_ref, dhf_ref, dhi_ref, dhg_ref, dlb_ref, dnw_ref = outs
        i = pl.program_id(0)
        blk = nb - 1 - i

        @pl.when(i == 0)
        def _():
            dst_s[...] = jnp.zeros_like(dst_s)
            dlb_ref[...] = jnp.zeros_like(dlb_ref)
            dnw_ref[...] = jnp.zeros_like(dnw_ref)
            if ride is not None:
                ride.start(*ride_refs)

        nwv = nw_ref[...]
        mallv, malltv, masksv = mall_ref[...], mallt_ref[...], masks_ref[...]

        def chunk(jx, carry):
            cix = ncb - 1 - jx
            r0 = pl.multiple_of(cix * c, c)
            rows = pl.ds(r0, c)
            valid = _row_valid(blk * tb + r0, c, lv)
            for h in range(HG_HEADS):
                sl = slice(h * HG_D, (h + 1) * HG_D)
                dhq, dhf, dhi, dhg, dlb, dnw, dst_new = _hgrn_chunk_bwd(
                    hq_ref[rows, sl], hf_ref[rows, sl], hi_ref[rows, sl], hg_ref[rows, sl], o_ref[rows, sl],
                    dob_ref[rows, sl], st_ref[h, cix], dst_s[h], lb_ref[:, sl], nwv, valid, mallv, malltv, masksv)
                dst_s[h] = dst_new
                dhq_ref[rows, sl] = dhq
                dhf_ref[rows, sl] = dhf
                dhi_ref[rows, sl] = dhi
                dhg_ref[rows, sl] = dhg
                dlb_ref[:, sl] += dlb
                dnw_ref[...] += dnw
            return carry

        lax.fori_loop(0, ncb, chunk, 0, unroll=HG_UNROLL)

        if ride is not None:
            @pl.when(i == nb - 1)
            def _():
                ride.finish(*ride_refs)

    zb = lambda off: pl.BlockSpec((tb, w), lambda i: (nb - 1 - i, off // w))
    full = pl.BlockSpec((tb, w), lambda i: (nb - 1 - i, 0))
    const = lambda shape: pl.BlockSpec(shape, lambda i: (0,) * len(shape))
    r_in, r_args, r_out, r_shape, r_scratch = _ride_call(ride)
    outs = pl.pallas_call(
        body, name=name, grid=(nb,),
        in_specs=[zb(zl["hq"]), zb(zl["hf"]), zb(zl["hi"]), zb(zl["hg"]), full, full,
                  pl.BlockSpec((HG_HEADS, ncb, HG_D, HG_D), lambda i: (0, nb - 1 - i, 0, 0)),
                  const((1, w)), const((1, HG_D)), const(mall.shape), const(mall_t.shape), const(masks.shape)] + r_in,
        out_specs=[full, full, full, full, const((1, w)), const((1, HG_D))] + r_out,
        out_shape=[jax.ShapeDtypeStruct((lp, w), F32)] * 4
                  + [jax.ShapeDtypeStruct((1, w), F32), jax.ShapeDtypeStruct((1, HG_D), F32)] + r_shape,
        scratch_shapes=[pltpu.VMEM((HG_HEADS, HG_D, HG_D), F32)] + r_scratch,
        compiler_params=_params("arbitrary"),
    )(z, z, z, z, o_pre, dob, states, lb.reshape(1, w), nw.reshape(1, HG_D), mall, mall_t, masks, *r_args)
    dhq, dhf, dhi, dhg, dlb, dnw = outs[:6]
    return dhq, dhf, dhi, dhg, dlb[0], dnw[0], list(outs[6:])


def _loss_head(h, w, tpad, *, lv, name):
    lp, d = h.shape
    tm = _pick(lp, ROW_TILE, 8)

    def body(h_ref, w_ref, t_ref, dh_ref, loss_ref, dw_ref):
        i = pl.program_id(0)
        r0 = i * tm + lax.broadcasted_iota(jnp.int32, (tm, 1), 0)
        valid = ((r0 >= ROW_X) & (r0 < lv)).astype(F32)
        xv, wv = h_ref[...], w_ref[...]
        r = lax.rsqrt(jnp.mean(xv * xv, axis=-1, keepdims=True) + EPS)
        e = (xv * r * wv - t_ref[...]) * valid
        dy = e * (1.0 / d)
        wdy = dy * wv
        dh_ref[...] = r * wdy - xv * (r * r * r) * jnp.mean(xv * wdy, axis=-1, keepdims=True)

        @pl.when(i == 0)
        def _():
            loss_ref[...] = jnp.zeros_like(loss_ref)
            dw_ref[...] = jnp.zeros_like(dw_ref)

        loss_ref[...] += 0.5 * jnp.sum(jnp.mean(e * e, axis=-1, keepdims=True), axis=0, keepdims=True)
        dw_ref[...] += jnp.sum(dy * xv * r, axis=0, keepdims=True)

    row = pl.BlockSpec((tm, d), lambda i: (i, 0))
    vec = pl.BlockSpec((1, d), lambda i: (0, 0))
    dh, loss, dw = pl.pallas_call(
        body, name=name, grid=(lp // tm,), in_specs=[row, vec, row],
        out_specs=[row, pl.BlockSpec((8, LANES), lambda i: (0, 0)), vec],
        out_shape=[jax.ShapeDtypeStruct((lp, d), F32), jax.ShapeDtypeStruct((8, LANES), F32),
                   jax.ShapeDtypeStruct((1, d), F32)],
        compiler_params=_params("arbitrary"),
    )(h, w.reshape(1, d), tpad)
    return dh, loss[0, 0], dw[0]


def _adamw(w, g, m, v, *, name):
    shape = w.shape
    cols = shape[-1]
    rows = int(np.prod(shape[:-1])) if len(shape) > 1 else 1
    tr = _pick(rows, 256, 8)
    c1 = 1.0 - ADAM_B1 ** ADAM_STEP
    c2 = 1.0 - ADAM_B2 ** ADAM_STEP

    def body(w_ref, g_ref, m_ref, v_ref, d_ref, nm_ref, nv_ref):
        gv = g_ref[...]
        nm = ADAM_B1 * m_ref[...] + (1.0 - ADAM_B1) * gv
        nv = ADAM_B2 * v_ref[...] + (1.0 - ADAM_B2) * (gv * gv)
        d_ref[...] = -ADAM_LR * ((nm / c1) / (jnp.sqrt(nv / c2) + ADAM_EPS) + ADAM_WD * w_ref[...])
        nm_ref[...] = nm
        nv_ref[...] = nv

    spec = pl.BlockSpec((tr, cols), lambda i: (i, 0))
    r2 = lambda a: a.reshape(rows, cols)
    outs = pl.pallas_call(
        body, name=name, grid=(rows // tr,), in_specs=[spec] * 4, out_specs=[spec] * 3,
        out_shape=[jax.ShapeDtypeStruct((rows, cols), F32)] * 3, compiler_params=_params("parallel"),
    )(r2(w), r2(g), r2(m), r2(v))
    return tuple(o.reshape(shape) for o in outs)


HBM_SPEC = pl.BlockSpec(memory_space=pl.ANY)


def _coords():
    return lax.axis_index("x"), lax.axis_index("y"), lax.axis_index("c")


def _other_chips(x, y):
    return [(1 - x, y), (x, 1 - y), (1 - x, 1 - y)]


def _remote(src, dst, ssem, rsem, dev):
    return pltpu.make_async_remote_copy(src_ref=src, dst_ref=dst, send_sem=ssem, recv_sem=rsem,
                                        device_id=dev, device_id_type=MESH)


def _row_halves(w, c):
    rows = w.shape[-2]
    rh = rows // 2
    align = 8 * 4 // w.dtype.itemsize
    assert rh * 2 == rows and rh % align == 0, w.shape
    return pl.ds(pl.multiple_of(c * rh, align), rh), pl.ds(pl.multiple_of((1 - c) * rh, align), rh)


def _gathered_shapes(ws):
    return [jax.ShapeDtypeStruct((4, *w.shape), w.dtype) for w in ws]


def _own_block(g4s, ws):
    xi, yi, _ = _coords()
    return [lax.dynamic_update_slice(g4, w[None], (2 * xi + yi, 0, 0)) for g4, w in zip(g4s, ws)]


def _gather_chips(ws, *, name):
    n = len(ws)

    def body(*refs):
        w_refs, out_refs, (send_sems, recv_sems) = refs[:n], refs[n:2 * n], refs[2 * n:]
        x, y, c = _coords()
        sib = (x, y, 1 - c)
        chips = _other_chips(x, y)
        sent = []
        for t in range(n):
            half, _ = _row_halves(ws[t], c)
            for j, (px, py) in enumerate(chips):
                cp = _remote(w_refs[t].at[half], out_refs[t].at[2 * x + y, half], send_sems.at[6 * t + j],
                             recv_sems.at[6 * t + j], (px, py, c))
                cp.start()
                sent.append(cp)
        for t in range(n):
            half, _ = _row_halves(ws[t], c)
            for j, (px, py) in enumerate(chips):
                blk = out_refs[t].at[2 * px + py, half]
                _remote(w_refs[t].at[half], blk, send_sems.at[6 * t + j], recv_sems.at[6 * t + j],
                        (px, py, c)).wait_recv()
                fw = _remote(blk, blk, send_sems.at[6 * t + 3 + j], recv_sems.at[6 * t + 3 + j], sib)
                fw.start()
                sent.append(fw)
        for t in range(n):
            _, ohalf = _row_halves(ws[t], c)
            for j, (px, py) in enumerate(chips):
                blk = out_refs[t].at[2 * px + py, ohalf]
                _remote(blk, blk, send_sems.at[6 * t + 3 + j], recv_sems.at[6 * t + 3 + j], sib).wait_recv()
        for cp in sent:
            cp.wait_send()

    g4s = pl.pallas_call(
        body, name=name, in_specs=[HBM_SPEC] * n, out_specs=[HBM_SPEC] * n, out_shape=_gathered_shapes(ws),
        scratch_shapes=[pltpu.SemaphoreType.DMA((6 * n,)), pltpu.SemaphoreType.DMA((6 * n,))],
    )(*ws)
    return _own_block(g4s, ws)


def _swap_halves(gp, *, name):
    n, rows, cols = gp.shape
    rh = rows // 2

    def body(g_ref, out_ref, send_sems, recv_sems):
        x, y, c = _coords()
        sib = (x, y, 1 - c)
        ohalf = pl.ds(pl.multiple_of((1 - c) * rh, 8 * 4 // gp.dtype.itemsize), rh)
        cps = [_remote(g_ref.at[s, ohalf], out_ref.at[s], send_sems.at[s], recv_sems.at[s], sib) for s in range(n)]
        for cp in cps:
            cp.start()
        for cp in cps:
            cp.wait_recv()
        for cp in cps:
            cp.wait_send()

    return pl.pallas_call(
        body, name=name, in_specs=[HBM_SPEC], out_specs=HBM_SPEC,
        out_shape=jax.ShapeDtypeStruct((n, rh, cols), gp.dtype),
        scratch_shapes=[pltpu.SemaphoreType.DMA((n,)), pltpu.SemaphoreType.DMA((n,))],
    )(gp)


def _add_half(gp, got, cidx, *, name):
    n, rows, cols = gp.shape
    rh = rows // 2
    tr = _pick(rh, 512, 16)
    nrb = rh // tr

    def body(c_ref, a_ref, b_ref, o_ref):
        o_ref[...] = (a_ref[...].astype(F32) + b_ref[...].astype(F32)).astype(BF16)

    grid_spec = pltpu.PrefetchScalarGridSpec(
        num_scalar_prefetch=1, grid=(n, nrb),
        in_specs=[pl.BlockSpec((1, tr, cols), lambda s, i, c_ref: (s, c_ref[0] * nrb + i, 0)),
                  pl.BlockSpec((1, tr, cols), lambda s, i, c_ref: (s, i, 0))],
        out_specs=pl.BlockSpec((1, tr, cols), lambda s, i, c_ref: (s, i, 0)))
    return pl.pallas_call(
        body, name=name, grid_spec=grid_spec, out_shape=jax.ShapeDtypeStruct((n, rh, cols), BF16),
        compiler_params=_params("parallel", "parallel"),
    )(cidx, gp, got)


def _scatter_chips(p, *, name):
    _, rh, cols = p.shape

    def body(p_ref, out_ref, send_sems, recv_sems):
        x, y, c = _coords()
        cps = []
        for j, (px, py) in enumerate(_other_chips(x, y)):
            cps.append(_remote(p_ref.at[2 * px + py], out_ref.at[j], send_sems.at[j], recv_sems.at[j], (px, py, c)))
        for cp in cps:
            cp.start()
        for cp in cps:
            cp.wait_recv()
        for cp in cps:
            cp.wait_send()

    return pl.pallas_call(
        body, name=name, in_specs=[HBM_SPEC], out_specs=HBM_SPEC,
        out_shape=jax.ShapeDtypeStruct((3, rh, cols), p.dtype),
        scratch_shapes=[pltpu.SemaphoreType.DMA((3,)), pltpu.SemaphoreType.DMA((3,))],
    )(p)


def _sum_arrivals(p, land, kidx, *, name):
    _, rh, cols = p.shape
    tr = _pick(rh, 512, 16)

    def body(k_ref, a_ref, l_ref, o_ref):
        f = lambda v: v.astype(F32)
        o_ref[...] = ((f(a_ref[0]) + f(l_ref[0])) + f(l_ref[1])) + f(l_ref[2])

    grid_spec = pltpu.PrefetchScalarGridSpec(
        num_scalar_prefetch=1, grid=(rh // tr,),
        in_specs=[pl.BlockSpec((1, tr, cols), lambda i, k_ref: (k_ref[0], i, 0)),
                  pl.BlockSpec((3, tr, cols), lambda i, k_ref: (0, i, 0))],
        out_specs=pl.BlockSpec((tr, cols), lambda i, k_ref: (i, 0)))
    return pl.pallas_call(
        body, name=name, grid_spec=grid_spec, out_shape=jax.ShapeDtypeStruct((rh, cols), F32),
        compiler_params=_params("parallel"),
    )(kidx, p, land)


def _join_halves(q, *, name):
    rh, cols = q.shape

    def body(q_ref, out_ref, send_sem, recv_sem):
        x, y, c = _coords()
        half = pl.ds(pl.multiple_of(c * rh, 8), rh)
        ohalf = pl.ds(pl.multiple_of((1 - c) * rh, 8), rh)
        cp = _remote(q_ref, out_ref.at[half], send_sem, recv_sem, (x, y, 1 - c))
        cp.start()
        _remote(q_ref, out_ref.at[ohalf], send_sem, recv_sem, (x, y, 1 - c)).wait_recv()
        cp.wait_send()

    full = pl.pallas_call(
        body, name=name, in_specs=[HBM_SPEC], out_specs=HBM_SPEC,
        out_shape=jax.ShapeDtypeStruct((2 * rh, cols), q.dtype),
        scratch_shapes=[pltpu.SemaphoreType.DMA, pltpu.SemaphoreType.DMA],
    )(q)
    return lax.dynamic_update_slice(full, q, (lax.axis_index("c") * rh, 0))


def _rs_begin(gp, cidx, *, tag):
    got = _swap_halves(gp, name=f"rs_swap_{tag}")
    return _add_half(gp, got, cidx, name=f"rs_add_{tag}")


def _rs_end(p, land, kidx, *, tag):
    q = _sum_arrivals(p, land, kidx, name=f"rs_sum_{tag}")
    return _join_halves(q, name=f"rs_join_{tag}")


class _ScatterRide:
    def __init__(self, p):
        _, rh, cols = p.shape
        self.args = [p]
        self.out_shape = [jax.ShapeDtypeStruct((3, rh, cols), p.dtype)]
        self.scratch = [pltpu.SemaphoreType.DMA((3,)), pltpu.SemaphoreType.DMA((3,))]

    def _copies(self, p_ref, out_ref, ssem, rsem):
        x, y, c = _coords()
        return [_remote(p_ref.at[2 * px + py], out_ref.at[j], ssem.at[j], rsem.at[j], (px, py, c))
                for j, (px, py) in enumerate(_other_chips(x, y))]

    def start(self, *refs):
        for cp in self._copies(*refs):
            cp.start()

    def finish(self, *refs):
        cps = self._copies(*refs)
        for cp in cps:
            cp.wait_recv()
        for cp in cps:
            cp.wait_send()


class _GatherRide:
    def __init__(self, ws):
        n = len(ws)
        self.args = list(ws)
        self.out_shape = _gathered_shapes(ws)
        self.scratch = [pltpu.SemaphoreType.DMA((3 * n,)), pltpu.SemaphoreType.DMA((3 * n,))]

    def _copies(self, *refs):
        n = len(self.args)
        w_refs, out_refs, (ssem, rsem) = refs[:n], refs[n:2 * n], refs[2 * n:]
        x, y, c = _coords()
        send, recv = [], []
        for t in range(n):
            half, _ = _row_halves(self.args[t], c)
            for j, (px, py) in enumerate(_other_chips(x, y)):
                sems = (ssem.at[3 * t + j], rsem.at[3 * t + j], (px, py, c))
                send.append(_remote(w_refs[t].at[half], out_refs[t].at[2 * x + y, half], *sems))
                recv.append(_remote(w_refs[t].at[half], out_refs[t].at[2 * px + py, half], *sems))
        return send, recv

    def start(self, *refs):
        for cp in self._copies(*refs)[0]:
            cp.start()

    def finish(self, *refs):
        send, recv = self._copies(*refs)
        for cp in recv:
            cp.wait_recv()
        for cp in send:
            cp.wait_send()


def _gather_forward(g4s, *, name):
    n = len(g4s)

    def body(*refs):
        out_refs, (send_sems, recv_sems) = refs[n:2 * n], refs[2 * n:]
        x, y, c = _coords()
        sib = (x, y, 1 - c)
        chips = _other_chips(x, y)
        sent = []
        for t in range(n):
            half, _ = _row_halves(g4s[t], c)
            for j, (px, py) in enumerate(chips):
                blk = out_refs[t].at[2 * px + py, half]
                cp = _remote(blk, blk, send_sems.at[3 * t + j], recv_sems.at[3 * t + j], sib)
                cp.start()
                sent.append(cp)
        for t in range(n):
            _, ohalf = _row_halves(g4s[t], c)
            for j, (px, py) in enumerate(chips):
                blk = out_refs[t].at[2 * px + py, ohalf]
                _remote(blk, blk, send_sems.at[3 * t + j], recv_sems.at[3 * t + j], sib).wait_recv()
        for cp in sent:
            cp.wait_send()

    return pl.pallas_call(
        body, name=name, in_specs=[HBM_SPEC] * n, out_specs=[HBM_SPEC] * n,
        out_shape=[jax.ShapeDtypeStruct(g.shape, g.dtype) for g in g4s],
        input_output_aliases={t: t for t in range(n)},
        scratch_shapes=[pltpu.SemaphoreType.DMA((3 * n,)), pltpu.SemaphoreType.DMA((3 * n,))],
    )(*g4s)


def _allreduce_small(s, *, name):
    rows, cols = s.shape

    def body(s_ref, o_ref, buf, send_sems, recv_sems):
        x, y, c = _coords()
        me = 4 * x + 2 * y + c
        buf[me] = s_ref[...]
        cps = []
        for r in range(1, 8):
            peer = tuple((1 - v) if (r >> sh) & 1 else v for v, sh in ((x, 2), (y, 1), (c, 0)))
            cps.append(_remote(s_ref, buf.at[me], send_sems.at[r - 1], recv_sems.at[r - 1], peer))
        for cp in cps:
            cp.start()
        for cp in cps:
            cp.wait_recv()
        for cp in cps:
            cp.wait_send()
        acc = buf[0]
        for d in range(1, 8):
            acc = acc + buf[d]
        o_ref[...] = acc

    vm = pl.BlockSpec(memory_space=pltpu.VMEM)
    return pl.pallas_call(
        body, name=name, in_specs=[vm], out_specs=vm, out_shape=jax.ShapeDtypeStruct((rows, cols), F32),
        scratch_shapes=[pltpu.VMEM((8, rows, cols), F32), pltpu.SemaphoreType.DMA((7,)),
                        pltpu.SemaphoreType.DMA((7,))],
    )(s)


PACKED = ("ffn1_w_gu", "ffn1_w_down", "w_in", "w_uq", "w_ukv", "w_proj_attn", "w_proj_rec", "w_out",
          "ffn2_w_gu", "ffn2_w_down")
ROW_SHARDED = ("ffn1_w_down", "w_out", "ffn2_w_down")


def _pack_plan(shard_shapes):
    plan, off = {}, 0
    for n in PACKED:
        r, c = shard_shapes[n]
        assert (r * c) % PACK_W == 0
        plan[n] = (off, r * c // PACK_W, (r, c))
        off += r * c // PACK_W
    total = -(-off // 32) * 32
    return plan, total


def _pack(tensors, plan, total, dtype):
    parts = [tensors[n].astype(dtype).reshape(-1, PACK_W) for n in PACKED]
    used = sum(p.shape[0] for p in parts)
    if total > used:
        parts.append(jnp.zeros((total - used, PACK_W), dtype))
    return jnp.concatenate(parts, axis=0)


def _full_weights(g4s):
    out = {}
    for n, g in zip(PACKED, g4s):
        _, r, c = g.shape
        out[n] = g.reshape(4 * r, c) if n in ROW_SHARDED else jnp.swapaxes(g, 0, 1).reshape(r, 4 * c)
    return out


def _pack_grads(grads, plan, total):
    blocks = []
    for s in range(4):
        t = {}
        for n in PACKED:
            _, _, (r, c) = plan[n]
            t[n] = grads[n][s * r:(s + 1) * r] if n in ROW_SHARDED else grads[n][:, s * c:(s + 1) * c]
        blocks.append(_pack(t, plan, total, BF16))
    return jnp.stack(blocks)


def _unpack_shard(p, plan):
    return {n: p[plan[n][0]:plan[n][0] + plan[n][1]].reshape(plan[n][2]) for n in PACKED}


def _swap_cols(w):
    hlf = w.shape[1] // 2
    return jnp.concatenate([-w[:, hlf:], w[:, :hlf]], axis=1)


def _unswap_cols(dw):
    hlf = dw.shape[1] // 2
    return jnp.concatenate([dw[:, hlf:], -dw[:, :hlf]], axis=1)


def _layer_weights(full, d):
    zl = _z_layout(d)
    f = full["ffn1_w_down"].shape[0]
    w_in = full["w_in"]
    o = 0
    cols = {}
    for nm, wd in (("cq", Q_LORA), ("ckv", KV_LORA), ("kpe", QK_ROPE), ("hq", 512), ("hf", 512), ("hi", 512),
                   ("hg", 512), ("ga", d), ("gb", d)):
        cols[nm] = w_in[:, o:o + wd]
        o += wd
    zc = lambda n: jnp.zeros((d, n), BF16)
    win_p = jnp.concatenate(
        [cols["cq"], zc(QK_NOPE), cols["kpe"], zc(32), cols["ckv"], zc(QK_NOPE), _swap_cols(cols["kpe"]), zc(32),
         zc(LANES), cols["ga"], cols["gb"], cols["hq"], cols["hf"], cols["hi"], cols["hg"]], axis=1)
    assert win_p.shape[1] == zl["total"]
    wq = full["w_uq"].reshape(Q_LORA, MLA_HEADS, QK_NOPE + QK_ROPE)
    nope, rope = wq[:, :, :QK_NOPE], wq[:, :, QK_NOPE:]
    z32 = jnp.zeros((Q_LORA, MLA_HEADS, 32), BF16)
    z64 = jnp.zeros((Q_LORA, MLA_HEADS, QK_NOPE), BF16)
    rope_sw = jnp.concatenate([-rope[:, :, 16:], rope[:, :, :16]], axis=2)
    wqa = jnp.concatenate([nope, rope, z32], axis=2).reshape(Q_LORA, QW)
    wqb = jnp.concatenate([z64, rope_sw, z32], axis=2).reshape(Q_LORA, QW)
    wpa = full["w_proj_attn"].reshape(MLA_HEADS, V_HEAD, d)
    wpa_p = jnp.concatenate([jnp.zeros_like(wpa), wpa], axis=1).reshape(QW, d)
    return dict(
        wg1=full["ffn1_w_gu"][:, :f], wu1=full["ffn1_w_gu"][:, f:], wd1=full["ffn1_w_down"],
        wg2=full["ffn2_w_gu"][:, :f], wu2=full["ffn2_w_gu"][:, f:], wd2=full["ffn2_w_down"],
        win=win_p, wq2=jnp.concatenate([wqa, wqb], axis=1), wqa=wqa, wqb=wqb, wkv=full["w_ukv"], wpa=wpa_p,
        wpr=full["w_proj_rec"], wout=full["w_out"])


def _natural_grads(g, d):
    zl = _z_layout(d)
    dwin = g["win"]
    kpe = dwin[:, Z_KPA + QK_NOPE:Z_KPA + QK_NOPE + QK_ROPE] + _unswap_cols(
        dwin[:, Z_KPB + QK_NOPE:Z_KPB + QK_NOPE + QK_ROPE])
    w_in = jnp.concatenate(
        [dwin[:, Z_Q:Z_Q + Q_LORA], dwin[:, Z_KV:Z_KV + KV_LORA], kpe, dwin[:, zl["hq"]:zl["hq"] + 2048],
         dwin[:, zl["ga"]:zl["ga"] + 2 * d]], axis=1)
    qa = g["wqa"].reshape(Q_LORA, MLA_HEADS, HEAD_W)
    qb = g["wqb"].reshape(Q_LORA, MLA_HEADS, HEAD_W)[:, :, QK_NOPE:QK_NOPE + QK_ROPE]
    rope = qa[:, :, QK_NOPE:QK_NOPE + QK_ROPE] + jnp.concatenate([qb[:, :, 16:], -qb[:, :, :16]], axis=2)
    w_uq = jnp.concatenate([qa[:, :, :QK_NOPE], rope], axis=2).reshape(Q_LORA, -1)
    wpa = g["wpa"].reshape(MLA_HEADS, 2 * V_HEAD, d)[:, V_HEAD:].reshape(MLA_HEADS * V_HEAD, d)
    return dict(
        ffn1_w_gu=jnp.concatenate([g["wg1"], g["wu1"]], axis=1), ffn1_w_down=g["wd1"],
        ffn2_w_gu=jnp.concatenate([g["wg2"], g["wu2"]], axis=1), ffn2_w_down=g["wd2"],
        w_in=w_in, w_uq=w_uq, w_ukv=g["wkv"], w_proj_attn=wpa, w_proj_rec=g["wpr"], w_out=g["wout"])


def _rope_tables(lp):
    pos = jnp.maximum(jnp.arange(lp) - FRONT, 0).astype(F32)
    half = QK_ROPE // 2
    inv = ROPE_THETA ** (-jnp.arange(half, dtype=F32) / half)
    ang = pos[:, None] * inv[None, :]
    cos, sin = jnp.cos(ang), jnp.sin(ang)
    cos_t = jnp.concatenate([jnp.ones((lp, QK_NOPE), F32), cos, cos, jnp.zeros((lp, 32), F32)], axis=1)
    sin_t = jnp.concatenate([jnp.zeros((lp, QK_NOPE), F32), sin, sin, jnp.zeros((lp, 32), F32)], axis=1)
    return cos_t, sin_t


def _lower_bounds(raw):
    p = jax.nn.softmax(raw.astype(F32), axis=0)
    return jnp.cumsum(p, axis=0) - p[0:1]


def _ffn_fwd(h, nw, wg, wu, wd, tag):
    a, a_t = _rmsnorm_fwd(h, nw, width=h.shape[1], col_block=0, transposed=True, name=f"norm_{tag}")
    g = _matmul([(a, wg)], "nn", out_dtype=BF16, name=f"gate_{tag}")
    u = _matmul([(a, wu)], "nn", out_dtype=BF16, name=f"up_{tag}")
    act, act_t = _swiglu_fwd(g, u, name=f"swiglu_{tag}")
    out = _matmul([(act, wd)], "nn", res=h, scale=0.5, name=f"down_{tag}")
    return out, dict(h=h, a_t=a_t, g=g, u=u, act_t=act_t)


def _ffn_bwd(dout, sv, nw, wg, wu, wd, lv, tag):
    dact = _matmul([(dout, wd)], "nt", scale=0.5, out_dtype=BF16, name=f"ddown_{tag}")
    dwd = _matmul([(sv["act_t"], dout)], "nn", scale=0.5, name=f"dwdown_{tag}")
    dg, du = _swiglu_bwd(dact, sv["g"], sv["u"], name=f"dswiglu_{tag}")
    dwg = _matmul([(sv["a_t"], dg)], "nn", name=f"dwgate_{tag}")
    dwu = _matmul([(sv["a_t"], du)], "nn", name=f"dwup_{tag}")
    da = _matmul([(dg, wg), (du, wu)], "nt", name=f"dnormed_{tag}")
    dh, dn = _rmsnorm_bwd(sv["h"], nw, da, width=da.shape[1], col_block=0, lv=lv, dres=dout, name=f"dnorm_{tag}")
    return dh, dn, dwg, dwu, dwd


def _layer_fwd(h0, lw, sm, lb, tabs, consts, lv, l, ride=None):
    d = h0.shape[1]
    zl = _z_layout(d)
    cos_t, sin_t = tabs[:2]
    h1, s1 = _ffn_fwd(h0, sm["ffn1_norm"], lw["wg1"], lw["wu1"], lw["wd1"], f"ffn1_{l}")
    um, um_t = _rmsnorm_fwd(h1, sm["mix_norm"], width=d, col_block=0, transposed=True, name=f"norm_mix_{l}")
    z = _matmul([(um, lw["win"])], "nn", name=f"inproj_{l}")
    qn, qn_t = _rmsnorm_fwd(z, sm["q_norm"], width=Q_LORA, col_block=Z_Q // Q_LORA, transposed=True,
                            name=f"norm_q_{l}")
    kvn, kvn_t = _rmsnorm_fwd(z, sm["kv_norm"], width=KV_LORA, col_block=Z_KV // KV_LORA, transposed=True,
                              name=f"norm_kv_{l}")
    q2 = _matmul([(qn, lw["wq2"])], "nn", name=f"uq_{l}")
    kv = _matmul([(kvn, lw["wkv"])], "nn", name=f"ukv_{l}")
    q, k, v = _qkv_prep_fwd(q2, kv, z, cos_t, sin_t, name=f"qkv_{l}")
    o, lse = _attn_fwd(q, k, v, tabs[2], lv=lv, name=f"attn_{l}")
    ya = _matmul([(o, lw["wpa"])], "nn", name=f"proj_attn_{l}")
    o_pre, ob, states, rode = _hgrn_fwd(z, lb, sm["hg_norm"], consts, zl=zl, lv=lv, name=f"hgrn_{l}", ride=ride)
    yb = _matmul([(ob, lw["wpr"])], "nn", name=f"proj_rec_{l}")
    mg, mg_t = _merge_fwd(ya, yb, z, zl=zl, name=f"merge_{l}")
    h2 = _matmul([(mg, lw["wout"])], "nn", res=h1, name=f"out_{l}")
    h3, s2 = _ffn_fwd(h2, sm["ffn2_norm"], lw["wg2"], lw["wu2"], lw["wd2"], f"ffn2_{l}")
    saved = dict(s1=s1, s2=s2, h1=h1, um_t=um_t, z=z, qn_t=qn_t, kvn_t=kvn_t, q=q, k=k, v=v, o=o, lse=lse, ya=ya, yb=yb,
                 o_pre=o_pre, ob=ob, states=states, mg_t=mg_t)
    return h3, saved, rode


def _layer_bwd(dh3, sv, lw, sm, lb, tabs, consts, lv, l, ride=None):
    d = dh3.shape[1]
    lp = dh3.shape[0]
    zl = _z_layout(d)
    cos_t, sin_t = tabs[:2]
    z = sv["z"]
    g = {}
    sg = {}
    dh2, sg["ffn2_norm"], g["wg2"], g["wu2"], g["wd2"] = _ffn_bwd(
        dh3, sv["s2"], sm["ffn2_norm"], lw["wg2"], lw["wu2"], lw["wd2"], lv, f"ffn2_{l}")
    dmg = _matmul([(dh2, lw["wout"])], "nt", name=f"dmerged_{l}")
    g["wout"] = _matmul([(sv["mg_t"], dh2)], "nn", name=f"dwout_{l}")
    dya, dyb, dga, dgb = _merge_bwd(dmg, sv["ya"], sv["yb"], z, zl=zl, name=f"dmerge_{l}")
    doa = _matmul([(dya, lw["wpa"])], "nt", name=f"dattn_out_{l}")
    g["wpa"] = _matmul([(sv["o"], dya)], "tn", name=f"dwproj_attn_{l}")
    dob = _matmul([(dyb, lw["wpr"])], "nt", name=f"drec_out_{l}")
    g["wpr"] = _matmul([(sv["ob"], dyb)], "tn", name=f"dwproj_rec_{l}")
    dhq, dhf, dhi, dhg, dlb, sg["hg_norm"], rode = _hgrn_bwd(
        z, sv["o_pre"], dob, sv["states"], lb, sm["hg_norm"], consts, zl=zl, lv=lv, name=f"dhgrn_{l}", ride=ride)
    delta = _attn_delta(doa, sv["o"], name=f"attn_delta_{l}")
    dq, dk, dv = _attn_bwd(sv["q"], sv["k"], sv["v"], doa, sv["lse"], delta, tabs[2], name=f"dattn_{l}")
    dqa, dqb, dkv, dza, dzb = _qkv_prep_bwd(dq, dk, dv, cos_t, sin_t, name=f"dqkv_{l}")
    dqn = _matmul([(dqa, lw["wqa"]), (dqb, lw["wqb"])], "nt", name=f"dqn_{l}")
    g["wqa"] = _matmul([(sv["qn_t"], dqa)], "nn", name=f"dwqa_{l}")
    g["wqb"] = _matmul([(sv["qn_t"], dqb)], "nn", name=f"dwqb_{l}")
    dkvn = _matmul([(dkv, lw["wkv"])], "nt", name=f"dkvn_{l}")
    g["wkv"] = _matmul([(sv["kvn_t"], dkv)], "nn", name=f"dwkv_{l}")
    dzq, sg["q_norm"] = _rmsnorm_bwd(z, sm["q_norm"], dqn, width=Q_LORA, col_block=Z_Q // Q_LORA, lv=lv,
                                     name=f"dnorm_q_{l}")
    dzkv, sg["kv_norm"] = _rmsnorm_bwd(z, sm["kv_norm"], dkvn, width=KV_LORA, col_block=Z_KV // KV_LORA, lv=lv,
                                       name=f"dnorm_kv_{l}")
    dz = jnp.concatenate([dzq, dza, dzkv, dzb, jnp.zeros((lp, LANES), F32), dga, dgb, dhq, dhf, dhi, dhg],
                         axis=1).astype(BF16)
    dum = _matmul([(dz, lw["win"])], "nt", name=f"dmixed_{l}")
    g["win"] = _matmul([(sv["um_t"], dz)], "nn", name=f"dwin_{l}")
    dh1, sg["mix_norm"] = _rmsnorm_bwd(sv["h1"], sm["mix_norm"], dum, width=d, col_block=0, lv=lv, dres=dh2,
                                       name=f"dnorm_mix_{l}")
    dh0, sg["ffn1_norm"], g["wg1"], g["wu1"], g["wd1"] = _ffn_bwd(
        dh1, sv["s1"], sm["ffn1_norm"], lw["wg1"], lw["wu1"], lw["wd1"], lv, f"ffn1_{l}")
    return dh0, g, sg, dlb, rode


WEIGHTS = ("meta_tokens", "ffn1_norm", "ffn1_w_gu", "ffn1_w_down", "mix_norm", "w_in", "q_norm", "kv_norm", "w_uq",
           "w_ukv", "hg_lb_raw", "hg_norm", "w_proj_attn", "w_proj_rec", "w_out", "ffn2_norm", "ffn2_w_gu",
           "ffn2_w_down", "final_norm")
SMALL = ("ffn1_norm", "mix_norm", "q_norm", "kv_norm", "hg_lb_raw", "hg_norm", "ffn2_norm")


def _small_rows(vals):
    pad = lambda a: jnp.pad(a, ((0, -a.shape[0] % 8), (0, PACK_W - a.shape[1])))
    rows = [pad(vals[n]) for n in SMALL]
    rows.append(pad(vals["final_norm"][None, :]))
    rows.append(pad(vals["meta_tokens"]))
    rows.append(pad(vals["loss"].reshape(1, 1)))
    return jnp.concatenate(rows, axis=0)


def _small_unrows(s, d, widths):
    out, o = {}, 0
    for n in SMALL:
        out[n] = s[o:o + DEPTH, :widths[n]]
        o += -(-DEPTH // 8) * 8
    out["final_norm"] = s[o, :d]
    o += 8
    out["meta_tokens"] = s[o:o + N_META, :d]
    o += -(-N_META // 8) * 8
    out["loss"] = s[o, 0]
    return out


def _step(args):
    x = args["x"][0]
    seq, d = x.shape
    assert d <= PACK_W
    lv = ROW_X + seq
    lp = -(-lv // ROW_TILE) * ROW_TILE
    xi, yi, ci = _coords()
    kidx = (2 * xi + yi).astype(jnp.int32).reshape(1)
    cidx = ci.astype(jnp.int32).reshape(1)
    consts = _hgrn_consts()
    tabs = (*_rope_tables(lp), _attn_consts(lp))

    shard_shapes = {n: args[n].shape[1:] for n in PACKED}
    plan, total = _pack_plan(shard_shapes)
    shards = [[args[n][l].astype(BF16) for n in PACKED] for l in range(DEPTH)]
    mt = args["meta_tokens"]
    mt4 = _gather_chips([mt], name="gather_meta")[0]
    meta = jnp.concatenate(list(mt4), axis=1)

    sm = [{n: args[n][l] for n in SMALL} for l in range(DEPTH)]
    lbs = _lower_bounds(args["hg_lb_raw"])

    h = jnp.concatenate([jnp.zeros((FRONT, d), F32), meta, x, jnp.zeros((lp - lv, d), F32)], axis=0)
    saved, lws = [], []
    g4s = _gather_chips(shards[0], name="gather_0")
    for l in range(DEPTH):
        lws.append(_layer_weights(_full_weights(g4s), d))
        ride = _GatherRide(shards[l + 1]) if l + 1 < DEPTH else None
        h, sv, rode = _layer_fwd(h, lws[l], sm[l], lbs[l], tabs, consts, lv, l, ride)
        saved.append(sv)
        if ride is not None:
            g4s = _own_block(_gather_forward(rode, name=f"gather_fwd_{l + 1}"), shards[l + 1])
    tpad = jnp.pad(args["loss_target"][0], ((ROW_X, lp - lv), (0, 0)))
    dh, loss, dfinal = _loss_head(h, args["final_norm"], tpad, lv=lv, name="loss_head")

    small = {n: [None] * DEPTH for n in SMALL}
    dlbs = [None] * DEPTH
    shard_grads = [None] * DEPTH
    waiting = None
    for l in reversed(range(DEPTH)):
        ride = _ScatterRide(waiting) if waiting is not None else None
        dh, g, sg, dlbs[l], rode = _layer_bwd(dh, saved[l], lws[l], sm[l], lbs[l], tabs, consts, lv, l, ride)
        if ride is not None:
            shard_grads[l + 1] = _unpack_shard(_rs_end(waiting, rode[0], kidx, tag=str(l + 1)), plan)
        for n in sg:
            small[n][l] = sg[n]
        waiting = _rs_begin(_pack_grads(_natural_grads(g, d), plan, total), cidx, tag=str(l))
    land = _scatter_chips(waiting, name="rs_scatter_0")
    shard_grads[0] = _unpack_shard(_rs_end(waiting, land, kidx, tag="0"), plan)

    _, lb_vjp = jax.vjp(_lower_bounds, args["hg_lb_raw"])
    small_vals = {n: jnp.stack(small[n]) for n in SMALL if n != "hg_lb_raw"}
    small_vals["hg_lb_raw"] = lb_vjp(jnp.stack(dlbs))[0]
    small_vals["final_norm"] = dfinal
    small_vals["meta_tokens"] = dh[FRONT:ROW_X]
    small_vals["loss"] = loss
    widths = {n: args[n].shape[1] for n in SMALL}
    tot = _small_unrows(_allreduce_small(_small_rows(small_vals), name="allreduce_small"), d, widths)

    grads = {n: jnp.stack([shard_grads[l][n] for l in range(DEPTH)]) for n in PACKED}
    for n in SMALL:
        grads[n] = tot[n]
    grads["final_norm"] = tot["final_norm"]
    mcols = mt.shape[1]
    grads["meta_tokens"] = lax.dynamic_slice_in_dim(tot["meta_tokens"], (2 * xi + yi) * mcols, mcols, axis=1)
    grad_x = dh[ROW_X:lv][None]

    delta, new_m, new_v = {}, {}, {}
    for n in WEIGHTS:
        delta[n], new_m[n], new_v[n] = _adamw(args[n], grads[n], args["m_" + n], args["v_" + n], name=f"adamw_{n}")
    return (tot["loss"], grad_x, *[grads[n] for n in WEIGHTS], *[delta[n] for n in WEIGHTS],
            *[new_m[n] for n in WEIGHTS], *[new_v[n] for n in WEIGHTS])


def kernel(x, meta_tokens, ffn1_norm, ffn1_w_gu, ffn1_w_down, mix_norm, w_in, q_norm, kv_norm, w_uq, w_ukv, hg_lb_raw, hg_norm, w_proj_attn, w_proj_rec, w_out, ffn2_norm, ffn2_w_gu, ffn2_w_down, final_norm, loss_target, m_meta_tokens, m_ffn1_norm, m_ffn1_w_gu, m_ffn1_w_down, m_mix_norm, m_w_in, m_q_norm, m_kv_norm, m_w_uq, m_w_ukv, m_hg_lb_raw, m_hg_norm, m_w_proj_attn, m_w_proj_rec, m_w_out, m_ffn2_norm, m_ffn2_w_gu, m_ffn2_w_down, m_final_norm, v_meta_tokens, v_ffn1_norm, v_ffn1_w_gu, v_ffn1_w_down, v_mix_norm, v_w_in, v_q_norm, v_kv_norm, v_w_uq, v_w_ukv, v_hg_lb_raw, v_hg_norm, v_w_proj_attn, v_w_proj_rec, v_w_out, v_ffn2_norm, v_ffn2_w_gu, v_ffn2_w_down, v_final_norm):
    return _step(dict(locals()))
```

```python
import functools
import math

import numpy as np
import jax
import jax.numpy as jnp
from jax import lax
from jax.experimental import pallas as pl
from jax.experimental.pallas import tpu as pltpu

F32 = jnp.float32
BF16 = jnp.bfloat16

N_META = 16
MLA_HEADS = 8
Q_LORA = 384
KV_LORA = 256
QK_NOPE = 64
QK_ROPE = 32
V_HEAD = 64
ROPE_THETA = 10000.0
HG_HEADS = 4
HG_D = 128
HG_CHUNK = 64
EPS = 1e-6
NEG_BIG = -1e30
F_MIN = 1e-20
DEPTH = 4

ADAM_LR = 0.001
ADAM_B1 = 0.9
ADAM_B2 = 0.999
ADAM_EPS = 1e-08
ADAM_WD = 0.01
ADAM_STEP = 10

LANES = 128
FRONT = (-N_META) % HG_CHUNK
ROW_X = FRONT + N_META
ROW_TILE = 640
HEAD_W = 128
QW = MLA_HEADS * HEAD_W
PREP_HEADS = 4
VMEM_LIMIT = 56 * 1024 * 1024
MATMUL_VMEM = 42 * 1024 * 1024
PACK_W = 1024
MESH = pl.DeviceIdType.MESH

Z_Q, Z_KPA, Z_KV, Z_KPB, Z_PAD, Z_GA = 0, 384, 512, 768, 896, 1024


def _z_layout(d):
    ga = Z_GA
    gb = ga + d
    hq = gb + d
    hf = hq + 512
    hi = hf + 512
    hg = hi + 512
    return dict(ga=ga, gb=gb, hq=hq, hf=hf, hi=hi, hg=hg, total=hg + 512)


def _pick(dim, cap, mult=LANES):
    if dim <= cap:
        return dim
    best = None
    for t in range(mult, cap + 1, mult):
        if dim % t == 0:
            best = t
    assert best is not None, (dim, cap, mult)
    return best


def _params(*sem):
    return pltpu.CompilerParams(dimension_semantics=sem, vmem_limit_bytes=VMEM_LIMIT)


def _sigmoid(x):
    return 1.0 / (1.0 + jnp.exp(-x))


def _row_valid(row0, n, lv):
    r = row0 + lax.broadcasted_iota(jnp.int32, (n, 1), 0)
    return ((r >= FRONT) & (r < lv)).astype(F32)


_DIMS = {"nn": (((1,), (0,)), ((), ())), "nt": (((1,), (1,)), ((), ())), "tn": (((0,), (0,)), ((), ()))}


def _matmul(pairs, mode, *, name, out_dtype=F32, res=None, scale=1.0):
    a0, b0 = pairs[0]
    if mode == "nn":
        (m, k), n = a0.shape, b0.shape[1]
    elif mode == "nt":
        (m, k), n = a0.shape, b0.shape[0]
    else:
        (k, m), n = a0.shape, b0.shape[1]
    if mode == "tn":
        tm, tn, tk = _pick(m, 1024), _pick(n, 1408), _pick(k, ROW_TILE, 8)
    else:
        if k > m:
            tm, tn, kcap = _pick(m, 1408, 16), _pick(n, 1408), 1664
        else:
            tm, tn, kcap = _pick(m, ROW_TILE, 8), _pick(n, 2816), 2816
        out_b = jnp.dtype(out_dtype).itemsize
        per_k = len(pairs) * 2 * (tm * a0.dtype.itemsize + tn * b0.dtype.itemsize)
        fixed = tm * tn * (2 * out_b + 4 + (8 if res is not None else 0))
        tk = _pick(k, kcap)
        while tk > LANES and fixed + per_k * tk > MATMUL_VMEM:
            tk = _pick(k, tk - LANES)
    nk = k // tk
    npair = len(pairs)
    dims = _DIMS[mode]

    def body(*refs):
        ins = refs[:2 * npair]
        pos = 2 * npair
        res_ref = None
        if res is not None:
            res_ref = refs[pos]
            pos += 1
        o_ref = refs[pos]
        kk = pl.program_id(2)

        part = None
        for p in range(npair):
            a = ins[2 * p][...].astype(BF16)
            b = ins[2 * p + 1][...].astype(BF16)
            d = lax.dot_general(a, b, dims, preferred_element_type=F32)
            part = d if part is None else part + d

        def finish(r):
            if scale != 1.0:
                r = r * scale
            if res_ref is not None:
                r = r + res_ref[...]
            o_ref[...] = r.astype(out_dtype)

        if nk == 1:
            finish(part)
            return
        acc = refs[pos + 1]

        @pl.when(kk == 0)
        def _():
            acc[...] = part

        @pl.when(kk > 0)
        def _():
            acc[...] += part

        @pl.when(kk == nk - 1)
        def _():
            finish(acc[...])

    if mode == "nn":
        a_spec = pl.BlockSpec((tm, tk), lambda i, j, q: (i, q))
        b_spec = pl.BlockSpec((tk, tn), lambda i, j, q: (q, j))
    elif mode == "nt":
        a_spec = pl.BlockSpec((tm, tk), lambda i, j, q: (i, q))
        b_spec = pl.BlockSpec((tn, tk), lambda i, j, q: (j, q))
    else:
        a_spec = pl.BlockSpec((tk, tm), lambda i, j, q: (q, i))
        b_spec = pl.BlockSpec((tk, tn), lambda i, j, q: (q, j))
    o_spec = pl.BlockSpec((tm, tn), lambda i, j, q: (i, j))
    in_specs, args = [], []
    for a, b in pairs:
        in_specs += [a_spec, b_spec]
        args += [a, b]
    if res is not None:
        in_specs.append(o_spec)
        args.append(res)
    return pl.pallas_call(
        body, name=name, grid=(m // tm, n // tn, nk), in_specs=in_specs, out_specs=o_spec,
        out_shape=jax.ShapeDtypeStruct((m, n), out_dtype),
        scratch_shapes=[pltpu.VMEM((tm, tn), F32)] if nk > 1 else [],
        compiler_params=_params("parallel", "parallel", "arbitrary"),
    )(*args)


def _rmsnorm_fwd(x, w, *, width, col_block, name, transposed=False):
    lp = x.shape[0]
    tm = _pick(lp, ROW_TILE, 8)

    def body(x_ref, w_ref, o_ref, *ot_ref):
        xv = x_ref[...]
        r = lax.rsqrt(jnp.mean(xv * xv, axis=-1, keepdims=True) + EPS)
        y = xv * r * w_ref[...]
        o_ref[...] = y.astype(BF16)
        if transposed:
            ot_ref[0][...] = y.T.astype(BF16)

    out_specs = [pl.BlockSpec((tm, width), lambda i: (i, 0))]
    out_shape = [jax.ShapeDtypeStruct((lp, width), BF16)]
    if transposed:
        out_specs.append(pl.BlockSpec((width, tm), lambda i: (0, i)))
        out_shape.append(jax.ShapeDtypeStruct((width, lp), BF16))
    outs = pl.pallas_call(
        body, name=name, grid=(lp // tm,),
        in_specs=[pl.BlockSpec((tm, width), lambda i: (i, col_block)), pl.BlockSpec((1, width), lambda i: (0, 0))],
        out_specs=out_specs, out_shape=out_shape, compiler_params=_params("parallel"),
    )(x, w.reshape(1, width))
    return tuple(outs) if transposed else outs[0]


def _rmsnorm_bwd(x, w, dy, *, width, col_block, lv, name, dres=None):
    lp = x.shape[0]
    tm = _pick(lp, ROW_TILE, 8)

    def body(*refs):
        if dres is None:
            x_ref, w_ref, dy_ref, dx_ref, dw_ref = refs
            dres_ref = None
        else:
            x_ref, w_ref, dy_ref, dres_ref, dx_ref, dw_ref = refs
        i = pl.program_id(0)
        xv = x_ref[...]
        dyv = dy_ref[...] * _row_valid(i * tm, tm, lv)
        r = lax.rsqrt(jnp.mean(xv * xv, axis=-1, keepdims=True) + EPS)
        wdy = dyv * w_ref[...]
        dx = r * wdy - xv * (r * r * r) * jnp.mean(xv * wdy, axis=-1, keepdims=True)
        if dres_ref is not None:
            dx = dx + dres_ref[...]
        dx_ref[...] = dx

        @pl.when(i == 0)
        def _():
            dw_ref[...] = jnp.zeros_like(dw_ref)

        dw_ref[...] += jnp.sum(dyv * xv * r, axis=0, keepdims=True)

    row = pl.BlockSpec((tm, width), lambda i: (i, 0))
    in_specs = [pl.BlockSpec((tm, width), lambda i: (i, col_block)), pl.BlockSpec((1, width), lambda i: (0, 0)), row]
    args = [x, w.reshape(1, width), dy]
    if dres is not None:
        in_specs.append(row)
        args.append(dres)
    dx, dw = pl.pallas_call(
        body, name=name, grid=(lp // tm,), in_specs=in_specs,
        out_specs=[row, pl.BlockSpec((1, width), lambda i: (0, 0))],
        out_shape=[jax.ShapeDtypeStruct((lp, width), F32), jax.ShapeDtypeStruct((1, width), F32)],
        compiler_params=_params("arbitrary"),
    )(*args)
    return dx, dw[0]


def _swiglu_fwd(g, u, *, name):
    lp, f = g.shape
    tm, tf = _pick(lp, ROW_TILE, 8), _pick(f, 1408)

    def body(g_ref, u_ref, o_ref, ot_ref):
        gv = g_ref[...].astype(F32)
        act = gv * _sigmoid(gv) * u_ref[...].astype(F32)
        o_ref[...] = act.astype(BF16)
        ot_ref[...] = act.T.astype(BF16)

    spec = pl.BlockSpec((tm, tf), lambda i, j: (i, j))
    return pl.pallas_call(
        body, name=name, grid=(lp // tm, f // tf), in_specs=[spec, spec],
        out_specs=[spec, pl.BlockSpec((tf, tm), lambda i, j: (j, i))],
        out_shape=[jax.ShapeDtypeStruct((lp, f), BF16), jax.ShapeDtypeStruct((f, lp), BF16)],
        compiler_params=_params("parallel", "parallel"),
    )(g, u)


def _swiglu_bwd(dact, g, u, *, name):
    lp, f = g.shape
    tm, tf = _pick(lp, ROW_TILE, 8), _pick(f, 1408)

    def body(d_ref, g_ref, u_ref, dg_ref, du_ref):
        gv, dv = g_ref[...].astype(F32), d_ref[...].astype(F32)
        s = _sigmoid(gv)
        dg_ref[...] = (dv * u_ref[...].astype(F32) * s * (1.0 + gv * (1.0 - s))).astype(BF16)
        du_ref[...] = (dv * gv * s).astype(BF16)

    spec = pl.BlockSpec((tm, tf), lambda i, j: (i, j))
    return pl.pallas_call(
        body, name=name, grid=(lp // tm, f // tf), in_specs=[spec, spec, spec], out_specs=[spec, spec],
        out_shape=[jax.ShapeDtypeStruct((lp, f), BF16)] * 2, compiler_params=_params("parallel", "parallel"),
    )(dact, g, u)


def _merge_fwd(ya, yb, z, *, zl, name):
    lp, d = ya.shape
    tm, td = _pick(lp, ROW_TILE, 8), _pick(d, 512)
    oa, ob = zl["ga"] // td, zl["gb"] // td

    def body(ya_ref, yb_ref, ga_ref, gb_ref, o_ref, ot_ref):
        mg = _sigmoid(ga_ref[...]) * ya_ref[...].astype(F32) + _sigmoid(gb_ref[...]) * yb_ref[...].astype(F32)
        o_ref[...] = mg.astype(BF16)
        ot_ref[...] = mg.T.astype(BF16)

    spec = pl.BlockSpec((tm, td), lambda i, j: (i, j))
    return pl.pallas_call(
        body, name=name, grid=(lp // tm, d // td),
        in_specs=[spec, spec, pl.BlockSpec((tm, td), lambda i, j: (i, oa + j)),
                  pl.BlockSpec((tm, td), lambda i, j: (i, ob + j))],
        out_specs=[spec, pl.BlockSpec((td, tm), lambda i, j: (j, i))],
        out_shape=[jax.ShapeDtypeStruct((lp, d), BF16), jax.ShapeDtypeStruct((d, lp), BF16)],
        compiler_params=_params("parallel", "parallel"),
    )(ya, yb, z, z)


def _merge_bwd(dmg, ya, yb, z, *, zl, name):
    lp, d = ya.shape
    tm, td = _pick(lp, ROW_TILE, 8), _pick(d, 512)
    oa, ob = zl["ga"] // td, zl["gb"] // td

    def body(d_ref, ya_ref, yb_ref, ga_ref, gb_ref, dya_ref, dyb_ref, dga_ref, dgb_ref):
        dv = d_ref[...].astype(F32)
        sa, sb = _sigmoid(ga_ref[...]), _sigmoid(gb_ref[...])
        dya_ref[...] = (dv * sa).astype(BF16)
        dyb_ref[...] = (dv * sb).astype(BF16)
        dga_ref[...] = dv * ya_ref[...].astype(F32) * sa * (1.0 - sa)
        dgb_ref[...] = dv * yb_ref[...].astype(F32) * sb * (1.0 - sb)

    spec = pl.BlockSpec((tm, td), lambda i, j: (i, j))
    return pl.pallas_call(
        body, name=name, grid=(lp // tm, d // td),
        in_specs=[spec, spec, spec, pl.BlockSpec((tm, td), lambda i, j: (i, oa + j)),
                  pl.BlockSpec((tm, td), lambda i, j: (i, ob + j))],
        out_specs=[spec] * 4,
        out_shape=[jax.ShapeDtypeStruct((lp, d), BF16)] * 2 + [jax.ShapeDtypeStruct((lp, d), F32)] * 2,
        compiler_params=_params("parallel", "parallel"),
    )(dmg, ya, yb, z, z)


def _qkv_prep_fwd(q2, kv, z, cos_t, sin_t, *, name):
    lp = q2.shape[0]
    tm = _pick(lp, ROW_TILE, 8)
    h = MLA_HEADS
    wd = PREP_HEADS * HEAD_W

    def body(qa_ref, qb_ref, kv_ref, za_ref, zb_ref, c_ref, s_ref, q_ref, k_ref, v_ref):
        c, s = c_ref[...], s_ref[...]
        lane = lax.broadcasted_iota(jnp.int32, (tm, HEAD_W), 1)
        kr = jnp.where(lane >= QK_NOPE, za_ref[...] * c + zb_ref[...] * s, 0.0)
        for g in range(PREP_HEADS):
            sl = slice(g * HEAD_W, (g + 1) * HEAD_W)
            q_ref[:, sl] = ((qa_ref[:, sl] * c + qb_ref[:, sl] * s) * Q_SCALE).astype(BF16)
            kvv = kv_ref[:, sl]
            k_ref[:, sl] = (jnp.where(lane < QK_NOPE, kvv, 0.0) + kr).astype(BF16)
            v_ref[:, sl] = jnp.where(lane >= QK_NOPE, kvv, jnp.where(lane == 0, 1.0, 0.0)).astype(BF16)

    blk = lambda w, f: pl.BlockSpec((tm, w), f)
    out = blk(wd, lambda i, j: (i, j))
    return pl.pallas_call(
        body, name=name, grid=(lp // tm, h // PREP_HEADS),
        in_specs=[blk(wd, lambda i, j: (i, j)), blk(wd, lambda i, j: (i, h // PREP_HEADS + j)),
                  blk(wd, lambda i, j: (i, j)),
                  blk(HEAD_W, lambda i, j: (i, Z_KPA // HEAD_W)), blk(HEAD_W, lambda i, j: (i, Z_KPB // HEAD_W)),
                  blk(HEAD_W, lambda i, j: (i, 0)), blk(HEAD_W, lambda i, j: (i, 0))],
        out_specs=[out, out, out], out_shape=[jax.ShapeDtypeStruct((lp, QW), BF16)] * 3,
        compiler_params=_params("parallel", "parallel"),
    )(q2, q2, kv, z, z, cos_t, sin_t)


def _qkv_prep_bwd(dq, dk, dv, cos_t, sin_t, *, name):
    lp = dq.shape[0]
    tm = _pick(lp, ROW_TILE, 8)
    h = MLA_HEADS
    wd = PREP_HEADS * HEAD_W

    def body(dq_ref, dk_ref, dv_ref, c_ref, s_ref, dqa_ref, dqb_ref, dkv_ref, dza_ref, dzb_ref):
        j = pl.program_id(1)
        c, s = c_ref[...], s_ref[...]
        lane = lax.broadcasted_iota(jnp.int32, (tm, HEAD_W), 1)
        dkr = jnp.zeros((tm, HEAD_W), F32)
        for g in range(PREP_HEADS):
            sl = slice(g * HEAD_W, (g + 1) * HEAD_W)
            dqv, dkv_ = dq_ref[:, sl], dk_ref[:, sl]
            dqa_ref[:, sl] = (dqv * c).astype(BF16)
            dqb_ref[:, sl] = (dqv * s).astype(BF16)
            dkv_ref[:, sl] = jnp.where(lane < QK_NOPE, dkv_, dv_ref[:, sl]).astype(BF16)
            dkr = dkr + jnp.where(lane >= QK_NOPE, dkv_, 0.0)

        @pl.when(j == 0)
        def _():
            dza_ref[...] = jnp.zeros_like(dza_ref)
            dzb_ref[...] = jnp.zeros_like(dzb_ref)

        dza_ref[...] += dkr * c
        dzb_ref[...] += dkr * s

    blk = lambda w, f: pl.BlockSpec((tm, w), f)
    per_head, shared = blk(wd, lambda i, j: (i, j)), blk(HEAD_W, lambda i, j: (i, 0))
    return pl.pallas_call(
        body, name=name, grid=(lp // tm, h // PREP_HEADS),
        in_specs=[per_head, per_head, per_head, shared, shared],
        out_specs=[per_head, per_head, per_head, shared, shared],
        out_shape=[jax.ShapeDtypeStruct((lp, QW), BF16)] * 3 + [jax.ShapeDtypeStruct((lp, HEAD_W), F32)] * 2,
        compiler_params=_params("parallel", "arbitrary"),
    )(dq, dk, dv, cos_t, sin_t)


def _attn_tile(lp):
    return _pick(lp, ROW_TILE, LANES)


Q_SCALE = (QK_NOPE + QK_ROPE) ** -0.5 * math.log2(math.e)
ATT_HP = 2
ATT_HP_FWD = 4


def _attn_consts(lp):
    t = _attn_tile(lp)
    nb = lp // t
    r = np.arange(t)
    causal = np.where(r[None, :] <= r[:, None], 0.0, NEG_BIG).astype(np.float32)
    front = np.where(r >= FRONT, 0.0, NEG_BIG).astype(np.float32)[None, :]
    diag = np.stack([np.minimum(causal, front), causal])
    qmaj = [(i, j) for i in range(nb) for j in range(i + 1)]
    kmaj = [(i, j) for j in range(nb) for i in range(j, nb)]
    tab = lambda pairs, c: jnp.asarray([p[c] for p in pairs], jnp.int32)
    return dict(diag=jnp.asarray(diag), front=jnp.asarray(front),
                fwd=(tab(qmaj, 0), tab(qmaj, 1)), bwd=(tab(kmaj, 0), tab(kmaj, 1)))


def _attn_fwd(q, k, v, ac, *, lv, name):
    lp = q.shape[0]
    t = _attn_tile(lp)
    nb = lp // t
    rep = t // HEAD_W
    qtab, ktab = ac["fwd"]

    def body(qt_ref, kt_ref, q_ref, k_ref, v_ref, bd_ref, bf_ref, o_ref, lse_ref, m_s, acc_s):
        step_id = pl.program_id(1)
        qb, kb = qt_ref[step_id], kt_ref[step_id]

        @pl.when(kb == 0)
        def _():
            m_s[...] = jnp.full_like(m_s, NEG_BIG)
            acc_s[...] = jnp.zeros_like(acc_s)

        def step(bias):
            b = None if bias is None else bias()
            for hh in range(ATT_HP_FWD):
                sl = slice(hh * HEAD_W, (hh + 1) * HEAD_W)
                s = lax.dot_general(q_ref[:, sl], k_ref[:, sl], _DIMS["nt"], preferred_element_type=F32)
                if b is not None:
                    s = s + b
                m_prev = m_s[:, sl]
                m_new = jnp.maximum(m_prev, jnp.max(s, axis=-1, keepdims=True))
                alpha = jnp.exp2(m_prev - m_new)
                p = jnp.exp2(s - jnp.tile(m_new, (1, rep)))
                acc_s[:, sl] = alpha * acc_s[:, sl] + jnp.dot(p.astype(BF16), v_ref[:, sl],
                                                              preferred_element_type=F32)
                m_s[:, sl] = m_new

        @pl.when((kb > 0) & (kb < qb))
        def _():
            step(None)

        @pl.when((kb == 0) & (qb > 0))
        def _():
            step(lambda: bf_ref[...])

        @pl.when(kb == qb)
        def _():
            step(lambda: bd_ref[0])
            valid = _row_valid(qb * t, t, lv)
            for hh in range(ATT_HP_FWD):
                sl = slice(hh * HEAD_W, (hh + 1) * HEAD_W)
                acc = acc_s[:, sl]
                l = acc[:, :1]
                o_ref[:, sl] = acc / l * valid
                lse_ref[:, sl] = m_s[:, sl] + jnp.log2(l)

    wd = ATT_HP_FWD * HEAD_W
    qs = pl.BlockSpec((t, wd), lambda h, s, qt, kt: (qt[s], h))
    ks = pl.BlockSpec((t, wd), lambda h, s, qt, kt: (kt[s], h))
    grid_spec = pltpu.PrefetchScalarGridSpec(
        num_scalar_prefetch=2, grid=(MLA_HEADS // ATT_HP_FWD, int(qtab.shape[0])),
        in_specs=[qs, ks, ks, pl.BlockSpec((1, t, t), lambda h, s, qt, kt: (jnp.minimum(qt[s], 1), 0, 0)),
                  pl.BlockSpec((1, t), lambda h, s, qt, kt: (0, 0))],
        out_specs=[qs, qs],
        scratch_shapes=[pltpu.VMEM((t, wd), F32), pltpu.VMEM((t, wd), F32)])
    return pl.pallas_call(
        body, name=name, grid_spec=grid_spec, out_shape=[jax.ShapeDtypeStruct((lp, QW), F32)] * 2,
        compiler_params=_params("parallel", "arbitrary"),
    )(qtab, ktab, q, k, v, ac["diag"], ac["front"])


def _attn_delta(do, o, *, name):
    lp = do.shape[0]
    tm = _pick(lp, ROW_TILE, 8)
    wd = PREP_HEADS * HEAD_W

    def body(do_ref, o_ref, d_ref):
        for g in range(PREP_HEADS):
            sl = slice(g * HEAD_W, (g + 1) * HEAD_W)
            d_ref[:, sl] = jnp.broadcast_to(jnp.sum(do_ref[:, sl] * o_ref[:, sl], axis=-1, keepdims=True),
                                            (tm, HEAD_W))

    spec = pl.BlockSpec((tm, wd), lambda i, j: (i, j))
    return pl.pallas_call(
        body, name=name, grid=(lp // tm, MLA_HEADS // PREP_HEADS), in_specs=[spec, spec], out_specs=spec,
        out_shape=jax.ShapeDtypeStruct((lp, QW), F32), compiler_params=_params("parallel", "parallel"),
    )(do, o)


def _attn_bwd(q, k, v, do, lse, delta, ac, *, name):
    lp = q.shape[0]
    t = _attn_tile(lp)
    nb = lp // t
    rep = t // HEAD_W
    scale = (QK_NOPE + QK_ROPE) ** -0.5
    qtab, ktab = ac["bwd"]

    def body(qt_ref, kt_ref, q_ref, k_ref, v_ref, do_ref, lse_ref, dl_ref, bd_ref, bf_ref, dq_ref, dk_ref, dv_ref,
             dk_s, dv_s):
        step_id = pl.program_id(1)
        qb, kb = qt_ref[step_id], kt_ref[step_id]

        @pl.when(qb == kb)
        def _():
            dk_s[...] = jnp.zeros_like(dk_s)
            dv_s[...] = jnp.zeros_like(dv_s)

        def step(bias):
            b = None if bias is None else bias()
            rows = pl.ds(pl.multiple_of(qb * t, t), t)
            contribs = []
            for hh in range(ATT_HP):
                sl = slice(hh * HEAD_W, (hh + 1) * HEAD_W)
                qv, kv_, vv = q_ref[:, sl], k_ref[:, sl], v_ref[:, sl]
                dof = do_ref[:, sl]
                dov = dof.astype(BF16)
                s = lax.dot_general(qv, kv_, _DIMS["nt"], preferred_element_type=F32)
                if b is not None:
                    s = s + b
                p = jnp.exp2(s - jnp.tile(lse_ref[:, sl], (1, rep)))
                dv_s[:, sl] += lax.dot_general(p.astype(BF16), dov, _DIMS["tn"], preferred_element_type=F32)
                dp = lax.dot_general((dof * scale).astype(BF16), vv, _DIMS["nt"], preferred_element_type=F32)
                ds = (p * (dp - jnp.tile(dl_ref[:, sl] * scale, (1, rep)))).astype(BF16)
                dk_s[:, sl] += lax.dot_general(ds, qv, _DIMS["tn"], preferred_element_type=F32)
                contribs.append(jnp.dot(ds, kv_, preferred_element_type=F32))
            contrib = jnp.concatenate(contribs, axis=1)

            @pl.when(kb == 0)
            def _():
                dq_ref[rows, :] = contrib

            @pl.when(kb > 0)
            def _():
                dq_ref[rows, :] += contrib

        @pl.when((kb > 0) & (kb < qb))
        def _():
            step(None)

        @pl.when((kb == 0) & (qb > 0))
        def _():
            step(lambda: bf_ref[...])

        @pl.when(kb == qb)
        def _():
            step(lambda: bd_ref[0])

        @pl.when(qb == nb - 1)
        def _():
            dk_ref[...] = dk_s[...] * (1.0 / Q_SCALE)
            dv_ref[...] = dv_s[...]

    wd = ATT_HP * HEAD_W
    qs = pl.BlockSpec((t, wd), lambda h, s, qt, kt: (qt[s], h))
    ks = pl.BlockSpec((t, wd), lambda h, s, qt, kt: (kt[s], h))
    dqs = pl.BlockSpec((lp, wd), lambda h, s, qt, kt: (0, h))
    grid_spec = pltpu.PrefetchScalarGridSpec(
        num_scalar_prefetch=2, grid=(MLA_HEADS // ATT_HP, int(qtab.shape[0])),
        in_specs=[qs, ks, ks, qs, qs, qs,
                  pl.BlockSpec((1, t, t), lambda h, s, qt, kt: (jnp.minimum(qt[s], 1), 0, 0)),
                  pl.BlockSpec((1, t), lambda h, s, qt, kt: (0, 0))],
        out_specs=[dqs, ks, ks],
        scratch_shapes=[pltpu.VMEM((t, wd), F32), pltpu.VMEM((t, wd), F32)])
    return pl.pallas_call(
        body, name=name, grid_spec=grid_spec, out_shape=[jax.ShapeDtypeStruct((lp, QW), F32)] * 3,
        compiler_params=_params("arbitrary", "arbitrary"),
    )(qtab, ktab, q, k, v, do, lse, delta, ac["diag"], ac["front"])


HG_UNROLL = 2
HG_LEVELS = (64, 32, 16, 8, 4, 2)
N_LEV = len(HG_LEVELS)


def _hgrn_consts():
    c = HG_CHUNK
    m = np.zeros((N_LEV + 2, c, c), np.float32)
    masks = np.zeros((N_LEV, c, c), np.float32)
    for li, p in enumerate(HG_LEVELS):
        for t in range(c):
            mid = (t // p) * p + p // 2
            if t >= mid:
                m[li, t, mid:t + 1] = 1.0
            else:
                m[li, t, t + 1:mid] = 1.0
            for s in range(c):
                if s // p == t // p and t >= mid and s < mid:
                    masks[li, t, s] = 1.0
    for t in range(c):
        m[N_LEV, t, :t + 1] = 1.0
        m[N_LEV + 1, t, t + 1:] = 1.0
    mall = m.reshape((N_LEV + 2) * c, c)
    return jnp.asarray(mall, BF16), jnp.asarray(mall.T.copy(), BF16), jnp.asarray(masks, F32)


def _split_terms(x):
    hi = x.astype(BF16)
    lo = (x - hi.astype(F32)).astype(BF16)
    return jnp.concatenate([hi, lo], axis=1)


def _sum_terms(e3):
    return e3[:, :HG_D] + e3[:, HG_D:]


def _hgrn_chunk_fwd(hq, hf, hi, lb, valid, mall, masks, st):
    c = HG_CHUNK
    scale = HG_D ** -0.5
    sq = _sigmoid(hq)
    qv = hq * sq
    sg = _sigmoid(hf)
    f = lb + (1.0 - lb) * sg
    fc = jnp.maximum(f, F_MIN)
    lf = jnp.log(fc) * valid
    kv = (1.0 - lb) * (1.0 - sg) * valid
    e = _sum_terms(jnp.dot(mall, _split_terms(lf), preferred_element_type=F32))
    x = jnp.exp(e)
    a = jnp.zeros((c, c), F32)
    qe, ke = [], []
    for l in range(N_LEV):
        xl = x[l * c:(l + 1) * c]
        qe.append(qv * xl)
        ke.append(kv * xl)
        a = a + masks[l] * lax.dot_general(qe[l].astype(BF16), ke[l].astype(BF16), _DIMS["nt"],
                                           preferred_element_type=F32)
    row = lax.broadcasted_iota(jnp.int32, (c, c), 0)
    col = lax.broadcasted_iota(jnp.int32, (c, c), 1)
    a = a + jnp.where(row == col, jnp.sum(qv * kv, axis=-1, keepdims=True), 0.0)
    xb = x[N_LEV * c:(N_LEV + 1) * c]
    qb = qv * xb
    kb = kv * x[(N_LEV + 1) * c:]
    x_last = xb[c - 1:c]
    hib = hi.astype(BF16)
    o = scale * (jnp.dot(a.astype(BF16), hib, preferred_element_type=F32)
                 + lax.dot_general(qb.astype(BF16), st.astype(BF16), _DIMS["nt"], preferred_element_type=F32))
    st_new = st * x_last + lax.dot_general(hib, kb.astype(BF16), _DIMS["tn"], preferred_element_type=F32)
    saved = dict(sq=sq, qv=qv, sg=sg, f=f, fc=fc, kv=kv, x=x, a=a, qe=qe, ke=ke, qb=qb, kb=kb, x_last=x_last)
    return o, st_new, saved


def _split_ride(refs, n_in, n_out, n_scratch, ride):
    ri = len(ride.args) if ride else 0
    ro = len(ride.out_shape) if ride else 0
    a = n_in + ri
    b = a + n_out + ro
    c = b + n_scratch
    return refs[:n_in], refs[a:a + n_out], refs[b:c], refs[n_in:a] + refs[a + n_out:b] + refs[c:]


def _ride_call(ride):
    if ride is None:
        return [], [], [], [], []
    hbm = [HBM_SPEC] * len(ride.args)
    return hbm, list(ride.args), [HBM_SPEC] * len(ride.out_shape), list(ride.out_shape), list(ride.scratch)


def _hgrn_fwd(z, lb, nw, consts, *, zl, lv, name, ride=None):
    lp = z.shape[0]
    tb = _pick(lp, ROW_TILE, HG_CHUNK)
    ncb = tb // HG_CHUNK
    nb = lp // tb
    mall, _, masks = consts
    w = HG_HEADS * HG_D

    def body(*refs):
        ins, outs, (st_s,), ride_refs = _split_ride(refs, 8, 3, 1, ride)
        hq_ref, hf_ref, hi_ref, hg_ref, lb_ref, nw_ref, mall_ref, masks_ref = ins
        o_ref, ob_ref, st_ref = outs
        i = pl.program_id(0)

        @pl.when(i == 0)
        def _():
            st_s[...] = jnp.zeros_like(st_s)
            if ride is not None:
                ride.start(*ride_refs)

        nwv = nw_ref[...]
        mallv, masksv = mall_ref[...], masks_ref[...]

        def chunk(cix, carry):
            r0 = pl.multiple_of(cix * HG_CHUNK, HG_CHUNK)
            rows = pl.ds(r0, HG_CHUNK)
            valid = _row_valid(i * tb + r0, HG_CHUNK, lv)
            for h in range(HG_HEADS):
                sl = slice(h * HG_D, (h + 1) * HG_D)
                st = st_s[h]
                st_ref[h, cix] = st
                o, st_new, _ = _hgrn_chunk_fwd(hq_ref[rows, sl], hf_ref[rows, sl], hi_ref[rows, sl], lb_ref[:, sl],
                                               valid, mallv, masksv, st)
                st_s[h] = st_new
                o_ref[rows, sl] = o
                hg = hg_ref[rows, sl]
                r = lax.rsqrt(jnp.mean(o * o, axis=-1, keepdims=True) + EPS)
                ob_ref[rows, sl] = (o * r * nwv * (hg * _sigmoid(hg))).astype(BF16)
            return carry

        lax.fori_loop(0, ncb, chunk, 0, unroll=HG_UNROLL)

        if ride is not None:
            @pl.when(i == nb - 1)
            def _():
                ride.finish(*ride_refs)

    zb = lambda off: pl.BlockSpec((tb, w), lambda i: (i, off // w))
    full = pl.BlockSpec((tb, w), lambda i: (i, 0))
    const = lambda shape: pl.BlockSpec(shape, lambda i: (0,) * len(shape))
    r_in, r_args, r_out, r_shape, r_scratch = _ride_call(ride)
    outs = pl.pallas_call(
        body, name=name, grid=(nb,),
        in_specs=[zb(zl["hq"]), zb(zl["hf"]), zb(zl["hi"]), zb(zl["hg"]), const((1, w)), const((1, HG_D)),
                  const(mall.shape), const(masks.shape)] + r_in,
        out_specs=[full, full, pl.BlockSpec((HG_HEADS, ncb, HG_D, HG_D), lambda i: (0, i, 0, 0))] + r_out,
        out_shape=[jax.ShapeDtypeStruct((lp, w), F32), jax.ShapeDtypeStruct((lp, w), BF16),
                   jax.ShapeDtypeStruct((HG_HEADS, lp // HG_CHUNK, HG_D, HG_D), F32)] + r_shape,
        scratch_shapes=[pltpu.VMEM((HG_HEADS, HG_D, HG_D), F32)] + r_scratch,
        compiler_params=_params("arbitrary"),
    )(z, z, z, z, lb.reshape(1, w), nw.reshape(1, HG_D), mall, masks, *r_args)
    return outs[0], outs[1], outs[2], list(outs[3:])


def _hgrn_chunk_bwd(hq, hf, hi, hg, o, dout, st, dst, lbv, nwv, valid, mallv, malltv, masksv):
    c = HG_CHUNK
    scale = HG_D ** -0.5
    _, _, sv = _hgrn_chunk_fwd(hq, hf, hi, lbv, valid, mallv, masksv, st)
    shg = _sigmoid(hg)
    r = lax.rsqrt(jnp.mean(o * o, axis=-1, keepdims=True) + EPS)
    don = dout * (hg * shg)
    dhg = dout * (o * r * nwv) * shg * (1.0 + hg * (1.0 - shg))
    dnw = jnp.sum(don * o * r, axis=0, keepdims=True)
    wd = don * nwv
    do = r * wd - o * (r * r * r) * jnp.mean(o * wd, axis=-1, keepdims=True)
    dob16, hib = do.astype(BF16), hi.astype(BF16)
    dst16 = dst.astype(BF16)
    da = scale * lax.dot_general(dob16, hib, _DIMS["nt"], preferred_element_type=F32)
    dv = (scale * lax.dot_general(sv["a"].astype(BF16), dob16, _DIMS["tn"], preferred_element_type=F32)
          + lax.dot_general(sv["kb"].astype(BF16), dst16, _DIMS["nt"], preferred_element_type=F32))
    dkb = jnp.dot(hib, dst16, preferred_element_type=F32)
    dqb = scale * jnp.dot(dob16, st.astype(BF16), preferred_element_type=F32)
    dst_new = dst * sv["x_last"] + scale * lax.dot_general(dob16, sv["qb"].astype(BF16), _DIMS["tn"],
                                                           preferred_element_type=F32)
    dxl = jnp.sum(dst * st, axis=0, keepdims=True)
    x = sv["x"]
    dq = dqb * x[N_LEV * c:(N_LEV + 1) * c]
    dk = dkb * x[(N_LEV + 1) * c:]
    de = []
    for l in range(N_LEV):
        dam = (masksv[l] * da).astype(BF16)
        dqe = jnp.dot(dam, sv["ke"][l].astype(BF16), preferred_element_type=F32)
        dke = lax.dot_general(dam, sv["qe"][l].astype(BF16), _DIMS["tn"], preferred_element_type=F32)
        xl = x[l * c:(l + 1) * c]
        dq = dq + dqe * xl
        dk = dk + dke * xl
        de.append(dqe * sv["qe"][l] + dke * sv["ke"][l])
    dd = scale * jnp.sum(do * hi, axis=-1, keepdims=True)
    dq = dq + dd * sv["kv"]
    dk = dk + dd * sv["qv"]
    last = lax.broadcasted_iota(jnp.int32, (c, 1), 0) == c - 1
    de.append(dqb * sv["qb"] + jnp.where(last, dxl * sv["x_last"], 0.0))
    de.append(dkb * sv["kb"])
    dlf = _sum_terms(jnp.dot(malltv, _split_terms(jnp.concatenate(de, axis=0)), preferred_element_type=F32))
    sg, sq = sv["sg"], sv["sq"]
    df = jnp.where(sv["f"] > F_MIN, dlf * valid / sv["fc"], 0.0)
    dkm = dk * valid
    dhf = (df - dkm) * (1.0 - lbv) * sg * (1.0 - sg)
    dlb = jnp.sum((df - dkm) * (1.0 - sg), axis=0, keepdims=True)
    dhq = dq * sq * (1.0 + hq * (1.0 - sq))
    return dhq, dhf, dv, dhg, dlb, dnw, dst_new


def _hgrn_bwd(z, o_pre, dob, states, lb, nw, consts, *, zl, lv, name, ride=None):
    lp = z.shape[0]
    tb = _pick(lp, ROW_TILE, HG_CHUNK)
    ncb = tb // HG_CHUNK
    nb = lp // tb
    mall, mall_t, masks = consts
    w = HG_HEADS * HG_D
    c = HG_CHUNK

    def body(*refs):
        ins, outs, (dst_s,), ride_refs = _split_ride(refs, 12, 6, 1, ride)
        hq_ref, hf_ref, hi_ref, hg_ref, o_ref, dob_ref, st_ref, lb_ref, nw_ref, mall_ref, mallt_ref, masks_ref = ins
        dhq_ref, dhf_ref, dhi_ref, dhg_ref, dlb_ref, dnw_ref = outs
        i = pl.program_id(0)
        blk = nb - 1 - i

        @pl.when(i == 0)
        def _():
            dst_s[...] = jnp.zeros_like(dst_s)
            dlb_ref[...] = jnp.zeros_like(dlb_ref)
            dnw_ref[...] = jnp.zeros_like(dnw_ref)
            if ride is not None:
                ride.start(*ride_refs)

        nwv = nw_ref[...]
        mallv, malltv, masksv = mall_ref[...], mallt_ref[...], masks_ref[...]

        def chunk(jx, carry):
            cix = ncb - 1 - jx
            r0 = pl.multiple_of(cix * c, c)
            rows = pl.ds(r0, c)
            valid = _row_valid(blk * tb + r0, c, lv)
            for h in range(HG_HEADS):
                sl = slice(h * HG_D, (h + 1) * HG_D)
                dhq, dhf, dhi, dhg, dlb, dnw, dst_new = _hgrn_chunk_bwd(
                    hq_ref[rows, sl], hf_ref[rows, sl], hi_ref[rows, sl], hg_ref[rows, sl], o_ref[rows, sl],
                    dob_ref[rows, sl], st_ref[h, cix], dst_s[h], lb_ref[:, sl], nwv, valid, mallv, malltv, masksv)
                dst_s[h] = dst_new
                dhq_ref[rows, sl] = dhq
                dhf_ref[rows, sl] = dhf
                dhi_ref[rows, sl] = dhi
                dhg_ref[rows, sl] = dhg
                dlb_ref[:, sl] += dlb
                dnw_ref[...] += dnw
            return carry

        lax.fori_loop(0, ncb, chunk, 0, unroll=HG_UNROLL)

        if ride is not None:
            @pl.when(i == nb - 1)
            def _():
                ride.finish(*ride_refs)

    zb = lambda off: pl.BlockSpec((tb, w), lambda i: (nb - 1 - i, off // w))
    full = pl.BlockSpec((tb, w), lambda i: (nb - 1 - i, 0))
    const = lambda shape: pl.BlockSpec(shape, lambda i: (0,) * len(shape))
    r_in, r_args, r_out, r_shape, r_scratch = _ride_call(ride)
    outs = pl.pallas_call(
        body, name=name, grid=(nb,),
        in_specs=[zb(zl["hq"]), zb(zl["hf"]), zb(zl["hi"]), zb(zl["hg"]), full, full,
                  pl.BlockSpec((HG_HEADS, ncb, HG_D, HG_D), lambda i: (0, nb - 1 - i, 0, 0)),
                  const((1, w)), const((1, HG_D)), const(mall.shape), const(mall_t.shape), const(masks.shape)] + r_in,
        out_specs=[full, full, full, full, const((1, w)), const((1, HG_D))] + r_out,
        out_shape=[jax.ShapeDtypeStruct((lp, w), F32)] * 4
                  + [jax.ShapeDtypeStruct((1, w), F32), jax.ShapeDtypeStruct((1, HG_D), F32)] + r_shape,
        scratch_shapes=[pltpu.VMEM((HG_HEADS, HG_D, HG_D), F32)] + r_scratch,
        compiler_params=_params("arbitrary"),
    )(z, z, z, z, o_pre, dob, states, lb.reshape(1, w), nw.reshape(1, HG_D), mall, mall_t, masks, *r_args)
    dhq, dhf, dhi, dhg, dlb, dnw = outs[:6]
    return dhq, dhf, dhi, dhg, dlb[0], dnw[0], list(outs[6:])


def _loss_head(h, w, tpad, *, lv, name):
    lp, d = h.shape
    tm = _pick(lp, ROW_TILE, 8)

    def body(h_ref, w_ref, t_ref, dh_ref, loss_ref, dw_ref):
        i = pl.program_id(0)
        r0 = i * tm + lax.broadcasted_iota(jnp.int32, (tm, 1), 0)
        valid = ((r0 >= ROW_X) & (r0 < lv)).astype(F32)
        xv, wv = h_ref[...], w_ref[...]
        r = lax.rsqrt(jnp.mean(xv * xv, axis=-1, keepdims=True) + EPS)
        e = (xv * r * wv - t_ref[...]) * valid
        dy = e * (1.0 / d)
        wdy = dy * wv
        dh_ref[...] = r * wdy - xv * (r * r * r) * jnp.mean(xv * wdy, axis=-1, keepdims=True)

        @pl.when(i == 0)
        def _():
            loss_ref[...] = jnp.zeros_like(loss_ref)
            dw_ref[...] = jnp.zeros_like(dw_ref)

        loss_ref[...] += 0.5 * jnp.sum(jnp.mean(e * e, axis=-1, keepdims=True), axis=0, keepdims=True)
        dw_ref[...] += jnp.sum(dy * xv * r, axis=0, keepdims=True)

    row = pl.BlockSpec((tm, d), lambda i: (i, 0))
    vec = pl.BlockSpec((1, d), lambda i: (0, 0))
    dh, loss, dw = pl.pallas_call(
        body, name=name, grid=(lp // tm,), in_specs=[row, vec, row],
        out_specs=[row, pl.BlockSpec((8, LANES), lambda i: (0, 0)), vec],
        out_shape=[jax.ShapeDtypeStruct((lp, d), F32), jax.ShapeDtypeStruct((8, LANES), F32),
                   jax.ShapeDtypeStruct((1, d), F32)],
        compiler_params=_params("arbitrary"),
    )(h, w.reshape(1, d), tpad)
    return dh, loss[0, 0], dw[0]


def _adamw(w, g, m, v, *, name):
    shape = w.shape
    cols = shape[-1]
    rows = int(np.prod(shape[:-1])) if len(shape) > 1 else 1
    tr = _pick(rows, 256, 8)
    c1 = 1.0 - ADAM_B1 ** ADAM_STEP
    c2 = 1.0 - ADAM_B2 ** ADAM_STEP

    def body(w_ref, g_ref, m_ref, v_ref, d_ref, nm_ref, nv_ref):
        gv = g_ref[...]
        nm = ADAM_B1 * m_ref[...] + (1.0 - ADAM_B1) * gv
        nv = ADAM_B2 * v_ref[...] + (1.0 - ADAM_B2) * (gv * gv)
        d_ref[...] = -ADAM_LR * ((nm / c1) / (jnp.sqrt(nv / c2) + ADAM_EPS) + ADAM_WD * w_ref[...])
        nm_ref[...] = nm
        nv_ref[...] = nv

    spec = pl.BlockSpec((tr, cols), lambda i: (i, 0))
    r2 = lambda a: a.reshape(rows, cols)
    outs = pl.pallas_call(
        body, name=name, grid=(rows // tr,), in_specs=[spec] * 4, out_specs=[spec] * 3,
        out_shape=[jax.ShapeDtypeStruct((rows, cols), F32)] * 3, compiler_params=_params("parallel"),
    )(r2(w), r2(g), r2(m), r2(v))
    return tuple(o.reshape(shape) for o in outs)


HBM_SPEC = pl.BlockSpec(memory_space=pl.ANY)


def _coords():
    return lax.axis_index("x"), lax.axis_index("y"), lax.axis_index("c")


def _other_chips(x, y):
    return [(1 - x, y), (x, 1 - y), (1 - x, 1 - y)]


def _remote(src, dst, ssem, rsem, dev):
    return pltpu.make_async_remote_copy(src_ref=src, dst_ref=dst, send_sem=ssem, recv_sem=rsem,
                                        device_id=dev, device_id_type=MESH)


def _row_halves(w, c):
    rows = w.shape[-2]
    rh = rows // 2
    align = 8 * 4 // w.dtype.itemsize
    assert rh * 2 == rows and rh % align == 0, w.shape
    return pl.ds(pl.multiple_of(c * rh, align), rh), pl.ds(pl.multiple_of((1 - c) * rh, align), rh)


def _gathered_shapes(ws):
    return [jax.ShapeDtypeStruct((4, *w.shape), w.dtype) for w in ws]


def _own_block(g4s, ws):
    xi, yi, _ = _coords()
    return [lax.dynamic_update_slice(g4, w[None], (2 * xi + yi, 0, 0)) for g4, w in zip(g4s, ws)]


def _gather_chips(ws, *, name):
    n = len(ws)

    def body(*refs):
        w_refs, out_refs, (send_sems, recv_sems) = refs[:n], refs[n:2 * n], refs[2 * n:]
        x, y, c = _coords()
        sib = (x, y, 1 - c)
        chips = _other_chips(x, y)
        sent = []
        for t in range(n):
            half, _ = _row_halves(ws[t], c)
            for j, (px, py) in enumerate(chips):
                cp = _remote(w_refs[t].at[half], out_refs[t].at[2 * x + y, half], send_sems.at[6 * t + j],
                             recv_sems.at[6 * t + j], (px, py, c))
                cp.start()
                sent.append(cp)
        for t in range(n):
            half, _ = _row_halves(ws[t], c)
            for j, (px, py) in enumerate(chips):
                blk = out_refs[t].at[2 * px + py, half]
                _remote(w_refs[t].at[half], blk, send_sems.at[6 * t + j], recv_sems.at[6 * t + j],
                        (px, py, c)).wait_recv()
                fw = _remote(blk, blk, send_sems.at[6 * t + 3 + j], recv_sems.at[6 * t + 3 + j], sib)
                fw.start()
                sent.append(fw)
        for t in range(n):
            _, ohalf = _row_halves(ws[t], c)
            for j, (px, py) in enumerate(chips):
                blk = out_refs[t].at[2 * px + py, ohalf]
                _remote(blk, blk, send_sems.at[6 * t + 3 + j], recv_sems.at[6 * t + 3 + j], sib).wait_recv()
        for cp in sent:
            cp.wait_send()

    g4s = pl.pallas_call(
        body, name=name, in_specs=[HBM_SPEC] * n, out_specs=[HBM_SPEC] * n, out_shape=_gathered_shapes(ws),
        scratch_shapes=[pltpu.SemaphoreType.DMA((6 * n,)), pltpu.SemaphoreType.DMA((6 * n,))],
    )(*ws)
    return _own_block(g4s, ws)


def _swap_halves(gp, *, name):
    n, rows, cols = gp.shape
    rh = rows // 2

    def body(g_ref, out_ref, send_sems, recv_sems):
        x, y, c = _coords()
        sib = (x, y, 1 - c)
        ohalf = pl.ds(pl.multiple_of((1 - c) * rh, 8 * 4 // gp.dtype.itemsize), rh)
        cps = [_remote(g_ref.at[s, ohalf], out_ref.at[s], send_sems.at[s], recv_sems.at[s], sib) for s in range(n)]
        for cp in cps:
            cp.start()
        for cp in cps:
            cp.wait_recv()
        for cp in cps:
            cp.wait_send()

    return pl.pallas_call(
        body, name=name, in_specs=[HBM_SPEC], out_specs=HBM_SPEC,
        out_shape=jax.ShapeDtypeStruct((n, rh, cols), gp.dtype),
        scratch_shapes=[pltpu.SemaphoreType.DMA((n,)), pltpu.SemaphoreType.DMA((n,))],
    )(gp)


def _add_half(gp, got, cidx, *, name):
    n, rows, cols = gp.shape
    rh = rows // 2
    tr = _pick(rh, 512, 16)
    nrb = rh // tr

    def body(c_ref, a_ref, b_ref, o_ref):
        o_ref[...] = (a_ref[...].astype(F32) + b_ref[...].astype(F32)).astype(BF16)

    grid_spec = pltpu.PrefetchScalarGridSpec(
        num_scalar_prefetch=1, grid=(n, nrb),
        in_specs=[pl.BlockSpec((1, tr, cols), lambda s, i, c_ref: (s, c_ref[0] * nrb + i, 0)),
                  pl.BlockSpec((1, tr, cols), lambda s, i, c_ref: (s, i, 0))],
        out_specs=pl.BlockSpec((1, tr, cols), lambda s, i, c_ref: (s, i, 0)))
    return pl.pallas_call(
        body, name=name, grid_spec=grid_spec, out_shape=jax.ShapeDtypeStruct((n, rh, cols), BF16),
        compiler_params=_params("parallel", "parallel"),
    )(cidx, gp, got)


def _scatter_chips(p, *, name):
    _, rh, cols = p.shape

    def body(p_ref, out_ref, send_sems, recv_sems):
        x, y, c = _coords()
        cps = []
        for j, (px, py) in enumerate(_other_chips(x, y)):
            cps.append(_remote(p_ref.at[2 * px + py], out_ref.at[j], send_sems.at[j], recv_sems.at[j], (px, py, c)))
        for cp in cps:
            cp.start()
        for cp in cps:
            cp.wait_recv()
        for cp in cps:
            cp.wait_send()

    return pl.pallas_call(
        body, name=name, in_specs=[HBM_SPEC], out_specs=HBM_SPEC,
        out_shape=jax.ShapeDtypeStruct((3, rh, cols), p.dtype),
        scratch_shapes=[pltpu.SemaphoreType.DMA((3,)), pltpu.SemaphoreType.DMA((3,))],
    )(p)


def _sum_arrivals(p, land, kidx, *, name):
    _, rh, cols = p.shape
    tr = _pick(rh, 512, 16)

    def body(k_ref, a_ref, l_ref, o_ref):
        f = lambda v: v.astype(F32)
        o_ref[...] = ((f(a_ref[0]) + f(l_ref[0])) + f(l_ref[1])) + f(l_ref[2])

    grid_spec = pltpu.PrefetchScalarGridSpec(
        num_scalar_prefetch=1, grid=(rh // tr,),
        in_specs=[pl.BlockSpec((1, tr, cols), lambda i, k_ref: (k_ref[0], i, 0)),
                  pl.BlockSpec((3, tr, cols), lambda i, k_ref: (0, i, 0))],
        out_specs=pl.BlockSpec((tr, cols), lambda i, k_ref: (i, 0)))
    return pl.pallas_call(
        body, name=name, grid_spec=grid_spec, out_shape=jax.ShapeDtypeStruct((rh, cols), F32),
        compiler_params=_params("parallel"),
    )(kidx, p, land)


def _join_halves(q, *, name):
    rh, cols = q.shape

    def body(q_ref, out_ref, send_sem, recv_sem):
        x, y, c = _coords()
        half = pl.ds(pl.multiple_of(c * rh, 8), rh)
        ohalf = pl.ds(pl.multiple_of((1 - c) * rh, 8), rh)
        cp = _remote(q_ref, out_ref.at[half], send_sem, recv_sem, (x, y, 1 - c))
        cp.start()
        _remote(q_ref, out_ref.at[ohalf], send_sem, recv_sem, (x, y, 1 - c)).wait_recv()
        cp.wait_send()

    full = pl.pallas_call(
        body, name=name, in_specs=[HBM_SPEC], out_specs=HBM_SPEC,
        out_shape=jax.ShapeDtypeStruct((2 * rh, cols), q.dtype),
        scratch_shapes=[pltpu.SemaphoreType.DMA, pltpu.SemaphoreType.DMA],
    )(q)
    return lax.dynamic_update_slice(full, q, (lax.axis_index("c") * rh, 0))


def _rs_begin(gp, cidx, *, tag):
    got = _swap_halves(gp, name=f"rs_swap_{tag}")
    return _add_half(gp, got, cidx, name=f"rs_add_{tag}")


def _rs_end(p, land, kidx, *, tag):
    q = _sum_arrivals(p, land, kidx, name=f"rs_sum_{tag}")
    return _join_halves(q, name=f"rs_join_{tag}")


class _ScatterRide:
    def __init__(self, p):
        _, rh, cols = p.shape
        self.args = [p]
        self.out_shape = [jax.ShapeDtypeStruct((3, rh, cols), p.dtype)]
        self.scratch = [pltpu.SemaphoreType.DMA((3,)), pltpu.SemaphoreType.DMA((3,))]

    def _copies(self, p_ref, out_ref, ssem, rsem):
        x, y, c = _coords()
        return [_remote(p_ref.at[2 * px + py], out_ref.at[j], ssem.at[j], rsem.at[j], (px, py, c))
                for j, (px, py) in enumerate(_other_chips(x, y))]

    def start(self, *refs):
        for cp in self._copies(*refs):
            cp.start()

    def finish(self, *refs):
        cps = self._copies(*refs)
        for cp in cps:
            cp.wait_recv()
        for cp in cps:
            cp.wait_send()


class _GatherRide:
    def __init__(self, ws):
        n = len(ws)
        self.args = list(ws)
        self.out_shape = _gathered_shapes(ws)
        self.scratch = [pltpu.SemaphoreType.DMA((3 * n,)), pltpu.SemaphoreType.DMA((3 * n,))]

    def _copies(self, *refs):
        n = len(self.args)
        w_refs, out_refs, (ssem, rsem) = refs[:n], refs[n:2 * n], refs[2 * n:]
        x, y, c = _coords()
        send, recv = [], []
        for t in range(n):
            half, _ = _row_halves(self.args[t], c)
            for j, (px, py) in enumerate(_other_chips(x, y)):
                sems = (ssem.at[3 * t + j], rsem.at[3 * t + j], (px, py, c))
                send.append(_remote(w_refs[t].at[half], out_refs[t].at[2 * x + y, half], *sems))
                recv.append(_remote(w_refs[t].at[half], out_refs[t].at[2 * px + py, half], *sems))
        return send, recv

    def start(self, *refs):
        for cp in self._copies(*refs)[0]:
            cp.start()

    def finish(self, *refs):
        send, recv = self._copies(*refs)
        for cp in recv:
            cp.wait_recv()
        for cp in send:
            cp.wait_send()


def _gather_forward(g4s, *, name):
    n = len(g4s)

    def body(*refs):
        out_refs, (send_sems, recv_sems) = refs[n:2 * n], refs[2 * n:]
        x, y, c = _coords()
        sib = (x, y, 1 - c)
        chips = _other_chips(x, y)
        sent = []
        for t in range(n):
            half, _ = _row_halves(g4s[t], c)
            for j, (px, py) in enumerate(chips):
                blk = out_refs[t].at[2 * px + py, half]
                cp = _remote(blk, blk, send_sems.at[3 * t + j], recv_sems.at[3 * t + j], sib)
                cp.start()
                sent.append(cp)
        for t in range(n):
            _, ohalf = _row_halves(g4s[t], c)
            for j, (px, py) in enumerate(chips):
                blk = out_refs[t].at[2 * px + py, ohalf]
                _remote(blk, blk, send_sems.at[3 * t + j], recv_sems.at[3 * t + j], sib).wait_recv()
        for cp in sent:
            cp.wait_send()

    return pl.pallas_call(
        body, name=name, in_specs=[HBM_SPEC] * n, out_specs=[HBM_SPEC] * n,
        out_shape=[jax.ShapeDtypeStruct(g.shape, g.dtype) for g in g4s],
        input_output_aliases={t: t for t in range(n)},
        scratch_shapes=[pltpu.SemaphoreType.DMA((3 * n,)), pltpu.SemaphoreType.DMA((3 * n,))],
    )(*g4s)


def _allreduce_small(s, *, name):
    rows, cols = s.shape

    def body(s_ref, o_ref, buf, send_sems, recv_sems):
        x, y, c = _coords()
        me = 4 * x + 2 * y + c
        buf[me] = s_ref[...]
        cps = []
        for r in range(1, 8):
            peer = tuple((1 - v) if (r >> sh) & 1 else v for v, sh in ((x, 2), (y, 1), (c, 0)))
            cps.append(_remote(s_ref, buf.at[me], send_sems.at[r - 1], recv_sems.at[r - 1], peer))
        for cp in cps:
            cp.start()
        for cp in cps:
            cp.wait_recv()
        for cp in cps:
            cp.wait_send()
        acc = buf[0]
        for d in range(1, 8):
            acc = acc + buf[d]
        o_ref[...] = acc

    vm = pl.BlockSpec(memory_space=pltpu.VMEM)
    return pl.pallas_call(
        body, name=name, in_specs=[vm], out_specs=vm, out_shape=jax.ShapeDtypeStruct((rows, cols), F32),
        scratch_shapes=[pltpu.VMEM((8, rows, cols), F32), pltpu.SemaphoreType.DMA((7,)),
                        pltpu.SemaphoreType.DMA((7,))],
    )(s)


PACKED = ("ffn1_w_gu", "ffn1_w_down", "w_in", "w_uq", "w_ukv", "w_proj_attn", "w_proj_rec", "w_out",
          "ffn2_w_gu", "ffn2_w_down")
ROW_SHARDED = ("ffn1_w_down", "w_out", "ffn2_w_down")


def _pack_plan(shard_shapes):
    plan, off = {}, 0
    for n in PACKED:
        r, c = shard_shapes[n]
        assert (r * c) % PACK_W == 0
        plan[n] = (off, r * c // PACK_W, (r, c))
        off += r * c // PACK_W
    total = -(-off // 32) * 32
    return plan, total


def _pack(tensors, plan, total, dtype):
    parts = [tensors[n].astype(dtype).reshape(-1, PACK_W) for n in PACKED]
    used = sum(p.shape[0] for p in parts)
    if total > used:
        parts.append(jnp.zeros((total - used, PACK_W), dtype))
    return jnp.concatenate(parts, axis=0)


def _full_weights(g4s):
    out = {}
    for n, g in zip(PACKED, g4s):
        _, r, c = g.shape
        out[n] = g.reshape(4 * r, c) if n in ROW_SHARDED else jnp.swapaxes(g, 0, 1).reshape(r, 4 * c)
    return out


def _pack_grads(grads, plan, total):
    blocks = []
    for s in range(4):
        t = {}
        for n in PACKED:
            _, _, (r, c) = plan[n]
            t[n] = grads[n][s * r:(s + 1) * r] if n in ROW_SHARDED else grads[n][:, s * c:(s + 1) * c]
        blocks.append(_pack(t, plan, total, BF16))
    return jnp.stack(blocks)


def _unpack_shard(p, plan):
    return {n: p[plan[n][0]:plan[n][0] + plan[n][1]].reshape(plan[n][2]) for n in PACKED}


def _swap_cols(w):
    hlf = w.shape[1] // 2
    return jnp.concatenate([-w[:, hlf:], w[:, :hlf]], axis=1)


def _unswap_cols(dw):
    hlf = dw.shape[1] // 2
    return jnp.concatenate([dw[:, hlf:], -dw[:, :hlf]], axis=1)


def _layer_weights(full, d):
    zl = _z_layout(d)
    f = full["ffn1_w_down"].shape[0]
    w_in = full["w_in"]
    o = 0
    cols = {}
    for nm, wd in (("cq", Q_LORA), ("ckv", KV_LORA), ("kpe", QK_ROPE), ("hq", 512), ("hf", 512), ("hi", 512),
                   ("hg", 512), ("ga", d), ("gb", d)):
        cols[nm] = w_in[:, o:o + wd]
        o += wd
    zc = lambda n: jnp.zeros((d, n), BF16)
    win_p = jnp.concatenate(
        [cols["cq"], zc(QK_NOPE), cols["kpe"], zc(32), cols["ckv"], zc(QK_NOPE), _swap_cols(cols["kpe"]), zc(32),
         zc(LANES), cols["ga"], cols["gb"], cols["hq"], cols["hf"], cols["hi"], cols["hg"]], axis=1)
    assert win_p.shape[1] == zl["total"]
    wq = full["w_uq"].reshape(Q_LORA, MLA_HEADS, QK_NOPE + QK_ROPE)
    nope, rope = wq[:, :, :QK_NOPE], wq[:, :, QK_NOPE:]
    z32 = jnp.zeros((Q_LORA, MLA_HEADS, 32), BF16)
    z64 = jnp.zeros((Q_LORA, MLA_HEADS, QK_NOPE), BF16)
    rope_sw = jnp.concatenate([-rope[:, :, 16:], rope[:, :, :16]], axis=2)
    wqa = jnp.concatenate([nope, rope, z32], axis=2).reshape(Q_LORA, QW)
    wqb = jnp.concatenate([z64, rope_sw, z32], axis=2).reshape(Q_LORA, QW)
    wpa = full["w_proj_attn"].reshape(MLA_HEADS, V_HEAD, d)
    wpa_p = jnp.concatenate([jnp.zeros_like(wpa), wpa], axis=1).reshape(QW, d)
    return dict(
        wg1=full["ffn1_w_gu"][:, :f], wu1=full["ffn1_w_gu"][:, f:], wd1=full["ffn1_w_down"],
        wg2=full["ffn2_w_gu"][:, :f], wu2=full["ffn2_w_gu"][:, f:], wd2=full["ffn2_w_down"],
        win=win_p, wq2=jnp.concatenate([wqa, wqb], axis=1), wqa=wqa, wqb=wqb, wkv=full["w_ukv"], wpa=wpa_p,
        wpr=full["w_proj_rec"], wout=full["w_out"])


def _natural_grads(g, d):
    zl = _z_layout(d)
    dwin = g["win"]
    kpe = dwin[:, Z_KPA + QK_NOPE:Z_KPA + QK_NOPE + QK_ROPE] + _unswap_cols(
        dwin[:, Z_KPB + QK_NOPE:Z_KPB + QK_NOPE + QK_ROPE])
    w_in = jnp.concatenate(
        [dwin[:, Z_Q:Z_Q + Q_LORA], dwin[:, Z_KV:Z_KV + KV_LORA], kpe, dwin[:, zl["hq"]:zl["hq"] + 2048],
         dwin[:, zl["ga"]:zl["ga"] + 2 * d]], axis=1)
    qa = g["wqa"].reshape(Q_LORA, MLA_HEADS, HEAD_W)
    qb = g["wqb"].reshape(Q_LORA, MLA_HEADS, HEAD_W)[:, :, QK_NOPE:QK_NOPE + QK_ROPE]
    rope = qa[:, :, QK_NOPE:QK_NOPE + QK_ROPE] + jnp.concatenate([qb[:, :, 16:], -qb[:, :, :16]], axis=2)
    w_uq = jnp.concatenate([qa[:, :, :QK_NOPE], rope], axis=2).reshape(Q_LORA, -1)
    wpa = g["wpa"].reshape(MLA_HEADS, 2 * V_HEAD, d)[:, V_HEAD:].reshape(MLA_HEADS * V_HEAD, d)
    return dict(
        ffn1_w_gu=jnp.concatenate([g["wg1"], g["wu1"]], axis=1), ffn1_w_down=g["wd1"],
        ffn2_w_gu=jnp.concatenate([g["wg2"], g["wu2"]], axis=1), ffn2_w_down=g["wd2"],
        w_in=w_in, w_uq=w_uq, w_ukv=g["wkv"], w_proj_attn=wpa, w_proj_rec=g["wpr"], w_out=g["wout"])


def _rope_tables(lp):
    pos = jnp.maximum(jnp.arange(lp) - FRONT, 0).astype(F32)
    half = QK_ROPE // 2
    inv = ROPE_THETA ** (-jnp.arange(half, dtype=F32) / half)
    ang = pos[:, None] * inv[None, :]
    cos, sin = jnp.cos(ang), jnp.sin(ang)
    cos_t = jnp.concatenate([jnp.ones((lp, QK_NOPE), F32), cos, cos, jnp.zeros((lp, 32), F32)], axis=1)
    sin_t = jnp.concatenate([jnp.zeros((lp, QK_NOPE), F32), sin, sin, jnp.zeros((lp, 32), F32)], axis=1)
    return cos_t, sin_t


def _lower_bounds(raw):
    p = jax.nn.softmax(raw.astype(F32), axis=0)
    return jnp.cumsum(p, axis=0) - p[0:1]


def _ffn_fwd(h, nw, wg, wu, wd, tag):
    a, a_t = _rmsnorm_fwd(h, nw, width=h.shape[1], col_block=0, transposed=True, name=f"norm_{tag}")
    g = _matmul([(a, wg)], "nn", out_dtype=BF16, name=f"gate_{tag}")
    u = _matmul([(a, wu)], "nn", out_dtype=BF16, name=f"up_{tag}")
    act, act_t = _swiglu_fwd(g, u, name=f"swiglu_{tag}")
    out = _matmul([(act, wd)], "nn", res=h, scale=0.5, name=f"down_{tag}")
    return out, dict(h=h, a_t=a_t, g=g, u=u, act_t=act_t)


def _ffn_bwd(dout, sv, nw, wg, wu, wd, lv, tag):
    dact = _matmul([(dout, wd)], "nt", scale=0.5, out_dtype=BF16, name=f"ddown_{tag}")
    dwd = _matmul([(sv["act_t"], dout)], "nn", scale=0.5, name=f"dwdown_{tag}")
    dg, du = _swiglu_bwd(dact, sv["g"], sv["u"], name=f"dswiglu_{tag}")
    dwg = _matmul([(sv["a_t"], dg)], "nn", name=f"dwgate_{tag}")
    dwu = _matmul([(sv["a_t"], du)], "nn", name=f"dwup_{tag}")
    da = _matmul([(dg, wg), (du, wu)], "nt", name=f"dnormed_{tag}")
    dh, dn = _rmsnorm_bwd(sv["h"], nw, da, width=da.shape[1], col_block=0, lv=lv, dres=dout, name=f"dnorm_{tag}")
    return dh, dn, dwg, dwu, dwd


def _layer_fwd(h0, lw, sm, lb, tabs, consts, lv, l, ride=None):
    d = h0.shape[1]
    zl = _z_layout(d)
    cos_t, sin_t = tabs[:2]
    h1, s1 = _ffn_fwd(h0, sm["ffn1_norm"], lw["wg1"], lw["wu1"], lw["wd1"], f"ffn1_{l}")
    um, um_t = _rmsnorm_fwd(h1, sm["mix_norm"], width=d, col_block=0, transposed=True, name=f"norm_mix_{l}")
    z = _matmul([(um, lw["win"])], "nn", name=f"inproj_{l}")
    qn, qn_t = _rmsnorm_fwd(z, sm["q_norm"], width=Q_LORA, col_block=Z_Q // Q_LORA, transposed=True,
                            name=f"norm_q_{l}")
    kvn, kvn_t = _rmsnorm_fwd(z, sm["kv_norm"], width=KV_LORA, col_block=Z_KV // KV_LORA, transposed=True,
                              name=f"norm_kv_{l}")
    q2 = _matmul([(qn, lw["wq2"])], "nn", name=f"uq_{l}")
    kv = _matmul([(kvn, lw["wkv"])], "nn", name=f"ukv_{l}")
    q, k, v = _qkv_prep_fwd(q2, kv, z, cos_t, sin_t, name=f"qkv_{l}")
    o, lse = _attn_fwd(q, k, v, tabs[2], lv=lv, name=f"attn_{l}")
    ya = _matmul([(o, lw["wpa"])], "nn", out_dtype=BF16, name=f"proj_attn_{l}")
    o_pre, ob, states, rode = _hgrn_fwd(z, lb, sm["hg_norm"], consts, zl=zl, lv=lv, name=f"hgrn_{l}", ride=ride)
    yb = _matmul([(ob, lw["wpr"])], "nn", out_dtype=BF16, name=f"proj_rec_{l}")
    mg, mg_t = _merge_fwd(ya, yb, z, zl=zl, name=f"merge_{l}")
    h2 = _matmul([(mg, lw["wout"])], "nn", res=h1, name=f"out_{l}")
    h3, s2 = _ffn_fwd(h2, sm["ffn2_norm"], lw["wg2"], lw["wu2"], lw["wd2"], f"ffn2_{l}")
    saved = dict(s1=s1, s2=s2, h1=h1, um_t=um_t, z=z, qn_t=qn_t, kvn_t=kvn_t, q=q, k=k, v=v, o=o, lse=lse, ya=ya, yb=yb,
                 o_pre=o_pre, ob=ob, states=states, mg_t=mg_t)
    return h3, saved, rode


def _layer_bwd(dh3, sv, lw, sm, lb, tabs, consts, lv, l, ride=None):
    d = dh3.shape[1]
    lp = dh3.shape[0]
    zl = _z_layout(d)
    cos_t, sin_t = tabs[:2]
    z = sv["z"]
    g = {}
    sg = {}
    dh2, sg["ffn2_norm"], g["wg2"], g["wu2"], g["wd2"] = _ffn_bwd(
        dh3, sv["s2"], sm["ffn2_norm"], lw["wg2"], lw["wu2"], lw["wd2"], lv, f"ffn2_{l}")
    dmg = _matmul([(dh2, lw["wout"])], "nt", out_dtype=BF16, name=f"dmerged_{l}")
    g["wout"] = _matmul([(sv["mg_t"], dh2)], "nn", name=f"dwout_{l}")
    dya, dyb, dga, dgb = _merge_bwd(dmg, sv["ya"], sv["yb"], z, zl=zl, name=f"dmerge_{l}")
    doa = _matmul([(dya, lw["wpa"])], "nt", name=f"dattn_out_{l}")
    g["wpa"] = _matmul([(sv["o"], dya)], "tn", name=f"dwproj_attn_{l}")
    dob = _matmul([(dyb, lw["wpr"])], "nt", name=f"drec_out_{l}")
    g["wpr"] = _matmul([(sv["ob"], dyb)], "tn", name=f"dwproj_rec_{l}")
    dhq, dhf, dhi, dhg, dlb, sg["hg_norm"], rode = _hgrn_bwd(
        z, sv["o_pre"], dob, sv["states"], lb, sm["hg_norm"], consts, zl=zl, lv=lv, name=f"dhgrn_{l}", ride=ride)
    delta = _attn_delta(doa, sv["o"], name=f"attn_delta_{l}")
    dq, dk, dv = _attn_bwd(sv["q"], sv["k"], sv["v"], doa, sv["lse"], delta, tabs[2], name=f"dattn_{l}")
    dqa, dqb, dkv, dza, dzb = _qkv_prep_bwd(dq, dk, dv, cos_t, sin_t, name=f"dqkv_{l}")
    dqn = _matmul([(dqa, lw["wqa"]), (dqb, lw["wqb"])], "nt", name=f"dqn_{l}")
    g["wqa"] = _matmul([(sv["qn_t"], dqa)], "nn", name=f"dwqa_{l}")
    g["wqb"] = _matmul([(sv["qn_t"], dqb)], "nn", name=f"dwqb_{l}")
    dkvn = _matmul([(dkv, lw["wkv"])], "nt", name=f"dkvn_{l}")
    g["wkv"] = _matmul([(sv["kvn_t"], dkv)], "nn", name=f"dwkv_{l}")
    dzq, sg["q_norm"] = _rmsnorm_bwd(z, sm["q_norm"], dqn, width=Q_LORA, col_block=Z_Q // Q_LORA, lv=lv,
                                     name=f"dnorm_q_{l}")
    dzkv, sg["kv_norm"] = _rmsnorm_bwd(z, sm["kv_norm"], dkvn, width=KV_LORA, col_block=Z_KV // KV_LORA, lv=lv,
                                       name=f"dnorm_kv_{l}")
    dz = jnp.concatenate([dzq, dza, dzkv, dzb, jnp.zeros((lp, LANES), F32), dga, dgb, dhq, dhf, dhi, dhg],
                         axis=1).astype(BF16)
    dum = _matmul([(dz, lw["win"])], "nt", name=f"dmixed_{l}")
    g["win"] = _matmul([(sv["um_t"], dz)], "nn", name=f"dwin_{l}")
    dh1, sg["mix_norm"] = _rmsnorm_bwd(sv["h1"], sm["mix_norm"], dum, width=d, col_block=0, lv=lv, dres=dh2,
                                       name=f"dnorm_mix_{l}")
    dh0, sg["ffn1_norm"], g["wg1"], g["wu1"], g["wd1"] = _ffn_bwd(
        dh1, sv["s1"], sm["ffn1_norm"], lw["wg1"], lw["wu1"], lw["wd1"], lv, f"ffn1_{l}")
    return dh0, g, sg, dlb, rode


WEIGHTS = ("meta_tokens", "ffn1_norm", "ffn1_w_gu", "ffn1_w_down", "mix_norm", "w_in", "q_norm", "kv_norm", "w_uq",
           "w_ukv", "hg_lb_raw", "hg_norm", "w_proj_attn", "w_proj_rec", "w_out", "ffn2_norm", "ffn2_w_gu",
           "ffn2_w_down", "final_norm")
SMALL = ("ffn1_norm", "mix_norm", "q_norm", "kv_norm", "hg_lb_raw", "hg_norm", "ffn2_norm")


def _small_rows(vals):
    pad = lambda a: jnp.pad(a, ((0, -a.shape[0] % 8), (0, PACK_W - a.shape[1])))
    rows = [pad(vals[n]) for n in SMALL]
    rows.append(pad(vals["final_norm"][None, :]))
    rows.append(pad(vals["meta_tokens"]))
    rows.append(pad(vals["loss"].reshape(1, 1)))
    return jnp.concatenate(rows, axis=0)


def _small_unrows(s, d, widths):
    out, o = {}, 0
    for n in SMALL:
        out[n] = s[o:o + DEPTH, :widths[n]]
        o += -(-DEPTH // 8) * 8
    out["final_norm"] = s[o, :d]
    o += 8
    out["meta_tokens"] = s[o:o + N_META, :d]
    o += -(-N_META // 8) * 8
    out["loss"] = s[o, 0]
    return out


def _step(args):
    x = args["x"][0]
    seq, d = x.shape
    assert d <= PACK_W
    lv = ROW_X + seq
    lp = -(-lv // ROW_TILE) * ROW_TILE
    xi, yi, ci = _coords()
    kidx = (2 * xi + yi).astype(jnp.int32).reshape(1)
    cidx = ci.astype(jnp.int32).reshape(1)
    consts = _hgrn_consts()
    tabs = (*_rope_tables(lp), _attn_consts(lp))

    shard_shapes = {n: args[n].shape[1:] for n in PACKED}
    plan, total = _pack_plan(shard_shapes)
    shards = [[args[n][l].astype(BF16) for n in PACKED] for l in range(DEPTH)]
    mt = args["meta_tokens"]
    mt4 = _gather_chips([mt], name="gather_meta")[0]
    meta = jnp.concatenate(list(mt4), axis=1)

    sm = [{n: args[n][l] for n in SMALL} for l in range(DEPTH)]
    lbs = _lower_bounds(args["hg_lb_raw"])

    h = jnp.concatenate([jnp.zeros((FRONT, d), F32), meta, x, jnp.zeros((lp - lv, d), F32)], axis=0)
    saved, lws = [], []
    g4s = _gather_chips(shards[0], name="gather_0")
    for l in range(DEPTH):
        lws.append(_layer_weights(_full_weights(g4s), d))
        ride = _GatherRide(shards[l + 1]) if l + 1 < DEPTH else None
        h, sv, rode = _layer_fwd(h, lws[l], sm[l], lbs[l], tabs, consts, lv, l, ride)
        saved.append(sv)
        if ride is not None:
            g4s = _own_block(_gather_forward(rode, name=f"gather_fwd_{l + 1}"), shards[l + 1])
    tpad = jnp.pad(args["loss_target"][0], ((ROW_X, lp - lv), (0, 0)))
    dh, loss, dfinal = _loss_head(h, args["final_norm"], tpad, lv=lv, name="loss_head")

    small = {n: [None] * DEPTH for n in SMALL}
    dlbs = [None] * DEPTH
    shard_grads = [None] * DEPTH
    waiting = None
    for l in reversed(range(DEPTH)):
        ride = _ScatterRide(waiting) if waiting is not None else None
        dh, g, sg, dlbs[l], rode = _layer_bwd(dh, saved[l], lws[l], sm[l], lbs[l], tabs, consts, lv, l, ride)
        if ride is not None:
            shard_grads[l + 1] = _unpack_shard(_rs_end(waiting, rode[0], kidx, tag=str(l + 1)), plan)
        for n in sg:
            small[n][l] = sg[n]
        waiting = _rs_begin(_pack_grads(_natural_grads(g, d), plan, total), cidx, tag=str(l))
    land = _scatter_chips(waiting, name="rs_scatter_0")
    shard_grads[0] = _unpack_shard(_rs_end(waiting, land, kidx, tag="0"), plan)

    _, lb_vjp = jax.vjp(_lower_bounds, args["hg_lb_raw"])
    small_vals = {n: jnp.stack(small[n]) for n in SMALL if n != "hg_lb_raw"}
    small_vals["hg_lb_raw"] = lb_vjp(jnp.stack(dlbs))[0]
    small_vals["final_norm"] = dfinal
    small_vals["meta_tokens"] = dh[FRONT:ROW_X]
    small_vals["loss"] = loss
    widths = {n: args[n].shape[1] for n in SMALL}
    tot = _small_unrows(_allreduce_small(_small_rows(small_vals), name="allreduce_small"), d, widths)

    grads = {n: jnp.stack([shard_grads[l][n] for l in range(DEPTH)]) for n in PACKED}
    for n in SMALL:
        grads[n] = tot[n]
    grads["final_norm"] = tot["final_norm"]
    mcols = mt.shape[1]
    grads["meta_tokens"] = lax.dynamic_slice_in_dim(tot["meta_tokens"], (2 * xi + yi) * mcols, mcols, axis=1)
    grad_x = dh[ROW_X:lv][None]

    delta, new_m, new_v = {}, {}, {}
    for n in WEIGHTS:
        delta[n], new_m[n], new_v[n] = _adamw(args[n], grads[n], args["m_" + n], args["v_" + n], name=f"adamw_{n}")
    return (tot["loss"], grad_x, *[grads[n] for n in WEIGHTS], *[delta[n] for n in WEIGHTS],
            *[new_m[n] for n in WEIGHTS], *[new_v[n] for n in WEIGHTS])


def kernel(x, meta_tokens, ffn1_norm, ffn1_w_gu, ffn1_w_down, mix_norm, w_in, q_norm, kv_norm, w_uq, w_ukv, hg_lb_raw, hg_norm, w_proj_attn, w_proj_rec, w_out, ffn2_norm, ffn2_w_gu, ffn2_w_down, final_norm, loss_target, m_meta_tokens, m_ffn1_norm, m_ffn1_w_gu, m_ffn1_w_down, m_mix_norm, m_w_in, m_q_norm, m_kv_norm, m_w_uq, m_w_ukv, m_hg_lb_raw, m_hg_norm, m_w_proj_attn, m_w_proj_rec, m_w_out, m_ffn2_norm, m_ffn2_w_gu, m_ffn2_w_down, m_final_norm, v_meta_tokens, v_ffn1_norm, v_ffn1_w_gu, v_ffn1_w_down, v_mix_norm, v_w_in, v_q_norm, v_kv_norm, v_w_uq, v_w_ukv, v_hg_lb_raw, v_hg_norm, v_w_proj_attn, v_w_proj_rec, v_w_out, v_ffn2_norm, v_ffn2_w_gu, v_ffn2_w_down, v_final_norm):
    return _step(dict(locals()))
```

```python
import functools
import math

import numpy as np
import jax
import jax.numpy as jnp
from jax import lax
from jax.experimental import pallas as pl
from jax.experimental.pallas import tpu as pltpu

F32 = jnp.float32
BF16 = jnp.bfloat16

N_META = 16
MLA_HEADS = 8
Q_LORA = 384
KV_LORA = 256
QK_NOPE = 64
QK_ROPE = 32
V_HEAD = 64
ROPE_THETA = 10000.0
HG_HEADS = 4
HG_D = 128
HG_CHUNK = 64
EPS = 1e-6
NEG_BIG = -1e30
F_MIN = 1e-20
DEPTH = 4

ADAM_LR = 0.001
ADAM_B1 = 0.9
ADAM_B2 = 0.999
ADAM_EPS = 1e-08
ADAM_WD = 0.01
ADAM_STEP = 10

LANES = 128
FRONT = (-N_META) % HG_CHUNK
ROW_X = FRONT + N_META
ROW_TILE = 640
HEAD_W = 128
QW = MLA_HEADS * HEAD_W
PREP_HEADS = 4
VMEM_LIMIT = 56 * 1024 * 1024
MATMUL_VMEM = 42 * 1024 * 1024
PACK_W = 1024
MESH = pl.DeviceIdType.MESH

Z_Q, Z_KPA, Z_KV, Z_KPB, Z_PAD, Z_GA = 0, 384, 512, 768, 896, 1024


def _z_layout(d):
    ga = Z_GA
    gb = ga + d
    hq = gb + d
    hf = hq + 512
    hi = hf + 512
    hg = hi + 512
    return dict(ga=ga, gb=gb, hq=hq, hf=hf, hi=hi, hg=hg, total=hg + 512)


def _pick(dim, cap, mult=LANES):
    if dim <= cap:
        return dim
    best = None
    for t in range(mult, cap + 1, mult):
        if dim % t == 0:
            best = t
    assert best is not None, (dim, cap, mult)
    return best


def _params(*sem):
    return pltpu.CompilerParams(dimension_semantics=sem, vmem_limit_bytes=VMEM_LIMIT)


def _sigmoid(x):
    return 1.0 / (1.0 + jnp.exp(-x))


def _row_valid(row0, n, lv):
    r = row0 + lax.broadcasted_iota(jnp.int32, (n, 1), 0)
    return ((r >= FRONT) & (r < lv)).astype(F32)


_DIMS = {"nn": (((1,), (0,)), ((), ())), "nt": (((1,), (1,)), ((), ())), "tn": (((0,), (0,)), ((), ()))}


def _matmul(pairs, mode, *, name, out_dtype=F32, res=None, scale=1.0):
    a0, b0 = pairs[0]
    if mode == "nn":
        (m, k), n = a0.shape, b0.shape[1]
    elif mode == "nt":
        (m, k), n = a0.shape, b0.shape[0]
    else:
        (k, m), n = a0.shape, b0.shape[1]
    if mode == "tn":
        tm, tn, tk = _pick(m, 1024), _pick(n, 1408), _pick(k, ROW_TILE, 8)
    else:
        if k > m:
            tm, tn, kcap = _pick(m, 1408, 16), _pick(n, 1408), 1664
        else:
            tm, tn, kcap = _pick(m, ROW_TILE, 8), _pick(n, 2816), 2816
        out_b = jnp.dtype(out_dtype).itemsize
        per_k = len(pairs) * 2 * (tm * a0.dtype.itemsize + tn * b0.dtype.itemsize)
        fixed = tm * tn * (2 * out_b + 4 + (8 if res is not None else 0))
        tk = _pick(k, kcap)
        while tk > LANES and fixed + per_k * tk > MATMUL_VMEM:
            tk = _pick(k, tk - LANES)
    nk = k // tk
    npair = len(pairs)
    dims = _DIMS[mode]

    def body(*refs):
        ins = refs[:2 * npair]
        pos = 2 * npair
        res_ref = None
        if res is not None:
            res_ref = refs[pos]
            pos += 1
        o_ref = refs[pos]
        kk = pl.program_id(2)

        part = None
        for p in range(npair):
            a = ins[2 * p][...].astype(BF16)
            b = ins[2 * p + 1][...].astype(BF16)
            d = lax.dot_general(a, b, dims, preferred_element_type=F32)
            part = d if part is None else part + d

        def finish(r):
            if scale != 1.0:
                r = r * scale
            if res_ref is not None:
                r = r + res_ref[...]
            o_ref[...] = r.astype(out_dtype)

        if nk == 1:
            finish(part)
            return
        acc = refs[pos + 1]

        @pl.when(kk == 0)
        def _():
            acc[...] = part

        @pl.when(kk > 0)
        def _():
            acc[...] += part

        @pl.when(kk == nk - 1)
        def _():
            finish(acc[...])

    if mode == "nn":
        a_spec = pl.BlockSpec((tm, tk), lambda i, j, q: (i, q))
        b_spec = pl.BlockSpec((tk, tn), lambda i, j, q: (q, j))
    elif mode == "nt":
        a_spec = pl.BlockSpec((tm, tk), lambda i, j, q: (i, q))
        b_spec = pl.BlockSpec((tn, tk), lambda i, j, q: (j, q))
    else:
        a_spec = pl.BlockSpec((tk, tm), lambda i, j, q: (q, i))
        b_spec = pl.BlockSpec((tk, tn), lambda i, j, q: (q, j))
    o_spec = pl.BlockSpec((tm, tn), lambda i, j, q: (i, j))
    in_specs, args = [], []
    for a, b in pairs:
        in_specs += [a_spec, b_spec]
        args += [a, b]
    if res is not None:
        in_specs.append(o_spec)
        args.append(res)
    return pl.pallas_call(
        body, name=name, grid=(m // tm, n // tn, nk), in_specs=in_specs, out_specs=o_spec,
        out_shape=jax.ShapeDtypeStruct((m, n), out_dtype),
        scratch_shapes=[pltpu.VMEM((tm, tn), F32)] if nk > 1 else [],
        compiler_params=_params("parallel", "parallel", "arbitrary"),
    )(*args)


def _rmsnorm_fwd(x, w, *, width, col_block, name, transposed=False):
    lp = x.shape[0]
    tm = _pick(lp, ROW_TILE, 8)

    def body(x_ref, w_ref, o_ref, *ot_ref):
        xv = x_ref[...]
        r = lax.rsqrt(jnp.mean(xv * xv, axis=-1, keepdims=True) + EPS)
        y = xv * r * w_ref[...]
        o_ref[...] = y.astype(BF16)
        if transposed:
            ot_ref[0][...] = y.T.astype(BF16)

    out_specs = [pl.BlockSpec((tm, width), lambda i: (i, 0))]
    out_shape = [jax.ShapeDtypeStruct((lp, width), BF16)]
    if transposed:
        out_specs.append(pl.BlockSpec((width, tm), lambda i: (0, i)))
        out_shape.append(jax.ShapeDtypeStruct((width, lp), BF16))
    outs = pl.pallas_call(
        body, name=name, grid=(lp // tm,),
        in_specs=[pl.BlockSpec((tm, width), lambda i: (i, col_block)), pl.BlockSpec((1, width), lambda i: (0, 0))],
        out_specs=out_specs, out_shape=out_shape, compiler_params=_params("parallel"),
    )(x, w.reshape(1, width))
    return tuple(outs) if transposed else outs[0]


def _rmsnorm_bwd(x, w, dy, *, width, col_block, lv, name, dres=None):
    lp = x.shape[0]
    tm = _pick(lp, ROW_TILE, 8)

    def body(*refs):
        if dres is None:
            x_ref, w_ref, dy_ref, dx_ref, dw_ref = refs
            dres_ref = dx16_ref = None
        else:
            x_ref, w_ref, dy_ref, dres_ref, dx_ref, dx16_ref, dw_ref = refs
        i = pl.program_id(0)
        xv = x_ref[...]
        dyv = dy_ref[...] * _row_valid(i * tm, tm, lv)
        r = lax.rsqrt(jnp.mean(xv * xv, axis=-1, keepdims=True) + EPS)
        wdy = dyv * w_ref[...]
        dx = r * wdy - xv * (r * r * r) * jnp.mean(xv * wdy, axis=-1, keepdims=True)
        if dres_ref is not None:
            dx = dx + dres_ref[...]
            dx16_ref[...] = dx.astype(BF16)
        dx_ref[...] = dx

        @pl.when(i == 0)
        def _():
            dw_ref[...] = jnp.zeros_like(dw_ref)

        dw_ref[...] += jnp.sum(dyv * xv * r, axis=0, keepdims=True)

    row = pl.BlockSpec((tm, width), lambda i: (i, 0))
    in_specs = [pl.BlockSpec((tm, width), lambda i: (i, col_block)), pl.BlockSpec((1, width), lambda i: (0, 0)), row]
    args = [x, w.reshape(1, width), dy]
    if dres is not None:
        in_specs.append(row)
        args.append(dres)
    extra = dres is not None
    outs = pl.pallas_call(
        body, name=name, grid=(lp // tm,), in_specs=in_specs,
        out_specs=[row] + [row] * extra + [pl.BlockSpec((1, width), lambda i: (0, 0))],
        out_shape=[jax.ShapeDtypeStruct((lp, width), F32)] + [jax.ShapeDtypeStruct((lp, width), BF16)] * extra
                  + [jax.ShapeDtypeStruct((1, width), F32)],
        compiler_params=_params("arbitrary"),
    )(*args)
    if extra:
        return outs[0], outs[1], outs[2][0]
    return outs[0], outs[1][0]


def _swiglu_fwd(g, u, *, name):
    lp, f = g.shape
    tm, tf = _pick(lp, ROW_TILE, 8), _pick(f, 1408)

    def body(g_ref, u_ref, o_ref, ot_ref):
        gv = g_ref[...].astype(F32)
        act = gv * _sigmoid(gv) * u_ref[...].astype(F32)
        o_ref[...] = act.astype(BF16)
        ot_ref[...] = act.T.astype(BF16)

    spec = pl.BlockSpec((tm, tf), lambda i, j: (i, j))
    return pl.pallas_call(
        body, name=name, grid=(lp // tm, f // tf), in_specs=[spec, spec],
        out_specs=[spec, pl.BlockSpec((tf, tm), lambda i, j: (j, i))],
        out_shape=[jax.ShapeDtypeStruct((lp, f), BF16), jax.ShapeDtypeStruct((f, lp), BF16)],
        compiler_params=_params("parallel", "parallel"),
    )(g, u)


def _swiglu_bwd(dact, g, u, *, name):
    lp, f = g.shape
    tm, tf = _pick(lp, ROW_TILE, 8), _pick(f, 1408)

    def body(d_ref, g_ref, u_ref, dg_ref, du_ref):
        gv, dv = g_ref[...].astype(F32), d_ref[...].astype(F32)
        s = _sigmoid(gv)
        dg_ref[...] = (dv * u_ref[...].astype(F32) * s * (1.0 + gv * (1.0 - s))).astype(BF16)
        du_ref[...] = (dv * gv * s).astype(BF16)

    spec = pl.BlockSpec((tm, tf), lambda i, j: (i, j))
    return pl.pallas_call(
        body, name=name, grid=(lp // tm, f // tf), in_specs=[spec, spec, spec], out_specs=[spec, spec],
        out_shape=[jax.ShapeDtypeStruct((lp, f), BF16)] * 2, compiler_params=_params("parallel", "parallel"),
    )(dact, g, u)


def _merge_fwd(ya, yb, z, *, zl, name):
    lp, d = ya.shape
    tm, td = _pick(lp, ROW_TILE, 8), _pick(d, 512)
    oa, ob = zl["ga"] // td, zl["gb"] // td

    def body(ya_ref, yb_ref, ga_ref, gb_ref, o_ref, ot_ref):
        mg = _sigmoid(ga_ref[...]) * ya_ref[...].astype(F32) + _sigmoid(gb_ref[...]) * yb_ref[...].astype(F32)
        o_ref[...] = mg.astype(BF16)
        ot_ref[...] = mg.T.astype(BF16)

    spec = pl.BlockSpec((tm, td), lambda i, j: (i, j))
    return pl.pallas_call(
        body, name=name, grid=(lp // tm, d // td),
        in_specs=[spec, spec, pl.BlockSpec((tm, td), lambda i, j: (i, oa + j)),
                  pl.BlockSpec((tm, td), lambda i, j: (i, ob + j))],
        out_specs=[spec, pl.BlockSpec((td, tm), lambda i, j: (j, i))],
        out_shape=[jax.ShapeDtypeStruct((lp, d), BF16), jax.ShapeDtypeStruct((d, lp), BF16)],
        compiler_params=_params("parallel", "parallel"),
    )(ya, yb, z, z)


def _merge_bwd(dmg, ya, yb, z, *, zl, name):
    lp, d = ya.shape
    tm, td = _pick(lp, ROW_TILE, 8), _pick(d, 512)
    oa, ob = zl["ga"] // td, zl["gb"] // td

    def body(d_ref, ya_ref, yb_ref, ga_ref, gb_ref, dya_ref, dyb_ref, dga_ref, dgb_ref):
        dv = d_ref[...].astype(F32)
        sa, sb = _sigmoid(ga_ref[...]), _sigmoid(gb_ref[...])
        dya_ref[...] = (dv * sa).astype(BF16)
        dyb_ref[...] = (dv * sb).astype(BF16)
        dga_ref[...] = dv * ya_ref[...].astype(F32) * sa * (1.0 - sa)
        dgb_ref[...] = dv * yb_ref[...].astype(F32) * sb * (1.0 - sb)

    spec = pl.BlockSpec((tm, td), lambda i, j: (i, j))
    return pl.pallas_call(
        body, name=name, grid=(lp // tm, d // td),
        in_specs=[spec, spec, spec, pl.BlockSpec((tm, td), lambda i, j: (i, oa + j)),
                  pl.BlockSpec((tm, td), lambda i, j: (i, ob + j))],
        out_specs=[spec] * 4,
        out_shape=[jax.ShapeDtypeStruct((lp, d), BF16)] * 2 + [jax.ShapeDtypeStruct((lp, d), F32)] * 2,
        compiler_params=_params("parallel", "parallel"),
    )(dmg, ya, yb, z, z)


def _qkv_prep_fwd(q2, kv, z, cos_t, sin_t, *, name):
    lp = q2.shape[0]
    tm = _pick(lp, ROW_TILE, 8)
    h = MLA_HEADS
    wd = PREP_HEADS * HEAD_W

    def body(qa_ref, qb_ref, kv_ref, za_ref, zb_ref, c_ref, s_ref, q_ref, k_ref, v_ref):
        c, s = c_ref[...], s_ref[...]
        lane = lax.broadcasted_iota(jnp.int32, (tm, HEAD_W), 1)
        kr = jnp.where(lane >= QK_NOPE, za_ref[...] * c + zb_ref[...] * s, 0.0)
        for g in range(PREP_HEADS):
            sl = slice(g * HEAD_W, (g + 1) * HEAD_W)
            q_ref[:, sl] = ((qa_ref[:, sl] * c + qb_ref[:, sl] * s) * Q_SCALE).astype(BF16)
            kvv = kv_ref[:, sl]
            k_ref[:, sl] = (jnp.where(lane < QK_NOPE, kvv, 0.0) + kr).astype(BF16)
            v_ref[:, sl] = jnp.where(lane >= QK_NOPE, kvv, jnp.where(lane == 0, 1.0, 0.0)).astype(BF16)

    blk = lambda w, f: pl.BlockSpec((tm, w), f)
    out = blk(wd, lambda i, j: (i, j))
    return pl.pallas_call(
        body, name=name, grid=(lp // tm, h // PREP_HEADS),
        in_specs=[blk(wd, lambda i, j: (i, j)), blk(wd, lambda i, j: (i, h // PREP_HEADS + j)),
                  blk(wd, lambda i, j: (i, j)),
                  blk(HEAD_W, lambda i, j: (i, Z_KPA // HEAD_W)), blk(HEAD_W, lambda i, j: (i, Z_KPB // HEAD_W)),
                  blk(HEAD_W, lambda i, j: (i, 0)), blk(HEAD_W, lambda i, j: (i, 0))],
        out_specs=[out, out, out], out_shape=[jax.ShapeDtypeStruct((lp, QW), BF16)] * 3,
        compiler_params=_params("parallel", "parallel"),
    )(q2, q2, kv, z, z, cos_t, sin_t)


def _qkv_prep_bwd(dq, dk, dv, cos_t, sin_t, *, name):
    lp = dq.shape[0]
    tm = _pick(lp, ROW_TILE, 8)
    h = MLA_HEADS
    wd = PREP_HEADS * HEAD_W

    def body(dq_ref, dk_ref, dv_ref, c_ref, s_ref, dqa_ref, dqb_ref, dkv_ref, dza_ref, dzb_ref):
        j = pl.program_id(1)
        c, s = c_ref[...], s_ref[...]
        lane = lax.broadcasted_iota(jnp.int32, (tm, HEAD_W), 1)
        dkr = jnp.zeros((tm, HEAD_W), F32)
        for g in range(PREP_HEADS):
            sl = slice(g * HEAD_W, (g + 1) * HEAD_W)
            dqv, dkv_ = dq_ref[:, sl], dk_ref[:, sl]
            dqa_ref[:, sl] = (dqv * c).astype(BF16)
            dqb_ref[:, sl] = (dqv * s).astype(BF16)
            dkv_ref[:, sl] = jnp.where(lane < QK_NOPE, dkv_, dv_ref[:, sl]).astype(BF16)
            dkr = dkr + jnp.where(lane >= QK_NOPE, dkv_, 0.0)

        @pl.when(j == 0)
        def _():
            dza_ref[...] = jnp.zeros_like(dza_ref)
            dzb_ref[...] = jnp.zeros_like(dzb_ref)

        dza_ref[...] += dkr * c
        dzb_ref[...] += dkr * s

    blk = lambda w, f: pl.BlockSpec((tm, w), f)
    per_head, shared = blk(wd, lambda i, j: (i, j)), blk(HEAD_W, lambda i, j: (i, 0))
    return pl.pallas_call(
        body, name=name, grid=(lp // tm, h // PREP_HEADS),
        in_specs=[per_head, per_head, per_head, shared, shared],
        out_specs=[per_head, per_head, per_head, shared, shared],
        out_shape=[jax.ShapeDtypeStruct((lp, QW), BF16)] * 3 + [jax.ShapeDtypeStruct((lp, HEAD_W), F32)] * 2,
        compiler_params=_params("parallel", "arbitrary"),
    )(dq, dk, dv, cos_t, sin_t)


def _attn_tile(lp):
    return _pick(lp, ROW_TILE, LANES)


Q_SCALE = (QK_NOPE + QK_ROPE) ** -0.5 * math.log2(math.e)
ATT_HP = 2
ATT_HP_FWD = 4


def _attn_consts(lp):
    t = _attn_tile(lp)
    nb = lp // t
    r = np.arange(t)
    causal = np.where(r[None, :] <= r[:, None], 0.0, NEG_BIG).astype(np.float32)
    front = np.where(r >= FRONT, 0.0, NEG_BIG).astype(np.float32)[None, :]
    diag = np.stack([np.minimum(causal, front), causal])
    qmaj = [(i, j) for i in range(nb) for j in range(i + 1)]
    kmaj = [(i, j) for j in range(nb) for i in range(j, nb)]
    tab = lambda pairs, c: jnp.asarray([p[c] for p in pairs], jnp.int32)
    return dict(diag=jnp.asarray(diag), front=jnp.asarray(front),
                fwd=(tab(qmaj, 0), tab(qmaj, 1)), bwd=(tab(kmaj, 0), tab(kmaj, 1)))


def _attn_fwd(q, k, v, ac, *, lv, name):
    lp = q.shape[0]
    t = _attn_tile(lp)
    nb = lp // t
    rep = t // HEAD_W
    qtab, ktab = ac["fwd"]

    def body(qt_ref, kt_ref, q_ref, k_ref, v_ref, bd_ref, bf_ref, o_ref, lse_ref, m_s, acc_s):
        step_id = pl.program_id(1)
        qb, kb = qt_ref[step_id], kt_ref[step_id]

        @pl.when(kb == 0)
        def _():
            m_s[...] = jnp.full_like(m_s, NEG_BIG)
            acc_s[...] = jnp.zeros_like(acc_s)

        def step(bias):
            b = None if bias is None else bias()
            for hh in range(ATT_HP_FWD):
                sl = slice(hh * HEAD_W, (hh + 1) * HEAD_W)
                s = lax.dot_general(q_ref[:, sl], k_ref[:, sl], _DIMS["nt"], preferred_element_type=F32)
                if b is not None:
                    s = s + b
                m_prev = m_s[:, sl]
                m_new = jnp.maximum(m_prev, jnp.max(s, axis=-1, keepdims=True))
                alpha = jnp.exp2(m_prev - m_new)
                p = jnp.exp2(s - jnp.tile(m_new, (1, rep)))
                acc_s[:, sl] = alpha * acc_s[:, sl] + jnp.dot(p.astype(BF16), v_ref[:, sl],
                                                              preferred_element_type=F32)
                m_s[:, sl] = m_new

        @pl.when((kb > 0) & (kb < qb))
        def _():
            step(None)

        @pl.when((kb == 0) & (qb > 0))
        def _():
            step(lambda: bf_ref[...])

        @pl.when(kb == qb)
        def _():
            step(lambda: bd_ref[0])
            valid = _row_valid(qb * t, t, lv)
            for hh in range(ATT_HP_FWD):
                sl = slice(hh * HEAD_W, (hh + 1) * HEAD_W)
                acc = acc_s[:, sl]
                l = acc[:, :1]
                o_ref[:, sl] = acc / l * valid
                lse_ref[:, sl] = m_s[:, sl] + jnp.log2(l)

    wd = ATT_HP_FWD * HEAD_W
    qs = pl.BlockSpec((t, wd), lambda h, s, qt, kt: (qt[s], h))
    ks = pl.BlockSpec((t, wd), lambda h, s, qt, kt: (kt[s], h))
    grid_spec = pltpu.PrefetchScalarGridSpec(
        num_scalar_prefetch=2, grid=(MLA_HEADS // ATT_HP_FWD, int(qtab.shape[0])),
        in_specs=[qs, ks, ks, pl.BlockSpec((1, t, t), lambda h, s, qt, kt: (jnp.minimum(qt[s], 1), 0, 0)),
                  pl.BlockSpec((1, t), lambda h, s, qt, kt: (0, 0))],
        out_specs=[qs, qs],
        scratch_shapes=[pltpu.VMEM((t, wd), F32), pltpu.VMEM((t, wd), F32)])
    return pl.pallas_call(
        body, name=name, grid_spec=grid_spec, out_shape=[jax.ShapeDtypeStruct((lp, QW), F32)] * 2,
        compiler_params=_params("parallel", "arbitrary"),
    )(qtab, ktab, q, k, v, ac["diag"], ac["front"])


def _attn_delta(do, o, *, name):
    lp = do.shape[0]
    tm = _pick(lp, ROW_TILE, 8)
    wd = PREP_HEADS * HEAD_W

    def body(do_ref, o_ref, d_ref):
        for g in range(PREP_HEADS):
            sl = slice(g * HEAD_W, (g + 1) * HEAD_W)
            d_ref[:, sl] = jnp.broadcast_to(jnp.sum(do_ref[:, sl] * o_ref[:, sl], axis=-1, keepdims=True),
                                            (tm, HEAD_W))

    spec = pl.BlockSpec((tm, wd), lambda i, j: (i, j))
    return pl.pallas_call(
        body, name=name, grid=(lp // tm, MLA_HEADS // PREP_HEADS), in_specs=[spec, spec], out_specs=spec,
        out_shape=jax.ShapeDtypeStruct((lp, QW), F32), compiler_params=_params("parallel", "parallel"),
    )(do, o)


def _attn_bwd(q, k, v, do, lse, delta, ac, *, name):
    lp = q.shape[0]
    t = _attn_tile(lp)
    nb = lp // t
    rep = t // HEAD_W
    scale = (QK_NOPE + QK_ROPE) ** -0.5
    qtab, ktab = ac["bwd"]

    def body(qt_ref, kt_ref, q_ref, k_ref, v_ref, do_ref, lse_ref, dl_ref, bd_ref, bf_ref, dq_ref, dk_ref, dv_ref,
             dk_s, dv_s):
        step_id = pl.program_id(1)
        qb, kb = qt_ref[step_id], kt_ref[step_id]

        @pl.when(qb == kb)
        def _():
            dk_s[...] = jnp.zeros_like(dk_s)
            dv_s[...] = jnp.zeros_like(dv_s)

        def step(bias):
            b = None if bias is None else bias()
            rows = pl.ds(pl.multiple_of(qb * t, t), t)
            contribs = []
            for hh in range(ATT_HP):
                sl = slice(hh * HEAD_W, (hh + 1) * HEAD_W)
                qv, kv_, vv = q_ref[:, sl], k_ref[:, sl], v_ref[:, sl]
                dof = do_ref[:, sl]
                dov = dof.astype(BF16)
                s = lax.dot_general(qv, kv_, _DIMS["nt"], preferred_element_type=F32)
                if b is not None:
                    s = s + b
                p = jnp.exp2(s - jnp.tile(lse_ref[:, sl], (1, rep)))
                dv_s[:, sl] += lax.dot_general(p.astype(BF16), dov, _DIMS["tn"], preferred_element_type=F32)
                dp = lax.dot_general((dof * scale).astype(BF16), vv, _DIMS["nt"], preferred_element_type=F32)
                ds = (p * (dp - jnp.tile(dl_ref[:, sl] * scale, (1, rep)))).astype(BF16)
                dk_s[:, sl] += lax.dot_general(ds, qv, _DIMS["tn"], preferred_element_type=F32)
                contribs.append(jnp.dot(ds, kv_, preferred_element_type=F32))
            contrib = jnp.concatenate(contribs, axis=1)

            @pl.when(kb == 0)
            def _():
                dq_ref[rows, :] = contrib

            @pl.when(kb > 0)
            def _():
                dq_ref[rows, :] += contrib

        @pl.when((kb > 0) & (kb < qb))
        def _():
            step(None)

        @pl.when((kb == 0) & (qb > 0))
        def _():
            step(lambda: bf_ref[...])

        @pl.when(kb == qb)
        def _():
            step(lambda: bd_ref[0])

        @pl.when(qb == nb - 1)
        def _():
            dk_ref[...] = dk_s[...] * (1.0 / Q_SCALE)
            dv_ref[...] = dv_s[...]

    wd = ATT_HP * HEAD_W
    qs = pl.BlockSpec((t, wd), lambda h, s, qt, kt: (qt[s], h))
    ks = pl.BlockSpec((t, wd), lambda h, s, qt, kt: (kt[s], h))
    dqs = pl.BlockSpec((lp, wd), lambda h, s, qt, kt: (0, h))
    grid_spec = pltpu.PrefetchScalarGridSpec(
        num_scalar_prefetch=2, grid=(MLA_HEADS // ATT_HP, int(qtab.shape[0])),
        in_specs=[qs, ks, ks, qs, qs, qs,
                  pl.BlockSpec((1, t, t), lambda h, s, qt, kt: (jnp.minimum(qt[s], 1), 0, 0)),
                  pl.BlockSpec((1, t), lambda h, s, qt, kt: (0, 0))],
        out_specs=[dqs, ks, ks],
        scratch_shapes=[pltpu.VMEM((t, wd), F32), pltpu.VMEM((t, wd), F32)])
    return pl.pallas_call(
        body, name=name, grid_spec=grid_spec, out_shape=[jax.ShapeDtypeStruct((lp, QW), F32)] * 3,
        compiler_params=_params("arbitrary", "arbitrary"),
    )(qtab, ktab, q, k, v, do, lse, delta, ac["diag"], ac["front"])


HG_UNROLL = 2
HG_LEVELS = (64, 32, 16, 8, 4, 2)
N_LEV = len(HG_LEVELS)


def _hgrn_consts():
    c = HG_CHUNK
    m = np.zeros((N_LEV + 2, c, c), np.float32)
    masks = np.zeros((N_LEV, c, c), np.float32)
    for li, p in enumerate(HG_LEVELS):
        for t in range(c):
            mid = (t // p) * p + p // 2
            if t >= mid:
                m[li, t, mid:t + 1] = 1.0
            else:
                m[li, t, t + 1:mid] = 1.0
            for s in range(c):
                if s // p == t // p and t >= mid and s < mid:
                    masks[li, t, s] = 1.0
    for t in range(c):
        m[N_LEV, t, :t + 1] = 1.0
        m[N_LEV + 1, t, t + 1:] = 1.0
    mall = m.reshape((N_LEV + 2) * c, c)
    return jnp.asarray(mall, BF16), jnp.asarray(mall.T.copy(), BF16), jnp.asarray(masks, F32)


def _split_terms(x):
    hi = x.astype(BF16)
    lo = (x - hi.astype(F32)).astype(BF16)
    return jnp.concatenate([hi, lo], axis=1)


def _sum_terms(e3):
    return e3[:, :HG_D] + e3[:, HG_D:]


def _hgrn_chunk_fwd(hq, hf, hi, lb, valid, mall, masks, st):
    c = HG_CHUNK
    scale = HG_D ** -0.5
    sq = _sigmoid(hq)
    qv = hq * sq
    sg = _sigmoid(hf)
    f = lb + (1.0 - lb) * sg
    fc = jnp.maximum(f, F_MIN)
    lf = jnp.log(fc) * valid
    kv = (1.0 - lb) * (1.0 - sg) * valid
    e = _sum_terms(jnp.dot(mall, _split_terms(lf), preferred_element_type=F32))
    x = jnp.exp(e)
    a = jnp.zeros((c, c), F32)
    qe, ke = [], []
    for l in range(N_LEV):
        xl = x[l * c:(l + 1) * c]
        qe.append(qv * xl)
        ke.append(kv * xl)
        a = a + masks[l] * lax.dot_general(qe[l].astype(BF16), ke[l].astype(BF16), _DIMS["nt"],
                                           preferred_element_type=F32)
    row = lax.broadcasted_iota(jnp.int32, (c, c), 0)
    col = lax.broadcasted_iota(jnp.int32, (c, c), 1)
    a = a + jnp.where(row == col, jnp.sum(qv * kv, axis=-1, keepdims=True), 0.0)
    xb = x[N_LEV * c:(N_LEV + 1) * c]
    qb = qv * xb
    kb = kv * x[(N_LEV + 1) * c:]
    x_last = xb[c - 1:c]
    hib = hi.astype(BF16)
    o = scale * (jnp.dot(a.astype(BF16), hib, preferred_element_type=F32)
                 + lax.dot_general(qb.astype(BF16), st.astype(BF16), _DIMS["nt"], preferred_element_type=F32))
    st_new = st * x_last + lax.dot_general(hib, kb.astype(BF16), _DIMS["tn"], preferred_element_type=F32)
    saved = dict(sq=sq, qv=qv, sg=sg, f=f, fc=fc, kv=kv, x=x, a=a, qe=qe, ke=ke, qb=qb, kb=kb, x_last=x_last)
    return o, st_new, saved


def _split_ride(refs, n_in, n_out, n_scratch, ride):
    ri = len(ride.args) if ride else 0
    ro = len(ride.out_shape) if ride else 0
    a = n_in + ri
    b = a + n_out + ro
    c = b + n_scratch
    return refs[:n_in], refs[a:a + n_out], refs[b:c], refs[n_in:a] + refs[a + n_out:b] + refs[c:]


def _ride_call(ride):
    if ride is None:
        return [], [], [], [], []
    hbm = [HBM_SPEC] * len(ride.args)
    return hbm, list(ride.args), [HBM_SPEC] * len(ride.out_shape), list(ride.out_shape), list(ride.scratch)


def _hgrn_fwd(z, lb, nw, consts, *, zl, lv, name, ride=None):
    lp = z.shape[0]
    tb = _pick(lp, ROW_TILE, HG_CHUNK)
    ncb = tb // HG_CHUNK
    nb = lp // tb
    mall, _, masks = consts
    w = HG_HEADS * HG_D

    def body(*refs):
        ins, outs, (st_s,), ride_refs = _split_ride(refs, 8, 3, 1, ride)
        hq_ref, hf_ref, hi_ref, hg_ref, lb_ref, nw_ref, mall_ref, masks_ref = ins
        o_ref, ob_ref, st_ref = outs
        i = pl.program_id(0)

        @pl.when(i == 0)
        def _():
            st_s[...] = jnp.zeros_like(st_s)
            if ride is not None:
                ride.start(*ride_refs)

        nwv = nw_ref[...]
        mallv, masksv = mall_ref[...], masks_ref[...]

        def chunk(cix, carry):
            r0 = pl.multiple_of(cix * HG_CHUNK, HG_CHUNK)
            rows = pl.ds(r0, HG_CHUNK)
            valid = _row_valid(i * tb + r0, HG_CHUNK, lv)
            for h in range(HG_HEADS):
                sl = slice(h * HG_D, (h + 1) * HG_D)
                st = st_s[h]
                st_ref[h, cix] = st
                o, st_new, _ = _hgrn_chunk_fwd(hq_ref[rows, sl], hf_ref[rows, sl], hi_ref[rows, sl], lb_ref[:, sl],
                                               valid, mallv, masksv, st)
                st_s[h] = st_new
                o_ref[rows, sl] = o
                hg = hg_ref[rows, sl]
                r = lax.rsqrt(jnp.mean(o * o, axis=-1, keepdims=True) + EPS)
                ob_ref[rows, sl] = (o * r * nwv * (hg * _sigmoid(hg))).astype(BF16)
            return carry

        lax.fori_loop(0, ncb, chunk, 0, unroll=HG_UNROLL)

        if ride is not None:
            @pl.when(i == nb - 1)
            def _():
                ride.finish(*ride_refs)

    zb = lambda off: pl.BlockSpec((tb, w), lambda i: (i, off // w))
    full = pl.BlockSpec((tb, w), lambda i: (i, 0))
    const = lambda shape: pl.BlockSpec(shape, lambda i: (0,) * len(shape))
    r_in, r_args, r_out, r_shape, r_scratch = _ride_call(ride)
    outs = pl.pallas_call(
        body, name=name, grid=(nb,),
        in_specs=[zb(zl["hq"]), zb(zl["hf"]), zb(zl["hi"]), zb(zl["hg"]), const((1, w)), const((1, HG_D)),
                  const(mall.shape), const(masks.shape)] + r_in,
        out_specs=[full, full, pl.BlockSpec((HG_HEADS, ncb, HG_D, HG_D), lambda i: (0, i, 0, 0))] + r_out,
        out_shape=[jax.ShapeDtypeStruct((lp, w), F32), jax.ShapeDtypeStruct((lp, w), BF16),
                   jax.ShapeDtypeStruct((HG_HEADS, lp // HG_CHUNK, HG_D, HG_D), F32)] + r_shape,
        scratch_shapes=[pltpu.VMEM((HG_HEADS, HG_D, HG_D), F32)] + r_scratch,
        compiler_params=_params("arbitrary"),
    )(z, z, z, z, lb.reshape(1, w), nw.reshape(1, HG_D), mall, masks, *r_args)
    return outs[0], outs[1], outs[2], list(outs[3:])


def _hgrn_chunk_bwd(hq, hf, hi, hg, o, dout, st, dst, lbv, nwv, valid, mallv, malltv, masksv):
    c = HG_CHUNK
    scale = HG_D ** -0.5
    _, _, sv = _hgrn_chunk_fwd(hq, hf, hi, lbv, valid, mallv, masksv, st)
    shg = _sigmoid(hg)
    r = lax.rsqrt(jnp.mean(o * o, axis=-1, keepdims=True) + EPS)
    don = dout * (hg * shg)
    dhg = dout * (o * r * nwv) * shg * (1.0 + hg * (1.0 - shg))
    dnw = jnp.sum(don * o * r, axis=0, keepdims=True)
    wd = don * nwv
    do = r * wd - o * (r * r * r) * jnp.mean(o * wd, axis=-1, keepdims=True)
    dob16, hib = do.astype(BF16), hi.astype(BF16)
    dst16 = dst.astype(BF16)
    da = scale * lax.dot_general(dob16, hib, _DIMS["nt"], preferred_element_type=F32)
    dv = (scale * lax.dot_general(sv["a"].astype(BF16), dob16, _DIMS["tn"], preferred_element_type=F32)
          + lax.dot_general(sv["kb"].astype(BF16), dst16, _DIMS["nt"], preferred_element_type=F32))
    dkb = jnp.dot(hib, dst16, preferred_element_type=F32)
    dqb = scale * jnp.dot(dob16, st.astype(BF16), preferred_element_type=F32)
    dst_new = dst * sv["x_last"] + scale * lax.dot_general(dob16, sv["qb"].astype(BF16), _DIMS["tn"],
                                                           preferred_element_type=F32)
    dxl = jnp.sum(dst * st, axis=0, keepdims=True)
    x = sv["x"]
    dq = dqb * x[N_LEV * c:(N_LEV + 1) * c]
    dk = dkb * x[(N_LEV + 1) * c:]
    de = []
    for l in range(N_LEV):
        dam = (masksv[l] * da).astype(BF16)
        dqe = jnp.dot(dam, sv["ke"][l].astype(BF16), preferred_element_type=F32)
        dke = lax.dot_general(dam, sv["qe"][l].astype(BF16), _DIMS["tn"], preferred_element_type=F32)
        xl = x[l * c:(l + 1) * c]
        dq = dq + dqe * xl
        dk = dk + dke * xl
        de.append(dqe * sv["qe"][l] + dke * sv["ke"][l])
    dd = scale * jnp.sum(do * hi, axis=-1, keepdims=True)
    dq = dq + dd * sv["kv"]
    dk = dk + dd * sv["qv"]
    last = lax.broadcasted_iota(jnp.int32, (c, 1), 0) == c - 1
    de.append(dqb * sv["qb"] + jnp.where(last, dxl * sv["x_last"], 0.0))
    de.append(dkb * sv["kb"])
    dlf = _sum_terms(jnp.dot(malltv, _split_terms(jnp.concatenate(de, axis=0)), preferred_element_type=F32))
    sg, sq = sv["sg"], sv["sq"]
    df = jnp.where(sv["f"] > F_MIN, dlf * valid / sv["fc"], 0.0)
    dkm = dk * valid
    dhf = (df - dkm) * (1.0 - lbv) * sg * (1.0 - sg)
    dlb = jnp.sum((df - dkm) * (1.0 - sg), axis=0, keepdims=True)
    dhq = dq * sq * (1.0 + hq * (1.0 - sq))
    return dhq, dhf, dv, dhg, dlb, dnw, dst_new


def _hgrn_bwd(z, o_pre, dob, states, lb, nw, consts, *, zl, lv, name, ride=None):
    lp = z.shape[0]
    tb = _pick(lp, ROW_TILE, HG_CHUNK)
    ncb = tb // HG_CHUNK
    nb = lp // tb
    mall, mall_t, masks = consts
    w = HG_HEADS * HG_D
    c = HG_CHUNK

    def body(*refs):
        ins, outs, (dst_s,), ride_refs = _split_ride(refs, 12, 6, 1, ride)
        hq_ref, hf_ref, hi_ref, hg_ref, o_ref, dob_ref, st_ref, lb_ref, nw_ref, mall_ref, mallt_ref, masks_ref = ins
        dhq_ref, dhf_ref, dhi_ref, dhg_ref, dlb_ref, dnw_ref = outs
        i = pl.program_id(0)
        blk = nb - 1 - i

        @pl.when(i == 0)
        def _():
            dst_s[...] = jnp.zeros_like(dst_s)
            dlb_ref[...] = jnp.zeros_like(dlb_ref)
            dnw_ref[...] = jnp.zeros_like(dnw_ref)
            if ride is not None:
                ride.start(*ride_refs)

        nwv = nw_ref[...]
        mallv, malltv, masksv = mall_ref[...], mallt_ref[...], masks_ref[...]

        def chunk(jx, carry):
            cix = ncb - 1 - jx
            r0 = pl.multiple_of(cix * c, c)
            rows = pl.ds(r0, c)
            valid = _row_valid(blk * tb + r0, c, lv)
            for h in range(HG_HEADS):
                sl = slice(h * HG_D, (h + 1) * HG_D)
                dhq, dhf, dhi, dhg, dlb, dnw, dst_new = _hgrn_chunk_bwd(
                    hq_ref[rows, sl], hf_ref[rows, sl], hi_ref[rows, sl], hg_ref[rows, sl], o_ref[rows, sl],
                    dob_ref[rows, sl], st_ref[h, cix], dst_s[h], lb_ref[:, sl], nwv, valid, mallv, malltv, masksv)
                dst_s[h] = dst_new
                dhq_ref[rows, sl] = dhq
                dhf_ref[rows, sl] = dhf
                dhi_ref[rows, sl] = dhi
                dhg_ref[rows, sl] = dhg
                dlb_ref[:, sl] += dlb
                dnw_ref[...] += dnw
            return carry

        lax.fori_loop(0, ncb, chunk, 0, unroll=HG_UNROLL)

        if ride is not None:
            @pl.when(i == nb - 1)
            def _():
                ride.finish(*ride_refs)

    zb = lambda off: pl.BlockSpec((tb, w), lambda i: (nb - 1 - i, off // w))
    full = pl.BlockSpec((tb, w), lambda i: (nb - 1 - i, 0))
    const = lambda shape: pl.BlockSpec(shape, lambda i: (0,) * len(shape))
    r_in, r_args, r_out, r_shape, r_scratch = _ride_call(ride)
    outs = pl.pallas_call(
        body, name=name, grid=(nb,),
        in_specs=[zb(zl["hq"]), zb(zl["hf"]), zb(zl["hi"]), zb(zl["hg"]), full, full,
                  pl.BlockSpec((HG_HEADS, ncb, HG_D, HG_D), lambda i: (0, nb - 1 - i, 0, 0)),
                  const((1, w)), const((1, HG_D)), const(mall.shape), const(mall_t.shape), const(masks.shape)] + r_in,
        out_specs=[full, full, full, full, const((1, w)), const((1, HG_D))] + r_out,
        out_shape=[jax.ShapeDtypeStruct((lp, w), F32)] * 4
                  + [jax.ShapeDtypeStruct((1, w), F32), jax.ShapeDtypeStruct((1, HG_D), F32)] + r_shape,
        scratch_shapes=[pltpu.VMEM((HG_HEADS, HG_D, HG_D), F32)] + r_scratch,
        compiler_params=_params("arbitrary"),
    )(z, z, z, z, o_pre, dob, states, lb.reshape(1, w), nw.reshape(1, HG_D), mall, mall_t, masks, *r_args)
    dhq, dhf, dhi, dhg, dlb, dnw = outs[:6]
    return dhq, dhf, dhi, dhg, dlb[0], dnw[0], list(outs[6:])


def _loss_head(h, w, tpad, *, lv, name):
    lp, d = h.shape
    tm = _pick(lp, ROW_TILE, 8)

    def body(h_ref, w_ref, t_ref, dh_ref, loss_ref, dw_ref):
        i = pl.program_id(0)
        r0 = i * tm + lax.broadcasted_iota(jnp.int32, (tm, 1), 0)
        valid = ((r0 >= ROW_X) & (r0 < lv)).astype(F32)
        xv, wv = h_ref[...], w_ref[...]
        r = lax.rsqrt(jnp.mean(xv * xv, axis=-1, keepdims=True) + EPS)
        e = (xv * r * wv - t_ref[...]) * valid
        dy = e * (1.0 / d)
        wdy = dy * wv
        dh_ref[...] = r * wdy - xv * (r * r * r) * jnp.mean(xv * wdy, axis=-1, keepdims=True)

        @pl.when(i == 0)
        def _():
            loss_ref[...] = jnp.zeros_like(loss_ref)
            dw_ref[...] = jnp.zeros_like(dw_ref)

        loss_ref[...] += 0.5 * jnp.sum(jnp.mean(e * e, axis=-1, keepdims=True), axis=0, keepdims=True)
        dw_ref[...] += jnp.sum(dy * xv * r, axis=0, keepdims=True)

    row = pl.BlockSpec((tm, d), lambda i: (i, 0))
    vec = pl.BlockSpec((1, d), lambda i: (0, 0))
    dh, loss, dw = pl.pallas_call(
        body, name=name, grid=(lp // tm,), in_specs=[row, vec, row],
        out_specs=[row, pl.BlockSpec((8, LANES), lambda i: (0, 0)), vec],
        out_shape=[jax.ShapeDtypeStruct((lp, d), F32), jax.ShapeDtypeStruct((8, LANES), F32),
                   jax.ShapeDtypeStruct((1, d), F32)],
        compiler_params=_params("arbitrary"),
    )(h, w.reshape(1, d), tpad)
    return dh, loss[0, 0], dw[0]


def _adamw(w, g, m, v, *, name):
    shape = w.shape
    cols = shape[-1]
    rows = int(np.prod(shape[:-1])) if len(shape) > 1 else 1
    tr = _pick(rows, 256, 8)
    c1 = 1.0 - ADAM_B1 ** ADAM_STEP
    c2 = 1.0 - ADAM_B2 ** ADAM_STEP

    def body(w_ref, g_ref, m_ref, v_ref, d_ref, nm_ref, nv_ref):
        gv = g_ref[...]
        nm = ADAM_B1 * m_ref[...] + (1.0 - ADAM_B1) * gv
        nv = ADAM_B2 * v_ref[...] + (1.0 - ADAM_B2) * (gv * gv)
        d_ref[...] = -ADAM_LR * ((nm / c1) / (jnp.sqrt(nv / c2) + ADAM_EPS) + ADAM_WD * w_ref[...])
        nm_ref[...] = nm
        nv_ref[...] = nv

    spec = pl.BlockSpec((tr, cols), lambda i: (i, 0))
    r2 = lambda a: a.reshape(rows, cols)
    outs = pl.pallas_call(
        body, name=name, grid=(rows // tr,), in_specs=[spec] * 4, out_specs=[spec] * 3,
        out_shape=[jax.ShapeDtypeStruct((rows, cols), F32)] * 3, compiler_params=_params("parallel"),
    )(r2(w), r2(g), r2(m), r2(v))
    return tuple(o.reshape(shape) for o in outs)


HBM_SPEC = pl.BlockSpec(memory_space=pl.ANY)


def _coords():
    return lax.axis_index("x"), lax.axis_index("y"), lax.axis_index("c")


def _other_chips(x, y):
    return [(1 - x, y), (x, 1 - y), (1 - x, 1 - y)]


def _remote(src, dst, ssem, rsem, dev):
    return pltpu.make_async_remote_copy(src_ref=src, dst_ref=dst, send_sem=ssem, recv_sem=rsem,
                                        device_id=dev, device_id_type=MESH)


def _row_halves(w, c):
    rows = w.shape[-2]
    rh = rows // 2
    align = 8 * 4 // w.dtype.itemsize
    assert rh * 2 == rows and rh % align == 0, w.shape
    return pl.ds(pl.multiple_of(c * rh, align), rh), pl.ds(pl.multiple_of((1 - c) * rh, align), rh)


def _gathered_shapes(ws):
    return [jax.ShapeDtypeStruct((4, *w.shape), w.dtype) for w in ws]


def _own_block(g4s, ws):
    xi, yi, _ = _coords()
    return [lax.dynamic_update_slice(g4, w[None], (2 * xi + yi, 0, 0)) for g4, w in zip(g4s, ws)]


def _gather_chips(ws, *, name):
    n = len(ws)

    def body(*refs):
        w_refs, out_refs, (send_sems, recv_sems) = refs[:n], refs[n:2 * n], refs[2 * n:]
        x, y, c = _coords()
        sib = (x, y, 1 - c)
        chips = _other_chips(x, y)
        sent = []
        for t in range(n):
            half, _ = _row_halves(ws[t], c)
            for j, (px, py) in enumerate(chips):
                cp = _remote(w_refs[t].at[half], out_refs[t].at[2 * x + y, half], send_sems.at[6 * t + j],
                             recv_sems.at[6 * t + j], (px, py, c))
                cp.start()
                sent.append(cp)
        for t in range(n):
            half, _ = _row_halves(ws[t], c)
            for j, (px, py) in enumerate(chips):
                blk = out_refs[t].at[2 * px + py, half]
                _remote(w_refs[t].at[half], blk, send_sems.at[6 * t + j], recv_sems.at[6 * t + j],
                        (px, py, c)).wait_recv()
                fw = _remote(blk, blk, send_sems.at[6 * t + 3 + j], recv_sems.at[6 * t + 3 + j], sib)
                fw.start()
                sent.append(fw)
        for t in range(n):
            _, ohalf = _row_halves(ws[t], c)
            for j, (px, py) in enumerate(chips):
                blk = out_refs[t].at[2 * px + py, ohalf]
                _remote(blk, blk, send_sems.at[6 * t + 3 + j], recv_sems.at[6 * t + 3 + j], sib).wait_recv()
        for cp in sent:
            cp.wait_send()

    g4s = pl.pallas_call(
        body, name=name, in_specs=[HBM_SPEC] * n, out_specs=[HBM_SPEC] * n, out_shape=_gathered_shapes(ws),
        scratch_shapes=[pltpu.SemaphoreType.DMA((6 * n,)), pltpu.SemaphoreType.DMA((6 * n,))],
    )(*ws)
    return _own_block(g4s, ws)


def _swap_halves(gp, *, name):
    n, rows, cols = gp.shape
    rh = rows // 2

    def body(g_ref, out_ref, send_sems, recv_sems):
        x, y, c = _coords()
        sib = (x, y, 1 - c)
        ohalf = pl.ds(pl.multiple_of((1 - c) * rh, 8 * 4 // gp.dtype.itemsize), rh)
        cps = [_remote(g_ref.at[s, ohalf], out_ref.at[s], send_sems.at[s], recv_sems.at[s], sib) for s in range(n)]
        for cp in cps:
            cp.start()
        for cp in cps:
            cp.wait_recv()
        for cp in cps:
            cp.wait_send()

    return pl.pallas_call(
        body, name=name, in_specs=[HBM_SPEC], out_specs=HBM_SPEC,
        out_shape=jax.ShapeDtypeStruct((n, rh, cols), gp.dtype),
        scratch_shapes=[pltpu.SemaphoreType.DMA((n,)), pltpu.SemaphoreType.DMA((n,))],
    )(gp)


def _add_half(gp, got, cidx, *, name):
    n, rows, cols = gp.shape
    rh = rows // 2
    tr = _pick(rh, 512, 16)
    nrb = rh // tr

    def body(c_ref, a_ref, b_ref, o_ref):
        o_ref[...] = (a_ref[...].astype(F32) + b_ref[...].astype(F32)).astype(BF16)

    grid_spec = pltpu.PrefetchScalarGridSpec(
        num_scalar_prefetch=1, grid=(n, nrb),
        in_specs=[pl.BlockSpec((1, tr, cols), lambda s, i, c_ref: (s, c_ref[0] * nrb + i, 0)),
                  pl.BlockSpec((1, tr, cols), lambda s, i, c_ref: (s, i, 0))],
        out_specs=pl.BlockSpec((1, tr, cols), lambda s, i, c_ref: (s, i, 0)))
    return pl.pallas_call(
        body, name=name, grid_spec=grid_spec, out_shape=jax.ShapeDtypeStruct((n, rh, cols), BF16),
        compiler_params=_params("parallel", "parallel"),
    )(cidx, gp, got)


def _scatter_chips(p, *, name):
    _, rh, cols = p.shape

    def body(p_ref, out_ref, send_sems, recv_sems):
        x, y, c = _coords()
        cps = []
        for j, (px, py) in enumerate(_other_chips(x, y)):
            cps.append(_remote(p_ref.at[2 * px + py], out_ref.at[j], send_sems.at[j], recv_sems.at[j], (px, py, c)))
        for cp in cps:
            cp.start()
        for cp in cps:
            cp.wait_recv()
        for cp in cps:
            cp.wait_send()

    return pl.pallas_call(
        body, name=name, in_specs=[HBM_SPEC], out_specs=HBM_SPEC,
        out_shape=jax.ShapeDtypeStruct((3, rh, cols), p.dtype),
        scratch_shapes=[pltpu.SemaphoreType.DMA((3,)), pltpu.SemaphoreType.DMA((3,))],
    )(p)


def _sum_arrivals(p, land, kidx, *, name):
    _, rh, cols = p.shape
    tr = _pick(rh, 512, 16)

    def body(k_ref, a_ref, l_ref, o_ref):
        f = lambda v: v.astype(F32)
        o_ref[...] = ((f(a_ref[0]) + f(l_ref[0])) + f(l_ref[1])) + f(l_ref[2])

    grid_spec = pltpu.PrefetchScalarGridSpec(
        num_scalar_prefetch=1, grid=(rh // tr,),
        in_specs=[pl.BlockSpec((1, tr, cols), lambda i, k_ref: (k_ref[0], i, 0)),
                  pl.BlockSpec((3, tr, cols), lambda i, k_ref: (0, i, 0))],
        out_specs=pl.BlockSpec((tr, cols), lambda i, k_ref: (i, 0)))
    return pl.pallas_call(
        body, name=name, grid_spec=grid_spec, out_shape=jax.ShapeDtypeStruct((rh, cols), F32),
        compiler_params=_params("parallel"),
    )(kidx, p, land)


def _join_halves(q, *, name):
    rh, cols = q.shape

    def body(q_ref, out_ref, send_sem, recv_sem):
        x, y, c = _coords()
        half = pl.ds(pl.multiple_of(c * rh, 8), rh)
        ohalf = pl.ds(pl.multiple_of((1 - c) * rh, 8), rh)
        cp = _remote(q_ref, out_ref.at[half], send_sem, recv_sem, (x, y, 1 - c))
        cp.start()
        _remote(q_ref, out_ref.at[ohalf], send_sem, recv_sem, (x, y, 1 - c)).wait_recv()
        cp.wait_send()

    full = pl.pallas_call(
        body, name=name, in_specs=[HBM_SPEC], out_specs=HBM_SPEC,
        out_shape=jax.ShapeDtypeStruct((2 * rh, cols), q.dtype),
        scratch_shapes=[pltpu.SemaphoreType.DMA, pltpu.SemaphoreType.DMA],
    )(q)
    return lax.dynamic_update_slice(full, q, (lax.axis_index("c") * rh, 0))


def _rs_begin(gp, cidx, *, tag):
    got = _swap_halves(gp, name=f"rs_swap_{tag}")
    return _add_half(gp, got, cidx, name=f"rs_add_{tag}")


def _rs_end(p, land, kidx, *, tag):
    q = _sum_arrivals(p, land, kidx, name=f"rs_sum_{tag}")
    return _join_halves(q, name=f"rs_join_{tag}")


class _ScatterRide:
    def __init__(self, p):
        _, rh, cols = p.shape
        self.args = [p]
        self.out_shape = [jax.ShapeDtypeStruct((3, rh, cols), p.dtype)]
        self.scratch = [pltpu.SemaphoreType.DMA((3,)), pltpu.SemaphoreType.DMA((3,))]

    def _copies(self, p_ref, out_ref, ssem, rsem):
        x, y, c = _coords()
        return [_remote(p_ref.at[2 * px + py], out_ref.at[j], ssem.at[j], rsem.at[j], (px, py, c))
                for j, (px, py) in enumerate(_other_chips(x, y))]

    def start(self, *refs):
        for cp in self._copies(*refs):
            cp.start()

    def finish(self, *refs):
        cps = self._copies(*refs)
        for cp in cps:
            cp.wait_recv()
        for cp in cps:
            cp.wait_send()


class _GatherRide:
    def __init__(self, ws):
        n = len(ws)
        self.args = list(ws)
        self.out_shape = _gathered_shapes(ws)
        self.scratch = [pltpu.SemaphoreType.DMA((3 * n,)), pltpu.SemaphoreType.DMA((3 * n,))]

    def _copies(self, *refs):
        n = len(self.args)
        w_refs, out_refs, (ssem, rsem) = refs[:n], refs[n:2 * n], refs[2 * n:]
        x, y, c = _coords()
        send, recv = [], []
        for t in range(n):
            half, _ = _row_halves(self.args[t], c)
            for j, (px, py) in enumerate(_other_chips(x, y)):
                sems = (ssem.at[3 * t + j], rsem.at[3 * t + j], (px, py, c))
                send.append(_remote(w_refs[t].at[half], out_refs[t].at[2 * x + y, half], *sems))
                recv.append(_remote(w_refs[t].at[half], out_refs[t].at[2 * px + py, half], *sems))
        return send, recv

    def start(self, *refs):
        for cp in self._copies(*refs)[0]:
            cp.start()

    def finish(self, *refs):
        send, recv = self._copies(*refs)
        for cp in recv:
            cp.wait_recv()
        for cp in send:
            cp.wait_send()


def _gather_forward(g4s, *, name):
    n = len(g4s)

    def body(*refs):
        out_refs, (send_sems, recv_sems) = refs[n:2 * n], refs[2 * n:]
        x, y, c = _coords()
        sib = (x, y, 1 - c)
        chips = _other_chips(x, y)
        sent = []
        for t in range(n):
            half, _ = _row_halves(g4s[t], c)
            for j, (px, py) in enumerate(chips):
                blk = out_refs[t].at[2 * px + py, half]
                cp = _remote(blk, blk, send_sems.at[3 * t + j], recv_sems.at[3 * t + j], sib)
                cp.start()
                sent.append(cp)
        for t in range(n):
            _, ohalf = _row_halves(g4s[t], c)
            for j, (px, py) in enumerate(chips):
                blk = out_refs[t].at[2 * px + py, ohalf]
                _remote(blk, blk, send_sems.at[3 * t + j], recv_sems.at[3 * t + j], sib).wait_recv()
        for cp in sent:
            cp.wait_send()

    return pl.pallas_call(
        body, name=name, in_specs=[HBM_SPEC] * n, out_specs=[HBM_SPEC] * n,
        out_shape=[jax.ShapeDtypeStruct(g.shape, g.dtype) for g in g4s],
        input_output_aliases={t: t for t in range(n)},
        scratch_shapes=[pltpu.SemaphoreType.DMA((3 * n,)), pltpu.SemaphoreType.DMA((3 * n,))],
    )(*g4s)


def _allreduce_small(s, *, name):
    rows, cols = s.shape

    def body(s_ref, o_ref, buf, send_sems, recv_sems):
        x, y, c = _coords()
        me = 4 * x + 2 * y + c
        buf[me] = s_ref[...]
        cps = []
        for r in range(1, 8):
            peer = tuple((1 - v) if (r >> sh) & 1 else v for v, sh in ((x, 2), (y, 1), (c, 0)))
            cps.append(_remote(s_ref, buf.at[me], send_sems.at[r - 1], recv_sems.at[r - 1], peer))
        for cp in cps:
            cp.start()
        for cp in cps:
            cp.wait_recv()
        for cp in cps:
            cp.wait_send()
        acc = buf[0]
        for d in range(1, 8):
            acc = acc + buf[d]
        o_ref[...] = acc

    vm = pl.BlockSpec(memory_space=pltpu.VMEM)
    return pl.pallas_call(
        body, name=name, in_specs=[vm], out_specs=vm, out_shape=jax.ShapeDtypeStruct((rows, cols), F32),
        scratch_shapes=[pltpu.VMEM((8, rows, cols), F32), pltpu.SemaphoreType.DMA((7,)),
                        pltpu.SemaphoreType.DMA((7,))],
    )(s)


PACKED = ("ffn1_w_gu", "ffn1_w_down", "w_in", "w_uq", "w_ukv", "w_proj_attn", "w_proj_rec", "w_out",
          "ffn2_w_gu", "ffn2_w_down")
ROW_SHARDED = ("ffn1_w_down", "w_out", "ffn2_w_down")


def _pack_plan(shard_shapes):
    plan, off = {}, 0
    for n in PACKED:
        r, c = shard_shapes[n]
        assert (r * c) % PACK_W == 0
        plan[n] = (off, r * c // PACK_W, (r, c))
        off += r * c // PACK_W
    total = -(-off // 32) * 32
    return plan, total


def _pack(tensors, plan, total, dtype):
    parts = [tensors[n].astype(dtype).reshape(-1, PACK_W) for n in PACKED]
    used = sum(p.shape[0] for p in parts)
    if total > used:
        parts.append(jnp.zeros((total - used, PACK_W), dtype))
    return jnp.concatenate(parts, axis=0)


def _full_weights(g4s):
    out = {}
    for n, g in zip(PACKED, g4s):
        _, r, c = g.shape
        out[n] = g.reshape(4 * r, c) if n in ROW_SHARDED else jnp.swapaxes(g, 0, 1).reshape(r, 4 * c)
    return out


def _pack_grads(grads, plan, total):
    blocks = []
    for s in range(4):
        t = {}
        for n in PACKED:
            _, _, (r, c) = plan[n]
            t[n] = grads[n][s * r:(s + 1) * r] if n in ROW_SHARDED else grads[n][:, s * c:(s + 1) * c]
        blocks.append(_pack(t, plan, total, BF16))
    return jnp.stack(blocks)


def _unpack_shard(p, plan):
    return {n: p[plan[n][0]:plan[n][0] + plan[n][1]].reshape(plan[n][2]) for n in PACKED}


def _swap_cols(w):
    hlf = w.shape[1] // 2
    return jnp.concatenate([-w[:, hlf:], w[:, :hlf]], axis=1)


def _unswap_cols(dw):
    hlf = dw.shape[1] // 2
    return jnp.concatenate([dw[:, hlf:], -dw[:, :hlf]], axis=1)


def _layer_weights(full, d):
    zl = _z_layout(d)
    f = full["ffn1_w_down"].shape[0]
    w_in = full["w_in"]
    o = 0
    cols = {}
    for nm, wd in (("cq", Q_LORA), ("ckv", KV_LORA), ("kpe", QK_ROPE), ("hq", 512), ("hf", 512), ("hi", 512),
                   ("hg", 512), ("ga", d), ("gb", d)):
        cols[nm] = w_in[:, o:o + wd]
        o += wd
    zc = lambda n: jnp.zeros((d, n), BF16)
    win_p = jnp.concatenate(
        [cols["cq"], zc(QK_NOPE), cols["kpe"], zc(32), cols["ckv"], zc(QK_NOPE), _swap_cols(cols["kpe"]), zc(32),
         zc(LANES), cols["ga"], cols["gb"], cols["hq"], cols["hf"], cols["hi"], cols["hg"]], axis=1)
    assert win_p.shape[1] == zl["total"]
    wq = full["w_uq"].reshape(Q_LORA, MLA_HEADS, QK_NOPE + QK_ROPE)
    nope, rope = wq[:, :, :QK_NOPE], wq[:, :, QK_NOPE:]
    z32 = jnp.zeros((Q_LORA, MLA_HEADS, 32), BF16)
    z64 = jnp.zeros((Q_LORA, MLA_HEADS, QK_NOPE), BF16)
    rope_sw = jnp.concatenate([-rope[:, :, 16:], rope[:, :, :16]], axis=2)
    wqa = jnp.concatenate([nope, rope, z32], axis=2).reshape(Q_LORA, QW)
    wqb = jnp.concatenate([z64, rope_sw, z32], axis=2).reshape(Q_LORA, QW)
    wpa = full["w_proj_attn"].reshape(MLA_HEADS, V_HEAD, d)
    wpa_p = jnp.concatenate([jnp.zeros_like(wpa), wpa], axis=1).reshape(QW, d)
    return dict(
        wg1=full["ffn1_w_gu"][:, :f], wu1=full["ffn1_w_gu"][:, f:], wd1=full["ffn1_w_down"],
        wg2=full["ffn2_w_gu"][:, :f], wu2=full["ffn2_w_gu"][:, f:], wd2=full["ffn2_w_down"],
        win=win_p, wq2=jnp.concatenate([wqa, wqb], axis=1), wqa=wqa, wqb=wqb, wkv=full["w_ukv"], wpa=wpa_p,
        wpr=full["w_proj_rec"], wout=full["w_out"])


def _natural_grads(g, d):
    zl = _z_layout(d)
    dwin = g["win"]
    kpe = dwin[:, Z_KPA + QK_NOPE:Z_KPA + QK_NOPE + QK_ROPE] + _unswap_cols(
        dwin[:, Z_KPB + QK_NOPE:Z_KPB + QK_NOPE + QK_ROPE])
    w_in = jnp.concatenate(
        [dwin[:, Z_Q:Z_Q + Q_LORA], dwin[:, Z_KV:Z_KV + KV_LORA], kpe, dwin[:, zl["hq"]:zl["hq"] + 2048],
         dwin[:, zl["ga"]:zl["ga"] + 2 * d]], axis=1)
    qa = g["wqa"].reshape(Q_LORA, MLA_HEADS, HEAD_W)
    qb = g["wqb"].reshape(Q_LORA, MLA_HEADS, HEAD_W)[:, :, QK_NOPE:QK_NOPE + QK_ROPE]
    rope = qa[:, :, QK_NOPE:QK_NOPE + QK_ROPE] + jnp.concatenate([qb[:, :, 16:], -qb[:, :, :16]], axis=2)
    w_uq = jnp.concatenate([qa[:, :, :QK_NOPE], rope], axis=2).reshape(Q_LORA, -1)
    wpa = g["wpa"].reshape(MLA_HEADS, 2 * V_HEAD, d)[:, V_HEAD:].reshape(MLA_HEADS * V_HEAD, d)
    return dict(
        ffn1_w_gu=jnp.concatenate([g["wg1"], g["wu1"]], axis=1), ffn1_w_down=g["wd1"],
        ffn2_w_gu=jnp.concatenate([g["wg2"], g["wu2"]], axis=1), ffn2_w_down=g["wd2"],
        w_in=w_in, w_uq=w_uq, w_ukv=g["wkv"], w_proj_attn=wpa, w_proj_rec=g["wpr"], w_out=g["wout"])


def _rope_tables(lp):
    pos = jnp.maximum(jnp.arange(lp) - FRONT, 0).astype(F32)
    half = QK_ROPE // 2
    inv = ROPE_THETA ** (-jnp.arange(half, dtype=F32) / half)
    ang = pos[:, None] * inv[None, :]
    cos, sin = jnp.cos(ang), jnp.sin(ang)
    cos_t = jnp.concatenate([jnp.ones((lp, QK_NOPE), F32), cos, cos, jnp.zeros((lp, 32), F32)], axis=1)
    sin_t = jnp.concatenate([jnp.zeros((lp, QK_NOPE), F32), sin, sin, jnp.zeros((lp, 32), F32)], axis=1)
    return cos_t, sin_t


def _lower_bounds(raw):
    p = jax.nn.softmax(raw.astype(F32), axis=0)
    return jnp.cumsum(p, axis=0) - p[0:1]


def _ffn_fwd(h, nw, wg, wu, wd, tag):
    a, a_t = _rmsnorm_fwd(h, nw, width=h.shape[1], col_block=0, transposed=True, name=f"norm_{tag}")
    g = _matmul([(a, wg)], "nn", out_dtype=BF16, name=f"gate_{tag}")
    u = _matmul([(a, wu)], "nn", out_dtype=BF16, name=f"up_{tag}")
    act, act_t = _swiglu_fwd(g, u, name=f"swiglu_{tag}")
    out = _matmul([(act, wd)], "nn", res=h, scale=0.5, name=f"down_{tag}")
    return out, dict(h=h, a_t=a_t, g=g, u=u, act_t=act_t)


def _ffn_bwd(dout, dout16, sv, nw, wg, wu, wd, lv, tag):
    dm = dout if dout16 is None else dout16
    dact = _matmul([(dm, wd)], "nt", scale=0.5, out_dtype=BF16, name=f"ddown_{tag}")
    dwd = _matmul([(sv["act_t"], dm)], "nn", scale=0.5, name=f"dwdown_{tag}")
    dg, du = _swiglu_bwd(dact, sv["g"], sv["u"], name=f"dswiglu_{tag}")
    dwg = _matmul([(sv["a_t"], dg)], "nn", name=f"dwgate_{tag}")
    dwu = _matmul([(sv["a_t"], du)], "nn", name=f"dwup_{tag}")
    da = _matmul([(dg, wg), (du, wu)], "nt", name=f"dnormed_{tag}")
    dh, dh16, dn = _rmsnorm_bwd(sv["h"], nw, da, width=da.shape[1], col_block=0, lv=lv, dres=dout,
                                name=f"dnorm_{tag}")
    return dh, dh16, dn, dwg, dwu, dwd


def _layer_fwd(h0, lw, sm, lb, tabs, consts, lv, l, ride=None):
    d = h0.shape[1]
    zl = _z_layout(d)
    cos_t, sin_t = tabs[:2]
    h1, s1 = _ffn_fwd(h0, sm["ffn1_norm"], lw["wg1"], lw["wu1"], lw["wd1"], f"ffn1_{l}")
    um, um_t = _rmsnorm_fwd(h1, sm["mix_norm"], width=d, col_block=0, transposed=True, name=f"norm_mix_{l}")
    z = _matmul([(um, lw["win"])], "nn", name=f"inproj_{l}")
    qn, qn_t = _rmsnorm_fwd(z, sm["q_norm"], width=Q_LORA, col_block=Z_Q // Q_LORA, transposed=True,
                            name=f"norm_q_{l}")
    kvn, kvn_t = _rmsnorm_fwd(z, sm["kv_norm"], width=KV_LORA, col_block=Z_KV // KV_LORA, transposed=True,
                              name=f"norm_kv_{l}")
    q2 = _matmul([(qn, lw["wq2"])], "nn", name=f"uq_{l}")
    kv = _matmul([(kvn, lw["wkv"])], "nn", name=f"ukv_{l}")
    q, k, v = _qkv_prep_fwd(q2, kv, z, cos_t, sin_t, name=f"qkv_{l}")
    o, lse = _attn_fwd(q, k, v, tabs[2], lv=lv, name=f"attn_{l}")
    ya = _matmul([(o, lw["wpa"])], "nn", out_dtype=BF16, name=f"proj_attn_{l}")
    o_pre, ob, states, rode = _hgrn_fwd(z, lb, sm["hg_norm"], consts, zl=zl, lv=lv, name=f"hgrn_{l}", ride=ride)
    yb = _matmul([(ob, lw["wpr"])], "nn", out_dtype=BF16, name=f"proj_rec_{l}")
    mg, mg_t = _merge_fwd(ya, yb, z, zl=zl, name=f"merge_{l}")
    h2 = _matmul([(mg, lw["wout"])], "nn", res=h1, name=f"out_{l}")
    h3, s2 = _ffn_fwd(h2, sm["ffn2_norm"], lw["wg2"], lw["wu2"], lw["wd2"], f"ffn2_{l}")
    saved = dict(s1=s1, s2=s2, h1=h1, um_t=um_t, z=z, qn_t=qn_t, kvn_t=kvn_t, q=q, k=k, v=v, o=o, lse=lse, ya=ya, yb=yb,
                 o_pre=o_pre, ob=ob, states=states, mg_t=mg_t)
    return h3, saved, rode


def _layer_bwd(dh3, dh3h, sv, lw, sm, lb, tabs, consts, lv, l, ride=None):
    d = dh3.shape[1]
    lp = dh3.shape[0]
    zl = _z_layout(d)
    cos_t, sin_t = tabs[:2]
    z = sv["z"]
    g = {}
    sg = {}
    dh2, dh2h, sg["ffn2_norm"], g["wg2"], g["wu2"], g["wd2"] = _ffn_bwd(
        dh3, dh3h, sv["s2"], sm["ffn2_norm"], lw["wg2"], lw["wu2"], lw["wd2"], lv, f"ffn2_{l}")
    dmg = _matmul([(dh2h, lw["wout"])], "nt", out_dtype=BF16, name=f"dmerged_{l}")
    g["wout"] = _matmul([(sv["mg_t"], dh2h)], "nn", name=f"dwout_{l}")
    dya, dyb, dga, dgb = _merge_bwd(dmg, sv["ya"], sv["yb"], z, zl=zl, name=f"dmerge_{l}")
    doa = _matmul([(dya, lw["wpa"])], "nt", name=f"dattn_out_{l}")
    g["wpa"] = _matmul([(sv["o"], dya)], "tn", name=f"dwproj_attn_{l}")
    dob = _matmul([(dyb, lw["wpr"])], "nt", name=f"drec_out_{l}")
    g["wpr"] = _matmul([(sv["ob"], dyb)], "tn", name=f"dwproj_rec_{l}")
    dhq, dhf, dhi, dhg, dlb, sg["hg_norm"], rode = _hgrn_bwd(
        z, sv["o_pre"], dob, sv["states"], lb, sm["hg_norm"], consts, zl=zl, lv=lv, name=f"dhgrn_{l}", ride=ride)
    delta = _attn_delta(doa, sv["o"], name=f"attn_delta_{l}")
    dq, dk, dv = _attn_bwd(sv["q"], sv["k"], sv["v"], doa, sv["lse"], delta, tabs[2], name=f"dattn_{l}")
    dqa, dqb, dkv, dza, dzb = _qkv_prep_bwd(dq, dk, dv, cos_t, sin_t, name=f"dqkv_{l}")
    dqn = _matmul([(dqa, lw["wqa"]), (dqb, lw["wqb"])], "nt", name=f"dqn_{l}")
    g["wqa"] = _matmul([(sv["qn_t"], dqa)], "nn", name=f"dwqa_{l}")
    g["wqb"] = _matmul([(sv["qn_t"], dqb)], "nn", name=f"dwqb_{l}")
    dkvn = _matmul([(dkv, lw["wkv"])], "nt", name=f"dkvn_{l}")
    g["wkv"] = _matmul([(sv["kvn_t"], dkv)], "nn", name=f"dwkv_{l}")
    dzq, sg["q_norm"] = _rmsnorm_bwd(z, sm["q_norm"], dqn, width=Q_LORA, col_block=Z_Q // Q_LORA, lv=lv,
                                     name=f"dnorm_q_{l}")
    dzkv, sg["kv_norm"] = _rmsnorm_bwd(z, sm["kv_norm"], dkvn, width=KV_LORA, col_block=Z_KV // KV_LORA, lv=lv,
                                       name=f"dnorm_kv_{l}")
    dz = jnp.concatenate([dzq, dza, dzkv, dzb, jnp.zeros((lp, LANES), F32), dga, dgb, dhq, dhf, dhi, dhg],
                         axis=1).astype(BF16)
    dum = _matmul([(dz, lw["win"])], "nt", name=f"dmixed_{l}")
    g["win"] = _matmul([(sv["um_t"], dz)], "nn", name=f"dwin_{l}")
    dh1, dh1h, sg["mix_norm"] = _rmsnorm_bwd(sv["h1"], sm["mix_norm"], dum, width=d, col_block=0, lv=lv, dres=dh2,
                                             name=f"dnorm_mix_{l}")
    dh0, dh0h, sg["ffn1_norm"], g["wg1"], g["wu1"], g["wd1"] = _ffn_bwd(
        dh1, dh1h, sv["s1"], sm["ffn1_norm"], lw["wg1"], lw["wu1"], lw["wd1"], lv, f"ffn1_{l}")
    return dh0, dh0h, g, sg, dlb, rode


WEIGHTS = ("meta_tokens", "ffn1_norm", "ffn1_w_gu", "ffn1_w_down", "mix_norm", "w_in", "q_norm", "kv_norm", "w_uq",
           "w_ukv", "hg_lb_raw", "hg_norm", "w_proj_attn", "w_proj_rec", "w_out", "ffn2_norm", "ffn2_w_gu",
           "ffn2_w_down", "final_norm")
SMALL = ("ffn1_norm", "mix_norm", "q_norm", "kv_norm", "hg_lb_raw", "hg_norm", "ffn2_norm")


def _small_rows(vals):
    pad = lambda a: jnp.pad(a, ((0, -a.shape[0] % 8), (0, PACK_W - a.shape[1])))
    rows = [pad(vals[n]) for n in SMALL]
    rows.append(pad(vals["final_norm"][None, :]))
    rows.append(pad(vals["meta_tokens"]))
    rows.append(pad(vals["loss"].reshape(1, 1)))
    return jnp.concatenate(rows, axis=0)


def _small_unrows(s, d, widths):
    out, o = {}, 0
    for n in SMALL:
        out[n] = s[o:o + DEPTH, :widths[n]]
        o += -(-DEPTH // 8) * 8
    out["final_norm"] = s[o, :d]
    o += 8
    out["meta_tokens"] = s[o:o + N_META, :d]
    o += -(-N_META // 8) * 8
    out["loss"] = s[o, 0]
    return out


def _step(args):
    x = args["x"][0]
    seq, d = x.shape
    assert d <= PACK_W
    lv = ROW_X + seq
    lp = -(-lv // ROW_TILE) * ROW_TILE
    xi, yi, ci = _coords()
    kidx = (2 * xi + yi).astype(jnp.int32).reshape(1)
    cidx = ci.astype(jnp.int32).reshape(1)
    consts = _hgrn_consts()
    tabs = (*_rope_tables(lp), _attn_consts(lp))

    shard_shapes = {n: args[n].shape[1:] for n in PACKED}
    plan, total = _pack_plan(shard_shapes)
    shards = [[args[n][l].astype(BF16) for n in PACKED] for l in range(DEPTH)]
    mt = args["meta_tokens"]
    mt4 = _gather_chips([mt], name="gather_meta")[0]
    meta = jnp.concatenate(list(mt4), axis=1)

    sm = [{n: args[n][l] for n in SMALL} for l in range(DEPTH)]
    lbs = _lower_bounds(args["hg_lb_raw"])

    h = jnp.concatenate([jnp.zeros((FRONT, d), F32), meta, x, jnp.zeros((lp - lv, d), F32)], axis=0)
    saved, lws = [], []
    g4s = _gather_chips(shards[0], name="gather_0")
    for l in range(DEPTH):
        lws.append(_layer_weights(_full_weights(g4s), d))
        ride = _GatherRide(shards[l + 1]) if l + 1 < DEPTH else None
        h, sv, rode = _layer_fwd(h, lws[l], sm[l], lbs[l], tabs, consts, lv, l, ride)
        saved.append(sv)
        if ride is not None:
            g4s = _own_block(_gather_forward(rode, name=f"gather_fwd_{l + 1}"), shards[l + 1])
    tpad = jnp.pad(args["loss_target"][0], ((ROW_X, lp - lv), (0, 0)))
    dh, loss, dfinal = _loss_head(h, args["final_norm"], tpad, lv=lv, name="loss_head")

    small = {n: [None] * DEPTH for n in SMALL}
    dlbs = [None] * DEPTH
    shard_grads = [None] * DEPTH
    waiting = None
    dhh = None
    for l in reversed(range(DEPTH)):
        ride = _ScatterRide(waiting) if waiting is not None else None
        dh, dhh, g, sg, dlbs[l], rode = _layer_bwd(dh, dhh, saved[l], lws[l], sm[l], lbs[l], tabs, consts, lv, l,
                                                   ride)
        if ride is not None:
            shard_grads[l + 1] = _unpack_shard(_rs_end(waiting, rode[0], kidx, tag=str(l + 1)), plan)
        for n in sg:
            small[n][l] = sg[n]
        waiting = _rs_begin(_pack_grads(_natural_grads(g, d), plan, total), cidx, tag=str(l))
    land = _scatter_chips(waiting, name="rs_scatter_0")
    shard_grads[0] = _unpack_shard(_rs_end(waiting, land, kidx, tag="0"), plan)

    _, lb_vjp = jax.vjp(_lower_bounds, args["hg_lb_raw"])
    small_vals = {n: jnp.stack(small[n]) for n in SMALL if n != "hg_lb_raw"}
    small_vals["hg_lb_raw"] = lb_vjp(jnp.stack(dlbs))[0]
    small_vals["final_norm"] = dfinal
    small_vals["meta_tokens"] = dh[FRONT:ROW_X]
    small_vals["loss"] = loss
    widths = {n: args[n].shape[1] for n in SMALL}
    tot = _small_unrows(_allreduce_small(_small_rows(small_vals), name="allreduce_small"), d, widths)

    grads = {n: jnp.stack([shard_grads[l][n] for l in range(DEPTH)]) for n in PACKED}
    for n in SMALL:
        grads[n] = tot[n]
    grads["final_norm"] = tot["final_norm"]
    mcols = mt.shape[1]
    grads["meta_tokens"] = lax.dynamic_slice_in_dim(tot["meta_tokens"], (2 * xi + yi) * mcols, mcols, axis=1)
    grad_x = dh[ROW_X:lv][None]

    delta, new_m, new_v = {}, {}, {}
    for n in WEIGHTS:
        delta[n], new_m[n], new_v[n] = _adamw(args[n], grads[n], args["m_" + n], args["v_" + n], name=f"adamw_{n}")
    return (tot["loss"], grad_x, *[grads[n] for n in WEIGHTS], *[delta[n] for n in WEIGHTS],
            *[new_m[n] for n in WEIGHTS], *[new_v[n] for n in WEIGHTS])


def kernel(x, meta_tokens, ffn1_norm, ffn1_w_gu, ffn1_w_down, mix_norm, w_in, q_norm, kv_norm, w_uq, w_ukv, hg_lb_raw, hg_norm, w_proj_attn, w_proj_rec, w_out, ffn2_norm, ffn2_w_gu, ffn2_w_down, final_norm, loss_target, m_meta_tokens, m_ffn1_norm, m_ffn1_w_gu, m_ffn1_w_down, m_mix_norm, m_w_in, m_q_norm, m_kv_norm, m_w_uq, m_w_ukv, m_hg_lb_raw, m_hg_norm, m_w_proj_attn, m_w_proj_rec, m_w_out, m_ffn2_norm, m_ffn2_w_gu, m_ffn2_w_down, m_final_norm, v_meta_tokens, v_ffn1_norm, v_ffn1_w_gu, v_ffn1_w_down, v_mix_norm, v_w_in, v_q_norm, v_kv_norm, v_w_uq, v_w_ukv, v_hg_lb_raw, v_hg_norm, v_w_proj_attn, v_w_proj_rec, v_w_out, v_ffn2_norm, v_ffn2_w_gu, v_ffn2_w_down, v_final_norm):
    return _step(dict(locals()))
```
